```python
import math
import jax, jax.numpy as jnp
from jax import lax
import numpy as np

D_MODEL = 2048
BATCH = 8
SEQ = 8192
DEPTH = 1

CHUNK = 64
RET_WIDTH = D_MODEL // 2
RET_HEADS = 8
RET_HEAD_DIM = RET_WIDTH // RET_HEADS
SSM_WIDTH = D_MODEL - RET_WIDTH
SSM_GROUP = 16
SSM_GROUPS = SSM_WIDTH // SSM_GROUP
SSM_STATE = 64
D_FF = -(-8 * D_MODEL // (3 * 256)) * 256
IN_WIDTH = 4 * RET_WIDTH + SSM_WIDTH
ROPE_BASE = 10000.0
EPS = 1e-6

kernel_name = "hybrid_retention_s5_block"


def rmsnorm(x, g):
    xf = x.astype(jnp.float32)
    y = xf * lax.rsqrt(jnp.mean(xf * xf, axis=-1, keepdims=True) + EPS) * g.astype(jnp.float32)
    return y.astype(x.dtype)


def rope(x, pos):
    dh = x.shape[-1]
    half = dh // 2
    freqs = ROPE_BASE ** (-jnp.arange(half, dtype=jnp.float32) / half)
    ang = pos[:, None] * freqs[None, :]
    cos = jnp.cos(ang)[None, :, None, :]
    sin = jnp.sin(ang)[None, :, None, :]
    xf = x.astype(jnp.float32)
    x1, x2 = xf[..., :half], xf[..., half:]
    return jnp.concatenate([x1 * cos - x2 * sin, x2 * cos + x1 * sin], axis=-1)


def retention_group(q, k, v, g, gn_gain):
    b, s, h, dk = q.shape
    n_chunks = s // CHUNK
    hh = jnp.arange(h, dtype=jnp.float32)
    log_g = jnp.log1p(-(2.0 ** (-5.0 - hh)))
    q = q.reshape(b, n_chunks, CHUNK, h, dk)
    k = k.reshape(b, n_chunks, CHUNK, h, dk) * (dk ** -0.5)
    v = v.astype(jnp.float32).reshape(b, n_chunks, CHUNK, h, dk)
    idx = jnp.arange(CHUNK, dtype=jnp.float32)
    dist = jnp.abs(idx[:, None] - idx[None, :])
    intra_decay = jnp.exp(log_g[:, None, None] * dist)
    scores = jnp.einsum('bnqhd,bnkhd->bnhqk', q, k) * intra_decay
    intra = jnp.einsum('bnhqk,bnkhe->bnqhe', scores, v)
    k_dec = jnp.exp(log_g[None, :] * (CHUNK - 1 - idx)[:, None])
    kv = jnp.einsum('bnchd,bnche->nbhde', k * k_dec[..., None], v)
    chunk_dec = jnp.exp(log_g * CHUNK)[:, None, None]

    def step(state, kv_i):
        return state * chunk_dec + kv_i, state

    _, state_prev = lax.scan(step, jnp.zeros_like(kv[0]), kv)
    q_dec = jnp.exp(log_g[None, :] * (idx + 1.0)[:, None])
    inter = jnp.einsum('bnchd,nbhde->bnche', q * q_dec[..., None], state_prev)
    out = (intra + inter).reshape(b, s, h, dk)
    mu = jnp.mean(out, axis=-1, keepdims=True)
    var = jnp.mean(jnp.square(out - mu), axis=-1, keepdims=True)
    out = ((out - mu) * lax.rsqrt(var + EPS)).reshape(b, s, h * dk) * gn_gain.astype(jnp.float32)
    return jax.nn.silu(g.astype(jnp.float32)) * out


def s5_group(u, a_re, a_im, log_dt, b_re, b_im, c_re, c_im, d_skip, w_glu, b_glu, out_gain):
    b, s, _ = u.shape
    f32 = jnp.float32
    uf = u.astype(f32).reshape(b, s, SSM_GROUPS, SSM_GROUP)
    lam = lax.complex(a_re.astype(f32), a_im.astype(f32))
    dt = jnp.exp(log_dt.astype(f32))[:, None]
    lam_bar = jnp.exp(lam * dt)
    b_c = lax.complex(b_re.astype(f32), b_im.astype(f32))
    b_bar = ((lam_bar - 1.0) / lam)[..., None] * b_c
    bu = lax.complex(jnp.einsum('gpc,bsgc->bsgp', jnp.real(b_bar), uf),
                     jnp.einsum('gpc,bsgc->bsgp', jnp.imag(b_bar), uf))
    a = jnp.broadcast_to(lam_bar, bu.shape)

    def combine(left, right):
        a_l, b_l = left
        a_r, b_r = right
        return a_r * a_l, a_r * b_l + b_r

    _, states = lax.associative_scan(combine, (a, bu), axis=1)
    y = (jnp.einsum('gcp,bsgp->bsgc', c_re.astype(f32), jnp.real(states))
         - jnp.einsum('gcp,bsgp->bsgc', c_im.astype(f32), jnp.imag(states)))
    y = (y + d_skip.astype(f32).reshape(SSM_GROUPS, SSM_GROUP) * uf).reshape(b, s, SSM_WIDTH)
    y1 = jax.nn.gelu(y)
    y = y1 * jax.nn.sigmoid(y1 @ w_glu.astype(f32) + b_glu.astype(f32))
    return rmsnorm(y, out_gain)


def _fwd_setup_inputs(seed: int = 0) -> dict:
    key = jax.random.key(seed)
    ks = jax.random.split(key, 24)
    f32 = jnp.float32
    L = DEPTH

    def nrm(k, shape, scale):
        return jax.random.normal(k, shape, f32) * scale

    def gain(k, shape):
        return 1.0 + 0.02 * jax.random.normal(k, shape, f32)

    n = jnp.arange(SSM_STATE, dtype=f32)
    return {
        "x": nrm(ks[0], (BATCH, SEQ, D_MODEL), 1.0),
        "norm_mix_g": gain(ks[1], (L, D_MODEL)),
        "w_in": nrm(ks[2], (L, D_MODEL, IN_WIDTH), D_MODEL ** -0.5),
        "ret_gn_g": gain(ks[3], (L, RET_WIDTH)),
        "ssm_a_re": -0.5 + 0.01 * nrm(ks[4], (L, SSM_GROUPS, SSM_STATE), 1.0),
        "ssm_a_im": math.pi * n + 0.01 * nrm(ks[5], (L, SSM_GROUPS, SSM_STATE), 1.0),
        "ssm_log_dt": jax.random.uniform(ks[6], (L, SSM_GROUPS), f32, math.log(1e-3), math.log(1e-1)),
        "ssm_b_re": nrm(ks[7], (L, SSM_GROUPS, SSM_STATE, SSM_GROUP), (2 * SSM_GROUP) ** -0.5),
        "ssm_b_im": nrm(ks[8], (L, SSM_GROUPS, SSM_STATE, SSM_GROUP), (2 * SSM_GROUP) ** -0.5),
        "ssm_c_re": nrm(ks[9], (L, SSM_GROUPS, SSM_GROUP, SSM_STATE), (2 * SSM_STATE) ** -0.5),
        "ssm_c_im": nrm(ks[10], (L, SSM_GROUPS, SSM_GROUP, SSM_STATE), (2 * SSM_STATE) ** -0.5),
        "ssm_d": nrm(ks[11], (L, SSM_WIDTH), 1.0),
        "ssm_w_glu": nrm(ks[12], (L, SSM_WIDTH, SSM_WIDTH), SSM_WIDTH ** -0.5),
        "ssm_b_glu": nrm(ks[13], (L, SSM_WIDTH), 0.01),
        "ssm_out_g": gain(ks[14], (L, SSM_WIDTH)),
        "w_out": nrm(ks[15], (L, D_MODEL, D_MODEL), D_MODEL ** -0.5),
        "norm_ffn_g": gain(ks[16], (L, D_MODEL)),
        "w_gate": nrm(ks[17], (L, D_MODEL, D_FF), D_MODEL ** -0.5),
        "w_up": nrm(ks[18], (L, D_MODEL, D_FF), D_MODEL ** -0.5),
        "w_down": nrm(ks[19], (L, D_FF, D_MODEL), D_FF ** -0.5),
        "norm_final_g": gain(ks[20], (D_MODEL,)),
    }


def _fwd_reference(x, norm_mix_g, w_in, ret_gn_g, ssm_a_re, ssm_a_im, ssm_log_dt, ssm_b_re, ssm_b_im,
              ssm_c_re, ssm_c_im, ssm_d, ssm_w_glu, ssm_b_glu, ssm_out_g, w_out, norm_ffn_g,
              w_gate, w_up, w_down, norm_final_g):
    b, s, _ = x.shape
    pos = jnp.arange(s, dtype=jnp.float32)
    R = RET_WIDTH
    for l in range(DEPTH):
        h = rmsnorm(x, norm_mix_g[l])
        proj = h @ w_in[l]
        q = rope(proj[..., 0:R].reshape(b, s, RET_HEADS, RET_HEAD_DIM), pos)
        k = rope(proj[..., R:2 * R].reshape(b, s, RET_HEADS, RET_HEAD_DIM), pos)
        v = proj[..., 2 * R:3 * R].reshape(b, s, RET_HEADS, RET_HEAD_DIM)
        g = proj[..., 3 * R:4 * R]
        u = proj[..., 4 * R:]
        y_ret = retention_group(q, k, v, g, ret_gn_g[l])
        y_ssm = s5_group(u, ssm_a_re[l], ssm_a_im[l], ssm_log_dt[l], ssm_b_re[l], ssm_b_im[l],
                         ssm_c_re[l], ssm_c_im[l], ssm_d[l], ssm_w_glu[l], ssm_b_glu[l], ssm_out_g[l])
        mix = jnp.concatenate([y_ret, y_ssm.astype(jnp.float32)], axis=-1).astype(x.dtype)
        x = x + mix @ w_out[l]
        h = rmsnorm(x, norm_ffn_g[l])
        x = x + (jax.nn.silu(h @ w_gate[l]) * (h @ w_up[l])) @ w_down[l]
    return rmsnorm(x, norm_final_g)


import jax as _jax
import jax.numpy as _jnp

TWIN_FORMAT = 'train_step'
FWD_PARAMS = ['x', 'norm_mix_g', 'w_in', 'ret_gn_g', 'ssm_a_re', 'ssm_a_im', 'ssm_log_dt', 'ssm_b_re', 'ssm_b_im', 'ssm_c_re', 'ssm_c_im', 'ssm_d', 'ssm_w_glu', 'ssm_b_glu', 'ssm_out_g', 'w_out', 'norm_ffn_g', 'w_gate', 'w_up', 'w_down', 'norm_final_g']
TWIN_WEIGHTS = ['norm_mix_g', 'w_in', 'ret_gn_g', 'ssm_a_re', 'ssm_a_im', 'ssm_log_dt', 'ssm_b_re', 'ssm_b_im', 'ssm_c_re', 'ssm_c_im', 'ssm_d', 'ssm_w_glu', 'ssm_b_glu', 'ssm_out_g', 'w_out', 'norm_ffn_g', 'w_gate', 'w_up', 'w_down', 'norm_final_g']
TWIN_DIFF_INPUT = 'x'
TWIN_INPUTS = ['x', 'norm_mix_g', 'w_in', 'ret_gn_g', 'ssm_a_re', 'ssm_a_im', 'ssm_log_dt', 'ssm_b_re', 'ssm_b_im', 'ssm_c_re', 'ssm_c_im', 'ssm_d', 'ssm_w_glu', 'ssm_b_glu', 'ssm_out_g', 'w_out', 'norm_ffn_g', 'w_gate', 'w_up', 'w_down', 'norm_final_g', 'loss_target', 'm_norm_mix_g', 'm_w_in', 'm_ret_gn_g', 'm_ssm_a_re', 'm_ssm_a_im', 'm_ssm_log_dt', 'm_ssm_b_re', 'm_ssm_b_im', 'm_ssm_c_re', 'm_ssm_c_im', 'm_ssm_d', 'm_ssm_w_glu', 'm_ssm_b_glu', 'm_ssm_out_g', 'm_w_out', 'm_norm_ffn_g', 'm_w_gate', 'm_w_up', 'm_w_down', 'm_norm_final_g', 'v_norm_mix_g', 'v_w_in', 'v_ret_gn_g', 'v_ssm_a_re', 'v_ssm_a_im', 'v_ssm_log_dt', 'v_ssm_b_re', 'v_ssm_b_im', 'v_ssm_c_re', 'v_ssm_c_im', 'v_ssm_d', 'v_ssm_w_glu', 'v_ssm_b_glu', 'v_ssm_out_g', 'v_w_out', 'v_norm_ffn_g', 'v_w_gate', 'v_w_up', 'v_w_down', 'v_norm_final_g']
TWIN_OUTPUTS = ['loss', 'grad_x', 'grad_norm_mix_g', 'grad_w_in', 'grad_ret_gn_g', 'grad_ssm_a_re', 'grad_ssm_a_im', 'grad_ssm_log_dt', 'grad_ssm_b_re', 'grad_ssm_b_im', 'grad_ssm_c_re', 'grad_ssm_c_im', 'grad_ssm_d', 'grad_ssm_w_glu', 'grad_ssm_b_glu', 'grad_ssm_out_g', 'grad_w_out', 'grad_norm_ffn_g', 'grad_w_gate', 'grad_w_up', 'grad_w_down', 'grad_norm_final_g', 'delta_norm_mix_g', 'delta_w_in', 'delta_ret_gn_g', 'delta_ssm_a_re', 'delta_ssm_a_im', 'delta_ssm_log_dt', 'delta_ssm_b_re', 'delta_ssm_b_im', 'delta_ssm_c_re', 'delta_ssm_c_im', 'delta_ssm_d', 'delta_ssm_w_glu', 'delta_ssm_b_glu', 'delta_ssm_out_g', 'delta_w_out', 'delta_norm_ffn_g', 'delta_w_gate', 'delta_w_up', 'delta_w_down', 'delta_norm_final_g', 'new_m_norm_mix_g', 'new_m_w_in', 'new_m_ret_gn_g', 'new_m_ssm_a_re', 'new_m_ssm_a_im', 'new_m_ssm_log_dt', 'new_m_ssm_b_re', 'new_m_ssm_b_im', 'new_m_ssm_c_re', 'new_m_ssm_c_im', 'new_m_ssm_d', 'new_m_ssm_w_glu', 'new_m_ssm_b_glu', 'new_m_ssm_out_g', 'new_m_w_out', 'new_m_norm_ffn_g', 'new_m_w_gate', 'new_m_w_up', 'new_m_w_down', 'new_m_norm_final_g', 'new_v_norm_mix_g', 'new_v_w_in', 'new_v_ret_gn_g', 'new_v_ssm_a_re', 'new_v_ssm_a_im', 'new_v_ssm_log_dt', 'new_v_ssm_b_re', 'new_v_ssm_b_im', 'new_v_ssm_c_re', 'new_v_ssm_c_im', 'new_v_ssm_d', 'new_v_ssm_w_glu', 'new_v_ssm_b_glu', 'new_v_ssm_out_g', 'new_v_w_out', 'new_v_norm_ffn_g', 'new_v_w_gate', 'new_v_w_up', 'new_v_w_down', 'new_v_norm_final_g']
TWIN_LEAF_KINDS = {'loss': 'loss', 'grad_x': 'grad_x', 'grad_norm_mix_g': 'grad_w', 'grad_w_in': 'grad_w', 'grad_ret_gn_g': 'grad_w', 'grad_ssm_a_re': 'grad_w', 'grad_ssm_a_im': 'grad_w', 'grad_ssm_log_dt': 'grad_w', 'grad_ssm_b_re': 'grad_w', 'grad_ssm_b_im': 'grad_w', 'grad_ssm_c_re': 'grad_w', 'grad_ssm_c_im': 'grad_w', 'grad_ssm_d': 'grad_w', 'grad_ssm_w_glu': 'grad_w', 'grad_ssm_b_glu': 'grad_w', 'grad_ssm_out_g': 'grad_w', 'grad_w_out': 'grad_w', 'grad_norm_ffn_g': 'grad_w', 'grad_w_gate': 'grad_w', 'grad_w_up': 'grad_w', 'grad_w_down': 'grad_w', 'grad_norm_final_g': 'grad_w', 'delta_norm_mix_g': 'delta_w', 'delta_w_in': 'delta_w', 'delta_ret_gn_g': 'delta_w', 'delta_ssm_a_re': 'delta_w', 'delta_ssm_a_im': 'delta_w', 'delta_ssm_log_dt': 'delta_w', 'delta_ssm_b_re': 'delta_w', 'delta_ssm_b_im': 'delta_w', 'delta_ssm_c_re': 'delta_w', 'delta_ssm_c_im': 'delta_w', 'delta_ssm_d': 'delta_w', 'delta_ssm_w_glu': 'delta_w', 'delta_ssm_b_glu': 'delta_w', 'delta_ssm_out_g': 'delta_w', 'delta_w_out': 'delta_w', 'delta_norm_ffn_g': 'delta_w', 'delta_w_gate': 'delta_w', 'delta_w_up': 'delta_w', 'delta_w_down': 'delta_w', 'delta_norm_final_g': 'delta_w', 'new_m_norm_mix_g': 'new_m', 'new_m_w_in': 'new_m', 'new_m_ret_gn_g': 'new_m', 'new_m_ssm_a_re': 'new_m', 'new_m_ssm_a_im': 'new_m', 'new_m_ssm_log_dt': 'new_m', 'new_m_ssm_b_re': 'new_m', 'new_m_ssm_b_im': 'new_m', 'new_m_ssm_c_re': 'new_m', 'new_m_ssm_c_im': 'new_m', 'new_m_ssm_d': 'new_m', 'new_m_ssm_w_glu': 'new_m', 'new_m_ssm_b_glu': 'new_m', 'new_m_ssm_out_g': 'new_m', 'new_m_w_out': 'new_m', 'new_m_norm_ffn_g': 'new_m', 'new_m_w_gate': 'new_m', 'new_m_w_up': 'new_m', 'new_m_w_down': 'new_m', 'new_m_norm_final_g': 'new_m', 'new_v_norm_mix_g': 'new_v', 'new_v_w_in': 'new_v', 'new_v_ret_gn_g': 'new_v', 'new_v_ssm_a_re': 'new_v', 'new_v_ssm_a_im': 'new_v', 'new_v_ssm_log_dt': 'new_v', 'new_v_ssm_b_re': 'new_v', 'new_v_ssm_b_im': 'new_v', 'new_v_ssm_c_re': 'new_v', 'new_v_ssm_c_im': 'new_v', 'new_v_ssm_d': 'new_v', 'new_v_ssm_w_glu': 'new_v', 'new_v_ssm_b_glu': 'new_v', 'new_v_ssm_out_g': 'new_v', 'new_v_w_out': 'new_v', 'new_v_norm_ffn_g': 'new_v', 'new_v_w_gate': 'new_v', 'new_v_w_up': 'new_v', 'new_v_w_down': 'new_v', 'new_v_norm_final_g': 'new_v'}


def _forward(args):
    return _fwd_reference(*[args[k] for k in FWD_PARAMS])


def _output_shape():
    def fwd():
        inp = _fwd_setup_inputs(0)
        return _fwd_reference(*[inp[k] for k in FWD_PARAMS])
    out = _jax.eval_shape(fwd)
    return out.shape, out.dtype

N_MICROBATCH = 1
ADAM_LR = 0.001
ADAM_B1 = 0.9
ADAM_B2 = 0.999
ADAM_EPS = 1e-08
ADAM_WD = 0.01
ADAM_STEP = 10
PER_EXAMPLE_BATCH_AXIS = {'x': 0, 'loss_target': 0}
SHARED_INPUTS = []
_WEIGHT_DTYPES = {'norm_mix_g': _jnp.float32, 'w_in': _jnp.float32, 'ret_gn_g': _jnp.float32, 'ssm_a_re': _jnp.float32, 'ssm_a_im': _jnp.float32, 'ssm_log_dt': _jnp.float32, 'ssm_b_re': _jnp.float32, 'ssm_b_im': _jnp.float32, 'ssm_c_re': _jnp.float32, 'ssm_c_im': _jnp.float32, 'ssm_d': _jnp.float32, 'ssm_w_glu': _jnp.float32, 'ssm_b_glu': _jnp.float32, 'ssm_out_g': _jnp.float32, 'w_out': _jnp.float32, 'norm_ffn_g': _jnp.float32, 'w_gate': _jnp.float32, 'w_up': _jnp.float32, 'w_down': _jnp.float32, 'norm_final_g': _jnp.float32}
MOMENT_SCALE = {'norm_mix_g': 1.213488e-01, 'w_in': 7.626328e-02, 'ret_gn_g': 6.466200e-02, 'ssm_a_re': 6.837067e-03, 'ssm_a_im': 5.935585e-03, 'ssm_log_dt': 2.738873e+00, 'ssm_b_re': 3.896350e-03, 'ssm_b_im': 3.814689e-03, 'ssm_c_re': 7.709120e-03, 'ssm_c_im': 7.734803e-03, 'ssm_d': 1.239778e-01, 'ssm_w_glu': 3.170982e-02, 'ssm_b_glu': 5.204434e-02, 'ssm_out_g': 1.193380e-01, 'w_out': 9.415398e-02, 'norm_ffn_g': 7.707207e-02, 'w_gate': 3.274215e-02, 'w_up': 3.188143e-02, 'w_down': 5.304079e-02, 'norm_final_g': 3.208072e+01}


def _to_microbatches(a, axis):
    t = _jnp.moveaxis(a, axis, 0)
    t = t.reshape((N_MICROBATCH, t.shape[0] // N_MICROBATCH) + t.shape[1:])
    return _jnp.moveaxis(t, 1, axis + 1)


def setup_inputs(seed: int = 0) -> dict:
    inp = _fwd_setup_inputs(seed)
    key = _jax.random.fold_in(_jax.random.key(seed), 7919)
    shape, _ = _output_shape()
    out = dict(inp)
    out["loss_target"] = _jax.random.normal(_jax.random.fold_in(key, 0), shape, _jnp.float32)
    for i, name in enumerate(TWIN_WEIGHTS):
        w = inp[name].astype(_jnp.float32)
        if MOMENT_SCALE is None:
            s = _jnp.sqrt(_jnp.mean(_jnp.square(w)) + 1e-30)
        else:
            s = MOMENT_SCALE[name]
        km, kv = _jax.random.split(_jax.random.fold_in(key, i + 1))
        out[name] = w
        out["m_" + name] = s * _jax.random.normal(km, w.shape, _jnp.float32)
        out["v_" + name] = (s * s) * _jax.random.uniform(kv, w.shape, _jnp.float32, 0.5, 1.5)
    if N_MICROBATCH > 1:
        for name, axis in PER_EXAMPLE_BATCH_AXIS.items():
            out[name] = _to_microbatches(out[name], axis)
    return {'x': out['x'], 'norm_mix_g': out['norm_mix_g'], 'w_in': out['w_in'], 'ret_gn_g': out['ret_gn_g'], 'ssm_a_re': out['ssm_a_re'], 'ssm_a_im': out['ssm_a_im'], 'ssm_log_dt': out['ssm_log_dt'], 'ssm_b_re': out['ssm_b_re'], 'ssm_b_im': out['ssm_b_im'], 'ssm_c_re': out['ssm_c_re'], 'ssm_c_im': out['ssm_c_im'], 'ssm_d': out['ssm_d'], 'ssm_w_glu': out['ssm_w_glu'], 'ssm_b_glu': out['ssm_b_glu'], 'ssm_out_g': out['ssm_out_g'], 'w_out': out['w_out'], 'norm_ffn_g': out['norm_ffn_g'], 'w_gate': out['w_gate'], 'w_up': out['w_up'], 'w_down': out['w_down'], 'norm_final_g': out['norm_final_g'], 'loss_target': out['loss_target'], 'm_norm_mix_g': out['m_norm_mix_g'], 'm_w_in': out['m_w_in'], 'm_ret_gn_g': out['m_ret_gn_g'], 'm_ssm_a_re': out['m_ssm_a_re'], 'm_ssm_a_im': out['m_ssm_a_im'], 'm_ssm_log_dt': out['m_ssm_log_dt'], 'm_ssm_b_re': out['m_ssm_b_re'], 'm_ssm_b_im': out['m_ssm_b_im'], 'm_ssm_c_re': out['m_ssm_c_re'], 'm_ssm_c_im': out['m_ssm_c_im'], 'm_ssm_d': out['m_ssm_d'], 'm_ssm_w_glu': out['m_ssm_w_glu'], 'm_ssm_b_glu': out['m_ssm_b_glu'], 'm_ssm_out_g': out['m_ssm_out_g'], 'm_w_out': out['m_w_out'], 'm_norm_ffn_g': out['m_norm_ffn_g'], 'm_w_gate': out['m_w_gate'], 'm_w_up': out['m_w_up'], 'm_w_down': out['m_w_down'], 'm_norm_final_g': out['m_norm_final_g'], 'v_norm_mix_g': out['v_norm_mix_g'], 'v_w_in': out['v_w_in'], 'v_ret_gn_g': out['v_ret_gn_g'], 'v_ssm_a_re': out['v_ssm_a_re'], 'v_ssm_a_im': out['v_ssm_a_im'], 'v_ssm_log_dt': out['v_ssm_log_dt'], 'v_ssm_b_re': out['v_ssm_b_re'], 'v_ssm_b_im': out['v_ssm_b_im'], 'v_ssm_c_re': out['v_ssm_c_re'], 'v_ssm_c_im': out['v_ssm_c_im'], 'v_ssm_d': out['v_ssm_d'], 'v_ssm_w_glu': out['v_ssm_w_glu'], 'v_ssm_b_glu': out['v_ssm_b_glu'], 'v_ssm_out_g': out['v_ssm_out_g'], 'v_w_out': out['v_w_out'], 'v_norm_ffn_g': out['v_norm_ffn_g'], 'v_w_gate': out['v_w_gate'], 'v_w_up': out['v_w_up'], 'v_w_down': out['v_w_down'], 'v_norm_final_g': out['v_norm_final_g']}


def _loss(weights, diff, rest, loss_target):
    with _jax.named_scope("forward"):
        args = {**rest, TWIN_DIFF_INPUT: diff, **{k: w.astype(_WEIGHT_DTYPES[k]) for k, w in weights.items()}}
        y = _forward(args)
    with _jax.named_scope("loss_head"):
        err = _jnp.square(y.astype(_jnp.float32) - loss_target)
        return 0.5 * _jnp.sum(_jnp.mean(err, axis=-1)) if err.ndim else 0.5 * err


def _adamw(w, g, m, v):
    m = ADAM_B1 * m + (1.0 - ADAM_B1) * g
    v = ADAM_B2 * v + (1.0 - ADAM_B2) * _jnp.square(g)
    m_hat = m / (1.0 - ADAM_B1 ** ADAM_STEP)
    v_hat = v / (1.0 - ADAM_B2 ** ADAM_STEP)
    delta = -ADAM_LR * (m_hat / (_jnp.sqrt(v_hat) + ADAM_EPS) + ADAM_WD * w)
    return delta, m, v


def reference(x, norm_mix_g, w_in, ret_gn_g, ssm_a_re, ssm_a_im, ssm_log_dt, ssm_b_re, ssm_b_im, ssm_c_re, ssm_c_im, ssm_d, ssm_w_glu, ssm_b_glu, ssm_out_g, w_out, norm_ffn_g, w_gate, w_up, w_down, norm_final_g, loss_target, m_norm_mix_g, m_w_in, m_ret_gn_g, m_ssm_a_re, m_ssm_a_im, m_ssm_log_dt, m_ssm_b_re, m_ssm_b_im, m_ssm_c_re, m_ssm_c_im, m_ssm_d, m_ssm_w_glu, m_ssm_b_glu, m_ssm_out_g, m_w_out, m_norm_ffn_g, m_w_gate, m_w_up, m_w_down, m_norm_final_g, v_norm_mix_g, v_w_in, v_ret_gn_g, v_ssm_a_re, v_ssm_a_im, v_ssm_log_dt, v_ssm_b_re, v_ssm_b_im, v_ssm_c_re, v_ssm_c_im, v_ssm_d, v_ssm_w_glu, v_ssm_b_glu, v_ssm_out_g, v_w_out, v_norm_ffn_g, v_w_gate, v_w_up, v_w_down, v_norm_final_g):
    given = dict(x=x, norm_mix_g=norm_mix_g, w_in=w_in, ret_gn_g=ret_gn_g, ssm_a_re=ssm_a_re, ssm_a_im=ssm_a_im, ssm_log_dt=ssm_log_dt, ssm_b_re=ssm_b_re, ssm_b_im=ssm_b_im, ssm_c_re=ssm_c_re, ssm_c_im=ssm_c_im, ssm_d=ssm_d, ssm_w_glu=ssm_w_glu, ssm_b_glu=ssm_b_glu, ssm_out_g=ssm_out_g, w_out=w_out, norm_ffn_g=norm_ffn_g, w_gate=w_gate, w_up=w_up, w_down=w_down, norm_final_g=norm_final_g, loss_target=loss_target, m_norm_mix_g=m_norm_mix_g, m_w_in=m_w_in, m_ret_gn_g=m_ret_gn_g, m_ssm_a_re=m_ssm_a_re, m_ssm_a_im=m_ssm_a_im, m_ssm_log_dt=m_ssm_log_dt, m_ssm_b_re=m_ssm_b_re, m_ssm_b_im=m_ssm_b_im, m_ssm_c_re=m_ssm_c_re, m_ssm_c_im=m_ssm_c_im, m_ssm_d=m_ssm_d, m_ssm_w_glu=m_ssm_w_glu, m_ssm_b_glu=m_ssm_b_glu, m_ssm_out_g=m_ssm_out_g, m_w_out=m_w_out, m_norm_ffn_g=m_norm_ffn_g, m_w_gate=m_w_gate, m_w_up=m_w_up, m_w_down=m_w_down, m_norm_final_g=m_norm_final_g, v_norm_mix_g=v_norm_mix_g, v_w_in=v_w_in, v_ret_gn_g=v_ret_gn_g, v_ssm_a_re=v_ssm_a_re, v_ssm_a_im=v_ssm_a_im, v_ssm_log_dt=v_ssm_log_dt, v_ssm_b_re=v_ssm_b_re, v_ssm_b_im=v_ssm_b_im, v_ssm_c_re=v_ssm_c_re, v_ssm_c_im=v_ssm_c_im, v_ssm_d=v_ssm_d, v_ssm_w_glu=v_ssm_w_glu, v_ssm_b_glu=v_ssm_b_glu, v_ssm_out_g=v_ssm_out_g, v_w_out=v_w_out, v_norm_ffn_g=v_norm_ffn_g, v_w_gate=v_w_gate, v_w_up=v_w_up, v_w_down=v_w_down, v_norm_final_g=v_norm_final_g)
    weights = {n: given[n] for n in TWIN_WEIGHTS}
    shared = {n: given[n] for n in SHARED_INPUTS}
    per_example = {n: given[n] for n in ['x']}
    grad_fn = _jax.value_and_grad(_loss, argnums=(0, 1))

    def one_microbatch(ex, loss_target):
        ex = dict(ex)
        diff = ex.pop(TWIN_DIFF_INPUT)
        return grad_fn(weights, diff, {**shared, **ex}, loss_target)

    if N_MICROBATCH == 1:
        loss, (grad_w, grad_x) = one_microbatch(per_example, given["loss_target"])
    else:
        def body(carry, xs):
            loss_sum, grad_sum = carry
            l_k, (gw_k, gx_k) = one_microbatch(xs[0], xs[1])
            with _jax.named_scope("update"):
                return (loss_sum + l_k, _jax.tree.map(_jnp.add, grad_sum, gw_k)), gx_k

        init = (_jnp.zeros((), _jnp.float32), _jax.tree.map(_jnp.zeros_like, weights))
        (loss, grad_w), grad_x = _jax.lax.scan(body, init, (per_example, given["loss_target"]))
    with _jax.named_scope("update"):
        delta_w, new_m, new_v = {}, {}, {}
        for n in TWIN_WEIGHTS:
            delta_w[n], new_m[n], new_v[n] = _adamw(weights[n], grad_w[n], given["m_" + n], given["v_" + n])
    return (loss, grad_x, *[grad_w[n] for n in TWIN_WEIGHTS], *[delta_w[n] for n in TWIN_WEIGHTS],
            *[new_m[n] for n in TWIN_WEIGHTS], *[new_v[n] for n in TWIN_WEIGHTS])
```

```python
import functools
import math

import jax
import jax.numpy as jnp
from jax import lax
from jax.experimental import pallas as pl
from jax.experimental.pallas import tpu as pltpu

F32 = jnp.float32
BF16 = jnp.bfloat16

D_MODEL = 2048
RET_WIDTH = 1024
RET_HEADS = 8
HEAD_DIM = 128
CHUNK = 64
SSM_WIDTH = 1024
SSM_GROUP = 16
SSM_GROUPS = 64
SSM_STATE = 64
D_FF = 5632
IN_WIDTH = 5120
ROPE_BASE = 10000.0
EPS = 1e-6
N_DEV = 8
MESH_AXES = ("x", "y", "c")

WIN_BLK = IN_WIDTH // N_DEV
FF_BLK = D_FF // N_DEV
RET_BLOCK = 256
S5_TILE = 256
S5_GB = 8
S5_NBLK = SSM_GROUPS // S5_GB
S5_LANES = S5_GB * SSM_STATE
LANE = 128

ADAM_LR = 0.001
ADAM_B1 = 0.9
ADAM_B2 = 0.999
ADAM_EPS = 1e-08
ADAM_WD = 0.01
ADAM_STEP = 10
ADAM_BC1 = 1.0 - ADAM_B1 ** ADAM_STEP
ADAM_BC2 = 1.0 - ADAM_B2 ** ADAM_STEP

VMEM_LIMIT = 56 * 1024 * 1024

NT = (((1,), (1,)), ((), ()))
TN = (((0,), (0,)), ((), ()))


def _params(n_grid):
    return pltpu.CompilerParams(dimension_semantics=("arbitrary",) * n_grid, vmem_limit_bytes=VMEM_LIMIT)


def _dot(a, b):
    return jnp.dot(a, b, preferred_element_type=F32)


def _dot_nt(a, b):
    return lax.dot_general(a, b, NT, preferred_element_type=F32)


def _dot_tn(a, b):
    return lax.dot_general(a, b, TN, preferred_element_type=F32)


def _sigmoid(x):
    return 1.0 / (1.0 + jnp.exp(-x))


_GELU_C = math.sqrt(2.0 / math.pi)
_GELU_A = 0.044715


def _gelu(x):
    t = jnp.tanh(_GELU_C * (x + _GELU_A * x * x * x))
    return 0.5 * x * (1.0 + t)


def _gelu_and_grad(x):
    t = jnp.tanh(_GELU_C * (x + _GELU_A * x * x * x))
    g = 0.5 * (1.0 + t) + 0.5 * x * (1.0 - t * t) * _GELU_C * (1.0 + 3.0 * _GELU_A * x * x)
    return 0.5 * x * (1.0 + t), g


def _rms_bwd(dy, x, r, g):
    w = dy * g
    dx = r * w - x * (r * r * r) * jnp.mean(w * x, axis=-1, keepdims=True)
    return dx, dy * x * r


def _in_proj_fwd(x, g, w, tm):
    T = x.shape[0]

    def body(x_ref, g_ref, w_ref, proj_ref, h_ref, r_ref):
        @pl.when(pl.program_id(1) == 0)
        def _():
            xf = x_ref[...]
            r = lax.rsqrt(jnp.mean(xf * xf, axis=-1, keepdims=True) + EPS)
            h_ref[...] = (xf * r * g_ref[...]).astype(BF16)
            r_ref[...] = r
        proj_ref[...] = _dot(h_ref[...], w_ref[...])

    return pl.pallas_call(
        body, name="in_proj_fwd", grid=(T // tm, N_DEV),
        in_specs=[pl.BlockSpec((tm, D_MODEL), lambda i, j: (i, 0)),
                  pl.BlockSpec((1, D_MODEL), lambda i, j: (0, 0)),
                  pl.BlockSpec((None, D_MODEL, WIN_BLK), lambda i, j: (j, 0, 0))],
        out_specs=[pl.BlockSpec((tm, WIN_BLK), lambda i, j: (i, j)),
                   pl.BlockSpec((tm, D_MODEL), lambda i, j: (i, 0)),
                   pl.BlockSpec((tm, 1), lambda i, j: (i, 0))],
        out_shape=[jax.ShapeDtypeStruct((T, IN_WIDTH), F32),
                   jax.ShapeDtypeStruct((T, D_MODEL), BF16),
                   jax.ShapeDtypeStruct((T, 1), F32)],
        compiler_params=_params(2),
    )(x, g, w)


def _ret_common(q_ref, k_ref, v_ref, cos_ref, sin_ref, mask_ref, rd_ref, sin_state):
    c = cos_ref[...]
    s = sin_ref[...]
    q = q_ref[...]
    q = q * c + pltpu.roll(q, HEAD_DIM // 2, 1) * s
    k = k_ref[...]
    k = (k * c + pltpu.roll(k, HEAD_DIM // 2, 1) * s) * (HEAD_DIM ** -0.5)
    qb = q.astype(BF16)
    kb = k.astype(BF16)
    vb = v_ref[...].astype(BF16)
    pm = (_dot_nt(qb, kb) * mask_ref[...]).astype(BF16)
    qd = (q * rd_ref[...]).astype(BF16)
    o = _dot(pm, vb) + _dot(qd, sin_state.astype(BF16))
    return q, k, qb, kb, vb, pm, qd, o


def _ret_specs(T, rev):
    nb = T // RET_BLOCK
    blk = (lambda b: nb - 1 - b) if rev else (lambda b: b)
    col = lambda off: pl.BlockSpec((RET_BLOCK, HEAD_DIM), lambda h, b: (blk(b), off + h))
    tab = pl.BlockSpec((RET_BLOCK, HEAD_DIM), lambda h, b: (blk(b), 0))
    per_head = pl.BlockSpec((None, RET_BLOCK, HEAD_DIM), lambda h, b: (h, 0, 0))
    return dict(
        q=col(0), k=col(RET_HEADS), v=col(2 * RET_HEADS), g=col(3 * RET_HEADS), tab=tab,
        mask=pl.BlockSpec((None, RET_BLOCK, RET_BLOCK), lambda h, b: (h, 0, 0)),
        dec=per_head,
        gtb=pl.BlockSpec((None, 1, HEAD_DIM), lambda h, b: (h, 0, 0)),
        gn=pl.BlockSpec((1, HEAD_DIM), lambda h, b: (0, h)),
        state=pl.BlockSpec((None, None, HEAD_DIM, HEAD_DIM), lambda h, b: (h, blk(b), 0, 0)),
        rows=pl.BlockSpec((RET_BLOCK, HEAD_DIM), lambda h, b: (blk(b), h)),
    )


def _ret_fwd(proj, cosf, sinf, mask, rowdec, kdec, gtb, gn):
    T = proj.shape[0]
    nb = T // RET_BLOCK
    sp = _ret_specs(T, False)

    def body(q_ref, k_ref, v_ref, g_ref, cos_ref, sin_ref, mask_ref, rd_ref, kd_ref, gtb_ref, gn_ref,
             y_ref, sb_ref, st):
        @pl.when(pl.program_id(1) == 0)
        def _():
            st[...] = jnp.zeros_like(st)
        s_in = st[...]
        sb_ref[...] = s_in
        q, k, qb, kb, vb, pm, qd, o = _ret_common(q_ref, k_ref, v_ref, cos_ref, sin_ref, mask_ref, rd_ref, s_in)
        st[...] = gtb_ref[...] * s_in + _dot_tn((k * kd_ref[...]).astype(BF16), vb)
        mu = jnp.mean(o, axis=-1, keepdims=True)
        oc = o - mu
        n = oc * lax.rsqrt(jnp.mean(oc * oc, axis=-1, keepdims=True) + EPS)
        gt = g_ref[...]
        y_ref[...] = (gt * _sigmoid(gt) * (n * gn_ref[...])).astype(BF16)

    return pl.pallas_call(
        body, name="ret_fwd", grid=(RET_HEADS, nb),
        in_specs=[sp["q"], sp["k"], sp["v"], sp["g"], sp["tab"], sp["tab"], sp["mask"], sp["dec"], sp["dec"],
                  sp["gtb"], sp["gn"]],
        out_specs=[sp["rows"], sp["state"]],
        out_shape=[jax.ShapeDtypeStruct((T, RET_WIDTH), BF16),
                   jax.ShapeDtypeStruct((RET_HEADS, nb, HEAD_DIM, HEAD_DIM), F32)],
        scratch_shapes=[pltpu.VMEM((HEAD_DIM, HEAD_DIM), F32)],
        compiler_params=_params(2),
    )(proj, proj, proj, proj, cosf, sinf, mask, rowdec, kdec, gtb, gn)


def _scan(re, im, ar, ai, reverse):
    n = re.shape[0]
    row = lax.broadcasted_iota(jnp.int32, re.shape, 0)
    s = 1
    while s < n:
        if reverse:
            keep = row < n - s
            sr = jnp.where(keep, pltpu.roll(re, n - s, 0), 0.0)
            si = jnp.where(keep, pltpu.roll(im, n - s, 0), 0.0)
        else:
            keep = row >= s
            sr = jnp.where(keep, pltpu.roll(re, s, 0), 0.0)
            si = jnp.where(keep, pltpu.roll(im, s, 0), 0.0)
        re, im = re + ar * sr - ai * si, im + ar * si + ai * sr
        ar, ai = ar * ar - ai * ai, 2.0 * ar * ai
        s *= 2
    return re, im


def _power_table(ar, ai, reverse):
    shape = (S5_TILE, S5_LANES)
    row = lax.broadcasted_iota(jnp.int32, shape, 0)
    at = (S5_TILE - 1) if reverse else 0
    re = jnp.where(row == at, jnp.broadcast_to(ar, shape), 0.0)
    im = jnp.where(row == at, jnp.broadcast_to(ai, shape), 0.0)
    return _scan(re, im, ar, ai, reverse)


def _s5_specs(T, rev):
    nt = T // S5_TILE
    tt = (lambda t: nt - 1 - t) if rev else (lambda t: t)
    return dict(
        u=pl.BlockSpec((S5_TILE, LANE), lambda b, t: (tt(t), 4 * RET_HEADS + b)),
        rows=pl.BlockSpec((S5_TILE, LANE), lambda b, t: (tt(t), b)),
        to_state=pl.BlockSpec((None, LANE, S5_LANES), lambda b, t: (b, 0, 0)),
        from_state=pl.BlockSpec((None, S5_LANES, LANE), lambda b, t: (b, 0, 0)),
        lam=pl.BlockSpec((None, 2, S5_LANES), lambda b, t: (b, 0, 0)),
        d=pl.BlockSpec((1, LANE), lambda b, t: (0, b)),
        bound=pl.BlockSpec((None, None, 2, S5_LANES), lambda b, t: (b, tt(t), 0, 0)),
    )


def _s5_fwd(proj, bre, bim, cre_t, cim_t, lam, d):
    T = proj.shape[0]
    nt = T // S5_TILE
    sp = _s5_specs(T, False)

    def body(u_ref, bre_ref, bim_ref, cre_ref, cim_ref, lam_ref, d_ref, y_ref, bound_ref, carry, ptab):
        lr = lam_ref[0:1, :]
        li = lam_ref[1:2, :]

        @pl.when(pl.program_id(1) == 0)
        def _():
            carry[...] = jnp.zeros_like(carry)
            pr0, pi0 = _power_table(lr, li, False)
            ptab[0] = pr0
            ptab[1] = pi0

        u = u_ref[...]
        ub = u.astype(BF16)
        sr, si = _scan(_dot(ub, bre_ref[...]), _dot(ub, bim_ref[...]), lr, li, False)
        cr = carry[0:1, :]
        ci = carry[1:2, :]
        bound_ref[...] = carry[...]
        pr = ptab[0]
        pi_ = ptab[1]
        sr = sr + pr * cr - pi_ * ci
        si = si + pr * ci + pi_ * cr
        carry[0:1, :] = sr[S5_TILE - 1:S5_TILE, :]
        carry[1:2, :] = si[S5_TILE - 1:S5_TILE, :]
        y_ref[...] = _dot(sr.astype(BF16), cre_ref[...]) - _dot(si.astype(BF16), cim_ref[...]) + d_ref[...] * u

    return pl.pallas_call(
        body, name="s5_fwd", grid=(S5_NBLK, nt),
        in_specs=[sp["u"], sp["to_state"], sp["to_state"], sp["from_state"], sp["from_state"], sp["lam"], sp["d"]],
        out_specs=[sp["rows"], sp["bound"]],
        out_shape=[jax.ShapeDtypeStruct((T, SSM_WIDTH), F32),
                   jax.ShapeDtypeStruct((S5_NBLK, nt, 2, S5_LANES), F32)],
        scratch_shapes=[pltpu.VMEM((2, S5_LANES), F32), pltpu.VMEM((2, S5_TILE, S5_LANES), F32)],
        compiler_params=_params(2),
    )(proj, bre, bim, cre_t, cim_t, lam, d)


def _glu_fwd(y, w, b, og, tm):
    T = y.shape[0]

    def body(y_ref, w_ref, b_ref, og_ref, z_ref, o_ref, r_ref):
        y1 = _gelu(y_ref[...])
        z = _dot(y1.astype(BF16), w_ref[...]) + b_ref[...]
        y2 = y1 * _sigmoid(z)
        r = lax.rsqrt(jnp.mean(y2 * y2, axis=-1, keepdims=True) + EPS)
        z_ref[...] = z
        o_ref[...] = (y2 * r * og_ref[...]).astype(BF16)
        r_ref[...] = r

    row = pl.BlockSpec((tm, SSM_WIDTH), lambda i: (i, 0))
    vec = pl.BlockSpec((1, SSM_WIDTH), lambda i: (0, 0))
    return pl.pallas_call(
        body, name="glu_fwd", grid=(T // tm,),
        in_specs=[row, pl.BlockSpec((SSM_WIDTH, SSM_WIDTH), lambda i: (0, 0)), vec, vec],
        out_specs=[row, row, pl.BlockSpec((tm, 1), lambda i: (i, 0))],
        out_shape=[jax.ShapeDtypeStruct((T, SSM_WIDTH), F32), jax.ShapeDtypeStruct((T, SSM_WIDTH), BF16),
                   jax.ShapeDtypeStruct((T, 1), F32)],
        compiler_params=_params(1),
    )(y, w, b, og)


def _out_proj_fwd(x, y_ret, y_ssm, w, g, tm):
    T = x.shape[0]

    def body(x_ref, a_ref, b_ref, w_ref, g_ref, x2_ref, h_ref, r_ref):
        x2 = x_ref[...] + _dot(a_ref[...], w_ref[0:RET_WIDTH, :]) + _dot(b_ref[...], w_ref[RET_WIDTH:D_MODEL, :])
        r = lax.rsqrt(jnp.mean(x2 * x2, axis=-1, keepdims=True) + EPS)
        x2_ref[...] = x2
        h_ref[...] = (x2 * r * g_ref[...]).astype(BF16)
        r_ref[...] = r

    full = pl.BlockSpec((tm, D_MODEL), lambda i: (i, 0))
    half = pl.BlockSpec((tm, RET_WIDTH), lambda i: (i, 0))
    return pl.pallas_call(
        body, name="out_proj_fwd", grid=(T // tm,),
        in_specs=[full, half, half, pl.BlockSpec((D_MODEL, D_MODEL), lambda i: (0, 0)),
                  pl.BlockSpec((1, D_MODEL), lambda i: (0, 0))],
        out_specs=[full, full, pl.BlockSpec((tm, 1), lambda i: (i, 0))],
        out_shape=[jax.ShapeDtypeStruct((T, D_MODEL), F32), jax.ShapeDtypeStruct((T, D_MODEL), BF16),
                   jax.ShapeDtypeStruct((T, 1), F32)],
        compiler_params=_params(1),
    )(x, y_ret, y_ssm, w, g)


def _ffn_up(h, wg, wu, tm):
    T = h.shape[0]

    def body(h_ref, wg_ref, wu_ref, a_ref, b_ref, f_ref):
        hb = h_ref[...]
        a = _dot(hb, wg_ref[...])
        b = _dot(hb, wu_ref[...])
        a_ref[...] = a.astype(BF16)
        b_ref[...] = b.astype(BF16)
        f_ref[...] = (a * _sigmoid(a) * b).astype(BF16)

    wspec = pl.BlockSpec((None, D_MODEL, FF_BLK), lambda j, i: (j, 0, 0))
    ospec = pl.BlockSpec((None, tm, FF_BLK), lambda j, i: (j, i, 0))
    oshape = jax.ShapeDtypeStruct((N_DEV, T, FF_BLK), BF16)
    return pl.pallas_call(
        body, name="ffn_up", grid=(N_DEV, T // tm),
        in_specs=[pl.BlockSpec((tm, D_MODEL), lambda j, i: (i, 0)), wspec, wspec],
        out_specs=[ospec, ospec, ospec], out_shape=[oshape, oshape, oshape],
        compiler_params=_params(2),
    )(h, wg, wu)


def _ffn_down_loss(f, wd, x2, tgt, g, tm):
    T = x2.shape[0]

    def body(f_ref, w_ref, x2_ref, t_ref, g_ref, dx_ref, dxb_ref, loss_ref, dg_ref, acc):
        i = pl.program_id(0)
        k = pl.program_id(1)

        @pl.when(k == 0)
        def _():
            acc[...] = x2_ref[...]

        acc[...] += _dot(f_ref[...], w_ref[...])

        @pl.when(k == N_DEV - 1)
        def _():
            x3 = acc[...]
            gv = g_ref[...]
            r = lax.rsqrt(jnp.mean(x3 * x3, axis=-1, keepdims=True) + EPS)
            err = x3 * r * gv - t_ref[...]
            tile_loss = 0.5 * jnp.sum(jnp.mean(err * err, axis=-1, keepdims=True), axis=0, keepdims=True)
            dy = err * (1.0 / D_MODEL)
            dx, dgt = _rms_bwd(dy, x3, r, gv)
            dx_ref[...] = dx
            dxb_ref[...] = dx.astype(BF16)
            dgs = jnp.sum(dgt, axis=0, keepdims=True)

            @pl.when(i == 0)
            def _():
                loss_ref[...] = jnp.zeros_like(loss_ref)
                dg_ref[...] = jnp.zeros_like(dg_ref)

            loss_ref[...] += jnp.broadcast_to(tile_loss, loss_ref.shape)
            dg_ref[...] += dgs

    full = pl.BlockSpec((tm, D_MODEL), lambda i, k: (i, 0))
    vec = pl.BlockSpec((1, D_MODEL), lambda i, k: (0, 0))
    return pl.pallas_call(
        body, name="ffn_down_loss", grid=(T // tm, N_DEV),
        in_specs=[pl.BlockSpec((None, tm, FF_BLK), lambda i, k: (k, i, 0)),
                  pl.BlockSpec((None, FF_BLK, D_MODEL), lambda i, k: (k, 0, 0)), full, full, vec],
        out_specs=[full, full, pl.BlockSpec((8, LANE), lambda i, k: (0, 0)), vec],
        out_shape=[jax.ShapeDtypeStruct((T, D_MODEL), F32), jax.ShapeDtypeStruct((T, D_MODEL), BF16),
                   jax.ShapeDtypeStruct((8, LANE), F32), jax.ShapeDtypeStruct((1, D_MODEL), F32)],
        scratch_shapes=[pltpu.VMEM((tm, D_MODEL), F32)],
        compiler_params=_params(2),
    )(f, wd, x2, tgt, g)


def _ffn_bwd_act(dxb, wd, a, b, tm):
    T = dxb.shape[0]

    def body(dx_ref, w_ref, a_ref, b_ref, da_ref, db_ref):
        df = _dot_nt(dx_ref[...], w_ref[...])
        a = a_ref[...].astype(F32)
        b = b_ref[...].astype(F32)
        sg = _sigmoid(a)
        da_ref[...] = (df * b * sg * (1.0 + a * (1.0 - sg))).astype(BF16)
        db_ref[...] = (df * a * sg).astype(BF16)

    blk = pl.BlockSpec((None, tm, FF_BLK), lambda j, i: (j, i, 0))
    oshape = jax.ShapeDtypeStruct((N_DEV, T, FF_BLK), BF16)
    return pl.pallas_call(
        body, name="ffn_bwd_act", grid=(N_DEV, T // tm),
        in_specs=[pl.BlockSpec((tm, D_MODEL), lambda j, i: (i, 0)),
                  pl.BlockSpec((None, FF_BLK, D_MODEL), lambda j, i: (j, 0, 0)), blk, blk],
        out_specs=[blk, blk], out_shape=[oshape, oshape],
        compiler_params=_params(2),
    )(dxb, wd, a, b)


def _ffn_bwd_in(da, db, wg, wu, x2, r2, g, dx3, tm):
    T = x2.shape[0]

    def body(da_ref, db_ref, wg_ref, wu_ref, x_ref, r_ref, g_ref, dx3_ref, dx_ref, dxb_ref, dg_ref, acc):
        i = pl.program_id(0)
        k = pl.program_id(1)

        @pl.when(k == 0)
        def _():
            acc[...] = jnp.zeros_like(acc)

        acc[...] += _dot_nt(da_ref[...], wg_ref[...]) + _dot_nt(db_ref[...], wu_ref[...])

        @pl.when(k == N_DEV - 1)
        def _():
            dxn, dgt = _rms_bwd(acc[...], x_ref[...], r_ref[...], g_ref[...])
            dx = dx3_ref[...] + dxn
            dx_ref[...] = dx
            dxb_ref[...] = dx.astype(BF16)

            @pl.when(i == 0)
            def _():
                dg_ref[...] = jnp.zeros_like(dg_ref)

            dg_ref[...] += jnp.sum(dgt, axis=0, keepdims=True)

    full = pl.BlockSpec((tm, D_MODEL), lambda i, k: (i, 0))
    vec = pl.BlockSpec((1, D_MODEL), lambda i, k: (0, 0))
    ablk = pl.BlockSpec((None, tm, FF_BLK), lambda i, k: (k, i, 0))
    wblk = pl.BlockSpec((None, D_MODEL, FF_BLK), lambda i, k: (k, 0, 0))
    return pl.pallas_call(
        body, name="ffn_bwd_in", grid=(T // tm, N_DEV),
        in_specs=[ablk, ablk, wblk, wblk, full, pl.BlockSpec((tm, 1), lambda i, k: (i, 0)), vec, full],
        out_specs=[full, full, vec],
        out_shape=[jax.ShapeDtypeStruct((T, D_MODEL), F32), jax.ShapeDtypeStruct((T, D_MODEL), BF16),
                   jax.ShapeDtypeStruct((1, D_MODEL), F32)],
        scratch_shapes=[pltpu.VMEM((tm, D_MODEL), F32)],
        compiler_params=_params(2),
    )(da, db, wg, wu, x2, r2, g, dx3)


def _ffn_wgrad_up(h, da, db, tk):
    T = h.shape[0]
    nk = T // tk

    def body(h_ref, da_ref, db_ref, g_ref, u_ref, accg, accu):
        k = pl.program_id(1)

        @pl.when(k == 0)
        def _():
            accg[...] = jnp.zeros_like(accg)
            accu[...] = jnp.zeros_like(accu)

        hb = h_ref[...]
        accg[...] += _dot_tn(hb, da_ref[...])
        accu[...] += _dot_tn(hb, db_ref[...])

        @pl.when(k == nk - 1)
        def _():
            g_ref[...] = accg[...].astype(BF16)
            u_ref[...] = accu[...].astype(BF16)

    blk = pl.BlockSpec((None, tk, FF_BLK), lambda j, k: (j, k, 0))
    ospec = pl.BlockSpec((None, D_MODEL, FF_BLK), lambda j, k: (j, 0, 0))
    oshape = jax.ShapeDtypeStruct((N_DEV, D_MODEL, FF_BLK), BF16)
    return pl.pallas_call(
        body, name="ffn_wgrad_up", grid=(N_DEV, nk),
        in_specs=[pl.BlockSpec((tk, D_MODEL), lambda j, k: (k, 0)), blk, blk],
        out_specs=[ospec, ospec], out_shape=[oshape, oshape],
        scratch_shapes=[pltpu.VMEM((D_MODEL, FF_BLK), F32), pltpu.VMEM((D_MODEL, FF_BLK), F32)],
        compiler_params=_params(2),
    )(h, da, db)


def _ffn_wgrad_down(f, dxb, tk):
    T = dxb.shape[0]
    nk = T // tk

    def body(f_ref, dx_ref, o_ref, acc):
        k = pl.program_id(1)

        @pl.when(k == 0)
        def _():
            acc[...] = jnp.zeros_like(acc)

        acc[...] += _dot_tn(f_ref[...], dx_ref[...])

        @pl.when(k == nk - 1)
        def _():
            o_ref[...] = acc[...].astype(BF16)

    return pl.pallas_call(
        body, name="ffn_wgrad_down", grid=(N_DEV, nk),
        in_specs=[pl.BlockSpec((None, tk, FF_BLK), lambda j, k: (j, k, 0)),
                  pl.BlockSpec((tk, D_MODEL), lambda j, k: (k, 0))],
        out_specs=pl.BlockSpec((None, FF_BLK, D_MODEL), lambda j, k: (j, 0, 0)),
        out_shape=jax.ShapeDtypeStruct((N_DEV, FF_BLK, D_MODEL), BF16),
        scratch_shapes=[pltpu.VMEM((FF_BLK, D_MODEL), F32)],
        compiler_params=_params(2),
    )(f, dxb)


def _out_proj_bwd(dxb, w, tm):
    T = dxb.shape[0]

    def body(dx_ref, w_ref, a_ref, b_ref):
        dxv = dx_ref[...]
        a_ref[...] = _dot_nt(dxv, w_ref[0:RET_WIDTH, :])
        b_ref[...] = _dot_nt(dxv, w_ref[RET_WIDTH:D_MODEL, :])

    half = pl.BlockSpec((tm, RET_WIDTH), lambda i: (i, 0))
    oshape = jax.ShapeDtypeStruct((T, RET_WIDTH), F32)
    return pl.pallas_call(
        body, name="out_proj_bwd", grid=(T // tm,),
        in_specs=[pl.BlockSpec((tm, D_MODEL), lambda i: (i, 0)), pl.BlockSpec((D_MODEL, D_MODEL), lambda i: (0, 0))],
        out_specs=[half, half], out_shape=[oshape, oshape],
        compiler_params=_params(1),
    )(dxb, w)


def _wgrad_rows(name, a, b, tk):
    T, M = a.shape
    N = b.shape[1]
    nk = T // tk

    def body(a_ref, b_ref, o_ref, acc):
        k = pl.program_id(0)

        @pl.when(k == 0)
        def _():
            acc[...] = jnp.zeros_like(acc)

        acc[...] += _dot_tn(a_ref[...], b_ref[...])

        @pl.when(k == nk - 1)
        def _():
            o_ref[...] = acc[...].astype(BF16)

    return pl.pallas_call(
        body, name=name, grid=(nk,),
        in_specs=[pl.BlockSpec((tk, M), lambda k: (k, 0)), pl.BlockSpec((tk, N), lambda k: (k, 0))],
        out_specs=pl.BlockSpec((M, N), lambda k: (0, 0)),
        out_shape=jax.ShapeDtypeStruct((M, N), BF16),
        scratch_shapes=[pltpu.VMEM((M, N), F32)],
        compiler_params=_params(1),
    )(a, b)


def _glu_bwd(y, z, r, dyo, w, og, tm):
    T = y.shape[0]

    def body(y_ref, z_ref, r_ref, d_ref, w_ref, og_ref, dy_ref, dw_ref, db_ref, dog_ref):
        @pl.when(pl.program_id(0) == 0)
        def _():
            dw_ref[...] = jnp.zeros_like(dw_ref)
            db_ref[...] = jnp.zeros_like(db_ref)
            dog_ref[...] = jnp.zeros_like(dog_ref)

        y1, g1 = _gelu_and_grad(y_ref[...])
        sg = _sigmoid(z_ref[...])
        y2 = y1 * sg
        dy2, dogt = _rms_bwd(d_ref[...], y2, r_ref[...], og_ref[...])
        dog_ref[...] += jnp.sum(dogt, axis=0, keepdims=True)
        dz = dy2 * y1 * sg * (1.0 - sg)
        db_ref[...] += jnp.sum(dz, axis=0, keepdims=True)
        dzb = dz.astype(BF16)
        dw_ref[...] += _dot_tn(y1.astype(BF16), dzb)
        dy_ref[...] = (dy2 * sg + _dot_nt(dzb, w_ref[...])) * g1

    row = pl.BlockSpec((tm, SSM_WIDTH), lambda i: (i, 0))
    vec = pl.BlockSpec((1, SSM_WIDTH), lambda i: (0, 0))
    sq = pl.BlockSpec((SSM_WIDTH, SSM_WIDTH), lambda i: (0, 0))
    return pl.pallas_call(
        body, name="glu_bwd", grid=(T // tm,),
        in_specs=[row, row, pl.BlockSpec((tm, 1), lambda i: (i, 0)), row, sq, vec],
        out_specs=[row, sq, vec, vec],
        out_shape=[jax.ShapeDtypeStruct((T, SSM_WIDTH), F32), jax.ShapeDtypeStruct((SSM_WIDTH, SSM_WIDTH), F32),
                   jax.ShapeDtypeStruct((1, SSM_WIDTH), F32), jax.ShapeDtypeStruct((1, SSM_WIDTH), F32)],
        compiler_params=_params(1),
    )(y, z, r, dyo, w, og)


def _s5_bwd(proj, dy, bound, bre, bim, bre_t, bim_t, cre, cim, lam, d):
    T = proj.shape[0]
    nt = T // S5_TILE
    sp = _s5_specs(T, True)

    def body(u_ref, dy_ref, bound_ref, bre_ref, bim_ref, bret_ref, bimt_ref, cre_ref, cim_ref, lam_ref, d_ref,
             du_ref, dbre_ref, dbim_ref, dcre_ref, dcim_ref, dlam_ref, dd_ref, carry, ptab, qtab):
        lr = lam_ref[0:1, :]
        li = lam_ref[1:2, :]

        @pl.when(pl.program_id(1) == 0)
        def _():
            carry[...] = jnp.zeros_like(carry)
            pr0, pi0 = _power_table(lr, li, False)
            ptab[0] = pr0
            ptab[1] = pi0
            qr0, qi0 = _power_table(lr, -li, True)
            qtab[0] = qr0
            qtab[1] = qi0
            for ref in (dbre_ref, dbim_ref, dcre_ref, dcim_ref, dlam_ref, dd_ref):
                ref[...] = jnp.zeros_like(ref)

        u = u_ref[...]
        ub = u.astype(BF16)
        dyv = dy_ref[...]
        dyb = dyv.astype(BF16)
        sr, si = _scan(_dot(ub, bre_ref[...]), _dot(ub, bim_ref[...]), lr, li, False)
        b_r = bound_ref[0:1, :]
        b_i = bound_ref[1:2, :]
        pr = ptab[0]
        pi_ = ptab[1]
        sr = sr + pr * b_r - pi_ * b_i
        si = si + pr * b_i + pi_ * b_r
        gr, gi = _scan(_dot(dyb, cre_ref[...]), -_dot(dyb, cim_ref[...]), lr, -li, True)
        cr = carry[0:1, :]
        ci = carry[1:2, :]
        qr = qtab[0]
        qi = qtab[1]
        gr = gr + qr * cr - qi * ci
        gi = gi + qr * ci + qi * cr
        carry[0:1, :] = gr[0:1, :]
        carry[1:2, :] = gi[0:1, :]
        grb = gr.astype(BF16)
        gib = gi.astype(BF16)
        du_ref[...] = (_dot(grb, bret_ref[...]) + _dot(gib, bimt_ref[...]) + d_ref[...] * dyv).astype(BF16)
        dbre_ref[...] += _dot_tn(grb, ub)
        dbim_ref[...] += _dot_tn(gib, ub)
        dcre_ref[...] += _dot_tn(dyb, sr.astype(BF16))
        dcim_ref[...] -= _dot_tn(dyb, si.astype(BF16))
        dd_ref[...] += jnp.sum(dyv * u, axis=0, keepdims=True)
        row = lax.broadcasted_iota(jnp.int32, sr.shape, 0)
        pr_ = jnp.where(row == 0, jnp.broadcast_to(b_r, sr.shape), pltpu.roll(sr, 1, 0))
        pi2 = jnp.where(row == 0, jnp.broadcast_to(b_i, si.shape), pltpu.roll(si, 1, 0))
        dlam_ref[0:1, :] += jnp.sum(gr * pr_ + gi * pi2, axis=0, keepdims=True)
        dlam_ref[1:2, :] += jnp.sum(gi * pr_ - gr * pi2, axis=0, keepdims=True)

    acc_ts = pl.BlockSpec((None, S5_LANES, LANE), lambda b, t: (b, 0, 0))
    acc_fs = pl.BlockSpec((None, LANE, S5_LANES), lambda b, t: (b, 0, 0))
    return pl.pallas_call(
        body, name="s5_bwd", grid=(S5_NBLK, nt),
        in_specs=[sp["u"], sp["rows"], sp["bound"], sp["to_state"], sp["to_state"], sp["from_state"],
                  sp["from_state"], sp["to_state"], sp["to_state"], sp["lam"], sp["d"]],
        out_specs=[sp["rows"], acc_ts, acc_ts, acc_fs, acc_fs, sp["lam"], sp["d"]],
        out_shape=[jax.ShapeDtypeStruct((T, SSM_WIDTH), BF16),
                   jax.ShapeDtypeStruct((S5_NBLK, S5_LANES, LANE), F32),
                   jax.ShapeDtypeStruct((S5_NBLK, S5_LANES, LANE), F32),
                   jax.ShapeDtypeStruct((S5_NBLK, LANE, S5_LANES), F32),
                   jax.ShapeDtypeStruct((S5_NBLK, LANE, S5_LANES), F32),
                   jax.ShapeDtypeStruct((S5_NBLK, 2, S5_LANES), F32),
                   jax.ShapeDtypeStruct((1, SSM_WIDTH), F32)],
        scratch_shapes=[pltpu.VMEM((2, S5_LANES), F32), pltpu.VMEM((2, S5_TILE, S5_LANES), F32),
                        pltpu.VMEM((2, S5_TILE, S5_LANES), F32)],
        compiler_params=_params(2),
    )(proj, dy, bound, bre, bim, bre_t, bim_t, cre, cim, lam, d)


def _ret_bwd(proj, cosf, sinf, mask, rowdec, kdec, gtb, gn, sblk, dyr):
    T = proj.shape[0]
    nb = T // RET_BLOCK
    sp = _ret_specs(T, True)

    def body(q_ref, k_ref, v_ref, g_ref, cos_ref, sin_ref, mask_ref, rd_ref, kd_ref, gtb_ref, gn_ref, sb_ref, dy_ref,
             dq_ref, dk_ref, dv_ref, dg_ref, dgn_ref, dst):
        @pl.when(pl.program_id(1) == 0)
        def _():
            dst[...] = jnp.zeros_like(dst)
            dgn_ref[...] = jnp.zeros_like(dgn_ref)

        s_in = sb_ref[...]
        q, k, qb, kb, vb, pm, qd, o = _ret_common(q_ref, k_ref, v_ref, cos_ref, sin_ref, mask_ref, rd_ref, s_in)
        mu = jnp.mean(o, axis=-1, keepdims=True)
        oc = o - mu
        rstd = lax.rsqrt(jnp.mean(oc * oc, axis=-1, keepdims=True) + EPS)
        n = oc * rstd
        gt = g_ref[...]
        sg = _sigmoid(gt)
        sil = gt * sg
        gnv = gn_ref[...]
        dyv = dy_ref[...]
        dg_ref[...] = (dyv * (n * gnv) * (sg * (1.0 + gt * (1.0 - sg)))).astype(BF16)
        dgn_ref[...] += jnp.sum(dyv * sil * n, axis=0, keepdims=True)
        dn = dyv * sil * gnv
        do = rstd * (dn - jnp.mean(dn, axis=-1, keepdims=True) - n * jnp.mean(dn * n, axis=-1, keepdims=True))
        dob = do.astype(BF16)
        ds = dst[...]
        dsb = ds.astype(BF16)
        kd = kd_ref[...]
        rd = rd_ref[...]
        dv_ref[...] = (_dot_tn(pm, dob) + _dot((k * kd).astype(BF16), dsb)).astype(BF16)
        dpb = (_dot_nt(dob, vb) * mask_ref[...]).astype(BF16)
        dq = _dot(dpb, kb) + _dot_nt(dob, s_in.astype(BF16)) * rd
        dk = (_dot_tn(dpb, qb) + _dot_nt(vb, dsb) * kd) * (HEAD_DIM ** -0.5)
        dst[...] = gtb_ref[...] * ds + _dot_tn(qd, dob)
        c = cos_ref[...]
        s = sin_ref[...]
        dq_ref[...] = (dq * c + pltpu.roll(dq * s, HEAD_DIM // 2, 1)).astype(BF16)
        dk_ref[...] = (dk * c + pltpu.roll(dk * s, HEAD_DIM // 2, 1)).astype(BF16)

    oshape = jax.ShapeDtypeStruct((T, RET_WIDTH), BF16)
    return pl.pallas_call(
        body, name="ret_bwd", grid=(RET_HEADS, nb),
        in_specs=[sp["q"], sp["k"], sp["v"], sp["g"], sp["tab"], sp["tab"], sp["mask"], sp["dec"], sp["dec"],
                  sp["gtb"], sp["gn"], sp["state"], sp["rows"]],
        out_specs=[sp["rows"], sp["rows"], sp["rows"], sp["rows"], sp["gn"]],
        out_shape=[oshape, oshape, oshape, oshape, jax.ShapeDtypeStruct((1, RET_WIDTH), F32)],
        scratch_shapes=[pltpu.VMEM((HEAD_DIM, HEAD_DIM), F32)],
        compiler_params=_params(2),
    )(proj, proj, proj, proj, cosf, sinf, mask, rowdec, kdec, gtb, gn, sblk, dyr)


def _in_proj_bwd(dproj, w, x, r1, g, dx2, tm):
    T = x.shape[0]

    def body(dp_ref, w_ref, x_ref, r_ref, g_ref, dx2_ref, gx_ref, dg_ref, acc):
        i = pl.program_id(0)
        k = pl.program_id(1)

        @pl.when(k == 0)
        def _():
            acc[...] = jnp.zeros_like(acc)

        acc[...] += _dot_nt(dp_ref[...], w_ref[...])

        @pl.when(k == N_DEV - 1)
        def _():
            dxn, dgt = _rms_bwd(acc[...], x_ref[...], r_ref[...], g_ref[...])
            gx_ref[...] = dx2_ref[...] + dxn

            @pl.when(i == 0)
            def _():
                dg_ref[...] = jnp.zeros_like(dg_ref)

            dg_ref[...] += jnp.sum(dgt, axis=0, keepdims=True)

    full = pl.BlockSpec((tm, D_MODEL), lambda i, k: (i, 0))
    vec = pl.BlockSpec((1, D_MODEL), lambda i, k: (0, 0))
    return pl.pallas_call(
        body, name="in_proj_bwd", grid=(T // tm, N_DEV),
        in_specs=[pl.BlockSpec((tm, WIN_BLK), lambda i, k: (i, k)),
                  pl.BlockSpec((None, D_MODEL, WIN_BLK), lambda i, k: (k, 0, 0)),
                  full, pl.BlockSpec((tm, 1), lambda i, k: (i, 0)), vec, full],
        out_specs=[full, vec],
        out_shape=[jax.ShapeDtypeStruct((T, D_MODEL), F32), jax.ShapeDtypeStruct((1, D_MODEL), F32)],
        scratch_shapes=[pltpu.VMEM((tm, D_MODEL), F32)],
        compiler_params=_params(2),
    )(dproj, w, x, r1, g, dx2)


def _in_proj_wgrad(h, dproj, tk):
    T = h.shape[0]
    nk = T // tk

    def body(h_ref, dp_ref, o_ref, acc):
        k = pl.program_id(1)

        @pl.when(k == 0)
        def _():
            acc[...] = jnp.zeros_like(acc)

        acc[...] += _dot_tn(h_ref[...], dp_ref[...])

        @pl.when(k == nk - 1)
        def _():
            o_ref[...] = acc[...].astype(BF16)

    return pl.pallas_call(
        body, name="in_proj_wgrad", grid=(N_DEV, nk),
        in_specs=[pl.BlockSpec((tk, D_MODEL), lambda j, k: (k, 0)), pl.BlockSpec((tk, WIN_BLK), lambda j, k: (k, j))],
        out_specs=pl.BlockSpec((None, D_MODEL, WIN_BLK), lambda j, k: (j, 0, 0)),
        out_shape=jax.ShapeDtypeStruct((N_DEV, D_MODEL, WIN_BLK), BF16),
        scratch_shapes=[pltpu.VMEM((D_MODEL, WIN_BLK), F32)],
        compiler_params=_params(2),
    )(h, dproj)


def _rope_tables(T):
    half = HEAD_DIM // 2
    freqs = ROPE_BASE ** (-jnp.arange(half, dtype=F32) / half)
    ang = jnp.arange(T, dtype=F32)[:, None] * freqs[None, :]
    c = jnp.cos(ang)
    s = jnp.sin(ang)
    return jnp.concatenate([c, c], axis=1), jnp.concatenate([-s, s], axis=1)


def _retention_tables():
    hh = jnp.arange(RET_HEADS, dtype=F32)
    log_g = jnp.log1p(-(2.0 ** (-5.0 - hh)))[:, None, None]
    i = jnp.arange(RET_BLOCK)
    ci = (i // CHUNK)[:, None]
    cj = (i // CHUNK)[None, :]
    diff = (i[:, None] - i[None, :]).astype(F32)
    expo = jnp.where(ci == cj, jnp.abs(diff), diff)
    mask = jnp.where((cj <= ci)[None], jnp.exp(log_g * expo[None]), 0.0)
    r = jnp.arange(RET_BLOCK, dtype=F32)[None, :, None]
    ones = jnp.ones((1, 1, HEAD_DIM), F32)
    rowdec = jnp.exp(log_g * (r + 1.0)) * ones
    kdec = jnp.exp(log_g * (RET_BLOCK - 1.0 - r)) * ones
    gtb = jnp.exp(log_g * float(RET_BLOCK)) * ones
    return mask, rowdec, kdec, gtb


def _s5_discretise(a_re, a_im, log_dt, b_re, b_im):
    lam = lax.complex(a_re, a_im)
    dt = jnp.exp(log_dt)[:, None]
    lam_bar = jnp.exp(lam * dt)
    b_bar = ((lam_bar - 1.0) / lam)[..., None] * lax.complex(b_re, b_im)
    return jnp.real(lam_bar), jnp.imag(lam_bar), jnp.real(b_bar), jnp.imag(b_bar)


def _to_state_blockdiag(m):
    eye = jnp.eye(S5_GB, dtype=m.dtype)
    t = jnp.einsum("bgpc,gh->bgchp", m.reshape(S5_NBLK, S5_GB, SSM_STATE, SSM_GROUP), eye)
    return t.reshape(S5_NBLK, LANE, S5_LANES)


def _from_state_blockdiag(m):
    eye = jnp.eye(S5_GB, dtype=m.dtype)
    t = jnp.einsum("bgcp,gh->bgphc", m.reshape(S5_NBLK, S5_GB, SSM_GROUP, SSM_STATE), eye)
    return t.reshape(S5_NBLK, S5_LANES, LANE)


def _diag_of_state_major(acc):
    eye = jnp.eye(S5_GB, dtype=acc.dtype)
    t = acc.reshape(S5_NBLK, S5_GB, SSM_STATE, S5_GB, SSM_GROUP)
    return jnp.einsum("bgphc,gh->bgpc", t, eye).reshape(SSM_GROUPS, SSM_STATE, SSM_GROUP)


def _diag_of_channel_major(acc):
    eye = jnp.eye(S5_GB, dtype=acc.dtype)
    t = acc.reshape(S5_NBLK, S5_GB, SSM_GROUP, S5_GB, SSM_STATE)
    return jnp.einsum("bgchp,gh->bgcp", t, eye).reshape(SSM_GROUPS, SSM_GROUP, SSM_STATE)


SMALL_PARTIALS = (("norm_mix_g", 2048), ("ret_gn_g", 1024), ("lam_re", 4096), ("lam_im", 4096),
                  ("bbar_re", 65536), ("bbar_im", 65536), ("c_re", 65536), ("c_im", 65536),
                  ("ssm_d", 1024), ("b_glu", 1024), ("out_g", 1024), ("norm_ffn_g", 2048), ("norm_final_g", 2048))


def _local_step(x, tgt, wts, sm, tm=512):
    T = x.shape[0]
    cosf, sinf = _rope_tables(T)
    mask, rowdec, kdec, gtb = _retention_tables()
    lbr, lbi, bbr, bbi = _s5_discretise(sm["ssm_a_re"], sm["ssm_a_im"], sm["ssm_log_dt"], sm["ssm_b_re"],
                                        sm["ssm_b_im"])
    bre = _to_state_blockdiag(bbr).astype(BF16)
    bim = _to_state_blockdiag(bbi).astype(BF16)
    cre_t = _from_state_blockdiag(sm["ssm_c_re"]).astype(BF16)
    cim_t = _from_state_blockdiag(sm["ssm_c_im"]).astype(BF16)
    bre_t = jnp.swapaxes(bre, 1, 2)
    bim_t = jnp.swapaxes(bim, 1, 2)
    cre = jnp.swapaxes(cre_t, 1, 2)
    cim = jnp.swapaxes(cim_t, 1, 2)
    lam = jnp.stack([lbr.reshape(S5_NBLK, S5_LANES), lbi.reshape(S5_NBLK, S5_LANES)], axis=1)
    row = lambda v: v.reshape(1, -1)
    g_mix, g_ffn, g_fin = row(sm["norm_mix_g"]), row(sm["norm_ffn_g"]), row(sm["norm_final_g"])
    gn, dsk, bglu, og = row(sm["ret_gn_g"]), row(sm["ssm_d"]), row(sm["ssm_b_glu"]), row(sm["ssm_out_g"])

    proj, h1, r1 = _in_proj_fwd(x, g_mix, wts["w_in"], tm)
    y_ret, sblk = _ret_fwd(proj, cosf, sinf, mask, rowdec, kdec, gtb, gn)
    y_s5, bound = _s5_fwd(proj, bre, bim, cre_t, cim_t, lam, dsk)
    z, y_ssm, r_ssm = _glu_fwd(y_s5, wts["w_glu"], bglu, og, 256)
    x2, h2, r2 = _out_proj_fwd(x, y_ret, y_ssm, wts["w_out"], g_ffn, 256)
    a, b, f = _ffn_up(h2, wts["w_gate"], wts["w_up"], tm)
    dx3, dx3b, loss8, dg_fin = _ffn_down_loss(f, wts["w_down"], x2, tgt, g_fin, 256)

    da, db = _ffn_bwd_act(dx3b, wts["w_down"], a, b, tm)
    dw_down = _ffn_wgrad_down(f, dx3b, tm)
    dw_gate, dw_up = _ffn_wgrad_up(h2, da, db, tm)
    dx2, dx2b, dg_ffn = _ffn_bwd_in(da, db, wts["w_gate"], wts["w_up"], x2, r2, g_ffn, dx3, 256)
    dy_ret, dy_ssm = _out_proj_bwd(dx2b, wts["w_out"], 256)
    dw_out = jnp.concatenate([_wgrad_rows("out_proj_wgrad_ret", y_ret, dx2b, tm),
                              _wgrad_rows("out_proj_wgrad_ssm", y_ssm, dx2b, tm)], axis=0)
    dy_s5, dw_glu, db_glu, dog = _glu_bwd(y_s5, z, r_ssm, dy_ssm, wts["w_glu"], og, 256)
    du, dbre, dbim, dcre, dcim, dlam, dd = _s5_bwd(proj, dy_s5, bound, bre, bim, bre_t, bim_t, cre, cim, lam, dsk)
    dq, dk, dv, dgate, dgn = _ret_bwd(proj, cosf, sinf, mask, rowdec, kdec, gtb, gn, sblk, dy_ret)
    dproj = jnp.concatenate([dq, dk, dv, dgate, du], axis=1)
    grad_x, dg_mix = _in_proj_bwd(dproj, wts["w_in"], x, r1, g_mix, dx2, 256)
    dw_in = _in_proj_wgrad(h1, dproj, tm)

    big = dict(w_in=dw_in, w_glu=dw_glu.astype(BF16), w_out=dw_out, w_gate=dw_gate, w_up=dw_up, w_down=dw_down)
    small = dict(norm_mix_g=dg_mix, ret_gn_g=dgn, lam_re=dlam[:, 0], lam_im=dlam[:, 1],
                 bbar_re=_diag_of_state_major(dbre), bbar_im=_diag_of_state_major(dbim),
                 c_re=_diag_of_channel_major(dcre), c_im=_diag_of_channel_major(dcim),
                 ssm_d=dd, b_glu=db_glu, out_g=dog, norm_ffn_g=dg_ffn, norm_final_g=dg_fin)
    return loss8[0, 0], grad_x, big, small


def _small_grads(summed, sm):
    _, vjp = jax.vjp(_s5_discretise, sm["ssm_a_re"], sm["ssm_a_im"], sm["ssm_log_dt"], sm["ssm_b_re"], sm["ssm_b_im"])
    gp = (SSM_GROUPS, SSM_STATE)
    da_re, da_im, dlog_dt, db_re, db_im = vjp((summed["lam_re"].reshape(gp), summed["lam_im"].reshape(gp),
                                               summed["bbar_re"].reshape(gp + (SSM_GROUP,)),
                                               summed["bbar_im"].reshape(gp + (SSM_GROUP,))))
    return dict(norm_mix_g=summed["norm_mix_g"], ret_gn_g=summed["ret_gn_g"], ssm_a_re=da_re, ssm_a_im=da_im,
                ssm_log_dt=dlog_dt, ssm_b_re=db_re, ssm_b_im=db_im,
                ssm_c_re=summed["c_re"].reshape(SSM_GROUPS, SSM_GROUP, SSM_STATE),
                ssm_c_im=summed["c_im"].reshape(SSM_GROUPS, SSM_GROUP, SSM_STATE),
                ssm_d=summed["ssm_d"], ssm_b_glu=summed["b_glu"], ssm_out_g=summed["out_g"],
                norm_ffn_g=summed["norm_ffn_g"], norm_final_g=summed["norm_final_g"])


HBM_SPEC = pl.BlockSpec(memory_space=pltpu.HBM)


def _my_block():
    return 4 * lax.axis_index("x") + 2 * lax.axis_index("y") + lax.axis_index("c")


def _peer(k):
    px = lax.axis_index("x") ^ ((k >> 2) & 1)
    py = lax.axis_index("y") ^ ((k >> 1) & 1)
    pc = lax.axis_index("c") ^ (k & 1)
    return (px, py, pc), 4 * px + 2 * py + pc


def _all_to_all(name, payloads, gather):
    n = len(payloads)

    def body(*refs):
        ins = refs[:n]
        outs = refs[n:2 * n]
        send_sems, recv_sems, local_sems = refs[2 * n:]
        me = _my_block()
        local, sends = [], []
        for i in range(n):
            src_of = (lambda blk, i=i: ins[i]) if gather else (lambda blk, i=i: ins[i].at[blk])
            cp = pltpu.make_async_copy(src_of(me), outs[i].at[me], local_sems.at[i])
            cp.start()
            local.append(cp)
            for k in range(1, N_DEV):
                dev, blk = _peer(k)
                rc = pltpu.make_async_remote_copy(
                    src_ref=src_of(blk), dst_ref=outs[i].at[me], send_sem=send_sems.at[i, k - 1],
                    recv_sem=recv_sems.at[i, k - 1], device_id=dev, device_id_type=pl.DeviceIdType.MESH)
                rc.start()
                sends.append(rc)
        for i in range(n):
            for k in range(1, N_DEV):
                dev, blk = _peer(k)
                pltpu.make_async_remote_copy(
                    src_ref=outs[i].at[blk], dst_ref=outs[i].at[blk], send_sem=send_sems.at[i, k - 1],
                    recv_sem=recv_sems.at[i, k - 1], device_id=dev, device_id_type=pl.DeviceIdType.MESH).wait_recv()
        for rc in sends:
            rc.wait_send()
        for cp in local:
            cp.wait()

    if gather:
        out_shape = [jax.ShapeDtypeStruct((N_DEV,) + p.shape, p.dtype) for p in payloads]
    else:
        out_shape = [jax.ShapeDtypeStruct(p.shape, p.dtype) for p in payloads]
    return pl.pallas_call(
        body, name=name, in_specs=[HBM_SPEC] * n, out_specs=[HBM_SPEC] * n, out_shape=out_shape,
        scratch_shapes=[pltpu.SemaphoreType.DMA((n, N_DEV - 1)), pltpu.SemaphoreType.DMA((n, N_DEV - 1)),
                        pltpu.SemaphoreType.DMA((n,))],
    )(*payloads)


def _adamw_math(w, g, m, v):
    m2 = ADAM_B1 * m + (1.0 - ADAM_B1) * g
    v2 = ADAM_B2 * v + (1.0 - ADAM_B2) * (g * g)
    delta = -ADAM_LR * ((m2 / ADAM_BC1) / (jnp.sqrt(v2 / ADAM_BC2) + ADAM_EPS) + ADAM_WD * w)
    return delta, m2, v2


def _adamw_shard(name, parts, w, m, v, tr):
    rows, cols = w.shape

    def body(p_ref, w_ref, m_ref, v_ref, g_ref, d_ref, m2_ref, v2_ref):
        g = p_ref[0].astype(F32)
        for s in range(1, N_DEV):
            g = g + p_ref[s].astype(F32)
        d, m2, v2 = _adamw_math(w_ref[...], g, m_ref[...], v_ref[...])
        g_ref[...] = g
        d_ref[...] = d
        m2_ref[...] = m2
        v2_ref[...] = v2

    blk = pl.BlockSpec((tr, cols), lambda i: (i, 0))
    oshape = jax.ShapeDtypeStruct((rows, cols), F32)
    return pl.pallas_call(
        body, name=name, grid=(rows // tr,),
        in_specs=[pl.BlockSpec((N_DEV, tr, cols), lambda i: (0, i, 0)), blk, blk, blk],
        out_specs=[blk, blk, blk, blk], out_shape=[oshape] * 4,
        compiler_params=_params(1),
    )(parts, w, m, v)


def _sum_partials(parts):
    rows = parts.shape[1]

    def body(p_ref, o_ref):
        g = p_ref[0]
        for s in range(1, N_DEV):
            g = g + p_ref[s]
        o_ref[...] = g

    return pl.pallas_call(
        body, name="small_grad_sum", grid=(1,),
        in_specs=[pl.BlockSpec((N_DEV, rows, LANE), lambda i: (0, 0, 0))],
        out_specs=pl.BlockSpec((rows, LANE), lambda i: (0, 0)),
        out_shape=jax.ShapeDtypeStruct((rows, LANE), F32),
        compiler_params=_params(1),
    )(parts)


def _adamw_small(w, g, m, v):
    rows = w.shape[0]

    def body(w_ref, g_ref, m_ref, v_ref, d_ref, m2_ref, v2_ref):
        d, m2, v2 = _adamw_math(w_ref[...], g_ref[...], m_ref[...], v_ref[...])
        d_ref[...] = d
        m2_ref[...] = m2
        v2_ref[...] = v2

    blk = pl.BlockSpec((rows, LANE), lambda i: (0, 0))
    oshape = jax.ShapeDtypeStruct((rows, LANE), F32)
    return pl.pallas_call(
        body, name="adamw_small", grid=(1,), in_specs=[blk] * 4, out_specs=[blk] * 3, out_shape=[oshape] * 3,
        compiler_params=_params(1),
    )(w, g, m, v)


def _pack(arrays):
    cols = []
    for a in arrays:
        flat = a.reshape(-1).astype(F32)
        pad = (-flat.shape[0]) % LANE
        cols.append(jnp.pad(flat, (0, pad)) if pad else flat)
    return jnp.concatenate(cols).reshape(-1, LANE)


def _unpack(packed, shapes):
    flat = packed.reshape(-1)
    out, off = [], 0
    for shp in shapes:
        n = math.prod(shp)
        out.append(flat[off:off + n].reshape(shp))
        off += n + ((-n) % LANE)
    return out


WEIGHTS = ("norm_mix_g", "w_in", "ret_gn_g", "ssm_a_re", "ssm_a_im", "ssm_log_dt", "ssm_b_re", "ssm_b_im",
           "ssm_c_re", "ssm_c_im", "ssm_d", "ssm_w_glu", "ssm_b_glu", "ssm_out_g", "w_out", "norm_ffn_g", "w_gate",
           "w_up", "w_down", "norm_final_g")
BIG = ("w_in", "ssm_w_glu", "w_out", "w_gate", "w_up", "w_down")
SMALL = tuple(n for n in WEIGHTS if n not in BIG)
ADAM_ROWS = {"w_in": 256, "ssm_w_glu": 128, "w_out": 128, "w_gate": 256, "w_up": 256, "w_down": 176}


def kernel(x, norm_mix_g, w_in, ret_gn_g, ssm_a_re, ssm_a_im, ssm_log_dt, ssm_b_re, ssm_b_im, ssm_c_re, ssm_c_im, ssm_d, ssm_w_glu, ssm_b_glu, ssm_out_g, w_out, norm_ffn_g, w_gate, w_up, w_down, norm_final_g, loss_target, m_norm_mix_g, m_w_in, m_ret_gn_g, m_ssm_a_re, m_ssm_a_im, m_ssm_log_dt, m_ssm_b_re, m_ssm_b_im, m_ssm_c_re, m_ssm_c_im, m_ssm_d, m_ssm_w_glu, m_ssm_b_glu, m_ssm_out_g, m_w_out, m_norm_ffn_g, m_w_gate, m_w_up, m_w_down, m_norm_final_g, v_norm_mix_g, v_w_in, v_ret_gn_g, v_ssm_a_re, v_ssm_a_im, v_ssm_log_dt, v_ssm_b_re, v_ssm_b_im, v_ssm_c_re, v_ssm_c_im, v_ssm_d, v_ssm_w_glu, v_ssm_b_glu, v_ssm_out_g, v_w_out, v_norm_ffn_g, v_w_gate, v_w_up, v_w_down, v_norm_final_g):
    given = dict(locals())
    w = {n: given[n] for n in WEIGHTS}
    m = {n: given["m_" + n] for n in WEIGHTS}
    v = {n: given["v_" + n] for n in WEIGHTS}
    drop = lambda n, a: a if n == "norm_final_g" else a[0]
    w0 = {n: drop(n, w[n]) for n in WEIGHTS}
    m0 = {n: drop(n, m[n]) for n in WEIGHTS}
    v0 = {n: drop(n, v[n]) for n in WEIGHTS}

    gathered = _all_to_all("weight_gather", [w0[n].astype(BF16) for n in BIG], gather=True)
    gw = dict(zip(BIG, gathered))
    wts = dict(w_in=gw["w_in"], w_glu=gw["ssm_w_glu"].reshape(SSM_WIDTH, SSM_WIDTH),
               w_out=gw["w_out"].reshape(D_MODEL, D_MODEL), w_gate=gw["w_gate"], w_up=gw["w_up"], w_down=gw["w_down"])

    sm = {n: w0[n] for n in SMALL}
    loss_local, grad_x, big, small = _local_step(x[0], loss_target[0], wts, sm)
    loss = lax.psum(loss_local, MESH_AXES)

    send = [big["w_in"], big["w_glu"].reshape(N_DEV, SSM_WIDTH // N_DEV, SSM_WIDTH),
            big["w_out"].reshape(N_DEV, D_MODEL // N_DEV, D_MODEL), big["w_gate"], big["w_up"], big["w_down"]]
    landed = _all_to_all("grad_exchange", send, gather=False)
    packed = _pack([small[n] for n, _ in SMALL_PARTIALS])
    (small_landed,) = _all_to_all("small_grad_gather", [packed], gather=True)
    summed = dict(zip([n for n, _ in SMALL_PARTIALS],
                      _unpack(_sum_partials(small_landed), [(sz,) for _, sz in SMALL_PARTIALS])))
    gsmall = _small_grads(summed, sm)

    grads, delta, new_m, new_v = {}, {}, {}, {}
    for n, parts in zip(BIG, landed):
        g, d, m2, v2 = _adamw_shard("adamw_" + n, parts, w0[n], m0[n], v0[n], ADAM_ROWS[n])
        grads[n], delta[n], new_m[n], new_v[n] = g, d, m2, v2
    shapes = [w0[n].shape for n in SMALL]
    gs = [gsmall[n].reshape(w0[n].shape) for n in SMALL]
    d_p, m_p, v_p = _adamw_small(_pack([w0[n] for n in SMALL]), _pack(gs), _pack([m0[n] for n in SMALL]),
                                 _pack([v0[n] for n in SMALL]))
    for n, g, d, m2, v2 in zip(SMALL, gs, _unpack(d_p, shapes), _unpack(m_p, shapes), _unpack(v_p, shapes)):
        grads[n], delta[n], new_m[n], new_v[n] = g, d, m2, v2

    lift = lambda n, a: a.reshape(w[n].shape)
    return (loss, grad_x[None], *[lift(n, grads[n]) for n in WEIGHTS], *[lift(n, delta[n]) for n in WEIGHTS],
            *[lift(n, new_m[n]) for n in WEIGHTS], *[lift(n, new_v[n]) for n in WEIGHTS])
```

```python
import functools
import math

import jax
import jax.numpy as jnp
from jax import lax
from jax.experimental import pallas as pl
from jax.experimental.pallas import tpu as pltpu

F32 = jnp.float32
BF16 = jnp.bfloat16

D_MODEL = 2048
RET_WIDTH = 1024
RET_HEADS = 8
HEAD_DIM = 128
CHUNK = 64
SSM_WIDTH = 1024
SSM_GROUP = 16
SSM_GROUPS = 64
SSM_STATE = 64
D_FF = 5632
IN_WIDTH = 5120
ROPE_BASE = 10000.0
EPS = 1e-6
N_DEV = 8
MESH_AXES = ("x", "y", "c")

WIN_BLK = IN_WIDTH // N_DEV
FF_BLK = D_FF // N_DEV
RET_BLOCK = 256
S5_TILE = 256
S5_GB = 8
S5_NBLK = SSM_GROUPS // S5_GB
S5_LANES = S5_GB * SSM_STATE
LANE = 128

ADAM_LR = 0.001
ADAM_B1 = 0.9
ADAM_B2 = 0.999
ADAM_EPS = 1e-08
ADAM_WD = 0.01
ADAM_STEP = 10
ADAM_BC1 = 1.0 - ADAM_B1 ** ADAM_STEP
ADAM_BC2 = 1.0 - ADAM_B2 ** ADAM_STEP

VMEM_LIMIT = 56 * 1024 * 1024

NT = (((1,), (1,)), ((), ()))
TN = (((0,), (0,)), ((), ()))


def _params(n_grid):
    return pltpu.CompilerParams(dimension_semantics=("arbitrary",) * n_grid, vmem_limit_bytes=VMEM_LIMIT)


def _dot(a, b):
    return jnp.dot(a, b, preferred_element_type=F32)


def _dot_nt(a, b):
    return lax.dot_general(a, b, NT, preferred_element_type=F32)


def _dot_tn(a, b):
    return lax.dot_general(a, b, TN, preferred_element_type=F32)


def _sigmoid(x):
    return 1.0 / (1.0 + jnp.exp(-x))


_GELU_C = math.sqrt(2.0 / math.pi)
_GELU_A = 0.044715


def _gelu(x):
    t = jnp.tanh(_GELU_C * (x + _GELU_A * x * x * x))
    return 0.5 * x * (1.0 + t)


def _gelu_and_grad(x):
    t = jnp.tanh(_GELU_C * (x + _GELU_A * x * x * x))
    g = 0.5 * (1.0 + t) + 0.5 * x * (1.0 - t * t) * _GELU_C * (1.0 + 3.0 * _GELU_A * x * x)
    return 0.5 * x * (1.0 + t), g


def _rms_bwd(dy, x, r, g):
    w = dy * g
    dx = r * w - x * (r * r * r) * jnp.mean(w * x, axis=-1, keepdims=True)
    return dx, dy * x * r


HBM_SPEC = pl.BlockSpec(memory_space=pltpu.HBM)


def _my_block():
    return 4 * lax.axis_index("x") + 2 * lax.axis_index("y") + lax.axis_index("c")


def _peer(k):
    px = lax.axis_index("x") ^ ((k >> 2) & 1)
    py = lax.axis_index("y") ^ ((k >> 1) & 1)
    pc = lax.axis_index("c") ^ (k & 1)
    return (px, py, pc), 4 * px + 2 * py + pc


class _Exchange:
    def __init__(self, payloads, gather):
        self.payloads = list(payloads)
        self.n = len(self.payloads)
        self.gather = [gather] * self.n if isinstance(gather, bool) else list(gather)

    def out_shape(self):
        return [jax.ShapeDtypeStruct(((N_DEV,) if g else ()) + p.shape, p.dtype)
                for p, g in zip(self.payloads, self.gather)]

    def scratch_shapes(self):
        return [pltpu.SemaphoreType.DMA((self.n, N_DEV - 1)), pltpu.SemaphoreType.DMA((self.n, N_DEV - 1)),
                pltpu.SemaphoreType.DMA((self.n,))]

    def _copies(self, ins, outs, sems, incoming):
        send_sems, recv_sems, local_sems = sems
        me = _my_block()
        src_of = lambda i, blk: ins[i] if self.gather[i] else ins[i].at[blk]
        local, remote = [], []
        for i in range(self.n):
            if not incoming:
                local.append(pltpu.make_async_copy(src_of(i, me), outs[i].at[me], local_sems.at[i]))
            for k in range(1, N_DEV):
                dev, blk = _peer(k)
                src, dst = (outs[i].at[blk], outs[i].at[blk]) if incoming else (src_of(i, blk), outs[i].at[me])
                remote.append(pltpu.make_async_remote_copy(
                    src_ref=src, dst_ref=dst, send_sem=send_sems.at[i, k - 1], recv_sem=recv_sems.at[i, k - 1],
                    device_id=dev, device_id_type=pl.DeviceIdType.MESH))
        return local, remote

    def start(self, ins, outs, sems):
        local, sends = self._copies(ins, outs, sems, False)
        for cp in local + sends:
            cp.start()

    def wait(self, ins, outs, sems):
        for cp in self._copies(ins, outs, sems, True)[1]:
            cp.wait_recv()
        local, sends = self._copies(ins, outs, sems, False)
        for cp in sends:
            cp.wait_send()
        for cp in local:
            cp.wait()


def _pcall(body, name, grid, in_specs, out_specs, out_shape, scratch_shapes, args, carry=None):
    n_in, n_out, n_scr = len(in_specs), len(out_specs), len(scratch_shapes)
    if carry is None:
        return pl.pallas_call(body, name=name, grid=grid, in_specs=in_specs, out_specs=out_specs, out_shape=out_shape,
                              scratch_shapes=scratch_shapes, compiler_params=_params(len(grid)))(*args)
    nx = carry.n

    def wrapped(*refs):
        cin, xin = refs[:n_in], refs[n_in:n_in + nx]
        cout, xout = refs[n_in + nx:n_in + nx + n_out], refs[n_in + nx + n_out:n_in + 2 * nx + n_out]
        rest = refs[n_in + 2 * nx + n_out:]
        cscr, sems = rest[:n_scr], rest[n_scr:]
        first = functools.reduce(jnp.logical_and, [pl.program_id(a) == 0 for a in range(len(grid))])
        last = functools.reduce(jnp.logical_and, [pl.program_id(a) == grid[a] - 1 for a in range(len(grid))])

        @pl.when(first)
        def _():
            carry.start(xin, xout, sems)

        body(*cin, *cout, *cscr)

        @pl.when(last)
        def _():
            carry.wait(xin, xout, sems)

    return pl.pallas_call(
        wrapped, name=name, grid=grid, in_specs=list(in_specs) + [HBM_SPEC] * nx,
        out_specs=list(out_specs) + [HBM_SPEC] * nx, out_shape=list(out_shape) + carry.out_shape(),
        scratch_shapes=list(scratch_shapes) + carry.scratch_shapes(), compiler_params=_params(len(grid)),
    )(*args, *carry.payloads)


def _exchange_call(name, payloads, gather):
    ex = _Exchange(payloads, gather)

    def body(*refs):
        ins, outs, sems = refs[:ex.n], refs[ex.n:2 * ex.n], refs[2 * ex.n:]
        ex.start(ins, outs, sems)
        ex.wait(ins, outs, sems)

    return pl.pallas_call(body, name=name, in_specs=[HBM_SPEC] * ex.n, out_specs=[HBM_SPEC] * ex.n,
                          out_shape=ex.out_shape(), scratch_shapes=ex.scratch_shapes())(*ex.payloads)


def _in_proj_fwd(x, g, w, tm, carry=None):
    T = x.shape[0]

    def body(x_ref, g_ref, w_ref, proj_ref, h_ref, r_ref):
        @pl.when(pl.program_id(1) == 0)
        def _():
            xf = x_ref[...]
            r = lax.rsqrt(jnp.mean(xf * xf, axis=-1, keepdims=True) + EPS)
            h_ref[...] = (xf * r * g_ref[...]).astype(BF16)
            r_ref[...] = r
        proj_ref[...] = _dot(h_ref[...], w_ref[...])

    return _pcall(
        body, "in_proj_fwd", (T // tm, N_DEV),
        [pl.BlockSpec((tm, D_MODEL), lambda i, j: (i, 0)),
         pl.BlockSpec((1, D_MODEL), lambda i, j: (0, 0)),
         pl.BlockSpec((None, D_MODEL, WIN_BLK), lambda i, j: (j, 0, 0))],
        [pl.BlockSpec((tm, WIN_BLK), lambda i, j: (i, j)),
         pl.BlockSpec((tm, D_MODEL), lambda i, j: (i, 0)),
         pl.BlockSpec((tm, 1), lambda i, j: (i, 0))],
        [jax.ShapeDtypeStruct((T, IN_WIDTH), F32),
         jax.ShapeDtypeStruct((T, D_MODEL), BF16),
         jax.ShapeDtypeStruct((T, 1), F32)],
        [], (x, g, w), carry)


def _ret_common(q_ref, k_ref, v_ref, cos_ref, sin_ref, mask_ref, rd_ref, sin_state):
    c = cos_ref[...]
    s = sin_ref[...]
    q = q_ref[...]
    q = q * c + pltpu.roll(q, HEAD_DIM // 2, 1) * s
    k = k_ref[...]
    k = (k * c + pltpu.roll(k, HEAD_DIM // 2, 1) * s) * (HEAD_DIM ** -0.5)
    qb = q.astype(BF16)
    kb = k.astype(BF16)
    vb = v_ref[...].astype(BF16)
    pm = (_dot_nt(qb, kb) * mask_ref[...]).astype(BF16)
    qd = (q * rd_ref[...]).astype(BF16)
    o = _dot(pm, vb) + _dot(qd, sin_state.astype(BF16))
    return q, k, qb, kb, vb, pm, qd, o


def _ret_specs(T, rev):
    nb = T // RET_BLOCK
    blk = (lambda b: nb - 1 - b) if rev else (lambda b: b)
    col = lambda off: pl.BlockSpec((RET_BLOCK, HEAD_DIM), lambda h, b: (blk(b), off + h))
    tab = pl.BlockSpec((RET_BLOCK, HEAD_DIM), lambda h, b: (blk(b), 0))
    per_head = pl.BlockSpec((None, RET_BLOCK, HEAD_DIM), lambda h, b: (h, 0, 0))
    return dict(
        q=col(0), k=col(RET_HEADS), v=col(2 * RET_HEADS), g=col(3 * RET_HEADS), tab=tab,
        mask=pl.BlockSpec((None, RET_BLOCK, RET_BLOCK), lambda h, b: (h, 0, 0)),
        dec=per_head,
        gtb=pl.BlockSpec((None, 1, HEAD_DIM), lambda h, b: (h, 0, 0)),
        gn=pl.BlockSpec((1, HEAD_DIM), lambda h, b: (0, h)),
        state=pl.BlockSpec((None, None, HEAD_DIM, HEAD_DIM), lambda h, b: (h, blk(b), 0, 0)),
        rows=pl.BlockSpec((RET_BLOCK, HEAD_DIM), lambda h, b: (blk(b), h)),
    )


def _ret_fwd(proj, cosf, sinf, mask, rowdec, kdec, gtb, gn, carry=None):
    T = proj.shape[0]
    nb = T // RET_BLOCK
    sp = _ret_specs(T, False)

    def body(q_ref, k_ref, v_ref, g_ref, cos_ref, sin_ref, mask_ref, rd_ref, kd_ref, gtb_ref, gn_ref,
             y_ref, sb_ref, st):
        @pl.when(pl.program_id(1) == 0)
        def _():
            st[...] = jnp.zeros_like(st)
        s_in = st[...]
        sb_ref[...] = s_in
        q, k, qb, kb, vb, pm, qd, o = _ret_common(q_ref, k_ref, v_ref, cos_ref, sin_ref, mask_ref, rd_ref, s_in)
        st[...] = gtb_ref[...] * s_in + _dot_tn((k * kd_ref[...]).astype(BF16), vb)
        mu = jnp.mean(o, axis=-1, keepdims=True)
        oc = o - mu
        n = oc * lax.rsqrt(jnp.mean(oc * oc, axis=-1, keepdims=True) + EPS)
        gt = g_ref[...]
        y_ref[...] = (gt * _sigmoid(gt) * (n * gn_ref[...])).astype(BF16)

    return _pcall(
        body, "ret_fwd", (RET_HEADS, nb),
        [sp["q"], sp["k"], sp["v"], sp["g"], sp["tab"], sp["tab"], sp["mask"], sp["dec"], sp["dec"],
         sp["gtb"], sp["gn"]],
        [sp["rows"], sp["state"]],
        [jax.ShapeDtypeStruct((T, RET_WIDTH), BF16),
         jax.ShapeDtypeStruct((RET_HEADS, nb, HEAD_DIM, HEAD_DIM), F32)],
        [pltpu.VMEM((HEAD_DIM, HEAD_DIM), F32)],
        (proj, proj, proj, proj, cosf, sinf, mask, rowdec, kdec, gtb, gn), carry)


def _scan(re, im, ar, ai, reverse):
    n = re.shape[0]
    row = lax.broadcasted_iota(jnp.int32, re.shape, 0)
    s = 1
    while s < n:
        if reverse:
            keep = row < n - s
            sr = jnp.where(keep, pltpu.roll(re, n - s, 0), 0.0)
            si = jnp.where(keep, pltpu.roll(im, n - s, 0), 0.0)
        else:
            keep = row >= s
            sr = jnp.where(keep, pltpu.roll(re, s, 0), 0.0)
            si = jnp.where(keep, pltpu.roll(im, s, 0), 0.0)
        re, im = re + ar * sr - ai * si, im + ar * si + ai * sr
        ar, ai = ar * ar - ai * ai, 2.0 * ar * ai
        s *= 2
    return re, im


def _power_table(ar, ai, reverse):
    shape = (S5_TILE, S5_LANES)
    row = lax.broadcasted_iota(jnp.int32, shape, 0)
    at = (S5_TILE - 1) if reverse else 0
    re = jnp.where(row == at, jnp.broadcast_to(ar, shape), 0.0)
    im = jnp.where(row == at, jnp.broadcast_to(ai, shape), 0.0)
    return _scan(re, im, ar, ai, reverse)


def _s5_specs(T, rev):
    nt = T // S5_TILE
    tt = (lambda t: nt - 1 - t) if rev else (lambda t: t)
    return dict(
        u=pl.BlockSpec((S5_TILE, LANE), lambda b, t: (tt(t), 4 * RET_HEADS + b)),
        rows=pl.BlockSpec((S5_TILE, LANE), lambda b, t: (tt(t), b)),
        to_state=pl.BlockSpec((None, LANE, S5_LANES), lambda b, t: (b, 0, 0)),
        from_state=pl.BlockSpec((None, S5_LANES, LANE), lambda b, t: (b, 0, 0)),
        lam=pl.BlockSpec((None, 2, S5_LANES), lambda b, t: (b, 0, 0)),
        d=pl.BlockSpec((1, LANE), lambda b, t: (0, b)),
        bound=pl.BlockSpec((None, None, 2, S5_LANES), lambda b, t: (b, tt(t), 0, 0)),
    )


def _s5_fwd(proj, bre, bim, cre_t, cim_t, lam, d):
    T = proj.shape[0]
    nt = T // S5_TILE
    sp = _s5_specs(T, False)

    def body(u_ref, bre_ref, bim_ref, cre_ref, cim_ref, lam_ref, d_ref, y_ref, bound_ref, carry, ptab):
        lr = lam_ref[0:1, :]
        li = lam_ref[1:2, :]

        @pl.when(pl.program_id(1) == 0)
        def _():
            carry[...] = jnp.zeros_like(carry)
            pr0, pi0 = _power_table(lr, li, False)
            ptab[0] = pr0
            ptab[1] = pi0

        u = u_ref[...]
        ub = u.astype(BF16)
        sr, si = _scan(_dot(ub, bre_ref[...]), _dot(ub, bim_ref[...]), lr, li, False)
        cr = carry[0:1, :]
        ci = carry[1:2, :]
        bound_ref[...] = carry[...]
        pr = ptab[0]
        pi_ = ptab[1]
        sr = sr + pr * cr - pi_ * ci
        si = si + pr * ci + pi_ * cr
        carry[0:1, :] = sr[S5_TILE - 1:S5_TILE, :]
        carry[1:2, :] = si[S5_TILE - 1:S5_TILE, :]
        y_ref[...] = _dot(sr.astype(BF16), cre_ref[...]) - _dot(si.astype(BF16), cim_ref[...]) + d_ref[...] * u

    return pl.pallas_call(
        body, name="s5_fwd", grid=(S5_NBLK, nt),
        in_specs=[sp["u"], sp["to_state"], sp["to_state"], sp["from_state"], sp["from_state"], sp["lam"], sp["d"]],
        out_specs=[sp["rows"], sp["bound"]],
        out_shape=[jax.ShapeDtypeStruct((T, SSM_WIDTH), F32),
                   jax.ShapeDtypeStruct((S5_NBLK, nt, 2, S5_LANES), F32)],
        scratch_shapes=[pltpu.VMEM((2, S5_LANES), F32), pltpu.VMEM((2, S5_TILE, S5_LANES), F32)],
        compiler_params=_params(2),
    )(proj, bre, bim, cre_t, cim_t, lam, d)


def _glu_fwd(y, w, b, og, tm):
    T = y.shape[0]

    def body(y_ref, w_ref, b_ref, og_ref, z_ref, o_ref, r_ref):
        y1 = _gelu(y_ref[...])
        z = _dot(y1.astype(BF16), w_ref[...]) + b_ref[...]
        y2 = y1 * _sigmoid(z)
        r = lax.rsqrt(jnp.mean(y2 * y2, axis=-1, keepdims=True) + EPS)
        z_ref[...] = z
        o_ref[...] = (y2 * r * og_ref[...]).astype(BF16)
        r_ref[...] = r

    row = pl.BlockSpec((tm, SSM_WIDTH), lambda i: (i, 0))
    vec = pl.BlockSpec((1, SSM_WIDTH), lambda i: (0, 0))
    return pl.pallas_call(
        body, name="glu_fwd", grid=(T // tm,),
        in_specs=[row, pl.BlockSpec((SSM_WIDTH, SSM_WIDTH), lambda i: (0, 0)), vec, vec],
        out_specs=[row, row, pl.BlockSpec((tm, 1), lambda i: (i, 0))],
        out_shape=[jax.ShapeDtypeStruct((T, SSM_WIDTH), F32), jax.ShapeDtypeStruct((T, SSM_WIDTH), BF16),
                   jax.ShapeDtypeStruct((T, 1), F32)],
        compiler_params=_params(1),
    )(y, w, b, og)


def _out_proj_fwd(x, y_ret, y_ssm, w, g, tm):
    T = x.shape[0]

    def body(x_ref, a_ref, b_ref, w_ref, g_ref, x2_ref, h_ref, r_ref):
        x2 = x_ref[...] + _dot(a_ref[...], w_ref[0:RET_WIDTH, :]) + _dot(b_ref[...], w_ref[RET_WIDTH:D_MODEL, :])
        r = lax.rsqrt(jnp.mean(x2 * x2, axis=-1, keepdims=True) + EPS)
        x2_ref[...] = x2
        h_ref[...] = (x2 * r * g_ref[...]).astype(BF16)
        r_ref[...] = r

    full = pl.BlockSpec((tm, D_MODEL), lambda i: (i, 0))
    half = pl.BlockSpec((tm, RET_WIDTH), lambda i: (i, 0))
    return pl.pallas_call(
        body, name="out_proj_fwd", grid=(T // tm,),
        in_specs=[full, half, half, pl.BlockSpec((D_MODEL, D_MODEL), lambda i: (0, 0)),
                  pl.BlockSpec((1, D_MODEL), lambda i: (0, 0))],
        out_specs=[full, full, pl.BlockSpec((tm, 1), lambda i: (i, 0))],
        out_shape=[jax.ShapeDtypeStruct((T, D_MODEL), F32), jax.ShapeDtypeStruct((T, D_MODEL), BF16),
                   jax.ShapeDtypeStruct((T, 1), F32)],
        compiler_params=_params(1),
    )(x, y_ret, y_ssm, w, g)


def _ffn_up(h, wg, wu, tm, carry=None):
    T = h.shape[0]

    def body(h_ref, wg_ref, wu_ref, a_ref, b_ref, f_ref):
        hb = h_ref[...]
        a = _dot(hb, wg_ref[...])
        b = _dot(hb, wu_ref[...])
        a_ref[...] = a.astype(BF16)
        b_ref[...] = b.astype(BF16)
        f_ref[...] = (a * _sigmoid(a) * b).astype(BF16)

    wspec = pl.BlockSpec((None, D_MODEL, FF_BLK), lambda j, i: (j, 0, 0))
    ospec = pl.BlockSpec((None, tm, FF_BLK), lambda j, i: (j, i, 0))
    oshape = jax.ShapeDtypeStruct((N_DEV, T, FF_BLK), BF16)
    return _pcall(
        body, "ffn_up", (N_DEV, T // tm),
        [pl.BlockSpec((tm, D_MODEL), lambda j, i: (i, 0)), wspec, wspec],
        [ospec, ospec, ospec], [oshape, oshape, oshape], [], (h, wg, wu), carry)


def _ffn_down_loss(f, wd, x2, tgt, g, tm):
    T = x2.shape[0]

    def body(f_ref, w_ref, x2_ref, t_ref, g_ref, dx_ref, dxb_ref, loss_ref, dg_ref, acc):
        i = pl.program_id(0)
        k = pl.program_id(1)

        @pl.when(k == 0)
        def _():
            acc[...] = x2_ref[...]

        acc[...] += _dot(f_ref[...], w_ref[...])

        @pl.when(k == N_DEV - 1)
        def _():
            x3 = acc[...]
            gv = g_ref[...]
            r = lax.rsqrt(jnp.mean(x3 * x3, axis=-1, keepdims=True) + EPS)
            err = x3 * r * gv - t_ref[...]
            tile_loss = 0.5 * jnp.sum(jnp.mean(err * err, axis=-1, keepdims=True), axis=0, keepdims=True)
            dy = err * (1.0 / D_MODEL)
            dx, dgt = _rms_bwd(dy, x3, r, gv)
            dx_ref[...] = dx
            dxb_ref[...] = dx.astype(BF16)
            dgs = jnp.sum(dgt, axis=0, keepdims=True)

            @pl.when(i == 0)
            def _():
                loss_ref[...] = jnp.zeros_like(loss_ref)
                dg_ref[...] = jnp.zeros_like(dg_ref)

            loss_ref[...] += jnp.broadcast_to(tile_loss, loss_ref.shape)
            dg_ref[...] += dgs

    full = pl.BlockSpec((tm, D_MODEL), lambda i, k: (i, 0))
    vec = pl.BlockSpec((1, D_MODEL), lambda i, k: (0, 0))
    return pl.pallas_call(
        body, name="ffn_down_loss", grid=(T // tm, N_DEV),
        in_specs=[pl.BlockSpec((None, tm, FF_BLK), lambda i, k: (k, i, 0)),
                  pl.BlockSpec((None, FF_BLK, D_MODEL), lambda i, k: (k, 0, 0)), full, full, vec],
        out_specs=[full, full, pl.BlockSpec((8, LANE), lambda i, k: (0, 0)), vec],
        out_shape=[jax.ShapeDtypeStruct((T, D_MODEL), F32), jax.ShapeDtypeStruct((T, D_MODEL), BF16),
                   jax.ShapeDtypeStruct((8, LANE), F32), jax.ShapeDtypeStruct((1, D_MODEL), F32)],
        scratch_shapes=[pltpu.VMEM((tm, D_MODEL), F32)],
        compiler_params=_params(2),
    )(f, wd, x2, tgt, g)


def _ffn_bwd_act(dxb, wd, a, b, tm):
    T = dxb.shape[0]

    def body(dx_ref, w_ref, a_ref, b_ref, da_ref, db_ref):
        df = _dot_nt(dx_ref[...], w_ref[...])
        a = a_ref[...].astype(F32)
        b = b_ref[...].astype(F32)
        sg = _sigmoid(a)
        da_ref[...] = (df * b * sg * (1.0 + a * (1.0 - sg))).astype(BF16)
        db_ref[...] = (df * a * sg).astype(BF16)

    blk = pl.BlockSpec((None, tm, FF_BLK), lambda j, i: (j, i, 0))
    oshape = jax.ShapeDtypeStruct((N_DEV, T, FF_BLK), BF16)
    return pl.pallas_call(
        body, name="ffn_bwd_act", grid=(N_DEV, T // tm),
        in_specs=[pl.BlockSpec((tm, D_MODEL), lambda j, i: (i, 0)),
                  pl.BlockSpec((None, FF_BLK, D_MODEL), lambda j, i: (j, 0, 0)), blk, blk],
        out_specs=[blk, blk], out_shape=[oshape, oshape],
        compiler_params=_params(2),
    )(dxb, wd, a, b)


def _ffn_bwd_in(da, db, wg, wu, x2, r2, g, dx3, tm, carry=None):
    T = x2.shape[0]

    def body(da_ref, db_ref, wg_ref, wu_ref, x_ref, r_ref, g_ref, dx3_ref, dx_ref, dxb_ref, dg_ref, acc):
        i = pl.program_id(0)
        k = pl.program_id(1)

        @pl.when(k == 0)
        def _():
            acc[...] = jnp.zeros_like(acc)

        acc[...] += _dot_nt(da_ref[...], wg_ref[...]) + _dot_nt(db_ref[...], wu_ref[...])

        @pl.when(k == N_DEV - 1)
        def _():
            dxn, dgt = _rms_bwd(acc[...], x_ref[...], r_ref[...], g_ref[...])
            dx = dx3_ref[...] + dxn
            dx_ref[...] = dx
            dxb_ref[...] = dx.astype(BF16)

            @pl.when(i == 0)
            def _():
                dg_ref[...] = jnp.zeros_like(dg_ref)

            dg_ref[...] += jnp.sum(dgt, axis=0, keepdims=True)

    full = pl.BlockSpec((tm, D_MODEL), lambda i, k: (i, 0))
    vec = pl.BlockSpec((1, D_MODEL), lambda i, k: (0, 0))
    ablk = pl.BlockSpec((None, tm, FF_BLK), lambda i, k: (k, i, 0))
    wblk = pl.BlockSpec((None, D_MODEL, FF_BLK), lambda i, k: (k, 0, 0))
    return _pcall(
        body, "ffn_bwd_in", (T // tm, N_DEV),
        [ablk, ablk, wblk, wblk, full, pl.BlockSpec((tm, 1), lambda i, k: (i, 0)), vec, full],
        [full, full, vec],
        [jax.ShapeDtypeStruct((T, D_MODEL), F32), jax.ShapeDtypeStruct((T, D_MODEL), BF16),
         jax.ShapeDtypeStruct((1, D_MODEL), F32)],
        [pltpu.VMEM((tm, D_MODEL), F32)], (da, db, wg, wu, x2, r2, g, dx3), carry)


def _ffn_wgrad_up(h, da, db, tk, carry=None):
    T = h.shape[0]
    nk = T // tk

    def body(h_ref, da_ref, db_ref, g_ref, u_ref, accg, accu):
        k = pl.program_id(1)

        @pl.when(k == 0)
        def _():
            accg[...] = jnp.zeros_like(accg)
            accu[...] = jnp.zeros_like(accu)

        hb = h_ref[...]
        accg[...] += _dot_tn(hb, da_ref[...])
        accu[...] += _dot_tn(hb, db_ref[...])

        @pl.when(k == nk - 1)
        def _():
            g_ref[...] = accg[...].astype(BF16)
            u_ref[...] = accu[...].astype(BF16)

    blk = pl.BlockSpec((None, tk, FF_BLK), lambda j, k: (j, k, 0))
    ospec = pl.BlockSpec((None, D_MODEL, FF_BLK), lambda j, k: (j, 0, 0))
    oshape = jax.ShapeDtypeStruct((N_DEV, D_MODEL, FF_BLK), BF16)
    return _pcall(
        body, "ffn_wgrad_up", (N_DEV, nk),
        [pl.BlockSpec((tk, D_MODEL), lambda j, k: (k, 0)), blk, blk],
        [ospec, ospec], [oshape, oshape],
        [pltpu.VMEM((D_MODEL, FF_BLK), F32), pltpu.VMEM((D_MODEL, FF_BLK), F32)], (h, da, db), carry)


def _ffn_wgrad_down(f, dxb, tk):
    T = dxb.shape[0]
    nk = T // tk

    def body(f_ref, dx_ref, o_ref, acc):
        k = pl.program_id(1)

        @pl.when(k == 0)
        def _():
            acc[...] = jnp.zeros_like(acc)

        acc[...] += _dot_tn(f_ref[...], dx_ref[...])

        @pl.when(k == nk - 1)
        def _():
            o_ref[...] = acc[...].astype(BF16)

    return pl.pallas_call(
        body, name="ffn_wgrad_down", grid=(N_DEV, nk),
        in_specs=[pl.BlockSpec((None, tk, FF_BLK), lambda j, k: (j, k, 0)),
                  pl.BlockSpec((tk, D_MODEL), lambda j, k: (k, 0))],
        out_specs=pl.BlockSpec((None, FF_BLK, D_MODEL), lambda j, k: (j, 0, 0)),
        out_shape=jax.ShapeDtypeStruct((N_DEV, FF_BLK, D_MODEL), BF16),
        scratch_shapes=[pltpu.VMEM((FF_BLK, D_MODEL), F32)],
        compiler_params=_params(2),
    )(f, dxb)


def _out_proj_bwd(dxb, w, tm):
    T = dxb.shape[0]

    def body(dx_ref, w_ref, a_ref, b_ref):
        dxv = dx_ref[...]
        a_ref[...] = _dot_nt(dxv, w_ref[0:RET_WIDTH, :])
        b_ref[...] = _dot_nt(dxv, w_ref[RET_WIDTH:D_MODEL, :])

    half = pl.BlockSpec((tm, RET_WIDTH), lambda i: (i, 0))
    oshape = jax.ShapeDtypeStruct((T, RET_WIDTH), F32)
    return pl.pallas_call(
        body, name="out_proj_bwd", grid=(T // tm,),
        in_specs=[pl.BlockSpec((tm, D_MODEL), lambda i: (i, 0)), pl.BlockSpec((D_MODEL, D_MODEL), lambda i: (0, 0))],
        out_specs=[half, half], out_shape=[oshape, oshape],
        compiler_params=_params(1),
    )(dxb, w)


def _wgrad_rows(name, a, b, tk):
    T, M = a.shape
    N = b.shape[1]
    nk = T // tk

    def body(a_ref, b_ref, o_ref, acc):
        k = pl.program_id(0)

        @pl.when(k == 0)
        def _():
            acc[...] = jnp.zeros_like(acc)

        acc[...] += _dot_tn(a_ref[...], b_ref[...])

        @pl.when(k == nk - 1)
        def _():
            o_ref[...] = acc[...].astype(BF16)

    return pl.pallas_call(
        body, name=name, grid=(nk,),
        in_specs=[pl.BlockSpec((tk, M), lambda k: (k, 0)), pl.BlockSpec((tk, N), lambda k: (k, 0))],
        out_specs=pl.BlockSpec((M, N), lambda k: (0, 0)),
        out_shape=jax.ShapeDtypeStruct((M, N), BF16),
        scratch_shapes=[pltpu.VMEM((M, N), F32)],
        compiler_params=_params(1),
    )(a, b)


def _glu_bwd(y, z, r, dyo, w, og, tm):
    T = y.shape[0]

    def body(y_ref, z_ref, r_ref, d_ref, w_ref, og_ref, dy_ref, dw_ref, db_ref, dog_ref):
        @pl.when(pl.program_id(0) == 0)
        def _():
            dw_ref[...] = jnp.zeros_like(dw_ref)
            db_ref[...] = jnp.zeros_like(db_ref)
            dog_ref[...] = jnp.zeros_like(dog_ref)

        y1, g1 = _gelu_and_grad(y_ref[...])
        sg = _sigmoid(z_ref[...])
        y2 = y1 * sg
        dy2, dogt = _rms_bwd(d_ref[...], y2, r_ref[...], og_ref[...])
        dog_ref[...] += jnp.sum(dogt, axis=0, keepdims=True)
        dz = dy2 * y1 * sg * (1.0 - sg)
        db_ref[...] += jnp.sum(dz, axis=0, keepdims=True)
        dzb = dz.astype(BF16)
        dw_ref[...] += _dot_tn(y1.astype(BF16), dzb)
        dy_ref[...] = (dy2 * sg + _dot_nt(dzb, w_ref[...])) * g1

    row = pl.BlockSpec((tm, SSM_WIDTH), lambda i: (i, 0))
    vec = pl.BlockSpec((1, SSM_WIDTH), lambda i: (0, 0))
    sq = pl.BlockSpec((SSM_WIDTH, SSM_WIDTH), lambda i: (0, 0))
    return pl.pallas_call(
        body, name="glu_bwd", grid=(T // tm,),
        in_specs=[row, row, pl.BlockSpec((tm, 1), lambda i: (i, 0)), row, sq, vec],
        out_specs=[row, sq, vec, vec],
        out_shape=[jax.ShapeDtypeStruct((T, SSM_WIDTH), F32), jax.ShapeDtypeStruct((SSM_WIDTH, SSM_WIDTH), F32),
                   jax.ShapeDtypeStruct((1, SSM_WIDTH), F32), jax.ShapeDtypeStruct((1, SSM_WIDTH), F32)],
        compiler_params=_params(1),
    )(y, z, r, dyo, w, og)


def _s5_bwd(proj, dy, bound, bre, bim, bre_t, bim_t, cre, cim, lam, d):
    T = proj.shape[0]
    nt = T // S5_TILE
    sp = _s5_specs(T, True)

    def body(u_ref, dy_ref, bound_ref, bre_ref, bim_ref, bret_ref, bimt_ref, cre_ref, cim_ref, lam_ref, d_ref,
             du_ref, dbre_ref, dbim_ref, dcre_ref, dcim_ref, dlam_ref, dd_ref, carry, ptab, qtab):
        lr = lam_ref[0:1, :]
        li = lam_ref[1:2, :]

        @pl.when(pl.program_id(1) == 0)
        def _():
            carry[...] = jnp.zeros_like(carry)
            pr0, pi0 = _power_table(lr, li, False)
            ptab[0] = pr0
            ptab[1] = pi0
            qr0, qi0 = _power_table(lr, -li, True)
            qtab[0] = qr0
            qtab[1] = qi0
            for ref in (dbre_ref, dbim_ref, dcre_ref, dcim_ref, dlam_ref, dd_ref):
                ref[...] = jnp.zeros_like(ref)

        u = u_ref[...]
        ub = u.astype(BF16)
        dyv = dy_ref[...]
        dyb = dyv.astype(BF16)
        sr, si = _scan(_dot(ub, bre_ref[...]), _dot(ub, bim_ref[...]), lr, li, False)
        b_r = bound_ref[0:1, :]
        b_i = bound_ref[1:2, :]
        pr = ptab[0]
        pi_ = ptab[1]
        sr = sr + pr * b_r - pi_ * b_i
        si = si + pr * b_i + pi_ * b_r
        gr, gi = _scan(_dot(dyb, cre_ref[...]), -_dot(dyb, cim_ref[...]), lr, -li, True)
        cr = carry[0:1, :]
        ci = carry[1:2, :]
        qr = qtab[0]
        qi = qtab[1]
        gr = gr + qr * cr - qi * ci
        gi = gi + qr * ci + qi * cr
        carry[0:1, :] = gr[0:1, :]
        carry[1:2, :] = gi[0:1, :]
        grb = gr.astype(BF16)
        gib = gi.astype(BF16)
        du_ref[...] = (_dot(grb, bret_ref[...]) + _dot(gib, bimt_ref[...]) + d_ref[...] * dyv).astype(BF16)
        dbre_ref[...] += _dot_tn(grb, ub)
        dbim_ref[...] += _dot_tn(gib, ub)
        dcre_ref[...] += _dot_tn(dyb, sr.astype(BF16))
        dcim_ref[...] -= _dot_tn(dyb, si.astype(BF16))
        dd_ref[...] += jnp.sum(dyv * u, axis=0, keepdims=True)
        row = lax.broadcasted_iota(jnp.int32, sr.shape, 0)
        pr_ = jnp.where(row == 0, jnp.broadcast_to(b_r, sr.shape), pltpu.roll(sr, 1, 0))
        pi2 = jnp.where(row == 0, jnp.broadcast_to(b_i, si.shape), pltpu.roll(si, 1, 0))
        dlam_ref[0:1, :] += jnp.sum(gr * pr_ + gi * pi2, axis=0, keepdims=True)
        dlam_ref[1:2, :] += jnp.sum(gi * pr_ - gr * pi2, axis=0, keepdims=True)

    acc_ts = pl.BlockSpec((None, S5_LANES, LANE), lambda b, t: (b, 0, 0))
    acc_fs = pl.BlockSpec((None, LANE, S5_LANES), lambda b, t: (b, 0, 0))
    return pl.pallas_call(
        body, name="s5_bwd", grid=(S5_NBLK, nt),
        in_specs=[sp["u"], sp["rows"], sp["bound"], sp["to_state"], sp["to_state"], sp["from_state"],
                  sp["from_state"], sp["to_state"], sp["to_state"], sp["lam"], sp["d"]],
        out_specs=[sp["rows"], acc_ts, acc_ts, acc_fs, acc_fs, sp["lam"], sp["d"]],
        out_shape=[jax.ShapeDtypeStruct((T, SSM_WIDTH), BF16),
                   jax.ShapeDtypeStruct((S5_NBLK, S5_LANES, LANE), F32),
                   jax.ShapeDtypeStruct((S5_NBLK, S5_LANES, LANE), F32),
                   jax.ShapeDtypeStruct((S5_NBLK, LANE, S5_LANES), F32),
                   jax.ShapeDtypeStruct((S5_NBLK, LANE, S5_LANES), F32),
                   jax.ShapeDtypeStruct((S5_NBLK, 2, S5_LANES), F32),
                   jax.ShapeDtypeStruct((1, SSM_WIDTH), F32)],
        scratch_shapes=[pltpu.VMEM((2, S5_LANES), F32), pltpu.VMEM((2, S5_TILE, S5_LANES), F32),
                        pltpu.VMEM((2, S5_TILE, S5_LANES), F32)],
        compiler_params=_params(2),
    )(proj, dy, bound, bre, bim, bre_t, bim_t, cre, cim, lam, d)


def _ret_bwd(proj, cosf, sinf, mask, rowdec, kdec, gtb, gn, sblk, dyr):
    T = proj.shape[0]
    nb = T // RET_BLOCK
    sp = _ret_specs(T, True)

    def body(q_ref, k_ref, v_ref, g_ref, cos_ref, sin_ref, mask_ref, rd_ref, kd_ref, gtb_ref, gn_ref, sb_ref, dy_ref,
             dq_ref, dk_ref, dv_ref, dg_ref, dgn_ref, dst):
        @pl.when(pl.program_id(1) == 0)
        def _():
            dst[...] = jnp.zeros_like(dst)
            dgn_ref[...] = jnp.zeros_like(dgn_ref)

        s_in = sb_ref[...]
        q, k, qb, kb, vb, pm, qd, o = _ret_common(q_ref, k_ref, v_ref, cos_ref, sin_ref, mask_ref, rd_ref, s_in)
        mu = jnp.mean(o, axis=-1, keepdims=True)
        oc = o - mu
        rstd = lax.rsqrt(jnp.mean(oc * oc, axis=-1, keepdims=True) + EPS)
        n = oc * rstd
        gt = g_ref[...]
        sg = _sigmoid(gt)
        sil = gt * sg
        gnv = gn_ref[...]
        dyv = dy_ref[...]
        dg_ref[...] = (dyv * (n * gnv) * (sg * (1.0 + gt * (1.0 - sg)))).astype(BF16)
        dgn_ref[...] += jnp.sum(dyv * sil * n, axis=0, keepdims=True)
        dn = dyv * sil * gnv
        do = rstd * (dn - jnp.mean(dn, axis=-1, keepdims=True) - n * jnp.mean(dn * n, axis=-1, keepdims=True))
        dob = do.astype(BF16)
        ds = dst[...]
        dsb = ds.astype(BF16)
        kd = kd_ref[...]
        rd = rd_ref[...]
        dv_ref[...] = (_dot_tn(pm, dob) + _dot((k * kd).astype(BF16), dsb)).astype(BF16)
        dpb = (_dot_nt(dob, vb) * mask_ref[...]).astype(BF16)
        dq = _dot(dpb, kb) + _dot_nt(dob, s_in.astype(BF16)) * rd
        dk = (_dot_tn(dpb, qb) + _dot_nt(vb, dsb) * kd) * (HEAD_DIM ** -0.5)
        dst[...] = gtb_ref[...] * ds + _dot_tn(qd, dob)
        c = cos_ref[...]
        s = sin_ref[...]
        dq_ref[...] = (dq * c + pltpu.roll(dq * s, HEAD_DIM // 2, 1)).astype(BF16)
        dk_ref[...] = (dk * c + pltpu.roll(dk * s, HEAD_DIM // 2, 1)).astype(BF16)

    oshape = jax.ShapeDtypeStruct((T, RET_WIDTH), BF16)
    return pl.pallas_call(
        body, name="ret_bwd", grid=(RET_HEADS, nb),
        in_specs=[sp["q"], sp["k"], sp["v"], sp["g"], sp["tab"], sp["tab"], sp["mask"], sp["dec"], sp["dec"],
                  sp["gtb"], sp["gn"], sp["state"], sp["rows"]],
        out_specs=[sp["rows"], sp["rows"], sp["rows"], sp["rows"], sp["gn"]],
        out_shape=[oshape, oshape, oshape, oshape, jax.ShapeDtypeStruct((1, RET_WIDTH), F32)],
        scratch_shapes=[pltpu.VMEM((HEAD_DIM, HEAD_DIM), F32)],
        compiler_params=_params(2),
    )(proj, proj, proj, proj, cosf, sinf, mask, rowdec, kdec, gtb, gn, sblk, dyr)


def _in_proj_bwd(dproj, w, x, r1, g, dx2, tm, carry=None):
    T = x.shape[0]

    def body(dp_ref, w_ref, x_ref, r_ref, g_ref, dx2_ref, gx_ref, dg_ref, acc):
        i = pl.program_id(0)
        k = pl.program_id(1)

        @pl.when(k == 0)
        def _():
            acc[...] = jnp.zeros_like(acc)

        acc[...] += _dot_nt(dp_ref[...], w_ref[...])

        @pl.when(k == N_DEV - 1)
        def _():
            dxn, dgt = _rms_bwd(acc[...], x_ref[...], r_ref[...], g_ref[...])
            gx_ref[...] = dx2_ref[...] + dxn

            @pl.when(i == 0)
            def _():
                dg_ref[...] = jnp.zeros_like(dg_ref)

            dg_ref[...] += jnp.sum(dgt, axis=0, keepdims=True)

    full = pl.BlockSpec((tm, D_MODEL), lambda i, k: (i, 0))
    vec = pl.BlockSpec((1, D_MODEL), lambda i, k: (0, 0))
    return _pcall(
        body, "in_proj_bwd", (T // tm, N_DEV),
        [pl.BlockSpec((tm, WIN_BLK), lambda i, k: (i, k)),
         pl.BlockSpec((None, D_MODEL, WIN_BLK), lambda i, k: (k, 0, 0)),
         full, pl.BlockSpec((tm, 1), lambda i, k: (i, 0)), vec, full],
        [full, vec],
        [jax.ShapeDtypeStruct((T, D_MODEL), F32), jax.ShapeDtypeStruct((1, D_MODEL), F32)],
        [pltpu.VMEM((tm, D_MODEL), F32)], (dproj, w, x, r1, g, dx2), carry)


def _in_proj_wgrad(h, dproj, tk, carry=None):
    T = h.shape[0]
    nk = T // tk

    def body(h_ref, dp_ref, o_ref, acc):
        k = pl.program_id(1)

        @pl.when(k == 0)
        def _():
            acc[...] = jnp.zeros_like(acc)

        acc[...] += _dot_tn(h_ref[...], dp_ref[...])

        @pl.when(k == nk - 1)
        def _():
            o_ref[...] = acc[...].astype(BF16)

    return _pcall(
        body, "in_proj_wgrad", (N_DEV, nk),
        [pl.BlockSpec((tk, D_MODEL), lambda j, k: (k, 0)), pl.BlockSpec((tk, WIN_BLK), lambda j, k: (k, j))],
        [pl.BlockSpec((None, D_MODEL, WIN_BLK), lambda j, k: (j, 0, 0))],
        [jax.ShapeDtypeStruct((N_DEV, D_MODEL, WIN_BLK), BF16)],
        [pltpu.VMEM((D_MODEL, WIN_BLK), F32)], (h, dproj), carry)


def _rope_tables(T):
    half = HEAD_DIM // 2
    freqs = ROPE_BASE ** (-jnp.arange(half, dtype=F32) / half)
    ang = jnp.arange(T, dtype=F32)[:, None] * freqs[None, :]
    c = jnp.cos(ang)
    s = jnp.sin(ang)
    return jnp.concatenate([c, c], axis=1), jnp.concatenate([-s, s], axis=1)


def _retention_tables():
    hh = jnp.arange(RET_HEADS, dtype=F32)
    log_g = jnp.log1p(-(2.0 ** (-5.0 - hh)))[:, None, None]
    i = jnp.arange(RET_BLOCK)
    ci = (i // CHUNK)[:, None]
    cj = (i // CHUNK)[None, :]
    diff = (i[:, None] - i[None, :]).astype(F32)
    expo = jnp.where(ci == cj, jnp.abs(diff), diff)
    mask = jnp.where((cj <= ci)[None], jnp.exp(log_g * expo[None]), 0.0)
    r = jnp.arange(RET_BLOCK, dtype=F32)[None, :, None]
    ones = jnp.ones((1, 1, HEAD_DIM), F32)
    rowdec = jnp.exp(log_g * (r + 1.0)) * ones
    kdec = jnp.exp(log_g * (RET_BLOCK - 1.0 - r)) * ones
    gtb = jnp.exp(log_g * float(RET_BLOCK)) * ones
    return mask, rowdec, kdec, gtb


def _s5_discretise(a_re, a_im, log_dt, b_re, b_im):
    lam = lax.complex(a_re, a_im)
    dt = jnp.exp(log_dt)[:, None]
    lam_bar = jnp.exp(lam * dt)
    b_bar = ((lam_bar - 1.0) / lam)[..., None] * lax.complex(b_re, b_im)
    return jnp.real(lam_bar), jnp.imag(lam_bar), jnp.real(b_bar), jnp.imag(b_bar)


def _to_state_blockdiag(m):
    eye = jnp.eye(S5_GB, dtype=m.dtype)
    t = jnp.einsum("bgpc,gh->bgchp", m.reshape(S5_NBLK, S5_GB, SSM_STATE, SSM_GROUP), eye)
    return t.reshape(S5_NBLK, LANE, S5_LANES)


def _from_state_blockdiag(m):
    eye = jnp.eye(S5_GB, dtype=m.dtype)
    t = jnp.einsum("bgcp,gh->bgphc", m.reshape(S5_NBLK, S5_GB, SSM_GROUP, SSM_STATE), eye)
    return t.reshape(S5_NBLK, S5_LANES, LANE)


def _diag_of_state_major(acc):
    eye = jnp.eye(S5_GB, dtype=acc.dtype)
    t = acc.reshape(S5_NBLK, S5_GB, SSM_STATE, S5_GB, SSM_GROUP)
    return jnp.einsum("bgphc,gh->bgpc", t, eye).reshape(SSM_GROUPS, SSM_STATE, SSM_GROUP)


def _diag_of_channel_major(acc):
    eye = jnp.eye(S5_GB, dtype=acc.dtype)
    t = acc.reshape(S5_NBLK, S5_GB, SSM_GROUP, S5_GB, SSM_STATE)
    return jnp.einsum("bgchp,gh->bgcp", t, eye).reshape(SSM_GROUPS, SSM_GROUP, SSM_STATE)


SMALL_PARTIALS = (("ret_gn_g", 1024), ("lam_re", 4096), ("lam_im", 4096),
                  ("bbar_re", 65536), ("bbar_im", 65536), ("c_re", 65536), ("c_im", 65536),
                  ("ssm_d", 1024), ("b_glu", 1024), ("out_g", 1024), ("norm_ffn_g", 2048), ("norm_final_g", 2048))


def _forward_backward(x, tgt, shards, sm, tm=512):
    T = x.shape[0]
    cosf, sinf = _rope_tables(T)
    mask, rowdec, kdec, gtb = _retention_tables()
    lbr, lbi, bbr, bbi = _s5_discretise(sm["ssm_a_re"], sm["ssm_a_im"], sm["ssm_log_dt"], sm["ssm_b_re"],
                                        sm["ssm_b_im"])
    bre = _to_state_blockdiag(bbr).astype(BF16)
    bim = _to_state_blockdiag(bbi).astype(BF16)
    cre_t = _from_state_blockdiag(sm["ssm_c_re"]).astype(BF16)
    cim_t = _from_state_blockdiag(sm["ssm_c_im"]).astype(BF16)
    bre_t = jnp.swapaxes(bre, 1, 2)
    bim_t = jnp.swapaxes(bim, 1, 2)
    cre = jnp.swapaxes(cre_t, 1, 2)
    cim = jnp.swapaxes(cim_t, 1, 2)
    lam = jnp.stack([lbr.reshape(S5_NBLK, S5_LANES), lbi.reshape(S5_NBLK, S5_LANES)], axis=1)
    row = lambda v: v.reshape(1, -1)
    g_mix, g_ffn, g_fin = row(sm["norm_mix_g"]), row(sm["norm_ffn_g"]), row(sm["norm_final_g"])
    gn, dsk, bglu, og = row(sm["ret_gn_g"]), row(sm["ssm_d"]), row(sm["ssm_b_glu"]), row(sm["ssm_out_g"])

    (w_in,) = _exchange_call("weight_gather", [shards["w_in"]], True)
    proj, h1, r1, w_glu, w_out, w_gate = _in_proj_fwd(
        x, g_mix, w_in, tm, _Exchange([shards["ssm_w_glu"], shards["w_out"], shards["w_gate"]], True))
    w_glu = w_glu.reshape(SSM_WIDTH, SSM_WIDTH)
    w_out = w_out.reshape(D_MODEL, D_MODEL)
    y_ret, sblk, w_up = _ret_fwd(proj, cosf, sinf, mask, rowdec, kdec, gtb, gn, _Exchange([shards["w_up"]], True))
    y_s5, bound = _s5_fwd(proj, bre, bim, cre_t, cim_t, lam, dsk)
    z, y_ssm, r_ssm = _glu_fwd(y_s5, w_glu, bglu, og, 256)
    x2, h2, r2 = _out_proj_fwd(x, y_ret, y_ssm, w_out, g_ffn, 256)
    a, b, f, w_down = _ffn_up(h2, w_gate, w_up, tm, _Exchange([shards["w_down"]], True))
    dx3, dx3b, loss8, dg_fin = _ffn_down_loss(f, w_down, x2, tgt, g_fin, 256)

    landed = {}
    da, db = _ffn_bwd_act(dx3b, w_down, a, b, tm)
    dw_down = _ffn_wgrad_down(f, dx3b, tm)
    dw_gate, dw_up, landed["w_down"] = _ffn_wgrad_up(h2, da, db, tm, _Exchange([dw_down], False))
    dx2, dx2b, dg_ffn, landed["w_gate"], landed["w_up"] = _ffn_bwd_in(
        da, db, w_gate, w_up, x2, r2, g_ffn, dx3, 256, _Exchange([dw_gate, dw_up], False))
    dy_ret, dy_ssm = _out_proj_bwd(dx2b, w_out, 256)
    dw_out = jnp.concatenate([_wgrad_rows("out_proj_wgrad_ret", y_ret, dx2b, tm),
                              _wgrad_rows("out_proj_wgrad_ssm", y_ssm, dx2b, tm)], axis=0)
    dy_s5, dw_glu, db_glu, dog = _glu_bwd(y_s5, z, r_ssm, dy_ssm, w_glu, og, 256)
    du, dbre, dbim, dcre, dcim, dlam, dd = _s5_bwd(proj, dy_s5, bound, bre, bim, bre_t, bim_t, cre, cim, lam, dsk)
    dq, dk, dv, dgate, dgn = _ret_bwd(proj, cosf, sinf, mask, rowdec, kdec, gtb, gn, sblk, dy_ret)
    dproj = jnp.concatenate([dq, dk, dv, dgate, du], axis=1)
    dw_in, landed["w_out"], landed["ssm_w_glu"] = _in_proj_wgrad(
        h1, dproj, tm, _Exchange([dw_out.reshape(N_DEV, D_MODEL // N_DEV, D_MODEL),
                                  dw_glu.astype(BF16).reshape(N_DEV, SSM_WIDTH // N_DEV, SSM_WIDTH)], False))
    small = dict(ret_gn_g=dgn, lam_re=dlam[:, 0], lam_im=dlam[:, 1],
                 bbar_re=_diag_of_state_major(dbre), bbar_im=_diag_of_state_major(dbim),
                 c_re=_diag_of_channel_major(dcre), c_im=_diag_of_channel_major(dcim),
                 ssm_d=dd, b_glu=db_glu, out_g=dog, norm_ffn_g=dg_ffn, norm_final_g=dg_fin)
    packed = _pack([small[n] for n, _ in SMALL_PARTIALS])
    grad_x, dg_mix, landed["w_in"], small_landed = _in_proj_bwd(
        dproj, w_in, x, r1, g_mix, dx2, 256, _Exchange([dw_in, packed], [False, True]))
    (mix_landed,) = _exchange_call("mix_gain_grad_gather", [_pack([dg_mix])], True)
    summed = dict(zip([n for n, _ in SMALL_PARTIALS],
                      _unpack(_sum_partials("small_grad_sum", small_landed), [(sz,) for _, sz in SMALL_PARTIALS])))
    summed["norm_mix_g"] = _sum_partials("mix_gain_grad_sum", mix_landed).reshape(-1)
    return loss8[0, 0], grad_x, landed, summed


def _small_grads(summed, sm):
    _, vjp = jax.vjp(_s5_discretise, sm["ssm_a_re"], sm["ssm_a_im"], sm["ssm_log_dt"], sm["ssm_b_re"], sm["ssm_b_im"])
    gp = (SSM_GROUPS, SSM_STATE)
    da_re, da_im, dlog_dt, db_re, db_im = vjp((summed["lam_re"].reshape(gp), summed["lam_im"].reshape(gp),
                                               summed["bbar_re"].reshape(gp + (SSM_GROUP,)),
                                               summed["bbar_im"].reshape(gp + (SSM_GROUP,))))
    return dict(norm_mix_g=summed["norm_mix_g"], ret_gn_g=summed["ret_gn_g"], ssm_a_re=da_re, ssm_a_im=da_im,
                ssm_log_dt=dlog_dt, ssm_b_re=db_re, ssm_b_im=db_im,
                ssm_c_re=summed["c_re"].reshape(SSM_GROUPS, SSM_GROUP, SSM_STATE),
                ssm_c_im=summed["c_im"].reshape(SSM_GROUPS, SSM_GROUP, SSM_STATE),
                ssm_d=summed["ssm_d"], ssm_b_glu=summed["b_glu"], ssm_out_g=summed["out_g"],
                norm_ffn_g=summed["norm_ffn_g"], norm_final_g=summed["norm_final_g"])


def _adamw_math(w, g, m, v):
    m2 = ADAM_B1 * m + (1.0 - ADAM_B1) * g
    v2 = ADAM_B2 * v + (1.0 - ADAM_B2) * (g * g)
    delta = -ADAM_LR * ((m2 / ADAM_BC1) / (jnp.sqrt(v2 / ADAM_BC2) + ADAM_EPS) + ADAM_WD * w)
    return delta, m2, v2


def _adamw_shard(name, parts, w, m, v, tr):
    rows, cols = w.shape

    def body(p_ref, w_ref, m_ref, v_ref, g_ref, d_ref, m2_ref, v2_ref):
        g = p_ref[0].astype(F32)
        for s in range(1, N_DEV):
            g = g + p_ref[s].astype(F32)
        d, m2, v2 = _adamw_math(w_ref[...], g, m_ref[...], v_ref[...])
        g_ref[...] = g
        d_ref[...] = d
        m2_ref[...] = m2
        v2_ref[...] = v2

    blk = pl.BlockSpec((tr, cols), lambda i: (i, 0))
    oshape = jax.ShapeDtypeStruct((rows, cols), F32)
    return pl.pallas_call(
        body, name=name, grid=(rows // tr,),
        in_specs=[pl.BlockSpec((N_DEV, tr, cols), lambda i: (0, i, 0)), blk, blk, blk],
        out_specs=[blk, blk, blk, blk], out_shape=[oshape] * 4,
        compiler_params=_params(1),
    )(parts, w, m, v)


def _sum_partials(name, parts):
    rows = parts.shape[1]

    def body(p_ref, o_ref):
        g = p_ref[0]
        for s in range(1, N_DEV):
            g = g + p_ref[s]
        o_ref[...] = g

    return pl.pallas_call(
        body, name=name, grid=(1,),
        in_specs=[pl.BlockSpec((N_DEV, rows, LANE), lambda i: (0, 0, 0))],
        out_specs=pl.BlockSpec((rows, LANE), lambda i: (0, 0)),
        out_shape=jax.ShapeDtypeStruct((rows, LANE), F32),
        compiler_params=_params(1),
    )(parts)


def _adamw_small(w, g, m, v):
    rows = w.shape[0]

    def body(w_ref, g_ref, m_ref, v_ref, d_ref, m2_ref, v2_ref):
        d, m2, v2 = _adamw_math(w_ref[...], g_ref[...], m_ref[...], v_ref[...])
        d_ref[...] = d
        m2_ref[...] = m2
        v2_ref[...] = v2

    blk = pl.BlockSpec((rows, LANE), lambda i: (0, 0))
    oshape = jax.ShapeDtypeStruct((rows, LANE), F32)
    return pl.pallas_call(
        body, name="adamw_small", grid=(1,), in_specs=[blk] * 4, out_specs=[blk] * 3, out_shape=[oshape] * 3,
        compiler_params=_params(1),
    )(w, g, m, v)


def _pack(arrays):
    cols = []
    for a in arrays:
        flat = a.reshape(-1).astype(F32)
        pad = (-flat.shape[0]) % LANE
        cols.append(jnp.pad(flat, (0, pad)) if pad else flat)
    return jnp.concatenate(cols).reshape(-1, LANE)


def _unpack(packed, shapes):
    flat = packed.reshape(-1)
    out, off = [], 0
    for shp in shapes:
        n = math.prod(shp)
        out.append(flat[off:off + n].reshape(shp))
        off += n + ((-n) % LANE)
    return out


WEIGHTS = ("norm_mix_g", "w_in", "ret_gn_g", "ssm_a_re", "ssm_a_im", "ssm_log_dt", "ssm_b_re", "ssm_b_im",
           "ssm_c_re", "ssm_c_im", "ssm_d", "ssm_w_glu", "ssm_b_glu", "ssm_out_g", "w_out", "norm_ffn_g", "w_gate",
           "w_up", "w_down", "norm_final_g")
BIG = ("w_in", "ssm_w_glu", "w_out", "w_gate", "w_up", "w_down")
SMALL = tuple(n for n in WEIGHTS if n not in BIG)
ADAM_ROWS = {"w_in": 256, "ssm_w_glu": 128, "w_out": 128, "w_gate": 256, "w_up": 256, "w_down": 176}


def kernel(x, norm_mix_g, w_in, ret_gn_g, ssm_a_re, ssm_a_im, ssm_log_dt, ssm_b_re, ssm_b_im, ssm_c_re, ssm_c_im, ssm_d, ssm_w_glu, ssm_b_glu, ssm_out_g, w_out, norm_ffn_g, w_gate, w_up, w_down, norm_final_g, loss_target, m_norm_mix_g, m_w_in, m_ret_gn_g, m_ssm_a_re, m_ssm_a_im, m_ssm_log_dt, m_ssm_b_re, m_ssm_b_im, m_ssm_c_re, m_ssm_c_im, m_ssm_d, m_ssm_w_glu, m_ssm_b_glu, m_ssm_out_g, m_w_out, m_norm_ffn_g, m_w_gate, m_w_up, m_w_down, m_norm_final_g, v_norm_mix_g, v_w_in, v_ret_gn_g, v_ssm_a_re, v_ssm_a_im, v_ssm_log_dt, v_ssm_b_re, v_ssm_b_im, v_ssm_c_re, v_ssm_c_im, v_ssm_d, v_ssm_w_glu, v_ssm_b_glu, v_ssm_out_g, v_w_out, v_norm_ffn_g, v_w_gate, v_w_up, v_w_down, v_norm_final_g):
    given = dict(locals())
    w = {n: given[n] for n in WEIGHTS}
    m = {n: given["m_" + n] for n in WEIGHTS}
    v = {n: given["v_" + n] for n in WEIGHTS}
    drop = lambda n, a: a if n == "norm_final_g" else a[0]
    w0 = {n: drop(n, w[n]) for n in WEIGHTS}
    m0 = {n: drop(n, m[n]) for n in WEIGHTS}
    v0 = {n: drop(n, v[n]) for n in WEIGHTS}

    sm = {n: w0[n] for n in SMALL}
    shards = {n: w0[n].astype(BF16) for n in BIG}
    loss_local, grad_x, landed, summed = _forward_backward(x[0], loss_target[0], shards, sm)
    loss = lax.psum(loss_local, MESH_AXES)
    gsmall = _small_grads(summed, sm)

    grads, delta, new_m, new_v = {}, {}, {}, {}
    for n in BIG:
        g, d, m2, v2 = _adamw_shard("adamw_" + n, landed[n], w0[n], m0[n], v0[n], ADAM_ROWS[n])
        grads[n], delta[n], new_m[n], new_v[n] = g, d, m2, v2
    shapes = [w0[n].shape for n in SMALL]
    gs = [gsmall[n].reshape(w0[n].shape) for n in SMALL]
    d_p, m_p, v_p = _adamw_small(_pack([w0[n] for n in SMALL]), _pack(gs), _pack([m0[n] for n in SMALL]),
                                 _pack([v0[n] for n in SMALL]))
    for n, g, d, m2, v2 in zip(SMALL, gs, _unpack(d_p, shapes), _unpack(m_p, shapes), _unpack(v_p, shapes)):
        grads[n], delta[n], new_m[n], new_v[n] = g, d, m2, v2

    lift = lambda n, a: a.reshape(w[n].shape)
    return (loss, grad_x[None], *[lift(n, grads[n]) for n in WEIGHTS], *[lift(n, delta[n]) for n in WEIGHTS],
            *[lift(n, new_m[n]) for n in WEIGHTS], *[lift(n, new_v[n]) for n in WEIGHTS])
```

```python
import functools
import math

import jax
import jax.numpy as jnp
from jax import lax
from jax.experimental import pallas as pl
from jax.experimental.pallas import tpu as pltpu

F32 = jnp.float32
BF16 = jnp.bfloat16

D_MODEL = 2048
RET_WIDTH = 1024
RET_HEADS = 8
HEAD_DIM = 128
CHUNK = 64
SSM_WIDTH = 1024
SSM_GROUP = 16
SSM_GROUPS = 64
SSM_STATE = 64
D_FF = 5632
IN_WIDTH = 5120
ROPE_BASE = 10000.0
EPS = 1e-6
N_DEV = 8
MESH_AXES = ("x", "y", "c")

WIN_BLK = IN_WIDTH // N_DEV
FF_BLK = D_FF // N_DEV
RET_BLOCK = 256
S5_TILE = 256
S5_CHUNKS = 8
S5_STEPS = S5_TILE // S5_CHUNKS
S5_GB = 8
S5_NBLK = SSM_GROUPS // S5_GB
S5_LANES = S5_GB * SSM_STATE
LANE = 128

ADAM_LR = 0.001
ADAM_B1 = 0.9
ADAM_B2 = 0.999
ADAM_EPS = 1e-08
ADAM_WD = 0.01
ADAM_STEP = 10
ADAM_BC1 = 1.0 - ADAM_B1 ** ADAM_STEP
ADAM_BC2 = 1.0 - ADAM_B2 ** ADAM_STEP

VMEM_LIMIT = 56 * 1024 * 1024

NT = (((1,), (1,)), ((), ()))
TN = (((0,), (0,)), ((), ()))


def _params(n_grid):
    return pltpu.CompilerParams(dimension_semantics=("arbitrary",) * n_grid, vmem_limit_bytes=VMEM_LIMIT)


def _dot(a, b):
    return jnp.dot(a, b, preferred_element_type=F32)


def _dot_nt(a, b):
    return lax.dot_general(a, b, NT, preferred_element_type=F32)


def _dot_tn(a, b):
    return lax.dot_general(a, b, TN, preferred_element_type=F32)


def _sigmoid(x):
    return 1.0 / (1.0 + jnp.exp(-x))


_GELU_C = math.sqrt(2.0 / math.pi)
_GELU_A = 0.044715


def _gelu(x):
    t = jnp.tanh(_GELU_C * (x + _GELU_A * x * x * x))
    return 0.5 * x * (1.0 + t)


def _gelu_and_grad(x):
    t = jnp.tanh(_GELU_C * (x + _GELU_A * x * x * x))
    g = 0.5 * (1.0 + t) + 0.5 * x * (1.0 - t * t) * _GELU_C * (1.0 + 3.0 * _GELU_A * x * x)
    return 0.5 * x * (1.0 + t), g


def _rms_bwd(dy, x, r, g):
    w = dy * g
    dx = r * w - x * (r * r * r) * jnp.mean(w * x, axis=-1, keepdims=True)
    return dx, dy * x * r


HBM_SPEC = pl.BlockSpec(memory_space=pltpu.HBM)
ANY_SPEC = pl.BlockSpec(memory_space=pl.ANY)


def _load_resident(src_hbm, dst_vmem, sem):
    cp = pltpu.make_async_copy(src_hbm, dst_vmem, sem)
    cp.start()
    cp.wait()


def _my_block():
    return 4 * lax.axis_index("x") + 2 * lax.axis_index("y") + lax.axis_index("c")


def _peer(k):
    px = lax.axis_index("x") ^ ((k >> 2) & 1)
    py = lax.axis_index("y") ^ ((k >> 1) & 1)
    pc = lax.axis_index("c") ^ (k & 1)
    return (px, py, pc), 4 * px + 2 * py + pc


class _Exchange:
    def __init__(self, payloads, gather):
        self.payloads = list(payloads)
        self.n = len(self.payloads)
        self.gather = [gather] * self.n if isinstance(gather, bool) else list(gather)

    def out_shape(self):
        return [jax.ShapeDtypeStruct(((N_DEV,) if g else ()) + p.shape, p.dtype)
                for p, g in zip(self.payloads, self.gather)]

    def scratch_shapes(self):
        return [pltpu.SemaphoreType.DMA((self.n, N_DEV - 1)), pltpu.SemaphoreType.DMA((self.n, N_DEV - 1)),
                pltpu.SemaphoreType.DMA((self.n,))]

    def _copies(self, ins, outs, sems, incoming):
        send_sems, recv_sems, local_sems = sems
        me = _my_block()
        src_of = lambda i, blk: ins[i] if self.gather[i] else ins[i].at[blk]
        local, remote = [], []
        for i in range(self.n):
            if not incoming:
                local.append(pltpu.make_async_copy(src_of(i, me), outs[i].at[me], local_sems.at[i]))
            for k in range(1, N_DEV):
                dev, blk = _peer(k)
                src, dst = (outs[i].at[blk], outs[i].at[blk]) if incoming else (src_of(i, blk), outs[i].at[me])
                remote.append(pltpu.make_async_remote_copy(
                    src_ref=src, dst_ref=dst, send_sem=send_sems.at[i, k - 1], recv_sem=recv_sems.at[i, k - 1],
                    device_id=dev, device_id_type=pl.DeviceIdType.MESH))
        return local, remote

    def start(self, ins, outs, sems):
        local, sends = self._copies(ins, outs, sems, False)
        for cp in local + sends:
            cp.start()

    def wait(self, ins, outs, sems):
        for cp in self._copies(ins, outs, sems, True)[1]:
            cp.wait_recv()
        local, sends = self._copies(ins, outs, sems, False)
        for cp in sends:
            cp.wait_send()
        for cp in local:
            cp.wait()


def _pcall(body, name, grid, in_specs, out_specs, out_shape, scratch_shapes, args, carry=None):
    n_in, n_out, n_scr = len(in_specs), len(out_specs), len(scratch_shapes)
    if carry is None:
        return pl.pallas_call(body, name=name, grid=grid, in_specs=in_specs, out_specs=out_specs, out_shape=out_shape,
                              scratch_shapes=scratch_shapes, compiler_params=_params(len(grid)))(*args)
    nx = carry.n

    def wrapped(*refs):
        cin, xin = refs[:n_in], refs[n_in:n_in + nx]
        cout, xout = refs[n_in + nx:n_in + nx + n_out], refs[n_in + nx + n_out:n_in + 2 * nx + n_out]
        rest = refs[n_in + 2 * nx + n_out:]
        cscr, sems = rest[:n_scr], rest[n_scr:]
        first = functools.reduce(jnp.logical_and, [pl.program_id(a) == 0 for a in range(len(grid))])
        last = functools.reduce(jnp.logical_and, [pl.program_id(a) == grid[a] - 1 for a in range(len(grid))])

        @pl.when(first)
        def _():
            carry.start(xin, xout, sems)

        body(*cin, *cout, *cscr)

        @pl.when(last)
        def _():
            carry.wait(xin, xout, sems)

    return pl.pallas_call(
        wrapped, name=name, grid=grid, in_specs=list(in_specs) + [HBM_SPEC] * nx,
        out_specs=list(out_specs) + [HBM_SPEC] * nx, out_shape=list(out_shape) + carry.out_shape(),
        scratch_shapes=list(scratch_shapes) + carry.scratch_shapes(), compiler_params=_params(len(grid)),
    )(*args, *carry.payloads)


def _exchange_call(name, payloads, gather):
    ex = _Exchange(payloads, gather)

    def body(*refs):
        ins, outs, sems = refs[:ex.n], refs[ex.n:2 * ex.n], refs[2 * ex.n:]
        ex.start(ins, outs, sems)
        ex.wait(ins, outs, sems)

    return pl.pallas_call(body, name=name, in_specs=[HBM_SPEC] * ex.n, out_specs=[HBM_SPEC] * ex.n,
                          out_shape=ex.out_shape(), scratch_shapes=ex.scratch_shapes())(*ex.payloads)


def _in_proj_fwd(x, g, w, tm, carry=None):
    T = x.shape[0]

    def body(x_ref, g_ref, w_hbm, proj_ref, h_ref, r_ref, w_ref, sem):
        @pl.when(pl.program_id(0) == 0)
        def _():
            _load_resident(w_hbm, w_ref, sem)

        xf = x_ref[...]
        r = lax.rsqrt(jnp.mean(xf * xf, axis=-1, keepdims=True) + EPS)
        h = (xf * r * g_ref[...]).astype(BF16)
        h_ref[...] = h
        r_ref[...] = r
        for j in range(N_DEV):
            proj_ref[:, j * WIN_BLK:(j + 1) * WIN_BLK] = _dot(h, w_ref[j])

    return _pcall(
        body, "in_proj_fwd", (T // tm,),
        [pl.BlockSpec((tm, D_MODEL), lambda i: (i, 0)), pl.BlockSpec((1, D_MODEL), lambda i: (0, 0)), ANY_SPEC],
        [pl.BlockSpec((tm, IN_WIDTH), lambda i: (i, 0)),
         pl.BlockSpec((tm, D_MODEL), lambda i: (i, 0)),
         pl.BlockSpec((tm, 1), lambda i: (i, 0))],
        [jax.ShapeDtypeStruct((T, IN_WIDTH), F32),
         jax.ShapeDtypeStruct((T, D_MODEL), BF16),
         jax.ShapeDtypeStruct((T, 1), F32)],
        [pltpu.VMEM(w.shape, w.dtype), pltpu.SemaphoreType.DMA], (x, g, w), carry)


def _ret_common(q_ref, k_ref, v_ref, cos_ref, sin_ref, mask_ref, rd_ref, sin_state):
    c = cos_ref[...]
    s = sin_ref[...]
    q = q_ref[...]
    q = q * c + pltpu.roll(q, HEAD_DIM // 2, 1) * s
    k = k_ref[...]
    k = (k * c + pltpu.roll(k, HEAD_DIM // 2, 1) * s) * (HEAD_DIM ** -0.5)
    qb = q.astype(BF16)
    kb = k.astype(BF16)
    vb = v_ref[...].astype(BF16)
    pm = (_dot_nt(qb, kb) * mask_ref[...]).astype(BF16)
    qd = (q * rd_ref[...]).astype(BF16)
    o = _dot(pm, vb) + _dot(qd, sin_state.astype(BF16))
    return q, k, qb, kb, vb, pm, qd, o


def _ret_specs(T, rev):
    nb = T // RET_BLOCK
    blk = (lambda b: nb - 1 - b) if rev else (lambda b: b)
    col = lambda off: pl.BlockSpec((RET_BLOCK, HEAD_DIM), lambda h, b: (blk(b), off + h))
    tab = pl.BlockSpec((RET_BLOCK, HEAD_DIM), lambda h, b: (blk(b), 0))
    per_head = pl.BlockSpec((None, RET_BLOCK, HEAD_DIM), lambda h, b: (h, 0, 0))
    return dict(
        q=col(0), k=col(RET_HEADS), v=col(2 * RET_HEADS), g=col(3 * RET_HEADS), tab=tab,
        mask=pl.BlockSpec((None, RET_BLOCK, RET_BLOCK), lambda h, b: (h, 0, 0)),
        dec=per_head,
        gtb=pl.BlockSpec((None, 1, HEAD_DIM), lambda h, b: (h, 0, 0)),
        gn=pl.BlockSpec((1, HEAD_DIM), lambda h, b: (0, h)),
        state=pl.BlockSpec((None, None, HEAD_DIM, HEAD_DIM), lambda h, b: (h, blk(b), 0, 0)),
        rows=pl.BlockSpec((RET_BLOCK, HEAD_DIM), lambda h, b: (blk(b), h)),
    )


def _ret_fwd(proj, cosf, sinf, mask, rowdec, kdec, gtb, gn, carry=None):
    T = proj.shape[0]
    nb = T // RET_BLOCK
    sp = _ret_specs(T, False)

    def body(q_ref, k_ref, v_ref, g_ref, cos_ref, sin_ref, mask_ref, rd_ref, kd_ref, gtb_ref, gn_ref,
             y_ref, sb_ref, st):
        @pl.when(pl.program_id(1) == 0)
        def _():
            st[...] = jnp.zeros_like(st)
        s_in = st[...]
        sb_ref[...] = s_in
        q, k, qb, kb, vb, pm, qd, o = _ret_common(q_ref, k_ref, v_ref, cos_ref, sin_ref, mask_ref, rd_ref, s_in)
        st[...] = gtb_ref[...] * s_in + _dot_tn((k * kd_ref[...]).astype(BF16), vb)
        mu = jnp.mean(o, axis=-1, keepdims=True)
        oc = o - mu
        n = oc * lax.rsqrt(jnp.mean(oc * oc, axis=-1, keepdims=True) + EPS)
        gt = g_ref[...]
        y_ref[...] = (gt * _sigmoid(gt) * (n * gn_ref[...])).astype(BF16)

    return _pcall(
        body, "ret_fwd", (RET_HEADS, nb),
        [sp["q"], sp["k"], sp["v"], sp["g"], sp["tab"], sp["tab"], sp["mask"], sp["dec"], sp["dec"],
         sp["gtb"], sp["gn"]],
        [sp["rows"], sp["state"]],
        [jax.ShapeDtypeStruct((T, RET_WIDTH), BF16),
         jax.ShapeDtypeStruct((RET_HEADS, nb, HEAD_DIM, HEAD_DIM), F32)],
        [pltpu.VMEM((HEAD_DIM, HEAD_DIM), F32)],
        (proj, proj, proj, proj, cosf, sinf, mask, rowdec, kdec, gtb, gn), carry)


def _scan(re, im, ar, ai, reverse):
    n = re.shape[0]
    row = lax.broadcasted_iota(jnp.int32, re.shape, 0)
    s = 1
    while s < n:
        if reverse:
            keep = row < n - s
            sr = jnp.where(keep, pltpu.roll(re, n - s, 0), 0.0)
            si = jnp.where(keep, pltpu.roll(im, n - s, 0), 0.0)
        else:
            keep = row >= s
            sr = jnp.where(keep, pltpu.roll(re, s, 0), 0.0)
            si = jnp.where(keep, pltpu.roll(im, s, 0), 0.0)
        re, im = re + ar * sr - ai * si, im + ar * si + ai * sr
        ar, ai = ar * ar - ai * ai, 2.0 * ar * ai
        s *= 2
    return re, im


S5_STATE_TILE = (S5_TILE, S5_LANES)


def _step_major_permutation():
    r = jnp.arange(S5_TILE)
    t_of_row = (r % S5_CHUNKS) * S5_STEPS + r // S5_CHUNKS
    return (t_of_row[:, None] == r[None, :]).astype(BF16)


def _permute_rows_f32(pm, x):
    hi = x.astype(BF16)
    rest = x - hi.astype(F32)
    mid = rest.astype(BF16)
    lo = (rest - mid.astype(F32)).astype(BF16)
    return _dot(pm, hi) + _dot(pm, mid) + _dot(pm, lo)


def _step_get(ref, j):
    return ref[j * S5_CHUNKS:(j + 1) * S5_CHUNKS, :]


def _step_set(ref, j, val):
    ref[j * S5_CHUNKS:(j + 1) * S5_CHUNKS, :] = val


def _tile_get(ref):
    return ref[...]


def _tile_set(ref, val):
    ref[...] = val


def _fill_power_table(ptab, lr, li):
    shape = (S5_CHUNKS, S5_LANES)
    lrb = jnp.broadcast_to(lr, shape)
    lib = jnp.broadcast_to(li, shape)
    pr, pi_ = lrb, lib
    for j in range(S5_STEPS):
        ptab[0, j * S5_CHUNKS:(j + 1) * S5_CHUNKS, :] = pr
        ptab[1, j * S5_CHUNKS:(j + 1) * S5_CHUNKS, :] = pi_
        pr, pi_ = lrb * pr - lib * pi_, lrb * pi_ + lib * pr


def _chunk_scans(xr, xi, lr, li, reverse):
    shape = (S5_CHUNKS, S5_LANES)
    lrb = jnp.broadcast_to(lr, shape)
    lib = jnp.broadcast_to(li, shape)
    sr = si = None
    for j in (range(S5_STEPS - 1, -1, -1) if reverse else range(S5_STEPS)):
        vr = _step_get(xr, j)
        vi = _step_get(xi, j)
        if sr is not None:
            vr, vi = vr + lrb * sr - lib * si, vi + lrb * si + lib * sr
            _step_set(xr, j, vr)
            _step_set(xi, j, vi)
        sr, si = vr, vi
    return sr, si


def _entering_states(zr, zi, cr, ci, ar, ai, reverse):
    shape = (S5_CHUNKS, S5_LANES)
    row = lax.broadcasted_iota(jnp.int32, shape, 0)
    if reverse:
        edge, shift = row == S5_CHUNKS - 1, S5_CHUNKS - 1
    else:
        edge, shift = row == 0, 1
    wr = jnp.where(edge, jnp.broadcast_to(cr, shape), pltpu.roll(zr, shift, 0))
    wi = jnp.where(edge, jnp.broadcast_to(ci, shape), pltpu.roll(zi, shift, 0))
    return _scan(wr, wi, ar, ai, reverse)


def _table_rows(ptab, j, conj):
    pr = ptab[0, j * S5_CHUNKS:(j + 1) * S5_CHUNKS, :]
    pi_ = ptab[1, j * S5_CHUNKS:(j + 1) * S5_CHUNKS, :]
    return pr, (-pi_ if conj else pi_)


def _s5_forward_states(xr, xi, lr, li, cr, ci, ptab):
    zr, zi = _chunk_scans(xr, xi, lr, li, False)
    ar, ai = _table_rows(ptab, S5_STEPS - 1, False)
    er, ei = _entering_states(zr, zi, cr, ci, ar, ai, False)
    for j in range(S5_STEPS):
        pr, pi_ = _table_rows(ptab, j, False)
        _step_set(xr, j, _step_get(xr, j) + pr * er - pi_ * ei)
        _step_set(xi, j, _step_get(xi, j) + pr * ei + pi_ * er)
    last = S5_CHUNKS - 1
    end_r = (ar * er - ai * ei + zr)[last:last + 1, :]
    end_i = (ar * ei + ai * er + zi)[last:last + 1, :]
    return er, ei, end_r, end_i


def _s5_specs(T, rev):
    nt = T // S5_TILE
    tt = (lambda t: nt - 1 - t) if rev else (lambda t: t)
    return dict(
        u=pl.BlockSpec((S5_TILE, LANE), lambda b, t: (tt(t), 4 * RET_HEADS + b)),
        rows=pl.BlockSpec((S5_TILE, LANE), lambda b, t: (tt(t), b)),
        to_state=pl.BlockSpec((None, LANE, S5_LANES), lambda b, t: (b, 0, 0)),
        from_state=pl.BlockSpec((None, S5_LANES, LANE), lambda b, t: (b, 0, 0)),
        lam=pl.BlockSpec((None, 2, S5_LANES), lambda b, t: (b, 0, 0)),
        d=pl.BlockSpec((1, LANE), lambda b, t: (0, b)),
        perm=pl.BlockSpec((S5_TILE, S5_TILE), lambda b, t: (0, 0)),
        bound=pl.BlockSpec((None, None, 2, S5_LANES), lambda b, t: (b, tt(t), 0, 0)),
    )


def _s5_fwd(proj, pm, pm_t, bre, bim, cre_t, cim_t, lam, d):
    T = proj.shape[0]
    nt = T // S5_TILE
    sp = _s5_specs(T, False)

    def body(u_ref, pm_ref, pmt_ref, bre_ref, bim_ref, cre_ref, cim_ref, lam_ref, d_ref, y_ref, bound_ref,
             carry, ptab, xr, xi):
        lr = lam_ref[0:1, :]
        li = lam_ref[1:2, :]

        @pl.when(pl.program_id(1) == 0)
        def _():
            carry[...] = jnp.zeros_like(carry)
            _fill_power_table(ptab, lr, li)

        u = _permute_rows_f32(pm_ref[...], u_ref[...])
        ub = u.astype(BF16)
        _tile_set(xr, _dot(ub, bre_ref[...]))
        _tile_set(xi, _dot(ub, bim_ref[...]))
        bound_ref[...] = carry[...]
        _, _, end_r, end_i = _s5_forward_states(xr, xi, lr, li, carry[0:1, :], carry[1:2, :], ptab)
        carry[0:1, :] = end_r
        carry[1:2, :] = end_i
        y = (_dot(_tile_get(xr).astype(BF16), cre_ref[...]) - _dot(_tile_get(xi).astype(BF16), cim_ref[...])
             + d_ref[...] * u)
        y_ref[...] = _permute_rows_f32(pmt_ref[...], y)

    state = pltpu.VMEM(S5_STATE_TILE, F32)
    return pl.pallas_call(
        body, name="s5_fwd", grid=(S5_NBLK, nt),
        in_specs=[sp["u"], sp["perm"], sp["perm"], sp["to_state"], sp["to_state"], sp["from_state"],
                  sp["from_state"], sp["lam"], sp["d"]],
        out_specs=[sp["rows"], sp["bound"]],
        out_shape=[jax.ShapeDtypeStruct((T, SSM_WIDTH), F32),
                   jax.ShapeDtypeStruct((S5_NBLK, nt, 2, S5_LANES), F32)],
        scratch_shapes=[pltpu.VMEM((2, S5_LANES), F32), pltpu.VMEM((2, S5_TILE, S5_LANES), F32), state, state],
        compiler_params=_params(2),
    )(proj, pm, pm_t, bre, bim, cre_t, cim_t, lam, d)


def _glu_fwd(y, w, b, og, tm):
    T = y.shape[0]

    def body(y_ref, w_ref, b_ref, og_ref, z_ref, o_ref, r_ref):
        y1 = _gelu(y_ref[...])
        z = _dot(y1.astype(BF16), w_ref[...]) + b_ref[...]
        y2 = y1 * _sigmoid(z)
        r = lax.rsqrt(jnp.mean(y2 * y2, axis=-1, keepdims=True) + EPS)
        z_ref[...] = z
        o_ref[...] = (y2 * r * og_ref[...]).astype(BF16)
        r_ref[...] = r

    row = pl.BlockSpec((tm, SSM_WIDTH), lambda i: (i, 0))
    vec = pl.BlockSpec((1, SSM_WIDTH), lambda i: (0, 0))
    return pl.pallas_call(
        body, name="glu_fwd", grid=(T // tm,),
        in_specs=[row, pl.BlockSpec((SSM_WIDTH, SSM_WIDTH), lambda i: (0, 0)), vec, vec],
        out_specs=[row, row, pl.BlockSpec((tm, 1), lambda i: (i, 0))],
        out_shape=[jax.ShapeDtypeStruct((T, SSM_WIDTH), F32), jax.ShapeDtypeStruct((T, SSM_WIDTH), BF16),
                   jax.ShapeDtypeStruct((T, 1), F32)],
        compiler_params=_params(1),
    )(y, w, b, og)


def _out_proj_fwd(x, y_ret, y_ssm, w, g, tm):
    T = x.shape[0]

    def body(x_ref, a_ref, b_ref, w_ref, g_ref, x2_ref, h_ref, r_ref):
        x2 = x_ref[...] + _dot(a_ref[...], w_ref[0:RET_WIDTH, :]) + _dot(b_ref[...], w_ref[RET_WIDTH:D_MODEL, :])
        r = lax.rsqrt(jnp.mean(x2 * x2, axis=-1, keepdims=True) + EPS)
        x2_ref[...] = x2
        h_ref[...] = (x2 * r * g_ref[...]).astype(BF16)
        r_ref[...] = r

    full = pl.BlockSpec((tm, D_MODEL), lambda i: (i, 0))
    half = pl.BlockSpec((tm, RET_WIDTH), lambda i: (i, 0))
    return pl.pallas_call(
        body, name="out_proj_fwd", grid=(T // tm,),
        in_specs=[full, half, half, pl.BlockSpec((D_MODEL, D_MODEL), lambda i: (0, 0)),
                  pl.BlockSpec((1, D_MODEL), lambda i: (0, 0))],
        out_specs=[full, full, pl.BlockSpec((tm, 1), lambda i: (i, 0))],
        out_shape=[jax.ShapeDtypeStruct((T, D_MODEL), F32), jax.ShapeDtypeStruct((T, D_MODEL), BF16),
                   jax.ShapeDtypeStruct((T, 1), F32)],
        compiler_params=_params(1),
    )(x, y_ret, y_ssm, w, g)


def _ffn_up(h, wg, wu, tm, carry=None):
    T = h.shape[0]

    def body(h_ref, wg_ref, wu_ref, a_ref, b_ref, f_ref):
        hb = h_ref[...]
        a = _dot(hb, wg_ref[...])
        b = _dot(hb, wu_ref[...])
        a_ref[...] = a.astype(BF16)
        b_ref[...] = b.astype(BF16)
        f_ref[...] = (a * _sigmoid(a) * b).astype(BF16)

    wspec = pl.BlockSpec((None, D_MODEL, FF_BLK), lambda j, i: (j, 0, 0))
    ospec = pl.BlockSpec((None, tm, FF_BLK), lambda j, i: (j, i, 0))
    oshape = jax.ShapeDtypeStruct((N_DEV, T, FF_BLK), BF16)
    return _pcall(
        body, "ffn_up", (N_DEV, T // tm),
        [pl.BlockSpec((tm, D_MODEL), lambda j, i: (i, 0)), wspec, wspec],
        [ospec, ospec, ospec], [oshape, oshape, oshape], [], (h, wg, wu), carry)


def _ffn_down_loss(f, wd, x2, tgt, g, tm):
    T = x2.shape[0]

    def body(f_ref, w_hbm, x2_ref, t_ref, g_ref, dx_ref, dxb_ref, loss_ref, dg_ref, w_ref, sem):
        i = pl.program_id(0)

        @pl.when(i == 0)
        def _():
            _load_resident(w_hbm, w_ref, sem)
            loss_ref[...] = jnp.zeros_like(loss_ref)
            dg_ref[...] = jnp.zeros_like(dg_ref)

        x3 = x2_ref[...]
        for k in range(N_DEV):
            x3 = x3 + _dot(f_ref[k], w_ref[k])
        gv = g_ref[...]
        r = lax.rsqrt(jnp.mean(x3 * x3, axis=-1, keepdims=True) + EPS)
        err = x3 * r * gv - t_ref[...]
        tile_loss = 0.5 * jnp.sum(jnp.mean(err * err, axis=-1, keepdims=True), axis=0, keepdims=True)
        dx, dgt = _rms_bwd(err * (1.0 / D_MODEL), x3, r, gv)
        dx_ref[...] = dx
        dxb_ref[...] = dx.astype(BF16)
        loss_ref[...] += jnp.broadcast_to(tile_loss, loss_ref.shape)
        dg_ref[...] += jnp.sum(dgt, axis=0, keepdims=True)

    full = pl.BlockSpec((tm, D_MODEL), lambda i: (i, 0))
    vec = pl.BlockSpec((1, D_MODEL), lambda i: (0, 0))
    return pl.pallas_call(
        body, name="ffn_down_loss", grid=(T // tm,),
        in_specs=[pl.BlockSpec((N_DEV, tm, FF_BLK), lambda i: (0, i, 0)), ANY_SPEC, full, full, vec],
        out_specs=[full, full, pl.BlockSpec((8, LANE), lambda i: (0, 0)), vec],
        out_shape=[jax.ShapeDtypeStruct((T, D_MODEL), F32), jax.ShapeDtypeStruct((T, D_MODEL), BF16),
                   jax.ShapeDtypeStruct((8, LANE), F32), jax.ShapeDtypeStruct((1, D_MODEL), F32)],
        scratch_shapes=[pltpu.VMEM(wd.shape, wd.dtype), pltpu.SemaphoreType.DMA],
        compiler_params=_params(1),
    )(f, wd, x2, tgt, g)


def _ffn_bwd_act(dxb, wd, a, b, tm):
    T = dxb.shape[0]

    def body(dx_ref, w_ref, a_ref, b_ref, da_ref, db_ref):
        df = _dot_nt(dx_ref[...], w_ref[...])
        a = a_ref[...].astype(F32)
        b = b_ref[...].astype(F32)
        sg = _sigmoid(a)
        da_ref[...] = (df * b * sg * (1.0 + a * (1.0 - sg))).astype(BF16)
        db_ref[...] = (df * a * sg).astype(BF16)

    blk = pl.BlockSpec((None, tm, FF_BLK), lambda j, i: (j, i, 0))
    oshape = jax.ShapeDtypeStruct((N_DEV, T, FF_BLK), BF16)
    return pl.pallas_call(
        body, name="ffn_bwd_act", grid=(N_DEV, T // tm),
        in_specs=[pl.BlockSpec((tm, D_MODEL), lambda j, i: (i, 0)),
                  pl.BlockSpec((None, FF_BLK, D_MODEL), lambda j, i: (j, 0, 0)), blk, blk],
        out_specs=[blk, blk], out_shape=[oshape, oshape],
        compiler_params=_params(2),
    )(dxb, wd, a, b)


def _ffn_bwd_in(da, db, wg, wu, tm, carry=None):
    T = da.shape[1]

    def body(da_ref, db_ref, wg_ref, wu_ref, dh_ref):
        part = _dot_nt(da_ref[...], wg_ref[...]) + _dot_nt(db_ref[...], wu_ref[...])

        @pl.when(pl.program_id(1) == 0)
        def _():
            dh_ref[...] = part

        @pl.when(pl.program_id(1) > 0)
        def _():
            dh_ref[...] += part

    ablk = pl.BlockSpec((None, tm, FF_BLK), lambda i, k: (k, i, 0))
    wblk = pl.BlockSpec((None, D_MODEL, FF_BLK), lambda i, k: (k, 0, 0))
    return _pcall(
        body, "ffn_bwd_in", (T // tm, N_DEV), [ablk, ablk, wblk, wblk],
        [pl.BlockSpec((tm, D_MODEL), lambda i, k: (i, 0))], [jax.ShapeDtypeStruct((T, D_MODEL), F32)],
        [], (da, db, wg, wu), carry)


def _ffn_wgrad_up(h, da, db, tk, carry=None):
    T = h.shape[0]
    nk = T // tk

    def body(h_ref, da_ref, db_ref, g_ref, u_ref, accg, accu):
        k = pl.program_id(1)

        @pl.when(k == 0)
        def _():
            accg[...] = jnp.zeros_like(accg)
            accu[...] = jnp.zeros_like(accu)

        hb = h_ref[...]
        accg[...] += _dot_tn(hb, da_ref[...])
        accu[...] += _dot_tn(hb, db_ref[...])

        @pl.when(k == nk - 1)
        def _():
            g_ref[...] = accg[...].astype(BF16)
            u_ref[...] = accu[...].astype(BF16)

    blk = pl.BlockSpec((None, tk, FF_BLK), lambda j, k: (j, k, 0))
    ospec = pl.BlockSpec((None, D_MODEL, FF_BLK), lambda j, k: (j, 0, 0))
    oshape = jax.ShapeDtypeStruct((N_DEV, D_MODEL, FF_BLK), BF16)
    return _pcall(
        body, "ffn_wgrad_up", (N_DEV, nk),
        [pl.BlockSpec((tk, D_MODEL), lambda j, k: (k, 0)), blk, blk],
        [ospec, ospec], [oshape, oshape],
        [pltpu.VMEM((D_MODEL, FF_BLK), F32), pltpu.VMEM((D_MODEL, FF_BLK), F32)], (h, da, db), carry)


def _ffn_wgrad_down(f, dxb, tk):
    T = dxb.shape[0]
    nk = T // tk

    def body(f_ref, dx_ref, o_ref, acc):
        k = pl.program_id(1)

        @pl.when(k == 0)
        def _():
            acc[...] = jnp.zeros_like(acc)

        acc[...] += _dot_tn(f_ref[...], dx_ref[...])

        @pl.when(k == nk - 1)
        def _():
            o_ref[...] = acc[...].astype(BF16)

    return pl.pallas_call(
        body, name="ffn_wgrad_down", grid=(N_DEV, nk),
        in_specs=[pl.BlockSpec((None, tk, FF_BLK), lambda j, k: (j, k, 0)),
                  pl.BlockSpec((tk, D_MODEL), lambda j, k: (k, 0))],
        out_specs=pl.BlockSpec((None, FF_BLK, D_MODEL), lambda j, k: (j, 0, 0)),
        out_shape=jax.ShapeDtypeStruct((N_DEV, FF_BLK, D_MODEL), BF16),
        scratch_shapes=[pltpu.VMEM((FF_BLK, D_MODEL), F32)],
        compiler_params=_params(2),
    )(f, dxb)


def _out_proj_bwd(dh2, x2, r2, g, dx3, w, tm):
    T = x2.shape[0]

    def body(dh_ref, x_ref, r_ref, g_ref, dx3_ref, w_ref, dx_ref, dxb_ref, dg_ref, a_ref, b_ref):
        @pl.when(pl.program_id(0) == 0)
        def _():
            dg_ref[...] = jnp.zeros_like(dg_ref)

        dxn, dgt = _rms_bwd(dh_ref[...], x_ref[...], r_ref[...], g_ref[...])
        dx = dx3_ref[...] + dxn
        dxv = dx.astype(BF16)
        dx_ref[...] = dx
        dxb_ref[...] = dxv
        dg_ref[...] += jnp.sum(dgt, axis=0, keepdims=True)
        a_ref[...] = _dot_nt(dxv, w_ref[0:RET_WIDTH, :])
        b_ref[...] = _dot_nt(dxv, w_ref[RET_WIDTH:D_MODEL, :])

    full = pl.BlockSpec((tm, D_MODEL), lambda i: (i, 0))
    vec = pl.BlockSpec((1, D_MODEL), lambda i: (0, 0))
    half = pl.BlockSpec((tm, RET_WIDTH), lambda i: (i, 0))
    hshape = jax.ShapeDtypeStruct((T, RET_WIDTH), F32)
    return pl.pallas_call(
        body, name="out_proj_bwd", grid=(T // tm,),
        in_specs=[full, full, pl.BlockSpec((tm, 1), lambda i: (i, 0)), vec, full,
                  pl.BlockSpec((D_MODEL, D_MODEL), lambda i: (0, 0))],
        out_specs=[full, full, vec, half, half],
        out_shape=[jax.ShapeDtypeStruct((T, D_MODEL), F32), jax.ShapeDtypeStruct((T, D_MODEL), BF16),
                   jax.ShapeDtypeStruct((1, D_MODEL), F32), hshape, hshape],
        compiler_params=_params(1),
    )(dh2, x2, r2, g, dx3, w)


def _wgrad_rows(name, a, b, tk):
    T, M = a.shape
    N = b.shape[1]
    nk = T // tk

    def body(a_ref, b_ref, o_ref, acc):
        k = pl.program_id(0)

        @pl.when(k == 0)
        def _():
            acc[...] = jnp.zeros_like(acc)

        acc[...] += _dot_tn(a_ref[...], b_ref[...])

        @pl.when(k == nk - 1)
        def _():
            o_ref[...] = acc[...].astype(BF16)

    return pl.pallas_call(
        body, name=name, grid=(nk,),
        in_specs=[pl.BlockSpec((tk, M), lambda k: (k, 0)), pl.BlockSpec((tk, N), lambda k: (k, 0))],
        out_specs=pl.BlockSpec((M, N), lambda k: (0, 0)),
        out_shape=jax.ShapeDtypeStruct((M, N), BF16),
        scratch_shapes=[pltpu.VMEM((M, N), F32)],
        compiler_params=_params(1),
    )(a, b)


def _glu_bwd(y, z, r, dyo, w, og, tm):
    T = y.shape[0]

    def body(y_ref, z_ref, r_ref, d_ref, w_ref, og_ref, dy_ref, dw_ref, db_ref, dog_ref):
        @pl.when(pl.program_id(0) == 0)
        def _():
            dw_ref[...] = jnp.zeros_like(dw_ref)
            db_ref[...] = jnp.zeros_like(db_ref)
            dog_ref[...] = jnp.zeros_like(dog_ref)

        y1, g1 = _gelu_and_grad(y_ref[...])
        sg = _sigmoid(z_ref[...])
        y2 = y1 * sg
        dy2, dogt = _rms_bwd(d_ref[...], y2, r_ref[...], og_ref[...])
        dog_ref[...] += jnp.sum(dogt, axis=0, keepdims=True)
        dz = dy2 * y1 * sg * (1.0 - sg)
        db_ref[...] += jnp.sum(dz, axis=0, keepdims=True)
        dzb = dz.astype(BF16)
        dw_ref[...] += _dot_tn(y1.astype(BF16), dzb)
        dy_ref[...] = (dy2 * sg + _dot_nt(dzb, w_ref[...])) * g1

    row = pl.BlockSpec((tm, SSM_WIDTH), lambda i: (i, 0))
    vec = pl.BlockSpec((1, SSM_WIDTH), lambda i: (0, 0))
    sq = pl.BlockSpec((SSM_WIDTH, SSM_WIDTH), lambda i: (0, 0))
    return pl.pallas_call(
        body, name="glu_bwd", grid=(T // tm,),
        in_specs=[row, row, pl.BlockSpec((tm, 1), lambda i: (i, 0)), row, sq, vec],
        out_specs=[row, sq, vec, vec],
        out_shape=[jax.ShapeDtypeStruct((T, SSM_WIDTH), F32), jax.ShapeDtypeStruct((SSM_WIDTH, SSM_WIDTH), F32),
                   jax.ShapeDtypeStruct((1, SSM_WIDTH), F32), jax.ShapeDtypeStruct((1, SSM_WIDTH), F32)],
        compiler_params=_params(1),
    )(y, z, r, dyo, w, og)


def _s5_bwd(proj, dy, bound, pm, pm_t, bre, bim, bre_t, bim_t, cre, cim, lam, d):
    T = proj.shape[0]
    nt = T // S5_TILE
    sp = _s5_specs(T, True)

    def body(u_ref, dy_ref, bound_ref, pm_ref, pmt_ref, bre_ref, bim_ref, bret_ref, bimt_ref, cre_ref, cim_ref,
             lam_ref, d_ref,
             du_ref, dbre_ref, dbim_ref, dcre_ref, dcim_ref, dlam_ref, dd_ref, carry, ptab, sr, si, gr, gi):
        lr = lam_ref[0:1, :]
        li = lam_ref[1:2, :]

        @pl.when(pl.program_id(1) == 0)
        def _():
            carry[...] = jnp.zeros_like(carry)
            _fill_power_table(ptab, lr, li)
            for ref in (dbre_ref, dbim_ref, dcre_ref, dcim_ref, dlam_ref, dd_ref):
                ref[...] = jnp.zeros_like(ref)

        u = _permute_rows_f32(pm_ref[...], u_ref[...])
        ub = u.astype(BF16)
        dyv = _permute_rows_f32(pm_ref[...], dy_ref[...])
        dyb = dyv.astype(BF16)
        _tile_set(sr, _dot(ub, bre_ref[...]))
        _tile_set(si, _dot(ub, bim_ref[...]))
        er, ei, _, _ = _s5_forward_states(sr, si, lr, li, bound_ref[0:1, :], bound_ref[1:2, :], ptab)
        _tile_set(gr, _dot(dyb, cre_ref[...]))
        _tile_set(gi, -_dot(dyb, cim_ref[...]))
        zr, zi = _chunk_scans(gr, gi, lr, -li, True)
        ar, ai = _table_rows(ptab, S5_STEPS - 1, True)
        fr, fi = _entering_states(zr, zi, carry[0:1, :], carry[1:2, :], ar, ai, True)
        acc_r = jnp.zeros((S5_CHUNKS, S5_LANES), F32)
        acc_i = jnp.zeros((S5_CHUNKS, S5_LANES), F32)
        for j in range(S5_STEPS):
            qr, qi = _table_rows(ptab, S5_STEPS - 1 - j, True)
            g_r = _step_get(gr, j) + qr * fr - qi * fi
            g_i = _step_get(gi, j) + qr * fi + qi * fr
            _step_set(gr, j, g_r)
            _step_set(gi, j, g_i)
            p_r, p_i = (er, ei) if j == 0 else (_step_get(sr, j - 1), _step_get(si, j - 1))
            acc_r += g_r * p_r + g_i * p_i
            acc_i += g_i * p_r - g_r * p_i
        dlam_ref[0:1, :] += jnp.sum(acc_r, axis=0, keepdims=True)
        dlam_ref[1:2, :] += jnp.sum(acc_i, axis=0, keepdims=True)
        g_all_r = _tile_get(gr)
        g_all_i = _tile_get(gi)
        carry[0:1, :] = g_all_r[0:1, :]
        carry[1:2, :] = g_all_i[0:1, :]
        grb = g_all_r.astype(BF16)
        gib = g_all_i.astype(BF16)
        du = (_dot(grb, bret_ref[...]) + _dot(gib, bimt_ref[...]) + d_ref[...] * dyv).astype(BF16)
        du_ref[...] = _dot(pmt_ref[...], du).astype(BF16)
        dbre_ref[...] += _dot_tn(grb, ub)
        dbim_ref[...] += _dot_tn(gib, ub)
        dcre_ref[...] += _dot_tn(dyb, _tile_get(sr).astype(BF16))
        dcim_ref[...] -= _dot_tn(dyb, _tile_get(si).astype(BF16))
        dd_ref[...] += jnp.sum(dyv * u, axis=0, keepdims=True)

    acc_ts = pl.BlockSpec((None, S5_LANES, LANE), lambda b, t: (b, 0, 0))
    acc_fs = pl.BlockSpec((None, LANE, S5_LANES), lambda b, t: (b, 0, 0))
    return pl.pallas_call(
        body, name="s5_bwd", grid=(S5_NBLK, nt),
        in_specs=[sp["u"], sp["rows"], sp["bound"], sp["perm"], sp["perm"], sp["to_state"], sp["to_state"],
                  sp["from_state"], sp["from_state"], sp["to_state"], sp["to_state"], sp["lam"], sp["d"]],
        out_specs=[sp["rows"], acc_ts, acc_ts, acc_fs, acc_fs, sp["lam"], sp["d"]],
        out_shape=[jax.ShapeDtypeStruct((T, SSM_WIDTH), BF16),
                   jax.ShapeDtypeStruct((S5_NBLK, S5_LANES, LANE), F32),
                   jax.ShapeDtypeStruct((S5_NBLK, S5_LANES, LANE), F32),
                   jax.ShapeDtypeStruct((S5_NBLK, LANE, S5_LANES), F32),
                   jax.ShapeDtypeStruct((S5_NBLK, LANE, S5_LANES), F32),
                   jax.ShapeDtypeStruct((S5_NBLK, 2, S5_LANES), F32),
                   jax.ShapeDtypeStruct((1, SSM_WIDTH), F32)],
        scratch_shapes=[pltpu.VMEM((2, S5_LANES), F32), pltpu.VMEM((2, S5_TILE, S5_LANES), F32)]
        + [pltpu.VMEM(S5_STATE_TILE, F32)] * 4,
        compiler_params=_params(2),
    )(proj, dy, bound, pm, pm_t, bre, bim, bre_t, bim_t, cre, cim, lam, d)


def _ret_bwd(proj, cosf, sinf, mask, rowdec, kdec, gtb, gn, sblk, dyr):
    T = proj.shape[0]
    nb = T // RET_BLOCK
    sp = _ret_specs(T, True)

    def body(q_ref, k_ref, v_ref, g_ref, cos_ref, sin_ref, mask_ref, rd_ref, kd_ref, gtb_ref, gn_ref, sb_ref, dy_ref,
             dq_ref, dk_ref, dv_ref, dg_ref, dgn_ref, dst):
        @pl.when(pl.program_id(1) == 0)
        def _():
            dst[...] = jnp.zeros_like(dst)
            dgn_ref[...] = jnp.zeros_like(dgn_ref)

        s_in = sb_ref[...]
        q, k, qb, kb, vb, pm, qd, o = _ret_common(q_ref, k_ref, v_ref, cos_ref, sin_ref, mask_ref, rd_ref, s_in)
        mu = jnp.mean(o, axis=-1, keepdims=True)
        oc = o - mu
        rstd = lax.rsqrt(jnp.mean(oc * oc, axis=-1, keepdims=True) + EPS)
        n = oc * rstd
        gt = g_ref[...]
        sg = _sigmoid(gt)
        sil = gt * sg
        gnv = gn_ref[...]
        dyv = dy_ref[...]
        dg_ref[...] = (dyv * (n * gnv) * (sg * (1.0 + gt * (1.0 - sg)))).astype(BF16)
        dgn_ref[...] += jnp.sum(dyv * sil * n, axis=0, keepdims=True)
        dn = dyv * sil * gnv
        do = rstd * (dn - jnp.mean(dn, axis=-1, keepdims=True) - n * jnp.mean(dn * n, axis=-1, keepdims=True))
        dob = do.astype(BF16)
        ds = dst[...]
        dsb = ds.astype(BF16)
        kd = kd_ref[...]
        rd = rd_ref[...]
        dv_ref[...] = (_dot_tn(pm, dob) + _dot((k * kd).astype(BF16), dsb)).astype(BF16)
        dpb = (_dot_nt(dob, vb) * mask_ref[...]).astype(BF16)
        dq = _dot(dpb, kb) + _dot_nt(dob, s_in.astype(BF16)) * rd
        dk = (_dot_tn(dpb, qb) + _dot_nt(vb, dsb) * kd) * (HEAD_DIM ** -0.5)
        dst[...] = gtb_ref[...] * ds + _dot_tn(qd, dob)
        c = cos_ref[...]
        s = sin_ref[...]
        dq_ref[...] = (dq * c + pltpu.roll(dq * s, HEAD_DIM // 2, 1)).astype(BF16)
        dk_ref[...] = (dk * c + pltpu.roll(dk * s, HEAD_DIM // 2, 1)).astype(BF16)

    oshape = jax.ShapeDtypeStruct((T, RET_WIDTH), BF16)
    return pl.pallas_call(
        body, name="ret_bwd", grid=(RET_HEADS, nb),
        in_specs=[sp["q"], sp["k"], sp["v"], sp["g"], sp["tab"], sp["tab"], sp["mask"], sp["dec"], sp["dec"],
                  sp["gtb"], sp["gn"], sp["state"], sp["rows"]],
        out_specs=[sp["rows"], sp["rows"], sp["rows"], sp["rows"], sp["gn"]],
        out_shape=[oshape, oshape, oshape, oshape, jax.ShapeDtypeStruct((1, RET_WIDTH), F32)],
        scratch_shapes=[pltpu.VMEM((HEAD_DIM, HEAD_DIM), F32)],
        compiler_params=_params(2),
    )(proj, proj, proj, proj, cosf, sinf, mask, rowdec, kdec, gtb, gn, sblk, dyr)


def _in_proj_bwd(dproj, w, x, r1, g, dx2, tm, carry=None):
    T = x.shape[0]

    def body(dp_ref, w_hbm, x_ref, r_ref, g_ref, dx2_ref, gx_ref, dg_ref, w_ref, sem):
        @pl.when(pl.program_id(0) == 0)
        def _():
            _load_resident(w_hbm, w_ref, sem)
            dg_ref[...] = jnp.zeros_like(dg_ref)

        dh = _dot_nt(dp_ref[:, 0:WIN_BLK], w_ref[0])
        for k in range(1, N_DEV):
            dh = dh + _dot_nt(dp_ref[:, k * WIN_BLK:(k + 1) * WIN_BLK], w_ref[k])
        dxn, dgt = _rms_bwd(dh, x_ref[...], r_ref[...], g_ref[...])
        gx_ref[...] = dx2_ref[...] + dxn
        dg_ref[...] += jnp.sum(dgt, axis=0, keepdims=True)

    full = pl.BlockSpec((tm, D_MODEL), lambda i: (i, 0))
    vec = pl.BlockSpec((1, D_MODEL), lambda i: (0, 0))
    return _pcall(
        body, "in_proj_bwd", (T // tm,),
        [pl.BlockSpec((tm, IN_WIDTH), lambda i: (i, 0)), ANY_SPEC,
         full, pl.BlockSpec((tm, 1), lambda i: (i, 0)), vec, full],
        [full, vec],
        [jax.ShapeDtypeStruct((T, D_MODEL), F32), jax.ShapeDtypeStruct((1, D_MODEL), F32)],
        [pltpu.VMEM(w.shape, w.dtype), pltpu.SemaphoreType.DMA], (dproj, w, x, r1, g, dx2), carry)


def _in_proj_wgrad(h, dproj, tk, carry=None):
    T = h.shape[0]
    nk = T // tk

    def body(h_ref, dp_ref, o_ref, acc):
        k = pl.program_id(1)

        @pl.when(k == 0)
        def _():
            acc[...] = jnp.zeros_like(acc)

        acc[...] += _dot_tn(h_ref[...], dp_ref[...])

        @pl.when(k == nk - 1)
        def _():
            o_ref[...] = acc[...].astype(BF16)

    return _pcall(
        body, "in_proj_wgrad", (N_DEV, nk),
        [pl.BlockSpec((tk, D_MODEL), lambda j, k: (k, 0)), pl.BlockSpec((tk, WIN_BLK), lambda j, k: (k, j))],
        [pl.BlockSpec((None, D_MODEL, WIN_BLK), lambda j, k: (j, 0, 0))],
        [jax.ShapeDtypeStruct((N_DEV, D_MODEL, WIN_BLK), BF16)],
        [pltpu.VMEM((D_MODEL, WIN_BLK), F32)], (h, dproj), carry)


def _rope_tables(T):
    half = HEAD_DIM // 2
    freqs = ROPE_BASE ** (-jnp.arange(half, dtype=F32) / half)
    ang = jnp.arange(T, dtype=F32)[:, None] * freqs[None, :]
    c = jnp.cos(ang)
    s = jnp.sin(ang)
    return jnp.concatenate([c, c], axis=1), jnp.concatenate([-s, s], axis=1)


def _retention_tables():
    hh = jnp.arange(RET_HEADS, dtype=F32)
    log_g = jnp.log1p(-(2.0 ** (-5.0 - hh)))[:, None, None]
    i = jnp.arange(RET_BLOCK)
    ci = (i // CHUNK)[:, None]
    cj = (i // CHUNK)[None, :]
    diff = (i[:, None] - i[None, :]).astype(F32)
    expo = jnp.where(ci == cj, jnp.abs(diff), diff)
    mask = jnp.where((cj <= ci)[None], jnp.exp(log_g * expo[None]), 0.0)
    r = jnp.arange(RET_BLOCK, dtype=F32)[None, :, None]
    ones = jnp.ones((1, 1, HEAD_DIM), F32)
    rowdec = jnp.exp(log_g * (r + 1.0)) * ones
    kdec = jnp.exp(log_g * (RET_BLOCK - 1.0 - r)) * ones
    gtb = jnp.exp(log_g * float(RET_BLOCK)) * ones
    return mask, rowdec, kdec, gtb


def _s5_discretise(a_re, a_im, log_dt, b_re, b_im):
    lam = lax.complex(a_re, a_im)
    dt = jnp.exp(log_dt)[:, None]
    lam_bar = jnp.exp(lam * dt)
    b_bar = ((lam_bar - 1.0) / lam)[..., None] * lax.complex(b_re, b_im)
    return jnp.real(lam_bar), jnp.imag(lam_bar), jnp.real(b_bar), jnp.imag(b_bar)


def _to_state_blockdiag(m):
    eye = jnp.eye(S5_GB, dtype=m.dtype)
    t = jnp.einsum("bgpc,gh->bgchp", m.reshape(S5_NBLK, S5_GB, SSM_STATE, SSM_GROUP), eye)
    return t.reshape(S5_NBLK, LANE, S5_LANES)


def _from_state_blockdiag(m):
    eye = jnp.eye(S5_GB, dtype=m.dtype)
    t = jnp.einsum("bgcp,gh->bgphc", m.reshape(S5_NBLK, S5_GB, SSM_GROUP, SSM_STATE), eye)
    return t.reshape(S5_NBLK, S5_LANES, LANE)


def _diag_of_state_major(acc):
    eye = jnp.eye(S5_GB, dtype=acc.dtype)
    t = acc.reshape(S5_NBLK, S5_GB, SSM_STATE, S5_GB, SSM_GROUP)
    return jnp.einsum("bgphc,gh->bgpc", t, eye).reshape(SSM_GROUPS, SSM_STATE, SSM_GROUP)


def _diag_of_channel_major(acc):
    eye = jnp.eye(S5_GB, dtype=acc.dtype)
    t = acc.reshape(S5_NBLK, S5_GB, SSM_GROUP, S5_GB, SSM_STATE)
    return jnp.einsum("bgchp,gh->bgcp", t, eye).reshape(SSM_GROUPS, SSM_GROUP, SSM_STATE)


SMALL_PARTIALS = (("ret_gn_g", 1024), ("lam_re", 4096), ("lam_im", 4096),
                  ("bbar_re", 65536), ("bbar_im", 65536), ("c_re", 65536), ("c_im", 65536),
                  ("ssm_d", 1024), ("b_glu", 1024), ("out_g", 1024), ("norm_ffn_g", 2048), ("norm_final_g", 2048))


def _forward_backward(x, tgt, shards, sm, tm=512):
    T = x.shape[0]
    cosf, sinf = _rope_tables(T)
    mask, rowdec, kdec, gtb = _retention_tables()
    lbr, lbi, bbr, bbi = _s5_discretise(sm["ssm_a_re"], sm["ssm_a_im"], sm["ssm_log_dt"], sm["ssm_b_re"],
                                        sm["ssm_b_im"])
    bre = _to_state_blockdiag(bbr).astype(BF16)
    bim = _to_state_blockdiag(bbi).astype(BF16)
    cre_t = _from_state_blockdiag(sm["ssm_c_re"]).astype(BF16)
    cim_t = _from_state_blockdiag(sm["ssm_c_im"]).astype(BF16)
    bre_t = jnp.swapaxes(bre, 1, 2)
    bim_t = jnp.swapaxes(bim, 1, 2)
    cre = jnp.swapaxes(cre_t, 1, 2)
    cim = jnp.swapaxes(cim_t, 1, 2)
    lam = jnp.stack([lbr.reshape(S5_NBLK, S5_LANES), lbi.reshape(S5_NBLK, S5_LANES)], axis=1)
    pm = _step_major_permutation()
    pm_t = pm.T
    row = lambda v: v.reshape(1, -1)
    g_mix, g_ffn, g_fin = row(sm["norm_mix_g"]), row(sm["norm_ffn_g"]), row(sm["norm_final_g"])
    gn, dsk, bglu, og = row(sm["ret_gn_g"]), row(sm["ssm_d"]), row(sm["ssm_b_glu"]), row(sm["ssm_out_g"])

    (w_in,) = _exchange_call("weight_gather", [shards["w_in"]], True)
    proj, h1, r1, w_glu, w_out, w_gate = _in_proj_fwd(
        x, g_mix, w_in, 256, _Exchange([shards["ssm_w_glu"], shards["w_out"], shards["w_gate"]], True))
    w_glu = w_glu.reshape(SSM_WIDTH, SSM_WIDTH)
    w_out = w_out.reshape(D_MODEL, D_MODEL)
    y_ret, sblk, w_up = _ret_fwd(proj, cosf, sinf, mask, rowdec, kdec, gtb, gn, _Exchange([shards["w_up"]], True))
    y_s5, bound = _s5_fwd(proj, pm, pm_t, bre, bim, cre_t, cim_t, lam, dsk)
    z, y_ssm, r_ssm = _glu_fwd(y_s5, w_glu, bglu, og, 256)
    x2, h2, r2 = _out_proj_fwd(x, y_ret, y_ssm, w_out, g_ffn, 256)
    a, b, f, w_down = _ffn_up(h2, w_gate, w_up, tm, _Exchange([shards["w_down"]], True))
    dx3, dx3b, loss8, dg_fin = _ffn_down_loss(f, w_down, x2, tgt, g_fin, 256)

    landed = {}
    da, db = _ffn_bwd_act(dx3b, w_down, a, b, tm)
    dw_down = _ffn_wgrad_down(f, dx3b, tm)
    dw_gate, dw_up, landed["w_down"] = _ffn_wgrad_up(h2, da, db, tm, _Exchange([dw_down], False))
    dh2, landed["w_gate"], landed["w_up"] = _ffn_bwd_in(da, db, w_gate, w_up, min(1024, T),
                                                        _Exchange([dw_gate, dw_up], False))
    dx2, dx2b, dg_ffn, dy_ret, dy_ssm = _out_proj_bwd(dh2, x2, r2, g_ffn, dx3, w_out, 256)
    dw_out = jnp.concatenate([_wgrad_rows("out_proj_wgrad_ret", y_ret, dx2b, tm),
                              _wgrad_rows("out_proj_wgrad_ssm", y_ssm, dx2b, tm)], axis=0)
    dy_s5, dw_glu, db_glu, dog = _glu_bwd(y_s5, z, r_ssm, dy_ssm, w_glu, og, 256)
    du, dbre, dbim, dcre, dcim, dlam, dd = _s5_bwd(proj, dy_s5, bound, pm, pm_t, bre, bim, bre_t, bim_t, cre, cim,
                                                   lam, dsk)
    dq, dk, dv, dgate, dgn = _ret_bwd(proj, cosf, sinf, mask, rowdec, kdec, gtb, gn, sblk, dy_ret)
    dproj = jnp.concatenate([dq, dk, dv, dgate, du], axis=1)
    dw_in, landed["w_out"], landed["ssm_w_glu"] = _in_proj_wgrad(
        h1, dproj, tm, _Exchange([dw_out.reshape(N_DEV, D_MODEL // N_DEV, D_MODEL),
                                  dw_glu.astype(BF16).reshape(N_DEV, SSM_WIDTH // N_DEV, SSM_WIDTH)], False))
    small = dict(ret_gn_g=dgn, lam_re=dlam[:, 0], lam_im=dlam[:, 1],
                 bbar_re=_diag_of_state_major(dbre), bbar_im=_diag_of_state_major(dbim),
                 c_re=_diag_of_channel_major(dcre), c_im=_diag_of_channel_major(dcim),
                 ssm_d=dd, b_glu=db_glu, out_g=dog, norm_ffn_g=dg_ffn, norm_final_g=dg_fin)
    packed = _pack([small[n] for n, _ in SMALL_PARTIALS])
    grad_x, dg_mix, landed["w_in"], small_landed = _in_proj_bwd(
        dproj, w_in, x, r1, g_mix, dx2, 256, _Exchange([dw_in, packed], [False, True]))
    (mix_landed,) = _exchange_call("mix_gain_grad_gather", [_pack([dg_mix])], True)
    summed = dict(zip([n for n, _ in SMALL_PARTIALS],
                      _unpack(_sum_partials("small_grad_sum", small_landed), [(sz,) for _, sz in SMALL_PARTIALS])))
    summed["norm_mix_g"] = _sum_partials("mix_gain_grad_sum", mix_landed).reshape(-1)
    return loss8[0, 0], grad_x, landed, summed


def _small_grads(summed, sm):
    _, vjp = jax.vjp(_s5_discretise, sm["ssm_a_re"], sm["ssm_a_im"], sm["ssm_log_dt"], sm["ssm_b_re"], sm["ssm_b_im"])
    gp = (SSM_GROUPS, SSM_STATE)
    da_re, da_im, dlog_dt, db_re, db_im = vjp((summed["lam_re"].reshape(gp), summed["lam_im"].reshape(gp),
                                               summed["bbar_re"].reshape(gp + (SSM_GROUP,)),
                                               summed["bbar_im"].reshape(gp + (SSM_GROUP,))))
    return dict(norm_mix_g=summed["norm_mix_g"], ret_gn_g=summed["ret_gn_g"], ssm_a_re=da_re, ssm_a_im=da_im,
                ssm_log_dt=dlog_dt, ssm_b_re=db_re, ssm_b_im=db_im,
                ssm_c_re=summed["c_re"].reshape(SSM_GROUPS, SSM_GROUP, SSM_STATE),
                ssm_c_im=summed["c_im"].reshape(SSM_GROUPS, SSM_GROUP, SSM_STATE),
                ssm_d=summed["ssm_d"], ssm_b_glu=summed["b_glu"], ssm_out_g=summed["out_g"],
                norm_ffn_g=summed["norm_ffn_g"], norm_final_g=summed["norm_final_g"])


def _adamw_math(w, g, m, v):
    m2 = ADAM_B1 * m + (1.0 - ADAM_B1) * g
    v2 = ADAM_B2 * v + (1.0 - ADAM_B2) * (g * g)
    delta = -ADAM_LR * ((m2 / ADAM_BC1) / (jnp.sqrt(v2 / ADAM_BC2) + ADAM_EPS) + ADAM_WD * w)
    return delta, m2, v2


def _adamw_shard(name, parts, w, m, v, tr):
    rows, cols = w.shape

    def body(p_ref, w_ref, m_ref, v_ref, g_ref, d_ref, m2_ref, v2_ref):
        g = p_ref[0].astype(F32)
        for s in range(1, N_DEV):
            g = g + p_ref[s].astype(F32)
        d, m2, v2 = _adamw_math(w_ref[...], g, m_ref[...], v_ref[...])
        g_ref[...] = g
        d_ref[...] = d
        m2_ref[...] = m2
        v2_ref[...] = v2

    blk = pl.BlockSpec((tr, cols), lambda i: (i, 0))
    oshape = jax.ShapeDtypeStruct((rows, cols), F32)
    return pl.pallas_call(
        body, name=name, grid=(rows // tr,),
        in_specs=[pl.BlockSpec((N_DEV, tr, cols), lambda i: (0, i, 0)), blk, blk, blk],
        out_specs=[blk, blk, blk, blk], out_shape=[oshape] * 4,
        compiler_params=_params(1),
    )(parts, w, m, v)


def _sum_partials(name, parts):
    rows = parts.shape[1]

    def body(p_ref, o_ref):
        g = p_ref[0]
        for s in range(1, N_DEV):
            g = g + p_ref[s]
        o_ref[...] = g

    return pl.pallas_call(
        body, name=name, grid=(1,),
        in_specs=[pl.BlockSpec((N_DEV, rows, LANE), lambda i: (0, 0, 0))],
        out_specs=pl.BlockSpec((rows, LANE), lambda i: (0, 0)),
        out_shape=jax.ShapeDtypeStruct((rows, LANE), F32),
        compiler_params=_params(1),
    )(parts)


def _adamw_small(w, g, m, v):
    rows = w.shape[0]

    def body(w_ref, g_ref, m_ref, v_ref, d_ref, m2_ref, v2_ref):
        d, m2, v2 = _adamw_math(w_ref[...], g_ref[...], m_ref[...], v_ref[...])
        d_ref[...] = d
        m2_ref[...] = m2
        v2_ref[...] = v2

    blk = pl.BlockSpec((rows, LANE), lambda i: (0, 0))
    oshape = jax.ShapeDtypeStruct((rows, LANE), F32)
    return pl.pallas_call(
        body, name="adamw_small", grid=(1,), in_specs=[blk] * 4, out_specs=[blk] * 3, out_shape=[oshape] * 3,
        compiler_params=_params(1),
    )(w, g, m, v)


def _pack(arrays):
    cols = []
    for a in arrays:
        flat = a.reshape(-1).astype(F32)
        pad = (-flat.shape[0]) % LANE
        cols.append(jnp.pad(flat, (0, pad)) if pad else flat)
    return jnp.concatenate(cols).reshape(-1, LANE)


def _unpack(packed, shapes):
    flat = packed.reshape(-1)
    out, off = [], 0
    for shp in shapes:
        n = math.prod(shp)
        out.append(flat[off:off + n].reshape(shp))
        off += n + ((-n) % LANE)
    return out


WEIGHTS = ("norm_mix_g", "w_in", "ret_gn_g", "ssm_a_re", "ssm_a_im", "ssm_log_dt", "ssm_b_re", "ssm_b_im",
           "ssm_c_re", "ssm_c_im", "ssm_d", "ssm_w_glu", "ssm_b_glu", "ssm_out_g", "w_out", "norm_ffn_g", "w_gate",
           "w_up", "w_down", "norm_final_g")
BIG = ("w_in", "ssm_w_glu", "w_out", "w_gate", "w_up", "w_down")
SMALL = tuple(n for n in WEIGHTS if n not in BIG)
ADAM_ROWS = {"w_in": 256, "ssm_w_glu": 128, "w_out": 128, "w_gate": 256, "w_up": 256, "w_down": 176}


def kernel(x, norm_mix_g, w_in, ret_gn_g, ssm_a_re, ssm_a_im, ssm_log_dt, ssm_b_re, ssm_b_im, ssm_c_re, ssm_c_im, ssm_d, ssm_w_glu, ssm_b_glu, ssm_out_g, w_out, norm_ffn_g, w_gate, w_up, w_down, norm_final_g, loss_target, m_norm_mix_g, m_w_in, m_ret_gn_g, m_ssm_a_re, m_ssm_a_im, m_ssm_log_dt, m_ssm_b_re, m_ssm_b_im, m_ssm_c_re, m_ssm_c_im, m_ssm_d, m_ssm_w_glu, m_ssm_b_glu, m_ssm_out_g, m_w_out, m_norm_ffn_g, m_w_gate, m_w_up, m_w_down, m_norm_final_g, v_norm_mix_g, v_w_in, v_ret_gn_g, v_ssm_a_re, v_ssm_a_im, v_ssm_log_dt, v_ssm_b_re, v_ssm_b_im, v_ssm_c_re, v_ssm_c_im, v_ssm_d, v_ssm_w_glu, v_ssm_b_glu, v_ssm_out_g, v_w_out, v_norm_ffn_g, v_w_gate, v_w_up, v_w_down, v_norm_final_g):
    given = dict(locals())
    w = {n: given[n] for n in WEIGHTS}
    m = {n: given["m_" + n] for n in WEIGHTS}
    v = {n: given["v_" + n] for n in WEIGHTS}
    drop = lambda n, a: a if n == "norm_final_g" else a[0]
    w0 = {n: drop(n, w[n]) for n in WEIGHTS}
    m0 = {n: drop(n, m[n]) for n in WEIGHTS}
    v0 = {n: drop(n, v[n]) for n in WEIGHTS}

    sm = {n: w0[n] for n in SMALL}
    shards = {n: w0[n].astype(BF16) for n in BIG}
    loss_local, grad_x, landed, summed = _forward_backward(x[0], loss_target[0], shards, sm)
    loss = lax.psum(loss_local, MESH_AXES)
    gsmall = _small_grads(summed, sm)

    grads, delta, new_m, new_v = {}, {}, {}, {}
    for n in BIG:
        g, d, m2, v2 = _adamw_shard("adamw_" + n, landed[n], w0[n], m0[n], v0[n], ADAM_ROWS[n])
        grads[n], delta[n], new_m[n], new_v[n] = g, d, m2, v2
    shapes = [w0[n].shape for n in SMALL]
    gs = [gsmall[n].reshape(w0[n].shape) for n in SMALL]
    d_p, m_p, v_p = _adamw_small(_pack([w0[n] for n in SMALL]), _pack(gs), _pack([m0[n] for n in SMALL]),
                                 _pack([v0[n] for n in SMALL]))
    for n, g, d, m2, v2 in zip(SMALL, gs, _unpack(d_p, shapes), _unpack(m_p, shapes), _unpack(v_p, shapes)):
        grads[n], delta[n], new_m[n], new_v[n] = g, d, m2, v2

    lift = lambda n, a: a.reshape(w[n].shape)
    return (loss, grad_x[None], *[lift(n, grads[n]) for n in WEIGHTS], *[lift(n, delta[n]) for n in WEIGHTS],
            *[lift(n, new_m[n]) for n in WEIGHTS], *[lift(n, new_v[n]) for n in WEIGHTS])
```

```python
import functools
import math

import jax
import jax.numpy as jnp
from jax import lax
from jax.experimental import pallas as pl
from jax.experimental.pallas import tpu as pltpu

F32 = jnp.float32
BF16 = jnp.bfloat16

D_MODEL = 2048
RET_WIDTH = 1024
RET_HEADS = 8
HEAD_DIM = 128
CHUNK = 64
SSM_WIDTH = 1024
SSM_GROUP = 16
SSM_GROUPS = 64
SSM_STATE = 64
D_FF = 5632
IN_WIDTH = 5120
ROPE_BASE = 10000.0
EPS = 1e-6
N_DEV = 8
MESH_AXES = ("x", "y", "c")

WIN_BLK = IN_WIDTH // N_DEV
FF_BLK = D_FF // N_DEV
RET_BLOCK = 256
RET_HPS = 2
S5_TILE = 256
S5_CHUNKS = 8
S5_STEPS = S5_TILE // S5_CHUNKS
S5_GB = 8
S5_NBLK = SSM_GROUPS // S5_GB
S5_LANES = S5_GB * SSM_STATE
LANE = 128

ADAM_LR = 0.001
ADAM_B1 = 0.9
ADAM_B2 = 0.999
ADAM_EPS = 1e-08
ADAM_WD = 0.01
ADAM_STEP = 10
ADAM_BC1 = 1.0 - ADAM_B1 ** ADAM_STEP
ADAM_BC2 = 1.0 - ADAM_B2 ** ADAM_STEP

VMEM_LIMIT = 56 * 1024 * 1024

NT = (((1,), (1,)), ((), ()))
TN = (((0,), (0,)), ((), ()))


def _params(n_grid):
    return pltpu.CompilerParams(dimension_semantics=("arbitrary",) * n_grid, vmem_limit_bytes=VMEM_LIMIT)


def _dot(a, b):
    return jnp.dot(a, b, preferred_element_type=F32)


def _dot_nt(a, b):
    return lax.dot_general(a, b, NT, preferred_element_type=F32)


def _dot_tn(a, b):
    return lax.dot_general(a, b, TN, preferred_element_type=F32)


def _sigmoid(x):
    return 1.0 / (1.0 + jnp.exp(-x))


_GELU_C = math.sqrt(2.0 / math.pi)
_GELU_A = 0.044715


def _gelu(x):
    t = jnp.tanh(_GELU_C * (x + _GELU_A * x * x * x))
    return 0.5 * x * (1.0 + t)


def _gelu_and_grad(x):
    t = jnp.tanh(_GELU_C * (x + _GELU_A * x * x * x))
    g = 0.5 * (1.0 + t) + 0.5 * x * (1.0 - t * t) * _GELU_C * (1.0 + 3.0 * _GELU_A * x * x)
    return 0.5 * x * (1.0 + t), g


def _rms_bwd(dy, x, r, g):
    w = dy * g
    dx = r * w - x * (r * r * r) * jnp.mean(w * x, axis=-1, keepdims=True)
    return dx, dy * x * r


HBM_SPEC = pl.BlockSpec(memory_space=pltpu.HBM)
ANY_SPEC = pl.BlockSpec(memory_space=pl.ANY)


def _load_resident(src_hbm, dst_vmem, sem):
    cp = pltpu.make_async_copy(src_hbm, dst_vmem, sem)
    cp.start()
    cp.wait()


def _my_block():
    return 4 * lax.axis_index("x") + 2 * lax.axis_index("y") + lax.axis_index("c")


def _peer(k):
    px = lax.axis_index("x") ^ ((k >> 2) & 1)
    py = lax.axis_index("y") ^ ((k >> 1) & 1)
    pc = lax.axis_index("c") ^ (k & 1)
    return (px, py, pc), 4 * px + 2 * py + pc


class _Exchange:
    def __init__(self, payloads, gather):
        self.payloads = list(payloads)
        self.n = len(self.payloads)
        self.gather = [gather] * self.n if isinstance(gather, bool) else list(gather)

    def out_shape(self):
        return [jax.ShapeDtypeStruct(((N_DEV,) if g else ()) + p.shape, p.dtype)
                for p, g in zip(self.payloads, self.gather)]

    def scratch_shapes(self):
        return [pltpu.SemaphoreType.DMA((self.n, N_DEV - 1)), pltpu.SemaphoreType.DMA((self.n, N_DEV - 1)),
                pltpu.SemaphoreType.DMA((self.n,))]

    def _copies(self, ins, outs, sems, incoming):
        send_sems, recv_sems, local_sems = sems
        me = _my_block()
        src_of = lambda i, blk: ins[i] if self.gather[i] else ins[i].at[blk]
        local, remote = [], []
        for i in range(self.n):
            if not incoming:
                local.append(pltpu.make_async_copy(src_of(i, me), outs[i].at[me], local_sems.at[i]))
            for k in range(1, N_DEV):
                dev, blk = _peer(k)
                src, dst = (outs[i].at[blk], outs[i].at[blk]) if incoming else (src_of(i, blk), outs[i].at[me])
                remote.append(pltpu.make_async_remote_copy(
                    src_ref=src, dst_ref=dst, send_sem=send_sems.at[i, k - 1], recv_sem=recv_sems.at[i, k - 1],
                    device_id=dev, device_id_type=pl.DeviceIdType.MESH))
        return local, remote

    def start(self, ins, outs, sems):
        local, sends = self._copies(ins, outs, sems, False)
        for cp in local + sends:
            cp.start()

    def wait(self, ins, outs, sems):
        for cp in self._copies(ins, outs, sems, True)[1]:
            cp.wait_recv()
        local, sends = self._copies(ins, outs, sems, False)
        for cp in sends:
            cp.wait_send()
        for cp in local:
            cp.wait()


def _pcall(body, name, grid, in_specs, out_specs, out_shape, scratch_shapes, args, carry=None):
    n_in, n_out, n_scr = len(in_specs), len(out_specs), len(scratch_shapes)
    if carry is None:
        return pl.pallas_call(body, name=name, grid=grid, in_specs=in_specs, out_specs=out_specs, out_shape=out_shape,
                              scratch_shapes=scratch_shapes, compiler_params=_params(len(grid)))(*args)
    nx = carry.n

    def wrapped(*refs):
        cin, xin = refs[:n_in], refs[n_in:n_in + nx]
        cout, xout = refs[n_in + nx:n_in + nx + n_out], refs[n_in + nx + n_out:n_in + 2 * nx + n_out]
        rest = refs[n_in + 2 * nx + n_out:]
        cscr, sems = rest[:n_scr], rest[n_scr:]
        first = functools.reduce(jnp.logical_and, [pl.program_id(a) == 0 for a in range(len(grid))])
        last = functools.reduce(jnp.logical_and, [pl.program_id(a) == grid[a] - 1 for a in range(len(grid))])

        @pl.when(first)
        def _():
            carry.start(xin, xout, sems)

        body(*cin, *cout, *cscr)

        @pl.when(last)
        def _():
            carry.wait(xin, xout, sems)

    return pl.pallas_call(
        wrapped, name=name, grid=grid, in_specs=list(in_specs) + [HBM_SPEC] * nx,
        out_specs=list(out_specs) + [HBM_SPEC] * nx, out_shape=list(out_shape) + carry.out_shape(),
        scratch_shapes=list(scratch_shapes) + carry.scratch_shapes(), compiler_params=_params(len(grid)),
    )(*args, *carry.payloads)


def _exchange_call(name, payloads, gather):
    ex = _Exchange(payloads, gather)

    def body(*refs):
        ins, outs, sems = refs[:ex.n], refs[ex.n:2 * ex.n], refs[2 * ex.n:]
        ex.start(ins, outs, sems)
        ex.wait(ins, outs, sems)

    return pl.pallas_call(body, name=name, in_specs=[HBM_SPEC] * ex.n, out_specs=[HBM_SPEC] * ex.n,
                          out_shape=ex.out_shape(), scratch_shapes=ex.scratch_shapes())(*ex.payloads)


def _in_proj_fwd(x, g, w, tm, carry=None):
    T = x.shape[0]

    def body(x_ref, g_ref, w_hbm, proj_ref, h_ref, r_ref, w_ref, sem):
        @pl.when(pl.program_id(0) == 0)
        def _():
            _load_resident(w_hbm, w_ref, sem)

        xf = x_ref[...]
        r = lax.rsqrt(jnp.mean(xf * xf, axis=-1, keepdims=True) + EPS)
        h = (xf * r * g_ref[...]).astype(BF16)
        h_ref[...] = h
        r_ref[...] = r
        for j in range(N_DEV):
            proj_ref[:, j * WIN_BLK:(j + 1) * WIN_BLK] = _dot(h, w_ref[j])

    return _pcall(
        body, "in_proj_fwd", (T // tm,),
        [pl.BlockSpec((tm, D_MODEL), lambda i: (i, 0)), pl.BlockSpec((1, D_MODEL), lambda i: (0, 0)), ANY_SPEC],
        [pl.BlockSpec((tm, IN_WIDTH), lambda i: (i, 0)),
         pl.BlockSpec((tm, D_MODEL), lambda i: (i, 0)),
         pl.BlockSpec((tm, 1), lambda i: (i, 0))],
        [jax.ShapeDtypeStruct((T, IN_WIDTH), F32),
         jax.ShapeDtypeStruct((T, D_MODEL), BF16),
         jax.ShapeDtypeStruct((T, 1), F32)],
        [pltpu.VMEM(w.shape, w.dtype), pltpu.SemaphoreType.DMA], (x, g, w), carry)


def _ret_common(q_ref, k_ref, v_ref, cos_ref, sin_ref, mask_ref, rd_ref, sin_state):
    c = cos_ref[...]
    s = sin_ref[...]
    q = q_ref[...]
    q = q * c + pltpu.roll(q, HEAD_DIM // 2, 1) * s
    k = k_ref[...]
    k = (k * c + pltpu.roll(k, HEAD_DIM // 2, 1) * s) * (HEAD_DIM ** -0.5)
    qb = q.astype(BF16)
    kb = k.astype(BF16)
    vb = v_ref[...].astype(BF16)
    pm = (_dot_nt(qb, kb) * mask_ref[...]).astype(BF16)
    qd = (q * rd_ref[...]).astype(BF16)
    o = _dot(pm, vb) + _dot(qd, sin_state.astype(BF16))
    return q, k, qb, kb, vb, pm, qd, o


def _ret_specs(T, rev):
    nb = T // RET_BLOCK
    groups = RET_HEADS // RET_HPS
    wide = RET_HPS * HEAD_DIM
    blk = (lambda b: nb - 1 - b) if rev else (lambda b: b)
    col = lambda piece: (pl.BlockSpec((RET_BLOCK, wide), lambda h, b: (blk(b), piece * groups + h)), "lane")
    return dict(
        q=col(0), k=col(1), v=col(2), g=col(3),
        tab=(pl.BlockSpec((RET_BLOCK, HEAD_DIM), lambda h, b: (blk(b), 0)), None),
        mask=(pl.BlockSpec((RET_HPS, RET_BLOCK, RET_BLOCK), lambda h, b: (h, 0, 0)), "lead"),
        dec=(pl.BlockSpec((RET_HPS, RET_BLOCK, HEAD_DIM), lambda h, b: (h, 0, 0)), "lead"),
        gtb=(pl.BlockSpec((RET_HPS, 1, HEAD_DIM), lambda h, b: (h, 0, 0)), "lead"),
        gn=(pl.BlockSpec((1, wide), lambda h, b: (0, h)), "lane"),
        state=(pl.BlockSpec((RET_HPS, None, HEAD_DIM, HEAD_DIM), lambda h, b: (h, blk(b), 0, 0)), "lead"),
        rows=(pl.BlockSpec((RET_BLOCK, wide), lambda h, b: (blk(b), h)), "lane"),
        scratch=(pltpu.VMEM((RET_HPS, HEAD_DIM, HEAD_DIM), F32), "lead"),
    )


def _per_head(head_body, kinds):
    def body(*refs):
        for hh in range(RET_HPS):
            views = []
            for ref, kind in zip(refs, kinds):
                if kind == "lane":
                    views.append(ref.at[:, hh * HEAD_DIM:(hh + 1) * HEAD_DIM])
                elif kind == "lead":
                    views.append(ref.at[hh])
                else:
                    views.append(ref)
            head_body(*views)
    return body


def _ret_fwd(proj, cosf, sinf, mask, rowdec, kdec, gtb, gn, carry=None):
    T = proj.shape[0]
    nb = T // RET_BLOCK
    sp = _ret_specs(T, False)

    def body(q_ref, k_ref, v_ref, g_ref, cos_ref, sin_ref, mask_ref, rd_ref, kd_ref, gtb_ref, gn_ref,
             y_ref, sb_ref, st):
        @pl.when(pl.program_id(1) == 0)
        def _():
            st[...] = jnp.zeros_like(st)
        s_in = st[...]
        sb_ref[...] = s_in
        q, k, qb, kb, vb, pm, qd, o = _ret_common(q_ref, k_ref, v_ref, cos_ref, sin_ref, mask_ref, rd_ref, s_in)
        st[...] = gtb_ref[...] * s_in + _dot_tn((k * kd_ref[...]).astype(BF16), vb)
        mu = jnp.mean(o, axis=-1, keepdims=True)
        oc = o - mu
        n = oc * lax.rsqrt(jnp.mean(oc * oc, axis=-1, keepdims=True) + EPS)
        gt = g_ref[...]
        y_ref[...] = (gt * _sigmoid(gt) * (n * gn_ref[...])).astype(BF16)

    ins = [sp[n] for n in ("q", "k", "v", "g", "tab", "tab", "mask", "dec", "dec", "gtb", "gn")]
    outs = [sp["rows"], sp["state"]]
    return _pcall(
        _per_head(body, [kind for _, kind in ins + outs + [sp["scratch"]]]), "ret_fwd", (RET_HEADS // RET_HPS, nb),
        [s for s, _ in ins], [s for s, _ in outs],
        [jax.ShapeDtypeStruct((T, RET_WIDTH), BF16),
         jax.ShapeDtypeStruct((RET_HEADS, nb, HEAD_DIM, HEAD_DIM), F32)],
        [sp["scratch"][0]],
        (proj, proj, proj, proj, cosf, sinf, mask, rowdec, kdec, gtb, gn), carry)


def _scan(re, im, ar, ai, reverse):
    n = re.shape[0]
    row = lax.broadcasted_iota(jnp.int32, re.shape, 0)
    s = 1
    while s < n:
        if reverse:
            keep = row < n - s
            sr = jnp.where(keep, pltpu.roll(re, n - s, 0), 0.0)
            si = jnp.where(keep, pltpu.roll(im, n - s, 0), 0.0)
        else:
            keep = row >= s
            sr = jnp.where(keep, pltpu.roll(re, s, 0), 0.0)
            si = jnp.where(keep, pltpu.roll(im, s, 0), 0.0)
        re, im = re + ar * sr - ai * si, im + ar * si + ai * sr
        ar, ai = ar * ar - ai * ai, 2.0 * ar * ai
        s *= 2
    return re, im


S5_STATE_TILE = (S5_TILE, S5_LANES)


def _step_major_permutation():
    r = jnp.arange(S5_TILE)
    t_of_row = (r % S5_CHUNKS) * S5_STEPS + r // S5_CHUNKS
    return (t_of_row[:, None] == r[None, :]).astype(BF16)


def _permute_rows_f32(pm, x):
    hi = x.astype(BF16)
    rest = x - hi.astype(F32)
    mid = rest.astype(BF16)
    lo = (rest - mid.astype(F32)).astype(BF16)
    return _dot(pm, hi) + _dot(pm, mid) + _dot(pm, lo)


def _step_get(ref, j):
    return ref[j * S5_CHUNKS:(j + 1) * S5_CHUNKS, :]


def _step_set(ref, j, val):
    ref[j * S5_CHUNKS:(j + 1) * S5_CHUNKS, :] = val


def _tile_get(ref):
    return ref[...]


def _tile_set(ref, val):
    ref[...] = val


def _fill_power_table(ptab, lr, li):
    shape = (S5_CHUNKS, S5_LANES)
    lrb = jnp.broadcast_to(lr, shape)
    lib = jnp.broadcast_to(li, shape)
    pr, pi_ = lrb, lib
    for j in range(S5_STEPS):
        ptab[0, j * S5_CHUNKS:(j + 1) * S5_CHUNKS, :] = pr
        ptab[1, j * S5_CHUNKS:(j + 1) * S5_CHUNKS, :] = pi_
        pr, pi_ = lrb * pr - lib * pi_, lrb * pi_ + lib * pr


def _chunk_scans(xr, xi, lr, li, reverse):
    shape = (S5_CHUNKS, S5_LANES)
    lrb = jnp.broadcast_to(lr, shape)
    lib = jnp.broadcast_to(li, shape)
    sr = si = None
    for j in (range(S5_STEPS - 1, -1, -1) if reverse else range(S5_STEPS)):
        vr = _step_get(xr, j)
        vi = _step_get(xi, j)
        if sr is not None:
            vr, vi = vr + lrb * sr - lib * si, vi + lrb * si + lib * sr
            _step_set(xr, j, vr)
            _step_set(xi, j, vi)
        sr, si = vr, vi
    return sr, si


def _entering_states(zr, zi, cr, ci, ar, ai, reverse):
    shape = (S5_CHUNKS, S5_LANES)
    row = lax.broadcasted_iota(jnp.int32, shape, 0)
    if reverse:
        edge, shift = row == S5_CHUNKS - 1, S5_CHUNKS - 1
    else:
        edge, shift = row == 0, 1
    wr = jnp.where(edge, jnp.broadcast_to(cr, shape), pltpu.roll(zr, shift, 0))
    wi = jnp.where(edge, jnp.broadcast_to(ci, shape), pltpu.roll(zi, shift, 0))
    return _scan(wr, wi, ar, ai, reverse)


def _table_rows(ptab, j, conj):
    pr = ptab[0, j * S5_CHUNKS:(j + 1) * S5_CHUNKS, :]
    pi_ = ptab[1, j * S5_CHUNKS:(j + 1) * S5_CHUNKS, :]
    return pr, (-pi_ if conj else pi_)


def _s5_forward_states(xr, xi, lr, li, cr, ci, ptab):
    zr, zi = _chunk_scans(xr, xi, lr, li, False)
    ar, ai = _table_rows(ptab, S5_STEPS - 1, False)
    er, ei = _entering_states(zr, zi, cr, ci, ar, ai, False)
    for j in range(S5_STEPS):
        pr, pi_ = _table_rows(ptab, j, False)
        _step_set(xr, j, _step_get(xr, j) + pr * er - pi_ * ei)
        _step_set(xi, j, _step_get(xi, j) + pr * ei + pi_ * er)
    last = S5_CHUNKS - 1
    end_r = (ar * er - ai * ei + zr)[last:last + 1, :]
    end_i = (ar * ei + ai * er + zi)[last:last + 1, :]
    return er, ei, end_r, end_i


def _s5_specs(T, rev):
    nt = T // S5_TILE
    tt = (lambda t: nt - 1 - t) if rev else (lambda t: t)
    return dict(
        u=pl.BlockSpec((S5_TILE, LANE), lambda b, t: (tt(t), 4 * RET_HEADS + b)),
        rows=pl.BlockSpec((S5_TILE, LANE), lambda b, t: (tt(t), b)),
        to_state=pl.BlockSpec((None, LANE, S5_LANES), lambda b, t: (b, 0, 0)),
        from_state=pl.BlockSpec((None, S5_LANES, LANE), lambda b, t: (b, 0, 0)),
        lam=pl.BlockSpec((None, 2, S5_LANES), lambda b, t: (b, 0, 0)),
        d=pl.BlockSpec((1, LANE), lambda b, t: (0, b)),
        perm=pl.BlockSpec((S5_TILE, S5_TILE), lambda b, t: (0, 0)),
        bound=pl.BlockSpec((None, None, 2, S5_LANES), lambda b, t: (b, tt(t), 0, 0)),
    )


def _s5_fwd(proj, pm, pm_t, bre, bim, cre_t, cim_t, lam, d, carry=None):
    T = proj.shape[0]
    nt = T // S5_TILE
    sp = _s5_specs(T, False)

    def body(u_ref, pm_ref, pmt_ref, bre_ref, bim_ref, cre_ref, cim_ref, lam_ref, d_ref, y_ref, bound_ref,
             carry, ptab, xr, xi):
        lr = lam_ref[0:1, :]
        li = lam_ref[1:2, :]

        @pl.when(pl.program_id(1) == 0)
        def _():
            carry[...] = jnp.zeros_like(carry)
            _fill_power_table(ptab, lr, li)

        u = _permute_rows_f32(pm_ref[...], u_ref[...])
        ub = u.astype(BF16)
        _tile_set(xr, _dot(ub, bre_ref[...]))
        _tile_set(xi, _dot(ub, bim_ref[...]))
        bound_ref[...] = carry[...]
        _, _, end_r, end_i = _s5_forward_states(xr, xi, lr, li, carry[0:1, :], carry[1:2, :], ptab)
        carry[0:1, :] = end_r
        carry[1:2, :] = end_i
        y = (_dot(_tile_get(xr).astype(BF16), cre_ref[...]) - _dot(_tile_get(xi).astype(BF16), cim_ref[...])
             + d_ref[...] * u)
        y_ref[...] = _permute_rows_f32(pmt_ref[...], y)

    state = pltpu.VMEM(S5_STATE_TILE, F32)
    return _pcall(
        body, "s5_fwd", (S5_NBLK, nt),
        [sp["u"], sp["perm"], sp["perm"], sp["to_state"], sp["to_state"], sp["from_state"],
         sp["from_state"], sp["lam"], sp["d"]],
        [sp["rows"], sp["bound"]],
        [jax.ShapeDtypeStruct((T, SSM_WIDTH), F32),
         jax.ShapeDtypeStruct((S5_NBLK, nt, 2, S5_LANES), F32)],
        [pltpu.VMEM((2, S5_LANES), F32), pltpu.VMEM((2, S5_TILE, S5_LANES), F32), state, state],
        (proj, pm, pm_t, bre, bim, cre_t, cim_t, lam, d), carry)


def _glu_fwd(y, w, b, og, tm):
    T = y.shape[0]

    def body(y_ref, w_ref, b_ref, og_ref, z_ref, o_ref, r_ref):
        y1 = _gelu(y_ref[...])
        z = _dot(y1.astype(BF16), w_ref[...]) + b_ref[...]
        y2 = y1 * _sigmoid(z)
        r = lax.rsqrt(jnp.mean(y2 * y2, axis=-1, keepdims=True) + EPS)
        z_ref[...] = z
        o_ref[...] = (y2 * r * og_ref[...]).astype(BF16)
        r_ref[...] = r

    row = pl.BlockSpec((tm, SSM_WIDTH), lambda i: (i, 0))
    vec = pl.BlockSpec((1, SSM_WIDTH), lambda i: (0, 0))
    return pl.pallas_call(
        body, name="glu_fwd", grid=(T // tm,),
        in_specs=[row, pl.BlockSpec((SSM_WIDTH, SSM_WIDTH), lambda i: (0, 0)), vec, vec],
        out_specs=[row, row, pl.BlockSpec((tm, 1), lambda i: (i, 0))],
        out_shape=[jax.ShapeDtypeStruct((T, SSM_WIDTH), F32), jax.ShapeDtypeStruct((T, SSM_WIDTH), BF16),
                   jax.ShapeDtypeStruct((T, 1), F32)],
        compiler_params=_params(1),
    )(y, w, b, og)


def _out_proj_fwd(x, y_ret, y_ssm, w, g, tm):
    T = x.shape[0]

    def body(x_ref, a_ref, b_ref, w_ref, g_ref, x2_ref, h_ref, r_ref):
        x2 = x_ref[...] + _dot(a_ref[...], w_ref[0:RET_WIDTH, :]) + _dot(b_ref[...], w_ref[RET_WIDTH:D_MODEL, :])
        r = lax.rsqrt(jnp.mean(x2 * x2, axis=-1, keepdims=True) + EPS)
        x2_ref[...] = x2
        h_ref[...] = (x2 * r * g_ref[...]).astype(BF16)
        r_ref[...] = r

    full = pl.BlockSpec((tm, D_MODEL), lambda i: (i, 0))
    half = pl.BlockSpec((tm, RET_WIDTH), lambda i: (i, 0))
    return pl.pallas_call(
        body, name="out_proj_fwd", grid=(T // tm,),
        in_specs=[full, half, half, pl.BlockSpec((D_MODEL, D_MODEL), lambda i: (0, 0)),
                  pl.BlockSpec((1, D_MODEL), lambda i: (0, 0))],
        out_specs=[full, full, pl.BlockSpec((tm, 1), lambda i: (i, 0))],
        out_shape=[jax.ShapeDtypeStruct((T, D_MODEL), F32), jax.ShapeDtypeStruct((T, D_MODEL), BF16),
                   jax.ShapeDtypeStruct((T, 1), F32)],
        compiler_params=_params(1),
    )(x, y_ret, y_ssm, w, g)


def _ffn_up(h, wg, wu, tm, carry=None):
    T = h.shape[0]

    def body(h_ref, wg_ref, wu_ref, a_ref, b_ref, f_ref):
        hb = h_ref[...]
        a = _dot(hb, wg_ref[...])
        b = _dot(hb, wu_ref[...])
        a_ref[...] = a.astype(BF16)
        b_ref[...] = b.astype(BF16)
        f_ref[...] = (a * _sigmoid(a) * b).astype(BF16)

    wspec = pl.BlockSpec((None, D_MODEL, FF_BLK), lambda j, i: (j, 0, 0))
    ospec = pl.BlockSpec((None, tm, FF_BLK), lambda j, i: (j, i, 0))
    oshape = jax.ShapeDtypeStruct((N_DEV, T, FF_BLK), BF16)
    return _pcall(
        body, "ffn_up", (N_DEV, T // tm),
        [pl.BlockSpec((tm, D_MODEL), lambda j, i: (i, 0)), wspec, wspec],
        [ospec, ospec, ospec], [oshape, oshape, oshape], [], (h, wg, wu), carry)


def _ffn_down_loss(f, wd, x2, tgt, g, tm):
    T = x2.shape[0]

    def body(f_ref, w_hbm, x2_ref, t_ref, g_ref, dx_ref, dxb_ref, loss_ref, dg_ref, w_ref, sem):
        i = pl.program_id(0)

        @pl.when(i == 0)
        def _():
            _load_resident(w_hbm, w_ref, sem)
            loss_ref[...] = jnp.zeros_like(loss_ref)
            dg_ref[...] = jnp.zeros_like(dg_ref)

        x3 = x2_ref[...]
        for k in range(N_DEV):
            x3 = x3 + _dot(f_ref[k], w_ref[k])
        gv = g_ref[...]
        r = lax.rsqrt(jnp.mean(x3 * x3, axis=-1, keepdims=True) + EPS)
        err = x3 * r * gv - t_ref[...]
        tile_loss = 0.5 * jnp.sum(jnp.mean(err * err, axis=-1, keepdims=True), axis=0, keepdims=True)
        dx, dgt = _rms_bwd(err * (1.0 / D_MODEL), x3, r, gv)
        dx_ref[...] = dx
        dxb_ref[...] = dx.astype(BF16)
        loss_ref[...] += jnp.broadcast_to(tile_loss, loss_ref.shape)
        dg_ref[...] += jnp.sum(dgt, axis=0, keepdims=True)

    full = pl.BlockSpec((tm, D_MODEL), lambda i: (i, 0))
    vec = pl.BlockSpec((1, D_MODEL), lambda i: (0, 0))
    return pl.pallas_call(
        body, name="ffn_down_loss", grid=(T // tm,),
        in_specs=[pl.BlockSpec((N_DEV, tm, FF_BLK), lambda i: (0, i, 0)), ANY_SPEC, full, full, vec],
        out_specs=[full, full, pl.BlockSpec((8, LANE), lambda i: (0, 0)), vec],
        out_shape=[jax.ShapeDtypeStruct((T, D_MODEL), F32), jax.ShapeDtypeStruct((T, D_MODEL), BF16),
                   jax.ShapeDtypeStruct((8, LANE), F32), jax.ShapeDtypeStruct((1, D_MODEL), F32)],
        scratch_shapes=[pltpu.VMEM(wd.shape, wd.dtype), pltpu.SemaphoreType.DMA],
        compiler_params=_params(1),
    )(f, wd, x2, tgt, g)


def _ffn_bwd_act(dxb, wd, a, b, tm):
    T = dxb.shape[0]

    def body(dx_ref, w_ref, a_ref, b_ref, da_ref, db_ref):
        df = _dot_nt(dx_ref[...], w_ref[...])
        a = a_ref[...].astype(F32)
        b = b_ref[...].astype(F32)
        sg = _sigmoid(a)
        da_ref[...] = (df * b * sg * (1.0 + a * (1.0 - sg))).astype(BF16)
        db_ref[...] = (df * a * sg).astype(BF16)

    blk = pl.BlockSpec((None, tm, FF_BLK), lambda j, i: (j, i, 0))
    oshape = jax.ShapeDtypeStruct((N_DEV, T, FF_BLK), BF16)
    return pl.pallas_call(
        body, name="ffn_bwd_act", grid=(N_DEV, T // tm),
        in_specs=[pl.BlockSpec((tm, D_MODEL), lambda j, i: (i, 0)),
                  pl.BlockSpec((None, FF_BLK, D_MODEL), lambda j, i: (j, 0, 0)), blk, blk],
        out_specs=[blk, blk], out_shape=[oshape, oshape],
        compiler_params=_params(2),
    )(dxb, wd, a, b)


def _ffn_bwd_in(da, db, wg, wu, tm, carry=None):
    T = da.shape[1]

    def body(da_ref, db_ref, wg_ref, wu_ref, dh_ref):
        part = _dot_nt(da_ref[...], wg_ref[...]) + _dot_nt(db_ref[...], wu_ref[...])

        @pl.when(pl.program_id(1) == 0)
        def _():
            dh_ref[...] = part

        @pl.when(pl.program_id(1) > 0)
        def _():
            dh_ref[...] += part

    ablk = pl.BlockSpec((None, tm, FF_BLK), lambda i, k: (k, i, 0))
    wblk = pl.BlockSpec((None, D_MODEL, FF_BLK), lambda i, k: (k, 0, 0))
    return _pcall(
        body, "ffn_bwd_in", (T // tm, N_DEV), [ablk, ablk, wblk, wblk],
        [pl.BlockSpec((tm, D_MODEL), lambda i, k: (i, 0))], [jax.ShapeDtypeStruct((T, D_MODEL), F32)],
        [], (da, db, wg, wu), carry)


def _ffn_wgrad_up(h, da, db, tk, carry=None):
    T = h.shape[0]
    nk = T // tk

    def body(h_ref, da_ref, db_ref, g_ref, u_ref, accg, accu):
        k = pl.program_id(1)

        @pl.when(k == 0)
        def _():
            accg[...] = jnp.zeros_like(accg)
            accu[...] = jnp.zeros_like(accu)

        hb = h_ref[...]
        accg[...] += _dot_tn(hb, da_ref[...])
        accu[...] += _dot_tn(hb, db_ref[...])

        @pl.when(k == nk - 1)
        def _():
            g_ref[...] = accg[...].astype(BF16)
            u_ref[...] = accu[...].astype(BF16)

    blk = pl.BlockSpec((None, tk, FF_BLK), lambda j, k: (j, k, 0))
    ospec = pl.BlockSpec((None, D_MODEL, FF_BLK), lambda j, k: (j, 0, 0))
    oshape = jax.ShapeDtypeStruct((N_DEV, D_MODEL, FF_BLK), BF16)
    return _pcall(
        body, "ffn_wgrad_up", (N_DEV, nk),
        [pl.BlockSpec((tk, D_MODEL), lambda j, k: (k, 0)), blk, blk],
        [ospec, ospec], [oshape, oshape],
        [pltpu.VMEM((D_MODEL, FF_BLK), F32), pltpu.VMEM((D_MODEL, FF_BLK), F32)], (h, da, db), carry)


def _ffn_wgrad_down(f, dxb, tk):
    T = dxb.shape[0]
    nk = T // tk

    def body(f_ref, dx_ref, o_ref, acc):
        k = pl.program_id(1)

        @pl.when(k == 0)
        def _():
            acc[...] = jnp.zeros_like(acc)

        acc[...] += _dot_tn(f_ref[...], dx_ref[...])

        @pl.when(k == nk - 1)
        def _():
            o_ref[...] = acc[...].astype(BF16)

    return pl.pallas_call(
        body, name="ffn_wgrad_down", grid=(N_DEV, nk),
        in_specs=[pl.BlockSpec((None, tk, FF_BLK), lambda j, k: (j, k, 0)),
                  pl.BlockSpec((tk, D_MODEL), lambda j, k: (k, 0))],
        out_specs=pl.BlockSpec((None, FF_BLK, D_MODEL), lambda j, k: (j, 0, 0)),
        out_shape=jax.ShapeDtypeStruct((N_DEV, FF_BLK, D_MODEL), BF16),
        scratch_shapes=[pltpu.VMEM((FF_BLK, D_MODEL), F32)],
        compiler_params=_params(2),
    )(f, dxb)


def _out_proj_bwd(dh2, x2, r2, g, dx3, w, tm):
    T = x2.shape[0]

    def body(dh_ref, x_ref, r_ref, g_ref, dx3_ref, w_ref, dx_ref, dxb_ref, dg_ref, a_ref, b_ref):
        @pl.when(pl.program_id(0) == 0)
        def _():
            dg_ref[...] = jnp.zeros_like(dg_ref)

        dxn, dgt = _rms_bwd(dh_ref[...], x_ref[...], r_ref[...], g_ref[...])
        dx = dx3_ref[...] + dxn
        dxv = dx.astype(BF16)
        dx_ref[...] = dx
        dxb_ref[...] = dxv
        dg_ref[...] += jnp.sum(dgt, axis=0, keepdims=True)
        a_ref[...] = _dot_nt(dxv, w_ref[0:RET_WIDTH, :])
        b_ref[...] = _dot_nt(dxv, w_ref[RET_WIDTH:D_MODEL, :])

    full = pl.BlockSpec((tm, D_MODEL), lambda i: (i, 0))
    vec = pl.BlockSpec((1, D_MODEL), lambda i: (0, 0))
    half = pl.BlockSpec((tm, RET_WIDTH), lambda i: (i, 0))
    hshape = jax.ShapeDtypeStruct((T, RET_WIDTH), F32)
    return pl.pallas_call(
        body, name="out_proj_bwd", grid=(T // tm,),
        in_specs=[full, full, pl.BlockSpec((tm, 1), lambda i: (i, 0)), vec, full,
                  pl.BlockSpec((D_MODEL, D_MODEL), lambda i: (0, 0))],
        out_specs=[full, full, vec, half, half],
        out_shape=[jax.ShapeDtypeStruct((T, D_MODEL), F32), jax.ShapeDtypeStruct((T, D_MODEL), BF16),
                   jax.ShapeDtypeStruct((1, D_MODEL), F32), hshape, hshape],
        compiler_params=_params(1),
    )(dh2, x2, r2, g, dx3, w)


def _wgrad_rows(name, a, b, tk):
    T, M = a.shape
    N = b.shape[1]
    nk = T // tk

    def body(a_ref, b_ref, o_ref, acc):
        k = pl.program_id(0)

        @pl.when(k == 0)
        def _():
            acc[...] = jnp.zeros_like(acc)

        acc[...] += _dot_tn(a_ref[...], b_ref[...])

        @pl.when(k == nk - 1)
        def _():
            o_ref[...] = acc[...].astype(BF16)

    return pl.pallas_call(
        body, name=name, grid=(nk,),
        in_specs=[pl.BlockSpec((tk, M), lambda k: (k, 0)), pl.BlockSpec((tk, N), lambda k: (k, 0))],
        out_specs=pl.BlockSpec((M, N), lambda k: (0, 0)),
        out_shape=jax.ShapeDtypeStruct((M, N), BF16),
        scratch_shapes=[pltpu.VMEM((M, N), F32)],
        compiler_params=_params(1),
    )(a, b)


def _glu_bwd(y, z, r, dyo, w, og, tm):
    T = y.shape[0]

    def body(y_ref, z_ref, r_ref, d_ref, w_ref, og_ref, dy_ref, dw_ref, db_ref, dog_ref):
        @pl.when(pl.program_id(0) == 0)
        def _():
            dw_ref[...] = jnp.zeros_like(dw_ref)
            db_ref[...] = jnp.zeros_like(db_ref)
            dog_ref[...] = jnp.zeros_like(dog_ref)

        y1, g1 = _gelu_and_grad(y_ref[...])
        sg = _sigmoid(z_ref[...])
        y2 = y1 * sg
        dy2, dogt = _rms_bwd(d_ref[...], y2, r_ref[...], og_ref[...])
        dog_ref[...] += jnp.sum(dogt, axis=0, keepdims=True)
        dz = dy2 * y1 * sg * (1.0 - sg)
        db_ref[...] += jnp.sum(dz, axis=0, keepdims=True)
        dzb = dz.astype(BF16)
        dw_ref[...] += _dot_tn(y1.astype(BF16), dzb)
        dy_ref[...] = (dy2 * sg + _dot_nt(dzb, w_ref[...])) * g1

    row = pl.BlockSpec((tm, SSM_WIDTH), lambda i: (i, 0))
    vec = pl.BlockSpec((1, SSM_WIDTH), lambda i: (0, 0))
    sq = pl.BlockSpec((SSM_WIDTH, SSM_WIDTH), lambda i: (0, 0))
    return pl.pallas_call(
        body, name="glu_bwd", grid=(T // tm,),
        in_specs=[row, row, pl.BlockSpec((tm, 1), lambda i: (i, 0)), row, sq, vec],
        out_specs=[row, sq, vec, vec],
        out_shape=[jax.ShapeDtypeStruct((T, SSM_WIDTH), F32), jax.ShapeDtypeStruct((SSM_WIDTH, SSM_WIDTH), F32),
                   jax.ShapeDtypeStruct((1, SSM_WIDTH), F32), jax.ShapeDtypeStruct((1, SSM_WIDTH), F32)],
        compiler_params=_params(1),
    )(y, z, r, dyo, w, og)


def _s5_bwd(proj, dy, bound, pm, pm_t, bre, bim, bre_t, bim_t, cre, cim, lam, d):
    T = proj.shape[0]
    nt = T // S5_TILE
    sp = _s5_specs(T, True)

    def body(u_ref, dy_ref, bound_ref, pm_ref, pmt_ref, bre_ref, bim_ref, bret_ref, bimt_ref, cre_ref, cim_ref,
             lam_ref, d_ref,
             du_ref, dbre_ref, dbim_ref, dcre_ref, dcim_ref, dlam_ref, dd_ref, carry, ptab, sr, si, gr, gi):
        lr = lam_ref[0:1, :]
        li = lam_ref[1:2, :]

        @pl.when(pl.program_id(1) == 0)
        def _():
            carry[...] = jnp.zeros_like(carry)
            _fill_power_table(ptab, lr, li)
            for ref in (dbre_ref, dbim_ref, dcre_ref, dcim_ref, dlam_ref, dd_ref):
                ref[...] = jnp.zeros_like(ref)

        u = _permute_rows_f32(pm_ref[...], u_ref[...])
        ub = u.astype(BF16)
        dyv = _permute_rows_f32(pm_ref[...], dy_ref[...])
        dyb = dyv.astype(BF16)
        _tile_set(sr, _dot(ub, bre_ref[...]))
        _tile_set(si, _dot(ub, bim_ref[...]))
        er, ei, _, _ = _s5_forward_states(sr, si, lr, li, bound_ref[0:1, :], bound_ref[1:2, :], ptab)
        _tile_set(gr, _dot(dyb, cre_ref[...]))
        _tile_set(gi, -_dot(dyb, cim_ref[...]))
        zr, zi = _chunk_scans(gr, gi, lr, -li, True)
        ar, ai = _table_rows(ptab, S5_STEPS - 1, True)
        fr, fi = _entering_states(zr, zi, carry[0:1, :], carry[1:2, :], ar, ai, True)
        acc_r = jnp.zeros((S5_CHUNKS, S5_LANES), F32)
        acc_i = jnp.zeros((S5_CHUNKS, S5_LANES), F32)
        for j in range(S5_STEPS):
            qr, qi = _table_rows(ptab, S5_STEPS - 1 - j, True)
            g_r = _step_get(gr, j) + qr * fr - qi * fi
            g_i = _step_get(gi, j) + qr * fi + qi * fr
            _step_set(gr, j, g_r)
            _step_set(gi, j, g_i)
            p_r, p_i = (er, ei) if j == 0 else (_step_get(sr, j - 1), _step_get(si, j - 1))
            acc_r += g_r * p_r + g_i * p_i
            acc_i += g_i * p_r - g_r * p_i
        dlam_ref[0:1, :] += jnp.sum(acc_r, axis=0, keepdims=True)
        dlam_ref[1:2, :] += jnp.sum(acc_i, axis=0, keepdims=True)
        g_all_r = _tile_get(gr)
        g_all_i = _tile_get(gi)
        carry[0:1, :] = g_all_r[0:1, :]
        carry[1:2, :] = g_all_i[0:1, :]
        grb = g_all_r.astype(BF16)
        gib = g_all_i.astype(BF16)
        du = (_dot(grb, bret_ref[...]) + _dot(gib, bimt_ref[...]) + d_ref[...] * dyv).astype(BF16)
        du_ref[...] = _dot(pmt_ref[...], du).astype(BF16)
        dbre_ref[...] += _dot_tn(grb, ub)
        dbim_ref[...] += _dot_tn(gib, ub)
        dcre_ref[...] += _dot_tn(dyb, _tile_get(sr).astype(BF16))
        dcim_ref[...] -= _dot_tn(dyb, _tile_get(si).astype(BF16))
        dd_ref[...] += jnp.sum(dyv * u, axis=0, keepdims=True)

    acc_ts = pl.BlockSpec((None, S5_LANES, LANE), lambda b, t: (b, 0, 0))
    acc_fs = pl.BlockSpec((None, LANE, S5_LANES), lambda b, t: (b, 0, 0))
    return pl.pallas_call(
        body, name="s5_bwd", grid=(S5_NBLK, nt),
        in_specs=[sp["u"], sp["rows"], sp["bound"], sp["perm"], sp["perm"], sp["to_state"], sp["to_state"],
                  sp["from_state"], sp["from_state"], sp["to_state"], sp["to_state"], sp["lam"], sp["d"]],
        out_specs=[sp["rows"], acc_ts, acc_ts, acc_fs, acc_fs, sp["lam"], sp["d"]],
        out_shape=[jax.ShapeDtypeStruct((T, SSM_WIDTH), BF16),
                   jax.ShapeDtypeStruct((S5_NBLK, S5_LANES, LANE), F32),
                   jax.ShapeDtypeStruct((S5_NBLK, S5_LANES, LANE), F32),
                   jax.ShapeDtypeStruct((S5_NBLK, LANE, S5_LANES), F32),
                   jax.ShapeDtypeStruct((S5_NBLK, LANE, S5_LANES), F32),
                   jax.ShapeDtypeStruct((S5_NBLK, 2, S5_LANES), F32),
                   jax.ShapeDtypeStruct((1, SSM_WIDTH), F32)],
        scratch_shapes=[pltpu.VMEM((2, S5_LANES), F32), pltpu.VMEM((2, S5_TILE, S5_LANES), F32)]
        + [pltpu.VMEM(S5_STATE_TILE, F32)] * 4,
        compiler_params=_params(2),
    )(proj, dy, bound, pm, pm_t, bre, bim, bre_t, bim_t, cre, cim, lam, d)


def _ret_bwd(proj, cosf, sinf, mask, rowdec, kdec, gtb, gn, sblk, dyr):
    T = proj.shape[0]
    nb = T // RET_BLOCK
    sp = _ret_specs(T, True)

    def body(q_ref, k_ref, v_ref, g_ref, cos_ref, sin_ref, mask_ref, rd_ref, kd_ref, gtb_ref, gn_ref, sb_ref, dy_ref,
             dq_ref, dk_ref, dv_ref, dg_ref, dgn_ref, dst):
        @pl.when(pl.program_id(1) == 0)
        def _():
            dst[...] = jnp.zeros_like(dst)
            dgn_ref[...] = jnp.zeros_like(dgn_ref)

        s_in = sb_ref[...]
        q, k, qb, kb, vb, pm, qd, o = _ret_common(q_ref, k_ref, v_ref, cos_ref, sin_ref, mask_ref, rd_ref, s_in)
        mu = jnp.mean(o, axis=-1, keepdims=True)
        oc = o - mu
        rstd = lax.rsqrt(jnp.mean(oc * oc, axis=-1, keepdims=True) + EPS)
        n = oc * rstd
        gt = g_ref[...]
        sg = _sigmoid(gt)
        sil = gt * sg
        gnv = gn_ref[...]
        dyv = dy_ref[...]
        dg_ref[...] = (dyv * (n * gnv) * (sg * (1.0 + gt * (1.0 - sg)))).astype(BF16)
        dgn_ref[...] += jnp.sum(dyv * sil * n, axis=0, keepdims=True)
        dn = dyv * sil * gnv
        do = rstd * (dn - jnp.mean(dn, axis=-1, keepdims=True) - n * jnp.mean(dn * n, axis=-1, keepdims=True))
        dob = do.astype(BF16)
        ds = dst[...]
        dsb = ds.astype(BF16)
        kd = kd_ref[...]
        rd = rd_ref[...]
        dv_ref[...] = (_dot_tn(pm, dob) + _dot((k * kd).astype(BF16), dsb)).astype(BF16)
        dpb = (_dot_nt(dob, vb) * mask_ref[...]).astype(BF16)
        dq = _dot(dpb, kb) + _dot_nt(dob, s_in.astype(BF16)) * rd
        dk = (_dot_tn(dpb, qb) + _dot_nt(vb, dsb) * kd) * (HEAD_DIM ** -0.5)
        dst[...] = gtb_ref[...] * ds + _dot_tn(qd, dob)
        c = cos_ref[...]
        s = sin_ref[...]
        dq_ref[...] = (dq * c + pltpu.roll(dq * s, HEAD_DIM // 2, 1)).astype(BF16)
        dk_ref[...] = (dk * c + pltpu.roll(dk * s, HEAD_DIM // 2, 1)).astype(BF16)

    oshape = jax.ShapeDtypeStruct((T, RET_WIDTH), BF16)
    ins = [sp[n] for n in ("q", "k", "v", "g", "tab", "tab", "mask", "dec", "dec", "gtb", "gn", "state", "rows")]
    outs = [sp["rows"], sp["rows"], sp["rows"], sp["rows"], sp["gn"]]
    return pl.pallas_call(
        _per_head(body, [kind for _, kind in ins + outs + [sp["scratch"]]]), name="ret_bwd",
        grid=(RET_HEADS // RET_HPS, nb), in_specs=[s for s, _ in ins], out_specs=[s for s, _ in outs],
        out_shape=[oshape, oshape, oshape, oshape, jax.ShapeDtypeStruct((1, RET_WIDTH), F32)],
        scratch_shapes=[sp["scratch"][0]],
        compiler_params=_params(2),
    )(proj, proj, proj, proj, cosf, sinf, mask, rowdec, kdec, gtb, gn, sblk, dyr)


def _in_proj_bwd(dproj, w, x, r1, g, dx2, tm, carry=None):
    T = x.shape[0]

    def body(dp_ref, w_hbm, x_ref, r_ref, g_ref, dx2_ref, gx_ref, dg_ref, w_ref, sem):
        @pl.when(pl.program_id(0) == 0)
        def _():
            _load_resident(w_hbm, w_ref, sem)
            dg_ref[...] = jnp.zeros_like(dg_ref)

        dh = _dot_nt(dp_ref[:, 0:WIN_BLK], w_ref[0])
        for k in range(1, N_DEV):
            dh = dh + _dot_nt(dp_ref[:, k * WIN_BLK:(k + 1) * WIN_BLK], w_ref[k])
        dxn, dgt = _rms_bwd(dh, x_ref[...], r_ref[...], g_ref[...])
        gx_ref[...] = dx2_ref[...] + dxn
        dg_ref[...] += jnp.sum(dgt, axis=0, keepdims=True)

    full = pl.BlockSpec((tm, D_MODEL), lambda i: (i, 0))
    vec = pl.BlockSpec((1, D_MODEL), lambda i: (0, 0))
    return _pcall(
        body, "in_proj_bwd", (T // tm,),
        [pl.BlockSpec((tm, IN_WIDTH), lambda i: (i, 0)), ANY_SPEC,
         full, pl.BlockSpec((tm, 1), lambda i: (i, 0)), vec, full],
        [full, vec],
        [jax.ShapeDtypeStruct((T, D_MODEL), F32), jax.ShapeDtypeStruct((1, D_MODEL), F32)],
        [pltpu.VMEM(w.shape, w.dtype), pltpu.SemaphoreType.DMA], (dproj, w, x, r1, g, dx2), carry)


def _in_proj_wgrad(h, dproj, tk, carry=None):
    T = h.shape[0]
    nk = T // tk

    def body(h_ref, dp_ref, o_ref, acc):
        k = pl.program_id(1)

        @pl.when(k == 0)
        def _():
            acc[...] = jnp.zeros_like(acc)

        acc[...] += _dot_tn(h_ref[...], dp_ref[...])

        @pl.when(k == nk - 1)
        def _():
            o_ref[...] = acc[...].astype(BF16)

    return _pcall(
        body, "in_proj_wgrad", (N_DEV, nk),
        [pl.BlockSpec((tk, D_MODEL), lambda j, k: (k, 0)), pl.BlockSpec((tk, WIN_BLK), lambda j, k: (k, j))],
        [pl.BlockSpec((None, D_MODEL, WIN_BLK), lambda j, k: (j, 0, 0))],
        [jax.ShapeDtypeStruct((N_DEV, D_MODEL, WIN_BLK), BF16)],
        [pltpu.VMEM((D_MODEL, WIN_BLK), F32)], (h, dproj), carry)


def _rope_tables(T):
    half = HEAD_DIM // 2
    freqs = ROPE_BASE ** (-jnp.arange(half, dtype=F32) / half)
    ang = jnp.arange(T, dtype=F32)[:, None] * freqs[None, :]
    c = jnp.cos(ang)
    s = jnp.sin(ang)
    return jnp.concatenate([c, c], axis=1), jnp.concatenate([-s, s], axis=1)


def _retention_tables():
    hh = jnp.arange(RET_HEADS, dtype=F32)
    log_g = jnp.log1p(-(2.0 ** (-5.0 - hh)))[:, None, None]
    i = jnp.arange(RET_BLOCK)
    ci = (i // CHUNK)[:, None]
    cj = (i // CHUNK)[None, :]
    diff = (i[:, None] - i[None, :]).astype(F32)
    expo = jnp.where(ci == cj, jnp.abs(diff), diff)
    mask = jnp.where((cj <= ci)[None], jnp.exp(log_g * expo[None]), 0.0)
    r = jnp.arange(RET_BLOCK, dtype=F32)[None, :, None]
    ones = jnp.ones((1, 1, HEAD_DIM), F32)
    rowdec = jnp.exp(log_g * (r + 1.0)) * ones
    kdec = jnp.exp(log_g * (RET_BLOCK - 1.0 - r)) * ones
    gtb = jnp.exp(log_g * float(RET_BLOCK)) * ones
    return mask, rowdec, kdec, gtb


def _s5_discretise(a_re, a_im, log_dt, b_re, b_im):
    lam = lax.complex(a_re, a_im)
    dt = jnp.exp(log_dt)[:, None]
    lam_bar = jnp.exp(lam * dt)
    b_bar = ((lam_bar - 1.0) / lam)[..., None] * lax.complex(b_re, b_im)
    return jnp.real(lam_bar), jnp.imag(lam_bar), jnp.real(b_bar), jnp.imag(b_bar)


def _to_state_blockdiag(m):
    eye = jnp.eye(S5_GB, dtype=m.dtype)
    t = jnp.einsum("bgpc,gh->bgchp", m.reshape(S5_NBLK, S5_GB, SSM_STATE, SSM_GROUP), eye)
    return t.reshape(S5_NBLK, LANE, S5_LANES)


def _from_state_blockdiag(m):
    eye = jnp.eye(S5_GB, dtype=m.dtype)
    t = jnp.einsum("bgcp,gh->bgphc", m.reshape(S5_NBLK, S5_GB, SSM_GROUP, SSM_STATE), eye)
    return t.reshape(S5_NBLK, S5_LANES, LANE)


def _diag_of_state_major(acc):
    eye = jnp.eye(S5_GB, dtype=acc.dtype)
    t = acc.reshape(S5_NBLK, S5_GB, SSM_STATE, S5_GB, SSM_GROUP)
    return jnp.einsum("bgphc,gh->bgpc", t, eye).reshape(SSM_GROUPS, SSM_STATE, SSM_GROUP)


def _diag_of_channel_major(acc):
    eye = jnp.eye(S5_GB, dtype=acc.dtype)
    t = acc.reshape(S5_NBLK, S5_GB, SSM_GROUP, S5_GB, SSM_STATE)
    return jnp.einsum("bgchp,gh->bgcp", t, eye).reshape(SSM_GROUPS, SSM_GROUP, SSM_STATE)


SMALL_PARTIALS = (("ret_gn_g", 1024), ("lam_re", 4096), ("lam_im", 4096),
                  ("bbar_re", 65536), ("bbar_im", 65536), ("c_re", 65536), ("c_im", 65536),
                  ("ssm_d", 1024), ("b_glu", 1024), ("out_g", 1024), ("norm_ffn_g", 2048), ("norm_final_g", 2048))


def _forward_backward(x, tgt, shards, sm, tm=512):
    T = x.shape[0]
    cosf, sinf = _rope_tables(T)
    mask, rowdec, kdec, gtb = _retention_tables()
    lbr, lbi, bbr, bbi = _s5_discretise(sm["ssm_a_re"], sm["ssm_a_im"], sm["ssm_log_dt"], sm["ssm_b_re"],
                                        sm["ssm_b_im"])
    bre = _to_state_blockdiag(bbr).astype(BF16)
    bim = _to_state_blockdiag(bbi).astype(BF16)
    cre_t = _from_state_blockdiag(sm["ssm_c_re"]).astype(BF16)
    cim_t = _from_state_blockdiag(sm["ssm_c_im"]).astype(BF16)
    bre_t = jnp.swapaxes(bre, 1, 2)
    bim_t = jnp.swapaxes(bim, 1, 2)
    cre = jnp.swapaxes(cre_t, 1, 2)
    cim = jnp.swapaxes(cim_t, 1, 2)
    lam = jnp.stack([lbr.reshape(S5_NBLK, S5_LANES), lbi.reshape(S5_NBLK, S5_LANES)], axis=1)
    pm = _step_major_permutation()
    pm_t = pm.T
    row = lambda v: v.reshape(1, -1)
    g_mix, g_ffn, g_fin = row(sm["norm_mix_g"]), row(sm["norm_ffn_g"]), row(sm["norm_final_g"])
    gn, dsk, bglu, og = row(sm["ret_gn_g"]), row(sm["ssm_d"]), row(sm["ssm_b_glu"]), row(sm["ssm_out_g"])

    (w_in,) = _exchange_call("weight_gather", [shards["w_in"]], True)
    proj, h1, r1, w_glu, w_out = _in_proj_fwd(
        x, g_mix, w_in, 256, _Exchange([shards["ssm_w_glu"], shards["w_out"]], True))
    w_glu = w_glu.reshape(SSM_WIDTH, SSM_WIDTH)
    w_out = w_out.reshape(D_MODEL, D_MODEL)
    y_ret, sblk, w_gate = _ret_fwd(proj, cosf, sinf, mask, rowdec, kdec, gtb, gn,
                                   _Exchange([shards["w_gate"]], True))
    y_s5, bound, w_up = _s5_fwd(proj, pm, pm_t, bre, bim, cre_t, cim_t, lam, dsk, _Exchange([shards["w_up"]], True))
    z, y_ssm, r_ssm = _glu_fwd(y_s5, w_glu, bglu, og, 256)
    x2, h2, r2 = _out_proj_fwd(x, y_ret, y_ssm, w_out, g_ffn, 256)
    a, b, f, w_down = _ffn_up(h2, w_gate, w_up, tm, _Exchange([shards["w_down"]], True))
    dx3, dx3b, loss8, dg_fin = _ffn_down_loss(f, w_down, x2, tgt, g_fin, 256)

    landed = {}
    da, db = _ffn_bwd_act(dx3b, w_down, a, b, tm)
    dw_down = _ffn_wgrad_down(f, dx3b, tm)
    dw_gate, dw_up, landed["w_down"] = _ffn_wgrad_up(h2, da, db, tm, _Exchange([dw_down], False))
    dh2, landed["w_gate"], landed["w_up"] = _ffn_bwd_in(da, db, w_gate, w_up, min(1024, T),
                                                        _Exchange([dw_gate, dw_up], False))
    dx2, dx2b, dg_ffn, dy_ret, dy_ssm = _out_proj_bwd(dh2, x2, r2, g_ffn, dx3, w_out, 256)
    dw_out = jnp.concatenate([_wgrad_rows("out_proj_wgrad_ret", y_ret, dx2b, tm),
                              _wgrad_rows("out_proj_wgrad_ssm", y_ssm, dx2b, tm)], axis=0)
    dy_s5, dw_glu, db_glu, dog = _glu_bwd(y_s5, z, r_ssm, dy_ssm, w_glu, og, 256)
    du, dbre, dbim, dcre, dcim, dlam, dd = _s5_bwd(proj, dy_s5, bound, pm, pm_t, bre, bim, bre_t, bim_t, cre, cim,
                                                   lam, dsk)
    dq, dk, dv, dgate, dgn = _ret_bwd(proj, cosf, sinf, mask, rowdec, kdec, gtb, gn, sblk, dy_ret)
    dproj = jnp.concatenate([dq, dk, dv, dgate, du], axis=1)
    dw_in, landed["w_out"], landed["ssm_w_glu"] = _in_proj_wgrad(
        h1, dproj, tm, _Exchange([dw_out.reshape(N_DEV, D_MODEL // N_DEV, D_MODEL),
                                  dw_glu.astype(BF16).reshape(N_DEV, SSM_WIDTH // N_DEV, SSM_WIDTH)], False))
    small = dict(ret_gn_g=dgn, lam_re=dlam[:, 0], lam_im=dlam[:, 1],
                 bbar_re=_diag_of_state_major(dbre), bbar_im=_diag_of_state_major(dbim),
                 c_re=_diag_of_channel_major(dcre), c_im=_diag_of_channel_major(dcim),
                 ssm_d=dd, b_glu=db_glu, out_g=dog, norm_ffn_g=dg_ffn, norm_final_g=dg_fin)
    packed = _pack([small[n] for n, _ in SMALL_PARTIALS])
    grad_x, dg_mix, landed["w_in"], small_landed = _in_proj_bwd(
        dproj, w_in, x, r1, g_mix, dx2, 256, _Exchange([dw_in, packed], [False, True]))
    (mix_landed,) = _exchange_call("mix_gain_grad_gather", [_pack([dg_mix])], True)
    summed = dict(zip([n for n, _ in SMALL_PARTIALS],
                      _unpack(_sum_partials("small_grad_sum", small_landed), [(sz,) for _, sz in SMALL_PARTIALS])))
    summed["norm_mix_g"] = _sum_partials("mix_gain_grad_sum", mix_landed).reshape(-1)
    return loss8[0, 0], grad_x, landed, summed


def _small_grads(summed, sm):
    _, vjp = jax.vjp(_s5_discretise, sm["ssm_a_re"], sm["ssm_a_im"], sm["ssm_log_dt"], sm["ssm_b_re"], sm["ssm_b_im"])
    gp = (SSM_GROUPS, SSM_STATE)
    da_re, da_im, dlog_dt, db_re, db_im = vjp((summed["lam_re"].reshape(gp), summed["lam_im"].reshape(gp),
                                               summed["bbar_re"].reshape(gp + (SSM_GROUP,)),
                                               summed["bbar_im"].reshape(gp + (SSM_GROUP,))))
    return dict(norm_mix_g=summed["norm_mix_g"], ret_gn_g=summed["ret_gn_g"], ssm_a_re=da_re, ssm_a_im=da_im,
                ssm_log_dt=dlog_dt, ssm_b_re=db_re, ssm_b_im=db_im,
                ssm_c_re=summed["c_re"].reshape(SSM_GROUPS, SSM_GROUP, SSM_STATE),
                ssm_c_im=summed["c_im"].reshape(SSM_GROUPS, SSM_GROUP, SSM_STATE),
                ssm_d=summed["ssm_d"], ssm_b_glu=summed["b_glu"], ssm_out_g=summed["out_g"],
                norm_ffn_g=summed["norm_ffn_g"], norm_final_g=summed["norm_final_g"])


def _adamw_math(w, g, m, v):
    m2 = ADAM_B1 * m + (1.0 - ADAM_B1) * g
    v2 = ADAM_B2 * v + (1.0 - ADAM_B2) * (g * g)
    delta = -ADAM_LR * ((m2 / ADAM_BC1) / (jnp.sqrt(v2 / ADAM_BC2) + ADAM_EPS) + ADAM_WD * w)
    return delta, m2, v2


def _adamw_shard(name, parts, w, m, v, tr):
    rows, cols = w.shape

    def body(p_ref, w_ref, m_ref, v_ref, g_ref, d_ref, m2_ref, v2_ref):
        g = p_ref[0].astype(F32)
        for s in range(1, N_DEV):
            g = g + p_ref[s].astype(F32)
        d, m2, v2 = _adamw_math(w_ref[...], g, m_ref[...], v_ref[...])
        g_ref[...] = g
        d_ref[...] = d
        m2_ref[...] = m2
        v2_ref[...] = v2

    blk = pl.BlockSpec((tr, cols), lambda i: (i, 0))
    oshape = jax.ShapeDtypeStruct((rows, cols), F32)
    return pl.pallas_call(
        body, name=name, grid=(rows // tr,),
        in_specs=[pl.BlockSpec((N_DEV, tr, cols), lambda i: (0, i, 0)), blk, blk, blk],
        out_specs=[blk, blk, blk, blk], out_shape=[oshape] * 4,
        compiler_params=_params(1),
    )(parts, w, m, v)


def _sum_partials(name, parts):
    rows = parts.shape[1]

    def body(p_ref, o_ref):
        g = p_ref[0]
        for s in range(1, N_DEV):
            g = g + p_ref[s]
        o_ref[...] = g

    return pl.pallas_call(
        body, name=name, grid=(1,),
        in_specs=[pl.BlockSpec((N_DEV, rows, LANE), lambda i: (0, 0, 0))],
        out_specs=pl.BlockSpec((rows, LANE), lambda i: (0, 0)),
        out_shape=jax.ShapeDtypeStruct((rows, LANE), F32),
        compiler_params=_params(1),
    )(parts)


def _adamw_small(w, g, m, v):
    rows = w.shape[0]

    def body(w_ref, g_ref, m_ref, v_ref, d_ref, m2_ref, v2_ref):
        d, m2, v2 = _adamw_math(w_ref[...], g_ref[...], m_ref[...], v_ref[...])
        d_ref[...] = d
        m2_ref[...] = m2
        v2_ref[...] = v2

    blk = pl.BlockSpec((rows, LANE), lambda i: (0, 0))
    oshape = jax.ShapeDtypeStruct((rows, LANE), F32)
    return pl.pallas_call(
        body, name="adamw_small", grid=(1,), in_specs=[blk] * 4, out_specs=[blk] * 3, out_shape=[oshape] * 3,
        compiler_params=_params(1),
    )(w, g, m, v)


def _pack(arrays):
    cols = []
    for a in arrays:
        flat = a.reshape(-1).astype(F32)
        pad = (-flat.shape[0]) % LANE
        cols.append(jnp.pad(flat, (0, pad)) if pad else flat)
    return jnp.concatenate(cols).reshape(-1, LANE)


def _unpack(packed, shapes):
    flat = packed.reshape(-1)
    out, off = [], 0
    for shp in shapes:
        n = math.prod(shp)
        out.append(flat[off:off + n].reshape(shp))
        off += n + ((-n) % LANE)
    return out


WEIGHTS = ("norm_mix_g", "w_in", "ret_gn_g", "ssm_a_re", "ssm_a_im", "ssm_log_dt", "ssm_b_re", "ssm_b_im",
           "ssm_c_re", "ssm_c_im", "ssm_d", "ssm_w_glu", "ssm_b_glu", "ssm_out_g", "w_out", "norm_ffn_g", "w_gate",
           "w_up", "w_down", "norm_final_g")
BIG = ("w_in", "ssm_w_glu", "w_out", "w_gate", "w_up", "w_down")
SMALL = tuple(n for n in WEIGHTS if n not in BIG)
ADAM_ROWS = {"w_in": 256, "ssm_w_glu": 128, "w_out": 128, "w_gate": 256, "w_up": 256, "w_down": 176}


def kernel(x, norm_mix_g, w_in, ret_gn_g, ssm_a_re, ssm_a_im, ssm_log_dt, ssm_b_re, ssm_b_im, ssm_c_re, ssm_c_im, ssm_d, ssm_w_glu, ssm_b_glu, ssm_out_g, w_out, norm_ffn_g, w_gate, w_up, w_down, norm_final_g, loss_target, m_norm_mix_g, m_w_in, m_ret_gn_g, m_ssm_a_re, m_ssm_a_im, m_ssm_log_dt, m_ssm_b_re, m_ssm_b_im, m_ssm_c_re, m_ssm_c_im, m_ssm_d, m_ssm_w_glu, m_ssm_b_glu, m_ssm_out_g, m_w_out, m_norm_ffn_g, m_w_gate, m_w_up, m_w_down, m_norm_final_g, v_norm_mix_g, v_w_in, v_ret_gn_g, v_ssm_a_re, v_ssm_a_im, v_ssm_log_dt, v_ssm_b_re, v_ssm_b_im, v_ssm_c_re, v_ssm_c_im, v_ssm_d, v_ssm_w_glu, v_ssm_b_glu, v_ssm_out_g, v_w_out, v_norm_ffn_g, v_w_gate, v_w_up, v_w_down, v_norm_final_g):
    given = dict(locals())
    w = {n: given[n] for n in WEIGHTS}
    m = {n: given["m_" + n] for n in WEIGHTS}
    v = {n: given["v_" + n] for n in WEIGHTS}
    drop = lambda n, a: a if n == "norm_final_g" else a[0]
    w0 = {n: drop(n, w[n]) for n in WEIGHTS}
    m0 = {n: drop(n, m[n]) for n in WEIGHTS}
    v0 = {n: drop(n, v[n]) for n in WEIGHTS}

    sm = {n: w0[n] for n in SMALL}
    shards = {n: w0[n].astype(BF16) for n in BIG}
    loss_local, grad_x, landed, summed = _forward_backward(x[0], loss_target[0], shards, sm)
    loss = lax.psum(loss_local, MESH_AXES)
    gsmall = _small_grads(summed, sm)

    grads, delta, new_m, new_v = {}, {}, {}, {}
    for n in BIG:
        g, d, m2, v2 = _adamw_shard("adamw_" + n, landed[n], w0[n], m0[n], v0[n], ADAM_ROWS[n])
        grads[n], delta[n], new_m[n], new_v[n] = g, d, m2, v2
    shapes = [w0[n].shape for n in SMALL]
    gs = [gsmall[n].reshape(w0[n].shape) for n in SMALL]
    d_p, m_p, v_p = _adamw_small(_pack([w0[n] for n in SMALL]), _pack(gs), _pack([m0[n] for n in SMALL]),
                                 _pack([v0[n] for n in SMALL]))
    for n, g, d, m2, v2 in zip(SMALL, gs, _unpack(d_p, shapes), _unpack(m_p, shapes), _unpack(v_p, shapes)):
        grads[n], delta[n], new_m[n], new_v[n] = g, d, m2, v2

    lift = lambda n, a: a.reshape(w[n].shape)
    return (loss, grad_x[None], *[lift(n, grads[n]) for n in WEIGHTS], *[lift(n, delta[n]) for n in WEIGHTS],
            *[lift(n, new_m[n]) for n in WEIGHTS], *[lift(n, new_v[n]) for n in WEIGHTS])
```

```python
import functools
import math

import jax
import jax.numpy as jnp
from jax import lax
from jax.experimental import pallas as pl
from jax.experimental.pallas import tpu as pltpu

F32 = jnp.float32
BF16 = jnp.bfloat16

D_MODEL = 2048
RET_WIDTH = 1024
RET_HEADS = 8
HEAD_DIM = 128
CHUNK = 64
SSM_WIDTH = 1024
SSM_GROUP = 16
SSM_GROUPS = 64
SSM_STATE = 64
D_FF = 5632
IN_WIDTH = 5120
ROPE_BASE = 10000.0
EPS = 1e-6
N_DEV = 8
MESH_AXES = ("x", "y", "c")

WIN_BLK = IN_WIDTH // N_DEV
FF_BLK = D_FF // N_DEV
RET_BLOCK = 256
RET_HPS = 2
S5_TILE = 256
S5_CHUNKS = 8
S5_STEPS = S5_TILE // S5_CHUNKS
S5_GB = 8
S5_NBLK = SSM_GROUPS // S5_GB
S5_LANES = S5_GB * SSM_STATE
LANE = 128

ADAM_LR = 0.001
ADAM_B1 = 0.9
ADAM_B2 = 0.999
ADAM_EPS = 1e-08
ADAM_WD = 0.01
ADAM_STEP = 10
ADAM_BC1 = 1.0 - ADAM_B1 ** ADAM_STEP
ADAM_BC2 = 1.0 - ADAM_B2 ** ADAM_STEP

VMEM_LIMIT = 56 * 1024 * 1024

NT = (((1,), (1,)), ((), ()))
TN = (((0,), (0,)), ((), ()))


def _params(n_grid):
    return pltpu.CompilerParams(dimension_semantics=("arbitrary",) * n_grid, vmem_limit_bytes=VMEM_LIMIT)


def _dot(a, b):
    return jnp.dot(a, b, preferred_element_type=F32)


def _dot_nt(a, b):
    return lax.dot_general(a, b, NT, preferred_element_type=F32)


def _dot_tn(a, b):
    return lax.dot_general(a, b, TN, preferred_element_type=F32)


def _sigmoid(x):
    return 1.0 / (1.0 + jnp.exp(-x))


_GELU_C = math.sqrt(2.0 / math.pi)
_GELU_A = 0.044715


def _gelu(x):
    t = jnp.tanh(_GELU_C * (x + _GELU_A * x * x * x))
    return 0.5 * x * (1.0 + t)


def _gelu_and_grad(x):
    t = jnp.tanh(_GELU_C * (x + _GELU_A * x * x * x))
    g = 0.5 * (1.0 + t) + 0.5 * x * (1.0 - t * t) * _GELU_C * (1.0 + 3.0 * _GELU_A * x * x)
    return 0.5 * x * (1.0 + t), g


def _rms_bwd(dy, x, r, g):
    w = dy * g
    dx = r * w - x * (r * r * r) * jnp.mean(w * x, axis=-1, keepdims=True)
    return dx, dy * x * r


HBM_SPEC = pl.BlockSpec(memory_space=pltpu.HBM)
ANY_SPEC = pl.BlockSpec(memory_space=pl.ANY)


def _load_resident(src_hbm, dst_vmem, sem):
    cp = pltpu.make_async_copy(src_hbm, dst_vmem, sem)
    cp.start()
    cp.wait()


def _my_block():
    return 4 * lax.axis_index("x") + 2 * lax.axis_index("y") + lax.axis_index("c")


def _peer(k):
    px = lax.axis_index("x") ^ ((k >> 2) & 1)
    py = lax.axis_index("y") ^ ((k >> 1) & 1)
    pc = lax.axis_index("c") ^ (k & 1)
    return (px, py, pc), 4 * px + 2 * py + pc


class _Exchange:
    def __init__(self, payloads, gather):
        self.payloads = list(payloads)
        self.n = len(self.payloads)
        self.gather = [gather] * self.n if isinstance(gather, bool) else list(gather)

    def out_shape(self):
        return [jax.ShapeDtypeStruct(((N_DEV,) if g else ()) + p.shape, p.dtype)
                for p, g in zip(self.payloads, self.gather)]

    def scratch_shapes(self):
        return [pltpu.SemaphoreType.DMA((self.n, N_DEV - 1)), pltpu.SemaphoreType.DMA((self.n, N_DEV - 1)),
                pltpu.SemaphoreType.DMA((self.n,))]

    def _copies(self, ins, outs, sems, incoming):
        send_sems, recv_sems, local_sems = sems
        me = _my_block()
        src_of = lambda i, blk: ins[i] if self.gather[i] else ins[i].at[blk]
        local, remote = [], []
        for i in range(self.n):
            if not incoming:
                local.append(pltpu.make_async_copy(src_of(i, me), outs[i].at[me], local_sems.at[i]))
            for k in range(1, N_DEV):
                dev, blk = _peer(k)
                src, dst = (outs[i].at[blk], outs[i].at[blk]) if incoming else (src_of(i, blk), outs[i].at[me])
                remote.append(pltpu.make_async_remote_copy(
                    src_ref=src, dst_ref=dst, send_sem=send_sems.at[i, k - 1], recv_sem=recv_sems.at[i, k - 1],
                    device_id=dev, device_id_type=pl.DeviceIdType.MESH))
        return local, remote

    def start(self, ins, outs, sems):
        local, sends = self._copies(ins, outs, sems, False)
        for cp in local + sends:
            cp.start()

    def wait(self, ins, outs, sems):
        for cp in self._copies(ins, outs, sems, True)[1]:
            cp.wait_recv()
        local, sends = self._copies(ins, outs, sems, False)
        for cp in sends:
            cp.wait_send()
        for cp in local:
            cp.wait()


def _pcall(body, name, grid, in_specs, out_specs, out_shape, scratch_shapes, args, carry=None):
    n_in, n_out, n_scr = len(in_specs), len(out_specs), len(scratch_shapes)
    if carry is None:
        return pl.pallas_call(body, name=name, grid=grid, in_specs=in_specs, out_specs=out_specs, out_shape=out_shape,
                              scratch_shapes=scratch_shapes, compiler_params=_params(len(grid)))(*args)
    nx = carry.n

    def wrapped(*refs):
        cin, xin = refs[:n_in], refs[n_in:n_in + nx]
        cout, xout = refs[n_in + nx:n_in + nx + n_out], refs[n_in + nx + n_out:n_in + 2 * nx + n_out]
        rest = refs[n_in + 2 * nx + n_out:]
        cscr, sems = rest[:n_scr], rest[n_scr:]
        first = functools.reduce(jnp.logical_and, [pl.program_id(a) == 0 for a in range(len(grid))])
        last = functools.reduce(jnp.logical_and, [pl.program_id(a) == grid[a] - 1 for a in range(len(grid))])

        @pl.when(first)
        def _():
            carry.start(xin, xout, sems)

        body(*cin, *cout, *cscr)

        @pl.when(last)
        def _():
            carry.wait(xin, xout, sems)

    return pl.pallas_call(
        wrapped, name=name, grid=grid, in_specs=list(in_specs) + [HBM_SPEC] * nx,
        out_specs=list(out_specs) + [HBM_SPEC] * nx, out_shape=list(out_shape) + carry.out_shape(),
        scratch_shapes=list(scratch_shapes) + carry.scratch_shapes(), compiler_params=_params(len(grid)),
    )(*args, *carry.payloads)


def _exchange_call(name, payloads, gather):
    ex = _Exchange(payloads, gather)

    def body(*refs):
        ins, outs, sems = refs[:ex.n], refs[ex.n:2 * ex.n], refs[2 * ex.n:]
        ex.start(ins, outs, sems)
        ex.wait(ins, outs, sems)

    return pl.pallas_call(body, name=name, in_specs=[HBM_SPEC] * ex.n, out_specs=[HBM_SPEC] * ex.n,
                          out_shape=ex.out_shape(), scratch_shapes=ex.scratch_shapes())(*ex.payloads)


def _gather_once_per_chip(name, shard):
    def body(src, out, send_sems, recv_sems, local_sem):
        x, y, c = lax.axis_index("x"), lax.axis_index("y"), lax.axis_index("c")
        me, sibling = (x, y, c), (x, y, 1 - c)
        chips = [(1 - x, y), (x, 1 - y), (1 - x, 1 - y)]
        slot = lambda px, py, pc: out.at[4 * px + 2 * py + pc]

        def copy(k, block, to, from_src=False):
            return pltpu.make_async_remote_copy(
                src_ref=src if from_src else slot(*block), dst_ref=slot(*block), send_sem=send_sems.at[k],
                recv_sem=recv_sems.at[k], device_id=to, device_id_type=pl.DeviceIdType.MESH)

        mine = pltpu.make_async_copy(src, slot(*me), local_sem)
        mine.start()
        first = [copy(0, me, sibling, True)] + [copy(1 + j, me, (*chip, c), True) for j, chip in enumerate(chips)]
        for cp in first:
            cp.start()
        passed = [copy(4 + j, (*chip, c), sibling) for j, chip in enumerate(chips)]
        for j, chip in enumerate(chips):
            copy(1 + j, (*chip, c), me).wait_recv()
            passed[j].start()
        copy(0, sibling, me).wait_recv()
        for j, chip in enumerate(chips):
            copy(4 + j, (*chip, 1 - c), me).wait_recv()
        for cp in first + passed:
            cp.wait_send()
        mine.wait()

    return pl.pallas_call(
        body, name=name, in_specs=[HBM_SPEC], out_specs=HBM_SPEC,
        out_shape=jax.ShapeDtypeStruct((N_DEV,) + shard.shape, shard.dtype),
        scratch_shapes=[pltpu.SemaphoreType.DMA((N_DEV - 1,)), pltpu.SemaphoreType.DMA((N_DEV - 1,)),
                        pltpu.SemaphoreType.DMA],
    )(shard)


def _in_proj_fwd(x, g, w, tm, carry=None):
    T = x.shape[0]

    def body(x_ref, g_ref, w_hbm, proj_ref, h_ref, r_ref, w_ref, sem):
        @pl.when(pl.program_id(0) == 0)
        def _():
            _load_resident(w_hbm, w_ref, sem)

        xf = x_ref[...]
        r = lax.rsqrt(jnp.mean(xf * xf, axis=-1, keepdims=True) + EPS)
        h = (xf * r * g_ref[...]).astype(BF16)
        h_ref[...] = h
        r_ref[...] = r
        for j in range(N_DEV):
            proj_ref[:, j * WIN_BLK:(j + 1) * WIN_BLK] = _dot(h, w_ref[j])

    return _pcall(
        body, "in_proj_fwd", (T // tm,),
        [pl.BlockSpec((tm, D_MODEL), lambda i: (i, 0)), pl.BlockSpec((1, D_MODEL), lambda i: (0, 0)), ANY_SPEC],
        [pl.BlockSpec((tm, IN_WIDTH), lambda i: (i, 0)),
         pl.BlockSpec((tm, D_MODEL), lambda i: (i, 0)),
         pl.BlockSpec((tm, 1), lambda i: (i, 0))],
        [jax.ShapeDtypeStruct((T, IN_WIDTH), F32),
         jax.ShapeDtypeStruct((T, D_MODEL), BF16),
         jax.ShapeDtypeStruct((T, 1), F32)],
        [pltpu.VMEM(w.shape, w.dtype), pltpu.SemaphoreType.DMA], (x, g, w), carry)


def _ret_common(q_ref, k_ref, v_ref, cos_ref, sin_ref, mask_ref, rd_ref, sin_state):
    c = cos_ref[...]
    s = sin_ref[...]
    q = q_ref[...]
    q = q * c + pltpu.roll(q, HEAD_DIM // 2, 1) * s
    k = k_ref[...]
    k = (k * c + pltpu.roll(k, HEAD_DIM // 2, 1) * s) * (HEAD_DIM ** -0.5)
    qb = q.astype(BF16)
    kb = k.astype(BF16)
    vb = v_ref[...].astype(BF16)
    pm = (_dot_nt(qb, kb) * mask_ref[...]).astype(BF16)
    qd = (q * rd_ref[...]).astype(BF16)
    o = _dot(pm, vb) + _dot(qd, sin_state.astype(BF16))
    return q, k, qb, kb, vb, pm, qd, o


def _ret_specs(T, rev):
    nb = T // RET_BLOCK
    groups = RET_HEADS // RET_HPS
    wide = RET_HPS * HEAD_DIM
    blk = (lambda b: nb - 1 - b) if rev else (lambda b: b)
    col = lambda piece: (pl.BlockSpec((RET_BLOCK, wide), lambda h, b: (blk(b), piece * groups + h)), "lane")
    return dict(
        q=col(0), k=col(1), v=col(2), g=col(3),
        tab=(pl.BlockSpec((RET_BLOCK, HEAD_DIM), lambda h, b: (blk(b), 0)), None),
        mask=(pl.BlockSpec((RET_HPS, RET_BLOCK, RET_BLOCK), lambda h, b: (h, 0, 0)), "lead"),
        dec=(pl.BlockSpec((RET_HPS, RET_BLOCK, HEAD_DIM), lambda h, b: (h, 0, 0)), "lead"),
        gtb=(pl.BlockSpec((RET_HPS, 1, HEAD_DIM), lambda h, b: (h, 0, 0)), "lead"),
        gn=(pl.BlockSpec((1, wide), lambda h, b: (0, h)), "lane"),
        state=(pl.BlockSpec((RET_HPS, None, HEAD_DIM, HEAD_DIM), lambda h, b: (h, blk(b), 0, 0)), "lead"),
        rows=(pl.BlockSpec((RET_BLOCK, wide), lambda h, b: (blk(b), h)), "lane"),
        scratch=(pltpu.VMEM((RET_HPS, HEAD_DIM, HEAD_DIM), F32), "lead"),
    )


def _per_head(head_body, kinds):
    def body(*refs):
        for hh in range(RET_HPS):
            views = []
            for ref, kind in zip(refs, kinds):
                if kind == "lane":
                    views.append(ref.at[:, hh * HEAD_DIM:(hh + 1) * HEAD_DIM])
                elif kind == "lead":
                    views.append(ref.at[hh])
                else:
                    views.append(ref)
            head_body(*views)
    return body


def _ret_fwd(proj, cosf, sinf, mask, rowdec, kdec, gtb, gn, carry=None):
    T = proj.shape[0]
    nb = T // RET_BLOCK
    sp = _ret_specs(T, False)

    def body(q_ref, k_ref, v_ref, g_ref, cos_ref, sin_ref, mask_ref, rd_ref, kd_ref, gtb_ref, gn_ref,
             y_ref, sb_ref, st):
        @pl.when(pl.program_id(1) == 0)
        def _():
            st[...] = jnp.zeros_like(st)
        s_in = st[...]
        sb_ref[...] = s_in
        q, k, qb, kb, vb, pm, qd, o = _ret_common(q_ref, k_ref, v_ref, cos_ref, sin_ref, mask_ref, rd_ref, s_in)
        st[...] = gtb_ref[...] * s_in + _dot_tn((k * kd_ref[...]).astype(BF16), vb)
        mu = jnp.mean(o, axis=-1, keepdims=True)
        oc = o - mu
        n = oc * lax.rsqrt(jnp.mean(oc * oc, axis=-1, keepdims=True) + EPS)
        gt = g_ref[...]
        y_ref[...] = (gt * _sigmoid(gt) * (n * gn_ref[...])).astype(BF16)

    ins = [sp[n] for n in ("q", "k", "v", "g", "tab", "tab", "mask", "dec", "dec", "gtb", "gn")]
    outs = [sp["rows"], sp["state"]]
    return _pcall(
        _per_head(body, [kind for _, kind in ins + outs + [sp["scratch"]]]), "ret_fwd", (RET_HEADS // RET_HPS, nb),
        [s for s, _ in ins], [s for s, _ in outs],
        [jax.ShapeDtypeStruct((T, RET_WIDTH), BF16),
         jax.ShapeDtypeStruct((RET_HEADS, nb, HEAD_DIM, HEAD_DIM), F32)],
        [sp["scratch"][0]],
        (proj, proj, proj, proj, cosf, sinf, mask, rowdec, kdec, gtb, gn), carry)


def _scan(re, im, ar, ai, reverse):
    n = re.shape[0]
    row = lax.broadcasted_iota(jnp.int32, re.shape, 0)
    s = 1
    while s < n:
        if reverse:
            keep = row < n - s
            sr = jnp.where(keep, pltpu.roll(re, n - s, 0), 0.0)
            si = jnp.where(keep, pltpu.roll(im, n - s, 0), 0.0)
        else:
            keep = row >= s
            sr = jnp.where(keep, pltpu.roll(re, s, 0), 0.0)
            si = jnp.where(keep, pltpu.roll(im, s, 0), 0.0)
        re, im = re + ar * sr - ai * si, im + ar * si + ai * sr
        ar, ai = ar * ar - ai * ai, 2.0 * ar * ai
        s *= 2
    return re, im


S5_STATE_TILE = (S5_TILE, S5_LANES)


def _step_major_permutation():
    r = jnp.arange(S5_TILE)
    t_of_row = (r % S5_CHUNKS) * S5_STEPS + r // S5_CHUNKS
    return (t_of_row[:, None] == r[None, :]).astype(BF16)


def _permute_rows_f32(pm, x):
    hi = x.astype(BF16)
    rest = x - hi.astype(F32)
    mid = rest.astype(BF16)
    lo = (rest - mid.astype(F32)).astype(BF16)
    return _dot(pm, hi) + _dot(pm, mid) + _dot(pm, lo)


def _step_get(ref, j):
    return ref[j * S5_CHUNKS:(j + 1) * S5_CHUNKS, :]


def _step_set(ref, j, val):
    ref[j * S5_CHUNKS:(j + 1) * S5_CHUNKS, :] = val


def _tile_get(ref):
    return ref[...]


def _tile_set(ref, val):
    ref[...] = val


def _fill_power_table(ptab, lr, li):
    shape = (S5_CHUNKS, S5_LANES)
    lrb = jnp.broadcast_to(lr, shape)
    lib = jnp.broadcast_to(li, shape)
    pr, pi_ = lrb, lib
    for j in range(S5_STEPS):
        ptab[0, j * S5_CHUNKS:(j + 1) * S5_CHUNKS, :] = pr
        ptab[1, j * S5_CHUNKS:(j + 1) * S5_CHUNKS, :] = pi_
        pr, pi_ = lrb * pr - lib * pi_, lrb * pi_ + lib * pr


def _chunk_scans(xr, xi, lr, li, reverse):
    shape = (S5_CHUNKS, S5_LANES)
    lrb = jnp.broadcast_to(lr, shape)
    lib = jnp.broadcast_to(li, shape)
    sr = si = None
    for j in (range(S5_STEPS - 1, -1, -1) if reverse else range(S5_STEPS)):
        vr = _step_get(xr, j)
        vi = _step_get(xi, j)
        if sr is not None:
            vr, vi = vr + lrb * sr - lib * si, vi + lrb * si + lib * sr
            _step_set(xr, j, vr)
            _step_set(xi, j, vi)
        sr, si = vr, vi
    return sr, si


def _entering_states(zr, zi, cr, ci, ar, ai, reverse):
    shape = (S5_CHUNKS, S5_LANES)
    row = lax.broadcasted_iota(jnp.int32, shape, 0)
    if reverse:
        edge, shift = row == S5_CHUNKS - 1, S5_CHUNKS - 1
    else:
        edge, shift = row == 0, 1
    wr = jnp.where(edge, jnp.broadcast_to(cr, shape), pltpu.roll(zr, shift, 0))
    wi = jnp.where(edge, jnp.broadcast_to(ci, shape), pltpu.roll(zi, shift, 0))
    return _scan(wr, wi, ar, ai, reverse)


def _table_rows(ptab, j, conj):
    pr = ptab[0, j * S5_CHUNKS:(j + 1) * S5_CHUNKS, :]
    pi_ = ptab[1, j * S5_CHUNKS:(j + 1) * S5_CHUNKS, :]
    return pr, (-pi_ if conj else pi_)


def _s5_forward_states(xr, xi, lr, li, cr, ci, ptab):
    zr, zi = _chunk_scans(xr, xi, lr, li, False)
    ar, ai = _table_rows(ptab, S5_STEPS - 1, False)
    er, ei = _entering_states(zr, zi, cr, ci, ar, ai, False)
    for j in range(S5_STEPS):
        pr, pi_ = _table_rows(ptab, j, False)
        _step_set(xr, j, _step_get(xr, j) + pr * er - pi_ * ei)
        _step_set(xi, j, _step_get(xi, j) + pr * ei + pi_ * er)
    last = S5_CHUNKS - 1
    end_r = (ar * er - ai * ei + zr)[last:last + 1, :]
    end_i = (ar * ei + ai * er + zi)[last:last + 1, :]
    return er, ei, end_r, end_i


def _s5_specs(T, rev):
    nt = T // S5_TILE
    tt = (lambda t: nt - 1 - t) if rev else (lambda t: t)
    return dict(
        u=pl.BlockSpec((S5_TILE, LANE), lambda b, t: (tt(t), 4 * RET_HEADS + b)),
        rows=pl.BlockSpec((S5_TILE, LANE), lambda b, t: (tt(t), b)),
        to_state=pl.BlockSpec((None, LANE, S5_LANES), lambda b, t: (b, 0, 0)),
        from_state=pl.BlockSpec((None, S5_LANES, LANE), lambda b, t: (b, 0, 0)),
        lam=pl.BlockSpec((None, 2, S5_LANES), lambda b, t: (b, 0, 0)),
        d=pl.BlockSpec((1, LANE), lambda b, t: (0, b)),
        perm=pl.BlockSpec((S5_TILE, S5_TILE), lambda b, t: (0, 0)),
        bound=pl.BlockSpec((None, None, 2, S5_LANES), lambda b, t: (b, tt(t), 0, 0)),
    )


def _s5_fwd(proj, pm, pm_t, bre, bim, cre_t, cim_t, lam, d, carry=None):
    T = proj.shape[0]
    nt = T // S5_TILE
    sp = _s5_specs(T, False)

    def body(u_ref, pm_ref, pmt_ref, bre_ref, bim_ref, cre_ref, cim_ref, lam_ref, d_ref, y_ref, bound_ref,
             carry, ptab, xr, xi):
        lr = lam_ref[0:1, :]
        li = lam_ref[1:2, :]

        @pl.when(pl.program_id(1) == 0)
        def _():
            carry[...] = jnp.zeros_like(carry)
            _fill_power_table(ptab, lr, li)

        u = _permute_rows_f32(pm_ref[...], u_ref[...])
        ub = u.astype(BF16)
        _tile_set(xr, _dot(ub, bre_ref[...]))
        _tile_set(xi, _dot(ub, bim_ref[...]))
        bound_ref[...] = carry[...]
        _, _, end_r, end_i = _s5_forward_states(xr, xi, lr, li, carry[0:1, :], carry[1:2, :], ptab)
        carry[0:1, :] = end_r
        carry[1:2, :] = end_i
        y = (_dot(_tile_get(xr).astype(BF16), cre_ref[...]) - _dot(_tile_get(xi).astype(BF16), cim_ref[...])
             + d_ref[...] * u)
        y_ref[...] = _permute_rows_f32(pmt_ref[...], y)

    state = pltpu.VMEM(S5_STATE_TILE, F32)
    return _pcall(
        body, "s5_fwd", (S5_NBLK, nt),
        [sp["u"], sp["perm"], sp["perm"], sp["to_state"], sp["to_state"], sp["from_state"],
         sp["from_state"], sp["lam"], sp["d"]],
        [sp["rows"], sp["bound"]],
        [jax.ShapeDtypeStruct((T, SSM_WIDTH), F32),
         jax.ShapeDtypeStruct((S5_NBLK, nt, 2, S5_LANES), F32)],
        [pltpu.VMEM((2, S5_LANES), F32), pltpu.VMEM((2, S5_TILE, S5_LANES), F32), state, state],
        (proj, pm, pm_t, bre, bim, cre_t, cim_t, lam, d), carry)


def _glu_fwd(y, w, b, og, tm):
    T = y.shape[0]

    def body(y_ref, w_ref, b_ref, og_ref, z_ref, o_ref, r_ref):
        y1 = _gelu(y_ref[...])
        z = _dot(y1.astype(BF16), w_ref[...]) + b_ref[...]
        y2 = y1 * _sigmoid(z)
        r = lax.rsqrt(jnp.mean(y2 * y2, axis=-1, keepdims=True) + EPS)
        z_ref[...] = z
        o_ref[...] = (y2 * r * og_ref[...]).astype(BF16)
        r_ref[...] = r

    row = pl.BlockSpec((tm, SSM_WIDTH), lambda i: (i, 0))
    vec = pl.BlockSpec((1, SSM_WIDTH), lambda i: (0, 0))
    return pl.pallas_call(
        body, name="glu_fwd", grid=(T // tm,),
        in_specs=[row, pl.BlockSpec((SSM_WIDTH, SSM_WIDTH), lambda i: (0, 0)), vec, vec],
        out_specs=[row, row, pl.BlockSpec((tm, 1), lambda i: (i, 0))],
        out_shape=[jax.ShapeDtypeStruct((T, SSM_WIDTH), F32), jax.ShapeDtypeStruct((T, SSM_WIDTH), BF16),
                   jax.ShapeDtypeStruct((T, 1), F32)],
        compiler_params=_params(1),
    )(y, w, b, og)


def _out_proj_fwd(x, y_ret, y_ssm, w, g, tm):
    T = x.shape[0]

    def body(x_ref, a_ref, b_ref, w_ref, g_ref, x2_ref, h_ref, r_ref):
        x2 = x_ref[...] + _dot(a_ref[...], w_ref[0:RET_WIDTH, :]) + _dot(b_ref[...], w_ref[RET_WIDTH:D_MODEL, :])
        r = lax.rsqrt(jnp.mean(x2 * x2, axis=-1, keepdims=True) + EPS)
        x2_ref[...] = x2
        h_ref[...] = (x2 * r * g_ref[...]).astype(BF16)
        r_ref[...] = r

    full = pl.BlockSpec((tm, D_MODEL), lambda i: (i, 0))
    half = pl.BlockSpec((tm, RET_WIDTH), lambda i: (i, 0))
    return pl.pallas_call(
        body, name="out_proj_fwd", grid=(T // tm,),
        in_specs=[full, half, half, pl.BlockSpec((D_MODEL, D_MODEL), lambda i: (0, 0)),
                  pl.BlockSpec((1, D_MODEL), lambda i: (0, 0))],
        out_specs=[full, full, pl.BlockSpec((tm, 1), lambda i: (i, 0))],
        out_shape=[jax.ShapeDtypeStruct((T, D_MODEL), F32), jax.ShapeDtypeStruct((T, D_MODEL), BF16),
                   jax.ShapeDtypeStruct((T, 1), F32)],
        compiler_params=_params(1),
    )(x, y_ret, y_ssm, w, g)


def _ffn_up(h, wg, wu, tm, carry=None):
    T = h.shape[0]

    def body(h_ref, wg_ref, wu_ref, a_ref, b_ref, f_ref):
        hb = h_ref[...]
        a = _dot(hb, wg_ref[...])
        b = _dot(hb, wu_ref[...])
        a_ref[...] = a.astype(BF16)
        b_ref[...] = b.astype(BF16)
        f_ref[...] = (a * _sigmoid(a) * b).astype(BF16)

    wspec = pl.BlockSpec((None, D_MODEL, FF_BLK), lambda j, i: (j, 0, 0))
    ospec = pl.BlockSpec((None, tm, FF_BLK), lambda j, i: (j, i, 0))
    oshape = jax.ShapeDtypeStruct((N_DEV, T, FF_BLK), BF16)
    return _pcall(
        body, "ffn_up", (N_DEV, T // tm),
        [pl.BlockSpec((tm, D_MODEL), lambda j, i: (i, 0)), wspec, wspec],
        [ospec, ospec, ospec], [oshape, oshape, oshape], [], (h, wg, wu), carry)


def _ffn_down_loss(f, wd, x2, tgt, g, tm):
    T = x2.shape[0]

    def body(f_ref, w_hbm, x2_ref, t_ref, g_ref, dx_ref, dxb_ref, loss_ref, dg_ref, w_ref, sem):
        i = pl.program_id(0)

        @pl.when(i == 0)
        def _():
            _load_resident(w_hbm, w_ref, sem)
            loss_ref[...] = jnp.zeros_like(loss_ref)
            dg_ref[...] = jnp.zeros_like(dg_ref)

        x3 = x2_ref[...]
        for k in range(N_DEV):
            x3 = x3 + _dot(f_ref[k], w_ref[k])
        gv = g_ref[...]
        r = lax.rsqrt(jnp.mean(x3 * x3, axis=-1, keepdims=True) + EPS)
        err = x3 * r * gv - t_ref[...]
        tile_loss = 0.5 * jnp.sum(jnp.mean(err * err, axis=-1, keepdims=True), axis=0, keepdims=True)
        dx, dgt = _rms_bwd(err * (1.0 / D_MODEL), x3, r, gv)
        dx_ref[...] = dx
        dxb_ref[...] = dx.astype(BF16)
        loss_ref[...] += jnp.broadcast_to(tile_loss, loss_ref.shape)
        dg_ref[...] += jnp.sum(dgt, axis=0, keepdims=True)

    full = pl.BlockSpec((tm, D_MODEL), lambda i: (i, 0))
    vec = pl.BlockSpec((1, D_MODEL), lambda i: (0, 0))
    return pl.pallas_call(
        body, name="ffn_down_loss", grid=(T // tm,),
        in_specs=[pl.BlockSpec((N_DEV, tm, FF_BLK), lambda i: (0, i, 0)), ANY_SPEC, full, full, vec],
        out_specs=[full, full, pl.BlockSpec((8, LANE), lambda i: (0, 0)), vec],
        out_shape=[jax.ShapeDtypeStruct((T, D_MODEL), F32), jax.ShapeDtypeStruct((T, D_MODEL), BF16),
                   jax.ShapeDtypeStruct((8, LANE), F32), jax.ShapeDtypeStruct((1, D_MODEL), F32)],
        scratch_shapes=[pltpu.VMEM(wd.shape, wd.dtype), pltpu.SemaphoreType.DMA],
        compiler_params=_params(1),
    )(f, wd, x2, tgt, g)


def _ffn_bwd_act(dxb, wd, a, b, tm):
    T = dxb.shape[0]

    def body(dx_ref, w_ref, a_ref, b_ref, da_ref, db_ref):
        df = _dot_nt(dx_ref[...], w_ref[...])
        a = a_ref[...].astype(F32)
        b = b_ref[...].astype(F32)
        sg = _sigmoid(a)
        da_ref[...] = (df * b * sg * (1.0 + a * (1.0 - sg))).astype(BF16)
        db_ref[...] = (df * a * sg).astype(BF16)

    blk = pl.BlockSpec((None, tm, FF_BLK), lambda j, i: (j, i, 0))
    oshape = jax.ShapeDtypeStruct((N_DEV, T, FF_BLK), BF16)
    return pl.pallas_call(
        body, name="ffn_bwd_act", grid=(N_DEV, T // tm),
        in_specs=[pl.BlockSpec((tm, D_MODEL), lambda j, i: (i, 0)),
                  pl.BlockSpec((None, FF_BLK, D_MODEL), lambda j, i: (j, 0, 0)), blk, blk],
        out_specs=[blk, blk], out_shape=[oshape, oshape],
        compiler_params=_params(2),
    )(dxb, wd, a, b)


def _ffn_bwd_in(da, db, wg, wu, tm, carry=None):
    T = da.shape[1]

    def body(da_ref, db_ref, wg_ref, wu_ref, dh_ref):
        part = _dot_nt(da_ref[...], wg_ref[...]) + _dot_nt(db_ref[...], wu_ref[...])

        @pl.when(pl.program_id(1) == 0)
        def _():
            dh_ref[...] = part

        @pl.when(pl.program_id(1) > 0)
        def _():
            dh_ref[...] += part

    ablk = pl.BlockSpec((None, tm, FF_BLK), lambda i, k: (k, i, 0))
    wblk = pl.BlockSpec((None, D_MODEL, FF_BLK), lambda i, k: (k, 0, 0))
    return _pcall(
        body, "ffn_bwd_in", (T // tm, N_DEV), [ablk, ablk, wblk, wblk],
        [pl.BlockSpec((tm, D_MODEL), lambda i, k: (i, 0))], [jax.ShapeDtypeStruct((T, D_MODEL), F32)],
        [], (da, db, wg, wu), carry)


def _ffn_wgrad_up(h, da, db, tk, carry=None):
    T = h.shape[0]
    nk = T // tk

    def body(h_ref, da_ref, db_ref, g_ref, u_ref, accg, accu):
        k = pl.program_id(1)

        @pl.when(k == 0)
        def _():
            accg[...] = jnp.zeros_like(accg)
            accu[...] = jnp.zeros_like(accu)

        hb = h_ref[...]
        accg[...] += _dot_tn(hb, da_ref[...])
        accu[...] += _dot_tn(hb, db_ref[...])

        @pl.when(k == nk - 1)
        def _():
            g_ref[...] = accg[...].astype(BF16)
            u_ref[...] = accu[...].astype(BF16)

    blk = pl.BlockSpec((None, tk, FF_BLK), lambda j, k: (j, k, 0))
    ospec = pl.BlockSpec((None, D_MODEL, FF_BLK), lambda j, k: (j, 0, 0))
    oshape = jax.ShapeDtypeStruct((N_DEV, D_MODEL, FF_BLK), BF16)
    return _pcall(
        body, "ffn_wgrad_up", (N_DEV, nk),
        [pl.BlockSpec((tk, D_MODEL), lambda j, k: (k, 0)), blk, blk],
        [ospec, ospec], [oshape, oshape],
        [pltpu.VMEM((D_MODEL, FF_BLK), F32), pltpu.VMEM((D_MODEL, FF_BLK), F32)], (h, da, db), carry)


def _ffn_wgrad_down(f, dxb, tk):
    T = dxb.shape[0]
    nk = T // tk

    def body(f_ref, dx_ref, o_ref, acc):
        k = pl.program_id(1)

        @pl.when(k == 0)
        def _():
            acc[...] = jnp.zeros_like(acc)

        acc[...] += _dot_tn(f_ref[...], dx_ref[...])

        @pl.when(k == nk - 1)
        def _():
            o_ref[...] = acc[...].astype(BF16)

    return pl.pallas_call(
        body, name="ffn_wgrad_down", grid=(N_DEV, nk),
        in_specs=[pl.BlockSpec((None, tk, FF_BLK), lambda j, k: (j, k, 0)),
                  pl.BlockSpec((tk, D_MODEL), lambda j, k: (k, 0))],
        out_specs=pl.BlockSpec((None, FF_BLK, D_MODEL), lambda j, k: (j, 0, 0)),
        out_shape=jax.ShapeDtypeStruct((N_DEV, FF_BLK, D_MODEL), BF16),
        scratch_shapes=[pltpu.VMEM((FF_BLK, D_MODEL), F32)],
        compiler_params=_params(2),
    )(f, dxb)


def _out_proj_bwd(dh2, x2, r2, g, dx3, w, tm):
    T = x2.shape[0]

    def body(dh_ref, x_ref, r_ref, g_ref, dx3_ref, w_ref, dx_ref, dxb_ref, dg_ref, a_ref, b_ref):
        @pl.when(pl.program_id(0) == 0)
        def _():
            dg_ref[...] = jnp.zeros_like(dg_ref)

        dxn, dgt = _rms_bwd(dh_ref[...], x_ref[...], r_ref[...], g_ref[...])
        dx = dx3_ref[...] + dxn
        dxv = dx.astype(BF16)
        dx_ref[...] = dx
        dxb_ref[...] = dxv
        dg_ref[...] += jnp.sum(dgt, axis=0, keepdims=True)
        a_ref[...] = _dot_nt(dxv, w_ref[0:RET_WIDTH, :])
        b_ref[...] = _dot_nt(dxv, w_ref[RET_WIDTH:D_MODEL, :])

    full = pl.BlockSpec((tm, D_MODEL), lambda i: (i, 0))
    vec = pl.BlockSpec((1, D_MODEL), lambda i: (0, 0))
    half = pl.BlockSpec((tm, RET_WIDTH), lambda i: (i, 0))
    hshape = jax.ShapeDtypeStruct((T, RET_WIDTH), F32)
    return pl.pallas_call(
        body, name="out_proj_bwd", grid=(T // tm,),
        in_specs=[full, full, pl.BlockSpec((tm, 1), lambda i: (i, 0)), vec, full,
                  pl.BlockSpec((D_MODEL, D_MODEL), lambda i: (0, 0))],
        out_specs=[full, full, vec, half, half],
        out_shape=[jax.ShapeDtypeStruct((T, D_MODEL), F32), jax.ShapeDtypeStruct((T, D_MODEL), BF16),
                   jax.ShapeDtypeStruct((1, D_MODEL), F32), hshape, hshape],
        compiler_params=_params(1),
    )(dh2, x2, r2, g, dx3, w)


def _wgrad_rows(name, a, b, tk):
    T, M = a.shape
    N = b.shape[1]
    nk = T // tk

    def body(a_ref, b_ref, o_ref, acc):
        k = pl.program_id(0)

        @pl.when(k == 0)
        def _():
            acc[...] = jnp.zeros_like(acc)

        acc[...] += _dot_tn(a_ref[...], b_ref[...])

        @pl.when(k == nk - 1)
        def _():
            o_ref[...] = acc[...].astype(BF16)

    return pl.pallas_call(
        body, name=name, grid=(nk,),
        in_specs=[pl.BlockSpec((tk, M), lambda k: (k, 0)), pl.BlockSpec((tk, N), lambda k: (k, 0))],
        out_specs=pl.BlockSpec((M, N), lambda k: (0, 0)),
        out_shape=jax.ShapeDtypeStruct((M, N), BF16),
        scratch_shapes=[pltpu.VMEM((M, N), F32)],
        compiler_params=_params(1),
    )(a, b)


def _glu_bwd(y, z, r, dyo, w, og, tm):
    T = y.shape[0]

    def body(y_ref, z_ref, r_ref, d_ref, w_ref, og_ref, dy_ref, dw_ref, db_ref, dog_ref):
        @pl.when(pl.program_id(0) == 0)
        def _():
            dw_ref[...] = jnp.zeros_like(dw_ref)
            db_ref[...] = jnp.zeros_like(db_ref)
            dog_ref[...] = jnp.zeros_like(dog_ref)

        y1, g1 = _gelu_and_grad(y_ref[...])
        sg = _sigmoid(z_ref[...])
        y2 = y1 * sg
        dy2, dogt = _rms_bwd(d_ref[...], y2, r_ref[...], og_ref[...])
        dog_ref[...] += jnp.sum(dogt, axis=0, keepdims=True)
        dz = dy2 * y1 * sg * (1.0 - sg)
        db_ref[...] += jnp.sum(dz, axis=0, keepdims=True)
        dzb = dz.astype(BF16)
        dw_ref[...] += _dot_tn(y1.astype(BF16), dzb)
        dy_ref[...] = (dy2 * sg + _dot_nt(dzb, w_ref[...])) * g1

    row = pl.BlockSpec((tm, SSM_WIDTH), lambda i: (i, 0))
    vec = pl.BlockSpec((1, SSM_WIDTH), lambda i: (0, 0))
    sq = pl.BlockSpec((SSM_WIDTH, SSM_WIDTH), lambda i: (0, 0))
    return pl.pallas_call(
        body, name="glu_bwd", grid=(T // tm,),
        in_specs=[row, row, pl.BlockSpec((tm, 1), lambda i: (i, 0)), row, sq, vec],
        out_specs=[row, sq, vec, vec],
        out_shape=[jax.ShapeDtypeStruct((T, SSM_WIDTH), F32), jax.ShapeDtypeStruct((SSM_WIDTH, SSM_WIDTH), F32),
                   jax.ShapeDtypeStruct((1, SSM_WIDTH), F32), jax.ShapeDtypeStruct((1, SSM_WIDTH), F32)],
        compiler_params=_params(1),
    )(y, z, r, dyo, w, og)


def _s5_bwd(proj, dy, bound, pm, pm_t, bre, bim, bre_t, bim_t, cre, cim, lam, d):
    T = proj.shape[0]
    nt = T // S5_TILE
    sp = _s5_specs(T, True)

    def body(u_ref, dy_ref, bound_ref, pm_ref, pmt_ref, bre_ref, bim_ref, bret_ref, bimt_ref, cre_ref, cim_ref,
             lam_ref, d_ref,
             du_ref, dbre_ref, dbim_ref, dcre_ref, dcim_ref, dlam_ref, dd_ref, carry, ptab, sr, si, gr, gi):
        lr = lam_ref[0:1, :]
        li = lam_ref[1:2, :]

        @pl.when(pl.program_id(1) == 0)
        def _():
            carry[...] = jnp.zeros_like(carry)
            _fill_power_table(ptab, lr, li)
            for ref in (dbre_ref, dbim_ref, dcre_ref, dcim_ref, dlam_ref, dd_ref):
                ref[...] = jnp.zeros_like(ref)

        u = _permute_rows_f32(pm_ref[...], u_ref[...])
        ub = u.astype(BF16)
        dyv = _permute_rows_f32(pm_ref[...], dy_ref[...])
        dyb = dyv.astype(BF16)
        _tile_set(sr, _dot(ub, bre_ref[...]))
        _tile_set(si, _dot(ub, bim_ref[...]))
        er, ei, _, _ = _s5_forward_states(sr, si, lr, li, bound_ref[0:1, :], bound_ref[1:2, :], ptab)
        _tile_set(gr, _dot(dyb, cre_ref[...]))
        _tile_set(gi, -_dot(dyb, cim_ref[...]))
        zr, zi = _chunk_scans(gr, gi, lr, -li, True)
        ar, ai = _table_rows(ptab, S5_STEPS - 1, True)
        fr, fi = _entering_states(zr, zi, carry[0:1, :], carry[1:2, :], ar, ai, True)
        acc_r = jnp.zeros((S5_CHUNKS, S5_LANES), F32)
        acc_i = jnp.zeros((S5_CHUNKS, S5_LANES), F32)
        for j in range(S5_STEPS):
            qr, qi = _table_rows(ptab, S5_STEPS - 1 - j, True)
            g_r = _step_get(gr, j) + qr * fr - qi * fi
            g_i = _step_get(gi, j) + qr * fi + qi * fr
            _step_set(gr, j, g_r)
            _step_set(gi, j, g_i)
            p_r, p_i = (er, ei) if j == 0 else (_step_get(sr, j - 1), _step_get(si, j - 1))
            acc_r += g_r * p_r + g_i * p_i
            acc_i += g_i * p_r - g_r * p_i
        dlam_ref[0:1, :] += jnp.sum(acc_r, axis=0, keepdims=True)
        dlam_ref[1:2, :] += jnp.sum(acc_i, axis=0, keepdims=True)
        g_all_r = _tile_get(gr)
        g_all_i = _tile_get(gi)
        carry[0:1, :] = g_all_r[0:1, :]
        carry[1:2, :] = g_all_i[0:1, :]
        grb = g_all_r.astype(BF16)
        gib = g_all_i.astype(BF16)
        du = (_dot(grb, bret_ref[...]) + _dot(gib, bimt_ref[...]) + d_ref[...] * dyv).astype(BF16)
        du_ref[...] = _dot(pmt_ref[...], du).astype(BF16)
        dbre_ref[...] += _dot_tn(grb, ub)
        dbim_ref[...] += _dot_tn(gib, ub)
        dcre_ref[...] += _dot_tn(dyb, _tile_get(sr).astype(BF16))
        dcim_ref[...] -= _dot_tn(dyb, _tile_get(si).astype(BF16))
        dd_ref[...] += jnp.sum(dyv * u, axis=0, keepdims=True)

    acc_ts = pl.BlockSpec((None, S5_LANES, LANE), lambda b, t: (b, 0, 0))
    acc_fs = pl.BlockSpec((None, LANE, S5_LANES), lambda b, t: (b, 0, 0))
    return pl.pallas_call(
        body, name="s5_bwd", grid=(S5_NBLK, nt),
        in_specs=[sp["u"], sp["rows"], sp["bound"], sp["perm"], sp["perm"], sp["to_state"], sp["to_state"],
                  sp["from_state"], sp["from_state"], sp["to_state"], sp["to_state"], sp["lam"], sp["d"]],
        out_specs=[sp["rows"], acc_ts, acc_ts, acc_fs, acc_fs, sp["lam"], sp["d"]],
        out_shape=[jax.ShapeDtypeStruct((T, SSM_WIDTH), BF16),
                   jax.ShapeDtypeStruct((S5_NBLK, S5_LANES, LANE), F32),
                   jax.ShapeDtypeStruct((S5_NBLK, S5_LANES, LANE), F32),
                   jax.ShapeDtypeStruct((S5_NBLK, LANE, S5_LANES), F32),
                   jax.ShapeDtypeStruct((S5_NBLK, LANE, S5_LANES), F32),
                   jax.ShapeDtypeStruct((S5_NBLK, 2, S5_LANES), F32),
                   jax.ShapeDtypeStruct((1, SSM_WIDTH), F32)],
        scratch_shapes=[pltpu.VMEM((2, S5_LANES), F32), pltpu.VMEM((2, S5_TILE, S5_LANES), F32)]
        + [pltpu.VMEM(S5_STATE_TILE, F32)] * 4,
        compiler_params=_params(2),
    )(proj, dy, bound, pm, pm_t, bre, bim, bre_t, bim_t, cre, cim, lam, d)


def _ret_bwd(proj, cosf, sinf, mask, rowdec, kdec, gtb, gn, sblk, dyr):
    T = proj.shape[0]
    nb = T // RET_BLOCK
    sp = _ret_specs(T, True)

    def body(q_ref, k_ref, v_ref, g_ref, cos_ref, sin_ref, mask_ref, rd_ref, kd_ref, gtb_ref, gn_ref, sb_ref, dy_ref,
             dq_ref, dk_ref, dv_ref, dg_ref, dgn_ref, dst):
        @pl.when(pl.program_id(1) == 0)
        def _():
            dst[...] = jnp.zeros_like(dst)
            dgn_ref[...] = jnp.zeros_like(dgn_ref)

        s_in = sb_ref[...]
        q, k, qb, kb, vb, pm, qd, o = _ret_common(q_ref, k_ref, v_ref, cos_ref, sin_ref, mask_ref, rd_ref, s_in)
        mu = jnp.mean(o, axis=-1, keepdims=True)
        oc = o - mu
        rstd = lax.rsqrt(jnp.mean(oc * oc, axis=-1, keepdims=True) + EPS)
        n = oc * rstd
        gt = g_ref[...]
        sg = _sigmoid(gt)
        sil = gt * sg
        gnv = gn_ref[...]
        dyv = dy_ref[...]
        dg_ref[...] = (dyv * (n * gnv) * (sg * (1.0 + gt * (1.0 - sg)))).astype(BF16)
        dgn_ref[...] += jnp.sum(dyv * sil * n, axis=0, keepdims=True)
        dn = dyv * sil * gnv
        do = rstd * (dn - jnp.mean(dn, axis=-1, keepdims=True) - n * jnp.mean(dn * n, axis=-1, keepdims=True))
        dob = do.astype(BF16)
        ds = dst[...]
        dsb = ds.astype(BF16)
        kd = kd_ref[...]
        rd = rd_ref[...]
        dv_ref[...] = (_dot_tn(pm, dob) + _dot((k * kd).astype(BF16), dsb)).astype(BF16)
        dpb = (_dot_nt(dob, vb) * mask_ref[...]).astype(BF16)
        dq = _dot(dpb, kb) + _dot_nt(dob, s_in.astype(BF16)) * rd
        dk = (_dot_tn(dpb, qb) + _dot_nt(vb, dsb) * kd) * (HEAD_DIM ** -0.5)
        dst[...] = gtb_ref[...] * ds + _dot_tn(qd, dob)
        c = cos_ref[...]
        s = sin_ref[...]
        dq_ref[...] = (dq * c + pltpu.roll(dq * s, HEAD_DIM // 2, 1)).astype(BF16)
        dk_ref[...] = (dk * c + pltpu.roll(dk * s, HEAD_DIM // 2, 1)).astype(BF16)

    oshape = jax.ShapeDtypeStruct((T, RET_WIDTH), BF16)
    ins = [sp[n] for n in ("q", "k", "v", "g", "tab", "tab", "mask", "dec", "dec", "gtb", "gn", "state", "rows")]
    outs = [sp["rows"], sp["rows"], sp["rows"], sp["rows"], sp["gn"]]
    return pl.pallas_call(
        _per_head(body, [kind for _, kind in ins + outs + [sp["scratch"]]]), name="ret_bwd",
        grid=(RET_HEADS // RET_HPS, nb), in_specs=[s for s, _ in ins], out_specs=[s for s, _ in outs],
        out_shape=[oshape, oshape, oshape, oshape, jax.ShapeDtypeStruct((1, RET_WIDTH), F32)],
        scratch_shapes=[sp["scratch"][0]],
        compiler_params=_params(2),
    )(proj, proj, proj, proj, cosf, sinf, mask, rowdec, kdec, gtb, gn, sblk, dyr)


def _in_proj_bwd(dproj, w, x, r1, g, dx2, tm, carry=None):
    T = x.shape[0]

    def body(dp_ref, w_ref, x_ref, r_ref, g_ref, dx2_ref, gx_ref, dg_ref, acc):
        i = pl.program_id(0)
        k = pl.program_id(1)
        part = _dot_nt(dp_ref[...], w_ref[...])

        @pl.when(k == 0)
        def _():
            acc[...] = part

        @pl.when(k > 0)
        def _():
            acc[...] += part

        @pl.when(k == N_DEV - 1)
        def _():
            dxn, dgt = _rms_bwd(acc[...], x_ref[...], r_ref[...], g_ref[...])
            gx_ref[...] = dx2_ref[...] + dxn

            @pl.when(i == 0)
            def _():
                dg_ref[...] = jnp.zeros_like(dg_ref)

            dg_ref[...] += jnp.sum(dgt, axis=0, keepdims=True)

    full = pl.BlockSpec((tm, D_MODEL), lambda i, k: (i, 0))
    vec = pl.BlockSpec((1, D_MODEL), lambda i, k: (0, 0))
    return _pcall(
        body, "in_proj_bwd", (T // tm, N_DEV),
        [pl.BlockSpec((tm, WIN_BLK), lambda i, k: (i, k)),
         pl.BlockSpec((None, D_MODEL, WIN_BLK), lambda i, k: (k, 0, 0)),
         full, pl.BlockSpec((tm, 1), lambda i, k: (i, 0)), vec, full],
        [full, vec],
        [jax.ShapeDtypeStruct((T, D_MODEL), F32), jax.ShapeDtypeStruct((1, D_MODEL), F32)],
        [pltpu.VMEM((tm, D_MODEL), F32)], (dproj, w, x, r1, g, dx2), carry)


def _in_proj_wgrad(h, dproj, tk, carry=None):
    T = h.shape[0]
    nk = T // tk

    def body(h_ref, dp_ref, o_ref, acc):
        k = pl.program_id(1)

        @pl.when(k == 0)
        def _():
            acc[...] = jnp.zeros_like(acc)

        acc[...] += _dot_tn(h_ref[...], dp_ref[...])

        @pl.when(k == nk - 1)
        def _():
            o_ref[...] = acc[...].astype(BF16)

    return _pcall(
        body, "in_proj_wgrad", (N_DEV, nk),
        [pl.BlockSpec((tk, D_MODEL), lambda j, k: (k, 0)), pl.BlockSpec((tk, WIN_BLK), lambda j, k: (k, j))],
        [pl.BlockSpec((None, D_MODEL, WIN_BLK), lambda j, k: (j, 0, 0))],
        [jax.ShapeDtypeStruct((N_DEV, D_MODEL, WIN_BLK), BF16)],
        [pltpu.VMEM((D_MODEL, WIN_BLK), F32)], (h, dproj), carry)


def _rope_tables(T):
    half = HEAD_DIM // 2
    freqs = ROPE_BASE ** (-jnp.arange(half, dtype=F32) / half)
    ang = jnp.arange(T, dtype=F32)[:, None] * freqs[None, :]
    c = jnp.cos(ang)
    s = jnp.sin(ang)
    return jnp.concatenate([c, c], axis=1), jnp.concatenate([-s, s], axis=1)


def _retention_tables():
    hh = jnp.arange(RET_HEADS, dtype=F32)
    log_g = jnp.log1p(-(2.0 ** (-5.0 - hh)))[:, None, None]
    i = jnp.arange(RET_BLOCK)
    ci = (i // CHUNK)[:, None]
    cj = (i // CHUNK)[None, :]
    diff = (i[:, None] - i[None, :]).astype(F32)
    expo = jnp.where(ci == cj, jnp.abs(diff), diff)
    mask = jnp.where((cj <= ci)[None], jnp.exp(log_g * expo[None]), 0.0)
    r = jnp.arange(RET_BLOCK, dtype=F32)[None, :, None]
    ones = jnp.ones((1, 1, HEAD_DIM), F32)
    rowdec = jnp.exp(log_g * (r + 1.0)) * ones
    kdec = jnp.exp(log_g * (RET_BLOCK - 1.0 - r)) * ones
    gtb = jnp.exp(log_g * float(RET_BLOCK)) * ones
    return mask, rowdec, kdec, gtb


def _s5_discretise(a_re, a_im, log_dt, b_re, b_im):
    lam = lax.complex(a_re, a_im)
    dt = jnp.exp(log_dt)[:, None]
    lam_bar = jnp.exp(lam * dt)
    b_bar = ((lam_bar - 1.0) / lam)[..., None] * lax.complex(b_re, b_im)
    return jnp.real(lam_bar), jnp.imag(lam_bar), jnp.real(b_bar), jnp.imag(b_bar)


def _to_state_blockdiag(m):
    eye = jnp.eye(S5_GB, dtype=m.dtype)
    t = jnp.einsum("bgpc,gh->bgchp", m.reshape(S5_NBLK, S5_GB, SSM_STATE, SSM_GROUP), eye)
    return t.reshape(S5_NBLK, LANE, S5_LANES)


def _from_state_blockdiag(m):
    eye = jnp.eye(S5_GB, dtype=m.dtype)
    t = jnp.einsum("bgcp,gh->bgphc", m.reshape(S5_NBLK, S5_GB, SSM_GROUP, SSM_STATE), eye)
    return t.reshape(S5_NBLK, S5_LANES, LANE)


def _diag_of_state_major(acc):
    eye = jnp.eye(S5_GB, dtype=acc.dtype)
    t = acc.reshape(S5_NBLK, S5_GB, SSM_STATE, S5_GB, SSM_GROUP)
    return jnp.einsum("bgphc,gh->bgpc", t, eye).reshape(SSM_GROUPS, SSM_STATE, SSM_GROUP)


def _diag_of_channel_major(acc):
    eye = jnp.eye(S5_GB, dtype=acc.dtype)
    t = acc.reshape(S5_NBLK, S5_GB, SSM_GROUP, S5_GB, SSM_STATE)
    return jnp.einsum("bgchp,gh->bgcp", t, eye).reshape(SSM_GROUPS, SSM_GROUP, SSM_STATE)


SMALL_PARTIALS = (("ret_gn_g", 1024), ("lam_re", 4096), ("lam_im", 4096),
                  ("bbar_re", 65536), ("bbar_im", 65536), ("c_re", 65536), ("c_im", 65536),
                  ("ssm_d", 1024), ("b_glu", 1024), ("out_g", 1024), ("norm_ffn_g", 2048), ("norm_final_g", 2048))


def _forward_backward(x, tgt, shards, sm, tm=512):
    T = x.shape[0]
    cosf, sinf = _rope_tables(T)
    mask, rowdec, kdec, gtb = _retention_tables()
    lbr, lbi, bbr, bbi = _s5_discretise(sm["ssm_a_re"], sm["ssm_a_im"], sm["ssm_log_dt"], sm["ssm_b_re"],
                                        sm["ssm_b_im"])
    bre = _to_state_blockdiag(bbr).astype(BF16)
    bim = _to_state_blockdiag(bbi).astype(BF16)
    cre_t = _from_state_blockdiag(sm["ssm_c_re"]).astype(BF16)
    cim_t = _from_state_blockdiag(sm["ssm_c_im"]).astype(BF16)
    bre_t = jnp.swapaxes(bre, 1, 2)
    bim_t = jnp.swapaxes(bim, 1, 2)
    cre = jnp.swapaxes(cre_t, 1, 2)
    cim = jnp.swapaxes(cim_t, 1, 2)
    lam = jnp.stack([lbr.reshape(S5_NBLK, S5_LANES), lbi.reshape(S5_NBLK, S5_LANES)], axis=1)
    pm = _step_major_permutation()
    pm_t = pm.T
    row = lambda v: v.reshape(1, -1)
    g_mix, g_ffn, g_fin = row(sm["norm_mix_g"]), row(sm["norm_ffn_g"]), row(sm["norm_final_g"])
    gn, dsk, bglu, og = row(sm["ret_gn_g"]), row(sm["ssm_d"]), row(sm["ssm_b_glu"]), row(sm["ssm_out_g"])

    w_in = _gather_once_per_chip("weight_gather", shards["w_in"])
    proj, h1, r1, w_glu, w_out = _in_proj_fwd(
        x, g_mix, w_in, 256, _Exchange([shards["ssm_w_glu"], shards["w_out"]], True))
    w_glu = w_glu.reshape(SSM_WIDTH, SSM_WIDTH)
    w_out = w_out.reshape(D_MODEL, D_MODEL)
    y_ret, sblk, w_gate = _ret_fwd(proj, cosf, sinf, mask, rowdec, kdec, gtb, gn,
                                   _Exchange([shards["w_gate"]], True))
    y_s5, bound, w_up = _s5_fwd(proj, pm, pm_t, bre, bim, cre_t, cim_t, lam, dsk, _Exchange([shards["w_up"]], True))
    z, y_ssm, r_ssm = _glu_fwd(y_s5, w_glu, bglu, og, 256)
    x2, h2, r2 = _out_proj_fwd(x, y_ret, y_ssm, w_out, g_ffn, 256)
    a, b, f, w_down = _ffn_up(h2, w_gate, w_up, tm, _Exchange([shards["w_down"]], True))
    dx3, dx3b, loss8, dg_fin = _ffn_down_loss(f, w_down, x2, tgt, g_fin, 256)

    landed = {}
    da, db = _ffn_bwd_act(dx3b, w_down, a, b, tm)
    dw_down = _ffn_wgrad_down(f, dx3b, tm)
    dw_gate, dw_up, landed["w_down"] = _ffn_wgrad_up(h2, da, db, tm, _Exchange([dw_down], False))
    dh2, landed["w_gate"], landed["w_up"] = _ffn_bwd_in(da, db, w_gate, w_up, min(1024, T),
                                                        _Exchange([dw_gate, dw_up], False))
    dx2, dx2b, dg_ffn, dy_ret, dy_ssm = _out_proj_bwd(dh2, x2, r2, g_ffn, dx3, w_out, 256)
    dw_out = jnp.concatenate([_wgrad_rows("out_proj_wgrad_ret", y_ret, dx2b, tm),
                              _wgrad_rows("out_proj_wgrad_ssm", y_ssm, dx2b, tm)], axis=0)
    dy_s5, dw_glu, db_glu, dog = _glu_bwd(y_s5, z, r_ssm, dy_ssm, w_glu, og, 256)
    du, dbre, dbim, dcre, dcim, dlam, dd = _s5_bwd(proj, dy_s5, bound, pm, pm_t, bre, bim, bre_t, bim_t, cre, cim,
                                                   lam, dsk)
    dq, dk, dv, dgate, dgn = _ret_bwd(proj, cosf, sinf, mask, rowdec, kdec, gtb, gn, sblk, dy_ret)
    dproj = jnp.concatenate([dq, dk, dv, dgate, du], axis=1)
    dw_in, landed["w_out"], landed["ssm_w_glu"] = _in_proj_wgrad(
        h1, dproj, tm, _Exchange([dw_out.reshape(N_DEV, D_MODEL // N_DEV, D_MODEL),
                                  dw_glu.astype(BF16).reshape(N_DEV, SSM_WIDTH // N_DEV, SSM_WIDTH)], False))
    small = dict(ret_gn_g=dgn, lam_re=dlam[:, 0], lam_im=dlam[:, 1],
                 bbar_re=_diag_of_state_major(dbre), bbar_im=_diag_of_state_major(dbim),
                 c_re=_diag_of_channel_major(dcre), c_im=_diag_of_channel_major(dcim),
                 ssm_d=dd, b_glu=db_glu, out_g=dog, norm_ffn_g=dg_ffn, norm_final_g=dg_fin)
    packed = _pack([small[n] for n, _ in SMALL_PARTIALS])
    grad_x, dg_mix, landed["w_in"], small_landed = _in_proj_bwd(
        dproj, w_in, x, r1, g_mix, dx2, tm, _Exchange([dw_in, packed], [False, True]))
    (mix_landed,) = _exchange_call("mix_gain_grad_gather", [_pack([dg_mix])], True)
    summed = dict(zip([n for n, _ in SMALL_PARTIALS],
                      _unpack(_sum_partials("small_grad_sum", small_landed), [(sz,) for _, sz in SMALL_PARTIALS])))
    summed["norm_mix_g"] = _sum_partials("mix_gain_grad_sum", mix_landed).reshape(-1)
    return loss8[0, 0], grad_x, landed, summed


def _small_grads(summed, sm):
    _, vjp = jax.vjp(_s5_discretise, sm["ssm_a_re"], sm["ssm_a_im"], sm["ssm_log_dt"], sm["ssm_b_re"], sm["ssm_b_im"])
    gp = (SSM_GROUPS, SSM_STATE)
    da_re, da_im, dlog_dt, db_re, db_im = vjp((summed["lam_re"].reshape(gp), summed["lam_im"].reshape(gp),
                                               summed["bbar_re"].reshape(gp + (SSM_GROUP,)),
                                               summed["bbar_im"].reshape(gp + (SSM_GROUP,))))
    return dict(norm_mix_g=summed["norm_mix_g"], ret_gn_g=summed["ret_gn_g"], ssm_a_re=da_re, ssm_a_im=da_im,
                ssm_log_dt=dlog_dt, ssm_b_re=db_re, ssm_b_im=db_im,
                ssm_c_re=summed["c_re"].reshape(SSM_GROUPS, SSM_GROUP, SSM_STATE),
                ssm_c_im=summed["c_im"].reshape(SSM_GROUPS, SSM_GROUP, SSM_STATE),
                ssm_d=summed["ssm_d"], ssm_b_glu=summed["b_glu"], ssm_out_g=summed["out_g"],
                norm_ffn_g=summed["norm_ffn_g"], norm_final_g=summed["norm_final_g"])


def _adamw_math(w, g, m, v):
    m2 = ADAM_B1 * m + (1.0 - ADAM_B1) * g
    v2 = ADAM_B2 * v + (1.0 - ADAM_B2) * (g * g)
    delta = -ADAM_LR * ((m2 / ADAM_BC1) / (jnp.sqrt(v2 / ADAM_BC2) + ADAM_EPS) + ADAM_WD * w)
    return delta, m2, v2


def _adamw_shard(name, parts, w, m, v, tr):
    rows, cols = w.shape

    def body(p_ref, w_ref, m_ref, v_ref, g_ref, d_ref, m2_ref, v2_ref):
        g = p_ref[0].astype(F32)
        for s in range(1, N_DEV):
            g = g + p_ref[s].astype(F32)
        d, m2, v2 = _adamw_math(w_ref[...], g, m_ref[...], v_ref[...])
        g_ref[...] = g
        d_ref[...] = d
        m2_ref[...] = m2
        v2_ref[...] = v2

    blk = pl.BlockSpec((tr, cols), lambda i: (i, 0))
    oshape = jax.ShapeDtypeStruct((rows, cols), F32)
    return pl.pallas_call(
        body, name=name, grid=(rows // tr,),
        in_specs=[pl.BlockSpec((N_DEV, tr, cols), lambda i: (0, i, 0)), blk, blk, blk],
        out_specs=[blk, blk, blk, blk], out_shape=[oshape] * 4,
        compiler_params=_params(1),
    )(parts, w, m, v)


def _sum_partials(name, parts):
    rows = parts.shape[1]

    def body(p_ref, o_ref):
        g = p_ref[0]
        for s in range(1, N_DEV):
            g = g + p_ref[s]
        o_ref[...] = g

    return pl.pallas_call(
        body, name=name, grid=(1,),
        in_specs=[pl.BlockSpec((N_DEV, rows, LANE), lambda i: (0, 0, 0))],
        out_specs=pl.BlockSpec((rows, LANE), lambda i: (0, 0)),
        out_shape=jax.ShapeDtypeStruct((rows, LANE), F32),
        compiler_params=_params(1),
    )(parts)


def _adamw_small(ws, gs, ms, vs):
    n = len(ws)

    def body(*refs):
        for i in range(n):
            w_ref, g_ref, m_ref, v_ref = (refs[k * n + i] for k in range(4))
            d_ref, m2_ref, v2_ref = (refs[(4 + k) * n + i] for k in range(3))
            d, m2, v2 = _adamw_math(w_ref[...], g_ref[...], m_ref[...], v_ref[...])
            d_ref[...] = d
            m2_ref[...] = m2
            v2_ref[...] = v2

    vmem = pl.BlockSpec(memory_space=pltpu.VMEM)
    out = pl.pallas_call(
        body, name="adamw_small", in_specs=[vmem] * (4 * n), out_specs=[vmem] * (3 * n),
        out_shape=[jax.ShapeDtypeStruct(w.shape, F32) for w in ws] * 3,
        compiler_params=pltpu.CompilerParams(vmem_limit_bytes=VMEM_LIMIT),
    )(*ws, *gs, *ms, *vs)
    return out[:n], out[n:2 * n], out[2 * n:]


def _pack(arrays):
    cols = []
    for a in arrays:
        flat = a.reshape(-1).astype(F32)
        pad = (-flat.shape[0]) % LANE
        cols.append(jnp.pad(flat, (0, pad)) if pad else flat)
    return jnp.concatenate(cols).reshape(-1, LANE)


def _unpack(packed, shapes):
    flat = packed.reshape(-1)
    out, off = [], 0
    for shp in shapes:
        n = math.prod(shp)
        out.append(flat[off:off + n].reshape(shp))
        off += n + ((-n) % LANE)
    return out


WEIGHTS = ("norm_mix_g", "w_in", "ret_gn_g", "ssm_a_re", "ssm_a_im", "ssm_log_dt", "ssm_b_re", "ssm_b_im",
           "ssm_c_re", "ssm_c_im", "ssm_d", "ssm_w_glu", "ssm_b_glu", "ssm_out_g", "w_out", "norm_ffn_g", "w_gate",
           "w_up", "w_down", "norm_final_g")
BIG = ("w_in", "ssm_w_glu", "w_out", "w_gate", "w_up", "w_down")
SMALL = tuple(n for n in WEIGHTS if n not in BIG)
ADAM_ROWS = {"w_in": 256, "ssm_w_glu": 128, "w_out": 128, "w_gate": 256, "w_up": 256, "w_down": 176}


def kernel(x, norm_mix_g, w_in, ret_gn_g, ssm_a_re, ssm_a_im, ssm_log_dt, ssm_b_re, ssm_b_im, ssm_c_re, ssm_c_im, ssm_d, ssm_w_glu, ssm_b_glu, ssm_out_g, w_out, norm_ffn_g, w_gate, w_up, w_down, norm_final_g, loss_target, m_norm_mix_g, m_w_in, m_ret_gn_g, m_ssm_a_re, m_ssm_a_im, m_ssm_log_dt, m_ssm_b_re, m_ssm_b_im, m_ssm_c_re, m_ssm_c_im, m_ssm_d, m_ssm_w_glu, m_ssm_b_glu, m_ssm_out_g, m_w_out, m_norm_ffn_g, m_w_gate, m_w_up, m_w_down, m_norm_final_g, v_norm_mix_g, v_w_in, v_ret_gn_g, v_ssm_a_re, v_ssm_a_im, v_ssm_log_dt, v_ssm_b_re, v_ssm_b_im, v_ssm_c_re, v_ssm_c_im, v_ssm_d, v_ssm_w_glu, v_ssm_b_glu, v_ssm_out_g, v_w_out, v_norm_ffn_g, v_w_gate, v_w_up, v_w_down, v_norm_final_g):
    given = dict(locals())
    w = {n: given[n] for n in WEIGHTS}
    m = {n: given["m_" + n] for n in WEIGHTS}
    v = {n: given["v_" + n] for n in WEIGHTS}
    drop = lambda n, a: a if n == "norm_final_g" else a[0]
    w0 = {n: drop(n, w[n]) for n in WEIGHTS}
    m0 = {n: drop(n, m[n]) for n in WEIGHTS}
    v0 = {n: drop(n, v[n]) for n in WEIGHTS}

    sm = {n: w0[n] for n in SMALL}
    shards = {n: w0[n].astype(BF16) for n in BIG}
    loss_local, grad_x, landed, summed = _forward_backward(x[0], loss_target[0], shards, sm)
    loss = lax.psum(loss_local, MESH_AXES)
    gsmall = _small_grads(summed, sm)

    grads, delta, new_m, new_v = {}, {}, {}, {}
    for n in BIG:
        g, d, m2, v2 = _adamw_shard("adamw_" + n, landed[n], w0[n], m0[n], v0[n], ADAM_ROWS[n])
        grads[n], delta[n], new_m[n], new_v[n] = g, d, m2, v2
    as_given = lambda n, a: a.reshape(1, -1) if n == "norm_final_g" else a.reshape(w[n].shape)
    gs = [as_given(n, gsmall[n]) for n in SMALL]
    ds, m2s, v2s = _adamw_small([as_given(n, w[n]) for n in SMALL], gs, [as_given(n, m[n]) for n in SMALL],
                                [as_given(n, v[n]) for n in SMALL])
    for n, g, d, m2, v2 in zip(SMALL, gs, ds, m2s, v2s):
        grads[n], delta[n], new_m[n], new_v[n] = g, d, m2, v2

    lift = lambda n, a: a.reshape(w[n].shape)
    return (loss, grad_x[None], *[lift(n, grads[n]) for n in WEIGHTS], *[lift(n, delta[n]) for n in WEIGHTS],
            *[lift(n, new_m[n]) for n in WEIGHTS], *[lift(n, new_v[n]) for n in WEIGHTS])
```

```python
import functools
import math

import jax
import jax.numpy as jnp
from jax import lax
from jax.experimental import pallas as pl
from jax.experimental.pallas import tpu as pltpu

F32 = jnp.float32
BF16 = jnp.bfloat16

D_MODEL = 2048
RET_WIDTH = 1024
RET_HEADS = 8
HEAD_DIM = 128
CHUNK = 64
SSM_WIDTH = 1024
SSM_GROUP = 16
SSM_GROUPS = 64
SSM_STATE = 64
D_FF = 5632
IN_WIDTH = 5120
ROPE_BASE = 10000.0
EPS = 1e-6
N_DEV = 8
MESH_AXES = ("x", "y", "c")

WIN_BLK = IN_WIDTH // N_DEV
FF_BLK = D_FF // N_DEV
RET_BLOCK = 256
RET_HPS = 2
S5_TILE = 256
S5_CHUNKS = 8
S5_STEPS = S5_TILE // S5_CHUNKS
S5_GB = 8
S5_NBLK = SSM_GROUPS // S5_GB
S5_LANES = S5_GB * SSM_STATE
LANE = 128

ADAM_LR = 0.001
ADAM_B1 = 0.9
ADAM_B2 = 0.999
ADAM_EPS = 1e-08
ADAM_WD = 0.01
ADAM_STEP = 10
ADAM_BC1 = 1.0 - ADAM_B1 ** ADAM_STEP
ADAM_BC2 = 1.0 - ADAM_B2 ** ADAM_STEP

VMEM_LIMIT = 56 * 1024 * 1024

NT = (((1,), (1,)), ((), ()))
TN = (((0,), (0,)), ((), ()))


def _params(n_grid):
    return pltpu.CompilerParams(dimension_semantics=("arbitrary",) * n_grid, vmem_limit_bytes=VMEM_LIMIT)


def _dot(a, b):
    return jnp.dot(a, b, preferred_element_type=F32)


def _dot_nt(a, b):
    return lax.dot_general(a, b, NT, preferred_element_type=F32)


def _dot_tn(a, b):
    return lax.dot_general(a, b, TN, preferred_element_type=F32)


def _sigmoid(x):
    return 1.0 / (1.0 + jnp.exp(-x))


_GELU_C = math.sqrt(2.0 / math.pi)
_GELU_A = 0.044715


def _gelu(x):
    t = jnp.tanh(_GELU_C * (x + _GELU_A * x * x * x))
    return 0.5 * x * (1.0 + t)


def _gelu_and_grad(x):
    t = jnp.tanh(_GELU_C * (x + _GELU_A * x * x * x))
    g = 0.5 * (1.0 + t) + 0.5 * x * (1.0 - t * t) * _GELU_C * (1.0 + 3.0 * _GELU_A * x * x)
    return 0.5 * x * (1.0 + t), g


def _rms_bwd(dy, x, r, g):
    w = dy * g
    dx = r * w - x * (r * r * r) * jnp.mean(w * x, axis=-1, keepdims=True)
    return dx, dy * x * r


HBM_SPEC = pl.BlockSpec(memory_space=pltpu.HBM)
ANY_SPEC = pl.BlockSpec(memory_space=pl.ANY)


def _load_resident(src_hbm, dst_vmem, sem):
    cp = pltpu.make_async_copy(src_hbm, dst_vmem, sem)
    cp.start()
    cp.wait()


def _my_block():
    return 4 * lax.axis_index("x") + 2 * lax.axis_index("y") + lax.axis_index("c")


def _peer(k):
    px = lax.axis_index("x") ^ ((k >> 2) & 1)
    py = lax.axis_index("y") ^ ((k >> 1) & 1)
    pc = lax.axis_index("c") ^ (k & 1)
    return (px, py, pc), 4 * px + 2 * py + pc


class _Exchange:
    def __init__(self, payloads, gather):
        self.payloads = list(payloads)
        self.n = len(self.payloads)
        self.gather = [gather] * self.n if isinstance(gather, bool) else list(gather)

    def out_shape(self):
        return [jax.ShapeDtypeStruct(((N_DEV,) if g else ()) + p.shape, p.dtype)
                for p, g in zip(self.payloads, self.gather)]

    def scratch_shapes(self):
        return [pltpu.SemaphoreType.DMA((self.n, N_DEV - 1)), pltpu.SemaphoreType.DMA((self.n, N_DEV - 1)),
                pltpu.SemaphoreType.DMA((self.n,))]

    def _copies(self, ins, outs, sems, incoming):
        send_sems, recv_sems, local_sems = sems
        me = _my_block()
        src_of = lambda i, blk: ins[i] if self.gather[i] else ins[i].at[blk]
        local, remote = [], []
        for i in range(self.n):
            if not incoming:
                local.append(pltpu.make_async_copy(src_of(i, me), outs[i].at[me], local_sems.at[i]))
            for k in range(1, N_DEV):
                dev, blk = _peer(k)
                src, dst = (outs[i].at[blk], outs[i].at[blk]) if incoming else (src_of(i, blk), outs[i].at[me])
                remote.append(pltpu.make_async_remote_copy(
                    src_ref=src, dst_ref=dst, send_sem=send_sems.at[i, k - 1], recv_sem=recv_sems.at[i, k - 1],
                    device_id=dev, device_id_type=pl.DeviceIdType.MESH))
        return local, remote

    def start(self, ins, outs, sems):
        local, sends = self._copies(ins, outs, sems, False)
        for cp in local + sends:
            cp.start()

    def wait(self, ins, outs, sems):
        for cp in self._copies(ins, outs, sems, True)[1]:
            cp.wait_recv()
        local, sends = self._copies(ins, outs, sems, False)
        for cp in sends:
            cp.wait_send()
        for cp in local:
            cp.wait()


def _pcall(body, name, grid, in_specs, out_specs, out_shape, scratch_shapes, args, carry=None):
    n_in, n_out, n_scr = len(in_specs), len(out_specs), len(scratch_shapes)
    if carry is None:
        return pl.pallas_call(body, name=name, grid=grid, in_specs=in_specs, out_specs=out_specs, out_shape=out_shape,
                              scratch_shapes=scratch_shapes, compiler_params=_params(len(grid)))(*args)
    nx = carry.n

    def wrapped(*refs):
        cin, xin = refs[:n_in], refs[n_in:n_in + nx]
        cout, xout = refs[n_in + nx:n_in + nx + n_out], refs[n_in + nx + n_out:n_in + 2 * nx + n_out]
        rest = refs[n_in + 2 * nx + n_out:]
        cscr, sems = rest[:n_scr], rest[n_scr:]
        first = functools.reduce(jnp.logical_and, [pl.program_id(a) == 0 for a in range(len(grid))])
        last = functools.reduce(jnp.logical_and, [pl.program_id(a) == grid[a] - 1 for a in range(len(grid))])

        @pl.when(first)
        def _():
            carry.start(xin, xout, sems)

        body(*cin, *cout, *cscr)

        @pl.when(last)
        def _():
            carry.wait(xin, xout, sems)

    return pl.pallas_call(
        wrapped, name=name, grid=grid, in_specs=list(in_specs) + [HBM_SPEC] * nx,
        out_specs=list(out_specs) + [HBM_SPEC] * nx, out_shape=list(out_shape) + carry.out_shape(),
        scratch_shapes=list(scratch_shapes) + carry.scratch_shapes(), compiler_params=_params(len(grid)),
    )(*args, *carry.payloads)


def _exchange_call(name, payloads, gather):
    ex = _Exchange(payloads, gather)

    def body(*refs):
        ins, outs, sems = refs[:ex.n], refs[ex.n:2 * ex.n], refs[2 * ex.n:]
        ex.start(ins, outs, sems)
        ex.wait(ins, outs, sems)

    return pl.pallas_call(body, name=name, in_specs=[HBM_SPEC] * ex.n, out_specs=[HBM_SPEC] * ex.n,
                          out_shape=ex.out_shape(), scratch_shapes=ex.scratch_shapes())(*ex.payloads)


def _gather_once_per_chip(name, shard):
    def body(src, out, send_sems, recv_sems, local_sem):
        x, y, c = lax.axis_index("x"), lax.axis_index("y"), lax.axis_index("c")
        me, sibling = (x, y, c), (x, y, 1 - c)
        chips = [(1 - x, y), (x, 1 - y), (1 - x, 1 - y)]
        slot = lambda px, py, pc: out.at[4 * px + 2 * py + pc]

        def copy(k, block, to, from_src=False):
            return pltpu.make_async_remote_copy(
                src_ref=src if from_src else slot(*block), dst_ref=slot(*block), send_sem=send_sems.at[k],
                recv_sem=recv_sems.at[k], device_id=to, device_id_type=pl.DeviceIdType.MESH)

        mine = pltpu.make_async_copy(src, slot(*me), local_sem)
        mine.start()
        first = [copy(0, me, sibling, True)] + [copy(1 + j, me, (*chip, c), True) for j, chip in enumerate(chips)]
        for cp in first:
            cp.start()
        passed = [copy(4 + j, (*chip, c), sibling) for j, chip in enumerate(chips)]
        for j, chip in enumerate(chips):
            copy(1 + j, (*chip, c), me).wait_recv()
            passed[j].start()
        copy(0, sibling, me).wait_recv()
        for j, chip in enumerate(chips):
            copy(4 + j, (*chip, 1 - c), me).wait_recv()
        for cp in first + passed:
            cp.wait_send()
        mine.wait()

    return pl.pallas_call(
        body, name=name, in_specs=[HBM_SPEC], out_specs=HBM_SPEC,
        out_shape=jax.ShapeDtypeStruct((N_DEV,) + shard.shape, shard.dtype),
        scratch_shapes=[pltpu.SemaphoreType.DMA((N_DEV - 1,)), pltpu.SemaphoreType.DMA((N_DEV - 1,)),
                        pltpu.SemaphoreType.DMA],
    )(shard)


def _in_proj_fwd(x, g, w, tm, carry=None):
    T = x.shape[0]

    def body(x_ref, g_ref, w_hbm, proj_ref, h_ref, r_ref, w_ref, sem):
        @pl.when(pl.program_id(0) == 0)
        def _():
            _load_resident(w_hbm, w_ref, sem)

        xf = x_ref[...]
        r = lax.rsqrt(jnp.mean(xf * xf, axis=-1, keepdims=True) + EPS)
        h = (xf * r * g_ref[...]).astype(BF16)
        h_ref[...] = h
        r_ref[...] = r
        for j in range(N_DEV):
            proj_ref[:, j * WIN_BLK:(j + 1) * WIN_BLK] = _dot(h, w_ref[j])

    return _pcall(
        body, "in_proj_fwd", (T // tm,),
        [pl.BlockSpec((tm, D_MODEL), lambda i: (i, 0)), pl.BlockSpec((1, D_MODEL), lambda i: (0, 0)), ANY_SPEC],
        [pl.BlockSpec((tm, IN_WIDTH), lambda i: (i, 0)),
         pl.BlockSpec((tm, D_MODEL), lambda i: (i, 0)),
         pl.BlockSpec((tm, 1), lambda i: (i, 0))],
        [jax.ShapeDtypeStruct((T, IN_WIDTH), F32),
         jax.ShapeDtypeStruct((T, D_MODEL), BF16),
         jax.ShapeDtypeStruct((T, 1), F32)],
        [pltpu.VMEM(w.shape, w.dtype), pltpu.SemaphoreType.DMA], (x, g, w), carry)


def _ret_common(q_ref, k_ref, v_ref, cos_ref, sin_ref, mask_ref, rd_ref, sin_state):
    c = cos_ref[...]
    s = sin_ref[...]
    q = q_ref[...]
    q = q * c + pltpu.roll(q, HEAD_DIM // 2, 1) * s
    k = k_ref[...]
    k = (k * c + pltpu.roll(k, HEAD_DIM // 2, 1) * s) * (HEAD_DIM ** -0.5)
    qb = q.astype(BF16)
    kb = k.astype(BF16)
    vb = v_ref[...].astype(BF16)
    pm = (_dot_nt(qb, kb) * mask_ref[...]).astype(BF16)
    qd = (q * rd_ref[...]).astype(BF16)
    o = _dot(pm, vb) + _dot(qd, sin_state.astype(BF16))
    return q, k, qb, kb, vb, pm, qd, o


def _ret_specs(T, rev):
    nb = T // RET_BLOCK
    groups = RET_HEADS // RET_HPS
    wide = RET_HPS * HEAD_DIM
    blk = (lambda b: nb - 1 - b) if rev else (lambda b: b)
    col = lambda piece: (pl.BlockSpec((RET_BLOCK, wide), lambda h, b: (blk(b), piece * groups + h)), "lane")
    return dict(
        q=col(0), k=col(1), v=col(2), g=col(3),
        tab=(pl.BlockSpec((RET_BLOCK, HEAD_DIM), lambda h, b: (blk(b), 0)), None),
        mask=(pl.BlockSpec((RET_HPS, RET_BLOCK, RET_BLOCK), lambda h, b: (h, 0, 0)), "lead"),
        dec=(pl.BlockSpec((RET_HPS, RET_BLOCK, HEAD_DIM), lambda h, b: (h, 0, 0)), "lead"),
        gtb=(pl.BlockSpec((RET_HPS, 1, HEAD_DIM), lambda h, b: (h, 0, 0)), "lead"),
        gn=(pl.BlockSpec((1, wide), lambda h, b: (0, h)), "lane"),
        state=(pl.BlockSpec((RET_HPS, None, HEAD_DIM, HEAD_DIM), lambda h, b: (h, blk(b), 0, 0)), "lead"),
        rows=(pl.BlockSpec((RET_BLOCK, wide), lambda h, b: (blk(b), h)), "lane"),
        scratch=(pltpu.VMEM((RET_HPS, HEAD_DIM, HEAD_DIM), F32), "lead"),
    )


def _per_head(head_body, kinds):
    def body(*refs):
        for hh in range(RET_HPS):
            views = []
            for ref, kind in zip(refs, kinds):
                if kind == "lane":
                    views.append(ref.at[:, hh * HEAD_DIM:(hh + 1) * HEAD_DIM])
                elif kind == "lead":
                    views.append(ref.at[hh])
                else:
                    views.append(ref)
            head_body(*views)
    return body


def _ret_fwd(proj, cosf, sinf, mask, rowdec, kdec, gtb, gn, carry=None):
    T = proj.shape[0]
    nb = T // RET_BLOCK
    sp = _ret_specs(T, False)

    def body(q_ref, k_ref, v_ref, g_ref, cos_ref, sin_ref, mask_ref, rd_ref, kd_ref, gtb_ref, gn_ref,
             y_ref, sb_ref, st):
        @pl.when(pl.program_id(1) == 0)
        def _():
            st[...] = jnp.zeros_like(st)
        s_in = st[...]
        sb_ref[...] = s_in
        q, k, qb, kb, vb, pm, qd, o = _ret_common(q_ref, k_ref, v_ref, cos_ref, sin_ref, mask_ref, rd_ref, s_in)
        st[...] = gtb_ref[...] * s_in + _dot_tn((k * kd_ref[...]).astype(BF16), vb)
        mu = jnp.mean(o, axis=-1, keepdims=True)
        oc = o - mu
        n = oc * lax.rsqrt(jnp.mean(oc * oc, axis=-1, keepdims=True) + EPS)
        gt = g_ref[...]
        y_ref[...] = (gt * _sigmoid(gt) * (n * gn_ref[...])).astype(BF16)

    ins = [sp[n] for n in ("q", "k", "v", "g", "tab", "tab", "mask", "dec", "dec", "gtb", "gn")]
    outs = [sp["rows"], sp["state"]]
    return _pcall(
        _per_head(body, [kind for _, kind in ins + outs + [sp["scratch"]]]), "ret_fwd", (RET_HEADS // RET_HPS, nb),
        [s for s, _ in ins], [s for s, _ in outs],
        [jax.ShapeDtypeStruct((T, RET_WIDTH), BF16),
         jax.ShapeDtypeStruct((RET_HEADS, nb, HEAD_DIM, HEAD_DIM), F32)],
        [sp["scratch"][0]],
        (proj, proj, proj, proj, cosf, sinf, mask, rowdec, kdec, gtb, gn), carry)


def _scan(re, im, ar, ai, reverse):
    n = re.shape[0]
    row = lax.broadcasted_iota(jnp.int32, re.shape, 0)
    s = 1
    while s < n:
        if reverse:
            keep = row < n - s
            sr = jnp.where(keep, pltpu.roll(re, n - s, 0), 0.0)
            si = jnp.where(keep, pltpu.roll(im, n - s, 0), 0.0)
        else:
            keep = row >= s
            sr = jnp.where(keep, pltpu.roll(re, s, 0), 0.0)
            si = jnp.where(keep, pltpu.roll(im, s, 0), 0.0)
        re, im = re + ar * sr - ai * si, im + ar * si + ai * sr
        ar, ai = ar * ar - ai * ai, 2.0 * ar * ai
        s *= 2
    return re, im


S5_STATE_TILE = (S5_TILE, S5_LANES)


def _step_major_permutation():
    r = jnp.arange(S5_TILE)
    t_of_row = (r % S5_CHUNKS) * S5_STEPS + r // S5_CHUNKS
    return (t_of_row[:, None] == r[None, :]).astype(BF16)


def _permute_rows_f32(pm, x):
    hi = x.astype(BF16)
    rest = x - hi.astype(F32)
    mid = rest.astype(BF16)
    lo = (rest - mid.astype(F32)).astype(BF16)
    return _dot(pm, hi) + _dot(pm, mid) + _dot(pm, lo)


def _step_get(ref, j):
    return ref[j * S5_CHUNKS:(j + 1) * S5_CHUNKS, :]


def _step_set(ref, j, val):
    ref[j * S5_CHUNKS:(j + 1) * S5_CHUNKS, :] = val


def _tile_get(ref):
    return ref[...]


def _tile_set(ref, val):
    ref[...] = val


def _fill_power_table(ptab, lr, li):
    shape = (S5_CHUNKS, S5_LANES)
    lrb = jnp.broadcast_to(lr, shape)
    lib = jnp.broadcast_to(li, shape)
    pr, pi_ = lrb, lib
    for j in range(S5_STEPS):
        ptab[0, j * S5_CHUNKS:(j + 1) * S5_CHUNKS, :] = pr
        ptab[1, j * S5_CHUNKS:(j + 1) * S5_CHUNKS, :] = pi_
        pr, pi_ = lrb * pr - lib * pi_, lrb * pi_ + lib * pr


def _chunk_scans(xr, xi, lr, li, reverse):
    shape = (S5_CHUNKS, S5_LANES)
    lrb = jnp.broadcast_to(lr, shape)
    lib = jnp.broadcast_to(li, shape)
    sr = si = None
    for j in (range(S5_STEPS - 1, -1, -1) if reverse else range(S5_STEPS)):
        vr = _step_get(xr, j)
        vi = _step_get(xi, j)
        if sr is not None:
            vr, vi = vr + lrb * sr - lib * si, vi + lrb * si + lib * sr
            _step_set(xr, j, vr)
            _step_set(xi, j, vi)
        sr, si = vr, vi
    return sr, si


def _entering_states(zr, zi, cr, ci, ar, ai, reverse):
    shape = (S5_CHUNKS, S5_LANES)
    row = lax.broadcasted_iota(jnp.int32, shape, 0)
    if reverse:
        edge, shift = row == S5_CHUNKS - 1, S5_CHUNKS - 1
    else:
        edge, shift = row == 0, 1
    wr = jnp.where(edge, jnp.broadcast_to(cr, shape), pltpu.roll(zr, shift, 0))
    wi = jnp.where(edge, jnp.broadcast_to(ci, shape), pltpu.roll(zi, shift, 0))
    return _scan(wr, wi, ar, ai, reverse)


def _table_rows(ptab, j, conj):
    pr = ptab[0, j * S5_CHUNKS:(j + 1) * S5_CHUNKS, :]
    pi_ = ptab[1, j * S5_CHUNKS:(j + 1) * S5_CHUNKS, :]
    return pr, (-pi_ if conj else pi_)


def _s5_forward_states(xr, xi, lr, li, cr, ci, ptab):
    zr, zi = _chunk_scans(xr, xi, lr, li, False)
    ar, ai = _table_rows(ptab, S5_STEPS - 1, False)
    er, ei = _entering_states(zr, zi, cr, ci, ar, ai, False)
    for j in range(S5_STEPS):
        pr, pi_ = _table_rows(ptab, j, False)
        _step_set(xr, j, _step_get(xr, j) + pr * er - pi_ * ei)
        _step_set(xi, j, _step_get(xi, j) + pr * ei + pi_ * er)
    last = S5_CHUNKS - 1
    end_r = (ar * er - ai * ei + zr)[last:last + 1, :]
    end_i = (ar * ei + ai * er + zi)[last:last + 1, :]
    return er, ei, end_r, end_i


def _s5_specs(T, rev):
    nt = T // S5_TILE
    tt = (lambda t: nt - 1 - t) if rev else (lambda t: t)
    return dict(
        u=pl.BlockSpec((S5_TILE, LANE), lambda b, t: (tt(t), 4 * RET_HEADS + b)),
        rows=pl.BlockSpec((S5_TILE, LANE), lambda b, t: (tt(t), b)),
        to_state=pl.BlockSpec((None, LANE, S5_LANES), lambda b, t: (b, 0, 0)),
        from_state=pl.BlockSpec((None, S5_LANES, LANE), lambda b, t: (b, 0, 0)),
        lam=pl.BlockSpec((None, 2, S5_LANES), lambda b, t: (b, 0, 0)),
        d=pl.BlockSpec((1, LANE), lambda b, t: (0, b)),
        perm=pl.BlockSpec((S5_TILE, S5_TILE), lambda b, t: (0, 0)),
        bound=pl.BlockSpec((None, None, 2, S5_LANES), lambda b, t: (b, tt(t), 0, 0)),
    )


def _s5_fwd(proj, pm, pm_t, bre, bim, cre_t, cim_t, lam, d, carry=None):
    T = proj.shape[0]
    nt = T // S5_TILE
    sp = _s5_specs(T, False)

    def body(u_ref, pm_ref, pmt_ref, bre_ref, bim_ref, cre_ref, cim_ref, lam_ref, d_ref, y_ref, bound_ref,
             carry, ptab, xr, xi):
        lr = lam_ref[0:1, :]
        li = lam_ref[1:2, :]

        @pl.when(pl.program_id(1) == 0)
        def _():
            carry[...] = jnp.zeros_like(carry)
            _fill_power_table(ptab, lr, li)

        u = _permute_rows_f32(pm_ref[...], u_ref[...])
        ub = u.astype(BF16)
        _tile_set(xr, _dot(ub, bre_ref[...]))
        _tile_set(xi, _dot(ub, bim_ref[...]))
        bound_ref[...] = carry[...]
        _, _, end_r, end_i = _s5_forward_states(xr, xi, lr, li, carry[0:1, :], carry[1:2, :], ptab)
        carry[0:1, :] = end_r
        carry[1:2, :] = end_i
        y = (_dot(_tile_get(xr).astype(BF16), cre_ref[...]) - _dot(_tile_get(xi).astype(BF16), cim_ref[...])
             + d_ref[...] * u)
        y_ref[...] = _permute_rows_f32(pmt_ref[...], y)

    state = pltpu.VMEM(S5_STATE_TILE, F32)
    return _pcall(
        body, "s5_fwd", (S5_NBLK, nt),
        [sp["u"], sp["perm"], sp["perm"], sp["to_state"], sp["to_state"], sp["from_state"],
         sp["from_state"], sp["lam"], sp["d"]],
        [sp["rows"], sp["bound"]],
        [jax.ShapeDtypeStruct((T, SSM_WIDTH), F32),
         jax.ShapeDtypeStruct((S5_NBLK, nt, 2, S5_LANES), F32)],
        [pltpu.VMEM((2, S5_LANES), F32), pltpu.VMEM((2, S5_TILE, S5_LANES), F32), state, state],
        (proj, pm, pm_t, bre, bim, cre_t, cim_t, lam, d), carry)


def _glu_fwd(y, w, b, og, tm):
    T = y.shape[0]

    def body(y_ref, w_ref, b_ref, og_ref, z_ref, o_ref, r_ref):
        y1 = _gelu(y_ref[...])
        z = _dot(y1.astype(BF16), w_ref[...]) + b_ref[...]
        y2 = y1 * _sigmoid(z)
        r = lax.rsqrt(jnp.mean(y2 * y2, axis=-1, keepdims=True) + EPS)
        z_ref[...] = z
        o_ref[...] = (y2 * r * og_ref[...]).astype(BF16)
        r_ref[...] = r

    row = pl.BlockSpec((tm, SSM_WIDTH), lambda i: (i, 0))
    vec = pl.BlockSpec((1, SSM_WIDTH), lambda i: (0, 0))
    return pl.pallas_call(
        body, name="glu_fwd", grid=(T // tm,),
        in_specs=[row, pl.BlockSpec((SSM_WIDTH, SSM_WIDTH), lambda i: (0, 0)), vec, vec],
        out_specs=[row, row, pl.BlockSpec((tm, 1), lambda i: (i, 0))],
        out_shape=[jax.ShapeDtypeStruct((T, SSM_WIDTH), F32), jax.ShapeDtypeStruct((T, SSM_WIDTH), BF16),
                   jax.ShapeDtypeStruct((T, 1), F32)],
        compiler_params=_params(1),
    )(y, w, b, og)


def _out_proj_fwd(x, y_ret, y_ssm, w, g, tm):
    T = x.shape[0]

    def body(x_ref, a_ref, b_ref, w_ref, g_ref, x2_ref, h_ref, r_ref):
        x2 = x_ref[...] + _dot(a_ref[...], w_ref[0:RET_WIDTH, :]) + _dot(b_ref[...], w_ref[RET_WIDTH:D_MODEL, :])
        r = lax.rsqrt(jnp.mean(x2 * x2, axis=-1, keepdims=True) + EPS)
        x2_ref[...] = x2
        h_ref[...] = (x2 * r * g_ref[...]).astype(BF16)
        r_ref[...] = r

    full = pl.BlockSpec((tm, D_MODEL), lambda i: (i, 0))
    half = pl.BlockSpec((tm, RET_WIDTH), lambda i: (i, 0))
    return pl.pallas_call(
        body, name="out_proj_fwd", grid=(T // tm,),
        in_specs=[full, half, half, pl.BlockSpec((D_MODEL, D_MODEL), lambda i: (0, 0)),
                  pl.BlockSpec((1, D_MODEL), lambda i: (0, 0))],
        out_specs=[full, full, pl.BlockSpec((tm, 1), lambda i: (i, 0))],
        out_shape=[jax.ShapeDtypeStruct((T, D_MODEL), F32), jax.ShapeDtypeStruct((T, D_MODEL), BF16),
                   jax.ShapeDtypeStruct((T, 1), F32)],
        compiler_params=_params(1),
    )(x, y_ret, y_ssm, w, g)


def _ffn_up(h, wg, wu, tm, carry=None):
    T = h.shape[0]

    def body(h_ref, wg_ref, wu_ref, a_ref, b_ref, f_ref):
        hb = h_ref[...]
        a = _dot(hb, wg_ref[...])
        b = _dot(hb, wu_ref[...])
        a_ref[...] = a.astype(BF16)
        b_ref[...] = b.astype(BF16)
        f_ref[...] = (a * _sigmoid(a) * b).astype(BF16)

    wspec = pl.BlockSpec((None, D_MODEL, FF_BLK), lambda j, i: (j, 0, 0))
    ospec = pl.BlockSpec((None, tm, FF_BLK), lambda j, i: (j, i, 0))
    oshape = jax.ShapeDtypeStruct((N_DEV, T, FF_BLK), BF16)
    return _pcall(
        body, "ffn_up", (N_DEV, T // tm),
        [pl.BlockSpec((tm, D_MODEL), lambda j, i: (i, 0)), wspec, wspec],
        [ospec, ospec, ospec], [oshape, oshape, oshape], [], (h, wg, wu), carry)


def _ffn_down_loss(f, wd, x2, tgt, g, tm):
    T = x2.shape[0]

    def body(f_ref, w_hbm, x2_ref, t_ref, g_ref, dx_ref, dxb_ref, loss_ref, dg_ref, w_ref, sem):
        i = pl.program_id(0)

        @pl.when(i == 0)
        def _():
            _load_resident(w_hbm, w_ref, sem)
            loss_ref[...] = jnp.zeros_like(loss_ref)
            dg_ref[...] = jnp.zeros_like(dg_ref)

        x3 = x2_ref[...]
        for k in range(N_DEV):
            x3 = x3 + _dot(f_ref[k], w_ref[k])
        gv = g_ref[...]
        r = lax.rsqrt(jnp.mean(x3 * x3, axis=-1, keepdims=True) + EPS)
        err = x3 * r * gv - t_ref[...]
        tile_loss = 0.5 * jnp.sum(jnp.mean(err * err, axis=-1, keepdims=True), axis=0, keepdims=True)
        dx, dgt = _rms_bwd(err * (1.0 / D_MODEL), x3, r, gv)
        dx_ref[...] = dx
        dxb_ref[...] = dx.astype(BF16)
        loss_ref[...] += jnp.broadcast_to(tile_loss, loss_ref.shape)
        dg_ref[...] += jnp.sum(dgt, axis=0, keepdims=True)

    full = pl.BlockSpec((tm, D_MODEL), lambda i: (i, 0))
    vec = pl.BlockSpec((1, D_MODEL), lambda i: (0, 0))
    return pl.pallas_call(
        body, name="ffn_down_loss", grid=(T // tm,),
        in_specs=[pl.BlockSpec((N_DEV, tm, FF_BLK), lambda i: (0, i, 0)), ANY_SPEC, full, full, vec],
        out_specs=[full, full, pl.BlockSpec((8, LANE), lambda i: (0, 0)), vec],
        out_shape=[jax.ShapeDtypeStruct((T, D_MODEL), F32), jax.ShapeDtypeStruct((T, D_MODEL), BF16),
                   jax.ShapeDtypeStruct((8, LANE), F32), jax.ShapeDtypeStruct((1, D_MODEL), F32)],
        scratch_shapes=[pltpu.VMEM(wd.shape, wd.dtype), pltpu.SemaphoreType.DMA],
        compiler_params=_params(1),
    )(f, wd, x2, tgt, g)


def _ffn_bwd_act(dxb, wd_t, a, b, tm):
    T = dxb.shape[0]

    def body(dx_ref, w_ref, a_ref, b_ref, da_ref, db_ref):
        df = _dot(dx_ref[...], w_ref[...])
        a = a_ref[...].astype(F32)
        b = b_ref[...].astype(F32)
        sg = _sigmoid(a)
        da_ref[...] = (df * b * sg * (1.0 + a * (1.0 - sg))).astype(BF16)
        db_ref[...] = (df * a * sg).astype(BF16)

    blk = pl.BlockSpec((None, tm, FF_BLK), lambda j, i: (j, i, 0))
    oshape = jax.ShapeDtypeStruct((N_DEV, T, FF_BLK), BF16)
    return pl.pallas_call(
        body, name="ffn_bwd_act", grid=(N_DEV, T // tm),
        in_specs=[pl.BlockSpec((tm, D_MODEL), lambda j, i: (i, 0)),
                  pl.BlockSpec((None, D_MODEL, FF_BLK), lambda j, i: (j, 0, 0)), blk, blk],
        out_specs=[blk, blk], out_shape=[oshape, oshape],
        compiler_params=_params(2),
    )(dxb, wd_t, a, b)


def _ffn_bwd_in(da, db, wg_t, wu_t, tm, carry=None):
    T = da.shape[1]

    def body(da_ref, db_ref, wg_ref, wu_ref, dh_ref):
        part = _dot(da_ref[...], wg_ref[...]) + _dot(db_ref[...], wu_ref[...])

        @pl.when(pl.program_id(1) == 0)
        def _():
            dh_ref[...] = part

        @pl.when(pl.program_id(1) > 0)
        def _():
            dh_ref[...] += part

    ablk = pl.BlockSpec((None, tm, FF_BLK), lambda i, k: (k, i, 0))
    wblk = pl.BlockSpec((None, FF_BLK, D_MODEL), lambda i, k: (k, 0, 0))
    return _pcall(
        body, "ffn_bwd_in", (T // tm, N_DEV), [ablk, ablk, wblk, wblk],
        [pl.BlockSpec((tm, D_MODEL), lambda i, k: (i, 0))], [jax.ShapeDtypeStruct((T, D_MODEL), F32)],
        [], (da, db, wg_t, wu_t), carry)


def _ffn_wgrad_up(h, da, db, tk, carry=None):
    T = h.shape[0]
    nk = T // tk

    def body(h_ref, da_ref, db_ref, g_ref, u_ref, accg, accu):
        k = pl.program_id(1)

        @pl.when(k == 0)
        def _():
            accg[...] = jnp.zeros_like(accg)
            accu[...] = jnp.zeros_like(accu)

        hb = h_ref[...]
        accg[...] += _dot_tn(hb, da_ref[...])
        accu[...] += _dot_tn(hb, db_ref[...])

        @pl.when(k == nk - 1)
        def _():
            g_ref[...] = accg[...].astype(BF16)
            u_ref[...] = accu[...].astype(BF16)

    blk = pl.BlockSpec((None, tk, FF_BLK), lambda j, k: (j, k, 0))
    ospec = pl.BlockSpec((None, D_MODEL, FF_BLK), lambda j, k: (j, 0, 0))
    oshape = jax.ShapeDtypeStruct((N_DEV, D_MODEL, FF_BLK), BF16)
    return _pcall(
        body, "ffn_wgrad_up", (N_DEV, nk),
        [pl.BlockSpec((tk, D_MODEL), lambda j, k: (k, 0)), blk, blk],
        [ospec, ospec], [oshape, oshape],
        [pltpu.VMEM((D_MODEL, FF_BLK), F32), pltpu.VMEM((D_MODEL, FF_BLK), F32)], (h, da, db), carry)


def _ffn_wgrad_down(f, dxb, tk):
    T = dxb.shape[0]
    nk = T // tk

    def body(f_ref, dx_ref, o_ref, acc):
        k = pl.program_id(1)

        @pl.when(k == 0)
        def _():
            acc[...] = jnp.zeros_like(acc)

        acc[...] += _dot_tn(f_ref[...], dx_ref[...])

        @pl.when(k == nk - 1)
        def _():
            o_ref[...] = acc[...].astype(BF16)

    return pl.pallas_call(
        body, name="ffn_wgrad_down", grid=(N_DEV, nk),
        in_specs=[pl.BlockSpec((None, tk, FF_BLK), lambda j, k: (j, k, 0)),
                  pl.BlockSpec((tk, D_MODEL), lambda j, k: (k, 0))],
        out_specs=pl.BlockSpec((None, FF_BLK, D_MODEL), lambda j, k: (j, 0, 0)),
        out_shape=jax.ShapeDtypeStruct((N_DEV, FF_BLK, D_MODEL), BF16),
        scratch_shapes=[pltpu.VMEM((FF_BLK, D_MODEL), F32)],
        compiler_params=_params(2),
    )(f, dxb)


def _out_proj_bwd(dh2, x2, r2, g, dx3, w, tm):
    T = x2.shape[0]

    def body(dh_ref, x_ref, r_ref, g_ref, dx3_ref, w_ref, dx_ref, dxb_ref, dg_ref, a_ref, b_ref):
        @pl.when(pl.program_id(0) == 0)
        def _():
            dg_ref[...] = jnp.zeros_like(dg_ref)

        dxn, dgt = _rms_bwd(dh_ref[...], x_ref[...], r_ref[...], g_ref[...])
        dx = dx3_ref[...] + dxn
        dxv = dx.astype(BF16)
        dx_ref[...] = dx
        dxb_ref[...] = dxv
        dg_ref[...] += jnp.sum(dgt, axis=0, keepdims=True)
        a_ref[...] = _dot(dxv, w_ref[:, 0:RET_WIDTH])
        b_ref[...] = _dot(dxv, w_ref[:, RET_WIDTH:D_MODEL])

    full = pl.BlockSpec((tm, D_MODEL), lambda i: (i, 0))
    vec = pl.BlockSpec((1, D_MODEL), lambda i: (0, 0))
    half = pl.BlockSpec((tm, RET_WIDTH), lambda i: (i, 0))
    hshape = jax.ShapeDtypeStruct((T, RET_WIDTH), F32)
    return pl.pallas_call(
        body, name="out_proj_bwd", grid=(T // tm,),
        in_specs=[full, full, pl.BlockSpec((tm, 1), lambda i: (i, 0)), vec, full,
                  pl.BlockSpec((D_MODEL, D_MODEL), lambda i: (0, 0))],
        out_specs=[full, full, vec, half, half],
        out_shape=[jax.ShapeDtypeStruct((T, D_MODEL), F32), jax.ShapeDtypeStruct((T, D_MODEL), BF16),
                   jax.ShapeDtypeStruct((1, D_MODEL), F32), hshape, hshape],
        compiler_params=_params(1),
    )(dh2, x2, r2, g, dx3, w)


def _wgrad_rows(name, a, b, tk):
    T, M = a.shape
    N = b.shape[1]
    nk = T // tk

    def body(a_ref, b_ref, o_ref, acc):
        k = pl.program_id(0)

        @pl.when(k == 0)
        def _():
            acc[...] = jnp.zeros_like(acc)

        acc[...] += _dot_tn(a_ref[...], b_ref[...])

        @pl.when(k == nk - 1)
        def _():
            o_ref[...] = acc[...].astype(BF16)

    return pl.pallas_call(
        body, name=name, grid=(nk,),
        in_specs=[pl.BlockSpec((tk, M), lambda k: (k, 0)), pl.BlockSpec((tk, N), lambda k: (k, 0))],
        out_specs=pl.BlockSpec((M, N), lambda k: (0, 0)),
        out_shape=jax.ShapeDtypeStruct((M, N), BF16),
        scratch_shapes=[pltpu.VMEM((M, N), F32)],
        compiler_params=_params(1),
    )(a, b)


def _glu_bwd(y, z, r, dyo, w, og, tm):
    T = y.shape[0]

    def body(y_ref, z_ref, r_ref, d_ref, w_ref, og_ref, dy_ref, dw_ref, db_ref, dog_ref):
        @pl.when(pl.program_id(0) == 0)
        def _():
            dw_ref[...] = jnp.zeros_like(dw_ref)
            db_ref[...] = jnp.zeros_like(db_ref)
            dog_ref[...] = jnp.zeros_like(dog_ref)

        y1, g1 = _gelu_and_grad(y_ref[...])
        sg = _sigmoid(z_ref[...])
        y2 = y1 * sg
        dy2, dogt = _rms_bwd(d_ref[...], y2, r_ref[...], og_ref[...])
        dog_ref[...] += jnp.sum(dogt, axis=0, keepdims=True)
        dz = dy2 * y1 * sg * (1.0 - sg)
        db_ref[...] += jnp.sum(dz, axis=0, keepdims=True)
        dzb = dz.astype(BF16)
        dw_ref[...] += _dot_tn(y1.astype(BF16), dzb)
        dy_ref[...] = (dy2 * sg + _dot_nt(dzb, w_ref[...])) * g1

    row = pl.BlockSpec((tm, SSM_WIDTH), lambda i: (i, 0))
    vec = pl.BlockSpec((1, SSM_WIDTH), lambda i: (0, 0))
    sq = pl.BlockSpec((SSM_WIDTH, SSM_WIDTH), lambda i: (0, 0))
    return pl.pallas_call(
        body, name="glu_bwd", grid=(T // tm,),
        in_specs=[row, row, pl.BlockSpec((tm, 1), lambda i: (i, 0)), row, sq, vec],
        out_specs=[row, sq, vec, vec],
        out_shape=[jax.ShapeDtypeStruct((T, SSM_WIDTH), F32), jax.ShapeDtypeStruct((SSM_WIDTH, SSM_WIDTH), F32),
                   jax.ShapeDtypeStruct((1, SSM_WIDTH), F32), jax.ShapeDtypeStruct((1, SSM_WIDTH), F32)],
        compiler_params=_params(1),
    )(y, z, r, dyo, w, og)


def _s5_bwd(proj, dy, bound, pm, pm_t, bre, bim, bre_t, bim_t, cre, cim, lam, d):
    T = proj.shape[0]
    nt = T // S5_TILE
    sp = _s5_specs(T, True)

    def body(u_ref, dy_ref, bound_ref, pm_ref, pmt_ref, bre_ref, bim_ref, bret_ref, bimt_ref, cre_ref, cim_ref,
             lam_ref, d_ref,
             du_ref, dbre_ref, dbim_ref, dcre_ref, dcim_ref, dlam_ref, dd_ref, carry, ptab, sr, si, gr, gi):
        lr = lam_ref[0:1, :]
        li = lam_ref[1:2, :]

        @pl.when(pl.program_id(1) == 0)
        def _():
            carry[...] = jnp.zeros_like(carry)
            _fill_power_table(ptab, lr, li)
            for ref in (dbre_ref, dbim_ref, dcre_ref, dcim_ref, dlam_ref, dd_ref):
                ref[...] = jnp.zeros_like(ref)

        u = _permute_rows_f32(pm_ref[...], u_ref[...])
        ub = u.astype(BF16)
        dyv = _permute_rows_f32(pm_ref[...], dy_ref[...])
        dyb = dyv.astype(BF16)
        _tile_set(sr, _dot(ub, bre_ref[...]))
        _tile_set(si, _dot(ub, bim_ref[...]))
        er, ei, _, _ = _s5_forward_states(sr, si, lr, li, bound_ref[0:1, :], bound_ref[1:2, :], ptab)
        _tile_set(gr, _dot(dyb, cre_ref[...]))
        _tile_set(gi, -_dot(dyb, cim_ref[...]))
        zr, zi = _chunk_scans(gr, gi, lr, -li, True)
        ar, ai = _table_rows(ptab, S5_STEPS - 1, True)
        fr, fi = _entering_states(zr, zi, carry[0:1, :], carry[1:2, :], ar, ai, True)
        acc_r = jnp.zeros((S5_CHUNKS, S5_LANES), F32)
        acc_i = jnp.zeros((S5_CHUNKS, S5_LANES), F32)
        for j in range(S5_STEPS):
            qr, qi = _table_rows(ptab, S5_STEPS - 1 - j, True)
            g_r = _step_get(gr, j) + qr * fr - qi * fi
            g_i = _step_get(gi, j) + qr * fi + qi * fr
            _step_set(gr, j, g_r)
            _step_set(gi, j, g_i)
            p_r, p_i = (er, ei) if j == 0 else (_step_get(sr, j - 1), _step_get(si, j - 1))
            acc_r += g_r * p_r + g_i * p_i
            acc_i += g_i * p_r - g_r * p_i
        dlam_ref[0:1, :] += jnp.sum(acc_r, axis=0, keepdims=True)
        dlam_ref[1:2, :] += jnp.sum(acc_i, axis=0, keepdims=True)
        g_all_r = _tile_get(gr)
        g_all_i = _tile_get(gi)
        carry[0:1, :] = g_all_r[0:1, :]
        carry[1:2, :] = g_all_i[0:1, :]
        grb = g_all_r.astype(BF16)
        gib = g_all_i.astype(BF16)
        du = (_dot(grb, bret_ref[...]) + _dot(gib, bimt_ref[...]) + d_ref[...] * dyv).astype(BF16)
        du_ref[...] = _dot(pmt_ref[...], du).astype(BF16)
        dbre_ref[...] += _dot_tn(grb, ub)
        dbim_ref[...] += _dot_tn(gib, ub)
        dcre_ref[...] += _dot_tn(dyb, _tile_get(sr).astype(BF16))
        dcim_ref[...] -= _dot_tn(dyb, _tile_get(si).astype(BF16))
        dd_ref[...] += jnp.sum(dyv * u, axis=0, keepdims=True)

    acc_ts = pl.BlockSpec((None, S5_LANES, LANE), lambda b, t: (b, 0, 0))
    acc_fs = pl.BlockSpec((None, LANE, S5_LANES), lambda b, t: (b, 0, 0))
    return pl.pallas_call(
        body, name="s5_bwd", grid=(S5_NBLK, nt),
        in_specs=[sp["u"], sp["rows"], sp["bound"], sp["perm"], sp["perm"], sp["to_state"], sp["to_state"],
                  sp["from_state"], sp["from_state"], sp["to_state"], sp["to_state"], sp["lam"], sp["d"]],
        out_specs=[sp["rows"], acc_ts, acc_ts, acc_fs, acc_fs, sp["lam"], sp["d"]],
        out_shape=[jax.ShapeDtypeStruct((T, SSM_WIDTH), BF16),
                   jax.ShapeDtypeStruct((S5_NBLK, S5_LANES, LANE), F32),
                   jax.ShapeDtypeStruct((S5_NBLK, S5_LANES, LANE), F32),
                   jax.ShapeDtypeStruct((S5_NBLK, LANE, S5_LANES), F32),
                   jax.ShapeDtypeStruct((S5_NBLK, LANE, S5_LANES), F32),
                   jax.ShapeDtypeStruct((S5_NBLK, 2, S5_LANES), F32),
                   jax.ShapeDtypeStruct((1, SSM_WIDTH), F32)],
        scratch_shapes=[pltpu.VMEM((2, S5_LANES), F32), pltpu.VMEM((2, S5_TILE, S5_LANES), F32)]
        + [pltpu.VMEM(S5_STATE_TILE, F32)] * 4,
        compiler_params=_params(2),
    )(proj, dy, bound, pm, pm_t, bre, bim, bre_t, bim_t, cre, cim, lam, d)


def _ret_bwd(proj, cosf, sinf, mask, rowdec, kdec, gtb, gn, sblk, dyr):
    T = proj.shape[0]
    nb = T // RET_BLOCK
    sp = _ret_specs(T, True)

    def body(q_ref, k_ref, v_ref, g_ref, cos_ref, sin_ref, mask_ref, rd_ref, kd_ref, gtb_ref, gn_ref, sb_ref, dy_ref,
             dq_ref, dk_ref, dv_ref, dg_ref, dgn_ref, dst):
        @pl.when(pl.program_id(1) == 0)
        def _():
            dst[...] = jnp.zeros_like(dst)
            dgn_ref[...] = jnp.zeros_like(dgn_ref)

        s_in = sb_ref[...]
        q, k, qb, kb, vb, pm, qd, o = _ret_common(q_ref, k_ref, v_ref, cos_ref, sin_ref, mask_ref, rd_ref, s_in)
        mu = jnp.mean(o, axis=-1, keepdims=True)
        oc = o - mu
        rstd = lax.rsqrt(jnp.mean(oc * oc, axis=-1, keepdims=True) + EPS)
        n = oc * rstd
        gt = g_ref[...]
        sg = _sigmoid(gt)
        sil = gt * sg
        gnv = gn_ref[...]
        dyv = dy_ref[...]
        dg_ref[...] = (dyv * (n * gnv) * (sg * (1.0 + gt * (1.0 - sg)))).astype(BF16)
        dgn_ref[...] += jnp.sum(dyv * sil * n, axis=0, keepdims=True)
        dn = dyv * sil * gnv
        do = rstd * (dn - jnp.mean(dn, axis=-1, keepdims=True) - n * jnp.mean(dn * n, axis=-1, keepdims=True))
        dob = do.astype(BF16)
        ds = dst[...]
        dsb = ds.astype(BF16)
        kd = kd_ref[...]
        rd = rd_ref[...]
        dv_ref[...] = (_dot_tn(pm, dob) + _dot((k * kd).astype(BF16), dsb)).astype(BF16)
        dpb = (_dot_nt(dob, vb) * mask_ref[...]).astype(BF16)
        dq = _dot(dpb, kb) + _dot_nt(dob, s_in.astype(BF16)) * rd
        dk = (_dot_tn(dpb, qb) + _dot_nt(vb, dsb) * kd) * (HEAD_DIM ** -0.5)
        dst[...] = gtb_ref[...] * ds + _dot_tn(qd, dob)
        c = cos_ref[...]
        s = sin_ref[...]
        dq_ref[...] = (dq * c + pltpu.roll(dq * s, HEAD_DIM // 2, 1)).astype(BF16)
        dk_ref[...] = (dk * c + pltpu.roll(dk * s, HEAD_DIM // 2, 1)).astype(BF16)

    oshape = jax.ShapeDtypeStruct((T, RET_WIDTH), BF16)
    ins = [sp[n] for n in ("q", "k", "v", "g", "tab", "tab", "mask", "dec", "dec", "gtb", "gn", "state", "rows")]
    outs = [sp["rows"], sp["rows"], sp["rows"], sp["rows"], sp["gn"]]
    return pl.pallas_call(
        _per_head(body, [kind for _, kind in ins + outs + [sp["scratch"]]]), name="ret_bwd",
        grid=(RET_HEADS // RET_HPS, nb), in_specs=[s for s, _ in ins], out_specs=[s for s, _ in outs],
        out_shape=[oshape, oshape, oshape, oshape, jax.ShapeDtypeStruct((1, RET_WIDTH), F32)],
        scratch_shapes=[sp["scratch"][0]],
        compiler_params=_params(2),
    )(proj, proj, proj, proj, cosf, sinf, mask, rowdec, kdec, gtb, gn, sblk, dyr)


def _in_proj_bwd(dproj, w, x, r1, g, dx2, tm, carry=None):
    T = x.shape[0]

    def body(dp_ref, w_hbm, x_ref, r_ref, g_ref, dx2_ref, gx_ref, dg_ref, w_ref, sem):
        @pl.when(pl.program_id(0) == 0)
        def _():
            _load_resident(w_hbm, w_ref, sem)
            dg_ref[...] = jnp.zeros_like(dg_ref)

        dxn, dgt = _rms_bwd(_dot(dp_ref[...], w_ref[...]), x_ref[...], r_ref[...], g_ref[...])
        gx_ref[...] = dx2_ref[...] + dxn
        dg_ref[...] += jnp.sum(dgt, axis=0, keepdims=True)

    full = pl.BlockSpec((tm, D_MODEL), lambda i: (i, 0))
    vec = pl.BlockSpec((1, D_MODEL), lambda i: (0, 0))
    return _pcall(
        body, "in_proj_bwd", (T // tm,),
        [pl.BlockSpec((tm, IN_WIDTH), lambda i: (i, 0)), ANY_SPEC,
         full, pl.BlockSpec((tm, 1), lambda i: (i, 0)), vec, full],
        [full, vec],
        [jax.ShapeDtypeStruct((T, D_MODEL), F32), jax.ShapeDtypeStruct((1, D_MODEL), F32)],
        [pltpu.VMEM(w.shape, w.dtype), pltpu.SemaphoreType.DMA], (dproj, w, x, r1, g, dx2), carry)


def _in_proj_wgrad(h, dproj, tk, carry=None):
    T = h.shape[0]
    nk = T // tk

    def body(h_ref, dp_ref, o_ref, acc):
        k = pl.program_id(1)

        @pl.when(k == 0)
        def _():
            acc[...] = jnp.zeros_like(acc)

        acc[...] += _dot_tn(h_ref[...], dp_ref[...])

        @pl.when(k == nk - 1)
        def _():
            o_ref[...] = acc[...].astype(BF16)

    return _pcall(
        body, "in_proj_wgrad", (N_DEV, nk),
        [pl.BlockSpec((tk, D_MODEL), lambda j, k: (k, 0)), pl.BlockSpec((tk, WIN_BLK), lambda j, k: (k, j))],
        [pl.BlockSpec((None, D_MODEL, WIN_BLK), lambda j, k: (j, 0, 0))],
        [jax.ShapeDtypeStruct((N_DEV, D_MODEL, WIN_BLK), BF16)],
        [pltpu.VMEM((D_MODEL, WIN_BLK), F32)], (h, dproj), carry)


def _rope_tables(T):
    half = HEAD_DIM // 2
    freqs = ROPE_BASE ** (-jnp.arange(half, dtype=F32) / half)
    ang = jnp.arange(T, dtype=F32)[:, None] * freqs[None, :]
    c = jnp.cos(ang)
    s = jnp.sin(ang)
    return jnp.concatenate([c, c], axis=1), jnp.concatenate([-s, s], axis=1)


def _retention_tables():
    hh = jnp.arange(RET_HEADS, dtype=F32)
    log_g = jnp.log1p(-(2.0 ** (-5.0 - hh)))[:, None, None]
    i = jnp.arange(RET_BLOCK)
    ci = (i // CHUNK)[:, None]
    cj = (i // CHUNK)[None, :]
    diff = (i[:, None] - i[None, :]).astype(F32)
    expo = jnp.where(ci == cj, jnp.abs(diff), diff)
    mask = jnp.where((cj <= ci)[None], jnp.exp(log_g * expo[None]), 0.0)
    r = jnp.arange(RET_BLOCK, dtype=F32)[None, :, None]
    ones = jnp.ones((1, 1, HEAD_DIM), F32)
    rowdec = jnp.exp(log_g * (r + 1.0)) * ones
    kdec = jnp.exp(log_g * (RET_BLOCK - 1.0 - r)) * ones
    gtb = jnp.exp(log_g * float(RET_BLOCK)) * ones
    return mask, rowdec, kdec, gtb


def _s5_discretise(a_re, a_im, log_dt, b_re, b_im):
    lam = lax.complex(a_re, a_im)
    dt = jnp.exp(log_dt)[:, None]
    lam_bar = jnp.exp(lam * dt)
    b_bar = ((lam_bar - 1.0) / lam)[..., None] * lax.complex(b_re, b_im)
    return jnp.real(lam_bar), jnp.imag(lam_bar), jnp.real(b_bar), jnp.imag(b_bar)


def _to_state_blockdiag(m):
    eye = jnp.eye(S5_GB, dtype=m.dtype)
    t = jnp.einsum("bgpc,gh->bgchp", m.reshape(S5_NBLK, S5_GB, SSM_STATE, SSM_GROUP), eye)
    return t.reshape(S5_NBLK, LANE, S5_LANES)


def _from_state_blockdiag(m):
    eye = jnp.eye(S5_GB, dtype=m.dtype)
    t = jnp.einsum("bgcp,gh->bgphc", m.reshape(S5_NBLK, S5_GB, SSM_GROUP, SSM_STATE), eye)
    return t.reshape(S5_NBLK, S5_LANES, LANE)


def _diag_of_state_major(acc):
    eye = jnp.eye(S5_GB, dtype=acc.dtype)
    t = acc.reshape(S5_NBLK, S5_GB, SSM_STATE, S5_GB, SSM_GROUP)
    return jnp.einsum("bgphc,gh->bgpc", t, eye).reshape(SSM_GROUPS, SSM_STATE, SSM_GROUP)


def _diag_of_channel_major(acc):
    eye = jnp.eye(S5_GB, dtype=acc.dtype)
    t = acc.reshape(S5_NBLK, S5_GB, SSM_GROUP, S5_GB, SSM_STATE)
    return jnp.einsum("bgchp,gh->bgcp", t, eye).reshape(SSM_GROUPS, SSM_GROUP, SSM_STATE)


SMALL_PARTIALS = (("ret_gn_g", 1024), ("lam_re", 4096), ("lam_im", 4096),
                  ("bbar_re", 65536), ("bbar_im", 65536), ("c_re", 65536), ("c_im", 65536),
                  ("ssm_d", 1024), ("b_glu", 1024), ("out_g", 1024), ("norm_ffn_g", 2048), ("norm_final_g", 2048))


def _forward_backward(x, tgt, shards, sm, tm=512):
    T = x.shape[0]
    cosf, sinf = _rope_tables(T)
    mask, rowdec, kdec, gtb = _retention_tables()
    lbr, lbi, bbr, bbi = _s5_discretise(sm["ssm_a_re"], sm["ssm_a_im"], sm["ssm_log_dt"], sm["ssm_b_re"],
                                        sm["ssm_b_im"])
    bre = _to_state_blockdiag(bbr).astype(BF16)
    bim = _to_state_blockdiag(bbi).astype(BF16)
    cre_t = _from_state_blockdiag(sm["ssm_c_re"]).astype(BF16)
    cim_t = _from_state_blockdiag(sm["ssm_c_im"]).astype(BF16)
    bre_t = jnp.swapaxes(bre, 1, 2)
    bim_t = jnp.swapaxes(bim, 1, 2)
    cre = jnp.swapaxes(cre_t, 1, 2)
    cim = jnp.swapaxes(cim_t, 1, 2)
    lam = jnp.stack([lbr.reshape(S5_NBLK, S5_LANES), lbi.reshape(S5_NBLK, S5_LANES)], axis=1)
    pm = _step_major_permutation()
    pm_t = pm.T
    row = lambda v: v.reshape(1, -1)
    g_mix, g_ffn, g_fin = row(sm["norm_mix_g"]), row(sm["norm_ffn_g"]), row(sm["norm_final_g"])
    gn, dsk, bglu, og = row(sm["ret_gn_g"]), row(sm["ssm_d"]), row(sm["ssm_b_glu"]), row(sm["ssm_out_g"])

    w_in = _gather_once_per_chip("weight_gather", shards["w_in"])
    proj, h1, r1, w_glu, w_out = _in_proj_fwd(
        x, g_mix, w_in, 256, _Exchange([shards["ssm_w_glu"], shards["w_out"]], True))
    w_glu = w_glu.reshape(SSM_WIDTH, SSM_WIDTH)
    w_out = w_out.reshape(D_MODEL, D_MODEL)
    y_ret, sblk, w_gate = _ret_fwd(proj, cosf, sinf, mask, rowdec, kdec, gtb, gn,
                                   _Exchange([shards["w_gate"]], True))
    y_s5, bound, w_up = _s5_fwd(proj, pm, pm_t, bre, bim, cre_t, cim_t, lam, dsk, _Exchange([shards["w_up"]], True))
    z, y_ssm, r_ssm = _glu_fwd(y_s5, w_glu, bglu, og, 256)
    x2, h2, r2 = _out_proj_fwd(x, y_ret, y_ssm, w_out, g_ffn, 256)
    a, b, f, w_down = _ffn_up(h2, w_gate, w_up, tm, _Exchange([shards["w_down"]], True))
    dx3, dx3b, loss8, dg_fin = _ffn_down_loss(f, w_down, x2, tgt, g_fin, 256)

    landed = {}
    w_down_t, w_gate_t, w_up_t = (jnp.swapaxes(w_, 1, 2) for w_ in (w_down, w_gate, w_up))
    w_out_t = w_out.T
    w_in_t = jnp.swapaxes(w_in, 1, 2).reshape(IN_WIDTH, D_MODEL)
    da, db = _ffn_bwd_act(dx3b, w_down_t, a, b, tm)
    dw_down = _ffn_wgrad_down(f, dx3b, tm)
    dw_gate, dw_up, landed["w_down"] = _ffn_wgrad_up(h2, da, db, tm, _Exchange([dw_down], False))
    dh2, landed["w_gate"], landed["w_up"] = _ffn_bwd_in(da, db, w_gate_t, w_up_t, min(1024, T),
                                                        _Exchange([dw_gate, dw_up], False))
    dx2, dx2b, dg_ffn, dy_ret, dy_ssm = _out_proj_bwd(dh2, x2, r2, g_ffn, dx3, w_out_t, 256)
    dw_out = jnp.concatenate([_wgrad_rows("out_proj_wgrad_ret", y_ret, dx2b, tm),
                              _wgrad_rows("out_proj_wgrad_ssm", y_ssm, dx2b, tm)], axis=0)
    dy_s5, dw_glu, db_glu, dog = _glu_bwd(y_s5, z, r_ssm, dy_ssm, w_glu, og, 256)
    du, dbre, dbim, dcre, dcim, dlam, dd = _s5_bwd(proj, dy_s5, bound, pm, pm_t, bre, bim, bre_t, bim_t, cre, cim,
                                                   lam, dsk)
    dq, dk, dv, dgate, dgn = _ret_bwd(proj, cosf, sinf, mask, rowdec, kdec, gtb, gn, sblk, dy_ret)
    dproj = jnp.concatenate([dq, dk, dv, dgate, du], axis=1)
    dw_in, landed["w_out"], landed["ssm_w_glu"] = _in_proj_wgrad(
        h1, dproj, tm, _Exchange([dw_out.reshape(N_DEV, D_MODEL // N_DEV, D_MODEL),
                                  dw_glu.astype(BF16).reshape(N_DEV, SSM_WIDTH // N_DEV, SSM_WIDTH)], False))
    small = dict(ret_gn_g=dgn, lam_re=dlam[:, 0], lam_im=dlam[:, 1],
                 bbar_re=_diag_of_state_major(dbre), bbar_im=_diag_of_state_major(dbim),
                 c_re=_diag_of_channel_major(dcre), c_im=_diag_of_channel_major(dcim),
                 ssm_d=dd, b_glu=db_glu, out_g=dog, norm_ffn_g=dg_ffn, norm_final_g=dg_fin)
    packed = _pack([small[n] for n, _ in SMALL_PARTIALS])
    grad_x, dg_mix, landed["w_in"], small_landed = _in_proj_bwd(
        dproj, w_in_t, x, r1, g_mix, dx2, 256, _Exchange([dw_in, packed], [False, True]))
    (mix_landed,) = _exchange_call("mix_gain_grad_gather", [_pack([dg_mix])], True)
    summed = dict(zip([n for n, _ in SMALL_PARTIALS],
                      _unpack(_sum_partials("small_grad_sum", small_landed), [(sz,) for _, sz in SMALL_PARTIALS])))
    summed["norm_mix_g"] = _sum_partials("mix_gain_grad_sum", mix_landed).reshape(-1)
    return loss8[0, 0], grad_x, landed, summed


def _small_grads(summed, sm):
    _, vjp = jax.vjp(_s5_discretise, sm["ssm_a_re"], sm["ssm_a_im"], sm["ssm_log_dt"], sm["ssm_b_re"], sm["ssm_b_im"])
    gp = (SSM_GROUPS, SSM_STATE)
    da_re, da_im, dlog_dt, db_re, db_im = vjp((summed["lam_re"].reshape(gp), summed["lam_im"].reshape(gp),
                                               summed["bbar_re"].reshape(gp + (SSM_GROUP,)),
                                               summed["bbar_im"].reshape(gp + (SSM_GROUP,))))
    return dict(norm_mix_g=summed["norm_mix_g"], ret_gn_g=summed["ret_gn_g"], ssm_a_re=da_re, ssm_a_im=da_im,
                ssm_log_dt=dlog_dt, ssm_b_re=db_re, ssm_b_im=db_im,
                ssm_c_re=summed["c_re"].reshape(SSM_GROUPS, SSM_GROUP, SSM_STATE),
                ssm_c_im=summed["c_im"].reshape(SSM_GROUPS, SSM_GROUP, SSM_STATE),
                ssm_d=summed["ssm_d"], ssm_b_glu=summed["b_glu"], ssm_out_g=summed["out_g"],
                norm_ffn_g=summed["norm_ffn_g"], norm_final_g=summed["norm_final_g"])


def _adamw_math(w, g, m, v):
    m2 = ADAM_B1 * m + (1.0 - ADAM_B1) * g
    v2 = ADAM_B2 * v + (1.0 - ADAM_B2) * (g * g)
    delta = -ADAM_LR * ((m2 / ADAM_BC1) / (jnp.sqrt(v2 / ADAM_BC2) + ADAM_EPS) + ADAM_WD * w)
    return delta, m2, v2


def _adamw_shard(name, parts, w, m, v, tr):
    rows, cols = w.shape

    def body(p_ref, w_ref, m_ref, v_ref, g_ref, d_ref, m2_ref, v2_ref):
        g = p_ref[0].astype(F32)
        for s in range(1, N_DEV):
            g = g + p_ref[s].astype(F32)
        d, m2, v2 = _adamw_math(w_ref[...], g, m_ref[...], v_ref[...])
        g_ref[...] = g
        d_ref[...] = d
        m2_ref[...] = m2
        v2_ref[...] = v2

    blk = pl.BlockSpec((tr, cols), lambda i: (i, 0))
    oshape = jax.ShapeDtypeStruct((rows, cols), F32)
    return pl.pallas_call(
        body, name=name, grid=(rows // tr,),
        in_specs=[pl.BlockSpec((N_DEV, tr, cols), lambda i: (0, i, 0)), blk, blk, blk],
        out_specs=[blk, blk, blk, blk], out_shape=[oshape] * 4,
        compiler_params=_params(1),
    )(parts, w, m, v)


def _sum_partials(name, parts):
    rows = parts.shape[1]

    def body(p_ref, o_ref):
        g = p_ref[0]
        for s in range(1, N_DEV):
            g = g + p_ref[s]
        o_ref[...] = g

    return pl.pallas_call(
        body, name=name, grid=(1,),
        in_specs=[pl.BlockSpec((N_DEV, rows, LANE), lambda i: (0, 0, 0))],
        out_specs=pl.BlockSpec((rows, LANE), lambda i: (0, 0)),
        out_shape=jax.ShapeDtypeStruct((rows, LANE), F32),
        compiler_params=_params(1),
    )(parts)


def _adamw_small(ws, gs, ms, vs):
    n = len(ws)

    def body(*refs):
        for i in range(n):
            w_ref, g_ref, m_ref, v_ref = (refs[k * n + i] for k in range(4))
            d_ref, m2_ref, v2_ref = (refs[(4 + k) * n + i] for k in range(3))
            d, m2, v2 = _adamw_math(w_ref[...], g_ref[...], m_ref[...], v_ref[...])
            d_ref[...] = d
            m2_ref[...] = m2
            v2_ref[...] = v2

    vmem = pl.BlockSpec(memory_space=pltpu.VMEM)
    out = pl.pallas_call(
        body, name="adamw_small", in_specs=[vmem] * (4 * n), out_specs=[vmem] * (3 * n),
        out_shape=[jax.ShapeDtypeStruct(w.shape, F32) for w in ws] * 3,
        compiler_params=pltpu.CompilerParams(vmem_limit_bytes=VMEM_LIMIT),
    )(*ws, *gs, *ms, *vs)
    return out[:n], out[n:2 * n], out[2 * n:]


def _pack(arrays):
    cols = []
    for a in arrays:
        flat = a.reshape(-1).astype(F32)
        pad = (-flat.shape[0]) % LANE
        cols.append(jnp.pad(flat, (0, pad)) if pad else flat)
    return jnp.concatenate(cols).reshape(-1, LANE)


def _unpack(packed, shapes):
    flat = packed.reshape(-1)
    out, off = [], 0
    for shp in shapes:
        n = math.prod(shp)
        out.append(flat[off:off + n].reshape(shp))
        off += n + ((-n) % LANE)
    return out


WEIGHTS = ("norm_mix_g", "w_in", "ret_gn_g", "ssm_a_re", "ssm_a_im", "ssm_log_dt", "ssm_b_re", "ssm_b_im",
           "ssm_c_re", "ssm_c_im", "ssm_d", "ssm_w_glu", "ssm_b_glu", "ssm_out_g", "w_out", "norm_ffn_g", "w_gate",
           "w_up", "w_down", "norm_final_g")
BIG = ("w_in", "ssm_w_glu", "w_out", "w_gate", "w_up", "w_down")
SMALL = tuple(n for n in WEIGHTS if n not in BIG)
ADAM_ROWS = {"w_in": 256, "ssm_w_glu": 128, "w_out": 128, "w_gate": 256, "w_up": 256, "w_down": 176}


def kernel(x, norm_mix_g, w_in, ret_gn_g, ssm_a_re, ssm_a_im, ssm_log_dt, ssm_b_re, ssm_b_im, ssm_c_re, ssm_c_im, ssm_d, ssm_w_glu, ssm_b_glu, ssm_out_g, w_out, norm_ffn_g, w_gate, w_up, w_down, norm_final_g, loss_target, m_norm_mix_g, m_w_in, m_ret_gn_g, m_ssm_a_re, m_ssm_a_im, m_ssm_log_dt, m_ssm_b_re, m_ssm_b_im, m_ssm_c_re, m_ssm_c_im, m_ssm_d, m_ssm_w_glu, m_ssm_b_glu, m_ssm_out_g, m_w_out, m_norm_ffn_g, m_w_gate, m_w_up, m_w_down, m_norm_final_g, v_norm_mix_g, v_w_in, v_ret_gn_g, v_ssm_a_re, v_ssm_a_im, v_ssm_log_dt, v_ssm_b_re, v_ssm_b_im, v_ssm_c_re, v_ssm_c_im, v_ssm_d, v_ssm_w_glu, v_ssm_b_glu, v_ssm_out_g, v_w_out, v_norm_ffn_g, v_w_gate, v_w_up, v_w_down, v_norm_final_g):
    given = dict(locals())
    w = {n: given[n] for n in WEIGHTS}
    m = {n: given["m_" + n] for n in WEIGHTS}
    v = {n: given["v_" + n] for n in WEIGHTS}
    drop = lambda n, a: a if n == "norm_final_g" else a[0]
    w0 = {n: drop(n, w[n]) for n in WEIGHTS}
    m0 = {n: drop(n, m[n]) for n in WEIGHTS}
    v0 = {n: drop(n, v[n]) for n in WEIGHTS}

    sm = {n: w0[n] for n in SMALL}
    shards = {n: w0[n].astype(BF16) for n in BIG}
    loss_local, grad_x, landed, summed = _forward_backward(x[0], loss_target[0], shards, sm)
    loss = lax.psum(loss_local, MESH_AXES)
    gsmall = _small_grads(summed, sm)

    grads, delta, new_m, new_v = {}, {}, {}, {}
    for n in BIG:
        g, d, m2, v2 = _adamw_shard("adamw_" + n, landed[n], w0[n], m0[n], v0[n], ADAM_ROWS[n])
        grads[n], delta[n], new_m[n], new_v[n] = g, d, m2, v2
    as_given = lambda n, a: a.reshape(1, -1) if n == "norm_final_g" else a.reshape(w[n].shape)
    gs = [as_given(n, gsmall[n]) for n in SMALL]
    ds, m2s, v2s = _adamw_small([as_given(n, w[n]) for n in SMALL], gs, [as_given(n, m[n]) for n in SMALL],
                                [as_given(n, v[n]) for n in SMALL])
    for n, g, d, m2, v2 in zip(SMALL, gs, ds, m2s, v2s):
        grads[n], delta[n], new_m[n], new_v[n] = g, d, m2, v2

    lift = lambda n, a: a.reshape(w[n].shape)
    return (loss, grad_x[None], *[lift(n, grads[n]) for n in WEIGHTS], *[lift(n, delta[n]) for n in WEIGHTS],
            *[lift(n, new_m[n]) for n in WEIGHTS], *[lift(n, new_v[n]) for n in WEIGHTS])
```

```python
import functools
import math

import jax
import jax.numpy as jnp
from jax import lax
from jax.experimental import pallas as pl
from jax.experimental.pallas import tpu as pltpu

F32 = jnp.float32
BF16 = jnp.bfloat16

D_MODEL = 2048
RET_WIDTH = 1024
RET_HEADS = 8
HEAD_DIM = 128
CHUNK = 64
SSM_WIDTH = 1024
SSM_GROUP = 16
SSM_GROUPS = 64
SSM_STATE = 64
D_FF = 5632
IN_WIDTH = 5120
ROPE_BASE = 10000.0
EPS = 1e-6
N_DEV = 8
MESH_AXES = ("x", "y", "c")

WIN_BLK = IN_WIDTH // N_DEV
FF_BLK = D_FF // N_DEV
RET_BLOCK = 256
RET_HPS = 2
S5_TILE = 256
S5_CHUNKS = 8
S5_STEPS = S5_TILE // S5_CHUNKS
S5_GB = 8
S5_NBLK = SSM_GROUPS // S5_GB
S5_LANES = S5_GB * SSM_STATE
LANE = 128

ADAM_LR = 0.001
ADAM_B1 = 0.9
ADAM_B2 = 0.999
ADAM_EPS = 1e-08
ADAM_WD = 0.01
ADAM_STEP = 10
ADAM_BC1 = 1.0 - ADAM_B1 ** ADAM_STEP
ADAM_BC2 = 1.0 - ADAM_B2 ** ADAM_STEP

VMEM_LIMIT = 56 * 1024 * 1024

NT = (((1,), (1,)), ((), ()))
TN = (((0,), (0,)), ((), ()))


def _params(n_grid):
    return pltpu.CompilerParams(dimension_semantics=("arbitrary",) * n_grid, vmem_limit_bytes=VMEM_LIMIT)


def _dot(a, b):
    return jnp.dot(a, b, preferred_element_type=F32)


def _dot_nt(a, b):
    return lax.dot_general(a, b, NT, preferred_element_type=F32)


def _dot_tn(a, b):
    return lax.dot_general(a, b, TN, preferred_element_type=F32)


def _sigmoid(x):
    return 1.0 / (1.0 + jnp.exp(-x))


_GELU_C = math.sqrt(2.0 / math.pi)
_GELU_A = 0.044715


def _gelu(x):
    t = jnp.tanh(_GELU_C * (x + _GELU_A * x * x * x))
    return 0.5 * x * (1.0 + t)


def _gelu_and_grad(x):
    t = jnp.tanh(_GELU_C * (x + _GELU_A * x * x * x))
    g = 0.5 * (1.0 + t) + 0.5 * x * (1.0 - t * t) * _GELU_C * (1.0 + 3.0 * _GELU_A * x * x)
    return 0.5 * x * (1.0 + t), g


def _rms_bwd(dy, x, r, g):
    w = dy * g
    dx = r * w - x * (r * r * r) * jnp.mean(w * x, axis=-1, keepdims=True)
    return dx, dy * x * r


HBM_SPEC = pl.BlockSpec(memory_space=pltpu.HBM)
ANY_SPEC = pl.BlockSpec(memory_space=pl.ANY)


def _load_resident(src_hbm, dst_vmem, sem):
    cp = pltpu.make_async_copy(src_hbm, dst_vmem, sem)
    cp.start()
    cp.wait()


def _my_block():
    return 4 * lax.axis_index("x") + 2 * lax.axis_index("y") + lax.axis_index("c")


def _peer(k):
    px = lax.axis_index("x") ^ ((k >> 2) & 1)
    py = lax.axis_index("y") ^ ((k >> 1) & 1)
    pc = lax.axis_index("c") ^ (k & 1)
    return (px, py, pc), 4 * px + 2 * py + pc


class _Exchange:
    def __init__(self, payloads, gather):
        self.payloads = list(payloads)
        self.n = len(self.payloads)
        self.gather = [gather] * self.n if isinstance(gather, bool) else list(gather)

    def out_shape(self):
        return [jax.ShapeDtypeStruct(((N_DEV,) if g else ()) + p.shape, p.dtype)
                for p, g in zip(self.payloads, self.gather)]

    def scratch_shapes(self):
        return [pltpu.SemaphoreType.DMA((self.n, N_DEV - 1)), pltpu.SemaphoreType.DMA((self.n, N_DEV - 1)),
                pltpu.SemaphoreType.DMA((self.n,))]

    def _copies(self, ins, outs, sems, incoming):
        send_sems, recv_sems, local_sems = sems
        me = _my_block()
        src_of = lambda i, blk: ins[i] if self.gather[i] else ins[i].at[blk]
        local, remote = [], []
        for i in range(self.n):
            if not incoming:
                local.append(pltpu.make_async_copy(src_of(i, me), outs[i].at[me], local_sems.at[i]))
            for k in range(1, N_DEV):
                dev, blk = _peer(k)
                src, dst = (outs[i].at[blk], outs[i].at[blk]) if incoming else (src_of(i, blk), outs[i].at[me])
                remote.append(pltpu.make_async_remote_copy(
                    src_ref=src, dst_ref=dst, send_sem=send_sems.at[i, k - 1], recv_sem=recv_sems.at[i, k - 1],
                    device_id=dev, device_id_type=pl.DeviceIdType.MESH))
        return local, remote

    def start(self, ins, outs, sems):
        local, sends = self._copies(ins, outs, sems, False)
        for cp in local + sends:
            cp.start()

    def wait(self, ins, outs, sems):
        for cp in self._copies(ins, outs, sems, True)[1]:
            cp.wait_recv()
        local, sends = self._copies(ins, outs, sems, False)
        for cp in sends:
            cp.wait_send()
        for cp in local:
            cp.wait()


def _pcall(body, name, grid, in_specs, out_specs, out_shape, scratch_shapes, args, carry=None):
    n_in, n_out, n_scr = len(in_specs), len(out_specs), len(scratch_shapes)
    if carry is None:
        return pl.pallas_call(body, name=name, grid=grid, in_specs=in_specs, out_specs=out_specs, out_shape=out_shape,
                              scratch_shapes=scratch_shapes, compiler_params=_params(len(grid)))(*args)
    nx = carry.n

    def wrapped(*refs):
        cin, xin = refs[:n_in], refs[n_in:n_in + nx]
        cout, xout = refs[n_in + nx:n_in + nx + n_out], refs[n_in + nx + n_out:n_in + 2 * nx + n_out]
        rest = refs[n_in + 2 * nx + n_out:]
        cscr, sems = rest[:n_scr], rest[n_scr:]
        first = functools.reduce(jnp.logical_and, [pl.program_id(a) == 0 for a in range(len(grid))])
        last = functools.reduce(jnp.logical_and, [pl.program_id(a) == grid[a] - 1 for a in range(len(grid))])

        @pl.when(first)
        def _():
            carry.start(xin, xout, sems)

        body(*cin, *cout, *cscr)

        @pl.when(last)
        def _():
            carry.wait(xin, xout, sems)

    return pl.pallas_call(
        wrapped, name=name, grid=grid, in_specs=list(in_specs) + [HBM_SPEC] * nx,
        out_specs=list(out_specs) + [HBM_SPEC] * nx, out_shape=list(out_shape) + carry.out_shape(),
        scratch_shapes=list(scratch_shapes) + carry.scratch_shapes(), compiler_params=_params(len(grid)),
    )(*args, *carry.payloads)


def _exchange_call(name, payloads, gather):
    ex = _Exchange(payloads, gather)

    def body(*refs):
        ins, outs, sems = refs[:ex.n], refs[ex.n:2 * ex.n], refs[2 * ex.n:]
        ex.start(ins, outs, sems)
        ex.wait(ins, outs, sems)

    return pl.pallas_call(body, name=name, in_specs=[HBM_SPEC] * ex.n, out_specs=[HBM_SPEC] * ex.n,
                          out_shape=ex.out_shape(), scratch_shapes=ex.scratch_shapes())(*ex.payloads)


def _gather_once_per_chip(name, shard):
    def body(src, out, send_sems, recv_sems, local_sem):
        x, y, c = lax.axis_index("x"), lax.axis_index("y"), lax.axis_index("c")
        me, sibling = (x, y, c), (x, y, 1 - c)
        chips = [(1 - x, y), (x, 1 - y), (1 - x, 1 - y)]
        slot = lambda px, py, pc: out.at[4 * px + 2 * py + pc]

        def copy(k, block, to, from_src=False):
            return pltpu.make_async_remote_copy(
                src_ref=src if from_src else slot(*block), dst_ref=slot(*block), send_sem=send_sems.at[k],
                recv_sem=recv_sems.at[k], device_id=to, device_id_type=pl.DeviceIdType.MESH)

        mine = pltpu.make_async_copy(src, slot(*me), local_sem)
        mine.start()
        first = [copy(0, me, sibling, True)] + [copy(1 + j, me, (*chip, c), True) for j, chip in enumerate(chips)]
        for cp in first:
            cp.start()
        passed = [copy(4 + j, (*chip, c), sibling) for j, chip in enumerate(chips)]
        for j, chip in enumerate(chips):
            copy(1 + j, (*chip, c), me).wait_recv()
            passed[j].start()
        copy(0, sibling, me).wait_recv()
        for j, chip in enumerate(chips):
            copy(4 + j, (*chip, 1 - c), me).wait_recv()
        for cp in first + passed:
            cp.wait_send()
        mine.wait()

    return pl.pallas_call(
        body, name=name, in_specs=[HBM_SPEC], out_specs=HBM_SPEC,
        out_shape=jax.ShapeDtypeStruct((N_DEV,) + shard.shape, shard.dtype),
        scratch_shapes=[pltpu.SemaphoreType.DMA((N_DEV - 1,)), pltpu.SemaphoreType.DMA((N_DEV - 1,)),
                        pltpu.SemaphoreType.DMA],
    )(shard)


def _in_proj_fwd(x, g, w, tm, carry=None):
    T = x.shape[0]

    def body(x_ref, g_ref, w_hbm, proj_ref, h_ref, r_ref, w_ref, sem):
        @pl.when(pl.program_id(0) == 0)
        def _():
            _load_resident(w_hbm, w_ref, sem)

        for rows in _row_chunks(tm, 1):
            xf = x_ref[rows, :]
            r = lax.rsqrt(jnp.mean(xf * xf, axis=-1, keepdims=True) + EPS)
            h = (xf * r * g_ref[...]).astype(BF16)
            h_ref[rows, :] = h
            r_ref[rows, :] = r
            for j in range(N_DEV):
                proj_ref[rows, j * WIN_BLK:(j + 1) * WIN_BLK] = _dot(h, w_ref[j])

    return _pcall(
        body, "in_proj_fwd", (T // tm,),
        [pl.BlockSpec((tm, D_MODEL), lambda i: (i, 0)), pl.BlockSpec((1, D_MODEL), lambda i: (0, 0)), ANY_SPEC],
        [pl.BlockSpec((tm, IN_WIDTH), lambda i: (i, 0)),
         pl.BlockSpec((tm, D_MODEL), lambda i: (i, 0)),
         pl.BlockSpec((tm, 1), lambda i: (i, 0))],
        [jax.ShapeDtypeStruct((T, IN_WIDTH), F32),
         jax.ShapeDtypeStruct((T, D_MODEL), BF16),
         jax.ShapeDtypeStruct((T, 1), F32)],
        [pltpu.VMEM(w.shape, w.dtype), pltpu.SemaphoreType.DMA], (x, g, w), carry)


def _ret_common(q_ref, k_ref, v_ref, cos_ref, sin_ref, mask_ref, rd_ref, sin_state):
    c = cos_ref[...]
    s = sin_ref[...]
    q = q_ref[...]
    q = q * c + pltpu.roll(q, HEAD_DIM // 2, 1) * s
    k = k_ref[...]
    k = (k * c + pltpu.roll(k, HEAD_DIM // 2, 1) * s) * (HEAD_DIM ** -0.5)
    qb = q.astype(BF16)
    kb = k.astype(BF16)
    vb = v_ref[...].astype(BF16)
    pm = (_dot_nt(qb, kb) * mask_ref[...]).astype(BF16)
    qd = (q * rd_ref[...]).astype(BF16)
    o = _dot(pm, vb) + _dot(qd, sin_state.astype(BF16))
    return q, k, qb, kb, vb, pm, qd, o


def _ret_specs(T, rev):
    nb = T // RET_BLOCK
    groups = RET_HEADS // RET_HPS
    wide = RET_HPS * HEAD_DIM
    blk = (lambda b: nb - 1 - b) if rev else (lambda b: b)
    col = lambda piece: (pl.BlockSpec((RET_BLOCK, wide), lambda h, b: (blk(b), piece * groups + h)), "lane")
    return dict(
        q=col(0), k=col(1), v=col(2), g=col(3),
        tab=(pl.BlockSpec((RET_BLOCK, HEAD_DIM), lambda h, b: (blk(b), 0)), None),
        mask=(pl.BlockSpec((RET_HPS, RET_BLOCK, RET_BLOCK), lambda h, b: (h, 0, 0)), "lead"),
        dec=(pl.BlockSpec((RET_HPS, RET_BLOCK, HEAD_DIM), lambda h, b: (h, 0, 0)), "lead"),
        gtb=(pl.BlockSpec((RET_HPS, 1, HEAD_DIM), lambda h, b: (h, 0, 0)), "lead"),
        gn=(pl.BlockSpec((1, wide), lambda h, b: (0, h)), "lane"),
        state=(pl.BlockSpec((RET_HPS, None, HEAD_DIM, HEAD_DIM), lambda h, b: (h, blk(b), 0, 0)), "lead"),
        rows=(pl.BlockSpec((RET_BLOCK, wide), lambda h, b: (blk(b), h)), "lane"),
        scratch=(pltpu.VMEM((RET_HPS, HEAD_DIM, HEAD_DIM), F32), "lead"),
    )


def _per_head(head_body, kinds):
    def body(*refs):
        for hh in range(RET_HPS):
            views = []
            for ref, kind in zip(refs, kinds):
                if kind == "lane":
                    views.append(ref.at[:, hh * HEAD_DIM:(hh + 1) * HEAD_DIM])
                elif kind == "lead":
                    views.append(ref.at[hh])
                else:
                    views.append(ref)
            head_body(*views)
    return body


def _ret_fwd(proj, cosf, sinf, mask, rowdec, kdec, gtb, gn, carry=None):
    T = proj.shape[0]
    nb = T // RET_BLOCK
    sp = _ret_specs(T, False)

    def body(q_ref, k_ref, v_ref, g_ref, cos_ref, sin_ref, mask_ref, rd_ref, kd_ref, gtb_ref, gn_ref,
             y_ref, sb_ref, st):
        @pl.when(pl.program_id(1) == 0)
        def _():
            st[...] = jnp.zeros_like(st)
        s_in = st[...]
        sb_ref[...] = s_in
        q, k, qb, kb, vb, pm, qd, o = _ret_common(q_ref, k_ref, v_ref, cos_ref, sin_ref, mask_ref, rd_ref, s_in)
        st[...] = gtb_ref[...] * s_in + _dot_tn((k * kd_ref[...]).astype(BF16), vb)
        mu = jnp.mean(o, axis=-1, keepdims=True)
        oc = o - mu
        n = oc * lax.rsqrt(jnp.mean(oc * oc, axis=-1, keepdims=True) + EPS)
        gt = g_ref[...]
        y_ref[...] = (gt * _sigmoid(gt) * (n * gn_ref[...])).astype(BF16)

    ins = [sp[n] for n in ("q", "k", "v", "g", "tab", "tab", "mask", "dec", "dec", "gtb", "gn")]
    outs = [sp["rows"], sp["state"]]
    return _pcall(
        _per_head(body, [kind for _, kind in ins + outs + [sp["scratch"]]]), "ret_fwd", (RET_HEADS // RET_HPS, nb),
        [s for s, _ in ins], [s for s, _ in outs],
        [jax.ShapeDtypeStruct((T, RET_WIDTH), BF16),
         jax.ShapeDtypeStruct((RET_HEADS, nb, HEAD_DIM, HEAD_DIM), F32)],
        [sp["scratch"][0]],
        (proj, proj, proj, proj, cosf, sinf, mask, rowdec, kdec, gtb, gn), carry)


def _scan(re, im, ar, ai, reverse):
    n = re.shape[0]
    row = lax.broadcasted_iota(jnp.int32, re.shape, 0)
    s = 1
    while s < n:
        if reverse:
            keep = row < n - s
            sr = jnp.where(keep, pltpu.roll(re, n - s, 0), 0.0)
            si = jnp.where(keep, pltpu.roll(im, n - s, 0), 0.0)
        else:
            keep = row >= s
            sr = jnp.where(keep, pltpu.roll(re, s, 0), 0.0)
            si = jnp.where(keep, pltpu.roll(im, s, 0), 0.0)
        re, im = re + ar * sr - ai * si, im + ar * si + ai * sr
        ar, ai = ar * ar - ai * ai, 2.0 * ar * ai
        s *= 2
    return re, im


S5_STATE_TILE = (S5_TILE, S5_LANES)


def _step_major_permutation():
    r = jnp.arange(S5_TILE)
    t_of_row = (r % S5_CHUNKS) * S5_STEPS + r // S5_CHUNKS
    return (t_of_row[:, None] == r[None, :]).astype(BF16)


def _permute_rows_f32(pm, x):
    hi = x.astype(BF16)
    rest = x - hi.astype(F32)
    mid = rest.astype(BF16)
    lo = (rest - mid.astype(F32)).astype(BF16)
    return _dot(pm, hi) + _dot(pm, mid) + _dot(pm, lo)


def _step_get(ref, j):
    return ref[j * S5_CHUNKS:(j + 1) * S5_CHUNKS, :]


def _step_set(ref, j, val):
    ref[j * S5_CHUNKS:(j + 1) * S5_CHUNKS, :] = val


def _tile_get(ref):
    return ref[...]


def _tile_set(ref, val):
    ref[...] = val


def _fill_power_table(ptab, lr, li):
    shape = (S5_CHUNKS, S5_LANES)
    lrb = jnp.broadcast_to(lr, shape)
    lib = jnp.broadcast_to(li, shape)
    pr, pi_ = lrb, lib
    for j in range(S5_STEPS):
        ptab[0, j * S5_CHUNKS:(j + 1) * S5_CHUNKS, :] = pr
        ptab[1, j * S5_CHUNKS:(j + 1) * S5_CHUNKS, :] = pi_
        pr, pi_ = lrb * pr - lib * pi_, lrb * pi_ + lib * pr


def _chunk_scans(xr, xi, lr, li, reverse):
    shape = (S5_CHUNKS, S5_LANES)
    lrb = jnp.broadcast_to(lr, shape)
    lib = jnp.broadcast_to(li, shape)
    sr = si = None
    for j in (range(S5_STEPS - 1, -1, -1) if reverse else range(S5_STEPS)):
        vr = _step_get(xr, j)
        vi = _step_get(xi, j)
        if sr is not None:
            vr, vi = vr + lrb * sr - lib * si, vi + lrb * si + lib * sr
            _step_set(xr, j, vr)
            _step_set(xi, j, vi)
        sr, si = vr, vi
    return sr, si


def _entering_states(zr, zi, cr, ci, ar, ai, reverse):
    shape = (S5_CHUNKS, S5_LANES)
    row = lax.broadcasted_iota(jnp.int32, shape, 0)
    if reverse:
        edge, shift = row == S5_CHUNKS - 1, S5_CHUNKS - 1
    else:
        edge, shift = row == 0, 1
    wr = jnp.where(edge, jnp.broadcast_to(cr, shape), pltpu.roll(zr, shift, 0))
    wi = jnp.where(edge, jnp.broadcast_to(ci, shape), pltpu.roll(zi, shift, 0))
    return _scan(wr, wi, ar, ai, reverse)


def _table_rows(ptab, j, conj):
    pr = ptab[0, j * S5_CHUNKS:(j + 1) * S5_CHUNKS, :]
    pi_ = ptab[1, j * S5_CHUNKS:(j + 1) * S5_CHUNKS, :]
    return pr, (-pi_ if conj else pi_)


def _s5_forward_states(xr, xi, lr, li, cr, ci, ptab):
    zr, zi = _chunk_scans(xr, xi, lr, li, False)
    ar, ai = _table_rows(ptab, S5_STEPS - 1, False)
    er, ei = _entering_states(zr, zi, cr, ci, ar, ai, False)
    for j in range(S5_STEPS):
        pr, pi_ = _table_rows(ptab, j, False)
        _step_set(xr, j, _step_get(xr, j) + pr * er - pi_ * ei)
        _step_set(xi, j, _step_get(xi, j) + pr * ei + pi_ * er)
    last = S5_CHUNKS - 1
    end_r = (ar * er - ai * ei + zr)[last:last + 1, :]
    end_i = (ar * ei + ai * er + zi)[last:last + 1, :]
    return er, ei, end_r, end_i


def _s5_specs(T, rev):
    nt = T // S5_TILE
    tt = (lambda t: nt - 1 - t) if rev else (lambda t: t)
    return dict(
        u=pl.BlockSpec((S5_TILE, LANE), lambda b, t: (tt(t), 4 * RET_HEADS + b)),
        rows=pl.BlockSpec((S5_TILE, LANE), lambda b, t: (tt(t), b)),
        to_state=pl.BlockSpec((None, LANE, S5_LANES), lambda b, t: (b, 0, 0)),
        from_state=pl.BlockSpec((None, S5_LANES, LANE), lambda b, t: (b, 0, 0)),
        lam=pl.BlockSpec((None, 2, S5_LANES), lambda b, t: (b, 0, 0)),
        d=pl.BlockSpec((1, LANE), lambda b, t: (0, b)),
        perm=pl.BlockSpec((S5_TILE, S5_TILE), lambda b, t: (0, 0)),
        bound=pl.BlockSpec((None, None, 2, S5_LANES), lambda b, t: (b, tt(t), 0, 0)),
    )


def _s5_fwd(proj, pm, pm_t, bre, bim, cre_t, cim_t, lam, d, carry=None):
    T = proj.shape[0]
    nt = T // S5_TILE
    sp = _s5_specs(T, False)

    def body(u_ref, pm_ref, pmt_ref, bre_ref, bim_ref, cre_ref, cim_ref, lam_ref, d_ref, y_ref, bound_ref,
             carry, ptab, xr, xi):
        lr = lam_ref[0:1, :]
        li = lam_ref[1:2, :]

        @pl.when(pl.program_id(1) == 0)
        def _():
            carry[...] = jnp.zeros_like(carry)
            _fill_power_table(ptab, lr, li)

        u = _permute_rows_f32(pm_ref[...], u_ref[...])
        ub = u.astype(BF16)
        _tile_set(xr, _dot(ub, bre_ref[...]))
        _tile_set(xi, _dot(ub, bim_ref[...]))
        bound_ref[...] = carry[...]
        _, _, end_r, end_i = _s5_forward_states(xr, xi, lr, li, carry[0:1, :], carry[1:2, :], ptab)
        carry[0:1, :] = end_r
        carry[1:2, :] = end_i
        y = (_dot(_tile_get(xr).astype(BF16), cre_ref[...]) - _dot(_tile_get(xi).astype(BF16), cim_ref[...])
             + d_ref[...] * u)
        y_ref[...] = _permute_rows_f32(pmt_ref[...], y)

    state = pltpu.VMEM(S5_STATE_TILE, F32)
    return _pcall(
        body, "s5_fwd", (S5_NBLK, nt),
        [sp["u"], sp["perm"], sp["perm"], sp["to_state"], sp["to_state"], sp["from_state"],
         sp["from_state"], sp["lam"], sp["d"]],
        [sp["rows"], sp["bound"]],
        [jax.ShapeDtypeStruct((T, SSM_WIDTH), F32),
         jax.ShapeDtypeStruct((S5_NBLK, nt, 2, S5_LANES), F32)],
        [pltpu.VMEM((2, S5_LANES), F32), pltpu.VMEM((2, S5_TILE, S5_LANES), F32), state, state],
        (proj, pm, pm_t, bre, bim, cre_t, cim_t, lam, d), carry)


def _glu_fwd(y, w, b, og, tm):
    T = y.shape[0]

    def body(y_ref, w_ref, b_ref, og_ref, z_ref, o_ref, r_ref):
        y1 = _gelu(y_ref[...])
        z = _dot(y1.astype(BF16), w_ref[...]) + b_ref[...]
        y2 = y1 * _sigmoid(z)
        r = lax.rsqrt(jnp.mean(y2 * y2, axis=-1, keepdims=True) + EPS)
        z_ref[...] = z
        o_ref[...] = (y2 * r * og_ref[...]).astype(BF16)
        r_ref[...] = r

    row = pl.BlockSpec((tm, SSM_WIDTH), lambda i: (i, 0))
    vec = pl.BlockSpec((1, SSM_WIDTH), lambda i: (0, 0))
    return pl.pallas_call(
        body, name="glu_fwd", grid=(T // tm,),
        in_specs=[row, pl.BlockSpec((SSM_WIDTH, SSM_WIDTH), lambda i: (0, 0)), vec, vec],
        out_specs=[row, row, pl.BlockSpec((tm, 1), lambda i: (i, 0))],
        out_shape=[jax.ShapeDtypeStruct((T, SSM_WIDTH), F32), jax.ShapeDtypeStruct((T, SSM_WIDTH), BF16),
                   jax.ShapeDtypeStruct((T, 1), F32)],
        compiler_params=_params(1),
    )(y, w, b, og)


def _out_proj_fwd(x, y_ret, y_ssm, w, g, tm):
    T = x.shape[0]

    def body(x_ref, a_ref, b_ref, w_ref, g_ref, x2_ref, h_ref, r_ref):
        for rows in _row_chunks(tm, 1):
            x2 = (x_ref[rows, :] + _dot(a_ref[rows, :], w_ref[0:RET_WIDTH, :])
                  + _dot(b_ref[rows, :], w_ref[RET_WIDTH:D_MODEL, :]))
            r = lax.rsqrt(jnp.mean(x2 * x2, axis=-1, keepdims=True) + EPS)
            x2_ref[rows, :] = x2
            h_ref[rows, :] = (x2 * r * g_ref[...]).astype(BF16)
            r_ref[rows, :] = r

    full = pl.BlockSpec((tm, D_MODEL), lambda i: (i, 0))
    half = pl.BlockSpec((tm, RET_WIDTH), lambda i: (i, 0))
    return pl.pallas_call(
        body, name="out_proj_fwd", grid=(T // tm,),
        in_specs=[full, half, half, pl.BlockSpec((D_MODEL, D_MODEL), lambda i: (0, 0)),
                  pl.BlockSpec((1, D_MODEL), lambda i: (0, 0))],
        out_specs=[full, full, pl.BlockSpec((tm, 1), lambda i: (i, 0))],
        out_shape=[jax.ShapeDtypeStruct((T, D_MODEL), F32), jax.ShapeDtypeStruct((T, D_MODEL), BF16),
                   jax.ShapeDtypeStruct((T, 1), F32)],
        compiler_params=_params(1),
    )(x, y_ret, y_ssm, w, g)


def _ffn_up(h, wg, wu, tm, carry=None):
    T = h.shape[0]

    def body(h_ref, wg_ref, wu_ref, a_ref, b_ref, f_ref):
        for rows in _row_chunks(tm, 1):
            hb = h_ref[rows, :]
            a = _dot(hb, wg_ref[...])
            b = _dot(hb, wu_ref[...])
            a_ref[rows, :] = a.astype(BF16)
            b_ref[rows, :] = b.astype(BF16)
            f_ref[rows, :] = (a * _sigmoid(a) * b).astype(BF16)

    wspec = pl.BlockSpec((None, D_MODEL, FF_BLK), lambda j, i: (j, 0, 0))
    ospec = pl.BlockSpec((None, tm, FF_BLK), lambda j, i: (j, i, 0))
    oshape = jax.ShapeDtypeStruct((N_DEV, T, FF_BLK), BF16)
    return _pcall(
        body, "ffn_up", (N_DEV, T // tm),
        [pl.BlockSpec((tm, D_MODEL), lambda j, i: (i, 0)), wspec, wspec],
        [ospec, ospec, ospec], [oshape, oshape, oshape], [], (h, wg, wu), carry)


def _ffn_down_loss(f, wd, x2, tgt, g, tm):
    T = x2.shape[0]

    def body(f_ref, w_hbm, x2_ref, t_ref, g_ref, dx_ref, dxb_ref, loss_ref, dg_ref, w_ref, sem):
        i = pl.program_id(0)

        @pl.when(i == 0)
        def _():
            _load_resident(w_hbm, w_ref, sem)
            loss_ref[...] = jnp.zeros_like(loss_ref)
            dg_ref[...] = jnp.zeros_like(dg_ref)

        gv = g_ref[...]
        for rows in _row_chunks(tm, 1):
            x3 = x2_ref[rows, :]
            for k in range(N_DEV):
                x3 = x3 + _dot(f_ref[k, rows, :], w_ref[k])
            r = lax.rsqrt(jnp.mean(x3 * x3, axis=-1, keepdims=True) + EPS)
            err = x3 * r * gv - t_ref[rows, :]
            part_loss = 0.5 * jnp.sum(jnp.mean(err * err, axis=-1, keepdims=True), axis=0, keepdims=True)
            dx, dgt = _rms_bwd(err * (1.0 / D_MODEL), x3, r, gv)
            dx_ref[rows, :] = dx
            dxb_ref[rows, :] = dx.astype(BF16)
            loss_ref[...] += jnp.broadcast_to(part_loss, loss_ref.shape)
            dg_ref[...] += jnp.sum(dgt, axis=0, keepdims=True)

    full = pl.BlockSpec((tm, D_MODEL), lambda i: (i, 0))
    vec = pl.BlockSpec((1, D_MODEL), lambda i: (0, 0))
    return pl.pallas_call(
        body, name="ffn_down_loss", grid=(T // tm,),
        in_specs=[pl.BlockSpec((N_DEV, tm, FF_BLK), lambda i: (0, i, 0)), ANY_SPEC, full, full, vec],
        out_specs=[full, full, pl.BlockSpec((8, LANE), lambda i: (0, 0)), vec],
        out_shape=[jax.ShapeDtypeStruct((T, D_MODEL), F32), jax.ShapeDtypeStruct((T, D_MODEL), BF16),
                   jax.ShapeDtypeStruct((8, LANE), F32), jax.ShapeDtypeStruct((1, D_MODEL), F32)],
        scratch_shapes=[pltpu.VMEM(wd.shape, wd.dtype), pltpu.SemaphoreType.DMA],
        compiler_params=_params(1),
    )(f, wd, x2, tgt, g)


def _row_chunks(tm, n):
    return [slice(c * (tm // n), (c + 1) * (tm // n)) for c in range(n)]


def _ffn_bwd_act(dxb, wd, a, b, tm):
    T = dxb.shape[0]

    def body(dx_ref, w_ref, a_ref, b_ref, da_ref, db_ref):
        for rows in _row_chunks(tm, 1):
            df = _dot_nt(dx_ref[rows, :], w_ref[...])
            a = a_ref[rows, :].astype(F32)
            b = b_ref[rows, :].astype(F32)
            sg = _sigmoid(a)
            da_ref[rows, :] = (df * b * sg * (1.0 + a * (1.0 - sg))).astype(BF16)
            db_ref[rows, :] = (df * a * sg).astype(BF16)

    blk = pl.BlockSpec((None, tm, FF_BLK), lambda j, i: (j, i, 0))
    oshape = jax.ShapeDtypeStruct((N_DEV, T, FF_BLK), BF16)
    return pl.pallas_call(
        body, name="ffn_bwd_act", grid=(N_DEV, T // tm),
        in_specs=[pl.BlockSpec((tm, D_MODEL), lambda j, i: (i, 0)),
                  pl.BlockSpec((None, FF_BLK, D_MODEL), lambda j, i: (j, 0, 0)), blk, blk],
        out_specs=[blk, blk], out_shape=[oshape, oshape],
        compiler_params=_params(2),
    )(dxb, wd, a, b)


def _ffn_bwd_in(da, db, wg, wu, tm, carry=None):
    T = da.shape[1]

    def body(da_ref, db_ref, wg_ref, wu_ref, dh_ref):
        part = _dot_nt(da_ref[...], wg_ref[...]) + _dot_nt(db_ref[...], wu_ref[...])

        @pl.when(pl.program_id(1) == 0)
        def _():
            dh_ref[...] = part

        @pl.when(pl.program_id(1) > 0)
        def _():
            dh_ref[...] += part

    ablk = pl.BlockSpec((None, tm, FF_BLK), lambda i, k: (k, i, 0))
    wblk = pl.BlockSpec((None, D_MODEL, FF_BLK), lambda i, k: (k, 0, 0))
    return _pcall(
        body, "ffn_bwd_in", (T // tm, N_DEV), [ablk, ablk, wblk, wblk],
        [pl.BlockSpec((tm, D_MODEL), lambda i, k: (i, 0))], [jax.ShapeDtypeStruct((T, D_MODEL), F32)],
        [], (da, db, wg, wu), carry)


def _ffn_wgrad_up(h, da, db, tk, carry=None):
    T = h.shape[0]
    nk = T // tk

    def body(h_ref, da_ref, db_ref, g_ref, u_ref, accg, accu):
        k = pl.program_id(1)

        @pl.when(k == 0)
        def _():
            accg[...] = jnp.zeros_like(accg)
            accu[...] = jnp.zeros_like(accu)

        hb = h_ref[...]
        accg[...] += _dot_tn(hb, da_ref[...])
        accu[...] += _dot_tn(hb, db_ref[...])

        @pl.when(k == nk - 1)
        def _():
            g_ref[...] = accg[...].astype(BF16)
            u_ref[...] = accu[...].astype(BF16)

    blk = pl.BlockSpec((None, tk, FF_BLK), lambda j, k: (j, k, 0))
    ospec = pl.BlockSpec((None, D_MODEL, FF_BLK), lambda j, k: (j, 0, 0))
    oshape = jax.ShapeDtypeStruct((N_DEV, D_MODEL, FF_BLK), BF16)
    return _pcall(
        body, "ffn_wgrad_up", (N_DEV, nk),
        [pl.BlockSpec((tk, D_MODEL), lambda j, k: (k, 0)), blk, blk],
        [ospec, ospec], [oshape, oshape],
        [pltpu.VMEM((D_MODEL, FF_BLK), F32), pltpu.VMEM((D_MODEL, FF_BLK), F32)], (h, da, db), carry)


def _ffn_wgrad_down(f, dxb, tk):
    T = dxb.shape[0]
    nk = T // tk

    def body(f_ref, dx_ref, o_ref, acc):
        k = pl.program_id(1)

        @pl.when(k == 0)
        def _():
            acc[...] = jnp.zeros_like(acc)

        acc[...] += _dot_tn(f_ref[...], dx_ref[...])

        @pl.when(k == nk - 1)
        def _():
            o_ref[...] = acc[...].astype(BF16)

    return pl.pallas_call(
        body, name="ffn_wgrad_down", grid=(N_DEV, nk),
        in_specs=[pl.BlockSpec((None, tk, FF_BLK), lambda j, k: (j, k, 0)),
                  pl.BlockSpec((tk, D_MODEL), lambda j, k: (k, 0))],
        out_specs=pl.BlockSpec((None, FF_BLK, D_MODEL), lambda j, k: (j, 0, 0)),
        out_shape=jax.ShapeDtypeStruct((N_DEV, FF_BLK, D_MODEL), BF16),
        scratch_shapes=[pltpu.VMEM((FF_BLK, D_MODEL), F32)],
        compiler_params=_params(2),
    )(f, dxb)


def _out_proj_bwd(dh2, x2, r2, g, dx3, w, tm):
    T = x2.shape[0]

    def body(dh_ref, x_ref, r_ref, g_ref, dx3_ref, w_ref, dx_ref, dxb_ref, dg_ref, a_ref, b_ref):
        @pl.when(pl.program_id(0) == 0)
        def _():
            dg_ref[...] = jnp.zeros_like(dg_ref)

        for rows in _row_chunks(tm, 1):
            dxn, dgt = _rms_bwd(dh_ref[rows, :], x_ref[rows, :], r_ref[rows, :], g_ref[...])
            dx = dx3_ref[rows, :] + dxn
            dxv = dx.astype(BF16)
            dx_ref[rows, :] = dx
            dxb_ref[rows, :] = dxv
            dg_ref[...] += jnp.sum(dgt, axis=0, keepdims=True)
            a_ref[rows, :] = _dot_nt(dxv, w_ref[0:RET_WIDTH, :])
            b_ref[rows, :] = _dot_nt(dxv, w_ref[RET_WIDTH:D_MODEL, :])

    full = pl.BlockSpec((tm, D_MODEL), lambda i: (i, 0))
    vec = pl.BlockSpec((1, D_MODEL), lambda i: (0, 0))
    half = pl.BlockSpec((tm, RET_WIDTH), lambda i: (i, 0))
    hshape = jax.ShapeDtypeStruct((T, RET_WIDTH), F32)
    return pl.pallas_call(
        body, name="out_proj_bwd", grid=(T // tm,),
        in_specs=[full, full, pl.BlockSpec((tm, 1), lambda i: (i, 0)), vec, full,
                  pl.BlockSpec((D_MODEL, D_MODEL), lambda i: (0, 0))],
        out_specs=[full, full, vec, half, half],
        out_shape=[jax.ShapeDtypeStruct((T, D_MODEL), F32), jax.ShapeDtypeStruct((T, D_MODEL), BF16),
                   jax.ShapeDtypeStruct((1, D_MODEL), F32), hshape, hshape],
        compiler_params=_params(1),
    )(dh2, x2, r2, g, dx3, w)


def _wgrad_rows(name, a, b, tk):
    T, M = a.shape
    N = b.shape[1]
    nk = T // tk

    def body(a_ref, b_ref, o_ref, acc):
        k = pl.program_id(0)

        @pl.when(k == 0)
        def _():
            acc[...] = jnp.zeros_like(acc)

        acc[...] += _dot_tn(a_ref[...], b_ref[...])

        @pl.when(k == nk - 1)
        def _():
            o_ref[...] = acc[...].astype(BF16)

    return pl.pallas_call(
        body, name=name, grid=(nk,),
        in_specs=[pl.BlockSpec((tk, M), lambda k: (k, 0)), pl.BlockSpec((tk, N), lambda k: (k, 0))],
        out_specs=pl.BlockSpec((M, N), lambda k: (0, 0)),
        out_shape=jax.ShapeDtypeStruct((M, N), BF16),
        scratch_shapes=[pltpu.VMEM((M, N), F32)],
        compiler_params=_params(1),
    )(a, b)


def _glu_bwd(y, z, r, dyo, w, og, tm):
    T = y.shape[0]

    def body(y_ref, z_ref, r_ref, d_ref, w_ref, og_ref, dy_ref, dw_ref, db_ref, dog_ref):
        @pl.when(pl.program_id(0) == 0)
        def _():
            dw_ref[...] = jnp.zeros_like(dw_ref)
            db_ref[...] = jnp.zeros_like(db_ref)
            dog_ref[...] = jnp.zeros_like(dog_ref)

        y1, g1 = _gelu_and_grad(y_ref[...])
        sg = _sigmoid(z_ref[...])
        y2 = y1 * sg
        dy2, dogt = _rms_bwd(d_ref[...], y2, r_ref[...], og_ref[...])
        dog_ref[...] += jnp.sum(dogt, axis=0, keepdims=True)
        dz = dy2 * y1 * sg * (1.0 - sg)
        db_ref[...] += jnp.sum(dz, axis=0, keepdims=True)
        dzb = dz.astype(BF16)
        dw_ref[...] += _dot_tn(y1.astype(BF16), dzb)
        dy_ref[...] = (dy2 * sg + _dot_nt(dzb, w_ref[...])) * g1

    row = pl.BlockSpec((tm, SSM_WIDTH), lambda i: (i, 0))
    vec = pl.BlockSpec((1, SSM_WIDTH), lambda i: (0, 0))
    sq = pl.BlockSpec((SSM_WIDTH, SSM_WIDTH), lambda i: (0, 0))
    return pl.pallas_call(
        body, name="glu_bwd", grid=(T // tm,),
        in_specs=[row, row, pl.BlockSpec((tm, 1), lambda i: (i, 0)), row, sq, vec],
        out_specs=[row, sq, vec, vec],
        out_shape=[jax.ShapeDtypeStruct((T, SSM_WIDTH), F32), jax.ShapeDtypeStruct((SSM_WIDTH, SSM_WIDTH), F32),
                   jax.ShapeDtypeStruct((1, SSM_WIDTH), F32), jax.ShapeDtypeStruct((1, SSM_WIDTH), F32)],
        compiler_params=_params(1),
    )(y, z, r, dyo, w, og)


def _s5_bwd(proj, dy, bound, pm, pm_t, bre, bim, bre_t, bim_t, cre, cim, lam, d, carry=None):
    T = proj.shape[0]
    nt = T // S5_TILE
    sp = _s5_specs(T, True)

    def body(u_ref, dy_ref, bound_ref, pm_ref, pmt_ref, bre_ref, bim_ref, bret_ref, bimt_ref, cre_ref, cim_ref,
             lam_ref, d_ref,
             du_ref, dbre_ref, dbim_ref, dcre_ref, dcim_ref, dlam_ref, dd_ref, carry, ptab, sr, si, gr, gi):
        lr = lam_ref[0:1, :]
        li = lam_ref[1:2, :]

        @pl.when(pl.program_id(1) == 0)
        def _():
            carry[...] = jnp.zeros_like(carry)
            _fill_power_table(ptab, lr, li)
            for ref in (dbre_ref, dbim_ref, dcre_ref, dcim_ref, dlam_ref, dd_ref):
                ref[...] = jnp.zeros_like(ref)

        u = _permute_rows_f32(pm_ref[...], u_ref[...])
        ub = u.astype(BF16)
        dyv = _permute_rows_f32(pm_ref[...], dy_ref[...])
        dyb = dyv.astype(BF16)
        _tile_set(sr, _dot(ub, bre_ref[...]))
        _tile_set(si, _dot(ub, bim_ref[...]))
        er, ei, _, _ = _s5_forward_states(sr, si, lr, li, bound_ref[0:1, :], bound_ref[1:2, :], ptab)
        _tile_set(gr, _dot(dyb, cre_ref[...]))
        _tile_set(gi, -_dot(dyb, cim_ref[...]))
        zr, zi = _chunk_scans(gr, gi, lr, -li, True)
        ar, ai = _table_rows(ptab, S5_STEPS - 1, True)
        fr, fi = _entering_states(zr, zi, carry[0:1, :], carry[1:2, :], ar, ai, True)
        acc_r = jnp.zeros((S5_CHUNKS, S5_LANES), F32)
        acc_i = jnp.zeros((S5_CHUNKS, S5_LANES), F32)
        for j in range(S5_STEPS):
            qr, qi = _table_rows(ptab, S5_STEPS - 1 - j, True)
            g_r = _step_get(gr, j) + qr * fr - qi * fi
            g_i = _step_get(gi, j) + qr * fi + qi * fr
            _step_set(gr, j, g_r)
            _step_set(gi, j, g_i)
            p_r, p_i = (er, ei) if j == 0 else (_step_get(sr, j - 1), _step_get(si, j - 1))
            acc_r += g_r * p_r + g_i * p_i
            acc_i += g_i * p_r - g_r * p_i
        dlam_ref[0:1, :] += jnp.sum(acc_r, axis=0, keepdims=True)
        dlam_ref[1:2, :] += jnp.sum(acc_i, axis=0, keepdims=True)
        g_all_r = _tile_get(gr)
        g_all_i = _tile_get(gi)
        carry[0:1, :] = g_all_r[0:1, :]
        carry[1:2, :] = g_all_i[0:1, :]
        grb = g_all_r.astype(BF16)
        gib = g_all_i.astype(BF16)
        du = (_dot(grb, bret_ref[...]) + _dot(gib, bimt_ref[...]) + d_ref[...] * dyv).astype(BF16)
        du_ref[...] = _dot(pmt_ref[...], du).astype(BF16)
        dbre_ref[...] += _dot_tn(grb, ub)
        dbim_ref[...] += _dot_tn(gib, ub)
        dcre_ref[...] += _dot_tn(dyb, _tile_get(sr).astype(BF16))
        dcim_ref[...] -= _dot_tn(dyb, _tile_get(si).astype(BF16))
        dd_ref[...] += jnp.sum(dyv * u, axis=0, keepdims=True)

    acc_ts = pl.BlockSpec((None, S5_LANES, LANE), lambda b, t: (b, 0, 0))
    acc_fs = pl.BlockSpec((None, LANE, S5_LANES), lambda b, t: (b, 0, 0))
    return _pcall(
        body, "s5_bwd", (S5_NBLK, nt),
        [sp["u"], sp["rows"], sp["bound"], sp["perm"], sp["perm"], sp["to_state"], sp["to_state"],
         sp["from_state"], sp["from_state"], sp["to_state"], sp["to_state"], sp["lam"], sp["d"]],
        [sp["rows"], acc_ts, acc_ts, acc_fs, acc_fs, sp["lam"], sp["d"]],
        [jax.ShapeDtypeStruct((T, SSM_WIDTH), BF16),
         jax.ShapeDtypeStruct((S5_NBLK, S5_LANES, LANE), F32),
         jax.ShapeDtypeStruct((S5_NBLK, S5_LANES, LANE), F32),
         jax.ShapeDtypeStruct((S5_NBLK, LANE, S5_LANES), F32),
         jax.ShapeDtypeStruct((S5_NBLK, LANE, S5_LANES), F32),
         jax.ShapeDtypeStruct((S5_NBLK, 2, S5_LANES), F32),
         jax.ShapeDtypeStruct((1, SSM_WIDTH), F32)],
        [pltpu.VMEM((2, S5_LANES), F32), pltpu.VMEM((2, S5_TILE, S5_LANES), F32)]
        + [pltpu.VMEM(S5_STATE_TILE, F32)] * 4,
        (proj, dy, bound, pm, pm_t, bre, bim, bre_t, bim_t, cre, cim, lam, d), carry)


def _ret_bwd(proj, cosf, sinf, mask, rowdec, kdec, gtb, gn, sblk, dyr):
    T = proj.shape[0]
    nb = T // RET_BLOCK
    sp = _ret_specs(T, True)

    def body(q_ref, k_ref, v_ref, g_ref, cos_ref, sin_ref, mask_ref, rd_ref, kd_ref, gtb_ref, gn_ref, sb_ref, dy_ref,
             dq_ref, dk_ref, dv_ref, dg_ref, dgn_ref, dst):
        @pl.when(pl.program_id(1) == 0)
        def _():
            dst[...] = jnp.zeros_like(dst)
            dgn_ref[...] = jnp.zeros_like(dgn_ref)

        s_in = sb_ref[...]
        q, k, qb, kb, vb, pm, qd, o = _ret_common(q_ref, k_ref, v_ref, cos_ref, sin_ref, mask_ref, rd_ref, s_in)
        mu = jnp.mean(o, axis=-1, keepdims=True)
        oc = o - mu
        rstd = lax.rsqrt(jnp.mean(oc * oc, axis=-1, keepdims=True) + EPS)
        n = oc * rstd
        gt = g_ref[...]
        sg = _sigmoid(gt)
        sil = gt * sg
        gnv = gn_ref[...]
        dyv = dy_ref[...]
        dg_ref[...] = (dyv * (n * gnv) * (sg * (1.0 + gt * (1.0 - sg)))).astype(BF16)
        dgn_ref[...] += jnp.sum(dyv * sil * n, axis=0, keepdims=True)
        dn = dyv * sil * gnv
        do = rstd * (dn - jnp.mean(dn, axis=-1, keepdims=True) - n * jnp.mean(dn * n, axis=-1, keepdims=True))
        dob = do.astype(BF16)
        ds = dst[...]
        dsb = ds.astype(BF16)
        kd = kd_ref[...]
        rd = rd_ref[...]
        dv_ref[...] = (_dot_tn(pm, dob) + _dot((k * kd).astype(BF16), dsb)).astype(BF16)
        dpb = (_dot_nt(dob, vb) * mask_ref[...]).astype(BF16)
        dq = _dot(dpb, kb) + _dot_nt(dob, s_in.astype(BF16)) * rd
        dk = (_dot_tn(dpb, qb) + _dot_nt(vb, dsb) * kd) * (HEAD_DIM ** -0.5)
        dst[...] = gtb_ref[...] * ds + _dot_tn(qd, dob)
        c = cos_ref[...]
        s = sin_ref[...]
        dq_ref[...] = (dq * c + pltpu.roll(dq * s, HEAD_DIM // 2, 1)).astype(BF16)
        dk_ref[...] = (dk * c + pltpu.roll(dk * s, HEAD_DIM // 2, 1)).astype(BF16)

    oshape = jax.ShapeDtypeStruct((T, RET_WIDTH), BF16)
    ins = [sp[n] for n in ("q", "k", "v", "g", "tab", "tab", "mask", "dec", "dec", "gtb", "gn", "state", "rows")]
    outs = [sp["rows"], sp["rows"], sp["rows"], sp["rows"], sp["gn"]]
    return pl.pallas_call(
        _per_head(body, [kind for _, kind in ins + outs + [sp["scratch"]]]), name="ret_bwd",
        grid=(RET_HEADS // RET_HPS, nb), in_specs=[s for s, _ in ins], out_specs=[s for s, _ in outs],
        out_shape=[oshape, oshape, oshape, oshape, jax.ShapeDtypeStruct((1, RET_WIDTH), F32)],
        scratch_shapes=[sp["scratch"][0]],
        compiler_params=_params(2),
    )(proj, proj, proj, proj, cosf, sinf, mask, rowdec, kdec, gtb, gn, sblk, dyr)


def _in_proj_bwd(dproj, w, x, r1, g, dx2, tm, carry=None):
    T = x.shape[0]

    def body(dp_ref, w_hbm, x_ref, r_ref, g_ref, dx2_ref, gx_ref, dg_ref, w_ref, sem):
        @pl.when(pl.program_id(0) == 0)
        def _():
            _load_resident(w_hbm, w_ref, sem)
            dg_ref[...] = jnp.zeros_like(dg_ref)

        dh = _dot_nt(dp_ref[:, 0:WIN_BLK], w_ref[0])
        for k in range(1, N_DEV):
            dh = dh + _dot_nt(dp_ref[:, k * WIN_BLK:(k + 1) * WIN_BLK], w_ref[k])
        dxn, dgt = _rms_bwd(dh, x_ref[...], r_ref[...], g_ref[...])
        gx_ref[...] = dx2_ref[...] + dxn
        dg_ref[...] += jnp.sum(dgt, axis=0, keepdims=True)

    full = pl.BlockSpec((tm, D_MODEL), lambda i: (i, 0))
    vec = pl.BlockSpec((1, D_MODEL), lambda i: (0, 0))
    return _pcall(
        body, "in_proj_bwd", (T // tm,),
        [pl.BlockSpec((tm, IN_WIDTH), lambda i: (i, 0)), ANY_SPEC,
         full, pl.BlockSpec((tm, 1), lambda i: (i, 0)), vec, full],
        [full, vec],
        [jax.ShapeDtypeStruct((T, D_MODEL), F32), jax.ShapeDtypeStruct((1, D_MODEL), F32)],
        [pltpu.VMEM(w.shape, w.dtype), pltpu.SemaphoreType.DMA], (dproj, w, x, r1, g, dx2), carry)


def _in_proj_wgrad(h, dproj, tk, carry=None):
    T = h.shape[0]
    nk = T // tk

    def body(h_ref, dp_ref, o_ref, acc):
        k = pl.program_id(1)

        @pl.when(k == 0)
        def _():
            acc[...] = jnp.zeros_like(acc)

        acc[...] += _dot_tn(h_ref[...], dp_ref[...])

        @pl.when(k == nk - 1)
        def _():
            o_ref[...] = acc[...].astype(BF16)

    return _pcall(
        body, "in_proj_wgrad", (N_DEV, nk),
        [pl.BlockSpec((tk, D_MODEL), lambda j, k: (k, 0)), pl.BlockSpec((tk, WIN_BLK), lambda j, k: (k, j))],
        [pl.BlockSpec((None, D_MODEL, WIN_BLK), lambda j, k: (j, 0, 0))],
        [jax.ShapeDtypeStruct((N_DEV, D_MODEL, WIN_BLK), BF16)],
        [pltpu.VMEM((D_MODEL, WIN_BLK), F32)], (h, dproj), carry)


def _rope_tables(T):
    half = HEAD_DIM // 2
    freqs = ROPE_BASE ** (-jnp.arange(half, dtype=F32) / half)
    ang = jnp.arange(T, dtype=F32)[:, None] * freqs[None, :]
    c = jnp.cos(ang)
    s = jnp.sin(ang)
    return jnp.concatenate([c, c], axis=1), jnp.concatenate([-s, s], axis=1)


def _retention_tables():
    hh = jnp.arange(RET_HEADS, dtype=F32)
    log_g = jnp.log1p(-(2.0 ** (-5.0 - hh)))[:, None, None]
    i = jnp.arange(RET_BLOCK)
    ci = (i // CHUNK)[:, None]
    cj = (i // CHUNK)[None, :]
    diff = (i[:, None] - i[None, :]).astype(F32)
    expo = jnp.where(ci == cj, jnp.abs(diff), diff)
    mask = jnp.where((cj <= ci)[None], jnp.exp(log_g * expo[None]), 0.0)
    r = jnp.arange(RET_BLOCK, dtype=F32)[None, :, None]
    ones = jnp.ones((1, 1, HEAD_DIM), F32)
    rowdec = jnp.exp(log_g * (r + 1.0)) * ones
    kdec = jnp.exp(log_g * (RET_BLOCK - 1.0 - r)) * ones
    gtb = jnp.exp(log_g * float(RET_BLOCK)) * ones
    return mask, rowdec, kdec, gtb


def _s5_discretise(a_re, a_im, log_dt, b_re, b_im):
    lam = lax.complex(a_re, a_im)
    dt = jnp.exp(log_dt)[:, None]
    lam_bar = jnp.exp(lam * dt)
    b_bar = ((lam_bar - 1.0) / lam)[..., None] * lax.complex(b_re, b_im)
    return jnp.real(lam_bar), jnp.imag(lam_bar), jnp.real(b_bar), jnp.imag(b_bar)


def _to_state_blockdiag(m):
    eye = jnp.eye(S5_GB, dtype=m.dtype)
    t = jnp.einsum("bgpc,gh->bgchp", m.reshape(S5_NBLK, S5_GB, SSM_STATE, SSM_GROUP), eye)
    return t.reshape(S5_NBLK, LANE, S5_LANES)


def _from_state_blockdiag(m):
    eye = jnp.eye(S5_GB, dtype=m.dtype)
    t = jnp.einsum("bgcp,gh->bgphc", m.reshape(S5_NBLK, S5_GB, SSM_GROUP, SSM_STATE), eye)
    return t.reshape(S5_NBLK, S5_LANES, LANE)


def _diag_of_state_major(acc):
    eye = jnp.eye(S5_GB, dtype=acc.dtype)
    t = acc.reshape(S5_NBLK, S5_GB, SSM_STATE, S5_GB, SSM_GROUP)
    return jnp.einsum("bgphc,gh->bgpc", t, eye).reshape(SSM_GROUPS, SSM_STATE, SSM_GROUP)


def _diag_of_channel_major(acc):
    eye = jnp.eye(S5_GB, dtype=acc.dtype)
    t = acc.reshape(S5_NBLK, S5_GB, SSM_GROUP, S5_GB, SSM_STATE)
    return jnp.einsum("bgchp,gh->bgcp", t, eye).reshape(SSM_GROUPS, SSM_GROUP, SSM_STATE)


SMALL_PARTIALS = (("ret_gn_g", 1024), ("lam_re", 4096), ("lam_im", 4096),
                  ("bbar_re", 65536), ("bbar_im", 65536), ("c_re", 65536), ("c_im", 65536),
                  ("ssm_d", 1024), ("b_glu", 1024), ("out_g", 1024), ("norm_ffn_g", 2048), ("norm_final_g", 2048))


def _forward_backward(x, tgt, shards, sm, tm=512):
    T = x.shape[0]
    cosf, sinf = _rope_tables(T)
    mask, rowdec, kdec, gtb = _retention_tables()
    lbr, lbi, bbr, bbi = _s5_discretise(sm["ssm_a_re"], sm["ssm_a_im"], sm["ssm_log_dt"], sm["ssm_b_re"],
                                        sm["ssm_b_im"])
    bre = _to_state_blockdiag(bbr).astype(BF16)
    bim = _to_state_blockdiag(bbi).astype(BF16)
    cre_t = _from_state_blockdiag(sm["ssm_c_re"]).astype(BF16)
    cim_t = _from_state_blockdiag(sm["ssm_c_im"]).astype(BF16)
    bre_t = jnp.swapaxes(bre, 1, 2)
    bim_t = jnp.swapaxes(bim, 1, 2)
    cre = jnp.swapaxes(cre_t, 1, 2)
    cim = jnp.swapaxes(cim_t, 1, 2)
    lam = jnp.stack([lbr.reshape(S5_NBLK, S5_LANES), lbi.reshape(S5_NBLK, S5_LANES)], axis=1)
    pm = _step_major_permutation()
    pm_t = pm.T
    row = lambda v: v.reshape(1, -1)
    g_mix, g_ffn, g_fin = row(sm["norm_mix_g"]), row(sm["norm_ffn_g"]), row(sm["norm_final_g"])
    gn, dsk, bglu, og = row(sm["ret_gn_g"]), row(sm["ssm_d"]), row(sm["ssm_b_glu"]), row(sm["ssm_out_g"])

    w_in = _gather_once_per_chip("weight_gather", shards["w_in"])
    proj, h1, r1, w_glu, w_out = _in_proj_fwd(
        x, g_mix, w_in, 256, _Exchange([shards["ssm_w_glu"], shards["w_out"]], True))
    w_glu = w_glu.reshape(SSM_WIDTH, SSM_WIDTH)
    w_out = w_out.reshape(D_MODEL, D_MODEL)
    y_ret, sblk, w_gate = _ret_fwd(proj, cosf, sinf, mask, rowdec, kdec, gtb, gn,
                                   _Exchange([shards["w_gate"]], True))
    y_s5, bound, w_up = _s5_fwd(proj, pm, pm_t, bre, bim, cre_t, cim_t, lam, dsk, _Exchange([shards["w_up"]], True))
    z, y_ssm, r_ssm = _glu_fwd(y_s5, w_glu, bglu, og, 256)
    x2, h2, r2 = _out_proj_fwd(x, y_ret, y_ssm, w_out, g_ffn, 256)
    a, b, f, w_down = _ffn_up(h2, w_gate, w_up, tm, _Exchange([shards["w_down"]], True))
    dx3, dx3b, loss8, dg_fin = _ffn_down_loss(f, w_down, x2, tgt, g_fin, 256)

    landed = {}
    da, db = _ffn_bwd_act(dx3b, w_down, a, b, tm)
    dw_down = _ffn_wgrad_down(f, dx3b, tm)
    dw_gate, dw_up, landed["w_down"] = _ffn_wgrad_up(h2, da, db, tm, _Exchange([dw_down], False))
    dh2, landed["w_gate"] = _ffn_bwd_in(da, db, w_gate, w_up, min(1024, T), _Exchange([dw_gate], False))
    dx2, dx2b, dg_ffn, dy_ret, dy_ssm = _out_proj_bwd(dh2, x2, r2, g_ffn, dx3, w_out, 256)
    dw_out = jnp.concatenate([_wgrad_rows("out_proj_wgrad_ret", y_ret, dx2b, tm),
                              _wgrad_rows("out_proj_wgrad_ssm", y_ssm, dx2b, tm)], axis=0)
    dy_s5, dw_glu, db_glu, dog = _glu_bwd(y_s5, z, r_ssm, dy_ssm, w_glu, og, 256)
    du, dbre, dbim, dcre, dcim, dlam, dd, landed["w_up"] = _s5_bwd(
        proj, dy_s5, bound, pm, pm_t, bre, bim, bre_t, bim_t, cre, cim, lam, dsk, _Exchange([dw_up], False))
    dq, dk, dv, dgate, dgn = _ret_bwd(proj, cosf, sinf, mask, rowdec, kdec, gtb, gn, sblk, dy_ret)
    dproj = jnp.concatenate([dq, dk, dv, dgate, du], axis=1)
    small = dict(ret_gn_g=dgn, lam_re=dlam[:, 0], lam_im=dlam[:, 1],
                 bbar_re=_diag_of_state_major(dbre), bbar_im=_diag_of_state_major(dbim),
                 c_re=_diag_of_channel_major(dcre), c_im=_diag_of_channel_major(dcim),
                 ssm_d=dd, b_glu=db_glu, out_g=dog, norm_ffn_g=dg_ffn, norm_final_g=dg_fin)
    packed = _pack([small[n] for n, _ in SMALL_PARTIALS])
    dw_in, landed["w_out"], landed["ssm_w_glu"], small_landed = _in_proj_wgrad(
        h1, dproj, tm, _Exchange([dw_out.reshape(N_DEV, D_MODEL // N_DEV, D_MODEL),
                                  dw_glu.astype(BF16).reshape(N_DEV, SSM_WIDTH // N_DEV, SSM_WIDTH), packed],
                                 [False, False, True]))
    grad_x, dg_mix, landed["w_in"] = _in_proj_bwd(dproj, w_in, x, r1, g_mix, dx2, 256, _Exchange([dw_in], False))
    (mix_landed,) = _exchange_call("mix_gain_grad_gather", [_pack([dg_mix])], True)
    summed = dict(zip([n for n, _ in SMALL_PARTIALS],
                      _unpack(_sum_partials("small_grad_sum", small_landed), [(sz,) for _, sz in SMALL_PARTIALS])))
    summed["norm_mix_g"] = _sum_partials("mix_gain_grad_sum", mix_landed).reshape(-1)
    return loss8[0, 0], grad_x, landed, summed


def _small_grads(summed, sm):
    _, vjp = jax.vjp(_s5_discretise, sm["ssm_a_re"], sm["ssm_a_im"], sm["ssm_log_dt"], sm["ssm_b_re"], sm["ssm_b_im"])
    gp = (SSM_GROUPS, SSM_STATE)
    da_re, da_im, dlog_dt, db_re, db_im = vjp((summed["lam_re"].reshape(gp), summed["lam_im"].reshape(gp),
                                               summed["bbar_re"].reshape(gp + (SSM_GROUP,)),
                                               summed["bbar_im"].reshape(gp + (SSM_GROUP,))))
    return dict(norm_mix_g=summed["norm_mix_g"], ret_gn_g=summed["ret_gn_g"], ssm_a_re=da_re, ssm_a_im=da_im,
                ssm_log_dt=dlog_dt, ssm_b_re=db_re, ssm_b_im=db_im,
                ssm_c_re=summed["c_re"].reshape(SSM_GROUPS, SSM_GROUP, SSM_STATE),
                ssm_c_im=summed["c_im"].reshape(SSM_GROUPS, SSM_GROUP, SSM_STATE),
                ssm_d=summed["ssm_d"], ssm_b_glu=summed["b_glu"], ssm_out_g=summed["out_g"],
                norm_ffn_g=summed["norm_ffn_g"], norm_final_g=summed["norm_final_g"])


def _adamw_math(w, g, m, v):
    m2 = ADAM_B1 * m + (1.0 - ADAM_B1) * g
    v2 = ADAM_B2 * v + (1.0 - ADAM_B2) * (g * g)
    delta = -ADAM_LR * ((m2 / ADAM_BC1) / (jnp.sqrt(v2 / ADAM_BC2) + ADAM_EPS) + ADAM_WD * w)
    return delta, m2, v2


def _adamw_shard(name, parts, w, m, v, tr):
    rows, cols = w.shape

    def body(p_ref, w_ref, m_ref, v_ref, g_ref, d_ref, m2_ref, v2_ref):
        g = p_ref[0].astype(F32)
        for s in range(1, N_DEV):
            g = g + p_ref[s].astype(F32)
        d, m2, v2 = _adamw_math(w_ref[...], g, m_ref[...], v_ref[...])
        g_ref[...] = g
        d_ref[...] = d
        m2_ref[...] = m2
        v2_ref[...] = v2

    blk = pl.BlockSpec((tr, cols), lambda i: (i, 0))
    oshape = jax.ShapeDtypeStruct((rows, cols), F32)
    return pl.pallas_call(
        body, name=name, grid=(rows // tr,),
        in_specs=[pl.BlockSpec((N_DEV, tr, cols), lambda i: (0, i, 0)), blk, blk, blk],
        out_specs=[blk, blk, blk, blk], out_shape=[oshape] * 4,
        compiler_params=_params(1),
    )(parts, w, m, v)


def _sum_partials(name, parts):
    rows = parts.shape[1]

    def body(p_ref, o_ref):
        g = p_ref[0]
        for s in range(1, N_DEV):
            g = g + p_ref[s]
        o_ref[...] = g

    return pl.pallas_call(
        body, name=name, grid=(1,),
        in_specs=[pl.BlockSpec((N_DEV, rows, LANE), lambda i: (0, 0, 0))],
        out_specs=pl.BlockSpec((rows, LANE), lambda i: (0, 0)),
        out_shape=jax.ShapeDtypeStruct((rows, LANE), F32),
        compiler_params=_params(1),
    )(parts)


def _adamw_small(ws, gs, ms, vs):
    n = len(ws)

    def body(*refs):
        for i in range(n):
            w_ref, g_ref, m_ref, v_ref = (refs[k * n + i] for k in range(4))
            d_ref, m2_ref, v2_ref = (refs[(4 + k) * n + i] for k in range(3))
            d, m2, v2 = _adamw_math(w_ref[...], g_ref[...], m_ref[...], v_ref[...])
            d_ref[...] = d
            m2_ref[...] = m2
            v2_ref[...] = v2

    vmem = pl.BlockSpec(memory_space=pltpu.VMEM)
    out = pl.pallas_call(
        body, name="adamw_small", in_specs=[vmem] * (4 * n), out_specs=[vmem] * (3 * n),
        out_shape=[jax.ShapeDtypeStruct(w.shape, F32) for w in ws] * 3,
        compiler_params=pltpu.CompilerParams(vmem_limit_bytes=VMEM_LIMIT),
    )(*ws, *gs, *ms, *vs)
    return out[:n], out[n:2 * n], out[2 * n:]


def _pack(arrays):
    cols = []
    for a in arrays:
        flat = a.reshape(-1).astype(F32)
        pad = (-flat.shape[0]) % LANE
        cols.append(jnp.pad(flat, (0, pad)) if pad else flat)
    return jnp.concatenate(cols).reshape(-1, LANE)


def _unpack(packed, shapes):
    flat = packed.reshape(-1)
    out, off = [], 0
    for shp in shapes:
        n = math.prod(shp)
        out.append(flat[off:off + n].reshape(shp))
        off += n + ((-n) % LANE)
    return out


WEIGHTS = ("norm_mix_g", "w_in", "ret_gn_g", "ssm_a_re", "ssm_a_im", "ssm_log_dt", "ssm_b_re", "ssm_b_im",
           "ssm_c_re", "ssm_c_im", "ssm_d", "ssm_w_glu", "ssm_b_glu", "ssm_out_g", "w_out", "norm_ffn_g", "w_gate",
           "w_up", "w_down", "norm_final_g")
BIG = ("w_in", "ssm_w_glu", "w_out", "w_gate", "w_up", "w_down")
SMALL = tuple(n for n in WEIGHTS if n not in BIG)
ADAM_ROWS = {"w_in": 256, "ssm_w_glu": 128, "w_out": 128, "w_gate": 256, "w_up": 256, "w_down": 176}


def kernel(x, norm_mix_g, w_in, ret_gn_g, ssm_a_re, ssm_a_im, ssm_log_dt, ssm_b_re, ssm_b_im, ssm_c_re, ssm_c_im, ssm_d, ssm_w_glu, ssm_b_glu, ssm_out_g, w_out, norm_ffn_g, w_gate, w_up, w_down, norm_final_g, loss_target, m_norm_mix_g, m_w_in, m_ret_gn_g, m_ssm_a_re, m_ssm_a_im, m_ssm_log_dt, m_ssm_b_re, m_ssm_b_im, m_ssm_c_re, m_ssm_c_im, m_ssm_d, m_ssm_w_glu, m_ssm_b_glu, m_ssm_out_g, m_w_out, m_norm_ffn_g, m_w_gate, m_w_up, m_w_down, m_norm_final_g, v_norm_mix_g, v_w_in, v_ret_gn_g, v_ssm_a_re, v_ssm_a_im, v_ssm_log_dt, v_ssm_b_re, v_ssm_b_im, v_ssm_c_re, v_ssm_c_im, v_ssm_d, v_ssm_w_glu, v_ssm_b_glu, v_ssm_out_g, v_w_out, v_norm_ffn_g, v_w_gate, v_w_up, v_w_down, v_norm_final_g):
    given = dict(locals())
    w = {n: given[n] for n in WEIGHTS}
    m = {n: given["m_" + n] for n in WEIGHTS}
    v = {n: given["v_" + n] for n in WEIGHTS}
    drop = lambda n, a: a if n == "norm_final_g" else a[0]
    w0 = {n: drop(n, w[n]) for n in WEIGHTS}
    m0 = {n: drop(n, m[n]) for n in WEIGHTS}
    v0 = {n: drop(n, v[n]) for n in WEIGHTS}

    sm = {n: w0[n] for n in SMALL}
    shards = {n: w0[n].astype(BF16) for n in BIG}
    loss_local, grad_x, landed, summed = _forward_backward(x[0], loss_target[0], shards, sm)
    loss = lax.psum(loss_local, MESH_AXES)
    gsmall = _small_grads(summed, sm)

    grads, delta, new_m, new_v = {}, {}, {}, {}
    for n in BIG:
        g, d, m2, v2 = _adamw_shard("adamw_" + n, landed[n], w0[n], m0[n], v0[n], ADAM_ROWS[n])
        grads[n], delta[n], new_m[n], new_v[n] = g, d, m2, v2
    as_given = lambda n, a: a.reshape(1, -1) if n == "norm_final_g" else a.reshape(w[n].shape)
    gs = [as_given(n, gsmall[n]) for n in SMALL]
    ds, m2s, v2s = _adamw_small([as_given(n, w[n]) for n in SMALL], gs, [as_given(n, m[n]) for n in SMALL],
                                [as_given(n, v[n]) for n in SMALL])
    for n, g, d, m2, v2 in zip(SMALL, gs, ds, m2s, v2s):
        grads[n], delta[n], new_m[n], new_v[n] = g, d, m2, v2

    lift = lambda n, a: a.reshape(w[n].shape)
    return (loss, grad_x[None], *[lift(n, grads[n]) for n in WEIGHTS], *[lift(n, delta[n]) for n in WEIGHTS],
            *[lift(n, new_m[n]) for n in WEIGHTS], *[lift(n, new_v[n]) for n in WEIGHTS])
```

```python
import functools
import math

import jax
import jax.numpy as jnp
from jax import lax
from jax.experimental import pallas as pl
from jax.experimental.pallas import tpu as pltpu

F32 = jnp.float32
BF16 = jnp.bfloat16

D_MODEL = 2048
RET_WIDTH = 1024
RET_HEADS = 8
HEAD_DIM = 128
CHUNK = 64
SSM_WIDTH = 1024
SSM_GROUP = 16
SSM_GROUPS = 64
SSM_STATE = 64
D_FF = 5632
IN_WIDTH = 5120
ROPE_BASE = 10000.0
EPS = 1e-6
N_DEV = 8
MESH_AXES = ("x", "y", "c")

WIN_BLK = IN_WIDTH // N_DEV
FF_BLK = D_FF // N_DEV
RET_BLOCK = 256
RET_HPS = 4
S5_TILE = 256
S5_CHUNKS = 8
S5_STEPS = S5_TILE // S5_CHUNKS
S5_GB = 8
S5_NBLK = SSM_GROUPS // S5_GB
S5_LANES = S5_GB * SSM_STATE
LANE = 128

ADAM_LR = 0.001
ADAM_B1 = 0.9
ADAM_B2 = 0.999
ADAM_EPS = 1e-08
ADAM_WD = 0.01
ADAM_STEP = 10
ADAM_BC1 = 1.0 - ADAM_B1 ** ADAM_STEP
ADAM_BC2 = 1.0 - ADAM_B2 ** ADAM_STEP

VMEM_LIMIT = 56 * 1024 * 1024

NT = (((1,), (1,)), ((), ()))
TN = (((0,), (0,)), ((), ()))


def _params(n_grid):
    return pltpu.CompilerParams(dimension_semantics=("arbitrary",) * n_grid, vmem_limit_bytes=VMEM_LIMIT)


def _dot(a, b):
    return jnp.dot(a, b, preferred_element_type=F32)


def _dot_nt(a, b):
    return lax.dot_general(a, b, NT, preferred_element_type=F32)


def _dot_tn(a, b):
    return lax.dot_general(a, b, TN, preferred_element_type=F32)


def _sigmoid(x):
    return 1.0 / (1.0 + jnp.exp(-x))


_GELU_C = math.sqrt(2.0 / math.pi)
_GELU_A = 0.044715


def _gelu(x):
    t = jnp.tanh(_GELU_C * (x + _GELU_A * x * x * x))
    return 0.5 * x * (1.0 + t)


def _gelu_and_grad(x):
    t = jnp.tanh(_GELU_C * (x + _GELU_A * x * x * x))
    g = 0.5 * (1.0 + t) + 0.5 * x * (1.0 - t * t) * _GELU_C * (1.0 + 3.0 * _GELU_A * x * x)
    return 0.5 * x * (1.0 + t), g


def _rms_bwd(dy, x, r, g):
    w = dy * g
    dx = r * w - x * (r * r * r) * jnp.mean(w * x, axis=-1, keepdims=True)
    return dx, dy * x * r


HBM_SPEC = pl.BlockSpec(memory_space=pltpu.HBM)
ANY_SPEC = pl.BlockSpec(memory_space=pl.ANY)


def _load_resident(src_hbm, dst_vmem, sem):
    cp = pltpu.make_async_copy(src_hbm, dst_vmem, sem)
    cp.start()
    cp.wait()


def _my_block():
    return 4 * lax.axis_index("x") + 2 * lax.axis_index("y") + lax.axis_index("c")


def _peer(k):
    px = lax.axis_index("x") ^ ((k >> 2) & 1)
    py = lax.axis_index("y") ^ ((k >> 1) & 1)
    pc = lax.axis_index("c") ^ (k & 1)
    return (px, py, pc), 4 * px + 2 * py + pc


class _Exchange:
    def __init__(self, payloads, gather):
        self.payloads = list(payloads)
        self.n = len(self.payloads)
        self.gather = [gather] * self.n if isinstance(gather, bool) else list(gather)

    def out_shape(self):
        return [jax.ShapeDtypeStruct(((N_DEV,) if g else ()) + p.shape, p.dtype)
                for p, g in zip(self.payloads, self.gather)]

    def scratch_shapes(self):
        return [pltpu.SemaphoreType.DMA((self.n, N_DEV - 1)), pltpu.SemaphoreType.DMA((self.n, N_DEV - 1)),
                pltpu.SemaphoreType.DMA((self.n,))]

    def _copies(self, ins, outs, sems, incoming):
        send_sems, recv_sems, local_sems = sems
        me = _my_block()
        src_of = lambda i, blk: ins[i] if self.gather[i] else ins[i].at[blk]
        local, remote = [], []
        for i in range(self.n):
            if not incoming:
                local.append(pltpu.make_async_copy(src_of(i, me), outs[i].at[me], local_sems.at[i]))
            for k in range(1, N_DEV):
                dev, blk = _peer(k)
                src, dst = (outs[i].at[blk], outs[i].at[blk]) if incoming else (src_of(i, blk), outs[i].at[me])
                remote.append(pltpu.make_async_remote_copy(
                    src_ref=src, dst_ref=dst, send_sem=send_sems.at[i, k - 1], recv_sem=recv_sems.at[i, k - 1],
                    device_id=dev, device_id_type=pl.DeviceIdType.MESH))
        return local, remote

    def start(self, ins, outs, sems):
        local, sends = self._copies(ins, outs, sems, False)
        for cp in local + sends:
            cp.start()

    def wait(self, ins, outs, sems):
        for cp in self._copies(ins, outs, sems, True)[1]:
            cp.wait_recv()
        local, sends = self._copies(ins, outs, sems, False)
        for cp in sends:
            cp.wait_send()
        for cp in local:
            cp.wait()


def _pcall(body, name, grid, in_specs, out_specs, out_shape, scratch_shapes, args, carry=None):
    n_in, n_out, n_scr = len(in_specs), len(out_specs), len(scratch_shapes)
    if carry is None:
        return pl.pallas_call(body, name=name, grid=grid, in_specs=in_specs, out_specs=out_specs, out_shape=out_shape,
                              scratch_shapes=scratch_shapes, compiler_params=_params(len(grid)))(*args)
    nx = carry.n

    def wrapped(*refs):
        cin, xin = refs[:n_in], refs[n_in:n_in + nx]
        cout, xout = refs[n_in + nx:n_in + nx + n_out], refs[n_in + nx + n_out:n_in + 2 * nx + n_out]
        rest = refs[n_in + 2 * nx + n_out:]
        cscr, sems = rest[:n_scr], rest[n_scr:]
        first = functools.reduce(jnp.logical_and, [pl.program_id(a) == 0 for a in range(len(grid))])
        last = functools.reduce(jnp.logical_and, [pl.program_id(a) == grid[a] - 1 for a in range(len(grid))])

        @pl.when(first)
        def _():
            carry.start(xin, xout, sems)

        body(*cin, *cout, *cscr)

        @pl.when(last)
        def _():
            carry.wait(xin, xout, sems)

    return pl.pallas_call(
        wrapped, name=name, grid=grid, in_specs=list(in_specs) + [HBM_SPEC] * nx,
        out_specs=list(out_specs) + [HBM_SPEC] * nx, out_shape=list(out_shape) + carry.out_shape(),
        scratch_shapes=list(scratch_shapes) + carry.scratch_shapes(), compiler_params=_params(len(grid)),
    )(*args, *carry.payloads)


def _exchange_call(name, payloads, gather):
    ex = _Exchange(payloads, gather)

    def body(*refs):
        ins, outs, sems = refs[:ex.n], refs[ex.n:2 * ex.n], refs[2 * ex.n:]
        ex.start(ins, outs, sems)
        ex.wait(ins, outs, sems)

    return pl.pallas_call(body, name=name, in_specs=[HBM_SPEC] * ex.n, out_specs=[HBM_SPEC] * ex.n,
                          out_shape=ex.out_shape(), scratch_shapes=ex.scratch_shapes())(*ex.payloads)


def _gather_once_per_chip(name, shard):
    def body(src, out, send_sems, recv_sems, local_sem):
        x, y, c = lax.axis_index("x"), lax.axis_index("y"), lax.axis_index("c")
        me, sibling = (x, y, c), (x, y, 1 - c)
        chips = [(1 - x, y), (x, 1 - y), (1 - x, 1 - y)]
        slot = lambda px, py, pc: out.at[4 * px + 2 * py + pc]

        def copy(k, block, to, from_src=False):
            return pltpu.make_async_remote_copy(
                src_ref=src if from_src else slot(*block), dst_ref=slot(*block), send_sem=send_sems.at[k],
                recv_sem=recv_sems.at[k], device_id=to, device_id_type=pl.DeviceIdType.MESH)

        mine = pltpu.make_async_copy(src, slot(*me), local_sem)
        mine.start()
        first = [copy(0, me, sibling, True)] + [copy(1 + j, me, (*chip, c), True) for j, chip in enumerate(chips)]
        for cp in first:
            cp.start()
        passed = [copy(4 + j, (*chip, c), sibling) for j, chip in enumerate(chips)]
        for j, chip in enumerate(chips):
            copy(1 + j, (*chip, c), me).wait_recv()
            passed[j].start()
        copy(0, sibling, me).wait_recv()
        for j, chip in enumerate(chips):
            copy(4 + j, (*chip, 1 - c), me).wait_recv()
        for cp in first + passed:
            cp.wait_send()
        mine.wait()

    return pl.pallas_call(
        body, name=name, in_specs=[HBM_SPEC], out_specs=HBM_SPEC,
        out_shape=jax.ShapeDtypeStruct((N_DEV,) + shard.shape, shard.dtype),
        scratch_shapes=[pltpu.SemaphoreType.DMA((N_DEV - 1,)), pltpu.SemaphoreType.DMA((N_DEV - 1,)),
                        pltpu.SemaphoreType.DMA],
    )(shard)


def _in_proj_fwd(x, g, w, tm, carry=None):
    T = x.shape[0]

    def body(x_ref, g_ref, w_hbm, proj_ref, h_ref, r_ref, w_ref, sem):
        @pl.when(pl.program_id(0) == 0)
        def _():
            _load_resident(w_hbm, w_ref, sem)

        for rows in _row_chunks(tm, 1):
            xf = x_ref[rows, :]
            r = lax.rsqrt(jnp.mean(xf * xf, axis=-1, keepdims=True) + EPS)
            h = (xf * r * g_ref[...]).astype(BF16)
            h_ref[rows, :] = h
            r_ref[rows, :] = r
            for j in range(N_DEV):
                proj_ref[rows, j * WIN_BLK:(j + 1) * WIN_BLK] = _dot(h, w_ref[j])

    return _pcall(
        body, "in_proj_fwd", (T // tm,),
        [pl.BlockSpec((tm, D_MODEL), lambda i: (i, 0)), pl.BlockSpec((1, D_MODEL), lambda i: (0, 0)), ANY_SPEC],
        [pl.BlockSpec((tm, IN_WIDTH), lambda i: (i, 0)),
         pl.BlockSpec((tm, D_MODEL), lambda i: (i, 0)),
         pl.BlockSpec((tm, 1), lambda i: (i, 0))],
        [jax.ShapeDtypeStruct((T, IN_WIDTH), F32),
         jax.ShapeDtypeStruct((T, D_MODEL), BF16),
         jax.ShapeDtypeStruct((T, 1), F32)],
        [pltpu.VMEM(w.shape, w.dtype), pltpu.SemaphoreType.DMA], (x, g, w), carry)


def _ret_common(q_ref, k_ref, v_ref, cos_ref, sin_ref, mask_ref, rd_ref, sin_state):
    c = cos_ref[...]
    s = sin_ref[...]
    q = q_ref[...]
    q = q * c + pltpu.roll(q, HEAD_DIM // 2, 1) * s
    k = k_ref[...]
    k = (k * c + pltpu.roll(k, HEAD_DIM // 2, 1) * s) * (HEAD_DIM ** -0.5)
    qb = q.astype(BF16)
    kb = k.astype(BF16)
    vb = v_ref[...].astype(BF16)
    pm = (_dot_nt(qb, kb) * mask_ref[...]).astype(BF16)
    qd = (q * rd_ref[...]).astype(BF16)
    o = _dot(pm, vb) + _dot(qd, sin_state.astype(BF16))
    return q, k, qb, kb, vb, pm, qd, o


def _ret_specs(T, rev):
    nb = T // RET_BLOCK
    groups = RET_HEADS // RET_HPS
    wide = RET_HPS * HEAD_DIM
    blk = (lambda b: nb - 1 - b) if rev else (lambda b: b)
    col = lambda piece: (pl.BlockSpec((RET_BLOCK, wide), lambda h, b: (blk(b), piece * groups + h)), "lane")
    return dict(
        q=col(0), k=col(1), v=col(2), g=col(3),
        tab=(pl.BlockSpec((RET_BLOCK, HEAD_DIM), lambda h, b: (blk(b), 0)), None),
        mask=(pl.BlockSpec((RET_HPS, RET_BLOCK, RET_BLOCK), lambda h, b: (h, 0, 0)), "lead"),
        dec=(pl.BlockSpec((RET_HPS, RET_BLOCK, HEAD_DIM), lambda h, b: (h, 0, 0)), "lead"),
        gtb=(pl.BlockSpec((RET_HPS, 1, HEAD_DIM), lambda h, b: (h, 0, 0)), "lead"),
        gn=(pl.BlockSpec((1, wide), lambda h, b: (0, h)), "lane"),
        state=(pl.BlockSpec((RET_HPS, None, HEAD_DIM, HEAD_DIM), lambda h, b: (h, blk(b), 0, 0)), "lead"),
        rows=(pl.BlockSpec((RET_BLOCK, wide), lambda h, b: (blk(b), h)), "lane"),
        scratch=(pltpu.VMEM((RET_HPS, HEAD_DIM, HEAD_DIM), F32), "lead"),
    )


def _per_head(head_body, kinds):
    def body(*refs):
        for hh in range(RET_HPS):
            views = []
            for ref, kind in zip(refs, kinds):
                if kind == "lane":
                    views.append(ref.at[:, hh * HEAD_DIM:(hh + 1) * HEAD_DIM])
                elif kind == "lead":
                    views.append(ref.at[hh])
                else:
                    views.append(ref)
            head_body(*views)
    return body


def _ret_fwd(proj, cosf, sinf, mask, rowdec, kdec, gtb, gn, carry=None):
    T = proj.shape[0]
    nb = T // RET_BLOCK
    sp = _ret_specs(T, False)

    def body(q_ref, k_ref, v_ref, g_ref, cos_ref, sin_ref, mask_ref, rd_ref, kd_ref, gtb_ref, gn_ref,
             y_ref, sb_ref, st):
        @pl.when(pl.program_id(1) == 0)
        def _():
            st[...] = jnp.zeros_like(st)
        s_in = st[...]
        sb_ref[...] = s_in
        q, k, qb, kb, vb, pm, qd, o = _ret_common(q_ref, k_ref, v_ref, cos_ref, sin_ref, mask_ref, rd_ref, s_in)
        st[...] = gtb_ref[...] * s_in + _dot_tn((k * kd_ref[...]).astype(BF16), vb)
        mu = jnp.mean(o, axis=-1, keepdims=True)
        oc = o - mu
        n = oc * lax.rsqrt(jnp.mean(oc * oc, axis=-1, keepdims=True) + EPS)
        gt = g_ref[...]
        y_ref[...] = (gt * _sigmoid(gt) * (n * gn_ref[...])).astype(BF16)

    ins = [sp[n] for n in ("q", "k", "v", "g", "tab", "tab", "mask", "dec", "dec", "gtb", "gn")]
    outs = [sp["rows"], sp["state"]]
    return _pcall(
        _per_head(body, [kind for _, kind in ins + outs + [sp["scratch"]]]), "ret_fwd", (RET_HEADS // RET_HPS, nb),
        [s for s, _ in ins], [s for s, _ in outs],
        [jax.ShapeDtypeStruct((T, RET_WIDTH), BF16),
         jax.ShapeDtypeStruct((RET_HEADS, nb, HEAD_DIM, HEAD_DIM), F32)],
        [sp["scratch"][0]],
        (proj, proj, proj, proj, cosf, sinf, mask, rowdec, kdec, gtb, gn), carry)


def _scan(re, im, ar, ai, reverse):
    n = re.shape[0]
    row = lax.broadcasted_iota(jnp.int32, re.shape, 0)
    s = 1
    while s < n:
        if reverse:
            keep = row < n - s
            sr = jnp.where(keep, pltpu.roll(re, n - s, 0), 0.0)
            si = jnp.where(keep, pltpu.roll(im, n - s, 0), 0.0)
        else:
            keep = row >= s
            sr = jnp.where(keep, pltpu.roll(re, s, 0), 0.0)
            si = jnp.where(keep, pltpu.roll(im, s, 0), 0.0)
        re, im = re + ar * sr - ai * si, im + ar * si + ai * sr
        ar, ai = ar * ar - ai * ai, 2.0 * ar * ai
        s *= 2
    return re, im


S5_STATE_TILE = (S5_TILE, S5_LANES)


def _step_major_permutation():
    r = jnp.arange(S5_TILE)
    t_of_row = (r % S5_CHUNKS) * S5_STEPS + r // S5_CHUNKS
    return (t_of_row[:, None] == r[None, :]).astype(BF16)


def _permute_rows_f32(pm, x):
    hi = x.astype(BF16)
    rest = x - hi.astype(F32)
    mid = rest.astype(BF16)
    lo = (rest - mid.astype(F32)).astype(BF16)
    return _dot(pm, hi) + _dot(pm, mid) + _dot(pm, lo)


def _step_get(ref, j):
    return ref[j * S5_CHUNKS:(j + 1) * S5_CHUNKS, :]


def _step_set(ref, j, val):
    ref[j * S5_CHUNKS:(j + 1) * S5_CHUNKS, :] = val


def _tile_get(ref):
    return ref[...]


def _tile_set(ref, val):
    ref[...] = val


def _fill_power_table(ptab, lr, li):
    shape = (S5_CHUNKS, S5_LANES)
    lrb = jnp.broadcast_to(lr, shape)
    lib = jnp.broadcast_to(li, shape)
    pr, pi_ = lrb, lib
    for j in range(S5_STEPS):
        ptab[0, j * S5_CHUNKS:(j + 1) * S5_CHUNKS, :] = pr
        ptab[1, j * S5_CHUNKS:(j + 1) * S5_CHUNKS, :] = pi_
        pr, pi_ = lrb * pr - lib * pi_, lrb * pi_ + lib * pr


def _chunk_scans(xr, xi, lr, li, reverse):
    shape = (S5_CHUNKS, S5_LANES)
    lrb = jnp.broadcast_to(lr, shape)
    lib = jnp.broadcast_to(li, shape)
    sr = si = None
    for j in (range(S5_STEPS - 1, -1, -1) if reverse else range(S5_STEPS)):
        vr = _step_get(xr, j)
        vi = _step_get(xi, j)
        if sr is not None:
            vr, vi = vr + lrb * sr - lib * si, vi + lrb * si + lib * sr
            _step_set(xr, j, vr)
            _step_set(xi, j, vi)
        sr, si = vr, vi
    return sr, si


def _entering_states(zr, zi, cr, ci, ar, ai, reverse):
    shape = (S5_CHUNKS, S5_LANES)
    row = lax.broadcasted_iota(jnp.int32, shape, 0)
    if reverse:
        edge, shift = row == S5_CHUNKS - 1, S5_CHUNKS - 1
    else:
        edge, shift = row == 0, 1
    wr = jnp.where(edge, jnp.broadcast_to(cr, shape), pltpu.roll(zr, shift, 0))
    wi = jnp.where(edge, jnp.broadcast_to(ci, shape), pltpu.roll(zi, shift, 0))
    return _scan(wr, wi, ar, ai, reverse)


def _table_rows(ptab, j, conj):
    pr = ptab[0, j * S5_CHUNKS:(j + 1) * S5_CHUNKS, :]
    pi_ = ptab[1, j * S5_CHUNKS:(j + 1) * S5_CHUNKS, :]
    return pr, (-pi_ if conj else pi_)


def _s5_forward_states(xr, xi, lr, li, cr, ci, ptab):
    zr, zi = _chunk_scans(xr, xi, lr, li, False)
    ar, ai = _table_rows(ptab, S5_STEPS - 1, False)
    er, ei = _entering_states(zr, zi, cr, ci, ar, ai, False)
    for j in range(S5_STEPS):
        pr, pi_ = _table_rows(ptab, j, False)
        _step_set(xr, j, _step_get(xr, j) + pr * er - pi_ * ei)
        _step_set(xi, j, _step_get(xi, j) + pr * ei + pi_ * er)
    last = S5_CHUNKS - 1
    end_r = (ar * er - ai * ei + zr)[last:last + 1, :]
    end_i = (ar * ei + ai * er + zi)[last:last + 1, :]
    return er, ei, end_r, end_i


def _s5_specs(T, rev):
    nt = T // S5_TILE
    tt = (lambda t: nt - 1 - t) if rev else (lambda t: t)
    return dict(
        u=pl.BlockSpec((S5_TILE, LANE), lambda b, t: (tt(t), 4 * RET_HEADS + b)),
        rows=pl.BlockSpec((S5_TILE, LANE), lambda b, t: (tt(t), b)),
        to_state=pl.BlockSpec((None, LANE, S5_LANES), lambda b, t: (b, 0, 0)),
        from_state=pl.BlockSpec((None, S5_LANES, LANE), lambda b, t: (b, 0, 0)),
        lam=pl.BlockSpec((None, 2, S5_LANES), lambda b, t: (b, 0, 0)),
        d=pl.BlockSpec((1, LANE), lambda b, t: (0, b)),
        perm=pl.BlockSpec((S5_TILE, S5_TILE), lambda b, t: (0, 0)),
        bound=pl.BlockSpec((None, None, 2, S5_LANES), lambda b, t: (b, tt(t), 0, 0)),
    )


def _s5_fwd(proj, pm, pm_t, bre, bim, cre_t, cim_t, lam, d, carry=None):
    T = proj.shape[0]
    nt = T // S5_TILE
    sp = _s5_specs(T, False)

    def body(u_ref, pm_ref, pmt_ref, bre_ref, bim_ref, cre_ref, cim_ref, lam_ref, d_ref, y_ref, bound_ref,
             carry, ptab, xr, xi):
        lr = lam_ref[0:1, :]
        li = lam_ref[1:2, :]

        @pl.when(pl.program_id(1) == 0)
        def _():
            carry[...] = jnp.zeros_like(carry)
            _fill_power_table(ptab, lr, li)

        u = _permute_rows_f32(pm_ref[...], u_ref[...])
        ub = u.astype(BF16)
        _tile_set(xr, _dot(ub, bre_ref[...]))
        _tile_set(xi, _dot(ub, bim_ref[...]))
        bound_ref[...] = carry[...]
        _, _, end_r, end_i = _s5_forward_states(xr, xi, lr, li, carry[0:1, :], carry[1:2, :], ptab)
        carry[0:1, :] = end_r
        carry[1:2, :] = end_i
        y = (_dot(_tile_get(xr).astype(BF16), cre_ref[...]) - _dot(_tile_get(xi).astype(BF16), cim_ref[...])
             + d_ref[...] * u)
        y_ref[...] = _permute_rows_f32(pmt_ref[...], y)

    state = pltpu.VMEM(S5_STATE_TILE, F32)
    return _pcall(
        body, "s5_fwd", (S5_NBLK, nt),
        [sp["u"], sp["perm"], sp["perm"], sp["to_state"], sp["to_state"], sp["from_state"],
         sp["from_state"], sp["lam"], sp["d"]],
        [sp["rows"], sp["bound"]],
        [jax.ShapeDtypeStruct((T, SSM_WIDTH), F32),
         jax.ShapeDtypeStruct((S5_NBLK, nt, 2, S5_LANES), F32)],
        [pltpu.VMEM((2, S5_LANES), F32), pltpu.VMEM((2, S5_TILE, S5_LANES), F32), state, state],
        (proj, pm, pm_t, bre, bim, cre_t, cim_t, lam, d), carry)


def _glu_fwd(y, w, b, og, tm):
    T = y.shape[0]

    def body(y_ref, w_ref, b_ref, og_ref, z_ref, o_ref, r_ref):
        y1 = _gelu(y_ref[...])
        z = _dot(y1.astype(BF16), w_ref[...]) + b_ref[...]
        y2 = y1 * _sigmoid(z)
        r = lax.rsqrt(jnp.mean(y2 * y2, axis=-1, keepdims=True) + EPS)
        z_ref[...] = z
        o_ref[...] = (y2 * r * og_ref[...]).astype(BF16)
        r_ref[...] = r

    row = pl.BlockSpec((tm, SSM_WIDTH), lambda i: (i, 0))
    vec = pl.BlockSpec((1, SSM_WIDTH), lambda i: (0, 0))
    return pl.pallas_call(
        body, name="glu_fwd", grid=(T // tm,),
        in_specs=[row, pl.BlockSpec((SSM_WIDTH, SSM_WIDTH), lambda i: (0, 0)), vec, vec],
        out_specs=[row, row, pl.BlockSpec((tm, 1), lambda i: (i, 0))],
        out_shape=[jax.ShapeDtypeStruct((T, SSM_WIDTH), F32), jax.ShapeDtypeStruct((T, SSM_WIDTH), BF16),
                   jax.ShapeDtypeStruct((T, 1), F32)],
        compiler_params=_params(1),
    )(y, w, b, og)


def _out_proj_fwd(x, y_ret, y_ssm, w, g, tm):
    T = x.shape[0]

    def body(x_ref, a_ref, b_ref, w_ref, g_ref, x2_ref, h_ref, r_ref):
        for rows in _row_chunks(tm, 1):
            x2 = (x_ref[rows, :] + _dot(a_ref[rows, :], w_ref[0:RET_WIDTH, :])
                  + _dot(b_ref[rows, :], w_ref[RET_WIDTH:D_MODEL, :]))
            r = lax.rsqrt(jnp.mean(x2 * x2, axis=-1, keepdims=True) + EPS)
            x2_ref[rows, :] = x2
            h_ref[rows, :] = (x2 * r * g_ref[...]).astype(BF16)
            r_ref[rows, :] = r

    full = pl.BlockSpec((tm, D_MODEL), lambda i: (i, 0))
    half = pl.BlockSpec((tm, RET_WIDTH), lambda i: (i, 0))
    return pl.pallas_call(
        body, name="out_proj_fwd", grid=(T // tm,),
        in_specs=[full, half, half, pl.BlockSpec((D_MODEL, D_MODEL), lambda i: (0, 0)),
                  pl.BlockSpec((1, D_MODEL), lambda i: (0, 0))],
        out_specs=[full, full, pl.BlockSpec((tm, 1), lambda i: (i, 0))],
        out_shape=[jax.ShapeDtypeStruct((T, D_MODEL), F32), jax.ShapeDtypeStruct((T, D_MODEL), BF16),
                   jax.ShapeDtypeStruct((T, 1), F32)],
        compiler_params=_params(1),
    )(x, y_ret, y_ssm, w, g)


def _ffn_up(h, wg, wu, tm, carry=None):
    T = h.shape[0]

    def body(h_ref, wg_ref, wu_ref, a_ref, b_ref, f_ref):
        for rows in _row_chunks(tm, 1):
            hb = h_ref[rows, :]
            a = _dot(hb, wg_ref[...])
            b = _dot(hb, wu_ref[...])
            a_ref[rows, :] = a.astype(BF16)
            b_ref[rows, :] = b.astype(BF16)
            f_ref[rows, :] = (a * _sigmoid(a) * b).astype(BF16)

    wspec = pl.BlockSpec((None, D_MODEL, FF_BLK), lambda j, i: (j, 0, 0))
    ospec = pl.BlockSpec((None, tm, FF_BLK), lambda j, i: (j, i, 0))
    oshape = jax.ShapeDtypeStruct((N_DEV, T, FF_BLK), BF16)
    return _pcall(
        body, "ffn_up", (N_DEV, T // tm),
        [pl.BlockSpec((tm, D_MODEL), lambda j, i: (i, 0)), wspec, wspec],
        [ospec, ospec, ospec], [oshape, oshape, oshape], [], (h, wg, wu), carry)


def _ffn_down_loss(f, wd, x2, tgt, g, tm):
    T = x2.shape[0]

    def body(f_ref, w_hbm, x2_ref, t_ref, g_ref, dx_ref, dxb_ref, loss_ref, dg_ref, w_ref, sem):
        i = pl.program_id(0)

        @pl.when(i == 0)
        def _():
            _load_resident(w_hbm, w_ref, sem)
            loss_ref[...] = jnp.zeros_like(loss_ref)
            dg_ref[...] = jnp.zeros_like(dg_ref)

        gv = g_ref[...]
        for rows in _row_chunks(tm, 1):
            x3 = x2_ref[rows, :]
            for k in range(N_DEV):
                x3 = x3 + _dot(f_ref[k, rows, :], w_ref[k])
            r = lax.rsqrt(jnp.mean(x3 * x3, axis=-1, keepdims=True) + EPS)
            err = x3 * r * gv - t_ref[rows, :]
            part_loss = 0.5 * jnp.sum(jnp.mean(err * err, axis=-1, keepdims=True), axis=0, keepdims=True)
            dx, dgt = _rms_bwd(err * (1.0 / D_MODEL), x3, r, gv)
            dx_ref[rows, :] = dx
            dxb_ref[rows, :] = dx.astype(BF16)
            loss_ref[...] += jnp.broadcast_to(part_loss, loss_ref.shape)
            dg_ref[...] += jnp.sum(dgt, axis=0, keepdims=True)

    full = pl.BlockSpec((tm, D_MODEL), lambda i: (i, 0))
    vec = pl.BlockSpec((1, D_MODEL), lambda i: (0, 0))
    return pl.pallas_call(
        body, name="ffn_down_loss", grid=(T // tm,),
        in_specs=[pl.BlockSpec((N_DEV, tm, FF_BLK), lambda i: (0, i, 0)), ANY_SPEC, full, full, vec],
        out_specs=[full, full, pl.BlockSpec((8, LANE), lambda i: (0, 0)), vec],
        out_shape=[jax.ShapeDtypeStruct((T, D_MODEL), F32), jax.ShapeDtypeStruct((T, D_MODEL), BF16),
                   jax.ShapeDtypeStruct((8, LANE), F32), jax.ShapeDtypeStruct((1, D_MODEL), F32)],
        scratch_shapes=[pltpu.VMEM(wd.shape, wd.dtype), pltpu.SemaphoreType.DMA],
        compiler_params=_params(1),
    )(f, wd, x2, tgt, g)


def _row_chunks(tm, n):
    return [slice(c * (tm // n), (c + 1) * (tm // n)) for c in range(n)]


def _ffn_bwd_act(dxb, wd, a, b, tm):
    T = dxb.shape[0]

    def body(dx_ref, w_ref, a_ref, b_ref, da_ref, db_ref):
        for rows in _row_chunks(tm, 1):
            df = _dot_nt(dx_ref[rows, :], w_ref[...])
            a = a_ref[rows, :].astype(F32)
            b = b_ref[rows, :].astype(F32)
            sg = _sigmoid(a)
            da_ref[rows, :] = (df * b * sg * (1.0 + a * (1.0 - sg))).astype(BF16)
            db_ref[rows, :] = (df * a * sg).astype(BF16)

    blk = pl.BlockSpec((None, tm, FF_BLK), lambda j, i: (j, i, 0))
    oshape = jax.ShapeDtypeStruct((N_DEV, T, FF_BLK), BF16)
    return pl.pallas_call(
        body, name="ffn_bwd_act", grid=(N_DEV, T // tm),
        in_specs=[pl.BlockSpec((tm, D_MODEL), lambda j, i: (i, 0)),
                  pl.BlockSpec((None, FF_BLK, D_MODEL), lambda j, i: (j, 0, 0)), blk, blk],
        out_specs=[blk, blk], out_shape=[oshape, oshape],
        compiler_params=_params(2),
    )(dxb, wd, a, b)


def _ffn_bwd_in(da, db, wg, wu, tm, carry=None):
    T = da.shape[1]

    def body(da_ref, db_ref, wg_ref, wu_ref, dh_ref):
        part = _dot_nt(da_ref[...], wg_ref[...]) + _dot_nt(db_ref[...], wu_ref[...])

        @pl.when(pl.program_id(1) == 0)
        def _():
            dh_ref[...] = part

        @pl.when(pl.program_id(1) > 0)
        def _():
            dh_ref[...] += part

    ablk = pl.BlockSpec((None, tm, FF_BLK), lambda i, k: (k, i, 0))
    wblk = pl.BlockSpec((None, D_MODEL, FF_BLK), lambda i, k: (k, 0, 0))
    return _pcall(
        body, "ffn_bwd_in", (T // tm, N_DEV), [ablk, ablk, wblk, wblk],
        [pl.BlockSpec((tm, D_MODEL), lambda i, k: (i, 0))], [jax.ShapeDtypeStruct((T, D_MODEL), F32)],
        [], (da, db, wg, wu), carry)


def _ffn_wgrad_up(h, da, db, tk, carry=None):
    T = h.shape[0]
    nk = T // tk

    def body(h_ref, da_ref, db_ref, g_ref, u_ref, accg, accu):
        k = pl.program_id(1)

        @pl.when(k == 0)
        def _():
            accg[...] = jnp.zeros_like(accg)
            accu[...] = jnp.zeros_like(accu)

        hb = h_ref[...]
        accg[...] += _dot_tn(hb, da_ref[...])
        accu[...] += _dot_tn(hb, db_ref[...])

        @pl.when(k == nk - 1)
        def _():
            g_ref[...] = accg[...].astype(BF16)
            u_ref[...] = accu[...].astype(BF16)

    blk = pl.BlockSpec((None, tk, FF_BLK), lambda j, k: (j, k, 0))
    ospec = pl.BlockSpec((None, D_MODEL, FF_BLK), lambda j, k: (j, 0, 0))
    oshape = jax.ShapeDtypeStruct((N_DEV, D_MODEL, FF_BLK), BF16)
    return _pcall(
        body, "ffn_wgrad_up", (N_DEV, nk),
        [pl.BlockSpec((tk, D_MODEL), lambda j, k: (k, 0)), blk, blk],
        [ospec, ospec], [oshape, oshape],
        [pltpu.VMEM((D_MODEL, FF_BLK), F32), pltpu.VMEM((D_MODEL, FF_BLK), F32)], (h, da, db), carry)


def _ffn_wgrad_down(f, dxb, tk):
    T = dxb.shape[0]
    nk = T // tk

    def body(f_ref, dx_ref, o_ref, acc):
        k = pl.program_id(1)

        @pl.when(k == 0)
        def _():
            acc[...] = jnp.zeros_like(acc)

        acc[...] += _dot_tn(f_ref[...], dx_ref[...])

        @pl.when(k == nk - 1)
        def _():
            o_ref[...] = acc[...].astype(BF16)

    return pl.pallas_call(
        body, name="ffn_wgrad_down", grid=(N_DEV, nk),
        in_specs=[pl.BlockSpec((None, tk, FF_BLK), lambda j, k: (j, k, 0)),
                  pl.BlockSpec((tk, D_MODEL), lambda j, k: (k, 0))],
        out_specs=pl.BlockSpec((None, FF_BLK, D_MODEL), lambda j, k: (j, 0, 0)),
        out_shape=jax.ShapeDtypeStruct((N_DEV, FF_BLK, D_MODEL), BF16),
        scratch_shapes=[pltpu.VMEM((FF_BLK, D_MODEL), F32)],
        compiler_params=_params(2),
    )(f, dxb)


def _out_proj_bwd(dh2, x2, r2, g, dx3, w, tm):
    T = x2.shape[0]

    def body(dh_ref, x_ref, r_ref, g_ref, dx3_ref, w_ref, dx_ref, dxb_ref, dg_ref, a_ref, b_ref):
        @pl.when(pl.program_id(0) == 0)
        def _():
            dg_ref[...] = jnp.zeros_like(dg_ref)

        for rows in _row_chunks(tm, 1):
            dxn, dgt = _rms_bwd(dh_ref[rows, :], x_ref[rows, :], r_ref[rows, :], g_ref[...])
            dx = dx3_ref[rows, :] + dxn
            dxv = dx.astype(BF16)
            dx_ref[rows, :] = dx
            dxb_ref[rows, :] = dxv
            dg_ref[...] += jnp.sum(dgt, axis=0, keepdims=True)
            a_ref[rows, :] = _dot_nt(dxv, w_ref[0:RET_WIDTH, :])
            b_ref[rows, :] = _dot_nt(dxv, w_ref[RET_WIDTH:D_MODEL, :])

    full = pl.BlockSpec((tm, D_MODEL), lambda i: (i, 0))
    vec = pl.BlockSpec((1, D_MODEL), lambda i: (0, 0))
    half = pl.BlockSpec((tm, RET_WIDTH), lambda i: (i, 0))
    hshape = jax.ShapeDtypeStruct((T, RET_WIDTH), F32)
    return pl.pallas_call(
        body, name="out_proj_bwd", grid=(T // tm,),
        in_specs=[full, full, pl.BlockSpec((tm, 1), lambda i: (i, 0)), vec, full,
                  pl.BlockSpec((D_MODEL, D_MODEL), lambda i: (0, 0))],
        out_specs=[full, full, vec, half, half],
        out_shape=[jax.ShapeDtypeStruct((T, D_MODEL), F32), jax.ShapeDtypeStruct((T, D_MODEL), BF16),
                   jax.ShapeDtypeStruct((1, D_MODEL), F32), hshape, hshape],
        compiler_params=_params(1),
    )(dh2, x2, r2, g, dx3, w)


def _wgrad_rows(name, a, b, tk):
    T, M = a.shape
    N = b.shape[1]
    nk = T // tk

    def body(a_ref, b_ref, o_ref, acc):
        k = pl.program_id(0)

        @pl.when(k == 0)
        def _():
            acc[...] = jnp.zeros_like(acc)

        acc[...] += _dot_tn(a_ref[...], b_ref[...])

        @pl.when(k == nk - 1)
        def _():
            o_ref[...] = acc[...].astype(BF16)

    return pl.pallas_call(
        body, name=name, grid=(nk,),
        in_specs=[pl.BlockSpec((tk, M), lambda k: (k, 0)), pl.BlockSpec((tk, N), lambda k: (k, 0))],
        out_specs=pl.BlockSpec((M, N), lambda k: (0, 0)),
        out_shape=jax.ShapeDtypeStruct((M, N), BF16),
        scratch_shapes=[pltpu.VMEM((M, N), F32)],
        compiler_params=_params(1),
    )(a, b)


def _glu_bwd(y, z, r, dyo, w, og, tm):
    T = y.shape[0]

    def body(y_ref, z_ref, r_ref, d_ref, w_ref, og_ref, dy_ref, dw_ref, db_ref, dog_ref):
        @pl.when(pl.program_id(0) == 0)
        def _():
            dw_ref[...] = jnp.zeros_like(dw_ref)
            db_ref[...] = jnp.zeros_like(db_ref)
            dog_ref[...] = jnp.zeros_like(dog_ref)

        y1, g1 = _gelu_and_grad(y_ref[...])
        sg = _sigmoid(z_ref[...])
        y2 = y1 * sg
        dy2, dogt = _rms_bwd(d_ref[...], y2, r_ref[...], og_ref[...])
        dog_ref[...] += jnp.sum(dogt, axis=0, keepdims=True)
        dz = dy2 * y1 * sg * (1.0 - sg)
        db_ref[...] += jnp.sum(dz, axis=0, keepdims=True)
        dzb = dz.astype(BF16)
        dw_ref[...] += _dot_tn(y1.astype(BF16), dzb)
        dy_ref[...] = (dy2 * sg + _dot_nt(dzb, w_ref[...])) * g1

    row = pl.BlockSpec((tm, SSM_WIDTH), lambda i: (i, 0))
    vec = pl.BlockSpec((1, SSM_WIDTH), lambda i: (0, 0))
    sq = pl.BlockSpec((SSM_WIDTH, SSM_WIDTH), lambda i: (0, 0))
    return pl.pallas_call(
        body, name="glu_bwd", grid=(T // tm,),
        in_specs=[row, row, pl.BlockSpec((tm, 1), lambda i: (i, 0)), row, sq, vec],
        out_specs=[row, sq, vec, vec],
        out_shape=[jax.ShapeDtypeStruct((T, SSM_WIDTH), F32), jax.ShapeDtypeStruct((SSM_WIDTH, SSM_WIDTH), F32),
                   jax.ShapeDtypeStruct((1, SSM_WIDTH), F32), jax.ShapeDtypeStruct((1, SSM_WIDTH), F32)],
        compiler_params=_params(1),
    )(y, z, r, dyo, w, og)


def _s5_bwd(proj, dy, bound, pm, pm_t, bre, bim, bre_t, bim_t, cre, cim, lam, d, carry=None):
    T = proj.shape[0]
    nt = T // S5_TILE
    sp = _s5_specs(T, True)

    def body(u_ref, dy_ref, bound_ref, pm_ref, pmt_ref, bre_ref, bim_ref, bret_ref, bimt_ref, cre_ref, cim_ref,
             lam_ref, d_ref,
             du_ref, dbre_ref, dbim_ref, dcre_ref, dcim_ref, dlam_ref, dd_ref, carry, ptab, sr, si, gr, gi):
        lr = lam_ref[0:1, :]
        li = lam_ref[1:2, :]

        @pl.when(pl.program_id(1) == 0)
        def _():
            carry[...] = jnp.zeros_like(carry)
            _fill_power_table(ptab, lr, li)
            for ref in (dbre_ref, dbim_ref, dcre_ref, dcim_ref, dlam_ref, dd_ref):
                ref[...] = jnp.zeros_like(ref)

        u = _permute_rows_f32(pm_ref[...], u_ref[...])
        ub = u.astype(BF16)
        dyv = _permute_rows_f32(pm_ref[...], dy_ref[...])
        dyb = dyv.astype(BF16)
        _tile_set(sr, _dot(ub, bre_ref[...]))
        _tile_set(si, _dot(ub, bim_ref[...]))
        er, ei, _, _ = _s5_forward_states(sr, si, lr, li, bound_ref[0:1, :], bound_ref[1:2, :], ptab)
        _tile_set(gr, _dot(dyb, cre_ref[...]))
        _tile_set(gi, -_dot(dyb, cim_ref[...]))
        zr, zi = _chunk_scans(gr, gi, lr, -li, True)
        ar, ai = _table_rows(ptab, S5_STEPS - 1, True)
        fr, fi = _entering_states(zr, zi, carry[0:1, :], carry[1:2, :], ar, ai, True)
        acc_r = jnp.zeros((S5_CHUNKS, S5_LANES), F32)
        acc_i = jnp.zeros((S5_CHUNKS, S5_LANES), F32)
        for j in range(S5_STEPS):
            qr, qi = _table_rows(ptab, S5_STEPS - 1 - j, True)
            g_r = _step_get(gr, j) + qr * fr - qi * fi
            g_i = _step_get(gi, j) + qr * fi + qi * fr
            _step_set(gr, j, g_r)
            _step_set(gi, j, g_i)
            p_r, p_i = (er, ei) if j == 0 else (_step_get(sr, j - 1), _step_get(si, j - 1))
            acc_r += g_r * p_r + g_i * p_i
            acc_i += g_i * p_r - g_r * p_i
        dlam_ref[0:1, :] += jnp.sum(acc_r, axis=0, keepdims=True)
        dlam_ref[1:2, :] += jnp.sum(acc_i, axis=0, keepdims=True)
        g_all_r = _tile_get(gr)
        g_all_i = _tile_get(gi)
        carry[0:1, :] = g_all_r[0:1, :]
        carry[1:2, :] = g_all_i[0:1, :]
        grb = g_all_r.astype(BF16)
        gib = g_all_i.astype(BF16)
        du = (_dot(grb, bret_ref[...]) + _dot(gib, bimt_ref[...]) + d_ref[...] * dyv).astype(BF16)
        du_ref[...] = _dot(pmt_ref[...], du).astype(BF16)
        dbre_ref[...] += _dot_tn(grb, ub)
        dbim_ref[...] += _dot_tn(gib, ub)
        dcre_ref[...] += _dot_tn(dyb, _tile_get(sr).astype(BF16))
        dcim_ref[...] -= _dot_tn(dyb, _tile_get(si).astype(BF16))
        dd_ref[...] += jnp.sum(dyv * u, axis=0, keepdims=True)

    acc_ts = pl.BlockSpec((None, S5_LANES, LANE), lambda b, t: (b, 0, 0))
    acc_fs = pl.BlockSpec((None, LANE, S5_LANES), lambda b, t: (b, 0, 0))
    return _pcall(
        body, "s5_bwd", (S5_NBLK, nt),
        [sp["u"], sp["rows"], sp["bound"], sp["perm"], sp["perm"], sp["to_state"], sp["to_state"],
         sp["from_state"], sp["from_state"], sp["to_state"], sp["to_state"], sp["lam"], sp["d"]],
        [sp["rows"], acc_ts, acc_ts, acc_fs, acc_fs, sp["lam"], sp["d"]],
        [jax.ShapeDtypeStruct((T, SSM_WIDTH), BF16),
         jax.ShapeDtypeStruct((S5_NBLK, S5_LANES, LANE), F32),
         jax.ShapeDtypeStruct((S5_NBLK, S5_LANES, LANE), F32),
         jax.ShapeDtypeStruct((S5_NBLK, LANE, S5_LANES), F32),
         jax.ShapeDtypeStruct((S5_NBLK, LANE, S5_LANES), F32),
         jax.ShapeDtypeStruct((S5_NBLK, 2, S5_LANES), F32),
         jax.ShapeDtypeStruct((1, SSM_WIDTH), F32)],
        [pltpu.VMEM((2, S5_LANES), F32), pltpu.VMEM((2, S5_TILE, S5_LANES), F32)]
        + [pltpu.VMEM(S5_STATE_TILE, F32)] * 4,
        (proj, dy, bound, pm, pm_t, bre, bim, bre_t, bim_t, cre, cim, lam, d), carry)


def _ret_bwd(proj, cosf, sinf, mask, rowdec, kdec, gtb, gn, sblk, dyr):
    T = proj.shape[0]
    nb = T // RET_BLOCK
    sp = _ret_specs(T, True)

    def body(q_ref, k_ref, v_ref, g_ref, cos_ref, sin_ref, mask_ref, rd_ref, kd_ref, gtb_ref, gn_ref, sb_ref, dy_ref,
             dq_ref, dk_ref, dv_ref, dg_ref, dgn_ref, dst):
        @pl.when(pl.program_id(1) == 0)
        def _():
            dst[...] = jnp.zeros_like(dst)
            dgn_ref[...] = jnp.zeros_like(dgn_ref)

        s_in = sb_ref[...]
        q, k, qb, kb, vb, pm, qd, o = _ret_common(q_ref, k_ref, v_ref, cos_ref, sin_ref, mask_ref, rd_ref, s_in)
        mu = jnp.mean(o, axis=-1, keepdims=True)
        oc = o - mu
        rstd = lax.rsqrt(jnp.mean(oc * oc, axis=-1, keepdims=True) + EPS)
        n = oc * rstd
        gt = g_ref[...]
        sg = _sigmoid(gt)
        sil = gt * sg
        gnv = gn_ref[...]
        dyv = dy_ref[...]
        dg_ref[...] = (dyv * (n * gnv) * (sg * (1.0 + gt * (1.0 - sg)))).astype(BF16)
        dgn_ref[...] += jnp.sum(dyv * sil * n, axis=0, keepdims=True)
        dn = dyv * sil * gnv
        do = rstd * (dn - jnp.mean(dn, axis=-1, keepdims=True) - n * jnp.mean(dn * n, axis=-1, keepdims=True))
        dob = do.astype(BF16)
        ds = dst[...]
        dsb = ds.astype(BF16)
        kd = kd_ref[...]
        rd = rd_ref[...]
        dv_ref[...] = (_dot_tn(pm, dob) + _dot((k * kd).astype(BF16), dsb)).astype(BF16)
        dpb = (_dot_nt(dob, vb) * mask_ref[...]).astype(BF16)
        dq = _dot(dpb, kb) + _dot_nt(dob, s_in.astype(BF16)) * rd
        dk = (_dot_tn(dpb, qb) + _dot_nt(vb, dsb) * kd) * (HEAD_DIM ** -0.5)
        dst[...] = gtb_ref[...] * ds + _dot_tn(qd, dob)
        c = cos_ref[...]
        s = sin_ref[...]
        dq_ref[...] = (dq * c + pltpu.roll(dq * s, HEAD_DIM // 2, 1)).astype(BF16)
        dk_ref[...] = (dk * c + pltpu.roll(dk * s, HEAD_DIM // 2, 1)).astype(BF16)

    oshape = jax.ShapeDtypeStruct((T, RET_WIDTH), BF16)
    ins = [sp[n] for n in ("q", "k", "v", "g", "tab", "tab", "mask", "dec", "dec", "gtb", "gn", "state", "rows")]
    outs = [sp["rows"], sp["rows"], sp["rows"], sp["rows"], sp["gn"]]
    return pl.pallas_call(
        _per_head(body, [kind for _, kind in ins + outs + [sp["scratch"]]]), name="ret_bwd",
        grid=(RET_HEADS // RET_HPS, nb), in_specs=[s for s, _ in ins], out_specs=[s for s, _ in outs],
        out_shape=[oshape, oshape, oshape, oshape, jax.ShapeDtypeStruct((1, RET_WIDTH), F32)],
        scratch_shapes=[sp["scratch"][0]],
        compiler_params=_params(2),
    )(proj, proj, proj, proj, cosf, sinf, mask, rowdec, kdec, gtb, gn, sblk, dyr)


def _in_proj_bwd(dproj, w, x, r1, g, dx2, tm, carry=None):
    T = x.shape[0]

    def body(dp_ref, w_hbm, x_ref, r_ref, g_ref, dx2_ref, gx_ref, dg_ref, w_ref, sem):
        @pl.when(pl.program_id(0) == 0)
        def _():
            _load_resident(w_hbm, w_ref, sem)
            dg_ref[...] = jnp.zeros_like(dg_ref)

        dh = _dot_nt(dp_ref[:, 0:WIN_BLK], w_ref[0])
        for k in range(1, N_DEV):
            dh = dh + _dot_nt(dp_ref[:, k * WIN_BLK:(k + 1) * WIN_BLK], w_ref[k])
        dxn, dgt = _rms_bwd(dh, x_ref[...], r_ref[...], g_ref[...])
        gx_ref[...] = dx2_ref[...] + dxn
        dg_ref[...] += jnp.sum(dgt, axis=0, keepdims=True)

    full = pl.BlockSpec((tm, D_MODEL), lambda i: (i, 0))
    vec = pl.BlockSpec((1, D_MODEL), lambda i: (0, 0))
    return _pcall(
        body, "in_proj_bwd", (T // tm,),
        [pl.BlockSpec((tm, IN_WIDTH), lambda i: (i, 0)), ANY_SPEC,
         full, pl.BlockSpec((tm, 1), lambda i: (i, 0)), vec, full],
        [full, vec],
        [jax.ShapeDtypeStruct((T, D_MODEL), F32), jax.ShapeDtypeStruct((1, D_MODEL), F32)],
        [pltpu.VMEM(w.shape, w.dtype), pltpu.SemaphoreType.DMA], (dproj, w, x, r1, g, dx2), carry)


def _in_proj_wgrad(h, dproj, tk, carry=None):
    T = h.shape[0]
    nk = T // tk

    def body(h_ref, dp_ref, o_ref, acc):
        k = pl.program_id(1)

        @pl.when(k == 0)
        def _():
            acc[...] = jnp.zeros_like(acc)

        acc[...] += _dot_tn(h_ref[...], dp_ref[...])

        @pl.when(k == nk - 1)
        def _():
            o_ref[...] = acc[...].astype(BF16)

    return _pcall(
        body, "in_proj_wgrad", (N_DEV, nk),
        [pl.BlockSpec((tk, D_MODEL), lambda j, k: (k, 0)), pl.BlockSpec((tk, WIN_BLK), lambda j, k: (k, j))],
        [pl.BlockSpec((None, D_MODEL, WIN_BLK), lambda j, k: (j, 0, 0))],
        [jax.ShapeDtypeStruct((N_DEV, D_MODEL, WIN_BLK), BF16)],
        [pltpu.VMEM((D_MODEL, WIN_BLK), F32)], (h, dproj), carry)


def _rope_tables(T):
    half = HEAD_DIM // 2
    freqs = ROPE_BASE ** (-jnp.arange(half, dtype=F32) / half)
    ang = jnp.arange(T, dtype=F32)[:, None] * freqs[None, :]
    c = jnp.cos(ang)
    s = jnp.sin(ang)
    return jnp.concatenate([c, c], axis=1), jnp.concatenate([-s, s], axis=1)


def _retention_tables():
    hh = jnp.arange(RET_HEADS, dtype=F32)
    log_g = jnp.log1p(-(2.0 ** (-5.0 - hh)))[:, None, None]
    i = jnp.arange(RET_BLOCK)
    ci = (i // CHUNK)[:, None]
    cj = (i // CHUNK)[None, :]
    diff = (i[:, None] - i[None, :]).astype(F32)
    expo = jnp.where(ci == cj, jnp.abs(diff), diff)
    mask = jnp.where((cj <= ci)[None], jnp.exp(log_g * expo[None]), 0.0)
    r = jnp.arange(RET_BLOCK, dtype=F32)[None, :, None]
    ones = jnp.ones((1, 1, HEAD_DIM), F32)
    rowdec = jnp.exp(log_g * (r + 1.0)) * ones
    kdec = jnp.exp(log_g * (RET_BLOCK - 1.0 - r)) * ones
    gtb = jnp.exp(log_g * float(RET_BLOCK)) * ones
    return mask, rowdec, kdec, gtb


def _s5_discretise(a_re, a_im, log_dt, b_re, b_im):
    lam = lax.complex(a_re, a_im)
    dt = jnp.exp(log_dt)[:, None]
    lam_bar = jnp.exp(lam * dt)
    b_bar = ((lam_bar - 1.0) / lam)[..., None] * lax.complex(b_re, b_im)
    return jnp.real(lam_bar), jnp.imag(lam_bar), jnp.real(b_bar), jnp.imag(b_bar)


def _to_state_blockdiag(m):
    eye = jnp.eye(S5_GB, dtype=m.dtype)
    t = jnp.einsum("bgpc,gh->bgchp", m.reshape(S5_NBLK, S5_GB, SSM_STATE, SSM_GROUP), eye)
    return t.reshape(S5_NBLK, LANE, S5_LANES)


def _from_state_blockdiag(m):
    eye = jnp.eye(S5_GB, dtype=m.dtype)
    t = jnp.einsum("bgcp,gh->bgphc", m.reshape(S5_NBLK, S5_GB, SSM_GROUP, SSM_STATE), eye)
    return t.reshape(S5_NBLK, S5_LANES, LANE)


def _diag_of_state_major(acc):
    eye = jnp.eye(S5_GB, dtype=acc.dtype)
    t = acc.reshape(S5_NBLK, S5_GB, SSM_STATE, S5_GB, SSM_GROUP)
    return jnp.einsum("bgphc,gh->bgpc", t, eye).reshape(SSM_GROUPS, SSM_STATE, SSM_GROUP)


def _diag_of_channel_major(acc):
    eye = jnp.eye(S5_GB, dtype=acc.dtype)
    t = acc.reshape(S5_NBLK, S5_GB, SSM_GROUP, S5_GB, SSM_STATE)
    return jnp.einsum("bgchp,gh->bgcp", t, eye).reshape(SSM_GROUPS, SSM_GROUP, SSM_STATE)


SMALL_PARTIALS = (("ret_gn_g", 1024), ("lam_re", 4096), ("lam_im", 4096),
                  ("bbar_re", 65536), ("bbar_im", 65536), ("c_re", 65536), ("c_im", 65536),
                  ("ssm_d", 1024), ("b_glu", 1024), ("out_g", 1024), ("norm_ffn_g", 2048), ("norm_final_g", 2048))


def _forward_backward(x, tgt, shards, sm):
    T = x.shape[0]
    tm = min(1024, T)
    cosf, sinf = _rope_tables(T)
    mask, rowdec, kdec, gtb = _retention_tables()
    lbr, lbi, bbr, bbi = _s5_discretise(sm["ssm_a_re"], sm["ssm_a_im"], sm["ssm_log_dt"], sm["ssm_b_re"],
                                        sm["ssm_b_im"])
    bre = _to_state_blockdiag(bbr).astype(BF16)
    bim = _to_state_blockdiag(bbi).astype(BF16)
    cre_t = _from_state_blockdiag(sm["ssm_c_re"]).astype(BF16)
    cim_t = _from_state_blockdiag(sm["ssm_c_im"]).astype(BF16)
    bre_t = jnp.swapaxes(bre, 1, 2)
    bim_t = jnp.swapaxes(bim, 1, 2)
    cre = jnp.swapaxes(cre_t, 1, 2)
    cim = jnp.swapaxes(cim_t, 1, 2)
    lam = jnp.stack([lbr.reshape(S5_NBLK, S5_LANES), lbi.reshape(S5_NBLK, S5_LANES)], axis=1)
    pm = _step_major_permutation()
    pm_t = pm.T
    row = lambda v: v.reshape(1, -1)
    g_mix, g_ffn, g_fin = row(sm["norm_mix_g"]), row(sm["norm_ffn_g"]), row(sm["norm_final_g"])
    gn, dsk, bglu, og = row(sm["ret_gn_g"]), row(sm["ssm_d"]), row(sm["ssm_b_glu"]), row(sm["ssm_out_g"])

    w_in = _gather_once_per_chip("weight_gather", shards["w_in"])
    proj, h1, r1, w_glu, w_out = _in_proj_fwd(
        x, g_mix, w_in, 256, _Exchange([shards["ssm_w_glu"], shards["w_out"]], True))
    w_glu = w_glu.reshape(SSM_WIDTH, SSM_WIDTH)
    w_out = w_out.reshape(D_MODEL, D_MODEL)
    y_ret, sblk, w_gate = _ret_fwd(proj, cosf, sinf, mask, rowdec, kdec, gtb, gn,
                                   _Exchange([shards["w_gate"]], True))
    y_s5, bound, w_up = _s5_fwd(proj, pm, pm_t, bre, bim, cre_t, cim_t, lam, dsk, _Exchange([shards["w_up"]], True))
    z, y_ssm, r_ssm = _glu_fwd(y_s5, w_glu, bglu, og, 256)
    x2, h2, r2 = _out_proj_fwd(x, y_ret, y_ssm, w_out, g_ffn, 256)
    a, b, f, w_down = _ffn_up(h2, w_gate, w_up, tm, _Exchange([shards["w_down"]], True))
    dx3, dx3b, loss8, dg_fin = _ffn_down_loss(f, w_down, x2, tgt, g_fin, 256)

    landed = {}
    da, db = _ffn_bwd_act(dx3b, w_down, a, b, tm)
    dw_down = _ffn_wgrad_down(f, dx3b, tm)
    dw_gate, dw_up, landed["w_down"] = _ffn_wgrad_up(h2, da, db, tm, _Exchange([dw_down], False))
    dh2, landed["w_gate"] = _ffn_bwd_in(da, db, w_gate, w_up, min(1024, T), _Exchange([dw_gate], False))
    dx2, dx2b, dg_ffn, dy_ret, dy_ssm = _out_proj_bwd(dh2, x2, r2, g_ffn, dx3, w_out, 256)
    dw_out = jnp.concatenate([_wgrad_rows("out_proj_wgrad_ret", y_ret, dx2b, tm),
                              _wgrad_rows("out_proj_wgrad_ssm", y_ssm, dx2b, tm)], axis=0)
    dy_s5, dw_glu, db_glu, dog = _glu_bwd(y_s5, z, r_ssm, dy_ssm, w_glu, og, 256)
    du, dbre, dbim, dcre, dcim, dlam, dd, landed["w_up"] = _s5_bwd(
        proj, dy_s5, bound, pm, pm_t, bre, bim, bre_t, bim_t, cre, cim, lam, dsk, _Exchange([dw_up], False))
    dq, dk, dv, dgate, dgn = _ret_bwd(proj, cosf, sinf, mask, rowdec, kdec, gtb, gn, sblk, dy_ret)
    dproj = jnp.concatenate([dq, dk, dv, dgate, du], axis=1)
    small = dict(ret_gn_g=dgn, lam_re=dlam[:, 0], lam_im=dlam[:, 1],
                 bbar_re=_diag_of_state_major(dbre), bbar_im=_diag_of_state_major(dbim),
                 c_re=_diag_of_channel_major(dcre), c_im=_diag_of_channel_major(dcim),
                 ssm_d=dd, b_glu=db_glu, out_g=dog, norm_ffn_g=dg_ffn, norm_final_g=dg_fin)
    packed = _pack([small[n] for n, _ in SMALL_PARTIALS])
    dw_in, landed["w_out"], landed["ssm_w_glu"], small_landed = _in_proj_wgrad(
        h1, dproj, tm, _Exchange([dw_out.reshape(N_DEV, D_MODEL // N_DEV, D_MODEL),
                                  dw_glu.astype(BF16).reshape(N_DEV, SSM_WIDTH // N_DEV, SSM_WIDTH), packed],
                                 [False, False, True]))
    grad_x, dg_mix, landed["w_in"] = _in_proj_bwd(dproj, w_in, x, r1, g_mix, dx2, 256, _Exchange([dw_in], False))
    (mix_landed,) = _exchange_call("mix_gain_grad_gather", [_pack([dg_mix])], True)
    summed = dict(zip([n for n, _ in SMALL_PARTIALS],
                      _unpack(_sum_partials("small_grad_sum", small_landed), [(sz,) for _, sz in SMALL_PARTIALS])))
    summed["norm_mix_g"] = _sum_partials("mix_gain_grad_sum", mix_landed).reshape(-1)
    return loss8[0, 0], grad_x, landed, summed


def _small_grads(summed, sm):
    _, vjp = jax.vjp(_s5_discretise, sm["ssm_a_re"], sm["ssm_a_im"], sm["ssm_log_dt"], sm["ssm_b_re"], sm["ssm_b_im"])
    gp = (SSM_GROUPS, SSM_STATE)
    da_re, da_im, dlog_dt, db_re, db_im = vjp((summed["lam_re"].reshape(gp), summed["lam_im"].reshape(gp),
                                               summed["bbar_re"].reshape(gp + (SSM_GROUP,)),
                                               summed["bbar_im"].reshape(gp + (SSM_GROUP,))))
    return dict(norm_mix_g=summed["norm_mix_g"], ret_gn_g=summed["ret_gn_g"], ssm_a_re=da_re, ssm_a_im=da_im,
                ssm_log_dt=dlog_dt, ssm_b_re=db_re, ssm_b_im=db_im,
                ssm_c_re=summed["c_re"].reshape(SSM_GROUPS, SSM_GROUP, SSM_STATE),
                ssm_c_im=summed["c_im"].reshape(SSM_GROUPS, SSM_GROUP, SSM_STATE),
                ssm_d=summed["ssm_d"], ssm_b_glu=summed["b_glu"], ssm_out_g=summed["out_g"],
                norm_ffn_g=summed["norm_ffn_g"], norm_final_g=summed["norm_final_g"])


def _adamw_math(w, g, m, v):
    m2 = ADAM_B1 * m + (1.0 - ADAM_B1) * g
    v2 = ADAM_B2 * v + (1.0 - ADAM_B2) * (g * g)
    delta = -ADAM_LR * ((m2 / ADAM_BC1) / (jnp.sqrt(v2 / ADAM_BC2) + ADAM_EPS) + ADAM_WD * w)
    return delta, m2, v2


def _adamw_shard(name, parts, w, m, v, tr):
    rows, cols = w.shape

    def body(p_ref, w_ref, m_ref, v_ref, g_ref, d_ref, m2_ref, v2_ref):
        g = p_ref[0].astype(F32)
        for s in range(1, N_DEV):
            g = g + p_ref[s].astype(F32)
        d, m2, v2 = _adamw_math(w_ref[...], g, m_ref[...], v_ref[...])
        g_ref[...] = g
        d_ref[...] = d
        m2_ref[...] = m2
        v2_ref[...] = v2

    blk = pl.BlockSpec((tr, cols), lambda i: (i, 0))
    oshape = jax.ShapeDtypeStruct((rows, cols), F32)
    return pl.pallas_call(
        body, name=name, grid=(rows // tr,),
        in_specs=[pl.BlockSpec((N_DEV, tr, cols), lambda i: (0, i, 0)), blk, blk, blk],
        out_specs=[blk, blk, blk, blk], out_shape=[oshape] * 4,
        compiler_params=_params(1),
    )(parts, w, m, v)


def _sum_partials(name, parts):
    rows = parts.shape[1]

    def body(p_ref, o_ref):
        g = p_ref[0]
        for s in range(1, N_DEV):
            g = g + p_ref[s]
        o_ref[...] = g

    return pl.pallas_call(
        body, name=name, grid=(1,),
        in_specs=[pl.BlockSpec((N_DEV, rows, LANE), lambda i: (0, 0, 0))],
        out_specs=pl.BlockSpec((rows, LANE), lambda i: (0, 0)),
        out_shape=jax.ShapeDtypeStruct((rows, LANE), F32),
        compiler_params=_params(1),
    )(parts)


def _adamw_small(ws, gs, ms, vs):
    n = len(ws)

    def body(*refs):
        for i in range(n):
            w_ref, g_ref, m_ref, v_ref = (refs[k * n + i] for k in range(4))
            d_ref, m2_ref, v2_ref = (refs[(4 + k) * n + i] for k in range(3))
            d, m2, v2 = _adamw_math(w_ref[...], g_ref[...], m_ref[...], v_ref[...])
            d_ref[...] = d
            m2_ref[...] = m2
            v2_ref[...] = v2

    vmem = pl.BlockSpec(memory_space=pltpu.VMEM)
    out = pl.pallas_call(
        body, name="adamw_small", in_specs=[vmem] * (4 * n), out_specs=[vmem] * (3 * n),
        out_shape=[jax.ShapeDtypeStruct(w.shape, F32) for w in ws] * 3,
        compiler_params=pltpu.CompilerParams(vmem_limit_bytes=VMEM_LIMIT),
    )(*ws, *gs, *ms, *vs)
    return out[:n], out[n:2 * n], out[2 * n:]


def _pack(arrays):
    cols = []
    for a in arrays:
        flat = a.reshape(-1).astype(F32)
        pad = (-flat.shape[0]) % LANE
        cols.append(jnp.pad(flat, (0, pad)) if pad else flat)
    return jnp.concatenate(cols).reshape(-1, LANE)


def _unpack(packed, shapes):
    flat = packed.reshape(-1)
    out, off = [], 0
    for shp in shapes:
        n = math.prod(shp)
        out.append(flat[off:off + n].reshape(shp))
        off += n + ((-n) % LANE)
    return out


WEIGHTS = ("norm_mix_g", "w_in", "ret_gn_g", "ssm_a_re", "ssm_a_im", "ssm_log_dt", "ssm_b_re", "ssm_b_im",
           "ssm_c_re", "ssm_c_im", "ssm_d", "ssm_w_glu", "ssm_b_glu", "ssm_out_g", "w_out", "norm_ffn_g", "w_gate",
           "w_up", "w_down", "norm_final_g")
BIG = ("w_in", "ssm_w_glu", "w_out", "w_gate", "w_up", "w_down")
SMALL = tuple(n for n in WEIGHTS if n not in BIG)
ADAM_ROWS = {"w_in": 256, "ssm_w_glu": 128, "w_out": 128, "w_gate": 256, "w_up": 256, "w_down": 176}


def kernel(x, norm_mix_g, w_in, ret_gn_g, ssm_a_re, ssm_a_im, ssm_log_dt, ssm_b_re, ssm_b_im, ssm_c_re, ssm_c_im, ssm_d, ssm_w_glu, ssm_b_glu, ssm_out_g, w_out, norm_ffn_g, w_gate, w_up, w_down, norm_final_g, loss_target, m_norm_mix_g, m_w_in, m_ret_gn_g, m_ssm_a_re, m_ssm_a_im, m_ssm_log_dt, m_ssm_b_re, m_ssm_b_im, m_ssm_c_re, m_ssm_c_im, m_ssm_d, m_ssm_w_glu, m_ssm_b_glu, m_ssm_out_g, m_w_out, m_norm_ffn_g, m_w_gate, m_w_up, m_w_down, m_norm_final_g, v_norm_mix_g, v_w_in, v_ret_gn_g, v_ssm_a_re, v_ssm_a_im, v_ssm_log_dt, v_ssm_b_re, v_ssm_b_im, v_ssm_c_re, v_ssm_c_im, v_ssm_d, v_ssm_w_glu, v_ssm_b_glu, v_ssm_out_g, v_w_out, v_norm_ffn_g, v_w_gate, v_w_up, v_w_down, v_norm_final_g):
    given = dict(locals())
    w = {n: given[n] for n in WEIGHTS}
    m = {n: given["m_" + n] for n in WEIGHTS}
    v = {n: given["v_" + n] for n in WEIGHTS}
    drop = lambda n, a: a if n == "norm_final_g" else a[0]
    w0 = {n: drop(n, w[n]) for n in WEIGHTS}
    m0 = {n: drop(n, m[n]) for n in WEIGHTS}
    v0 = {n: drop(n, v[n]) for n in WEIGHTS}

    sm = {n: w0[n] for n in SMALL}
    shards = {n: w0[n].astype(BF16) for n in BIG}
    loss_local, grad_x, landed, summed = _forward_backward(x[0], loss_target[0], shards, sm)
    loss = lax.psum(loss_local, MESH_AXES)
    gsmall = _small_grads(summed, sm)

    grads, delta, new_m, new_v = {}, {}, {}, {}
    for n in BIG:
        g, d, m2, v2 = _adamw_shard("adamw_" + n, landed[n], w0[n], m0[n], v0[n], ADAM_ROWS[n])
        grads[n], delta[n], new_m[n], new_v[n] = g, d, m2, v2
    as_given = lambda n, a: a.reshape(1, -1) if n == "norm_final_g" else a.reshape(w[n].shape)
    gs = [as_given(n, gsmall[n]) for n in SMALL]
    ds, m2s, v2s = _adamw_small([as_given(n, w[n]) for n in SMALL], gs, [as_given(n, m[n]) for n in SMALL],
                                [as_given(n, v[n]) for n in SMALL])
    for n, g, d, m2, v2 in zip(SMALL, gs, ds, m2s, v2s):
        grads[n], delta[n], new_m[n], new_v[n] = g, d, m2, v2

    lift = lambda n, a: a.reshape(w[n].shape)
    return (loss, grad_x[None], *[lift(n, grads[n]) for n in WEIGHTS], *[lift(n, delta[n]) for n in WEIGHTS],
            *[lift(n, new_m[n]) for n in WEIGHTS], *[lift(n, new_v[n]) for n in WEIGHTS])
```

```python
import functools
import math

import jax
import jax.numpy as jnp
from jax import lax
from jax.experimental import pallas as pl
from jax.experimental.pallas import tpu as pltpu

F32 = jnp.float32
BF16 = jnp.bfloat16

D_MODEL = 2048
RET_WIDTH = 1024
RET_HEADS = 8
HEAD_DIM = 128
CHUNK = 64
SSM_WIDTH = 1024
SSM_GROUP = 16
SSM_GROUPS = 64
SSM_STATE = 64
D_FF = 5632
IN_WIDTH = 5120
ROPE_BASE = 10000.0
EPS = 1e-6
N_DEV = 8
MESH_AXES = ("x", "y", "c")

WIN_BLK = IN_WIDTH // N_DEV
FF_BLK = D_FF // N_DEV
RET_BLOCK = 256
RET_HPS = 4
S5_TILE = 256
S5_CHUNKS = 8
S5_STEPS = S5_TILE // S5_CHUNKS
S5_GB = 8
S5_NBLK = SSM_GROUPS // S5_GB
S5_LANES = S5_GB * SSM_STATE
LANE = 128

ADAM_LR = 0.001
ADAM_B1 = 0.9
ADAM_B2 = 0.999
ADAM_EPS = 1e-08
ADAM_WD = 0.01
ADAM_STEP = 10
ADAM_BC1 = 1.0 - ADAM_B1 ** ADAM_STEP
ADAM_BC2 = 1.0 - ADAM_B2 ** ADAM_STEP

VMEM_LIMIT = 56 * 1024 * 1024

NT = (((1,), (1,)), ((), ()))
TN = (((0,), (0,)), ((), ()))


def _params(n_grid):
    return pltpu.CompilerParams(dimension_semantics=("arbitrary",) * n_grid, vmem_limit_bytes=VMEM_LIMIT)


def _dot(a, b):
    return jnp.dot(a, b, preferred_element_type=F32)


def _dot_nt(a, b):
    return lax.dot_general(a, b, NT, preferred_element_type=F32)


def _dot_tn(a, b):
    return lax.dot_general(a, b, TN, preferred_element_type=F32)


def _sigmoid(x):
    return 1.0 / (1.0 + jnp.exp(-x))


_GELU_C = math.sqrt(2.0 / math.pi)
_GELU_A = 0.044715


def _gelu(x):
    t = jnp.tanh(_GELU_C * (x + _GELU_A * x * x * x))
    return 0.5 * x * (1.0 + t)


def _gelu_and_grad(x):
    t = jnp.tanh(_GELU_C * (x + _GELU_A * x * x * x))
    g = 0.5 * (1.0 + t) + 0.5 * x * (1.0 - t * t) * _GELU_C * (1.0 + 3.0 * _GELU_A * x * x)
    return 0.5 * x * (1.0 + t), g


def _rms_bwd(dy, x, r, g):
    w = dy * g
    dx = r * w - x * (r * r * r) * jnp.mean(w * x, axis=-1, keepdims=True)
    return dx, dy * x * r


HBM_SPEC = pl.BlockSpec(memory_space=pltpu.HBM)
ANY_SPEC = pl.BlockSpec(memory_space=pl.ANY)


def _load_resident(src_hbm, dst_vmem, sem):
    cp = pltpu.make_async_copy(src_hbm, dst_vmem, sem)
    cp.start()
    cp.wait()


def _my_block():
    return 4 * lax.axis_index("x") + 2 * lax.axis_index("y") + lax.axis_index("c")


def _peer(k):
    px = lax.axis_index("x") ^ ((k >> 2) & 1)
    py = lax.axis_index("y") ^ ((k >> 1) & 1)
    pc = lax.axis_index("c") ^ (k & 1)
    return (px, py, pc), 4 * px + 2 * py + pc


class _Exchange:
    def __init__(self, payloads, gather):
        self.payloads = list(payloads)
        self.n = len(self.payloads)
        self.gather = [gather] * self.n if isinstance(gather, bool) else list(gather)

    def out_shape(self):
        return [jax.ShapeDtypeStruct(((N_DEV,) if g else ()) + p.shape, p.dtype)
                for p, g in zip(self.payloads, self.gather)]

    def scratch_shapes(self):
        return [pltpu.SemaphoreType.DMA((self.n, N_DEV - 1)), pltpu.SemaphoreType.DMA((self.n, N_DEV - 1)),
                pltpu.SemaphoreType.DMA((self.n,))]

    def _copies(self, ins, outs, sems, incoming):
        send_sems, recv_sems, local_sems = sems
        me = _my_block()
        src_of = lambda i, blk: ins[i] if self.gather[i] else ins[i].at[blk]
        local, remote = [], []
        for i in range(self.n):
            if not incoming:
                local.append(pltpu.make_async_copy(src_of(i, me), outs[i].at[me], local_sems.at[i]))
            for k in range(1, N_DEV):
                dev, blk = _peer(k)
                src, dst = (outs[i].at[blk], outs[i].at[blk]) if incoming else (src_of(i, blk), outs[i].at[me])
                remote.append(pltpu.make_async_remote_copy(
                    src_ref=src, dst_ref=dst, send_sem=send_sems.at[i, k - 1], recv_sem=recv_sems.at[i, k - 1],
                    device_id=dev, device_id_type=pl.DeviceIdType.MESH))
        return local, remote

    def start(self, ins, outs, sems):
        local, sends = self._copies(ins, outs, sems, False)
        for cp in local + sends:
            cp.start()

    def wait(self, ins, outs, sems):
        for cp in self._copies(ins, outs, sems, True)[1]:
            cp.wait_recv()
        local, sends = self._copies(ins, outs, sems, False)
        for cp in sends:
            cp.wait_send()
        for cp in local:
            cp.wait()


def _pcall(body, name, grid, in_specs, out_specs, out_shape, scratch_shapes, args, carry=None):
    n_in, n_out, n_scr = len(in_specs), len(out_specs), len(scratch_shapes)
    if carry is None:
        return pl.pallas_call(body, name=name, grid=grid, in_specs=in_specs, out_specs=out_specs, out_shape=out_shape,
                              scratch_shapes=scratch_shapes, compiler_params=_params(len(grid)))(*args)
    nx = carry.n

    def wrapped(*refs):
        cin, xin = refs[:n_in], refs[n_in:n_in + nx]
        cout, xout = refs[n_in + nx:n_in + nx + n_out], refs[n_in + nx + n_out:n_in + 2 * nx + n_out]
        rest = refs[n_in + 2 * nx + n_out:]
        cscr, sems = rest[:n_scr], rest[n_scr:]
        first = functools.reduce(jnp.logical_and, [pl.program_id(a) == 0 for a in range(len(grid))])
        last = functools.reduce(jnp.logical_and, [pl.program_id(a) == grid[a] - 1 for a in range(len(grid))])

        @pl.when(first)
        def _():
            carry.start(xin, xout, sems)

        body(*cin, *cout, *cscr)

        @pl.when(last)
        def _():
            carry.wait(xin, xout, sems)

    return pl.pallas_call(
        wrapped, name=name, grid=grid, in_specs=list(in_specs) + [HBM_SPEC] * nx,
        out_specs=list(out_specs) + [HBM_SPEC] * nx, out_shape=list(out_shape) + carry.out_shape(),
        scratch_shapes=list(scratch_shapes) + carry.scratch_shapes(), compiler_params=_params(len(grid)),
    )(*args, *carry.payloads)


def _exchange_call(name, payloads, gather):
    ex = _Exchange(payloads, gather)

    def body(*refs):
        ins, outs, sems = refs[:ex.n], refs[ex.n:2 * ex.n], refs[2 * ex.n:]
        ex.start(ins, outs, sems)
        ex.wait(ins, outs, sems)

    return pl.pallas_call(body, name=name, in_specs=[HBM_SPEC] * ex.n, out_specs=[HBM_SPEC] * ex.n,
                          out_shape=ex.out_shape(), scratch_shapes=ex.scratch_shapes())(*ex.payloads)


def _gather_once_per_chip(name, shard):
    def body(src, out, send_sems, recv_sems, local_sem):
        x, y, c = lax.axis_index("x"), lax.axis_index("y"), lax.axis_index("c")
        me, sibling = (x, y, c), (x, y, 1 - c)
        chips = [(1 - x, y), (x, 1 - y), (1 - x, 1 - y)]
        slot = lambda px, py, pc: out.at[4 * px + 2 * py + pc]

        def copy(k, block, to, from_src=False):
            return pltpu.make_async_remote_copy(
                src_ref=src if from_src else slot(*block), dst_ref=slot(*block), send_sem=send_sems.at[k],
                recv_sem=recv_sems.at[k], device_id=to, device_id_type=pl.DeviceIdType.MESH)

        mine = pltpu.make_async_copy(src, slot(*me), local_sem)
        mine.start()
        first = [copy(0, me, sibling, True)] + [copy(1 + j, me, (*chip, c), True) for j, chip in enumerate(chips)]
        for cp in first:
            cp.start()
        passed = [copy(4 + j, (*chip, c), sibling) for j, chip in enumerate(chips)]
        for j, chip in enumerate(chips):
            copy(1 + j, (*chip, c), me).wait_recv()
            passed[j].start()
        copy(0, sibling, me).wait_recv()
        for j, chip in enumerate(chips):
            copy(4 + j, (*chip, 1 - c), me).wait_recv()
        for cp in first + passed:
            cp.wait_send()
        mine.wait()

    return pl.pallas_call(
        body, name=name, in_specs=[HBM_SPEC], out_specs=HBM_SPEC,
        out_shape=jax.ShapeDtypeStruct((N_DEV,) + shard.shape, shard.dtype),
        scratch_shapes=[pltpu.SemaphoreType.DMA((N_DEV - 1,)), pltpu.SemaphoreType.DMA((N_DEV - 1,)),
                        pltpu.SemaphoreType.DMA],
    )(shard)


def _in_proj_fwd(x, g, w, tm, carry=None):
    T = x.shape[0]

    def body(x_ref, g_ref, w_hbm, proj_ref, h_ref, r_ref, w_ref, sem):
        @pl.when(pl.program_id(0) == 0)
        def _():
            _load_resident(w_hbm, w_ref, sem)

        xf = x_ref[...]
        r = lax.rsqrt(jnp.mean(xf * xf, axis=-1, keepdims=True) + EPS)
        h = (xf * r * g_ref[...]).astype(BF16)
        h_ref[...] = h
        r_ref[...] = r
        for j in range(N_DEV):
            proj_ref[:, j * WIN_BLK:(j + 1) * WIN_BLK] = _dot(h, w_ref[j])

    return _pcall(
        body, "in_proj_fwd", (T // tm,),
        [pl.BlockSpec((tm, D_MODEL), lambda i: (i, 0)), pl.BlockSpec((1, D_MODEL), lambda i: (0, 0)), ANY_SPEC],
        [pl.BlockSpec((tm, IN_WIDTH), lambda i: (i, 0)),
         pl.BlockSpec((tm, D_MODEL), lambda i: (i, 0)),
         pl.BlockSpec((tm, 1), lambda i: (i, 0))],
        [jax.ShapeDtypeStruct((T, IN_WIDTH), F32),
         jax.ShapeDtypeStruct((T, D_MODEL), BF16),
         jax.ShapeDtypeStruct((T, 1), F32)],
        [pltpu.VMEM(w.shape, w.dtype), pltpu.SemaphoreType.DMA], (x, g, w), carry)


def _ret_common(q_ref, k_ref, v_ref, cos_ref, sin_ref, mask_ref, rd_ref, sin_state):
    c = cos_ref[...]
    s = sin_ref[...]
    q = q_ref[...]
    q = q * c + pltpu.roll(q, HEAD_DIM // 2, 1) * s
    k = k_ref[...]
    k = (k * c + pltpu.roll(k, HEAD_DIM // 2, 1) * s) * (HEAD_DIM ** -0.5)
    qb = q.astype(BF16)
    kb = k.astype(BF16)
    vb = v_ref[...].astype(BF16)
    pm = (_dot_nt(qb, kb) * mask_ref[...]).astype(BF16)
    qd = (q * rd_ref[...]).astype(BF16)
    o = _dot(pm, vb) + _dot(qd, sin_state.astype(BF16))
    return q, k, qb, kb, vb, pm, qd, o


def _ret_specs(T, rev):
    nb = T // RET_BLOCK
    groups = RET_HEADS // RET_HPS
    wide = RET_HPS * HEAD_DIM
    blk = (lambda b: nb - 1 - b) if rev else (lambda b: b)
    col = lambda piece: (pl.BlockSpec((RET_BLOCK, wide), lambda h, b: (blk(b), piece * groups + h)), "lane")
    return dict(
        q=col(0), k=col(1), v=col(2), g=col(3),
        tab=(pl.BlockSpec((RET_BLOCK, HEAD_DIM), lambda h, b: (blk(b), 0)), None),
        mask=(pl.BlockSpec((RET_HPS, RET_BLOCK, RET_BLOCK), lambda h, b: (h, 0, 0)), "lead"),
        dec=(pl.BlockSpec((RET_HPS, RET_BLOCK, HEAD_DIM), lambda h, b: (h, 0, 0)), "lead"),
        gtb=(pl.BlockSpec((RET_HPS, 1, HEAD_DIM), lambda h, b: (h, 0, 0)), "lead"),
        gn=(pl.BlockSpec((1, wide), lambda h, b: (0, h)), "lane"),
        state=(pl.BlockSpec((RET_HPS, None, HEAD_DIM, HEAD_DIM), lambda h, b: (h, blk(b), 0, 0)), "lead"),
        rows=(pl.BlockSpec((RET_BLOCK, wide), lambda h, b: (blk(b), h)), "lane"),
        scratch=(pltpu.VMEM((RET_HPS, HEAD_DIM, HEAD_DIM), F32), "lead"),
    )


def _per_head(head_body, kinds):
    def body(*refs):
        for hh in range(RET_HPS):
            views = []
            for ref, kind in zip(refs, kinds):
                if kind == "lane":
                    views.append(ref.at[:, hh * HEAD_DIM:(hh + 1) * HEAD_DIM])
                elif kind == "lead":
                    views.append(ref.at[hh])
                else:
                    views.append(ref)
            head_body(*views)
    return body


def _ret_fwd(proj, cosf, sinf, mask, rowdec, kdec, gtb, gn, carry=None):
    T = proj.shape[0]
    nb = T // RET_BLOCK
    sp = _ret_specs(T, False)

    def body(q_ref, k_ref, v_ref, g_ref, cos_ref, sin_ref, mask_ref, rd_ref, kd_ref, gtb_ref, gn_ref,
             y_ref, sb_ref, st):
        @pl.when(pl.program_id(1) == 0)
        def _():
            st[...] = jnp.zeros_like(st)
        s_in = st[...]
        sb_ref[...] = s_in
        q, k, qb, kb, vb, pm, qd, o = _ret_common(q_ref, k_ref, v_ref, cos_ref, sin_ref, mask_ref, rd_ref, s_in)
        st[...] = gtb_ref[...] * s_in + _dot_tn((k * kd_ref[...]).astype(BF16), vb)
        mu = jnp.mean(o, axis=-1, keepdims=True)
        oc = o - mu
        n = oc * lax.rsqrt(jnp.mean(oc * oc, axis=-1, keepdims=True) + EPS)
        gt = g_ref[...]
        y_ref[...] = (gt * _sigmoid(gt) * (n * gn_ref[...])).astype(BF16)

    ins = [sp[n] for n in ("q", "k", "v", "g", "tab", "tab", "mask", "dec", "dec", "gtb", "gn")]
    outs = [sp["rows"], sp["state"]]
    return _pcall(
        _per_head(body, [kind for _, kind in ins + outs + [sp["scratch"]]]), "ret_fwd", (RET_HEADS // RET_HPS, nb),
        [s for s, _ in ins], [s for s, _ in outs],
        [jax.ShapeDtypeStruct((T, RET_WIDTH), BF16),
         jax.ShapeDtypeStruct((RET_HEADS, nb, HEAD_DIM, HEAD_DIM), F32)],
        [sp["scratch"][0]],
        (proj, proj, proj, proj, cosf, sinf, mask, rowdec, kdec, gtb, gn), carry)


def _scan(re, im, ar, ai, reverse):
    n = re.shape[0]
    row = lax.broadcasted_iota(jnp.int32, re.shape, 0)
    s = 1
    while s < n:
        if reverse:
            keep = row < n - s
            sr = jnp.where(keep, pltpu.roll(re, n - s, 0), 0.0)
            si = jnp.where(keep, pltpu.roll(im, n - s, 0), 0.0)
        else:
            keep = row >= s
            sr = jnp.where(keep, pltpu.roll(re, s, 0), 0.0)
            si = jnp.where(keep, pltpu.roll(im, s, 0), 0.0)
        re, im = re + ar * sr - ai * si, im + ar * si + ai * sr
        ar, ai = ar * ar - ai * ai, 2.0 * ar * ai
        s *= 2
    return re, im


S5_STATE_TILE = (S5_TILE, S5_LANES)


def _step_major_permutation():
    r = jnp.arange(S5_TILE)
    t_of_row = (r % S5_CHUNKS) * S5_STEPS + r // S5_CHUNKS
    return (t_of_row[:, None] == r[None, :]).astype(BF16)


def _permute_rows_f32(pm, x):
    hi = x.astype(BF16)
    rest = x - hi.astype(F32)
    mid = rest.astype(BF16)
    lo = (rest - mid.astype(F32)).astype(BF16)
    return _dot(pm, hi) + _dot(pm, mid) + _dot(pm, lo)


def _step_get(ref, j):
    return ref[j * S5_CHUNKS:(j + 1) * S5_CHUNKS, :]


def _step_set(ref, j, val):
    ref[j * S5_CHUNKS:(j + 1) * S5_CHUNKS, :] = val


def _tile_get(ref):
    return ref[...]


def _tile_set(ref, val):
    ref[...] = val


def _fill_power_table(ptab, lr, li):
    shape = (S5_CHUNKS, S5_LANES)
    lrb = jnp.broadcast_to(lr, shape)
    lib = jnp.broadcast_to(li, shape)
    pr, pi_ = lrb, lib
    for j in range(S5_STEPS):
        ptab[0, j * S5_CHUNKS:(j + 1) * S5_CHUNKS, :] = pr
        ptab[1, j * S5_CHUNKS:(j + 1) * S5_CHUNKS, :] = pi_
        pr, pi_ = lrb * pr - lib * pi_, lrb * pi_ + lib * pr


def _chunk_scans(xr, xi, lr, li, reverse):
    shape = (S5_CHUNKS, S5_LANES)
    lrb = jnp.broadcast_to(lr, shape)
    lib = jnp.broadcast_to(li, shape)
    sr = si = None
    for j in (range(S5_STEPS - 1, -1, -1) if reverse else range(S5_STEPS)):
        vr = _step_get(xr, j)
        vi = _step_get(xi, j)
        if sr is not None:
            vr, vi = vr + lrb * sr - lib * si, vi + lrb * si + lib * sr
            _step_set(xr, j, vr)
            _step_set(xi, j, vi)
        sr, si = vr, vi
    return sr, si


def _entering_states(zr, zi, cr, ci, ar, ai, reverse):
    shape = (S5_CHUNKS, S5_LANES)
    row = lax.broadcasted_iota(jnp.int32, shape, 0)
    if reverse:
        edge, shift = row == S5_CHUNKS - 1, S5_CHUNKS - 1
    else:
        edge, shift = row == 0, 1
    wr = jnp.where(edge, jnp.broadcast_to(cr, shape), pltpu.roll(zr, shift, 0))
    wi = jnp.where(edge, jnp.broadcast_to(ci, shape), pltpu.roll(zi, shift, 0))
    return _scan(wr, wi, ar, ai, reverse)


def _table_rows(ptab, j, conj):
    pr = ptab[0, j * S5_CHUNKS:(j + 1) * S5_CHUNKS, :]
    pi_ = ptab[1, j * S5_CHUNKS:(j + 1) * S5_CHUNKS, :]
    return pr, (-pi_ if conj else pi_)


def _s5_forward_states(xr, xi, lr, li, cr, ci, ptab):
    zr, zi = _chunk_scans(xr, xi, lr, li, False)
    ar, ai = _table_rows(ptab, S5_STEPS - 1, False)
    er, ei = _entering_states(zr, zi, cr, ci, ar, ai, False)
    for j in range(S5_STEPS):
        pr, pi_ = _table_rows(ptab, j, False)
        _step_set(xr, j, _step_get(xr, j) + pr * er - pi_ * ei)
        _step_set(xi, j, _step_get(xi, j) + pr * ei + pi_ * er)
    last = S5_CHUNKS - 1
    end_r = (ar * er - ai * ei + zr)[last:last + 1, :]
    end_i = (ar * ei + ai * er + zi)[last:last + 1, :]
    return er, ei, end_r, end_i


def _s5_specs(T, rev):
    nt = T // S5_TILE
    tt = (lambda t: nt - 1 - t) if rev else (lambda t: t)
    return dict(
        u=pl.BlockSpec((S5_TILE, LANE), lambda b, t: (tt(t), 4 * RET_HEADS + b)),
        rows=pl.BlockSpec((S5_TILE, LANE), lambda b, t: (tt(t), b)),
        to_state=pl.BlockSpec((None, LANE, S5_LANES), lambda b, t: (b, 0, 0)),
        from_state=pl.BlockSpec((None, S5_LANES, LANE), lambda b, t: (b, 0, 0)),
        lam=pl.BlockSpec((None, 2, S5_LANES), lambda b, t: (b, 0, 0)),
        d=pl.BlockSpec((1, LANE), lambda b, t: (0, b)),
        perm=pl.BlockSpec((S5_TILE, S5_TILE), lambda b, t: (0, 0)),
        bound=pl.BlockSpec((None, None, 2, S5_LANES), lambda b, t: (b, tt(t), 0, 0)),
    )


def _s5_fwd(proj, pm, pm_t, bre, bim, cre_t, cim_t, lam, d, carry=None):
    T = proj.shape[0]
    nt = T // S5_TILE
    sp = _s5_specs(T, False)

    def body(u_ref, pm_ref, pmt_ref, bre_ref, bim_ref, cre_ref, cim_ref, lam_ref, d_ref, y_ref, bound_ref,
             carry, ptab, xr, xi):
        lr = lam_ref[0:1, :]
        li = lam_ref[1:2, :]

        @pl.when(pl.program_id(1) == 0)
        def _():
            carry[...] = jnp.zeros_like(carry)
            _fill_power_table(ptab, lr, li)

        u = _permute_rows_f32(pm_ref[...], u_ref[...])
        ub = u.astype(BF16)
        _tile_set(xr, _dot(ub, bre_ref[...]))
        _tile_set(xi, _dot(ub, bim_ref[...]))
        bound_ref[...] = carry[...]
        _, _, end_r, end_i = _s5_forward_states(xr, xi, lr, li, carry[0:1, :], carry[1:2, :], ptab)
        carry[0:1, :] = end_r
        carry[1:2, :] = end_i
        y = (_dot(_tile_get(xr).astype(BF16), cre_ref[...]) - _dot(_tile_get(xi).astype(BF16), cim_ref[...])
             + d_ref[...] * u)
        y_ref[...] = _permute_rows_f32(pmt_ref[...], y)

    state = pltpu.VMEM(S5_STATE_TILE, F32)
    return _pcall(
        body, "s5_fwd", (S5_NBLK, nt),
        [sp["u"], sp["perm"], sp["perm"], sp["to_state"], sp["to_state"], sp["from_state"],
         sp["from_state"], sp["lam"], sp["d"]],
        [sp["rows"], sp["bound"]],
        [jax.ShapeDtypeStruct((T, SSM_WIDTH), F32),
         jax.ShapeDtypeStruct((S5_NBLK, nt, 2, S5_LANES), F32)],
        [pltpu.VMEM((2, S5_LANES), F32), pltpu.VMEM((2, S5_TILE, S5_LANES), F32), state, state],
        (proj, pm, pm_t, bre, bim, cre_t, cim_t, lam, d), carry)


def _glu_fwd(y, w, b, og, tm):
    T = y.shape[0]

    def body(y_ref, w_ref, b_ref, og_ref, z_ref, o_ref, r_ref):
        y1 = _gelu(y_ref[...])
        z = _dot(y1.astype(BF16), w_ref[...]) + b_ref[...]
        y2 = y1 * _sigmoid(z)
        r = lax.rsqrt(jnp.mean(y2 * y2, axis=-1, keepdims=True) + EPS)
        z_ref[...] = z
        o_ref[...] = (y2 * r * og_ref[...]).astype(BF16)
        r_ref[...] = r

    row = pl.BlockSpec((tm, SSM_WIDTH), lambda i: (i, 0))
    vec = pl.BlockSpec((1, SSM_WIDTH), lambda i: (0, 0))
    return pl.pallas_call(
        body, name="glu_fwd", grid=(T // tm,),
        in_specs=[row, pl.BlockSpec((SSM_WIDTH, SSM_WIDTH), lambda i: (0, 0)), vec, vec],
        out_specs=[row, row, pl.BlockSpec((tm, 1), lambda i: (i, 0))],
        out_shape=[jax.ShapeDtypeStruct((T, SSM_WIDTH), F32), jax.ShapeDtypeStruct((T, SSM_WIDTH), BF16),
                   jax.ShapeDtypeStruct((T, 1), F32)],
        compiler_params=_params(1),
    )(y, w, b, og)


def _out_proj_fwd(x, y_ret, y_ssm, w, g, tm):
    T = x.shape[0]

    def body(x_ref, a_ref, b_ref, w_ref, g_ref, x2_ref, h_ref, r_ref):
        x2 = x_ref[...] + _dot(a_ref[...], w_ref[0:RET_WIDTH, :]) + _dot(b_ref[...], w_ref[RET_WIDTH:D_MODEL, :])
        r = lax.rsqrt(jnp.mean(x2 * x2, axis=-1, keepdims=True) + EPS)
        x2_ref[...] = x2
        h_ref[...] = (x2 * r * g_ref[...]).astype(BF16)
        r_ref[...] = r

    full = pl.BlockSpec((tm, D_MODEL), lambda i: (i, 0))
    half = pl.BlockSpec((tm, RET_WIDTH), lambda i: (i, 0))
    return pl.pallas_call(
        body, name="out_proj_fwd", grid=(T // tm,),
        in_specs=[full, half, half, pl.BlockSpec((D_MODEL, D_MODEL), lambda i: (0, 0)),
                  pl.BlockSpec((1, D_MODEL), lambda i: (0, 0))],
        out_specs=[full, full, pl.BlockSpec((tm, 1), lambda i: (i, 0))],
        out_shape=[jax.ShapeDtypeStruct((T, D_MODEL), F32), jax.ShapeDtypeStruct((T, D_MODEL), BF16),
                   jax.ShapeDtypeStruct((T, 1), F32)],
        compiler_params=_params(1),
    )(x, y_ret, y_ssm, w, g)


def _ffn_up(h, wg, wu, tm, carry=None):
    T = h.shape[0]

    def body(h_ref, wg_ref, wu_ref, a_ref, b_ref, f_ref):
        hb = h_ref[...]
        a = _dot(hb, wg_ref[...])
        b = _dot(hb, wu_ref[...])
        a_ref[...] = a.astype(BF16)
        b_ref[...] = b.astype(BF16)
        f_ref[...] = (a * _sigmoid(a) * b).astype(BF16)

    wspec = pl.BlockSpec((None, D_MODEL, FF_BLK), lambda j, i: (j, 0, 0))
    ospec = pl.BlockSpec((None, tm, FF_BLK), lambda j, i: (j, i, 0))
    oshape = jax.ShapeDtypeStruct((N_DEV, T, FF_BLK), BF16)
    return _pcall(
        body, "ffn_up", (N_DEV, T // tm),
        [pl.BlockSpec((tm, D_MODEL), lambda j, i: (i, 0)), wspec, wspec],
        [ospec, ospec, ospec], [oshape, oshape, oshape], [], (h, wg, wu), carry)


def _ffn_down_loss(f, wd, x2, tgt, g, tm):
    T = x2.shape[0]

    def body(f_ref, w_hbm, x2_ref, t_ref, g_ref, dx_ref, dxb_ref, loss_ref, dg_ref, w_ref, sem):
        i = pl.program_id(0)

        @pl.when(i == 0)
        def _():
            _load_resident(w_hbm, w_ref, sem)
            loss_ref[...] = jnp.zeros_like(loss_ref)
            dg_ref[...] = jnp.zeros_like(dg_ref)

        gv = g_ref[...]
        x3 = x2_ref[...]
        for k in range(N_DEV):
            x3 = x3 + _dot(f_ref[k], w_ref[k])
        r = lax.rsqrt(jnp.mean(x3 * x3, axis=-1, keepdims=True) + EPS)
        err = x3 * r * gv - t_ref[...]
        tile_loss = 0.5 * jnp.sum(jnp.mean(err * err, axis=-1, keepdims=True), axis=0, keepdims=True)
        dx, dgt = _rms_bwd(err * (1.0 / D_MODEL), x3, r, gv)
        dx_ref[...] = dx
        dxb_ref[...] = dx.astype(BF16)
        loss_ref[...] += jnp.broadcast_to(tile_loss, loss_ref.shape)
        dg_ref[...] += jnp.sum(dgt, axis=0, keepdims=True)

    full = pl.BlockSpec((tm, D_MODEL), lambda i: (i, 0))
    vec = pl.BlockSpec((1, D_MODEL), lambda i: (0, 0))
    return pl.pallas_call(
        body, name="ffn_down_loss", grid=(T // tm,),
        in_specs=[pl.BlockSpec((N_DEV, tm, FF_BLK), lambda i: (0, i, 0)), ANY_SPEC, full, full, vec],
        out_specs=[full, full, pl.BlockSpec((8, LANE), lambda i: (0, 0)), vec],
        out_shape=[jax.ShapeDtypeStruct((T, D_MODEL), F32), jax.ShapeDtypeStruct((T, D_MODEL), BF16),
                   jax.ShapeDtypeStruct((8, LANE), F32), jax.ShapeDtypeStruct((1, D_MODEL), F32)],
        scratch_shapes=[pltpu.VMEM(wd.shape, wd.dtype), pltpu.SemaphoreType.DMA],
        compiler_params=_params(1),
    )(f, wd, x2, tgt, g)


def _ffn_bwd_act(dxb, wd, a, b, tm):
    T = dxb.shape[0]

    def body(dx_ref, w_ref, a_ref, b_ref, da_ref, db_ref):
        df = _dot_nt(dx_ref[...], w_ref[...])
        a = a_ref[...].astype(F32)
        b = b_ref[...].astype(F32)
        sg = _sigmoid(a)
        da_ref[...] = (df * b * sg * (1.0 + a * (1.0 - sg))).astype(BF16)
        db_ref[...] = (df * a * sg).astype(BF16)

    blk = pl.BlockSpec((None, tm, FF_BLK), lambda j, i: (j, i, 0))
    oshape = jax.ShapeDtypeStruct((N_DEV, T, FF_BLK), BF16)
    return pl.pallas_call(
        body, name="ffn_bwd_act", grid=(N_DEV, T // tm),
        in_specs=[pl.BlockSpec((tm, D_MODEL), lambda j, i: (i, 0)),
                  pl.BlockSpec((None, FF_BLK, D_MODEL), lambda j, i: (j, 0, 0)), blk, blk],
        out_specs=[blk, blk], out_shape=[oshape, oshape],
        compiler_params=_params(2),
    )(dxb, wd, a, b)


def _ffn_bwd_in(da, db, wg, wu, tm, carry=None):
    T = da.shape[1]

    def body(da_ref, db_ref, wg_ref, wu_ref, dh_ref):
        part = _dot_nt(da_ref[...], wg_ref[...]) + _dot_nt(db_ref[...], wu_ref[...])

        @pl.when(pl.program_id(1) == 0)
        def _():
            dh_ref[...] = part

        @pl.when(pl.program_id(1) > 0)
        def _():
            dh_ref[...] += part

    ablk = pl.BlockSpec((None, tm, FF_BLK), lambda i, k: (k, i, 0))
    wblk = pl.BlockSpec((None, D_MODEL, FF_BLK), lambda i, k: (k, 0, 0))
    return _pcall(
        body, "ffn_bwd_in", (T // tm, N_DEV), [ablk, ablk, wblk, wblk],
        [pl.BlockSpec((tm, D_MODEL), lambda i, k: (i, 0))], [jax.ShapeDtypeStruct((T, D_MODEL), F32)],
        [], (da, db, wg, wu), carry)


def _ffn_wgrad_up(h, da, db, tk, carry=None):
    T = h.shape[0]
    nk = T // tk

    def body(h_ref, da_ref, db_ref, g_ref, u_ref, accg, accu):
        k = pl.program_id(1)

        @pl.when(k == 0)
        def _():
            accg[...] = jnp.zeros_like(accg)
            accu[...] = jnp.zeros_like(accu)

        hb = h_ref[...]
        accg[...] += _dot_tn(hb, da_ref[...])
        accu[...] += _dot_tn(hb, db_ref[...])

        @pl.when(k == nk - 1)
        def _():
            g_ref[...] = accg[...].astype(BF16)
            u_ref[...] = accu[...].astype(BF16)

    blk = pl.BlockSpec((None, tk, FF_BLK), lambda j, k: (j, k, 0))
    ospec = pl.BlockSpec((None, D_MODEL, FF_BLK), lambda j, k: (j, 0, 0))
    oshape = jax.ShapeDtypeStruct((N_DEV, D_MODEL, FF_BLK), BF16)
    return _pcall(
        body, "ffn_wgrad_up", (N_DEV, nk),
        [pl.BlockSpec((tk, D_MODEL), lambda j, k: (k, 0)), blk, blk],
        [ospec, ospec], [oshape, oshape],
        [pltpu.VMEM((D_MODEL, FF_BLK), F32), pltpu.VMEM((D_MODEL, FF_BLK), F32)], (h, da, db), carry)


def _ffn_wgrad_down(f, dxb, tk):
    T = dxb.shape[0]
    nk = T // tk

    def body(f_ref, dx_ref, o_ref, acc):
        k = pl.program_id(1)

        @pl.when(k == 0)
        def _():
            acc[...] = jnp.zeros_like(acc)

        acc[...] += _dot_tn(f_ref[...], dx_ref[...])

        @pl.when(k == nk - 1)
        def _():
            o_ref[...] = acc[...].astype(BF16)

    return pl.pallas_call(
        body, name="ffn_wgrad_down", grid=(N_DEV, nk),
        in_specs=[pl.BlockSpec((None, tk, FF_BLK), lambda j, k: (j, k, 0)),
                  pl.BlockSpec((tk, D_MODEL), lambda j, k: (k, 0))],
        out_specs=pl.BlockSpec((None, FF_BLK, D_MODEL), lambda j, k: (j, 0, 0)),
        out_shape=jax.ShapeDtypeStruct((N_DEV, FF_BLK, D_MODEL), BF16),
        scratch_shapes=[pltpu.VMEM((FF_BLK, D_MODEL), F32)],
        compiler_params=_params(2),
    )(f, dxb)


def _out_proj_bwd(dh2, x2, r2, g, dx3, w, tm):
    T = x2.shape[0]

    def body(dh_ref, x_ref, r_ref, g_ref, dx3_ref, w_ref, dx_ref, dxb_ref, dg_ref, a_ref, b_ref):
        @pl.when(pl.program_id(0) == 0)
        def _():
            dg_ref[...] = jnp.zeros_like(dg_ref)

        dxn, dgt = _rms_bwd(dh_ref[...], x_ref[...], r_ref[...], g_ref[...])
        dx = dx3_ref[...] + dxn
        dxv = dx.astype(BF16)
        dx_ref[...] = dx
        dxb_ref[...] = dxv
        dg_ref[...] += jnp.sum(dgt, axis=0, keepdims=True)
        a_ref[...] = _dot_nt(dxv, w_ref[0:RET_WIDTH, :])
        b_ref[...] = _dot_nt(dxv, w_ref[RET_WIDTH:D_MODEL, :])

    full = pl.BlockSpec((tm, D_MODEL), lambda i: (i, 0))
    vec = pl.BlockSpec((1, D_MODEL), lambda i: (0, 0))
    half = pl.BlockSpec((tm, RET_WIDTH), lambda i: (i, 0))
    hshape = jax.ShapeDtypeStruct((T, RET_WIDTH), F32)
    return pl.pallas_call(
        body, name="out_proj_bwd", grid=(T // tm,),
        in_specs=[full, full, pl.BlockSpec((tm, 1), lambda i: (i, 0)), vec, full,
                  pl.BlockSpec((D_MODEL, D_MODEL), lambda i: (0, 0))],
        out_specs=[full, full, vec, half, half],
        out_shape=[jax.ShapeDtypeStruct((T, D_MODEL), F32), jax.ShapeDtypeStruct((T, D_MODEL), BF16),
                   jax.ShapeDtypeStruct((1, D_MODEL), F32), hshape, hshape],
        compiler_params=_params(1),
    )(dh2, x2, r2, g, dx3, w)


def _wgrad_rows(name, a, b, tk):
    T, M = a.shape
    N = b.shape[1]
    nk = T // tk

    def body(a_ref, b_ref, o_ref, acc):
        k = pl.program_id(0)

        @pl.when(k == 0)
        def _():
            acc[...] = jnp.zeros_like(acc)

        acc[...] += _dot_tn(a_ref[...], b_ref[...])

        @pl.when(k == nk - 1)
        def _():
            o_ref[...] = acc[...].astype(BF16)

    return pl.pallas_call(
        body, name=name, grid=(nk,),
        in_specs=[pl.BlockSpec((tk, M), lambda k: (k, 0)), pl.BlockSpec((tk, N), lambda k: (k, 0))],
        out_specs=pl.BlockSpec((M, N), lambda k: (0, 0)),
        out_shape=jax.ShapeDtypeStruct((M, N), BF16),
        scratch_shapes=[pltpu.VMEM((M, N), F32)],
        compiler_params=_params(1),
    )(a, b)


def _glu_bwd(y, z, r, dyo, w, og, tm):
    T = y.shape[0]

    def body(y_ref, z_ref, r_ref, d_ref, w_ref, og_ref, dy_ref, dw_ref, db_ref, dog_ref):
        @pl.when(pl.program_id(0) == 0)
        def _():
            dw_ref[...] = jnp.zeros_like(dw_ref)
            db_ref[...] = jnp.zeros_like(db_ref)
            dog_ref[...] = jnp.zeros_like(dog_ref)

        y1, g1 = _gelu_and_grad(y_ref[...])
        sg = _sigmoid(z_ref[...])
        y2 = y1 * sg
        dy2, dogt = _rms_bwd(d_ref[...], y2, r_ref[...], og_ref[...])
        dog_ref[...] += jnp.sum(dogt, axis=0, keepdims=True)
        dz = dy2 * y1 * sg * (1.0 - sg)
        db_ref[...] += jnp.sum(dz, axis=0, keepdims=True)
        dzb = dz.astype(BF16)
        dw_ref[...] += _dot_tn(y1.astype(BF16), dzb)
        dy_ref[...] = (dy2 * sg + _dot_nt(dzb, w_ref[...])) * g1

    row = pl.BlockSpec((tm, SSM_WIDTH), lambda i: (i, 0))
    vec = pl.BlockSpec((1, SSM_WIDTH), lambda i: (0, 0))
    sq = pl.BlockSpec((SSM_WIDTH, SSM_WIDTH), lambda i: (0, 0))
    return pl.pallas_call(
        body, name="glu_bwd", grid=(T // tm,),
        in_specs=[row, row, pl.BlockSpec((tm, 1), lambda i: (i, 0)), row, sq, vec],
        out_specs=[row, sq, vec, vec],
        out_shape=[jax.ShapeDtypeStruct((T, SSM_WIDTH), F32), jax.ShapeDtypeStruct((SSM_WIDTH, SSM_WIDTH), F32),
                   jax.ShapeDtypeStruct((1, SSM_WIDTH), F32), jax.ShapeDtypeStruct((1, SSM_WIDTH), F32)],
        compiler_params=_params(1),
    )(y, z, r, dyo, w, og)


def _s5_bwd(proj, dy, bound, pm, pm_t, bre, bim, bre_t, bim_t, cre, cim, lam, d, carry=None):
    T = proj.shape[0]
    nt = T // S5_TILE
    sp = _s5_specs(T, True)

    def body(u_ref, dy_ref, bound_ref, pm_ref, pmt_ref, bre_ref, bim_ref, bret_ref, bimt_ref, cre_ref, cim_ref,
             lam_ref, d_ref,
             du_ref, dbre_ref, dbim_ref, dcre_ref, dcim_ref, dlam_ref, dd_ref, carry, ptab, sr, si, gr, gi):
        lr = lam_ref[0:1, :]
        li = lam_ref[1:2, :]

        @pl.when(pl.program_id(1) == 0)
        def _():
            carry[...] = jnp.zeros_like(carry)
            _fill_power_table(ptab, lr, li)
            for ref in (dbre_ref, dbim_ref, dcre_ref, dcim_ref, dlam_ref, dd_ref):
                ref[...] = jnp.zeros_like(ref)

        u = _permute_rows_f32(pm_ref[...], u_ref[...])
        ub = u.astype(BF16)
        dyv = _permute_rows_f32(pm_ref[...], dy_ref[...])
        dyb = dyv.astype(BF16)
        _tile_set(sr, _dot(ub, bre_ref[...]))
        _tile_set(si, _dot(ub, bim_ref[...]))
        er, ei, _, _ = _s5_forward_states(sr, si, lr, li, bound_ref[0:1, :], bound_ref[1:2, :], ptab)
        _tile_set(gr, _dot(dyb, cre_ref[...]))
        _tile_set(gi, -_dot(dyb, cim_ref[...]))
        zr, zi = _chunk_scans(gr, gi, lr, -li, True)
        ar, ai = _table_rows(ptab, S5_STEPS - 1, True)
        fr, fi = _entering_states(zr, zi, carry[0:1, :], carry[1:2, :], ar, ai, True)
        acc_r = jnp.zeros((S5_CHUNKS, S5_LANES), F32)
        acc_i = jnp.zeros((S5_CHUNKS, S5_LANES), F32)
        for j in range(S5_STEPS):
            qr, qi = _table_rows(ptab, S5_STEPS - 1 - j, True)
            g_r = _step_get(gr, j) + qr * fr - qi * fi
            g_i = _step_get(gi, j) + qr * fi + qi * fr
            _step_set(gr, j, g_r)
            _step_set(gi, j, g_i)
            p_r, p_i = (er, ei) if j == 0 else (_step_get(sr, j - 1), _step_get(si, j - 1))
            acc_r += g_r * p_r + g_i * p_i
            acc_i += g_i * p_r - g_r * p_i
        dlam_ref[0:1, :] += jnp.sum(acc_r, axis=0, keepdims=True)
        dlam_ref[1:2, :] += jnp.sum(acc_i, axis=0, keepdims=True)
        g_all_r = _tile_get(gr)
        g_all_i = _tile_get(gi)
        carry[0:1, :] = g_all_r[0:1, :]
        carry[1:2, :] = g_all_i[0:1, :]
        grb = g_all_r.astype(BF16)
        gib = g_all_i.astype(BF16)
        du = (_dot(grb, bret_ref[...]) + _dot(gib, bimt_ref[...]) + d_ref[...] * dyv).astype(BF16)
        du_ref[...] = _dot(pmt_ref[...], du).astype(BF16)
        dbre_ref[...] += _dot_tn(grb, ub)
        dbim_ref[...] += _dot_tn(gib, ub)
        dcre_ref[...] += _dot_tn(dyb, _tile_get(sr).astype(BF16))
        dcim_ref[...] -= _dot_tn(dyb, _tile_get(si).astype(BF16))
        dd_ref[...] += jnp.sum(dyv * u, axis=0, keepdims=True)

    acc_ts = pl.BlockSpec((None, S5_LANES, LANE), lambda b, t: (b, 0, 0))
    acc_fs = pl.BlockSpec((None, LANE, S5_LANES), lambda b, t: (b, 0, 0))
    return _pcall(
        body, "s5_bwd", (S5_NBLK, nt),
        [sp["u"], sp["rows"], sp["bound"], sp["perm"], sp["perm"], sp["to_state"], sp["to_state"],
         sp["from_state"], sp["from_state"], sp["to_state"], sp["to_state"], sp["lam"], sp["d"]],
        [sp["rows"], acc_ts, acc_ts, acc_fs, acc_fs, sp["lam"], sp["d"]],
        [jax.ShapeDtypeStruct((T, SSM_WIDTH), BF16),
         jax.ShapeDtypeStruct((S5_NBLK, S5_LANES, LANE), F32),
         jax.ShapeDtypeStruct((S5_NBLK, S5_LANES, LANE), F32),
         jax.ShapeDtypeStruct((S5_NBLK, LANE, S5_LANES), F32),
         jax.ShapeDtypeStruct((S5_NBLK, LANE, S5_LANES), F32),
         jax.ShapeDtypeStruct((S5_NBLK, 2, S5_LANES), F32),
         jax.ShapeDtypeStruct((1, SSM_WIDTH), F32)],
        [pltpu.VMEM((2, S5_LANES), F32), pltpu.VMEM((2, S5_TILE, S5_LANES), F32)]
        + [pltpu.VMEM(S5_STATE_TILE, F32)] * 4,
        (proj, dy, bound, pm, pm_t, bre, bim, bre_t, bim_t, cre, cim, lam, d), carry)


def _ret_bwd(proj, cosf, sinf, mask, rowdec, kdec, gtb, gn, sblk, dyr):
    T = proj.shape[0]
    nb = T // RET_BLOCK
    sp = _ret_specs(T, True)

    def body(q_ref, k_ref, v_ref, g_ref, cos_ref, sin_ref, mask_ref, rd_ref, kd_ref, gtb_ref, gn_ref, sb_ref, dy_ref,
             dq_ref, dk_ref, dv_ref, dg_ref, dgn_ref, dst):
        @pl.when(pl.program_id(1) == 0)
        def _():
            dst[...] = jnp.zeros_like(dst)
            dgn_ref[...] = jnp.zeros_like(dgn_ref)

        s_in = sb_ref[...]
        q, k, qb, kb, vb, pm, qd, o = _ret_common(q_ref, k_ref, v_ref, cos_ref, sin_ref, mask_ref, rd_ref, s_in)
        mu = jnp.mean(o, axis=-1, keepdims=True)
        oc = o - mu
        rstd = lax.rsqrt(jnp.mean(oc * oc, axis=-1, keepdims=True) + EPS)
        n = oc * rstd
        gt = g_ref[...]
        sg = _sigmoid(gt)
        sil = gt * sg
        gnv = gn_ref[...]
        dyv = dy_ref[...]
        dg_ref[...] = (dyv * (n * gnv) * (sg * (1.0 + gt * (1.0 - sg)))).astype(BF16)
        dgn_ref[...] += jnp.sum(dyv * sil * n, axis=0, keepdims=True)
        dn = dyv * sil * gnv
        do = rstd * (dn - jnp.mean(dn, axis=-1, keepdims=True) - n * jnp.mean(dn * n, axis=-1, keepdims=True))
        dob = do.astype(BF16)
        ds = dst[...]
        dsb = ds.astype(BF16)
        kd = kd_ref[...]
        rd = rd_ref[...]
        dv_ref[...] = (_dot_tn(pm, dob) + _dot((k * kd).astype(BF16), dsb)).astype(BF16)
        dpb = (_dot_nt(dob, vb) * mask_ref[...]).astype(BF16)
        dq = _dot(dpb, kb) + _dot_nt(dob, s_in.astype(BF16)) * rd
        dk = (_dot_tn(dpb, qb) + _dot_nt(vb, dsb) * kd) * (HEAD_DIM ** -0.5)
        dst[...] = gtb_ref[...] * ds + _dot_tn(qd, dob)
        c = cos_ref[...]
        s = sin_ref[...]
        dq_ref[...] = (dq * c + pltpu.roll(dq * s, HEAD_DIM // 2, 1)).astype(BF16)
        dk_ref[...] = (dk * c + pltpu.roll(dk * s, HEAD_DIM // 2, 1)).astype(BF16)

    oshape = jax.ShapeDtypeStruct((T, RET_WIDTH), BF16)
    ins = [sp[n] for n in ("q", "k", "v", "g", "tab", "tab", "mask", "dec", "dec", "gtb", "gn", "state", "rows")]
    outs = [sp["rows"], sp["rows"], sp["rows"], sp["rows"], sp["gn"]]
    return pl.pallas_call(
        _per_head(body, [kind for _, kind in ins + outs + [sp["scratch"]]]), name="ret_bwd",
        grid=(RET_HEADS // RET_HPS, nb), in_specs=[s for s, _ in ins], out_specs=[s for s, _ in outs],
        out_shape=[oshape, oshape, oshape, oshape, jax.ShapeDtypeStruct((1, RET_WIDTH), F32)],
        scratch_shapes=[sp["scratch"][0]],
        compiler_params=_params(2),
    )(proj, proj, proj, proj, cosf, sinf, mask, rowdec, kdec, gtb, gn, sblk, dyr)


def _in_proj_bwd(dproj, w, x, r1, g, dx2, tm, carry=None):
    T = x.shape[0]

    def body(dp_ref, w_hbm, x_ref, r_ref, g_ref, dx2_ref, gx_ref, dg_ref, w_ref, sem):
        @pl.when(pl.program_id(0) == 0)
        def _():
            _load_resident(w_hbm, w_ref, sem)
            dg_ref[...] = jnp.zeros_like(dg_ref)

        dh = _dot_nt(dp_ref[:, 0:WIN_BLK], w_ref[0])
        for k in range(1, N_DEV):
            dh = dh + _dot_nt(dp_ref[:, k * WIN_BLK:(k + 1) * WIN_BLK], w_ref[k])
        dxn, dgt = _rms_bwd(dh, x_ref[...], r_ref[...], g_ref[...])
        gx_ref[...] = dx2_ref[...] + dxn
        dg_ref[...] += jnp.sum(dgt, axis=0, keepdims=True)

    full = pl.BlockSpec((tm, D_MODEL), lambda i: (i, 0))
    vec = pl.BlockSpec((1, D_MODEL), lambda i: (0, 0))
    return _pcall(
        body, "in_proj_bwd", (T // tm,),
        [pl.BlockSpec((tm, IN_WIDTH), lambda i: (i, 0)), ANY_SPEC,
         full, pl.BlockSpec((tm, 1), lambda i: (i, 0)), vec, full],
        [full, vec],
        [jax.ShapeDtypeStruct((T, D_MODEL), F32), jax.ShapeDtypeStruct((1, D_MODEL), F32)],
        [pltpu.VMEM(w.shape, w.dtype), pltpu.SemaphoreType.DMA], (dproj, w, x, r1, g, dx2), carry)


def _in_proj_wgrad(h, dproj, tk, carry=None):
    T = h.shape[0]
    nk = T // tk

    def body(h_ref, dp_ref, o_ref, acc):
        k = pl.program_id(1)

        @pl.when(k == 0)
        def _():
            acc[...] = jnp.zeros_like(acc)

        acc[...] += _dot_tn(h_ref[...], dp_ref[...])

        @pl.when(k == nk - 1)
        def _():
            o_ref[...] = acc[...].astype(BF16)

    return _pcall(
        body, "in_proj_wgrad", (N_DEV, nk),
        [pl.BlockSpec((tk, D_MODEL), lambda j, k: (k, 0)), pl.BlockSpec((tk, WIN_BLK), lambda j, k: (k, j))],
        [pl.BlockSpec((None, D_MODEL, WIN_BLK), lambda j, k: (j, 0, 0))],
        [jax.ShapeDtypeStruct((N_DEV, D_MODEL, WIN_BLK), BF16)],
        [pltpu.VMEM((D_MODEL, WIN_BLK), F32)], (h, dproj), carry)


def _rope_tables(T):
    half = HEAD_DIM // 2
    freqs = ROPE_BASE ** (-jnp.arange(half, dtype=F32) / half)
    ang = jnp.arange(T, dtype=F32)[:, None] * freqs[None, :]
    c = jnp.cos(ang)
    s = jnp.sin(ang)
    return jnp.concatenate([c, c], axis=1), jnp.concatenate([-s, s], axis=1)


def _retention_tables():
    hh = jnp.arange(RET_HEADS, dtype=F32)
    log_g = jnp.log1p(-(2.0 ** (-5.0 - hh)))[:, None, None]
    i = jnp.arange(RET_BLOCK)
    ci = (i // CHUNK)[:, None]
    cj = (i // CHUNK)[None, :]
    diff = (i[:, None] - i[None, :]).astype(F32)
    expo = jnp.where(ci == cj, jnp.abs(diff), diff)
    mask = jnp.where((cj <= ci)[None], jnp.exp(log_g * expo[None]), 0.0)
    r = jnp.arange(RET_BLOCK, dtype=F32)[None, :, None]
    ones = jnp.ones((1, 1, HEAD_DIM), F32)
    rowdec = jnp.exp(log_g * (r + 1.0)) * ones
    kdec = jnp.exp(log_g * (RET_BLOCK - 1.0 - r)) * ones
    gtb = jnp.exp(log_g * float(RET_BLOCK)) * ones
    return mask, rowdec, kdec, gtb


def _s5_discretise(a_re, a_im, log_dt, b_re, b_im):
    lam = lax.complex(a_re, a_im)
    dt = jnp.exp(log_dt)[:, None]
    lam_bar = jnp.exp(lam * dt)
    b_bar = ((lam_bar - 1.0) / lam)[..., None] * lax.complex(b_re, b_im)
    return jnp.real(lam_bar), jnp.imag(lam_bar), jnp.real(b_bar), jnp.imag(b_bar)


def _to_state_blockdiag(m):
    eye = jnp.eye(S5_GB, dtype=m.dtype)
    t = jnp.einsum("bgpc,gh->bgchp", m.reshape(S5_NBLK, S5_GB, SSM_STATE, SSM_GROUP), eye)
    return t.reshape(S5_NBLK, LANE, S5_LANES)


def _from_state_blockdiag(m):
    eye = jnp.eye(S5_GB, dtype=m.dtype)
    t = jnp.einsum("bgcp,gh->bgphc", m.reshape(S5_NBLK, S5_GB, SSM_GROUP, SSM_STATE), eye)
    return t.reshape(S5_NBLK, S5_LANES, LANE)


def _diag_of_state_major(acc):
    eye = jnp.eye(S5_GB, dtype=acc.dtype)
    t = acc.reshape(S5_NBLK, S5_GB, SSM_STATE, S5_GB, SSM_GROUP)
    return jnp.einsum("bgphc,gh->bgpc", t, eye).reshape(SSM_GROUPS, SSM_STATE, SSM_GROUP)


def _diag_of_channel_major(acc):
    eye = jnp.eye(S5_GB, dtype=acc.dtype)
    t = acc.reshape(S5_NBLK, S5_GB, SSM_GROUP, S5_GB, SSM_STATE)
    return jnp.einsum("bgchp,gh->bgcp", t, eye).reshape(SSM_GROUPS, SSM_GROUP, SSM_STATE)


SMALL_PARTIALS = (("ret_gn_g", 1024), ("lam_re", 4096), ("lam_im", 4096),
                  ("bbar_re", 65536), ("bbar_im", 65536), ("c_re", 65536), ("c_im", 65536),
                  ("ssm_d", 1024), ("b_glu", 1024), ("out_g", 1024), ("norm_ffn_g", 2048), ("norm_final_g", 2048))


def _forward_backward(x, tgt, shards, sm):
    T = x.shape[0]
    tm = min(1024, T)
    cosf, sinf = _rope_tables(T)
    mask, rowdec, kdec, gtb = _retention_tables()
    lbr, lbi, bbr, bbi = _s5_discretise(sm["ssm_a_re"], sm["ssm_a_im"], sm["ssm_log_dt"], sm["ssm_b_re"],
                                        sm["ssm_b_im"])
    bre = _to_state_blockdiag(bbr).astype(BF16)
    bim = _to_state_blockdiag(bbi).astype(BF16)
    cre_t = _from_state_blockdiag(sm["ssm_c_re"]).astype(BF16)
    cim_t = _from_state_blockdiag(sm["ssm_c_im"]).astype(BF16)
    bre_t = jnp.swapaxes(bre, 1, 2)
    bim_t = jnp.swapaxes(bim, 1, 2)
    cre = jnp.swapaxes(cre_t, 1, 2)
    cim = jnp.swapaxes(cim_t, 1, 2)
    lam = jnp.stack([lbr.reshape(S5_NBLK, S5_LANES), lbi.reshape(S5_NBLK, S5_LANES)], axis=1)
    pm = _step_major_permutation()
    pm_t = pm.T
    row = lambda v: v.reshape(1, -1)
    g_mix, g_ffn, g_fin = row(sm["norm_mix_g"]), row(sm["norm_ffn_g"]), row(sm["norm_final_g"])
    gn, dsk, bglu, og = row(sm["ret_gn_g"]), row(sm["ssm_d"]), row(sm["ssm_b_glu"]), row(sm["ssm_out_g"])

    w_in = _gather_once_per_chip("weight_gather", shards["w_in"])
    proj, h1, r1, w_gate = _in_proj_fwd(x, g_mix, w_in, 256, _Exchange([shards["w_gate"]], True))
    y_ret, sblk, w_glu, w_out = _ret_fwd(proj, cosf, sinf, mask, rowdec, kdec, gtb, gn,
                                         _Exchange([shards["ssm_w_glu"], shards["w_out"]], True))
    w_glu = w_glu.reshape(SSM_WIDTH, SSM_WIDTH)
    w_out = w_out.reshape(D_MODEL, D_MODEL)
    y_s5, bound, w_up = _s5_fwd(proj, pm, pm_t, bre, bim, cre_t, cim_t, lam, dsk, _Exchange([shards["w_up"]], True))
    z, y_ssm, r_ssm = _glu_fwd(y_s5, w_glu, bglu, og, 256)
    x2, h2, r2 = _out_proj_fwd(x, y_ret, y_ssm, w_out, g_ffn, 256)
    a, b, f, w_down = _ffn_up(h2, w_gate, w_up, tm, _Exchange([shards["w_down"]], True))
    dx3, dx3b, loss8, dg_fin = _ffn_down_loss(f, w_down, x2, tgt, g_fin, 256)

    landed = {}
    da, db = _ffn_bwd_act(dx3b, w_down, a, b, tm)
    dw_down = _ffn_wgrad_down(f, dx3b, tm)
    dw_gate, dw_up, landed["w_down"] = _ffn_wgrad_up(h2, da, db, tm, _Exchange([dw_down], False))
    dh2, landed["w_gate"] = _ffn_bwd_in(da, db, w_gate, w_up, min(1024, T), _Exchange([dw_gate], False))
    dx2, dx2b, dg_ffn, dy_ret, dy_ssm = _out_proj_bwd(dh2, x2, r2, g_ffn, dx3, w_out, 256)
    dw_out = jnp.concatenate([_wgrad_rows("out_proj_wgrad_ret", y_ret, dx2b, tm),
                              _wgrad_rows("out_proj_wgrad_ssm", y_ssm, dx2b, tm)], axis=0)
    dy_s5, dw_glu, db_glu, dog = _glu_bwd(y_s5, z, r_ssm, dy_ssm, w_glu, og, 256)
    du, dbre, dbim, dcre, dcim, dlam, dd, landed["w_up"] = _s5_bwd(
        proj, dy_s5, bound, pm, pm_t, bre, bim, bre_t, bim_t, cre, cim, lam, dsk, _Exchange([dw_up], False))
    dq, dk, dv, dgate, dgn = _ret_bwd(proj, cosf, sinf, mask, rowdec, kdec, gtb, gn, sblk, dy_ret)
    dproj = jnp.concatenate([dq, dk, dv, dgate, du], axis=1)
    small = dict(ret_gn_g=dgn, lam_re=dlam[:, 0], lam_im=dlam[:, 1],
                 bbar_re=_diag_of_state_major(dbre), bbar_im=_diag_of_state_major(dbim),
                 c_re=_diag_of_channel_major(dcre), c_im=_diag_of_channel_major(dcim),
                 ssm_d=dd, b_glu=db_glu, out_g=dog, norm_ffn_g=dg_ffn, norm_final_g=dg_fin)
    packed = _pack([small[n] for n, _ in SMALL_PARTIALS])
    dw_in, landed["w_out"], landed["ssm_w_glu"], small_landed = _in_proj_wgrad(
        h1, dproj, tm, _Exchange([dw_out.reshape(N_DEV, D_MODEL // N_DEV, D_MODEL),
                                  dw_glu.astype(BF16).reshape(N_DEV, SSM_WIDTH // N_DEV, SSM_WIDTH), packed],
                                 [False, False, True]))
    grad_x, dg_mix, landed["w_in"] = _in_proj_bwd(dproj, w_in, x, r1, g_mix, dx2, 256, _Exchange([dw_in], False))
    (mix_landed,) = _exchange_call("mix_gain_grad_gather", [_pack([dg_mix])], True)
    summed = dict(zip([n for n, _ in SMALL_PARTIALS],
                      _unpack(_sum_partials("small_grad_sum", small_landed), [(sz,) for _, sz in SMALL_PARTIALS])))
    summed["norm_mix_g"] = _sum_partials("mix_gain_grad_sum", mix_landed).reshape(-1)
    return loss8[0, 0], grad_x, landed, summed


def _small_grads(summed, sm):
    _, vjp = jax.vjp(_s5_discretise, sm["ssm_a_re"], sm["ssm_a_im"], sm["ssm_log_dt"], sm["ssm_b_re"], sm["ssm_b_im"])
    gp = (SSM_GROUPS, SSM_STATE)
    da_re, da_im, dlog_dt, db_re, db_im = vjp((summed["lam_re"].reshape(gp), summed["lam_im"].reshape(gp),
                                               summed["bbar_re"].reshape(gp + (SSM_GROUP,)),
                                               summed["bbar_im"].reshape(gp + (SSM_GROUP,))))
    return dict(norm_mix_g=summed["norm_mix_g"], ret_gn_g=summed["ret_gn_g"], ssm_a_re=da_re, ssm_a_im=da_im,
                ssm_log_dt=dlog_dt, ssm_b_re=db_re, ssm_b_im=db_im,
                ssm_c_re=summed["c_re"].reshape(SSM_GROUPS, SSM_GROUP, SSM_STATE),
                ssm_c_im=summed["c_im"].reshape(SSM_GROUPS, SSM_GROUP, SSM_STATE),
                ssm_d=summed["ssm_d"], ssm_b_glu=summed["b_glu"], ssm_out_g=summed["out_g"],
                norm_ffn_g=summed["norm_ffn_g"], norm_final_g=summed["norm_final_g"])


def _adamw_math(w, g, m, v):
    m2 = ADAM_B1 * m + (1.0 - ADAM_B1) * g
    v2 = ADAM_B2 * v + (1.0 - ADAM_B2) * (g * g)
    delta = -ADAM_LR * ((m2 / ADAM_BC1) / (jnp.sqrt(v2 / ADAM_BC2) + ADAM_EPS) + ADAM_WD * w)
    return delta, m2, v2


def _adamw_shard(name, parts, w, m, v, tr):
    rows, cols = w.shape

    def body(p_ref, w_ref, m_ref, v_ref, g_ref, d_ref, m2_ref, v2_ref):
        g = p_ref[0].astype(F32)
        for s in range(1, N_DEV):
            g = g + p_ref[s].astype(F32)
        d, m2, v2 = _adamw_math(w_ref[...], g, m_ref[...], v_ref[...])
        g_ref[...] = g
        d_ref[...] = d
        m2_ref[...] = m2
        v2_ref[...] = v2

    blk = pl.BlockSpec((tr, cols), lambda i: (i, 0))
    oshape = jax.ShapeDtypeStruct((rows, cols), F32)
    return pl.pallas_call(
        body, name=name, grid=(rows // tr,),
        in_specs=[pl.BlockSpec((N_DEV, tr, cols), lambda i: (0, i, 0)), blk, blk, blk],
        out_specs=[blk, blk, blk, blk], out_shape=[oshape] * 4,
        compiler_params=_params(1),
    )(parts, w, m, v)


def _sum_partials(name, parts):
    rows = parts.shape[1]

    def body(p_ref, o_ref):
        g = p_ref[0]
        for s in range(1, N_DEV):
            g = g + p_ref[s]
        o_ref[...] = g

    return pl.pallas_call(
        body, name=name, grid=(1,),
        in_specs=[pl.BlockSpec((N_DEV, rows, LANE), lambda i: (0, 0, 0))],
        out_specs=pl.BlockSpec((rows, LANE), lambda i: (0, 0)),
        out_shape=jax.ShapeDtypeStruct((rows, LANE), F32),
        compiler_params=_params(1),
    )(parts)


def _adamw_small(ws, gs, ms, vs):
    n = len(ws)

    def body(*refs):
        for i in range(n):
            w_ref, g_ref, m_ref, v_ref = (refs[k * n + i] for k in range(4))
            d_ref, m2_ref, v2_ref = (refs[(4 + k) * n + i] for k in range(3))
            d, m2, v2 = _adamw_math(w_ref[...], g_ref[...], m_ref[...], v_ref[...])
            d_ref[...] = d
            m2_ref[...] = m2
            v2_ref[...] = v2

    vmem = pl.BlockSpec(memory_space=pltpu.VMEM)
    out = pl.pallas_call(
        body, name="adamw_small", in_specs=[vmem] * (4 * n), out_specs=[vmem] * (3 * n),
        out_shape=[jax.ShapeDtypeStruct(w.shape, F32) for w in ws] * 3,
        compiler_params=pltpu.CompilerParams(vmem_limit_bytes=VMEM_LIMIT),
    )(*ws, *gs, *ms, *vs)
    return out[:n], out[n:2 * n], out[2 * n:]


def _pack(arrays):
    parts = [a.reshape(-1, LANE) for a in arrays]
    assert all(p.shape[0] % 8 == 0 for p in parts)
    return parts[0] if len(parts) == 1 else jnp.concatenate(parts, axis=0)


def _unpack(packed, shapes):
    flat = packed.reshape(-1)
    out, off = [], 0
    for shp in shapes:
        n = math.prod(shp)
        out.append(flat[off:off + n].reshape(shp))
        off += n + ((-n) % LANE)
    return out


WEIGHTS = ("norm_mix_g", "w_in", "ret_gn_g", "ssm_a_re", "ssm_a_im", "ssm_log_dt", "ssm_b_re", "ssm_b_im",
           "ssm_c_re", "ssm_c_im", "ssm_d", "ssm_w_glu", "ssm_b_glu", "ssm_out_g", "w_out", "norm_ffn_g", "w_gate",
           "w_up", "w_down", "norm_final_g")
BIG = ("w_in", "ssm_w_glu", "w_out", "w_gate", "w_up", "w_down")
SMALL = tuple(n for n in WEIGHTS if n not in BIG)
ADAM_ROWS = {"w_in": 256, "ssm_w_glu": 128, "w_out": 128, "w_gate": 256, "w_up": 256, "w_down": 176}


def kernel(x, norm_mix_g, w_in, ret_gn_g, ssm_a_re, ssm_a_im, ssm_log_dt, ssm_b_re, ssm_b_im, ssm_c_re, ssm_c_im, ssm_d, ssm_w_glu, ssm_b_glu, ssm_out_g, w_out, norm_ffn_g, w_gate, w_up, w_down, norm_final_g, loss_target, m_norm_mix_g, m_w_in, m_ret_gn_g, m_ssm_a_re, m_ssm_a_im, m_ssm_log_dt, m_ssm_b_re, m_ssm_b_im, m_ssm_c_re, m_ssm_c_im, m_ssm_d, m_ssm_w_glu, m_ssm_b_glu, m_ssm_out_g, m_w_out, m_norm_ffn_g, m_w_gate, m_w_up, m_w_down, m_norm_final_g, v_norm_mix_g, v_w_in, v_ret_gn_g, v_ssm_a_re, v_ssm_a_im, v_ssm_log_dt, v_ssm_b_re, v_ssm_b_im, v_ssm_c_re, v_ssm_c_im, v_ssm_d, v_ssm_w_glu, v_ssm_b_glu, v_ssm_out_g, v_w_out, v_norm_ffn_g, v_w_gate, v_w_up, v_w_down, v_norm_final_g):
    given = dict(locals())
    w = {n: given[n] for n in WEIGHTS}
    m = {n: given["m_" + n] for n in WEIGHTS}
    v = {n: given["v_" + n] for n in WEIGHTS}
    drop = lambda n, a: a if n == "norm_final_g" else a[0]
    w0 = {n: drop(n, w[n]) for n in WEIGHTS}
    m0 = {n: drop(n, m[n]) for n in WEIGHTS}
    v0 = {n: drop(n, v[n]) for n in WEIGHTS}

    sm = {n: w0[n] for n in SMALL}
    shards = {n: w0[n].astype(BF16) for n in BIG}
    loss_local, grad_x, landed, summed = _forward_backward(x[0], loss_target[0], shards, sm)
    loss = lax.psum(loss_local, MESH_AXES)
    gsmall = _small_grads(summed, sm)

    grads, delta, new_m, new_v = {}, {}, {}, {}
    for n in BIG:
        g, d, m2, v2 = _adamw_shard("adamw_" + n, landed[n], w0[n], m0[n], v0[n], ADAM_ROWS[n])
        grads[n], delta[n], new_m[n], new_v[n] = g, d, m2, v2
    as_given = lambda n, a: a.reshape(1, -1) if n == "norm_final_g" else a.reshape(w[n].shape)
    gs = [as_given(n, gsmall[n]) for n in SMALL]
    ds, m2s, v2s = _adamw_small([as_given(n, w[n]) for n in SMALL], gs, [as_given(n, m[n]) for n in SMALL],
                                [as_given(n, v[n]) for n in SMALL])
    for n, g, d, m2, v2 in zip(SMALL, gs, ds, m2s, v2s):
        grads[n], delta[n], new_m[n], new_v[n] = g, d, m2, v2

    lift = lambda n, a: a.reshape(w[n].shape)
    return (loss, grad_x[None], *[lift(n, grads[n]) for n in WEIGHTS], *[lift(n, delta[n]) for n in WEIGHTS],
            *[lift(n, new_m[n]) for n in WEIGHTS], *[lift(n, new_v[n]) for n in WEIGHTS])
```

```python
import functools
import math

import jax
import jax.numpy as jnp
from jax import lax
from jax.experimental import pallas as pl
from jax.experimental.pallas import tpu as pltpu

F32 = jnp.float32
BF16 = jnp.bfloat16

D_MODEL = 2048
RET_WIDTH = 1024
RET_HEADS = 8
HEAD_DIM = 128
CHUNK = 64
SSM_WIDTH = 1024
SSM_GROUP = 16
SSM_GROUPS = 64
SSM_STATE = 64
D_FF = 5632
IN_WIDTH = 5120
ROPE_BASE = 10000.0
EPS = 1e-6
N_DEV = 8
MESH_AXES = ("x", "y", "c")

WIN_BLK = IN_WIDTH // N_DEV
FF_BLK = D_FF // N_DEV
RET_BLOCK = 256
RET_HPS = 4
S5_TILE = 256
S5_CHUNKS = 8
S5_STEPS = S5_TILE // S5_CHUNKS
S5_GB = 8
S5_NBLK = SSM_GROUPS // S5_GB
S5_LANES = S5_GB * SSM_STATE
LANE = 128

ADAM_LR = 0.001
ADAM_B1 = 0.9
ADAM_B2 = 0.999
ADAM_EPS = 1e-08
ADAM_WD = 0.01
ADAM_STEP = 10
ADAM_BC1 = 1.0 - ADAM_B1 ** ADAM_STEP
ADAM_BC2 = 1.0 - ADAM_B2 ** ADAM_STEP

VMEM_LIMIT = 56 * 1024 * 1024

NT = (((1,), (1,)), ((), ()))
TN = (((0,), (0,)), ((), ()))


def _params(n_grid):
    return pltpu.CompilerParams(dimension_semantics=("arbitrary",) * n_grid, vmem_limit_bytes=VMEM_LIMIT)


def _dot(a, b):
    return jnp.dot(a, b, preferred_element_type=F32)


def _dot_nt(a, b):
    return lax.dot_general(a, b, NT, preferred_element_type=F32)


def _dot_tn(a, b):
    return lax.dot_general(a, b, TN, preferred_element_type=F32)


def _sigmoid(x):
    return 1.0 / (1.0 + jnp.exp(-x))


_GELU_C = math.sqrt(2.0 / math.pi)
_GELU_A = 0.044715


def _gelu(x):
    t = jnp.tanh(_GELU_C * (x + _GELU_A * x * x * x))
    return 0.5 * x * (1.0 + t)


def _gelu_and_grad(x):
    t = jnp.tanh(_GELU_C * (x + _GELU_A * x * x * x))
    g = 0.5 * (1.0 + t) + 0.5 * x * (1.0 - t * t) * _GELU_C * (1.0 + 3.0 * _GELU_A * x * x)
    return 0.5 * x * (1.0 + t), g


def _rms_bwd(dy, x, r, g):
    w = dy * g
    dx = r * w - x * (r * r * r) * jnp.mean(w * x, axis=-1, keepdims=True)
    return dx, dy * x * r


HBM_SPEC = pl.BlockSpec(memory_space=pltpu.HBM)
ANY_SPEC = pl.BlockSpec(memory_space=pl.ANY)


def _load_resident(src_hbm, dst_vmem, sem):
    cp = pltpu.make_async_copy(src_hbm, dst_vmem, sem)
    cp.start()
    cp.wait()


def _my_block():
    return 4 * lax.axis_index("x") + 2 * lax.axis_index("y") + lax.axis_index("c")


def _peer(k):
    px = lax.axis_index("x") ^ ((k >> 2) & 1)
    py = lax.axis_index("y") ^ ((k >> 1) & 1)
    pc = lax.axis_index("c") ^ (k & 1)
    return (px, py, pc), 4 * px + 2 * py + pc


class _Exchange:
    def __init__(self, payloads, gather, via_sibling=False):
        self.payloads = list(payloads)
        self.n = len(self.payloads)
        self.gather = [gather] * self.n if isinstance(gather, bool) else list(gather)
        self.via_sibling = via_sibling
        assert not via_sibling or all(self.gather)

    def out_shape(self):
        return [jax.ShapeDtypeStruct(((N_DEV,) if g else ()) + p.shape, p.dtype)
                for p, g in zip(self.payloads, self.gather)]

    def scratch_shapes(self):
        return [pltpu.SemaphoreType.DMA((self.n, N_DEV - 1)), pltpu.SemaphoreType.DMA((self.n, N_DEV - 1)),
                pltpu.SemaphoreType.DMA((self.n,))]

    def _copies(self, ins, outs, sems, incoming):
        send_sems, recv_sems, local_sems = sems
        me = _my_block()
        src_of = lambda i, blk: ins[i] if self.gather[i] else ins[i].at[blk]
        local, remote = [], []
        for i in range(self.n):
            if not incoming:
                local.append(pltpu.make_async_copy(src_of(i, me), outs[i].at[me], local_sems.at[i]))
            for k in range(1, N_DEV):
                dev, blk = _peer(k)
                src, dst = (outs[i].at[blk], outs[i].at[blk]) if incoming else (src_of(i, blk), outs[i].at[me])
                remote.append(pltpu.make_async_remote_copy(
                    src_ref=src, dst_ref=dst, send_sem=send_sems.at[i, k - 1], recv_sem=recv_sems.at[i, k - 1],
                    device_id=dev, device_id_type=pl.DeviceIdType.MESH))
        return local, remote

    def _copy(self, i, k, outs, sems, src, dst_blk, to_k):
        send_sems, recv_sems, _ = sems
        return pltpu.make_async_remote_copy(
            src_ref=src, dst_ref=outs[i].at[dst_blk], send_sem=send_sems.at[i, k - 1], recv_sem=recv_sems.at[i, k - 1],
            device_id=_peer(to_k)[0], device_id_type=pl.DeviceIdType.MESH)

    FIRST_HOPS = (1, 2, 4, 6)
    FROM_CHIPS = (2, 4, 6)

    def start(self, ins, outs, sems):
        if not self.via_sibling:
            local, sends = self._copies(ins, outs, sems, False)
            for cp in local + sends:
                cp.start()
            return
        me = _my_block()
        for i in range(self.n):
            pltpu.make_async_copy(ins[i], outs[i].at[me], sems[2].at[i]).start()
            for k in self.FIRST_HOPS:
                self._copy(i, k, outs, sems, ins[i], me, k).start()

    def wait(self, ins, outs, sems):
        if not self.via_sibling:
            for cp in self._copies(ins, outs, sems, True)[1]:
                cp.wait_recv()
            local, sends = self._copies(ins, outs, sems, False)
            for cp in sends:
                cp.wait_send()
            for cp in local:
                cp.wait()
            return
        me = _my_block()
        landed = lambda i, k: self._copy(i, k, outs, sems, outs[i].at[_peer(k)[1]], _peer(k)[1], k)
        for i in range(self.n):
            for s in self.FROM_CHIPS:
                landed(i, s).wait_recv()
                self._copy(i, s ^ 1, outs, sems, outs[i].at[_peer(s)[1]], _peer(s)[1], 1).start()
        for i in range(self.n):
            for k in (1, 3, 5, 7):
                landed(i, k).wait_recv()
            for k in self.FIRST_HOPS:
                self._copy(i, k, outs, sems, ins[i], me, k).wait_send()
            for s in self.FROM_CHIPS:
                self._copy(i, s ^ 1, outs, sems, outs[i].at[_peer(s)[1]], _peer(s)[1], 1).wait_send()
            pltpu.make_async_copy(ins[i], outs[i].at[me], sems[2].at[i]).wait()


def _pcall(body, name, grid, in_specs, out_specs, out_shape, scratch_shapes, args, carry=None):
    n_in, n_out, n_scr = len(in_specs), len(out_specs), len(scratch_shapes)
    if carry is None:
        return pl.pallas_call(body, name=name, grid=grid, in_specs=in_specs, out_specs=out_specs, out_shape=out_shape,
                              scratch_shapes=scratch_shapes, compiler_params=_params(len(grid)))(*args)
    nx = carry.n

    def wrapped(*refs):
        cin, xin = refs[:n_in], refs[n_in:n_in + nx]
        cout, xout = refs[n_in + nx:n_in + nx + n_out], refs[n_in + nx + n_out:n_in + 2 * nx + n_out]
        rest = refs[n_in + 2 * nx + n_out:]
        cscr, sems = rest[:n_scr], rest[n_scr:]
        first = functools.reduce(jnp.logical_and, [pl.program_id(a) == 0 for a in range(len(grid))])
        last = functools.reduce(jnp.logical_and, [pl.program_id(a) == grid[a] - 1 for a in range(len(grid))])

        @pl.when(first)
        def _():
            carry.start(xin, xout, sems)

        body(*cin, *cout, *cscr)

        @pl.when(last)
        def _():
            carry.wait(xin, xout, sems)

    return pl.pallas_call(
        wrapped, name=name, grid=grid, in_specs=list(in_specs) + [HBM_SPEC] * nx,
        out_specs=list(out_specs) + [HBM_SPEC] * nx, out_shape=list(out_shape) + carry.out_shape(),
        scratch_shapes=list(scratch_shapes) + carry.scratch_shapes(), compiler_params=_params(len(grid)),
    )(*args, *carry.payloads)


def _exchange_call(name, payloads, gather, via_sibling=False):
    ex = _Exchange(payloads, gather, via_sibling)

    def body(*refs):
        ins, outs, sems = refs[:ex.n], refs[ex.n:2 * ex.n], refs[2 * ex.n:]
        ex.start(ins, outs, sems)
        ex.wait(ins, outs, sems)

    return pl.pallas_call(body, name=name, in_specs=[HBM_SPEC] * ex.n, out_specs=[HBM_SPEC] * ex.n,
                          out_shape=ex.out_shape(), scratch_shapes=ex.scratch_shapes())(*ex.payloads)


def _in_proj_fwd(x, g, w, tm, carry=None):
    T = x.shape[0]

    def body(x_ref, g_ref, w_hbm, proj_ref, h_ref, r_ref, w_ref, sem):
        @pl.when(pl.program_id(0) == 0)
        def _():
            _load_resident(w_hbm, w_ref, sem)

        xf = x_ref[...]
        r = lax.rsqrt(jnp.mean(xf * xf, axis=-1, keepdims=True) + EPS)
        h = (xf * r * g_ref[...]).astype(BF16)
        h_ref[...] = h
        r_ref[...] = r
        for j in range(N_DEV):
            proj_ref[:, j * WIN_BLK:(j + 1) * WIN_BLK] = _dot(h, w_ref[j])

    return _pcall(
        body, "in_proj_fwd", (T // tm,),
        [pl.BlockSpec((tm, D_MODEL), lambda i: (i, 0)), pl.BlockSpec((1, D_MODEL), lambda i: (0, 0)), ANY_SPEC],
        [pl.BlockSpec((tm, IN_WIDTH), lambda i: (i, 0)),
         pl.BlockSpec((tm, D_MODEL), lambda i: (i, 0)),
         pl.BlockSpec((tm, 1), lambda i: (i, 0))],
        [jax.ShapeDtypeStruct((T, IN_WIDTH), F32),
         jax.ShapeDtypeStruct((T, D_MODEL), BF16),
         jax.ShapeDtypeStruct((T, 1), F32)],
        [pltpu.VMEM(w.shape, w.dtype), pltpu.SemaphoreType.DMA], (x, g, w), carry)


def _ret_common(q_ref, k_ref, v_ref, cos_ref, sin_ref, mask_ref, rd_ref, sin_state):
    c = cos_ref[...]
    s = sin_ref[...]
    q = q_ref[...]
    q = q * c + pltpu.roll(q, HEAD_DIM // 2, 1) * s
    k = k_ref[...]
    k = (k * c + pltpu.roll(k, HEAD_DIM // 2, 1) * s) * (HEAD_DIM ** -0.5)
    qb = q.astype(BF16)
    kb = k.astype(BF16)
    vb = v_ref[...].astype(BF16)
    pm = (_dot_nt(qb, kb) * mask_ref[...]).astype(BF16)
    qd = (q * rd_ref[...]).astype(BF16)
    o = _dot(pm, vb) + _dot(qd, sin_state.astype(BF16))
    return q, k, qb, kb, vb, pm, qd, o


def _ret_specs(T, rev):
    nb = T // RET_BLOCK
    groups = RET_HEADS // RET_HPS
    wide = RET_HPS * HEAD_DIM
    blk = (lambda b: nb - 1 - b) if rev else (lambda b: b)
    col = lambda piece: (pl.BlockSpec((RET_BLOCK, wide), lambda h, b: (blk(b), piece * groups + h)), "lane")
    return dict(
        q=col(0), k=col(1), v=col(2), g=col(3),
        tab=(pl.BlockSpec((RET_BLOCK, HEAD_DIM), lambda h, b: (blk(b), 0)), None),
        mask=(pl.BlockSpec((RET_HPS, RET_BLOCK, RET_BLOCK), lambda h, b: (h, 0, 0)), "lead"),
        dec=(pl.BlockSpec((RET_HPS, RET_BLOCK, HEAD_DIM), lambda h, b: (h, 0, 0)), "lead"),
        gtb=(pl.BlockSpec((RET_HPS, 1, HEAD_DIM), lambda h, b: (h, 0, 0)), "lead"),
        gn=(pl.BlockSpec((1, wide), lambda h, b: (0, h)), "lane"),
        state=(pl.BlockSpec((RET_HPS, None, HEAD_DIM, HEAD_DIM), lambda h, b: (h, blk(b), 0, 0)), "lead"),
        rows=(pl.BlockSpec((RET_BLOCK, wide), lambda h, b: (blk(b), h)), "lane"),
        scratch=(pltpu.VMEM((RET_HPS, HEAD_DIM, HEAD_DIM), F32), "lead"),
    )


def _per_head(head_body, kinds):
    def body(*refs):
        for hh in range(RET_HPS):
            views = []
            for ref, kind in zip(refs, kinds):
                if kind == "lane":
                    views.append(ref.at[:, hh * HEAD_DIM:(hh + 1) * HEAD_DIM])
                elif kind == "lead":
                    views.append(ref.at[hh])
                else:
                    views.append(ref)
            head_body(*views)
    return body


def _ret_fwd(proj, cosf, sinf, mask, rowdec, kdec, gtb, gn, carry=None):
    T = proj.shape[0]
    nb = T // RET_BLOCK
    sp = _ret_specs(T, False)

    def body(q_ref, k_ref, v_ref, g_ref, cos_ref, sin_ref, mask_ref, rd_ref, kd_ref, gtb_ref, gn_ref,
             y_ref, sb_ref, st):
        @pl.when(pl.program_id(1) == 0)
        def _():
            st[...] = jnp.zeros_like(st)
        s_in = st[...]
        sb_ref[...] = s_in
        q, k, qb, kb, vb, pm, qd, o = _ret_common(q_ref, k_ref, v_ref, cos_ref, sin_ref, mask_ref, rd_ref, s_in)
        st[...] = gtb_ref[...] * s_in + _dot_tn((k * kd_ref[...]).astype(BF16), vb)
        mu = jnp.mean(o, axis=-1, keepdims=True)
        oc = o - mu
        n = oc * lax.rsqrt(jnp.mean(oc * oc, axis=-1, keepdims=True) + EPS)
        gt = g_ref[...]
        y_ref[...] = (gt * _sigmoid(gt) * (n * gn_ref[...])).astype(BF16)

    ins = [sp[n] for n in ("q", "k", "v", "g", "tab", "tab", "mask", "dec", "dec", "gtb", "gn")]
    outs = [sp["rows"], sp["state"]]
    return _pcall(
        _per_head(body, [kind for _, kind in ins + outs + [sp["scratch"]]]), "ret_fwd", (RET_HEADS // RET_HPS, nb),
        [s for s, _ in ins], [s for s, _ in outs],
        [jax.ShapeDtypeStruct((T, RET_WIDTH), BF16),
         jax.ShapeDtypeStruct((RET_HEADS, nb, HEAD_DIM, HEAD_DIM), F32)],
        [sp["scratch"][0]],
        (proj, proj, proj, proj, cosf, sinf, mask, rowdec, kdec, gtb, gn), carry)


def _scan(re, im, ar, ai, reverse):
    n = re.shape[0]
    row = lax.broadcasted_iota(jnp.int32, re.shape, 0)
    s = 1
    while s < n:
        if reverse:
            keep = row < n - s
            sr = jnp.where(keep, pltpu.roll(re, n - s, 0), 0.0)
            si = jnp.where(keep, pltpu.roll(im, n - s, 0), 0.0)
        else:
            keep = row >= s
            sr = jnp.where(keep, pltpu.roll(re, s, 0), 0.0)
            si = jnp.where(keep, pltpu.roll(im, s, 0), 0.0)
        re, im = re + ar * sr - ai * si, im + ar * si + ai * sr
        ar, ai = ar * ar - ai * ai, 2.0 * ar * ai
        s *= 2
    return re, im


S5_STATE_TILE = (S5_TILE, S5_LANES)


def _step_major_permutation():
    r = jnp.arange(S5_TILE)
    t_of_row = (r % S5_CHUNKS) * S5_STEPS + r // S5_CHUNKS
    return (t_of_row[:, None] == r[None, :]).astype(BF16)


def _permute_rows_f32(pm, x):
    hi = x.astype(BF16)
    rest = x - hi.astype(F32)
    mid = rest.astype(BF16)
    lo = (rest - mid.astype(F32)).astype(BF16)
    return _dot(pm, hi) + _dot(pm, mid) + _dot(pm, lo)


def _step_get(ref, j):
    return ref[j * S5_CHUNKS:(j + 1) * S5_CHUNKS, :]


def _step_set(ref, j, val):
    ref[j * S5_CHUNKS:(j + 1) * S5_CHUNKS, :] = val


def _tile_get(ref):
    return ref[...]


def _tile_set(ref, val):
    ref[...] = val


def _fill_power_table(ptab, lr, li):
    shape = (S5_CHUNKS, S5_LANES)
    lrb = jnp.broadcast_to(lr, shape)
    lib = jnp.broadcast_to(li, shape)
    pr, pi_ = lrb, lib
    for j in range(S5_STEPS):
        ptab[0, j * S5_CHUNKS:(j + 1) * S5_CHUNKS, :] = pr
        ptab[1, j * S5_CHUNKS:(j + 1) * S5_CHUNKS, :] = pi_
        pr, pi_ = lrb * pr - lib * pi_, lrb * pi_ + lib * pr


def _chunk_scans(xr, xi, lr, li, reverse):
    shape = (S5_CHUNKS, S5_LANES)
    lrb = jnp.broadcast_to(lr, shape)
    lib = jnp.broadcast_to(li, shape)
    sr = si = None
    for j in (range(S5_STEPS - 1, -1, -1) if reverse else range(S5_STEPS)):
        vr = _step_get(xr, j)
        vi = _step_get(xi, j)
        if sr is not None:
            vr, vi = vr + lrb * sr - lib * si, vi + lrb * si + lib * sr
            _step_set(xr, j, vr)
            _step_set(xi, j, vi)
        sr, si = vr, vi
    return sr, si


def _entering_states(zr, zi, cr, ci, ar, ai, reverse):
    shape = (S5_CHUNKS, S5_LANES)
    row = lax.broadcasted_iota(jnp.int32, shape, 0)
    if reverse:
        edge, shift = row == S5_CHUNKS - 1, S5_CHUNKS - 1
    else:
        edge, shift = row == 0, 1
    wr = jnp.where(edge, jnp.broadcast_to(cr, shape), pltpu.roll(zr, shift, 0))
    wi = jnp.where(edge, jnp.broadcast_to(ci, shape), pltpu.roll(zi, shift, 0))
    return _scan(wr, wi, ar, ai, reverse)


def _table_rows(ptab, j, conj):
    pr = ptab[0, j * S5_CHUNKS:(j + 1) * S5_CHUNKS, :]
    pi_ = ptab[1, j * S5_CHUNKS:(j + 1) * S5_CHUNKS, :]
    return pr, (-pi_ if conj else pi_)


def _s5_forward_states(xr, xi, lr, li, cr, ci, ptab):
    zr, zi = _chunk_scans(xr, xi, lr, li, False)
    ar, ai = _table_rows(ptab, S5_STEPS - 1, False)
    er, ei = _entering_states(zr, zi, cr, ci, ar, ai, False)
    for j in range(S5_STEPS):
        pr, pi_ = _table_rows(ptab, j, False)
        _step_set(xr, j, _step_get(xr, j) + pr * er - pi_ * ei)
        _step_set(xi, j, _step_get(xi, j) + pr * ei + pi_ * er)
    last = S5_CHUNKS - 1
    end_r = (ar * er - ai * ei + zr)[last:last + 1, :]
    end_i = (ar * ei + ai * er + zi)[last:last + 1, :]
    return er, ei, end_r, end_i


def _s5_specs(T, rev):
    nt = T // S5_TILE
    tt = (lambda t: nt - 1 - t) if rev else (lambda t: t)
    return dict(
        u=pl.BlockSpec((S5_TILE, LANE), lambda b, t: (tt(t), 4 * RET_HEADS + b)),
        rows=pl.BlockSpec((S5_TILE, LANE), lambda b, t: (tt(t), b)),
        to_state=pl.BlockSpec((None, LANE, S5_LANES), lambda b, t: (b, 0, 0)),
        from_state=pl.BlockSpec((None, S5_LANES, LANE), lambda b, t: (b, 0, 0)),
        lam=pl.BlockSpec((None, 2, S5_LANES), lambda b, t: (b, 0, 0)),
        d=pl.BlockSpec((1, LANE), lambda b, t: (0, b)),
        perm=pl.BlockSpec((S5_TILE, S5_TILE), lambda b, t: (0, 0)),
        bound=pl.BlockSpec((None, None, 2, S5_LANES), lambda b, t: (b, tt(t), 0, 0)),
    )


def _s5_fwd(proj, pm, pm_t, bre, bim, cre_t, cim_t, lam, d, carry=None):
    T = proj.shape[0]
    nt = T // S5_TILE
    sp = _s5_specs(T, False)

    def body(u_ref, pm_ref, pmt_ref, bre_ref, bim_ref, cre_ref, cim_ref, lam_ref, d_ref, y_ref, bound_ref,
             carry, ptab, xr, xi):
        lr = lam_ref[0:1, :]
        li = lam_ref[1:2, :]

        @pl.when(pl.program_id(1) == 0)
        def _():
            carry[...] = jnp.zeros_like(carry)
            _fill_power_table(ptab, lr, li)

        u = _permute_rows_f32(pm_ref[...], u_ref[...])
        ub = u.astype(BF16)
        _tile_set(xr, _dot(ub, bre_ref[...]))
        _tile_set(xi, _dot(ub, bim_ref[...]))
        bound_ref[...] = carry[...]
        _, _, end_r, end_i = _s5_forward_states(xr, xi, lr, li, carry[0:1, :], carry[1:2, :], ptab)
        carry[0:1, :] = end_r
        carry[1:2, :] = end_i
        y = (_dot(_tile_get(xr).astype(BF16), cre_ref[...]) - _dot(_tile_get(xi).astype(BF16), cim_ref[...])
             + d_ref[...] * u)
        y_ref[...] = _permute_rows_f32(pmt_ref[...], y)

    state = pltpu.VMEM(S5_STATE_TILE, F32)
    return _pcall(
        body, "s5_fwd", (S5_NBLK, nt),
        [sp["u"], sp["perm"], sp["perm"], sp["to_state"], sp["to_state"], sp["from_state"],
         sp["from_state"], sp["lam"], sp["d"]],
        [sp["rows"], sp["bound"]],
        [jax.ShapeDtypeStruct((T, SSM_WIDTH), F32),
         jax.ShapeDtypeStruct((S5_NBLK, nt, 2, S5_LANES), F32)],
        [pltpu.VMEM((2, S5_LANES), F32), pltpu.VMEM((2, S5_TILE, S5_LANES), F32), state, state],
        (proj, pm, pm_t, bre, bim, cre_t, cim_t, lam, d), carry)


def _glu_fwd(y, w, b, og, tm):
    T = y.shape[0]

    def body(y_ref, w_ref, b_ref, og_ref, z_ref, o_ref, r_ref):
        y1 = _gelu(y_ref[...])
        z = _dot(y1.astype(BF16), w_ref[...]) + b_ref[...]
        y2 = y1 * _sigmoid(z)
        r = lax.rsqrt(jnp.mean(y2 * y2, axis=-1, keepdims=True) + EPS)
        z_ref[...] = z
        o_ref[...] = (y2 * r * og_ref[...]).astype(BF16)
        r_ref[...] = r

    row = pl.BlockSpec((tm, SSM_WIDTH), lambda i: (i, 0))
    vec = pl.BlockSpec((1, SSM_WIDTH), lambda i: (0, 0))
    return pl.pallas_call(
        body, name="glu_fwd", grid=(T // tm,),
        in_specs=[row, pl.BlockSpec((SSM_WIDTH, SSM_WIDTH), lambda i: (0, 0)), vec, vec],
        out_specs=[row, row, pl.BlockSpec((tm, 1), lambda i: (i, 0))],
        out_shape=[jax.ShapeDtypeStruct((T, SSM_WIDTH), F32), jax.ShapeDtypeStruct((T, SSM_WIDTH), BF16),
                   jax.ShapeDtypeStruct((T, 1), F32)],
        compiler_params=_params(1),
    )(y, w, b, og)


def _out_proj_fwd(x, y_ret, y_ssm, w, g, tm):
    T = x.shape[0]

    def body(x_ref, a_ref, b_ref, w_ref, g_ref, x2_ref, h_ref, r_ref):
        x2 = x_ref[...] + _dot(a_ref[...], w_ref[0:RET_WIDTH, :]) + _dot(b_ref[...], w_ref[RET_WIDTH:D_MODEL, :])
        r = lax.rsqrt(jnp.mean(x2 * x2, axis=-1, keepdims=True) + EPS)
        x2_ref[...] = x2
        h_ref[...] = (x2 * r * g_ref[...]).astype(BF16)
        r_ref[...] = r

    full = pl.BlockSpec((tm, D_MODEL), lambda i: (i, 0))
    half = pl.BlockSpec((tm, RET_WIDTH), lambda i: (i, 0))
    return pl.pallas_call(
        body, name="out_proj_fwd", grid=(T // tm,),
        in_specs=[full, half, half, pl.BlockSpec((D_MODEL, D_MODEL), lambda i: (0, 0)),
                  pl.BlockSpec((1, D_MODEL), lambda i: (0, 0))],
        out_specs=[full, full, pl.BlockSpec((tm, 1), lambda i: (i, 0))],
        out_shape=[jax.ShapeDtypeStruct((T, D_MODEL), F32), jax.ShapeDtypeStruct((T, D_MODEL), BF16),
                   jax.ShapeDtypeStruct((T, 1), F32)],
        compiler_params=_params(1),
    )(x, y_ret, y_ssm, w, g)


def _ffn_up(h, wg, wu, tm, carry=None):
    T = h.shape[0]

    def body(h_ref, wg_ref, wu_ref, a_ref, b_ref, f_ref):
        hb = h_ref[...]
        a = _dot(hb, wg_ref[...])
        b = _dot(hb, wu_ref[...])
        a_ref[...] = a.astype(BF16)
        b_ref[...] = b.astype(BF16)
        f_ref[...] = (a * _sigmoid(a) * b).astype(BF16)

    wspec = pl.BlockSpec((None, D_MODEL, FF_BLK), lambda j, i: (j, 0, 0))
    ospec = pl.BlockSpec((None, tm, FF_BLK), lambda j, i: (j, i, 0))
    oshape = jax.ShapeDtypeStruct((N_DEV, T, FF_BLK), BF16)
    return _pcall(
        body, "ffn_up", (N_DEV, T // tm),
        [pl.BlockSpec((tm, D_MODEL), lambda j, i: (i, 0)), wspec, wspec],
        [ospec, ospec, ospec], [oshape, oshape, oshape], [], (h, wg, wu), carry)


def _ffn_down_loss(f, wd, x2, tgt, g, tm):
    T = x2.shape[0]

    def body(f_ref, w_hbm, x2_ref, t_ref, g_ref, dx_ref, dxb_ref, loss_ref, dg_ref, w_ref, sem):
        i = pl.program_id(0)

        @pl.when(i == 0)
        def _():
            _load_resident(w_hbm, w_ref, sem)
            loss_ref[...] = jnp.zeros_like(loss_ref)
            dg_ref[...] = jnp.zeros_like(dg_ref)

        gv = g_ref[...]
        x3 = x2_ref[...]
        for k in range(N_DEV):
            x3 = x3 + _dot(f_ref[k], w_ref[k])
        r = lax.rsqrt(jnp.mean(x3 * x3, axis=-1, keepdims=True) + EPS)
        err = x3 * r * gv - t_ref[...]
        tile_loss = 0.5 * jnp.sum(jnp.mean(err * err, axis=-1, keepdims=True), axis=0, keepdims=True)
        dx, dgt = _rms_bwd(err * (1.0 / D_MODEL), x3, r, gv)
        dx_ref[...] = dx
        dxb_ref[...] = dx.astype(BF16)
        loss_ref[...] += jnp.broadcast_to(tile_loss, loss_ref.shape)
        dg_ref[...] += jnp.sum(dgt, axis=0, keepdims=True)

    full = pl.BlockSpec((tm, D_MODEL), lambda i: (i, 0))
    vec = pl.BlockSpec((1, D_MODEL), lambda i: (0, 0))
    return pl.pallas_call(
        body, name="ffn_down_loss", grid=(T // tm,),
        in_specs=[pl.BlockSpec((N_DEV, tm, FF_BLK), lambda i: (0, i, 0)), ANY_SPEC, full, full, vec],
        out_specs=[full, full, pl.BlockSpec((8, LANE), lambda i: (0, 0)), vec],
        out_shape=[jax.ShapeDtypeStruct((T, D_MODEL), F32), jax.ShapeDtypeStruct((T, D_MODEL), BF16),
                   jax.ShapeDtypeStruct((8, LANE), F32), jax.ShapeDtypeStruct((1, D_MODEL), F32)],
        scratch_shapes=[pltpu.VMEM(wd.shape, wd.dtype), pltpu.SemaphoreType.DMA],
        compiler_params=_params(1),
    )(f, wd, x2, tgt, g)


def _ffn_bwd_act(dxb, wd, a, b, tm):
    T = dxb.shape[0]

    def body(dx_ref, w_ref, a_ref, b_ref, da_ref, db_ref):
        df = _dot_nt(dx_ref[...], w_ref[...])
        a = a_ref[...].astype(F32)
        b = b_ref[...].astype(F32)
        sg = _sigmoid(a)
        da_ref[...] = (df * b * sg * (1.0 + a * (1.0 - sg))).astype(BF16)
        db_ref[...] = (df * a * sg).astype(BF16)

    blk = pl.BlockSpec((None, tm, FF_BLK), lambda j, i: (j, i, 0))
    oshape = jax.ShapeDtypeStruct((N_DEV, T, FF_BLK), BF16)
    return pl.pallas_call(
        body, name="ffn_bwd_act", grid=(N_DEV, T // tm),
        in_specs=[pl.BlockSpec((tm, D_MODEL), lambda j, i: (i, 0)),
                  pl.BlockSpec((None, FF_BLK, D_MODEL), lambda j, i: (j, 0, 0)), blk, blk],
        out_specs=[blk, blk], out_shape=[oshape, oshape],
        compiler_params=_params(2),
    )(dxb, wd, a, b)


def _ffn_bwd_in(da, db, wg, wu, tm, carry=None):
    T = da.shape[1]

    def body(da_ref, db_ref, wg_ref, wu_ref, dh_ref):
        part = _dot_nt(da_ref[...], wg_ref[...]) + _dot_nt(db_ref[...], wu_ref[...])

        @pl.when(pl.program_id(1) == 0)
        def _():
            dh_ref[...] = part

        @pl.when(pl.program_id(1) > 0)
        def _():
            dh_ref[...] += part

    ablk = pl.BlockSpec((None, tm, FF_BLK), lambda i, k: (k, i, 0))
    wblk = pl.BlockSpec((None, D_MODEL, FF_BLK), lambda i, k: (k, 0, 0))
    return _pcall(
        body, "ffn_bwd_in", (T // tm, N_DEV), [ablk, ablk, wblk, wblk],
        [pl.BlockSpec((tm, D_MODEL), lambda i, k: (i, 0))], [jax.ShapeDtypeStruct((T, D_MODEL), F32)],
        [], (da, db, wg, wu), carry)


def _ffn_wgrad_up(h, da, db, tk, carry=None):
    T = h.shape[0]
    nk = T // tk

    def body(h_ref, da_ref, db_ref, g_ref, u_ref, accg, accu):
        k = pl.program_id(1)

        @pl.when(k == 0)
        def _():
            accg[...] = jnp.zeros_like(accg)
            accu[...] = jnp.zeros_like(accu)

        hb = h_ref[...]
        accg[...] += _dot_tn(hb, da_ref[...])
        accu[...] += _dot_tn(hb, db_ref[...])

        @pl.when(k == nk - 1)
        def _():
            g_ref[...] = accg[...].astype(BF16)
            u_ref[...] = accu[...].astype(BF16)

    blk = pl.BlockSpec((None, tk, FF_BLK), lambda j, k: (j, k, 0))
    ospec = pl.BlockSpec((None, D_MODEL, FF_BLK), lambda j, k: (j, 0, 0))
    oshape = jax.ShapeDtypeStruct((N_DEV, D_MODEL, FF_BLK), BF16)
    return _pcall(
        body, "ffn_wgrad_up", (N_DEV, nk),
        [pl.BlockSpec((tk, D_MODEL), lambda j, k: (k, 0)), blk, blk],
        [ospec, ospec], [oshape, oshape],
        [pltpu.VMEM((D_MODEL, FF_BLK), F32), pltpu.VMEM((D_MODEL, FF_BLK), F32)], (h, da, db), carry)


def _ffn_wgrad_down(f, dxb, tk):
    T = dxb.shape[0]
    nk = T // tk

    def body(f_ref, dx_ref, o_ref, acc):
        k = pl.program_id(1)

        @pl.when(k == 0)
        def _():
            acc[...] = jnp.zeros_like(acc)

        acc[...] += _dot_tn(f_ref[...], dx_ref[...])

        @pl.when(k == nk - 1)
        def _():
            o_ref[...] = acc[...].astype(BF16)

    return pl.pallas_call(
        body, name="ffn_wgrad_down", grid=(N_DEV, nk),
        in_specs=[pl.BlockSpec((None, tk, FF_BLK), lambda j, k: (j, k, 0)),
                  pl.BlockSpec((tk, D_MODEL), lambda j, k: (k, 0))],
        out_specs=pl.BlockSpec((None, FF_BLK, D_MODEL), lambda j, k: (j, 0, 0)),
        out_shape=jax.ShapeDtypeStruct((N_DEV, FF_BLK, D_MODEL), BF16),
        scratch_shapes=[pltpu.VMEM((FF_BLK, D_MODEL), F32)],
        compiler_params=_params(2),
    )(f, dxb)


def _out_proj_bwd(dh2, x2, r2, g, dx3, w, tm):
    T = x2.shape[0]

    def body(dh_ref, x_ref, r_ref, g_ref, dx3_ref, w_ref, dx_ref, dxb_ref, dg_ref, a_ref, b_ref):
        @pl.when(pl.program_id(0) == 0)
        def _():
            dg_ref[...] = jnp.zeros_like(dg_ref)

        dxn, dgt = _rms_bwd(dh_ref[...], x_ref[...], r_ref[...], g_ref[...])
        dx = dx3_ref[...] + dxn
        dxv = dx.astype(BF16)
        dx_ref[...] = dx
        dxb_ref[...] = dxv
        dg_ref[...] += jnp.sum(dgt, axis=0, keepdims=True)
        a_ref[...] = _dot_nt(dxv, w_ref[0:RET_WIDTH, :])
        b_ref[...] = _dot_nt(dxv, w_ref[RET_WIDTH:D_MODEL, :])

    full = pl.BlockSpec((tm, D_MODEL), lambda i: (i, 0))
    vec = pl.BlockSpec((1, D_MODEL), lambda i: (0, 0))
    half = pl.BlockSpec((tm, RET_WIDTH), lambda i: (i, 0))
    hshape = jax.ShapeDtypeStruct((T, RET_WIDTH), F32)
    return pl.pallas_call(
        body, name="out_proj_bwd", grid=(T // tm,),
        in_specs=[full, full, pl.BlockSpec((tm, 1), lambda i: (i, 0)), vec, full,
                  pl.BlockSpec((D_MODEL, D_MODEL), lambda i: (0, 0))],
        out_specs=[full, full, vec, half, half],
        out_shape=[jax.ShapeDtypeStruct((T, D_MODEL), F32), jax.ShapeDtypeStruct((T, D_MODEL), BF16),
                   jax.ShapeDtypeStruct((1, D_MODEL), F32), hshape, hshape],
        compiler_params=_params(1),
    )(dh2, x2, r2, g, dx3, w)


def _wgrad_rows(name, a, b, tk):
    T, M = a.shape
    N = b.shape[1]
    nk = T // tk

    def body(a_ref, b_ref, o_ref, acc):
        k = pl.program_id(0)

        @pl.when(k == 0)
        def _():
            acc[...] = jnp.zeros_like(acc)

        acc[...] += _dot_tn(a_ref[...], b_ref[...])

        @pl.when(k == nk - 1)
        def _():
            o_ref[...] = acc[...].astype(BF16)

    return pl.pallas_call(
        body, name=name, grid=(nk,),
        in_specs=[pl.BlockSpec((tk, M), lambda k: (k, 0)), pl.BlockSpec((tk, N), lambda k: (k, 0))],
        out_specs=pl.BlockSpec((M, N), lambda k: (0, 0)),
        out_shape=jax.ShapeDtypeStruct((M, N), BF16),
        scratch_shapes=[pltpu.VMEM((M, N), F32)],
        compiler_params=_params(1),
    )(a, b)


def _glu_bwd(y, z, r, dyo, w, og, tm):
    T = y.shape[0]

    def body(y_ref, z_ref, r_ref, d_ref, w_ref, og_ref, dy_ref, dw_ref, db_ref, dog_ref):
        @pl.when(pl.program_id(0) == 0)
        def _():
            dw_ref[...] = jnp.zeros_like(dw_ref)
            db_ref[...] = jnp.zeros_like(db_ref)
            dog_ref[...] = jnp.zeros_like(dog_ref)

        y1, g1 = _gelu_and_grad(y_ref[...])
        sg = _sigmoid(z_ref[...])
        y2 = y1 * sg
        dy2, dogt = _rms_bwd(d_ref[...], y2, r_ref[...], og_ref[...])
        dog_ref[...] += jnp.sum(dogt, axis=0, keepdims=True)
        dz = dy2 * y1 * sg * (1.0 - sg)
        db_ref[...] += jnp.sum(dz, axis=0, keepdims=True)
        dzb = dz.astype(BF16)
        dw_ref[...] += _dot_tn(y1.astype(BF16), dzb)
        dy_ref[...] = (dy2 * sg + _dot_nt(dzb, w_ref[...])) * g1

    row = pl.BlockSpec((tm, SSM_WIDTH), lambda i: (i, 0))
    vec = pl.BlockSpec((1, SSM_WIDTH), lambda i: (0, 0))
    sq = pl.BlockSpec((SSM_WIDTH, SSM_WIDTH), lambda i: (0, 0))
    return pl.pallas_call(
        body, name="glu_bwd", grid=(T // tm,),
        in_specs=[row, row, pl.BlockSpec((tm, 1), lambda i: (i, 0)), row, sq, vec],
        out_specs=[row, sq, vec, vec],
        out_shape=[jax.ShapeDtypeStruct((T, SSM_WIDTH), F32), jax.ShapeDtypeStruct((SSM_WIDTH, SSM_WIDTH), F32),
                   jax.ShapeDtypeStruct((1, SSM_WIDTH), F32), jax.ShapeDtypeStruct((1, SSM_WIDTH), F32)],
        compiler_params=_params(1),
    )(y, z, r, dyo, w, og)


def _s5_bwd(proj, dy, bound, pm, pm_t, bre, bim, bre_t, bim_t, cre, cim, lam, d, carry=None):
    T = proj.shape[0]
    nt = T // S5_TILE
    sp = _s5_specs(T, True)

    def body(u_ref, dy_ref, bound_ref, pm_ref, pmt_ref, bre_ref, bim_ref, bret_ref, bimt_ref, cre_ref, cim_ref,
             lam_ref, d_ref,
             du_ref, dbre_ref, dbim_ref, dcre_ref, dcim_ref, dlam_ref, dd_ref, carry, ptab, sr, si, gr, gi):
        lr = lam_ref[0:1, :]
        li = lam_ref[1:2, :]

        @pl.when(pl.program_id(1) == 0)
        def _():
            carry[...] = jnp.zeros_like(carry)
            _fill_power_table(ptab, lr, li)
            for ref in (dbre_ref, dbim_ref, dcre_ref, dcim_ref, dlam_ref, dd_ref):
                ref[...] = jnp.zeros_like(ref)

        u = _permute_rows_f32(pm_ref[...], u_ref[...])
        ub = u.astype(BF16)
        dyv = _permute_rows_f32(pm_ref[...], dy_ref[...])
        dyb = dyv.astype(BF16)
        _tile_set(sr, _dot(ub, bre_ref[...]))
        _tile_set(si, _dot(ub, bim_ref[...]))
        er, ei, _, _ = _s5_forward_states(sr, si, lr, li, bound_ref[0:1, :], bound_ref[1:2, :], ptab)
        _tile_set(gr, _dot(dyb, cre_ref[...]))
        _tile_set(gi, -_dot(dyb, cim_ref[...]))
        zr, zi = _chunk_scans(gr, gi, lr, -li, True)
        ar, ai = _table_rows(ptab, S5_STEPS - 1, True)
        fr, fi = _entering_states(zr, zi, carry[0:1, :], carry[1:2, :], ar, ai, True)
        acc_r = jnp.zeros((S5_CHUNKS, S5_LANES), F32)
        acc_i = jnp.zeros((S5_CHUNKS, S5_LANES), F32)
        for j in range(S5_STEPS):
            qr, qi = _table_rows(ptab, S5_STEPS - 1 - j, True)
            g_r = _step_get(gr, j) + qr * fr - qi * fi
            g_i = _step_get(gi, j) + qr * fi + qi * fr
            _step_set(gr, j, g_r)
            _step_set(gi, j, g_i)
            p_r, p_i = (er, ei) if j == 0 else (_step_get(sr, j - 1), _step_get(si, j - 1))
            acc_r += g_r * p_r + g_i * p_i
            acc_i += g_i * p_r - g_r * p_i
        dlam_ref[0:1, :] += jnp.sum(acc_r, axis=0, keepdims=True)
        dlam_ref[1:2, :] += jnp.sum(acc_i, axis=0, keepdims=True)
        g_all_r = _tile_get(gr)
        g_all_i = _tile_get(gi)
        carry[0:1, :] = g_all_r[0:1, :]
        carry[1:2, :] = g_all_i[0:1, :]
        grb = g_all_r.astype(BF16)
        gib = g_all_i.astype(BF16)
        du = (_dot(grb, bret_ref[...]) + _dot(gib, bimt_ref[...]) + d_ref[...] * dyv).astype(BF16)
        du_ref[...] = _dot(pmt_ref[...], du).astype(BF16)
        dbre_ref[...] += _dot_tn(grb, ub)
        dbim_ref[...] += _dot_tn(gib, ub)
        dcre_ref[...] += _dot_tn(dyb, _tile_get(sr).astype(BF16))
        dcim_ref[...] -= _dot_tn(dyb, _tile_get(si).astype(BF16))
        dd_ref[...] += jnp.sum(dyv * u, axis=0, keepdims=True)

    acc_ts = pl.BlockSpec((None, S5_LANES, LANE), lambda b, t: (b, 0, 0))
    acc_fs = pl.BlockSpec((None, LANE, S5_LANES), lambda b, t: (b, 0, 0))
    return _pcall(
        body, "s5_bwd", (S5_NBLK, nt),
        [sp["u"], sp["rows"], sp["bound"], sp["perm"], sp["perm"], sp["to_state"], sp["to_state"],
         sp["from_state"], sp["from_state"], sp["to_state"], sp["to_state"], sp["lam"], sp["d"]],
        [sp["rows"], acc_ts, acc_ts, acc_fs, acc_fs, sp["lam"], sp["d"]],
        [jax.ShapeDtypeStruct((T, SSM_WIDTH), BF16),
         jax.ShapeDtypeStruct((S5_NBLK, S5_LANES, LANE), F32),
         jax.ShapeDtypeStruct((S5_NBLK, S5_LANES, LANE), F32),
         jax.ShapeDtypeStruct((S5_NBLK, LANE, S5_LANES), F32),
         jax.ShapeDtypeStruct((S5_NBLK, LANE, S5_LANES), F32),
         jax.ShapeDtypeStruct((S5_NBLK, 2, S5_LANES), F32),
         jax.ShapeDtypeStruct((1, SSM_WIDTH), F32)],
        [pltpu.VMEM((2, S5_LANES), F32), pltpu.VMEM((2, S5_TILE, S5_LANES), F32)]
        + [pltpu.VMEM(S5_STATE_TILE, F32)] * 4,
        (proj, dy, bound, pm, pm_t, bre, bim, bre_t, bim_t, cre, cim, lam, d), carry)


def _ret_bwd(proj, cosf, sinf, mask, rowdec, kdec, gtb, gn, sblk, dyr):
    T = proj.shape[0]
    nb = T // RET_BLOCK
    sp = _ret_specs(T, True)

    def body(q_ref, k_ref, v_ref, g_ref, cos_ref, sin_ref, mask_ref, rd_ref, kd_ref, gtb_ref, gn_ref, sb_ref, dy_ref,
             dq_ref, dk_ref, dv_ref, dg_ref, dgn_ref, dst):
        @pl.when(pl.program_id(1) == 0)
        def _():
            dst[...] = jnp.zeros_like(dst)
            dgn_ref[...] = jnp.zeros_like(dgn_ref)

        s_in = sb_ref[...]
        q, k, qb, kb, vb, pm, qd, o = _ret_common(q_ref, k_ref, v_ref, cos_ref, sin_ref, mask_ref, rd_ref, s_in)
        mu = jnp.mean(o, axis=-1, keepdims=True)
        oc = o - mu
        rstd = lax.rsqrt(jnp.mean(oc * oc, axis=-1, keepdims=True) + EPS)
        n = oc * rstd
        gt = g_ref[...]
        sg = _sigmoid(gt)
        sil = gt * sg
        gnv = gn_ref[...]
        dyv = dy_ref[...]
        dg_ref[...] = (dyv * (n * gnv) * (sg * (1.0 + gt * (1.0 - sg)))).astype(BF16)
        dgn_ref[...] += jnp.sum(dyv * sil * n, axis=0, keepdims=True)
        dn = dyv * sil * gnv
        do = rstd * (dn - jnp.mean(dn, axis=-1, keepdims=True) - n * jnp.mean(dn * n, axis=-1, keepdims=True))
        dob = do.astype(BF16)
        ds = dst[...]
        dsb = ds.astype(BF16)
        kd = kd_ref[...]
        rd = rd_ref[...]
        dv_ref[...] = (_dot_tn(pm, dob) + _dot((k * kd).astype(BF16), dsb)).astype(BF16)
        dpb = (_dot_nt(dob, vb) * mask_ref[...]).astype(BF16)
        dq = _dot(dpb, kb) + _dot_nt(dob, s_in.astype(BF16)) * rd
        dk = (_dot_tn(dpb, qb) + _dot_nt(vb, dsb) * kd) * (HEAD_DIM ** -0.5)
        dst[...] = gtb_ref[...] * ds + _dot_tn(qd, dob)
        c = cos_ref[...]
        s = sin_ref[...]
        dq_ref[...] = (dq * c + pltpu.roll(dq * s, HEAD_DIM // 2, 1)).astype(BF16)
        dk_ref[...] = (dk * c + pltpu.roll(dk * s, HEAD_DIM // 2, 1)).astype(BF16)

    oshape = jax.ShapeDtypeStruct((T, RET_WIDTH), BF16)
    ins = [sp[n] for n in ("q", "k", "v", "g", "tab", "tab", "mask", "dec", "dec", "gtb", "gn", "state", "rows")]
    outs = [sp["rows"], sp["rows"], sp["rows"], sp["rows"], sp["gn"]]
    return pl.pallas_call(
        _per_head(body, [kind for _, kind in ins + outs + [sp["scratch"]]]), name="ret_bwd",
        grid=(RET_HEADS // RET_HPS, nb), in_specs=[s for s, _ in ins], out_specs=[s for s, _ in outs],
        out_shape=[oshape, oshape, oshape, oshape, jax.ShapeDtypeStruct((1, RET_WIDTH), F32)],
        scratch_shapes=[sp["scratch"][0]],
        compiler_params=_params(2),
    )(proj, proj, proj, proj, cosf, sinf, mask, rowdec, kdec, gtb, gn, sblk, dyr)


def _in_proj_bwd(dproj, w, x, r1, g, dx2, tm, carry=None):
    T = x.shape[0]

    def body(dp_ref, w_hbm, x_ref, r_ref, g_ref, dx2_ref, gx_ref, dg_ref, w_ref, sem):
        @pl.when(pl.program_id(0) == 0)
        def _():
            _load_resident(w_hbm, w_ref, sem)
            dg_ref[...] = jnp.zeros_like(dg_ref)

        dh = _dot_nt(dp_ref[:, 0:WIN_BLK], w_ref[0])
        for k in range(1, N_DEV):
            dh = dh + _dot_nt(dp_ref[:, k * WIN_BLK:(k + 1) * WIN_BLK], w_ref[k])
        dxn, dgt = _rms_bwd(dh, x_ref[...], r_ref[...], g_ref[...])
        gx_ref[...] = dx2_ref[...] + dxn
        dg_ref[...] += jnp.sum(dgt, axis=0, keepdims=True)

    full = pl.BlockSpec((tm, D_MODEL), lambda i: (i, 0))
    vec = pl.BlockSpec((1, D_MODEL), lambda i: (0, 0))
    return _pcall(
        body, "in_proj_bwd", (T // tm,),
        [pl.BlockSpec((tm, IN_WIDTH), lambda i: (i, 0)), ANY_SPEC,
         full, pl.BlockSpec((tm, 1), lambda i: (i, 0)), vec, full],
        [full, vec],
        [jax.ShapeDtypeStruct((T, D_MODEL), F32), jax.ShapeDtypeStruct((1, D_MODEL), F32)],
        [pltpu.VMEM(w.shape, w.dtype), pltpu.SemaphoreType.DMA], (dproj, w, x, r1, g, dx2), carry)


def _in_proj_wgrad(h, dproj, tk, carry=None):
    T = h.shape[0]
    nk = T // tk

    def body(h_ref, dp_ref, o_ref, acc):
        k = pl.program_id(1)

        @pl.when(k == 0)
        def _():
            acc[...] = jnp.zeros_like(acc)

        acc[...] += _dot_tn(h_ref[...], dp_ref[...])

        @pl.when(k == nk - 1)
        def _():
            o_ref[...] = acc[...].astype(BF16)

    return _pcall(
        body, "in_proj_wgrad", (N_DEV, nk),
        [pl.BlockSpec((tk, D_MODEL), lambda j, k: (k, 0)), pl.BlockSpec((tk, WIN_BLK), lambda j, k: (k, j))],
        [pl.BlockSpec((None, D_MODEL, WIN_BLK), lambda j, k: (j, 0, 0))],
        [jax.ShapeDtypeStruct((N_DEV, D_MODEL, WIN_BLK), BF16)],
        [pltpu.VMEM((D_MODEL, WIN_BLK), F32)], (h, dproj), carry)


def _rope_tables(T):
    half = HEAD_DIM // 2
    freqs = ROPE_BASE ** (-jnp.arange(half, dtype=F32) / half)
    ang = jnp.arange(T, dtype=F32)[:, None] * freqs[None, :]
    c = jnp.cos(ang)
    s = jnp.sin(ang)
    return jnp.concatenate([c, c], axis=1), jnp.concatenate([-s, s], axis=1)


def _retention_tables():
    hh = jnp.arange(RET_HEADS, dtype=F32)
    log_g = jnp.log1p(-(2.0 ** (-5.0 - hh)))[:, None, None]
    i = jnp.arange(RET_BLOCK)
    ci = (i // CHUNK)[:, None]
    cj = (i // CHUNK)[None, :]
    diff = (i[:, None] - i[None, :]).astype(F32)
    expo = jnp.where(ci == cj, jnp.abs(diff), diff)
    mask = jnp.where((cj <= ci)[None], jnp.exp(log_g * expo[None]), 0.0)
    r = jnp.arange(RET_BLOCK, dtype=F32)[None, :, None]
    ones = jnp.ones((1, 1, HEAD_DIM), F32)
    rowdec = jnp.exp(log_g * (r + 1.0)) * ones
    kdec = jnp.exp(log_g * (RET_BLOCK - 1.0 - r)) * ones
    gtb = jnp.exp(log_g * float(RET_BLOCK)) * ones
    return mask, rowdec, kdec, gtb


def _s5_discretise(a_re, a_im, log_dt, b_re, b_im):
    lam = lax.complex(a_re, a_im)
    dt = jnp.exp(log_dt)[:, None]
    lam_bar = jnp.exp(lam * dt)
    b_bar = ((lam_bar - 1.0) / lam)[..., None] * lax.complex(b_re, b_im)
    return jnp.real(lam_bar), jnp.imag(lam_bar), jnp.real(b_bar), jnp.imag(b_bar)


def _to_state_blockdiag(m):
    eye = jnp.eye(S5_GB, dtype=m.dtype)
    t = jnp.einsum("bgpc,gh->bgchp", m.reshape(S5_NBLK, S5_GB, SSM_STATE, SSM_GROUP), eye)
    return t.reshape(S5_NBLK, LANE, S5_LANES)


def _from_state_blockdiag(m):
    eye = jnp.eye(S5_GB, dtype=m.dtype)
    t = jnp.einsum("bgcp,gh->bgphc", m.reshape(S5_NBLK, S5_GB, SSM_GROUP, SSM_STATE), eye)
    return t.reshape(S5_NBLK, S5_LANES, LANE)


def _diag_of_state_major(acc):
    eye = jnp.eye(S5_GB, dtype=acc.dtype)
    t = acc.reshape(S5_NBLK, S5_GB, SSM_STATE, S5_GB, SSM_GROUP)
    return jnp.einsum("bgphc,gh->bgpc", t, eye).reshape(SSM_GROUPS, SSM_STATE, SSM_GROUP)


def _diag_of_channel_major(acc):
    eye = jnp.eye(S5_GB, dtype=acc.dtype)
    t = acc.reshape(S5_NBLK, S5_GB, SSM_GROUP, S5_GB, SSM_STATE)
    return jnp.einsum("bgchp,gh->bgcp", t, eye).reshape(SSM_GROUPS, SSM_GROUP, SSM_STATE)


SMALL_PARTIALS = (("ret_gn_g", 1024), ("lam_re", 4096), ("lam_im", 4096),
                  ("bbar_re", 65536), ("bbar_im", 65536), ("c_re", 65536), ("c_im", 65536),
                  ("ssm_d", 1024), ("b_glu", 1024), ("out_g", 1024), ("norm_ffn_g", 2048), ("norm_final_g", 2048))


def _forward_backward(x, tgt, shards, sm):
    T = x.shape[0]
    tm = min(1024, T)
    cosf, sinf = _rope_tables(T)
    mask, rowdec, kdec, gtb = _retention_tables()
    lbr, lbi, bbr, bbi = _s5_discretise(sm["ssm_a_re"], sm["ssm_a_im"], sm["ssm_log_dt"], sm["ssm_b_re"],
                                        sm["ssm_b_im"])
    bre = _to_state_blockdiag(bbr).astype(BF16)
    bim = _to_state_blockdiag(bbi).astype(BF16)
    cre_t = _from_state_blockdiag(sm["ssm_c_re"]).astype(BF16)
    cim_t = _from_state_blockdiag(sm["ssm_c_im"]).astype(BF16)
    bre_t = jnp.swapaxes(bre, 1, 2)
    bim_t = jnp.swapaxes(bim, 1, 2)
    cre = jnp.swapaxes(cre_t, 1, 2)
    cim = jnp.swapaxes(cim_t, 1, 2)
    lam = jnp.stack([lbr.reshape(S5_NBLK, S5_LANES), lbi.reshape(S5_NBLK, S5_LANES)], axis=1)
    pm = _step_major_permutation()
    pm_t = pm.T
    row = lambda v: v.reshape(1, -1)
    g_mix, g_ffn, g_fin = row(sm["norm_mix_g"]), row(sm["norm_ffn_g"]), row(sm["norm_final_g"])
    gn, dsk, bglu, og = row(sm["ret_gn_g"]), row(sm["ssm_d"]), row(sm["ssm_b_glu"]), row(sm["ssm_out_g"])

    (w_in,) = _exchange_call("weight_gather", [shards["w_in"]], True, via_sibling=True)
    proj, h1, r1, w_gate = _in_proj_fwd(x, g_mix, w_in, 256, _Exchange([shards["w_gate"]], True, via_sibling=True))
    y_ret, sblk, w_glu, w_out = _ret_fwd(proj, cosf, sinf, mask, rowdec, kdec, gtb, gn,
                                         _Exchange([shards["ssm_w_glu"], shards["w_out"]], True))
    w_glu = w_glu.reshape(SSM_WIDTH, SSM_WIDTH)
    w_out = w_out.reshape(D_MODEL, D_MODEL)
    y_s5, bound, w_up = _s5_fwd(proj, pm, pm_t, bre, bim, cre_t, cim_t, lam, dsk, _Exchange([shards["w_up"]], True))
    z, y_ssm, r_ssm = _glu_fwd(y_s5, w_glu, bglu, og, 256)
    x2, h2, r2 = _out_proj_fwd(x, y_ret, y_ssm, w_out, g_ffn, 256)
    a, b, f, w_down = _ffn_up(h2, w_gate, w_up, tm, _Exchange([shards["w_down"]], True))
    dx3, dx3b, loss8, dg_fin = _ffn_down_loss(f, w_down, x2, tgt, g_fin, 256)

    landed = {}
    da, db = _ffn_bwd_act(dx3b, w_down, a, b, tm)
    dw_down = _ffn_wgrad_down(f, dx3b, tm)
    dw_gate, dw_up, landed["w_down"] = _ffn_wgrad_up(h2, da, db, tm, _Exchange([dw_down], False))
    dh2, landed["w_gate"] = _ffn_bwd_in(da, db, w_gate, w_up, min(1024, T), _Exchange([dw_gate], False))
    dx2, dx2b, dg_ffn, dy_ret, dy_ssm = _out_proj_bwd(dh2, x2, r2, g_ffn, dx3, w_out, 256)
    dw_out = jnp.concatenate([_wgrad_rows("out_proj_wgrad_ret", y_ret, dx2b, tm),
                              _wgrad_rows("out_proj_wgrad_ssm", y_ssm, dx2b, tm)], axis=0)
    dy_s5, dw_glu, db_glu, dog = _glu_bwd(y_s5, z, r_ssm, dy_ssm, w_glu, og, 256)
    du, dbre, dbim, dcre, dcim, dlam, dd, landed["w_up"] = _s5_bwd(
        proj, dy_s5, bound, pm, pm_t, bre, bim, bre_t, bim_t, cre, cim, lam, dsk, _Exchange([dw_up], False))
    dq, dk, dv, dgate, dgn = _ret_bwd(proj, cosf, sinf, mask, rowdec, kdec, gtb, gn, sblk, dy_ret)
    dproj = jnp.concatenate([dq, dk, dv, dgate, du], axis=1)
    small = dict(ret_gn_g=dgn, lam_re=dlam[:, 0], lam_im=dlam[:, 1],
                 bbar_re=_diag_of_state_major(dbre), bbar_im=_diag_of_state_major(dbim),
                 c_re=_diag_of_channel_major(dcre), c_im=_diag_of_channel_major(dcim),
                 ssm_d=dd, b_glu=db_glu, out_g=dog, norm_ffn_g=dg_ffn, norm_final_g=dg_fin)
    packed = _pack([small[n] for n, _ in SMALL_PARTIALS])
    dw_in, landed["w_out"], landed["ssm_w_glu"], small_landed = _in_proj_wgrad(
        h1, dproj, tm, _Exchange([dw_out.reshape(N_DEV, D_MODEL // N_DEV, D_MODEL),
                                  dw_glu.astype(BF16).reshape(N_DEV, SSM_WIDTH // N_DEV, SSM_WIDTH), packed],
                                 [False, False, True]))
    grad_x, dg_mix, landed["w_in"] = _in_proj_bwd(dproj, w_in, x, r1, g_mix, dx2, 256, _Exchange([dw_in], False))
    (mix_landed,) = _exchange_call("mix_gain_grad_gather", [_pack([dg_mix])], True)
    summed = dict(zip([n for n, _ in SMALL_PARTIALS],
                      _unpack(_sum_partials("small_grad_sum", small_landed), [(sz,) for _, sz in SMALL_PARTIALS])))
    summed["norm_mix_g"] = _sum_partials("mix_gain_grad_sum", mix_landed).reshape(-1)
    return loss8[0, 0], grad_x, landed, summed


def _small_grads(summed, sm):
    _, vjp = jax.vjp(_s5_discretise, sm["ssm_a_re"], sm["ssm_a_im"], sm["ssm_log_dt"], sm["ssm_b_re"], sm["ssm_b_im"])
    gp = (SSM_GROUPS, SSM_STATE)
    da_re, da_im, dlog_dt, db_re, db_im = vjp((summed["lam_re"].reshape(gp), summed["lam_im"].reshape(gp),
                                               summed["bbar_re"].reshape(gp + (SSM_GROUP,)),
                                               summed["bbar_im"].reshape(gp + (SSM_GROUP,))))
    return dict(norm_mix_g=summed["norm_mix_g"], ret_gn_g=summed["ret_gn_g"], ssm_a_re=da_re, ssm_a_im=da_im,
                ssm_log_dt=dlog_dt, ssm_b_re=db_re, ssm_b_im=db_im,
                ssm_c_re=summed["c_re"].reshape(SSM_GROUPS, SSM_GROUP, SSM_STATE),
                ssm_c_im=summed["c_im"].reshape(SSM_GROUPS, SSM_GROUP, SSM_STATE),
                ssm_d=summed["ssm_d"], ssm_b_glu=summed["b_glu"], ssm_out_g=summed["out_g"],
                norm_ffn_g=summed["norm_ffn_g"], norm_final_g=summed["norm_final_g"])


def _adamw_math(w, g, m, v):
    m2 = ADAM_B1 * m + (1.0 - ADAM_B1) * g
    v2 = ADAM_B2 * v + (1.0 - ADAM_B2) * (g * g)
    delta = -ADAM_LR * ((m2 / ADAM_BC1) / (jnp.sqrt(v2 / ADAM_BC2) + ADAM_EPS) + ADAM_WD * w)
    return delta, m2, v2


def _adamw_shard(name, parts, w, m, v, tr):
    rows, cols = w.shape

    def body(p_ref, w_ref, m_ref, v_ref, g_ref, d_ref, m2_ref, v2_ref):
        g = p_ref[0].astype(F32)
        for s in range(1, N_DEV):
            g = g + p_ref[s].astype(F32)
        d, m2, v2 = _adamw_math(w_ref[...], g, m_ref[...], v_ref[...])
        g_ref[...] = g
        d_ref[...] = d
        m2_ref[...] = m2
        v2_ref[...] = v2

    blk = pl.BlockSpec((tr, cols), lambda i: (i, 0))
    oshape = jax.ShapeDtypeStruct((rows, cols), F32)
    return pl.pallas_call(
        body, name=name, grid=(rows // tr,),
        in_specs=[pl.BlockSpec((N_DEV, tr, cols), lambda i: (0, i, 0)), blk, blk, blk],
        out_specs=[blk, blk, blk, blk], out_shape=[oshape] * 4,
        compiler_params=_params(1),
    )(parts, w, m, v)


def _sum_partials(name, parts):
    rows = parts.shape[1]

    def body(p_ref, o_ref):
        g = p_ref[0]
        for s in range(1, N_DEV):
            g = g + p_ref[s]
        o_ref[...] = g

    return pl.pallas_call(
        body, name=name, grid=(1,),
        in_specs=[pl.BlockSpec((N_DEV, rows, LANE), lambda i: (0, 0, 0))],
        out_specs=pl.BlockSpec((rows, LANE), lambda i: (0, 0)),
        out_shape=jax.ShapeDtypeStruct((rows, LANE), F32),
        compiler_params=_params(1),
    )(parts)


def _adamw_small(ws, gs, ms, vs):
    n = len(ws)

    def body(*refs):
        for i in range(n):
            w_ref, g_ref, m_ref, v_ref = (refs[k * n + i] for k in range(4))
            d_ref, m2_ref, v2_ref = (refs[(4 + k) * n + i] for k in range(3))
            d, m2, v2 = _adamw_math(w_ref[...], g_ref[...], m_ref[...], v_ref[...])
            d_ref[...] = d
            m2_ref[...] = m2
            v2_ref[...] = v2

    vmem = pl.BlockSpec(memory_space=pltpu.VMEM)
    out = pl.pallas_call(
        body, name="adamw_small", in_specs=[vmem] * (4 * n), out_specs=[vmem] * (3 * n),
        out_shape=[jax.ShapeDtypeStruct(w.shape, F32) for w in ws] * 3,
        compiler_params=pltpu.CompilerParams(vmem_limit_bytes=VMEM_LIMIT),
    )(*ws, *gs, *ms, *vs)
    return out[:n], out[n:2 * n], out[2 * n:]


def _pack(arrays):
    parts = [a.reshape(-1, LANE) for a in arrays]
    assert all(p.shape[0] % 8 == 0 for p in parts)
    return parts[0] if len(parts) == 1 else jnp.concatenate(parts, axis=0)


def _unpack(packed, shapes):
    flat = packed.reshape(-1)
    out, off = [], 0
    for shp in shapes:
        n = math.prod(shp)
        out.append(flat[off:off + n].reshape(shp))
        off += n + ((-n) % LANE)
    return out


WEIGHTS = ("norm_mix_g", "w_in", "ret_gn_g", "ssm_a_re", "ssm_a_im", "ssm_log_dt", "ssm_b_re", "ssm_b_im",
           "ssm_c_re", "ssm_c_im", "ssm_d", "ssm_w_glu", "ssm_b_glu", "ssm_out_g", "w_out", "norm_ffn_g", "w_gate",
           "w_up", "w_down", "norm_final_g")
BIG = ("w_in", "ssm_w_glu", "w_out", "w_gate", "w_up", "w_down")
SMALL = tuple(n for n in WEIGHTS if n not in BIG)
ADAM_ROWS = {"w_in": 256, "ssm_w_glu": 128, "w_out": 128, "w_gate": 256, "w_up": 256, "w_down": 176}


def kernel(x, norm_mix_g, w_in, ret_gn_g, ssm_a_re, ssm_a_im, ssm_log_dt, ssm_b_re, ssm_b_im, ssm_c_re, ssm_c_im, ssm_d, ssm_w_glu, ssm_b_glu, ssm_out_g, w_out, norm_ffn_g, w_gate, w_up, w_down, norm_final_g, loss_target, m_norm_mix_g, m_w_in, m_ret_gn_g, m_ssm_a_re, m_ssm_a_im, m_ssm_log_dt, m_ssm_b_re, m_ssm_b_im, m_ssm_c_re, m_ssm_c_im, m_ssm_d, m_ssm_w_glu, m_ssm_b_glu, m_ssm_out_g, m_w_out, m_norm_ffn_g, m_w_gate, m_w_up, m_w_down, m_norm_final_g, v_norm_mix_g, v_w_in, v_ret_gn_g, v_ssm_a_re, v_ssm_a_im, v_ssm_log_dt, v_ssm_b_re, v_ssm_b_im, v_ssm_c_re, v_ssm_c_im, v_ssm_d, v_ssm_w_glu, v_ssm_b_glu, v_ssm_out_g, v_w_out, v_norm_ffn_g, v_w_gate, v_w_up, v_w_down, v_norm_final_g):
    given = dict(locals())
    w = {n: given[n] for n in WEIGHTS}
    m = {n: given["m_" + n] for n in WEIGHTS}
    v = {n: given["v_" + n] for n in WEIGHTS}
    drop = lambda n, a: a if n == "norm_final_g" else a[0]
    w0 = {n: drop(n, w[n]) for n in WEIGHTS}
    m0 = {n: drop(n, m[n]) for n in WEIGHTS}
    v0 = {n: drop(n, v[n]) for n in WEIGHTS}

    sm = {n: w0[n] for n in SMALL}
    shards = {n: w0[n].astype(BF16) for n in BIG}
    loss_local, grad_x, landed, summed = _forward_backward(x[0], loss_target[0], shards, sm)
    loss = lax.psum(loss_local, MESH_AXES)
    gsmall = _small_grads(summed, sm)

    grads, delta, new_m, new_v = {}, {}, {}, {}
    for n in BIG:
        g, d, m2, v2 = _adamw_shard("adamw_" + n, landed[n], w0[n], m0[n], v0[n], ADAM_ROWS[n])
        grads[n], delta[n], new_m[n], new_v[n] = g, d, m2, v2
    as_given = lambda n, a: a.reshape(1, -1) if n == "norm_final_g" else a.reshape(w[n].shape)
    gs = [as_given(n, gsmall[n]) for n in SMALL]
    ds, m2s, v2s = _adamw_small([as_given(n, w[n]) for n in SMALL], gs, [as_given(n, m[n]) for n in SMALL],
                                [as_given(n, v[n]) for n in SMALL])
    for n, g, d, m2, v2 in zip(SMALL, gs, ds, m2s, v2s):
        grads[n], delta[n], new_m[n], new_v[n] = g, d, m2, v2

    lift = lambda n, a: a.reshape(w[n].shape)
    return (loss, grad_x[None], *[lift(n, grads[n]) for n in WEIGHTS], *[lift(n, delta[n]) for n in WEIGHTS],
            *[lift(n, new_m[n]) for n in WEIGHTS], *[lift(n, new_v[n]) for n in WEIGHTS])
```

```python
import functools
import math

import jax
import jax.numpy as jnp
from jax import lax
from jax.experimental import pallas as pl
from jax.experimental.pallas import tpu as pltpu

F32 = jnp.float32
BF16 = jnp.bfloat16

D_MODEL = 2048
RET_WIDTH = 1024
RET_HEADS = 8
HEAD_DIM = 128
CHUNK = 64
SSM_WIDTH = 1024
SSM_GROUP = 16
SSM_GROUPS = 64
SSM_STATE = 64
D_FF = 5632
IN_WIDTH = 5120
ROPE_BASE = 10000.0
EPS = 1e-6
N_DEV = 8
MESH_AXES = ("x", "y", "c")

WIN_BLK = IN_WIDTH // N_DEV
FF_BLK = D_FF // N_DEV
RET_BLOCK = 256
RET_HPS = 4
S5_TILE = 256
S5_CHUNKS = 8
S5_STEPS = S5_TILE // S5_CHUNKS
S5_PER_STEP = 2
S5_GB = 8
S5_NBLK = SSM_GROUPS // S5_GB
S5_LANES = S5_GB * SSM_STATE
LANE = 128

ADAM_LR = 0.001
ADAM_B1 = 0.9
ADAM_B2 = 0.999
ADAM_EPS = 1e-08
ADAM_WD = 0.01
ADAM_STEP = 10
ADAM_BC1 = 1.0 - ADAM_B1 ** ADAM_STEP
ADAM_BC2 = 1.0 - ADAM_B2 ** ADAM_STEP

VMEM_LIMIT = 56 * 1024 * 1024

NT = (((1,), (1,)), ((), ()))
TN = (((0,), (0,)), ((), ()))


def _params(n_grid):
    return pltpu.CompilerParams(dimension_semantics=("arbitrary",) * n_grid, vmem_limit_bytes=VMEM_LIMIT)


def _dot(a, b):
    return jnp.dot(a, b, preferred_element_type=F32)


def _dot_nt(a, b):
    return lax.dot_general(a, b, NT, preferred_element_type=F32)


def _dot_tn(a, b):
    return lax.dot_general(a, b, TN, preferred_element_type=F32)


def _sigmoid(x):
    return 1.0 / (1.0 + jnp.exp(-x))


_GELU_C = math.sqrt(2.0 / math.pi)
_GELU_A = 0.044715


def _gelu(x):
    t = jnp.tanh(_GELU_C * (x + _GELU_A * x * x * x))
    return 0.5 * x * (1.0 + t)


def _gelu_and_grad(x):
    t = jnp.tanh(_GELU_C * (x + _GELU_A * x * x * x))
    g = 0.5 * (1.0 + t) + 0.5 * x * (1.0 - t * t) * _GELU_C * (1.0 + 3.0 * _GELU_A * x * x)
    return 0.5 * x * (1.0 + t), g


def _rms_bwd(dy, x, r, g):
    w = dy * g
    dx = r * w - x * (r * r * r) * jnp.mean(w * x, axis=-1, keepdims=True)
    return dx, dy * x * r


HBM_SPEC = pl.BlockSpec(memory_space=pltpu.HBM)
ANY_SPEC = pl.BlockSpec(memory_space=pl.ANY)


def _load_resident(src_hbm, dst_vmem, sem):
    cp = pltpu.make_async_copy(src_hbm, dst_vmem, sem)
    cp.start()
    cp.wait()


def _my_block():
    return 4 * lax.axis_index("x") + 2 * lax.axis_index("y") + lax.axis_index("c")


def _peer(k):
    px = lax.axis_index("x") ^ ((k >> 2) & 1)
    py = lax.axis_index("y") ^ ((k >> 1) & 1)
    pc = lax.axis_index("c") ^ (k & 1)
    return (px, py, pc), 4 * px + 2 * py + pc


class _Exchange:
    def __init__(self, payloads, gather, via_sibling=False):
        self.payloads = list(payloads)
        self.n = len(self.payloads)
        self.gather = [gather] * self.n if isinstance(gather, bool) else list(gather)
        self.via_sibling = via_sibling
        assert not via_sibling or all(self.gather)

    def out_shape(self):
        return [jax.ShapeDtypeStruct(((N_DEV,) if g else ()) + p.shape, p.dtype)
                for p, g in zip(self.payloads, self.gather)]

    def scratch_shapes(self):
        return [pltpu.SemaphoreType.DMA((self.n, N_DEV - 1)), pltpu.SemaphoreType.DMA((self.n, N_DEV - 1)),
                pltpu.SemaphoreType.DMA((self.n,))]

    def _copies(self, ins, outs, sems, incoming):
        send_sems, recv_sems, local_sems = sems
        me = _my_block()
        src_of = lambda i, blk: ins[i] if self.gather[i] else ins[i].at[blk]
        local, remote = [], []
        for i in range(self.n):
            if not incoming:
                local.append(pltpu.make_async_copy(src_of(i, me), outs[i].at[me], local_sems.at[i]))
            for k in range(1, N_DEV):
                dev, blk = _peer(k)
                src, dst = (outs[i].at[blk], outs[i].at[blk]) if incoming else (src_of(i, blk), outs[i].at[me])
                remote.append(pltpu.make_async_remote_copy(
                    src_ref=src, dst_ref=dst, send_sem=send_sems.at[i, k - 1], recv_sem=recv_sems.at[i, k - 1],
                    device_id=dev, device_id_type=pl.DeviceIdType.MESH))
        return local, remote

    def _copy(self, i, k, outs, sems, src, dst_blk, to_k):
        send_sems, recv_sems, _ = sems
        return pltpu.make_async_remote_copy(
            src_ref=src, dst_ref=outs[i].at[dst_blk], send_sem=send_sems.at[i, k - 1], recv_sem=recv_sems.at[i, k - 1],
            device_id=_peer(to_k)[0], device_id_type=pl.DeviceIdType.MESH)

    FIRST_HOPS = (1, 2, 4, 6)
    FROM_CHIPS = (2, 4, 6)

    def start(self, ins, outs, sems):
        if not self.via_sibling:
            local, sends = self._copies(ins, outs, sems, False)
            for cp in local + sends:
                cp.start()
            return
        me = _my_block()
        for i in range(self.n):
            pltpu.make_async_copy(ins[i], outs[i].at[me], sems[2].at[i]).start()
            for k in self.FIRST_HOPS:
                self._copy(i, k, outs, sems, ins[i], me, k).start()

    def wait(self, ins, outs, sems):
        if not self.via_sibling:
            for cp in self._copies(ins, outs, sems, True)[1]:
                cp.wait_recv()
            local, sends = self._copies(ins, outs, sems, False)
            for cp in sends:
                cp.wait_send()
            for cp in local:
                cp.wait()
            return
        me = _my_block()
        landed = lambda i, k: self._copy(i, k, outs, sems, outs[i].at[_peer(k)[1]], _peer(k)[1], k)
        for i in range(self.n):
            for s in self.FROM_CHIPS:
                landed(i, s).wait_recv()
                self._copy(i, s ^ 1, outs, sems, outs[i].at[_peer(s)[1]], _peer(s)[1], 1).start()
        for i in range(self.n):
            for k in (1, 3, 5, 7):
                landed(i, k).wait_recv()
            for k in self.FIRST_HOPS:
                self._copy(i, k, outs, sems, ins[i], me, k).wait_send()
            for s in self.FROM_CHIPS:
                self._copy(i, s ^ 1, outs, sems, outs[i].at[_peer(s)[1]], _peer(s)[1], 1).wait_send()
            pltpu.make_async_copy(ins[i], outs[i].at[me], sems[2].at[i]).wait()


def _pcall(body, name, grid, in_specs, out_specs, out_shape, scratch_shapes, args, carry=None):
    n_in, n_out, n_scr = len(in_specs), len(out_specs), len(scratch_shapes)
    if carry is None:
        return pl.pallas_call(body, name=name, grid=grid, in_specs=in_specs, out_specs=out_specs, out_shape=out_shape,
                              scratch_shapes=scratch_shapes, compiler_params=_params(len(grid)))(*args)
    nx = carry.n

    def wrapped(*refs):
        cin, xin = refs[:n_in], refs[n_in:n_in + nx]
        cout, xout = refs[n_in + nx:n_in + nx + n_out], refs[n_in + nx + n_out:n_in + 2 * nx + n_out]
        rest = refs[n_in + 2 * nx + n_out:]
        cscr, sems = rest[:n_scr], rest[n_scr:]
        first = functools.reduce(jnp.logical_and, [pl.program_id(a) == 0 for a in range(len(grid))])
        last = functools.reduce(jnp.logical_and, [pl.program_id(a) == grid[a] - 1 for a in range(len(grid))])

        @pl.when(first)
        def _():
            carry.start(xin, xout, sems)

        body(*cin, *cout, *cscr)

        @pl.when(last)
        def _():
            carry.wait(xin, xout, sems)

    return pl.pallas_call(
        wrapped, name=name, grid=grid, in_specs=list(in_specs) + [HBM_SPEC] * nx,
        out_specs=list(out_specs) + [HBM_SPEC] * nx, out_shape=list(out_shape) + carry.out_shape(),
        scratch_shapes=list(scratch_shapes) + carry.scratch_shapes(), compiler_params=_params(len(grid)),
    )(*args, *carry.payloads)


def _exchange_call(name, payloads, gather, via_sibling=False):
    ex = _Exchange(payloads, gather, via_sibling)

    def body(*refs):
        ins, outs, sems = refs[:ex.n], refs[ex.n:2 * ex.n], refs[2 * ex.n:]
        ex.start(ins, outs, sems)
        ex.wait(ins, outs, sems)

    return pl.pallas_call(body, name=name, in_specs=[HBM_SPEC] * ex.n, out_specs=[HBM_SPEC] * ex.n,
                          out_shape=ex.out_shape(), scratch_shapes=ex.scratch_shapes())(*ex.payloads)


def _in_proj_fwd(x, g, w, tm, carry=None):
    T = x.shape[0]

    def body(x_ref, g_ref, w_hbm, proj_ref, h_ref, r_ref, w_ref, sem):
        @pl.when(pl.program_id(0) == 0)
        def _():
            _load_resident(w_hbm, w_ref, sem)

        xf = x_ref[...]
        r = lax.rsqrt(jnp.mean(xf * xf, axis=-1, keepdims=True) + EPS)
        h = (xf * r * g_ref[...]).astype(BF16)
        h_ref[...] = h
        r_ref[...] = r
        for j in range(N_DEV):
            proj_ref[:, j * WIN_BLK:(j + 1) * WIN_BLK] = _dot(h, w_ref[j])

    return _pcall(
        body, "in_proj_fwd", (T // tm,),
        [pl.BlockSpec((tm, D_MODEL), lambda i: (i, 0)), pl.BlockSpec((1, D_MODEL), lambda i: (0, 0)), ANY_SPEC],
        [pl.BlockSpec((tm, IN_WIDTH), lambda i: (i, 0)),
         pl.BlockSpec((tm, D_MODEL), lambda i: (i, 0)),
         pl.BlockSpec((tm, 1), lambda i: (i, 0))],
        [jax.ShapeDtypeStruct((T, IN_WIDTH), F32),
         jax.ShapeDtypeStruct((T, D_MODEL), BF16),
         jax.ShapeDtypeStruct((T, 1), F32)],
        [pltpu.VMEM(w.shape, w.dtype), pltpu.SemaphoreType.DMA], (x, g, w), carry)


def _ret_common(q_ref, k_ref, v_ref, cos_ref, sin_ref, mask_ref, rd_ref, sin_state):
    c = cos_ref[...]
    s = sin_ref[...]
    q = q_ref[...]
    q = q * c + pltpu.roll(q, HEAD_DIM // 2, 1) * s
    k = k_ref[...]
    k = (k * c + pltpu.roll(k, HEAD_DIM // 2, 1) * s) * (HEAD_DIM ** -0.5)
    qb = q.astype(BF16)
    kb = k.astype(BF16)
    vb = v_ref[...].astype(BF16)
    pm = (_dot_nt(qb, kb) * mask_ref[...]).astype(BF16)
    qd = (q * rd_ref[...]).astype(BF16)
    o = _dot(pm, vb) + _dot(qd, sin_state.astype(BF16))
    return q, k, qb, kb, vb, pm, qd, o


def _ret_specs(T, rev):
    nb = T // RET_BLOCK
    groups = RET_HEADS // RET_HPS
    wide = RET_HPS * HEAD_DIM
    blk = (lambda b: nb - 1 - b) if rev else (lambda b: b)
    col = lambda piece: (pl.BlockSpec((RET_BLOCK, wide), lambda h, b: (blk(b), piece * groups + h)), "lane")
    return dict(
        q=col(0), k=col(1), v=col(2), g=col(3),
        tab=(pl.BlockSpec((RET_BLOCK, HEAD_DIM), lambda h, b: (blk(b), 0)), None),
        mask=(pl.BlockSpec((RET_HPS, RET_BLOCK, RET_BLOCK), lambda h, b: (h, 0, 0)), "lead"),
        dec=(pl.BlockSpec((RET_HPS, RET_BLOCK, HEAD_DIM), lambda h, b: (h, 0, 0)), "lead"),
        gtb=(pl.BlockSpec((RET_HPS, 1, HEAD_DIM), lambda h, b: (h, 0, 0)), "lead"),
        gn=(pl.BlockSpec((1, wide), lambda h, b: (0, h)), "lane"),
        state=(pl.BlockSpec((RET_HPS, None, HEAD_DIM, HEAD_DIM), lambda h, b: (h, blk(b), 0, 0)), "lead"),
        rows=(pl.BlockSpec((RET_BLOCK, wide), lambda h, b: (blk(b), h)), "lane"),
        scratch=(pltpu.VMEM((RET_HPS, HEAD_DIM, HEAD_DIM), F32), "lead"),
    )


def _per_head(head_body, kinds):
    def body(*refs):
        for hh in range(RET_HPS):
            views = []
            for ref, kind in zip(refs, kinds):
                if kind == "lane":
                    views.append(ref.at[:, hh * HEAD_DIM:(hh + 1) * HEAD_DIM])
                elif kind == "lead":
                    views.append(ref.at[hh])
                else:
                    views.append(ref)
            head_body(*views)
    return body


def _ret_fwd(proj, cosf, sinf, mask, rowdec, kdec, gtb, gn, carry=None):
    T = proj.shape[0]
    nb = T // RET_BLOCK
    sp = _ret_specs(T, False)

    def body(q_ref, k_ref, v_ref, g_ref, cos_ref, sin_ref, mask_ref, rd_ref, kd_ref, gtb_ref, gn_ref,
             y_ref, sb_ref, st):
        @pl.when(pl.program_id(1) == 0)
        def _():
            st[...] = jnp.zeros_like(st)
        s_in = st[...]
        sb_ref[...] = s_in
        q, k, qb, kb, vb, pm, qd, o = _ret_common(q_ref, k_ref, v_ref, cos_ref, sin_ref, mask_ref, rd_ref, s_in)
        st[...] = gtb_ref[...] * s_in + _dot_tn((k * kd_ref[...]).astype(BF16), vb)
        mu = jnp.mean(o, axis=-1, keepdims=True)
        oc = o - mu
        n = oc * lax.rsqrt(jnp.mean(oc * oc, axis=-1, keepdims=True) + EPS)
        gt = g_ref[...]
        y_ref[...] = (gt * _sigmoid(gt) * (n * gn_ref[...])).astype(BF16)

    ins = [sp[n] for n in ("q", "k", "v", "g", "tab", "tab", "mask", "dec", "dec", "gtb", "gn")]
    outs = [sp["rows"], sp["state"]]
    return _pcall(
        _per_head(body, [kind for _, kind in ins + outs + [sp["scratch"]]]), "ret_fwd", (RET_HEADS // RET_HPS, nb),
        [s for s, _ in ins], [s for s, _ in outs],
        [jax.ShapeDtypeStruct((T, RET_WIDTH), BF16),
         jax.ShapeDtypeStruct((RET_HEADS, nb, HEAD_DIM, HEAD_DIM), F32)],
        [sp["scratch"][0]],
        (proj, proj, proj, proj, cosf, sinf, mask, rowdec, kdec, gtb, gn), carry)


def _scan(re, im, ar, ai, reverse):
    n = re.shape[0]
    row = lax.broadcasted_iota(jnp.int32, re.shape, 0)
    s = 1
    while s < n:
        if reverse:
            keep = row < n - s
            sr = jnp.where(keep, pltpu.roll(re, n - s, 0), 0.0)
            si = jnp.where(keep, pltpu.roll(im, n - s, 0), 0.0)
        else:
            keep = row >= s
            sr = jnp.where(keep, pltpu.roll(re, s, 0), 0.0)
            si = jnp.where(keep, pltpu.roll(im, s, 0), 0.0)
        re, im = re + ar * sr - ai * si, im + ar * si + ai * sr
        ar, ai = ar * ar - ai * ai, 2.0 * ar * ai
        s *= 2
    return re, im


S5_STATE_TILE = (S5_TILE, S5_LANES)


def _step_major_permutation():
    r = jnp.arange(S5_TILE)
    t_of_row = (r % S5_CHUNKS) * S5_STEPS + r // S5_CHUNKS
    return (t_of_row[:, None] == r[None, :]).astype(BF16)


def _permute_rows_f32(pm, x):
    hi = x.astype(BF16)
    rest = x - hi.astype(F32)
    mid = rest.astype(BF16)
    lo = (rest - mid.astype(F32)).astype(BF16)
    return _dot(pm, hi) + _dot(pm, mid) + _dot(pm, lo)


def _step_get(ref, j):
    return ref[j * S5_CHUNKS:(j + 1) * S5_CHUNKS, :]


def _step_set(ref, j, val):
    ref[j * S5_CHUNKS:(j + 1) * S5_CHUNKS, :] = val


def _tile_get(ref):
    return ref[...]


def _tile_set(ref, val):
    ref[...] = val


def _fill_power_table(ptab, lr, li):
    shape = (S5_CHUNKS, S5_LANES)
    lrb = jnp.broadcast_to(lr, shape)
    lib = jnp.broadcast_to(li, shape)
    pr, pi_ = lrb, lib
    for j in range(S5_STEPS):
        ptab[0, j * S5_CHUNKS:(j + 1) * S5_CHUNKS, :] = pr
        ptab[1, j * S5_CHUNKS:(j + 1) * S5_CHUNKS, :] = pi_
        pr, pi_ = lrb * pr - lib * pi_, lrb * pi_ + lib * pr


def _chunk_scans(xr, xi, lr, li, reverse):
    shape = (S5_CHUNKS, S5_LANES)
    lrb = jnp.broadcast_to(lr, shape)
    lib = jnp.broadcast_to(li, shape)
    sr = si = None
    for j in (range(S5_STEPS - 1, -1, -1) if reverse else range(S5_STEPS)):
        vr = _step_get(xr, j)
        vi = _step_get(xi, j)
        if sr is not None:
            vr, vi = vr + lrb * sr - lib * si, vi + lrb * si + lib * sr
            _step_set(xr, j, vr)
            _step_set(xi, j, vi)
        sr, si = vr, vi
    return sr, si


def _entering_states(zr, zi, cr, ci, ar, ai, reverse):
    shape = (S5_CHUNKS, S5_LANES)
    row = lax.broadcasted_iota(jnp.int32, shape, 0)
    if reverse:
        edge, shift = row == S5_CHUNKS - 1, S5_CHUNKS - 1
    else:
        edge, shift = row == 0, 1
    wr = jnp.where(edge, jnp.broadcast_to(cr, shape), pltpu.roll(zr, shift, 0))
    wi = jnp.where(edge, jnp.broadcast_to(ci, shape), pltpu.roll(zi, shift, 0))
    return _scan(wr, wi, ar, ai, reverse)


def _table_rows(ptab, j, conj):
    pr = ptab[0, j * S5_CHUNKS:(j + 1) * S5_CHUNKS, :]
    pi_ = ptab[1, j * S5_CHUNKS:(j + 1) * S5_CHUNKS, :]
    return pr, (-pi_ if conj else pi_)


def _s5_forward_states(xr, xi, lr, li, cr, ci, ptab):
    zr, zi = _chunk_scans(xr, xi, lr, li, False)
    ar, ai = _table_rows(ptab, S5_STEPS - 1, False)
    er, ei = _entering_states(zr, zi, cr, ci, ar, ai, False)
    for j in range(S5_STEPS):
        pr, pi_ = _table_rows(ptab, j, False)
        _step_set(xr, j, _step_get(xr, j) + pr * er - pi_ * ei)
        _step_set(xi, j, _step_get(xi, j) + pr * ei + pi_ * er)
    last = S5_CHUNKS - 1
    end_r = (ar * er - ai * ei + zr)[last:last + 1, :]
    end_i = (ar * ei + ai * er + zi)[last:last + 1, :]
    return er, ei, end_r, end_i


def _s5_specs(T, rev):
    rows = S5_TILE * S5_PER_STEP
    nt = T // rows
    tt = (lambda t: nt - 1 - t) if rev else (lambda t: t)
    return dict(
        u=pl.BlockSpec((rows, LANE), lambda b, t: (tt(t), 4 * RET_HEADS + b)),
        rows=pl.BlockSpec((rows, LANE), lambda b, t: (tt(t), b)),
        to_state=pl.BlockSpec((None, LANE, S5_LANES), lambda b, t: (b, 0, 0)),
        from_state=pl.BlockSpec((None, S5_LANES, LANE), lambda b, t: (b, 0, 0)),
        lam=pl.BlockSpec((None, 2, S5_LANES), lambda b, t: (b, 0, 0)),
        d=pl.BlockSpec((1, LANE), lambda b, t: (0, b)),
        perm=pl.BlockSpec((S5_TILE, S5_TILE), lambda b, t: (0, 0)),
        bound=pl.BlockSpec((None, S5_PER_STEP, 2, S5_LANES), lambda b, t: (b, tt(t), 0, 0)),
    )


def _s5_fwd(proj, pm, pm_t, bre, bim, cre_t, cim_t, lam, d, carry=None):
    T = proj.shape[0]
    nt = T // S5_TILE
    sp = _s5_specs(T, False)

    def body(u_ref, pm_ref, pmt_ref, bre_ref, bim_ref, cre_ref, cim_ref, lam_ref, d_ref, y_ref, bound_ref,
             carry, ptab, xr, xi):
        lr = lam_ref[0:1, :]
        li = lam_ref[1:2, :]

        @pl.when(pl.program_id(1) == 0)
        def _():
            carry[...] = jnp.zeros_like(carry)
            _fill_power_table(ptab, lr, li)

        for s in range(S5_PER_STEP):
            rows = slice(s * S5_TILE, (s + 1) * S5_TILE)
            u = _permute_rows_f32(pm_ref[...], u_ref[rows, :])
            ub = u.astype(BF16)
            _tile_set(xr, _dot(ub, bre_ref[...]))
            _tile_set(xi, _dot(ub, bim_ref[...]))
            bound_ref[s] = carry[...]
            _, _, end_r, end_i = _s5_forward_states(xr, xi, lr, li, carry[0:1, :], carry[1:2, :], ptab)
            carry[0:1, :] = end_r
            carry[1:2, :] = end_i
            y = (_dot(_tile_get(xr).astype(BF16), cre_ref[...]) - _dot(_tile_get(xi).astype(BF16), cim_ref[...])
                 + d_ref[...] * u)
            y_ref[rows, :] = _permute_rows_f32(pmt_ref[...], y)

    state = pltpu.VMEM(S5_STATE_TILE, F32)
    return _pcall(
        body, "s5_fwd", (S5_NBLK, nt // S5_PER_STEP),
        [sp["u"], sp["perm"], sp["perm"], sp["to_state"], sp["to_state"], sp["from_state"],
         sp["from_state"], sp["lam"], sp["d"]],
        [sp["rows"], sp["bound"]],
        [jax.ShapeDtypeStruct((T, SSM_WIDTH), F32),
         jax.ShapeDtypeStruct((S5_NBLK, nt, 2, S5_LANES), F32)],
        [pltpu.VMEM((2, S5_LANES), F32), pltpu.VMEM((2, S5_TILE, S5_LANES), F32), state, state],
        (proj, pm, pm_t, bre, bim, cre_t, cim_t, lam, d), carry)


def _glu_fwd(y, w, b, og, tm):
    T = y.shape[0]

    def body(y_ref, w_ref, b_ref, og_ref, z_ref, o_ref, r_ref):
        y1 = _gelu(y_ref[...])
        z = _dot(y1.astype(BF16), w_ref[...]) + b_ref[...]
        y2 = y1 * _sigmoid(z)
        r = lax.rsqrt(jnp.mean(y2 * y2, axis=-1, keepdims=True) + EPS)
        z_ref[...] = z
        o_ref[...] = (y2 * r * og_ref[...]).astype(BF16)
        r_ref[...] = r

    row = pl.BlockSpec((tm, SSM_WIDTH), lambda i: (i, 0))
    vec = pl.BlockSpec((1, SSM_WIDTH), lambda i: (0, 0))
    return pl.pallas_call(
        body, name="glu_fwd", grid=(T // tm,),
        in_specs=[row, pl.BlockSpec((SSM_WIDTH, SSM_WIDTH), lambda i: (0, 0)), vec, vec],
        out_specs=[row, row, pl.BlockSpec((tm, 1), lambda i: (i, 0))],
        out_shape=[jax.ShapeDtypeStruct((T, SSM_WIDTH), F32), jax.ShapeDtypeStruct((T, SSM_WIDTH), BF16),
                   jax.ShapeDtypeStruct((T, 1), F32)],
        compiler_params=_params(1),
    )(y, w, b, og)


def _out_proj_fwd(x, y_ret, y_ssm, w, g, tm):
    T = x.shape[0]

    def body(x_ref, a_ref, b_ref, w_ref, g_ref, x2_ref, h_ref, r_ref):
        x2 = x_ref[...] + _dot(a_ref[...], w_ref[0:RET_WIDTH, :]) + _dot(b_ref[...], w_ref[RET_WIDTH:D_MODEL, :])
        r = lax.rsqrt(jnp.mean(x2 * x2, axis=-1, keepdims=True) + EPS)
        x2_ref[...] = x2
        h_ref[...] = (x2 * r * g_ref[...]).astype(BF16)
        r_ref[...] = r

    full = pl.BlockSpec((tm, D_MODEL), lambda i: (i, 0))
    half = pl.BlockSpec((tm, RET_WIDTH), lambda i: (i, 0))
    return pl.pallas_call(
        body, name="out_proj_fwd", grid=(T // tm,),
        in_specs=[full, half, half, pl.BlockSpec((D_MODEL, D_MODEL), lambda i: (0, 0)),
                  pl.BlockSpec((1, D_MODEL), lambda i: (0, 0))],
        out_specs=[full, full, pl.BlockSpec((tm, 1), lambda i: (i, 0))],
        out_shape=[jax.ShapeDtypeStruct((T, D_MODEL), F32), jax.ShapeDtypeStruct((T, D_MODEL), BF16),
                   jax.ShapeDtypeStruct((T, 1), F32)],
        compiler_params=_params(1),
    )(x, y_ret, y_ssm, w, g)


def _ffn_up(h, wg, wu, tm, carry=None):
    T = h.shape[0]

    def body(h_ref, wg_ref, wu_ref, a_ref, b_ref, f_ref):
        hb = h_ref[...]
        a = _dot(hb, wg_ref[...])
        b = _dot(hb, wu_ref[...])
        a_ref[...] = a.astype(BF16)
        b_ref[...] = b.astype(BF16)
        f_ref[...] = (a * _sigmoid(a) * b).astype(BF16)

    wspec = pl.BlockSpec((None, D_MODEL, FF_BLK), lambda j, i: (j, 0, 0))
    ospec = pl.BlockSpec((None, tm, FF_BLK), lambda j, i: (j, i, 0))
    oshape = jax.ShapeDtypeStruct((N_DEV, T, FF_BLK), BF16)
    return _pcall(
        body, "ffn_up", (N_DEV, T // tm),
        [pl.BlockSpec((tm, D_MODEL), lambda j, i: (i, 0)), wspec, wspec],
        [ospec, ospec, ospec], [oshape, oshape, oshape], [], (h, wg, wu), carry)


def _ffn_down_loss(f, wd, x2, tgt, g, tm):
    T = x2.shape[0]

    def body(f_ref, w_hbm, x2_ref, t_ref, g_ref, dx_ref, dxb_ref, loss_ref, dg_ref, w_ref, sem):
        i = pl.program_id(0)

        @pl.when(i == 0)
        def _():
            _load_resident(w_hbm, w_ref, sem)
            loss_ref[...] = jnp.zeros_like(loss_ref)
            dg_ref[...] = jnp.zeros_like(dg_ref)

        gv = g_ref[...]
        x3 = x2_ref[...]
        for k in range(N_DEV):
            x3 = x3 + _dot(f_ref[k], w_ref[k])
        r = lax.rsqrt(jnp.mean(x3 * x3, axis=-1, keepdims=True) + EPS)
        err = x3 * r * gv - t_ref[...]
        tile_loss = 0.5 * jnp.sum(jnp.mean(err * err, axis=-1, keepdims=True), axis=0, keepdims=True)
        dx, dgt = _rms_bwd(err * (1.0 / D_MODEL), x3, r, gv)
        dx_ref[...] = dx
        dxb_ref[...] = dx.astype(BF16)
        loss_ref[...] += jnp.broadcast_to(tile_loss, loss_ref.shape)
        dg_ref[...] += jnp.sum(dgt, axis=0, keepdims=True)

    full = pl.BlockSpec((tm, D_MODEL), lambda i: (i, 0))
    vec = pl.BlockSpec((1, D_MODEL), lambda i: (0, 0))
    return pl.pallas_call(
        body, name="ffn_down_loss", grid=(T // tm,),
        in_specs=[pl.BlockSpec((N_DEV, tm, FF_BLK), lambda i: (0, i, 0)), ANY_SPEC, full, full, vec],
        out_specs=[full, full, pl.BlockSpec((8, LANE), lambda i: (0, 0)), vec],
        out_shape=[jax.ShapeDtypeStruct((T, D_MODEL), F32), jax.ShapeDtypeStruct((T, D_MODEL), BF16),
                   jax.ShapeDtypeStruct((8, LANE), F32), jax.ShapeDtypeStruct((1, D_MODEL), F32)],
        scratch_shapes=[pltpu.VMEM(wd.shape, wd.dtype), pltpu.SemaphoreType.DMA],
        compiler_params=_params(1),
    )(f, wd, x2, tgt, g)


def _ffn_bwd_act(dxb, wd, a, b, tm):
    T = dxb.shape[0]

    def body(dx_ref, w_ref, a_ref, b_ref, da_ref, db_ref):
        df = _dot_nt(dx_ref[...], w_ref[...])
        a = a_ref[...].astype(F32)
        b = b_ref[...].astype(F32)
        sg = _sigmoid(a)
        da_ref[...] = (df * b * sg * (1.0 + a * (1.0 - sg))).astype(BF16)
        db_ref[...] = (df * a * sg).astype(BF16)

    blk = pl.BlockSpec((None, tm, FF_BLK), lambda j, i: (j, i, 0))
    oshape = jax.ShapeDtypeStruct((N_DEV, T, FF_BLK), BF16)
    return pl.pallas_call(
        body, name="ffn_bwd_act", grid=(N_DEV, T // tm),
        in_specs=[pl.BlockSpec((tm, D_MODEL), lambda j, i: (i, 0)),
                  pl.BlockSpec((None, FF_BLK, D_MODEL), lambda j, i: (j, 0, 0)), blk, blk],
        out_specs=[blk, blk], out_shape=[oshape, oshape],
        compiler_params=_params(2),
    )(dxb, wd, a, b)


def _ffn_bwd_in(da, db, wg, wu, tm, carry=None):
    T = da.shape[1]

    def body(da_ref, db_ref, wg_ref, wu_ref, dh_ref):
        part = _dot_nt(da_ref[...], wg_ref[...]) + _dot_nt(db_ref[...], wu_ref[...])

        @pl.when(pl.program_id(1) == 0)
        def _():
            dh_ref[...] = part

        @pl.when(pl.program_id(1) > 0)
        def _():
            dh_ref[...] += part

    ablk = pl.BlockSpec((None, tm, FF_BLK), lambda i, k: (k, i, 0))
    wblk = pl.BlockSpec((None, D_MODEL, FF_BLK), lambda i, k: (k, 0, 0))
    return _pcall(
        body, "ffn_bwd_in", (T // tm, N_DEV), [ablk, ablk, wblk, wblk],
        [pl.BlockSpec((tm, D_MODEL), lambda i, k: (i, 0))], [jax.ShapeDtypeStruct((T, D_MODEL), F32)],
        [], (da, db, wg, wu), carry)


def _ffn_wgrad_up(h, da, db, tk, carry=None):
    T = h.shape[0]
    nk = T // tk

    def body(h_ref, da_ref, db_ref, g_ref, u_ref, accg, accu):
        k = pl.program_id(1)

        @pl.when(k == 0)
        def _():
            accg[...] = jnp.zeros_like(accg)
            accu[...] = jnp.zeros_like(accu)

        hb = h_ref[...]
        accg[...] += _dot_tn(hb, da_ref[...])
        accu[...] += _dot_tn(hb, db_ref[...])

        @pl.when(k == nk - 1)
        def _():
            g_ref[...] = accg[...].astype(BF16)
            u_ref[...] = accu[...].astype(BF16)

    blk = pl.BlockSpec((None, tk, FF_BLK), lambda j, k: (j, k, 0))
    ospec = pl.BlockSpec((None, D_MODEL, FF_BLK), lambda j, k: (j, 0, 0))
    oshape = jax.ShapeDtypeStruct((N_DEV, D_MODEL, FF_BLK), BF16)
    return _pcall(
        body, "ffn_wgrad_up", (N_DEV, nk),
        [pl.BlockSpec((tk, D_MODEL), lambda j, k: (k, 0)), blk, blk],
        [ospec, ospec], [oshape, oshape],
        [pltpu.VMEM((D_MODEL, FF_BLK), F32), pltpu.VMEM((D_MODEL, FF_BLK), F32)], (h, da, db), carry)


def _ffn_wgrad_down(f, dxb, tk):
    T = dxb.shape[0]
    nk = T // tk

    def body(f_ref, dx_ref, o_ref, acc):
        k = pl.program_id(1)

        @pl.when(k == 0)
        def _():
            acc[...] = jnp.zeros_like(acc)

        acc[...] += _dot_tn(f_ref[...], dx_ref[...])

        @pl.when(k == nk - 1)
        def _():
            o_ref[...] = acc[...].astype(BF16)

    return pl.pallas_call(
        body, name="ffn_wgrad_down", grid=(N_DEV, nk),
        in_specs=[pl.BlockSpec((None, tk, FF_BLK), lambda j, k: (j, k, 0)),
                  pl.BlockSpec((tk, D_MODEL), lambda j, k: (k, 0))],
        out_specs=pl.BlockSpec((None, FF_BLK, D_MODEL), lambda j, k: (j, 0, 0)),
        out_shape=jax.ShapeDtypeStruct((N_DEV, FF_BLK, D_MODEL), BF16),
        scratch_shapes=[pltpu.VMEM((FF_BLK, D_MODEL), F32)],
        compiler_params=_params(2),
    )(f, dxb)


def _out_proj_bwd(dh2, x2, r2, g, dx3, w, tm):
    T = x2.shape[0]

    def body(dh_ref, x_ref, r_ref, g_ref, dx3_ref, w_ref, dx_ref, dxb_ref, dg_ref, a_ref, b_ref):
        @pl.when(pl.program_id(0) == 0)
        def _():
            dg_ref[...] = jnp.zeros_like(dg_ref)

        dxn, dgt = _rms_bwd(dh_ref[...], x_ref[...], r_ref[...], g_ref[...])
        dx = dx3_ref[...] + dxn
        dxv = dx.astype(BF16)
        dx_ref[...] = dx
        dxb_ref[...] = dxv
        dg_ref[...] += jnp.sum(dgt, axis=0, keepdims=True)
        a_ref[...] = _dot_nt(dxv, w_ref[0:RET_WIDTH, :])
        b_ref[...] = _dot_nt(dxv, w_ref[RET_WIDTH:D_MODEL, :])

    full = pl.BlockSpec((tm, D_MODEL), lambda i: (i, 0))
    vec = pl.BlockSpec((1, D_MODEL), lambda i: (0, 0))
    half = pl.BlockSpec((tm, RET_WIDTH), lambda i: (i, 0))
    hshape = jax.ShapeDtypeStruct((T, RET_WIDTH), F32)
    return pl.pallas_call(
        body, name="out_proj_bwd", grid=(T // tm,),
        in_specs=[full, full, pl.BlockSpec((tm, 1), lambda i: (i, 0)), vec, full,
                  pl.BlockSpec((D_MODEL, D_MODEL), lambda i: (0, 0))],
        out_specs=[full, full, vec, half, half],
        out_shape=[jax.ShapeDtypeStruct((T, D_MODEL), F32), jax.ShapeDtypeStruct((T, D_MODEL), BF16),
                   jax.ShapeDtypeStruct((1, D_MODEL), F32), hshape, hshape],
        compiler_params=_params(1),
    )(dh2, x2, r2, g, dx3, w)


def _wgrad_rows(name, a, b, tk):
    T, M = a.shape
    N = b.shape[1]
    nk = T // tk

    def body(a_ref, b_ref, o_ref, acc):
        k = pl.program_id(0)

        @pl.when(k == 0)
        def _():
            acc[...] = jnp.zeros_like(acc)

        acc[...] += _dot_tn(a_ref[...], b_ref[...])

        @pl.when(k == nk - 1)
        def _():
            o_ref[...] = acc[...].astype(BF16)

    return pl.pallas_call(
        body, name=name, grid=(nk,),
        in_specs=[pl.BlockSpec((tk, M), lambda k: (k, 0)), pl.BlockSpec((tk, N), lambda k: (k, 0))],
        out_specs=pl.BlockSpec((M, N), lambda k: (0, 0)),
        out_shape=jax.ShapeDtypeStruct((M, N), BF16),
        scratch_shapes=[pltpu.VMEM((M, N), F32)],
        compiler_params=_params(1),
    )(a, b)


def _glu_bwd(y, z, r, dyo, w, og, tm):
    T = y.shape[0]

    def body(y_ref, z_ref, r_ref, d_ref, w_ref, og_ref, dy_ref, dw_ref, db_ref, dog_ref):
        @pl.when(pl.program_id(0) == 0)
        def _():
            dw_ref[...] = jnp.zeros_like(dw_ref)
            db_ref[...] = jnp.zeros_like(db_ref)
            dog_ref[...] = jnp.zeros_like(dog_ref)

        y1, g1 = _gelu_and_grad(y_ref[...])
        sg = _sigmoid(z_ref[...])
        y2 = y1 * sg
        dy2, dogt = _rms_bwd(d_ref[...], y2, r_ref[...], og_ref[...])
        dog_ref[...] += jnp.sum(dogt, axis=0, keepdims=True)
        dz = dy2 * y1 * sg * (1.0 - sg)
        db_ref[...] += jnp.sum(dz, axis=0, keepdims=True)
        dzb = dz.astype(BF16)
        dw_ref[...] += _dot_tn(y1.astype(BF16), dzb)
        dy_ref[...] = (dy2 * sg + _dot_nt(dzb, w_ref[...])) * g1

    row = pl.BlockSpec((tm, SSM_WIDTH), lambda i: (i, 0))
    vec = pl.BlockSpec((1, SSM_WIDTH), lambda i: (0, 0))
    sq = pl.BlockSpec((SSM_WIDTH, SSM_WIDTH), lambda i: (0, 0))
    return pl.pallas_call(
        body, name="glu_bwd", grid=(T // tm,),
        in_specs=[row, row, pl.BlockSpec((tm, 1), lambda i: (i, 0)), row, sq, vec],
        out_specs=[row, sq, vec, vec],
        out_shape=[jax.ShapeDtypeStruct((T, SSM_WIDTH), F32), jax.ShapeDtypeStruct((SSM_WIDTH, SSM_WIDTH), F32),
                   jax.ShapeDtypeStruct((1, SSM_WIDTH), F32), jax.ShapeDtypeStruct((1, SSM_WIDTH), F32)],
        compiler_params=_params(1),
    )(y, z, r, dyo, w, og)


def _s5_bwd(proj, dy, bound, pm, pm_t, bre, bim, bre_t, bim_t, cre, cim, lam, d, carry=None):
    T = proj.shape[0]
    nt = T // S5_TILE
    sp = _s5_specs(T, True)

    def body(u_ref, dy_ref, bound_ref, pm_ref, pmt_ref, bre_ref, bim_ref, bret_ref, bimt_ref, cre_ref, cim_ref,
             lam_ref, d_ref,
             du_ref, dbre_ref, dbim_ref, dcre_ref, dcim_ref, dlam_ref, dd_ref, carry, ptab, sr, si, gr, gi):
        lr = lam_ref[0:1, :]
        li = lam_ref[1:2, :]

        @pl.when(pl.program_id(1) == 0)
        def _():
            carry[...] = jnp.zeros_like(carry)
            _fill_power_table(ptab, lr, li)
            for ref in (dbre_ref, dbim_ref, dcre_ref, dcim_ref, dlam_ref, dd_ref):
                ref[...] = jnp.zeros_like(ref)

        def one_tile(u_in, dy_in, b_r, b_i):
            u = _permute_rows_f32(pm_ref[...], u_in)
            ub = u.astype(BF16)
            dyv = _permute_rows_f32(pm_ref[...], dy_in)
            dyb = dyv.astype(BF16)
            _tile_set(sr, _dot(ub, bre_ref[...]))
            _tile_set(si, _dot(ub, bim_ref[...]))
            er, ei, _, _ = _s5_forward_states(sr, si, lr, li, b_r, b_i, ptab)
            _tile_set(gr, _dot(dyb, cre_ref[...]))
            _tile_set(gi, -_dot(dyb, cim_ref[...]))
            zr, zi = _chunk_scans(gr, gi, lr, -li, True)
            ar, ai = _table_rows(ptab, S5_STEPS - 1, True)
            fr, fi = _entering_states(zr, zi, carry[0:1, :], carry[1:2, :], ar, ai, True)
            acc_r = jnp.zeros((S5_CHUNKS, S5_LANES), F32)
            acc_i = jnp.zeros((S5_CHUNKS, S5_LANES), F32)
            for j in range(S5_STEPS):
                qr, qi = _table_rows(ptab, S5_STEPS - 1 - j, True)
                g_r = _step_get(gr, j) + qr * fr - qi * fi
                g_i = _step_get(gi, j) + qr * fi + qi * fr
                _step_set(gr, j, g_r)
                _step_set(gi, j, g_i)
                p_r, p_i = (er, ei) if j == 0 else (_step_get(sr, j - 1), _step_get(si, j - 1))
                acc_r += g_r * p_r + g_i * p_i
                acc_i += g_i * p_r - g_r * p_i
            dlam_ref[0:1, :] += jnp.sum(acc_r, axis=0, keepdims=True)
            dlam_ref[1:2, :] += jnp.sum(acc_i, axis=0, keepdims=True)
            g_all_r = _tile_get(gr)
            g_all_i = _tile_get(gi)
            carry[0:1, :] = g_all_r[0:1, :]
            carry[1:2, :] = g_all_i[0:1, :]
            grb = g_all_r.astype(BF16)
            gib = g_all_i.astype(BF16)
            du = (_dot(grb, bret_ref[...]) + _dot(gib, bimt_ref[...]) + d_ref[...] * dyv).astype(BF16)
            dbre_ref[...] += _dot_tn(grb, ub)
            dbim_ref[...] += _dot_tn(gib, ub)
            dcre_ref[...] += _dot_tn(dyb, _tile_get(sr).astype(BF16))
            dcim_ref[...] -= _dot_tn(dyb, _tile_get(si).astype(BF16))
            dd_ref[...] += jnp.sum(dyv * u, axis=0, keepdims=True)
            return _dot(pmt_ref[...], du).astype(BF16)

        for s in reversed(range(S5_PER_STEP)):
            rows = slice(s * S5_TILE, (s + 1) * S5_TILE)
            du_ref[rows, :] = one_tile(u_ref[rows, :], dy_ref[rows, :], bound_ref[s, 0:1, :], bound_ref[s, 1:2, :])

    acc_ts = pl.BlockSpec((None, S5_LANES, LANE), lambda b, t: (b, 0, 0))
    acc_fs = pl.BlockSpec((None, LANE, S5_LANES), lambda b, t: (b, 0, 0))
    return _pcall(
        body, "s5_bwd", (S5_NBLK, nt // S5_PER_STEP),
        [sp["u"], sp["rows"], sp["bound"], sp["perm"], sp["perm"], sp["to_state"], sp["to_state"],
         sp["from_state"], sp["from_state"], sp["to_state"], sp["to_state"], sp["lam"], sp["d"]],
        [sp["rows"], acc_ts, acc_ts, acc_fs, acc_fs, sp["lam"], sp["d"]],
        [jax.ShapeDtypeStruct((T, SSM_WIDTH), BF16),
         jax.ShapeDtypeStruct((S5_NBLK, S5_LANES, LANE), F32),
         jax.ShapeDtypeStruct((S5_NBLK, S5_LANES, LANE), F32),
         jax.ShapeDtypeStruct((S5_NBLK, LANE, S5_LANES), F32),
         jax.ShapeDtypeStruct((S5_NBLK, LANE, S5_LANES), F32),
         jax.ShapeDtypeStruct((S5_NBLK, 2, S5_LANES), F32),
         jax.ShapeDtypeStruct((1, SSM_WIDTH), F32)],
        [pltpu.VMEM((2, S5_LANES), F32), pltpu.VMEM((2, S5_TILE, S5_LANES), F32)]
        + [pltpu.VMEM(S5_STATE_TILE, F32)] * 4,
        (proj, dy, bound, pm, pm_t, bre, bim, bre_t, bim_t, cre, cim, lam, d), carry)


def _ret_bwd(proj, cosf, sinf, mask, rowdec, kdec, gtb, gn, sblk, dyr):
    T = proj.shape[0]
    nb = T // RET_BLOCK
    sp = _ret_specs(T, True)

    def body(q_ref, k_ref, v_ref, g_ref, cos_ref, sin_ref, mask_ref, rd_ref, kd_ref, gtb_ref, gn_ref, sb_ref, dy_ref,
             dq_ref, dk_ref, dv_ref, dg_ref, dgn_ref, dst):
        @pl.when(pl.program_id(1) == 0)
        def _():
            dst[...] = jnp.zeros_like(dst)
            dgn_ref[...] = jnp.zeros_like(dgn_ref)

        s_in = sb_ref[...]
        q, k, qb, kb, vb, pm, qd, o = _ret_common(q_ref, k_ref, v_ref, cos_ref, sin_ref, mask_ref, rd_ref, s_in)
        mu = jnp.mean(o, axis=-1, keepdims=True)
        oc = o - mu
        rstd = lax.rsqrt(jnp.mean(oc * oc, axis=-1, keepdims=True) + EPS)
        n = oc * rstd
        gt = g_ref[...]
        sg = _sigmoid(gt)
        sil = gt * sg
        gnv = gn_ref[...]
        dyv = dy_ref[...]
        dg_ref[...] = (dyv * (n * gnv) * (sg * (1.0 + gt * (1.0 - sg)))).astype(BF16)
        dgn_ref[...] += jnp.sum(dyv * sil * n, axis=0, keepdims=True)
        dn = dyv * sil * gnv
        do = rstd * (dn - jnp.mean(dn, axis=-1, keepdims=True) - n * jnp.mean(dn * n, axis=-1, keepdims=True))
        dob = do.astype(BF16)
        ds = dst[...]
        dsb = ds.astype(BF16)
        kd = kd_ref[...]
        rd = rd_ref[...]
        dv_ref[...] = (_dot_tn(pm, dob) + _dot((k * kd).astype(BF16), dsb)).astype(BF16)
        dpb = (_dot_nt(dob, vb) * mask_ref[...]).astype(BF16)
        dq = _dot(dpb, kb) + _dot_nt(dob, s_in.astype(BF16)) * rd
        dk = (_dot_tn(dpb, qb) + _dot_nt(vb, dsb) * kd) * (HEAD_DIM ** -0.5)
        dst[...] = gtb_ref[...] * ds + _dot_tn(qd, dob)
        c = cos_ref[...]
        s = sin_ref[...]
        dq_ref[...] = (dq * c + pltpu.roll(dq * s, HEAD_DIM // 2, 1)).astype(BF16)
        dk_ref[...] = (dk * c + pltpu.roll(dk * s, HEAD_DIM // 2, 1)).astype(BF16)

    oshape = jax.ShapeDtypeStruct((T, RET_WIDTH), BF16)
    ins = [sp[n] for n in ("q", "k", "v", "g", "tab", "tab", "mask", "dec", "dec", "gtb", "gn", "state", "rows")]
    outs = [sp["rows"], sp["rows"], sp["rows"], sp["rows"], sp["gn"]]
    return pl.pallas_call(
        _per_head(body, [kind for _, kind in ins + outs + [sp["scratch"]]]), name="ret_bwd",
        grid=(RET_HEADS // RET_HPS, nb), in_specs=[s for s, _ in ins], out_specs=[s for s, _ in outs],
        out_shape=[oshape, oshape, oshape, oshape, jax.ShapeDtypeStruct((1, RET_WIDTH), F32)],
        scratch_shapes=[sp["scratch"][0]],
        compiler_params=_params(2),
    )(proj, proj, proj, proj, cosf, sinf, mask, rowdec, kdec, gtb, gn, sblk, dyr)


def _in_proj_bwd(dproj, w, x, r1, g, dx2, tm, carry=None):
    T = x.shape[0]

    def body(dp_ref, w_hbm, x_ref, r_ref, g_ref, dx2_ref, gx_ref, dg_ref, w_ref, sem):
        @pl.when(pl.program_id(0) == 0)
        def _():
            _load_resident(w_hbm, w_ref, sem)
            dg_ref[...] = jnp.zeros_like(dg_ref)

        dh = _dot_nt(dp_ref[:, 0:WIN_BLK], w_ref[0])
        for k in range(1, N_DEV):
            dh = dh + _dot_nt(dp_ref[:, k * WIN_BLK:(k + 1) * WIN_BLK], w_ref[k])
        dxn, dgt = _rms_bwd(dh, x_ref[...], r_ref[...], g_ref[...])
        gx_ref[...] = dx2_ref[...] + dxn
        dg_ref[...] += jnp.sum(dgt, axis=0, keepdims=True)

    full = pl.BlockSpec((tm, D_MODEL), lambda i: (i, 0))
    vec = pl.BlockSpec((1, D_MODEL), lambda i: (0, 0))
    return _pcall(
        body, "in_proj_bwd", (T // tm,),
        [pl.BlockSpec((tm, IN_WIDTH), lambda i: (i, 0)), ANY_SPEC,
         full, pl.BlockSpec((tm, 1), lambda i: (i, 0)), vec, full],
        [full, vec],
        [jax.ShapeDtypeStruct((T, D_MODEL), F32), jax.ShapeDtypeStruct((1, D_MODEL), F32)],
        [pltpu.VMEM(w.shape, w.dtype), pltpu.SemaphoreType.DMA], (dproj, w, x, r1, g, dx2), carry)


def _in_proj_wgrad(h, dproj, tk, carry=None):
    T = h.shape[0]
    nk = T // tk

    def body(h_ref, dp_ref, o_ref, acc):
        k = pl.program_id(1)

        @pl.when(k == 0)
        def _():
            acc[...] = jnp.zeros_like(acc)

        acc[...] += _dot_tn(h_ref[...], dp_ref[...])

        @pl.when(k == nk - 1)
        def _():
            o_ref[...] = acc[...].astype(BF16)

    return _pcall(
        body, "in_proj_wgrad", (N_DEV, nk),
        [pl.BlockSpec((tk, D_MODEL), lambda j, k: (k, 0)), pl.BlockSpec((tk, WIN_BLK), lambda j, k: (k, j))],
        [pl.BlockSpec((None, D_MODEL, WIN_BLK), lambda j, k: (j, 0, 0))],
        [jax.ShapeDtypeStruct((N_DEV, D_MODEL, WIN_BLK), BF16)],
        [pltpu.VMEM((D_MODEL, WIN_BLK), F32)], (h, dproj), carry)


def _rope_tables(T):
    half = HEAD_DIM // 2
    freqs = ROPE_BASE ** (-jnp.arange(half, dtype=F32) / half)
    ang = jnp.arange(T, dtype=F32)[:, None] * freqs[None, :]
    c = jnp.cos(ang)
    s = jnp.sin(ang)
    return jnp.concatenate([c, c], axis=1), jnp.concatenate([-s, s], axis=1)


def _retention_tables():
    hh = jnp.arange(RET_HEADS, dtype=F32)
    log_g = jnp.log1p(-(2.0 ** (-5.0 - hh)))[:, None, None]
    i = jnp.arange(RET_BLOCK)
    ci = (i // CHUNK)[:, None]
    cj = (i // CHUNK)[None, :]
    diff = (i[:, None] - i[None, :]).astype(F32)
    expo = jnp.where(ci == cj, jnp.abs(diff), diff)
    mask = jnp.where((cj <= ci)[None], jnp.exp(log_g * expo[None]), 0.0)
    r = jnp.arange(RET_BLOCK, dtype=F32)[None, :, None]
    ones = jnp.ones((1, 1, HEAD_DIM), F32)
    rowdec = jnp.exp(log_g * (r + 1.0)) * ones
    kdec = jnp.exp(log_g * (RET_BLOCK - 1.0 - r)) * ones
    gtb = jnp.exp(log_g * float(RET_BLOCK)) * ones
    return mask, rowdec, kdec, gtb


def _s5_discretise(a_re, a_im, log_dt, b_re, b_im):
    lam = lax.complex(a_re, a_im)
    dt = jnp.exp(log_dt)[:, None]
    lam_bar = jnp.exp(lam * dt)
    b_bar = ((lam_bar - 1.0) / lam)[..., None] * lax.complex(b_re, b_im)
    return jnp.real(lam_bar), jnp.imag(lam_bar), jnp.real(b_bar), jnp.imag(b_bar)


def _to_state_blockdiag(m):
    eye = jnp.eye(S5_GB, dtype=m.dtype)
    t = jnp.einsum("bgpc,gh->bgchp", m.reshape(S5_NBLK, S5_GB, SSM_STATE, SSM_GROUP), eye)
    return t.reshape(S5_NBLK, LANE, S5_LANES)


def _from_state_blockdiag(m):
    eye = jnp.eye(S5_GB, dtype=m.dtype)
    t = jnp.einsum("bgcp,gh->bgphc", m.reshape(S5_NBLK, S5_GB, SSM_GROUP, SSM_STATE), eye)
    return t.reshape(S5_NBLK, S5_LANES, LANE)


def _diag_of_state_major(acc):
    eye = jnp.eye(S5_GB, dtype=acc.dtype)
    t = acc.reshape(S5_NBLK, S5_GB, SSM_STATE, S5_GB, SSM_GROUP)
    return jnp.einsum("bgphc,gh->bgpc", t, eye).reshape(SSM_GROUPS, SSM_STATE, SSM_GROUP)


def _diag_of_channel_major(acc):
    eye = jnp.eye(S5_GB, dtype=acc.dtype)
    t = acc.reshape(S5_NBLK, S5_GB, SSM_GROUP, S5_GB, SSM_STATE)
    return jnp.einsum("bgchp,gh->bgcp", t, eye).reshape(SSM_GROUPS, SSM_GROUP, SSM_STATE)


SMALL_PARTIALS = (("ret_gn_g", 1024), ("lam_re", 4096), ("lam_im", 4096),
                  ("bbar_re", 65536), ("bbar_im", 65536), ("c_re", 65536), ("c_im", 65536),
                  ("ssm_d", 1024), ("b_glu", 1024), ("out_g", 1024), ("norm_ffn_g", 2048), ("norm_final_g", 2048))


def _forward_backward(x, tgt, shards, sm):
    T = x.shape[0]
    tm = min(1024, T)
    cosf, sinf = _rope_tables(T)
    mask, rowdec, kdec, gtb = _retention_tables()
    lbr, lbi, bbr, bbi = _s5_discretise(sm["ssm_a_re"], sm["ssm_a_im"], sm["ssm_log_dt"], sm["ssm_b_re"],
                                        sm["ssm_b_im"])
    bre = _to_state_blockdiag(bbr).astype(BF16)
    bim = _to_state_blockdiag(bbi).astype(BF16)
    cre_t = _from_state_blockdiag(sm["ssm_c_re"]).astype(BF16)
    cim_t = _from_state_blockdiag(sm["ssm_c_im"]).astype(BF16)
    bre_t = jnp.swapaxes(bre, 1, 2)
    bim_t = jnp.swapaxes(bim, 1, 2)
    cre = jnp.swapaxes(cre_t, 1, 2)
    cim = jnp.swapaxes(cim_t, 1, 2)
    lam = jnp.stack([lbr.reshape(S5_NBLK, S5_LANES), lbi.reshape(S5_NBLK, S5_LANES)], axis=1)
    pm = _step_major_permutation()
    pm_t = pm.T
    row = lambda v: v.reshape(1, -1)
    g_mix, g_ffn, g_fin = row(sm["norm_mix_g"]), row(sm["norm_ffn_g"]), row(sm["norm_final_g"])
    gn, dsk, bglu, og = row(sm["ret_gn_g"]), row(sm["ssm_d"]), row(sm["ssm_b_glu"]), row(sm["ssm_out_g"])

    (w_in,) = _exchange_call("weight_gather", [shards["w_in"]], True, via_sibling=True)
    proj, h1, r1, w_gate = _in_proj_fwd(x, g_mix, w_in, 256, _Exchange([shards["w_gate"]], True, via_sibling=True))
    y_ret, sblk, w_glu, w_out = _ret_fwd(proj, cosf, sinf, mask, rowdec, kdec, gtb, gn,
                                         _Exchange([shards["ssm_w_glu"], shards["w_out"]], True))
    w_glu = w_glu.reshape(SSM_WIDTH, SSM_WIDTH)
    w_out = w_out.reshape(D_MODEL, D_MODEL)
    y_s5, bound, w_up = _s5_fwd(proj, pm, pm_t, bre, bim, cre_t, cim_t, lam, dsk, _Exchange([shards["w_up"]], True))
    z, y_ssm, r_ssm = _glu_fwd(y_s5, w_glu, bglu, og, 256)
    x2, h2, r2 = _out_proj_fwd(x, y_ret, y_ssm, w_out, g_ffn, 256)
    a, b, f, w_down = _ffn_up(h2, w_gate, w_up, tm, _Exchange([shards["w_down"]], True))
    dx3, dx3b, loss8, dg_fin = _ffn_down_loss(f, w_down, x2, tgt, g_fin, 256)

    landed = {}
    da, db = _ffn_bwd_act(dx3b, w_down, a, b, tm)
    dw_down = _ffn_wgrad_down(f, dx3b, tm)
    dw_gate, dw_up, landed["w_down"] = _ffn_wgrad_up(h2, da, db, tm, _Exchange([dw_down], False))
    dh2, landed["w_gate"] = _ffn_bwd_in(da, db, w_gate, w_up, min(1024, T), _Exchange([dw_gate], False))
    dx2, dx2b, dg_ffn, dy_ret, dy_ssm = _out_proj_bwd(dh2, x2, r2, g_ffn, dx3, w_out, 256)
    dw_out = jnp.concatenate([_wgrad_rows("out_proj_wgrad_ret", y_ret, dx2b, tm),
                              _wgrad_rows("out_proj_wgrad_ssm", y_ssm, dx2b, tm)], axis=0)
    dy_s5, dw_glu, db_glu, dog = _glu_bwd(y_s5, z, r_ssm, dy_ssm, w_glu, og, 256)
    du, dbre, dbim, dcre, dcim, dlam, dd, landed["w_up"] = _s5_bwd(
        proj, dy_s5, bound, pm, pm_t, bre, bim, bre_t, bim_t, cre, cim, lam, dsk, _Exchange([dw_up], False))
    dq, dk, dv, dgate, dgn = _ret_bwd(proj, cosf, sinf, mask, rowdec, kdec, gtb, gn, sblk, dy_ret)
    dproj = jnp.concatenate([dq, dk, dv, dgate, du], axis=1)
    small = dict(ret_gn_g=dgn, lam_re=dlam[:, 0], lam_im=dlam[:, 1],
                 bbar_re=_diag_of_state_major(dbre), bbar_im=_diag_of_state_major(dbim),
                 c_re=_diag_of_channel_major(dcre), c_im=_diag_of_channel_major(dcim),
                 ssm_d=dd, b_glu=db_glu, out_g=dog, norm_ffn_g=dg_ffn, norm_final_g=dg_fin)
    packed = _pack([small[n] for n, _ in SMALL_PARTIALS])
    dw_in, landed["w_out"], landed["ssm_w_glu"], small_landed = _in_proj_wgrad(
        h1, dproj, tm, _Exchange([dw_out.reshape(N_DEV, D_MODEL // N_DEV, D_MODEL),
                                  dw_glu.astype(BF16).reshape(N_DEV, SSM_WIDTH // N_DEV, SSM_WIDTH), packed],
                                 [False, False, True]))
    grad_x, dg_mix, landed["w_in"] = _in_proj_bwd(dproj, w_in, x, r1, g_mix, dx2, 256, _Exchange([dw_in], False))
    (mix_landed,) = _exchange_call("mix_gain_grad_gather", [_pack([dg_mix])], True)
    summed = dict(zip([n for n, _ in SMALL_PARTIALS],
                      _unpack(_sum_partials("small_grad_sum", small_landed), [(sz,) for _, sz in SMALL_PARTIALS])))
    summed["norm_mix_g"] = _sum_partials("mix_gain_grad_sum", mix_landed).reshape(-1)
    return loss8[0, 0], grad_x, landed, summed


def _small_grads(summed, sm):
    _, vjp = jax.vjp(_s5_discretise, sm["ssm_a_re"], sm["ssm_a_im"], sm["ssm_log_dt"], sm["ssm_b_re"], sm["ssm_b_im"])
    gp = (SSM_GROUPS, SSM_STATE)
    da_re, da_im, dlog_dt, db_re, db_im = vjp((summed["lam_re"].reshape(gp), summed["lam_im"].reshape(gp),
                                               summed["bbar_re"].reshape(gp + (SSM_GROUP,)),
                                               summed["bbar_im"].reshape(gp + (SSM_GROUP,))))
    return dict(norm_mix_g=summed["norm_mix_g"], ret_gn_g=summed["ret_gn_g"], ssm_a_re=da_re, ssm_a_im=da_im,
                ssm_log_dt=dlog_dt, ssm_b_re=db_re, ssm_b_im=db_im,
                ssm_c_re=summed["c_re"].reshape(SSM_GROUPS, SSM_GROUP, SSM_STATE),
                ssm_c_im=summed["c_im"].reshape(SSM_GROUPS, SSM_GROUP, SSM_STATE),
                ssm_d=summed["ssm_d"], ssm_b_glu=summed["b_glu"], ssm_out_g=summed["out_g"],
                norm_ffn_g=summed["norm_ffn_g"], norm_final_g=summed["norm_final_g"])


def _adamw_math(w, g, m, v):
    m2 = ADAM_B1 * m + (1.0 - ADAM_B1) * g
    v2 = ADAM_B2 * v + (1.0 - ADAM_B2) * (g * g)
    delta = -ADAM_LR * ((m2 / ADAM_BC1) / (jnp.sqrt(v2 / ADAM_BC2) + ADAM_EPS) + ADAM_WD * w)
    return delta, m2, v2


def _adamw_shard(name, parts, w, m, v, tr):
    rows, cols = w.shape

    def body(p_ref, w_ref, m_ref, v_ref, g_ref, d_ref, m2_ref, v2_ref):
        g = p_ref[0].astype(F32)
        for s in range(1, N_DEV):
            g = g + p_ref[s].astype(F32)
        d, m2, v2 = _adamw_math(w_ref[...], g, m_ref[...], v_ref[...])
        g_ref[...] = g
        d_ref[...] = d
        m2_ref[...] = m2
        v2_ref[...] = v2

    blk = pl.BlockSpec((tr, cols), lambda i: (i, 0))
    oshape = jax.ShapeDtypeStruct((rows, cols), F32)
    return pl.pallas_call(
        body, name=name, grid=(rows // tr,),
        in_specs=[pl.BlockSpec((N_DEV, tr, cols), lambda i: (0, i, 0)), blk, blk, blk],
        out_specs=[blk, blk, blk, blk], out_shape=[oshape] * 4,
        compiler_params=_params(1),
    )(parts, w, m, v)


def _sum_partials(name, parts):
    rows = parts.shape[1]

    def body(p_ref, o_ref):
        g = p_ref[0]
        for s in range(1, N_DEV):
            g = g + p_ref[s]
        o_ref[...] = g

    return pl.pallas_call(
        body, name=name, grid=(1,),
        in_specs=[pl.BlockSpec((N_DEV, rows, LANE), lambda i: (0, 0, 0))],
        out_specs=pl.BlockSpec((rows, LANE), lambda i: (0, 0)),
        out_shape=jax.ShapeDtypeStruct((rows, LANE), F32),
        compiler_params=_params(1),
    )(parts)


def _adamw_small(ws, gs, ms, vs):
    n = len(ws)

    def body(*refs):
        for i in range(n):
            w_ref, g_ref, m_ref, v_ref = (refs[k * n + i] for k in range(4))
            d_ref, m2_ref, v2_ref = (refs[(4 + k) * n + i] for k in range(3))
            d, m2, v2 = _adamw_math(w_ref[...], g_ref[...], m_ref[...], v_ref[...])
            d_ref[...] = d
            m2_ref[...] = m2
            v2_ref[...] = v2

    vmem = pl.BlockSpec(memory_space=pltpu.VMEM)
    out = pl.pallas_call(
        body, name="adamw_small", in_specs=[vmem] * (4 * n), out_specs=[vmem] * (3 * n),
        out_shape=[jax.ShapeDtypeStruct(w.shape, F32) for w in ws] * 3,
        compiler_params=pltpu.CompilerParams(vmem_limit_bytes=VMEM_LIMIT),
    )(*ws, *gs, *ms, *vs)
    return out[:n], out[n:2 * n], out[2 * n:]


def _pack(arrays):
    parts = [a.reshape(-1, LANE) for a in arrays]
    assert all(p.shape[0] % 8 == 0 for p in parts)
    return parts[0] if len(parts) == 1 else jnp.concatenate(parts, axis=0)


def _unpack(packed, shapes):
    flat = packed.reshape(-1)
    out, off = [], 0
    for shp in shapes:
        n = math.prod(shp)
        out.append(flat[off:off + n].reshape(shp))
        off += n + ((-n) % LANE)
    return out


WEIGHTS = ("norm_mix_g", "w_in", "ret_gn_g", "ssm_a_re", "ssm_a_im", "ssm_log_dt", "ssm_b_re", "ssm_b_im",
           "ssm_c_re", "ssm_c_im", "ssm_d", "ssm_w_glu", "ssm_b_glu", "ssm_out_g", "w_out", "norm_ffn_g", "w_gate",
           "w_up", "w_down", "norm_final_g")
BIG = ("w_in", "ssm_w_glu", "w_out", "w_gate", "w_up", "w_down")
SMALL = tuple(n for n in WEIGHTS if n not in BIG)
ADAM_ROWS = {"w_in": 256, "ssm_w_glu": 128, "w_out": 128, "w_gate": 256, "w_up": 256, "w_down": 176}


def kernel(x, norm_mix_g, w_in, ret_gn_g, ssm_a_re, ssm_a_im, ssm_log_dt, ssm_b_re, ssm_b_im, ssm_c_re, ssm_c_im, ssm_d, ssm_w_glu, ssm_b_glu, ssm_out_g, w_out, norm_ffn_g, w_gate, w_up, w_down, norm_final_g, loss_target, m_norm_mix_g, m_w_in, m_ret_gn_g, m_ssm_a_re, m_ssm_a_im, m_ssm_log_dt, m_ssm_b_re, m_ssm_b_im, m_ssm_c_re, m_ssm_c_im, m_ssm_d, m_ssm_w_glu, m_ssm_b_glu, m_ssm_out_g, m_w_out, m_norm_ffn_g, m_w_gate, m_w_up, m_w_down, m_norm_final_g, v_norm_mix_g, v_w_in, v_ret_gn_g, v_ssm_a_re, v_ssm_a_im, v_ssm_log_dt, v_ssm_b_re, v_ssm_b_im, v_ssm_c_re, v_ssm_c_im, v_ssm_d, v_ssm_w_glu, v_ssm_b_glu, v_ssm_out_g, v_w_out, v_norm_ffn_g, v_w_gate, v_w_up, v_w_down, v_norm_final_g):
    given = dict(locals())
    w = {n: given[n] for n in WEIGHTS}
    m = {n: given["m_" + n] for n in WEIGHTS}
    v = {n: given["v_" + n] for n in WEIGHTS}
    drop = lambda n, a: a if n == "norm_final_g" else a[0]
    w0 = {n: drop(n, w[n]) for n in WEIGHTS}
    m0 = {n: drop(n, m[n]) for n in WEIGHTS}
    v0 = {n: drop(n, v[n]) for n in WEIGHTS}

    sm = {n: w0[n] for n in SMALL}
    shards = {n: w0[n].astype(BF16) for n in BIG}
    loss_local, grad_x, landed, summed = _forward_backward(x[0], loss_target[0], shards, sm)
    loss = lax.psum(loss_local, MESH_AXES)
    gsmall = _small_grads(summed, sm)

    grads, delta, new_m, new_v = {}, {}, {}, {}
    for n in BIG:
        g, d, m2, v2 = _adamw_shard("adamw_" + n, landed[n], w0[n], m0[n], v0[n], ADAM_ROWS[n])
        grads[n], delta[n], new_m[n], new_v[n] = g, d, m2, v2
    as_given = lambda n, a: a.reshape(1, -1) if n == "norm_final_g" else a.reshape(w[n].shape)
    gs = [as_given(n, gsmall[n]) for n in SMALL]
    ds, m2s, v2s = _adamw_small([as_given(n, w[n]) for n in SMALL], gs, [as_given(n, m[n]) for n in SMALL],
                                [as_given(n, v[n]) for n in SMALL])
    for n, g, d, m2, v2 in zip(SMALL, gs, ds, m2s, v2s):
        grads[n], delta[n], new_m[n], new_v[n] = g, d, m2, v2

    lift = lambda n, a: a.reshape(w[n].shape)
    return (loss, grad_x[None], *[lift(n, grads[n]) for n in WEIGHTS], *[lift(n, delta[n]) for n in WEIGHTS],
            *[lift(n, new_m[n]) for n in WEIGHTS], *[lift(n, new_v[n]) for n in WEIGHTS])
```

```python
import functools
import math

import jax
import jax.numpy as jnp
from jax import lax
from jax.experimental import pallas as pl
from jax.experimental.pallas import tpu as pltpu

F32 = jnp.float32
BF16 = jnp.bfloat16

D_MODEL = 2048
RET_WIDTH = 1024
RET_HEADS = 8
HEAD_DIM = 128
CHUNK = 64
SSM_WIDTH = 1024
SSM_GROUP = 16
SSM_GROUPS = 64
SSM_STATE = 64
D_FF = 5632
IN_WIDTH = 5120
ROPE_BASE = 10000.0
EPS = 1e-6
N_DEV = 8
MESH_AXES = ("x", "y", "c")

WIN_BLK = IN_WIDTH // N_DEV
FF_BLK = D_FF // N_DEV
RET_BLOCK = 256
RET_HPS = 4
S5_TILE = 256
S5_CHUNKS = 8
S5_STEPS = S5_TILE // S5_CHUNKS
S5_PER_STEP = 4
S5_GB = 8
S5_NBLK = SSM_GROUPS // S5_GB
S5_LANES = S5_GB * SSM_STATE
LANE = 128

ADAM_LR = 0.001
ADAM_B1 = 0.9
ADAM_B2 = 0.999
ADAM_EPS = 1e-08
ADAM_WD = 0.01
ADAM_STEP = 10
ADAM_BC1 = 1.0 - ADAM_B1 ** ADAM_STEP
ADAM_BC2 = 1.0 - ADAM_B2 ** ADAM_STEP

VMEM_LIMIT = 56 * 1024 * 1024

NT = (((1,), (1,)), ((), ()))
TN = (((0,), (0,)), ((), ()))


def _params(n_grid):
    return pltpu.CompilerParams(dimension_semantics=("arbitrary",) * n_grid, vmem_limit_bytes=VMEM_LIMIT)


def _dot(a, b):
    return jnp.dot(a, b, preferred_element_type=F32)


def _dot_nt(a, b):
    return lax.dot_general(a, b, NT, preferred_element_type=F32)


def _dot_tn(a, b):
    return lax.dot_general(a, b, TN, preferred_element_type=F32)


def _sigmoid(x):
    return 1.0 / (1.0 + jnp.exp(-x))


_GELU_C = math.sqrt(2.0 / math.pi)
_GELU_A = 0.044715


def _gelu(x):
    t = jnp.tanh(_GELU_C * (x + _GELU_A * x * x * x))
    return 0.5 * x * (1.0 + t)


def _gelu_and_grad(x):
    t = jnp.tanh(_GELU_C * (x + _GELU_A * x * x * x))
    g = 0.5 * (1.0 + t) + 0.5 * x * (1.0 - t * t) * _GELU_C * (1.0 + 3.0 * _GELU_A * x * x)
    return 0.5 * x * (1.0 + t), g


def _rms_bwd(dy, x, r, g):
    w = dy * g
    dx = r * w - x * (r * r * r) * jnp.mean(w * x, axis=-1, keepdims=True)
    return dx, dy * x * r


HBM_SPEC = pl.BlockSpec(memory_space=pltpu.HBM)
ANY_SPEC = pl.BlockSpec(memory_space=pl.ANY)


def _load_resident(src_hbm, dst_vmem, sem):
    cp = pltpu.make_async_copy(src_hbm, dst_vmem, sem)
    cp.start()
    cp.wait()


def _my_block():
    return 4 * lax.axis_index("x") + 2 * lax.axis_index("y") + lax.axis_index("c")


def _peer(k):
    px = lax.axis_index("x") ^ ((k >> 2) & 1)
    py = lax.axis_index("y") ^ ((k >> 1) & 1)
    pc = lax.axis_index("c") ^ (k & 1)
    return (px, py, pc), 4 * px + 2 * py + pc


class _Exchange:
    def __init__(self, payloads, gather, via_sibling=False):
        self.payloads = list(payloads)
        self.n = len(self.payloads)
        self.gather = [gather] * self.n if isinstance(gather, bool) else list(gather)
        self.via_sibling = via_sibling
        assert not via_sibling or all(self.gather)

    def out_shape(self):
        return [jax.ShapeDtypeStruct(((N_DEV,) if g else ()) + p.shape, p.dtype)
                for p, g in zip(self.payloads, self.gather)]

    def scratch_shapes(self):
        return [pltpu.SemaphoreType.DMA((self.n, N_DEV - 1)), pltpu.SemaphoreType.DMA((self.n, N_DEV - 1)),
                pltpu.SemaphoreType.DMA((self.n,))]

    def _copies(self, ins, outs, sems, incoming):
        send_sems, recv_sems, local_sems = sems
        me = _my_block()
        src_of = lambda i, blk: ins[i] if self.gather[i] else ins[i].at[blk]
        local, remote = [], []
        for i in range(self.n):
            if not incoming:
                local.append(pltpu.make_async_copy(src_of(i, me), outs[i].at[me], local_sems.at[i]))
            for k in range(1, N_DEV):
                dev, blk = _peer(k)
                src, dst = (outs[i].at[blk], outs[i].at[blk]) if incoming else (src_of(i, blk), outs[i].at[me])
                remote.append(pltpu.make_async_remote_copy(
                    src_ref=src, dst_ref=dst, send_sem=send_sems.at[i, k - 1], recv_sem=recv_sems.at[i, k - 1],
                    device_id=dev, device_id_type=pl.DeviceIdType.MESH))
        return local, remote

    def _copy(self, i, k, outs, sems, src, dst_blk, to_k):
        send_sems, recv_sems, _ = sems
        return pltpu.make_async_remote_copy(
            src_ref=src, dst_ref=outs[i].at[dst_blk], send_sem=send_sems.at[i, k - 1], recv_sem=recv_sems.at[i, k - 1],
            device_id=_peer(to_k)[0], device_id_type=pl.DeviceIdType.MESH)

    FIRST_HOPS = (1, 2, 4, 6)
    FROM_CHIPS = (2, 4, 6)

    def start(self, ins, outs, sems):
        if not self.via_sibling:
            local, sends = self._copies(ins, outs, sems, False)
            for cp in local + sends:
                cp.start()
            return
        me = _my_block()
        for i in range(self.n):
            pltpu.make_async_copy(ins[i], outs[i].at[me], sems[2].at[i]).start()
            for k in self.FIRST_HOPS:
                self._copy(i, k, outs, sems, ins[i], me, k).start()

    def wait(self, ins, outs, sems):
        if not self.via_sibling:
            for cp in self._copies(ins, outs, sems, True)[1]:
                cp.wait_recv()
            local, sends = self._copies(ins, outs, sems, False)
            for cp in sends:
                cp.wait_send()
            for cp in local:
                cp.wait()
            return
        me = _my_block()
        landed = lambda i, k: self._copy(i, k, outs, sems, outs[i].at[_peer(k)[1]], _peer(k)[1], k)
        for i in range(self.n):
            for s in self.FROM_CHIPS:
                landed(i, s).wait_recv()
                self._copy(i, s ^ 1, outs, sems, outs[i].at[_peer(s)[1]], _peer(s)[1], 1).start()
        for i in range(self.n):
            for k in (1, 3, 5, 7):
                landed(i, k).wait_recv()
            for k in self.FIRST_HOPS:
                self._copy(i, k, outs, sems, ins[i], me, k).wait_send()
            for s in self.FROM_CHIPS:
                self._copy(i, s ^ 1, outs, sems, outs[i].at[_peer(s)[1]], _peer(s)[1], 1).wait_send()
            pltpu.make_async_copy(ins[i], outs[i].at[me], sems[2].at[i]).wait()


def _pcall(body, name, grid, in_specs, out_specs, out_shape, scratch_shapes, args, carry=None):
    n_in, n_out, n_scr = len(in_specs), len(out_specs), len(scratch_shapes)
    if carry is None:
        return pl.pallas_call(body, name=name, grid=grid, in_specs=in_specs, out_specs=out_specs, out_shape=out_shape,
                              scratch_shapes=scratch_shapes, compiler_params=_params(len(grid)))(*args)
    nx = carry.n

    def wrapped(*refs):
        cin, xin = refs[:n_in], refs[n_in:n_in + nx]
        cout, xout = refs[n_in + nx:n_in + nx + n_out], refs[n_in + nx + n_out:n_in + 2 * nx + n_out]
        rest = refs[n_in + 2 * nx + n_out:]
        cscr, sems = rest[:n_scr], rest[n_scr:]
        first = functools.reduce(jnp.logical_and, [pl.program_id(a) == 0 for a in range(len(grid))])
        last = functools.reduce(jnp.logical_and, [pl.program_id(a) == grid[a] - 1 for a in range(len(grid))])

        @pl.when(first)
        def _():
            carry.start(xin, xout, sems)

        body(*cin, *cout, *cscr)

        @pl.when(last)
        def _():
            carry.wait(xin, xout, sems)

    return pl.pallas_call(
        wrapped, name=name, grid=grid, in_specs=list(in_specs) + [HBM_SPEC] * nx,
        out_specs=list(out_specs) + [HBM_SPEC] * nx, out_shape=list(out_shape) + carry.out_shape(),
        scratch_shapes=list(scratch_shapes) + carry.scratch_shapes(), compiler_params=_params(len(grid)),
    )(*args, *carry.payloads)


def _exchange_call(name, payloads, gather, via_sibling=False):
    ex = _Exchange(payloads, gather, via_sibling)

    def body(*refs):
        ins, outs, sems = refs[:ex.n], refs[ex.n:2 * ex.n], refs[2 * ex.n:]
        ex.start(ins, outs, sems)
        ex.wait(ins, outs, sems)

    return pl.pallas_call(body, name=name, in_specs=[HBM_SPEC] * ex.n, out_specs=[HBM_SPEC] * ex.n,
                          out_shape=ex.out_shape(), scratch_shapes=ex.scratch_shapes())(*ex.payloads)


def _in_proj_fwd(x, g, w, tm, carry=None):
    T = x.shape[0]

    def body(x_ref, g_ref, w_hbm, proj_ref, h_ref, r_ref, w_ref, sem):
        @pl.when(pl.program_id(0) == 0)
        def _():
            _load_resident(w_hbm, w_ref, sem)

        xf = x_ref[...]
        r = lax.rsqrt(jnp.mean(xf * xf, axis=-1, keepdims=True) + EPS)
        h = (xf * r * g_ref[...]).astype(BF16)
        h_ref[...] = h
        r_ref[...] = r
        for j in range(N_DEV):
            proj_ref[:, j * WIN_BLK:(j + 1) * WIN_BLK] = _dot(h, w_ref[j])

    return _pcall(
        body, "in_proj_fwd", (T // tm,),
        [pl.BlockSpec((tm, D_MODEL), lambda i: (i, 0)), pl.BlockSpec((1, D_MODEL), lambda i: (0, 0)), ANY_SPEC],
        [pl.BlockSpec((tm, IN_WIDTH), lambda i: (i, 0)),
         pl.BlockSpec((tm, D_MODEL), lambda i: (i, 0)),
         pl.BlockSpec((tm, 1), lambda i: (i, 0))],
        [jax.ShapeDtypeStruct((T, IN_WIDTH), F32),
         jax.ShapeDtypeStruct((T, D_MODEL), BF16),
         jax.ShapeDtypeStruct((T, 1), F32)],
        [pltpu.VMEM(w.shape, w.dtype), pltpu.SemaphoreType.DMA], (x, g, w), carry)


def _ret_common(q_ref, k_ref, v_ref, cos_ref, sin_ref, mask_ref, rd_ref, sin_state):
    c = cos_ref[...]
    s = sin_ref[...]
    q = q_ref[...]
    q = q * c + pltpu.roll(q, HEAD_DIM // 2, 1) * s
    k = k_ref[...]
    k = (k * c + pltpu.roll(k, HEAD_DIM // 2, 1) * s) * (HEAD_DIM ** -0.5)
    qb = q.astype(BF16)
    kb = k.astype(BF16)
    vb = v_ref[...].astype(BF16)
    pm = (_dot_nt(qb, kb) * mask_ref[...]).astype(BF16)
    qd = (q * rd_ref[...]).astype(BF16)
    o = _dot(pm, vb) + _dot(qd, sin_state.astype(BF16))
    return q, k, qb, kb, vb, pm, qd, o


def _ret_specs(T, rev):
    nb = T // RET_BLOCK
    groups = RET_HEADS // RET_HPS
    wide = RET_HPS * HEAD_DIM
    blk = (lambda b: nb - 1 - b) if rev else (lambda b: b)
    col = lambda piece: (pl.BlockSpec((RET_BLOCK, wide), lambda h, b: (blk(b), piece * groups + h)), "lane")
    return dict(
        q=col(0), k=col(1), v=col(2), g=col(3),
        tab=(pl.BlockSpec((RET_BLOCK, HEAD_DIM), lambda h, b: (blk(b), 0)), None),
        mask=(pl.BlockSpec((RET_HPS, RET_BLOCK, RET_BLOCK), lambda h, b: (h, 0, 0)), "lead"),
        dec=(pl.BlockSpec((RET_HPS, RET_BLOCK, HEAD_DIM), lambda h, b: (h, 0, 0)), "lead"),
        gtb=(pl.BlockSpec((RET_HPS, 1, HEAD_DIM), lambda h, b: (h, 0, 0)), "lead"),
        gn=(pl.BlockSpec((1, wide), lambda h, b: (0, h)), "lane"),
        state=(pl.BlockSpec((RET_HPS, None, HEAD_DIM, HEAD_DIM), lambda h, b: (h, blk(b), 0, 0)), "lead"),
        rows=(pl.BlockSpec((RET_BLOCK, wide), lambda h, b: (blk(b), h)), "lane"),
        scratch=(pltpu.VMEM((RET_HPS, HEAD_DIM, HEAD_DIM), F32), "lead"),
    )


def _per_head(head_body, kinds):
    def body(*refs):
        for hh in range(RET_HPS):
            views = []
            for ref, kind in zip(refs, kinds):
                if kind == "lane":
                    views.append(ref.at[:, hh * HEAD_DIM:(hh + 1) * HEAD_DIM])
                elif kind == "lead":
                    views.append(ref.at[hh])
                else:
                    views.append(ref)
            head_body(*views)
    return body


def _ret_fwd(proj, cosf, sinf, mask, rowdec, kdec, gtb, gn, carry=None):
    T = proj.shape[0]
    nb = T // RET_BLOCK
    sp = _ret_specs(T, False)

    def body(q_ref, k_ref, v_ref, g_ref, cos_ref, sin_ref, mask_ref, rd_ref, kd_ref, gtb_ref, gn_ref,
             y_ref, sb_ref, st):
        @pl.when(pl.program_id(1) == 0)
        def _():
            st[...] = jnp.zeros_like(st)
        s_in = st[...]
        sb_ref[...] = s_in
        q, k, qb, kb, vb, pm, qd, o = _ret_common(q_ref, k_ref, v_ref, cos_ref, sin_ref, mask_ref, rd_ref, s_in)
        st[...] = gtb_ref[...] * s_in + _dot_tn((k * kd_ref[...]).astype(BF16), vb)
        mu = jnp.mean(o, axis=-1, keepdims=True)
        oc = o - mu
        n = oc * lax.rsqrt(jnp.mean(oc * oc, axis=-1, keepdims=True) + EPS)
        gt = g_ref[...]
        y_ref[...] = (gt * _sigmoid(gt) * (n * gn_ref[...])).astype(BF16)

    ins = [sp[n] for n in ("q", "k", "v", "g", "tab", "tab", "mask", "dec", "dec", "gtb", "gn")]
    outs = [sp["rows"], sp["state"]]
    return _pcall(
        _per_head(body, [kind for _, kind in ins + outs + [sp["scratch"]]]), "ret_fwd", (RET_HEADS // RET_HPS, nb),
        [s for s, _ in ins], [s for s, _ in outs],
        [jax.ShapeDtypeStruct((T, RET_WIDTH), BF16),
         jax.ShapeDtypeStruct((RET_HEADS, nb, HEAD_DIM, HEAD_DIM), F32)],
        [sp["scratch"][0]],
        (proj, proj, proj, proj, cosf, sinf, mask, rowdec, kdec, gtb, gn), carry)


def _scan(re, im, ar, ai, reverse):
    n = re.shape[0]
    row = lax.broadcasted_iota(jnp.int32, re.shape, 0)
    s = 1
    while s < n:
        if reverse:
            keep = row < n - s
            sr = jnp.where(keep, pltpu.roll(re, n - s, 0), 0.0)
            si = jnp.where(keep, pltpu.roll(im, n - s, 0), 0.0)
        else:
            keep = row >= s
            sr = jnp.where(keep, pltpu.roll(re, s, 0), 0.0)
            si = jnp.where(keep, pltpu.roll(im, s, 0), 0.0)
        re, im = re + ar * sr - ai * si, im + ar * si + ai * sr
        ar, ai = ar * ar - ai * ai, 2.0 * ar * ai
        s *= 2
    return re, im


S5_STATE_TILE = (S5_TILE, S5_LANES)


def _step_major_permutation():
    r = jnp.arange(S5_TILE)
    t_of_row = (r % S5_CHUNKS) * S5_STEPS + r // S5_CHUNKS
    return (t_of_row[:, None] == r[None, :]).astype(BF16)


def _permute_rows_f32(pm, x):
    hi = x.astype(BF16)
    rest = x - hi.astype(F32)
    mid = rest.astype(BF16)
    lo = (rest - mid.astype(F32)).astype(BF16)
    return _dot(pm, hi) + _dot(pm, mid) + _dot(pm, lo)


def _step_get(ref, j):
    return ref[j * S5_CHUNKS:(j + 1) * S5_CHUNKS, :]


def _step_set(ref, j, val):
    ref[j * S5_CHUNKS:(j + 1) * S5_CHUNKS, :] = val


def _tile_get(ref):
    return ref[...]


def _tile_set(ref, val):
    ref[...] = val


def _fill_power_table(ptab, lr, li):
    shape = (S5_CHUNKS, S5_LANES)
    lrb = jnp.broadcast_to(lr, shape)
    lib = jnp.broadcast_to(li, shape)
    pr, pi_ = lrb, lib
    for j in range(S5_STEPS):
        ptab[0, j * S5_CHUNKS:(j + 1) * S5_CHUNKS, :] = pr
        ptab[1, j * S5_CHUNKS:(j + 1) * S5_CHUNKS, :] = pi_
        pr, pi_ = lrb * pr - lib * pi_, lrb * pi_ + lib * pr


def _chunk_scans(xr, xi, lr, li, reverse):
    shape = (S5_CHUNKS, S5_LANES)
    lrb = jnp.broadcast_to(lr, shape)
    lib = jnp.broadcast_to(li, shape)
    sr = si = None
    for j in (range(S5_STEPS - 1, -1, -1) if reverse else range(S5_STEPS)):
        vr = _step_get(xr, j)
        vi = _step_get(xi, j)
        if sr is not None:
            vr, vi = vr + lrb * sr - lib * si, vi + lrb * si + lib * sr
            _step_set(xr, j, vr)
            _step_set(xi, j, vi)
        sr, si = vr, vi
    return sr, si


def _entering_states(zr, zi, cr, ci, ar, ai, reverse):
    shape = (S5_CHUNKS, S5_LANES)
    row = lax.broadcasted_iota(jnp.int32, shape, 0)
    if reverse:
        edge, shift = row == S5_CHUNKS - 1, S5_CHUNKS - 1
    else:
        edge, shift = row == 0, 1
    wr = jnp.where(edge, jnp.broadcast_to(cr, shape), pltpu.roll(zr, shift, 0))
    wi = jnp.where(edge, jnp.broadcast_to(ci, shape), pltpu.roll(zi, shift, 0))
    return _scan(wr, wi, ar, ai, reverse)


def _table_rows(ptab, j, conj):
    pr = ptab[0, j * S5_CHUNKS:(j + 1) * S5_CHUNKS, :]
    pi_ = ptab[1, j * S5_CHUNKS:(j + 1) * S5_CHUNKS, :]
    return pr, (-pi_ if conj else pi_)


def _s5_forward_states(xr, xi, lr, li, cr, ci, ptab):
    zr, zi = _chunk_scans(xr, xi, lr, li, False)
    ar, ai = _table_rows(ptab, S5_STEPS - 1, False)
    er, ei = _entering_states(zr, zi, cr, ci, ar, ai, False)
    for j in range(S5_STEPS):
        pr, pi_ = _table_rows(ptab, j, False)
        _step_set(xr, j, _step_get(xr, j) + pr * er - pi_ * ei)
        _step_set(xi, j, _step_get(xi, j) + pr * ei + pi_ * er)
    last = S5_CHUNKS - 1
    end_r = (ar * er - ai * ei + zr)[last:last + 1, :]
    end_i = (ar * ei + ai * er + zi)[last:last + 1, :]
    return er, ei, end_r, end_i


def _s5_specs(T, rev):
    rows = S5_TILE * S5_PER_STEP
    nt = T // rows
    tt = (lambda t: nt - 1 - t) if rev else (lambda t: t)
    return dict(
        u=pl.BlockSpec((rows, LANE), lambda b, t: (tt(t), 4 * RET_HEADS + b)),
        rows=pl.BlockSpec((rows, LANE), lambda b, t: (tt(t), b)),
        to_state=pl.BlockSpec((None, LANE, S5_LANES), lambda b, t: (b, 0, 0)),
        from_state=pl.BlockSpec((None, S5_LANES, LANE), lambda b, t: (b, 0, 0)),
        lam=pl.BlockSpec((None, 2, S5_LANES), lambda b, t: (b, 0, 0)),
        d=pl.BlockSpec((1, LANE), lambda b, t: (0, b)),
        perm=pl.BlockSpec((S5_TILE, S5_TILE), lambda b, t: (0, 0)),
        bound=pl.BlockSpec((None, S5_PER_STEP, 2, S5_LANES), lambda b, t: (b, tt(t), 0, 0)),
    )


def _s5_fwd(proj, pm, pm_t, bre, bim, cre_t, cim_t, lam, d, carry=None):
    T = proj.shape[0]
    nt = T // S5_TILE
    sp = _s5_specs(T, False)

    def body(u_ref, pm_ref, pmt_ref, bre_ref, bim_ref, cre_ref, cim_ref, lam_ref, d_ref, y_ref, bound_ref,
             carry, ptab, xr, xi):
        lr = lam_ref[0:1, :]
        li = lam_ref[1:2, :]

        @pl.when(pl.program_id(1) == 0)
        def _():
            carry[...] = jnp.zeros_like(carry)
            _fill_power_table(ptab, lr, li)

        for s in range(S5_PER_STEP):
            rows = slice(s * S5_TILE, (s + 1) * S5_TILE)
            u = _permute_rows_f32(pm_ref[...], u_ref[rows, :])
            ub = u.astype(BF16)
            _tile_set(xr, _dot(ub, bre_ref[...]))
            _tile_set(xi, _dot(ub, bim_ref[...]))
            bound_ref[s] = carry[...]
            _, _, end_r, end_i = _s5_forward_states(xr, xi, lr, li, carry[0:1, :], carry[1:2, :], ptab)
            carry[0:1, :] = end_r
            carry[1:2, :] = end_i
            y = (_dot(_tile_get(xr).astype(BF16), cre_ref[...]) - _dot(_tile_get(xi).astype(BF16), cim_ref[...])
                 + d_ref[...] * u)
            y_ref[rows, :] = _permute_rows_f32(pmt_ref[...], y)

    state = pltpu.VMEM(S5_STATE_TILE, F32)
    return _pcall(
        body, "s5_fwd", (S5_NBLK, nt // S5_PER_STEP),
        [sp["u"], sp["perm"], sp["perm"], sp["to_state"], sp["to_state"], sp["from_state"],
         sp["from_state"], sp["lam"], sp["d"]],
        [sp["rows"], sp["bound"]],
        [jax.ShapeDtypeStruct((T, SSM_WIDTH), F32),
         jax.ShapeDtypeStruct((S5_NBLK, nt, 2, S5_LANES), F32)],
        [pltpu.VMEM((2, S5_LANES), F32), pltpu.VMEM((2, S5_TILE, S5_LANES), F32), state, state],
        (proj, pm, pm_t, bre, bim, cre_t, cim_t, lam, d), carry)


def _glu_fwd(y, w, b, og, tm):
    T = y.shape[0]

    def body(y_ref, w_ref, b_ref, og_ref, z_ref, o_ref, r_ref):
        y1 = _gelu(y_ref[...])
        z = _dot(y1.astype(BF16), w_ref[...]) + b_ref[...]
        y2 = y1 * _sigmoid(z)
        r = lax.rsqrt(jnp.mean(y2 * y2, axis=-1, keepdims=True) + EPS)
        z_ref[...] = z
        o_ref[...] = (y2 * r * og_ref[...]).astype(BF16)
        r_ref[...] = r

    row = pl.BlockSpec((tm, SSM_WIDTH), lambda i: (i, 0))
    vec = pl.BlockSpec((1, SSM_WIDTH), lambda i: (0, 0))
    return pl.pallas_call(
        body, name="glu_fwd", grid=(T // tm,),
        in_specs=[row, pl.BlockSpec((SSM_WIDTH, SSM_WIDTH), lambda i: (0, 0)), vec, vec],
        out_specs=[row, row, pl.BlockSpec((tm, 1), lambda i: (i, 0))],
        out_shape=[jax.ShapeDtypeStruct((T, SSM_WIDTH), F32), jax.ShapeDtypeStruct((T, SSM_WIDTH), BF16),
                   jax.ShapeDtypeStruct((T, 1), F32)],
        compiler_params=_params(1),
    )(y, w, b, og)


def _out_proj_fwd(x, y_ret, y_ssm, w, g, tm):
    T = x.shape[0]

    def body(x_ref, a_ref, b_ref, w_ref, g_ref, x2_ref, h_ref, r_ref):
        x2 = x_ref[...] + _dot(a_ref[...], w_ref[0:RET_WIDTH, :]) + _dot(b_ref[...], w_ref[RET_WIDTH:D_MODEL, :])
        r = lax.rsqrt(jnp.mean(x2 * x2, axis=-1, keepdims=True) + EPS)
        x2_ref[...] = x2
        h_ref[...] = (x2 * r * g_ref[...]).astype(BF16)
        r_ref[...] = r

    full = pl.BlockSpec((tm, D_MODEL), lambda i: (i, 0))
    half = pl.BlockSpec((tm, RET_WIDTH), lambda i: (i, 0))
    return pl.pallas_call(
        body, name="out_proj_fwd", grid=(T // tm,),
        in_specs=[full, half, half, pl.BlockSpec((D_MODEL, D_MODEL), lambda i: (0, 0)),
                  pl.BlockSpec((1, D_MODEL), lambda i: (0, 0))],
        out_specs=[full, full, pl.BlockSpec((tm, 1), lambda i: (i, 0))],
        out_shape=[jax.ShapeDtypeStruct((T, D_MODEL), F32), jax.ShapeDtypeStruct((T, D_MODEL), BF16),
                   jax.ShapeDtypeStruct((T, 1), F32)],
        compiler_params=_params(1),
    )(x, y_ret, y_ssm, w, g)


def _ffn_up(h, wg, wu, tm, carry=None):
    T = h.shape[0]

    def body(h_ref, wg_ref, wu_ref, a_ref, b_ref, f_ref):
        hb = h_ref[...]
        a = _dot(hb, wg_ref[...])
        b = _dot(hb, wu_ref[...])
        a_ref[...] = a.astype(BF16)
        b_ref[...] = b.astype(BF16)
        f_ref[...] = (a * _sigmoid(a) * b).astype(BF16)

    wspec = pl.BlockSpec((None, D_MODEL, FF_BLK), lambda j, i: (j, 0, 0))
    ospec = pl.BlockSpec((None, tm, FF_BLK), lambda j, i: (j, i, 0))
    oshape = jax.ShapeDtypeStruct((N_DEV, T, FF_BLK), BF16)
    return _pcall(
        body, "ffn_up", (N_DEV, T // tm),
        [pl.BlockSpec((tm, D_MODEL), lambda j, i: (i, 0)), wspec, wspec],
        [ospec, ospec, ospec], [oshape, oshape, oshape], [], (h, wg, wu), carry)


def _ffn_down_loss(f, wd, x2, tgt, g, tm):
    T = x2.shape[0]

    def body(f_ref, w_hbm, x2_ref, t_ref, g_ref, dx_ref, dxb_ref, loss_ref, dg_ref, w_ref, sem):
        i = pl.program_id(0)

        @pl.when(i == 0)
        def _():
            _load_resident(w_hbm, w_ref, sem)
            loss_ref[...] = jnp.zeros_like(loss_ref)
            dg_ref[...] = jnp.zeros_like(dg_ref)

        gv = g_ref[...]
        x3 = x2_ref[...]
        for k in range(N_DEV):
            x3 = x3 + _dot(f_ref[k], w_ref[k])
        r = lax.rsqrt(jnp.mean(x3 * x3, axis=-1, keepdims=True) + EPS)
        err = x3 * r * gv - t_ref[...]
        tile_loss = 0.5 * jnp.sum(jnp.mean(err * err, axis=-1, keepdims=True), axis=0, keepdims=True)
        dx, dgt = _rms_bwd(err * (1.0 / D_MODEL), x3, r, gv)
        dx_ref[...] = dx
        dxb_ref[...] = dx.astype(BF16)
        loss_ref[...] += jnp.broadcast_to(tile_loss, loss_ref.shape)
        dg_ref[...] += jnp.sum(dgt, axis=0, keepdims=True)

    full = pl.BlockSpec((tm, D_MODEL), lambda i: (i, 0))
    vec = pl.BlockSpec((1, D_MODEL), lambda i: (0, 0))
    return pl.pallas_call(
        body, name="ffn_down_loss", grid=(T // tm,),
        in_specs=[pl.BlockSpec((N_DEV, tm, FF_BLK), lambda i: (0, i, 0)), ANY_SPEC, full, full, vec],
        out_specs=[full, full, pl.BlockSpec((8, LANE), lambda i: (0, 0)), vec],
        out_shape=[jax.ShapeDtypeStruct((T, D_MODEL), F32), jax.ShapeDtypeStruct((T, D_MODEL), BF16),
                   jax.ShapeDtypeStruct((8, LANE), F32), jax.ShapeDtypeStruct((1, D_MODEL), F32)],
        scratch_shapes=[pltpu.VMEM(wd.shape, wd.dtype), pltpu.SemaphoreType.DMA],
        compiler_params=_params(1),
    )(f, wd, x2, tgt, g)


def _ffn_bwd_act(dxb, wd, a, b, tm):
    T = dxb.shape[0]

    def body(dx_ref, w_ref, a_ref, b_ref, da_ref, db_ref):
        df = _dot_nt(dx_ref[...], w_ref[...])
        a = a_ref[...].astype(F32)
        b = b_ref[...].astype(F32)
        sg = _sigmoid(a)
        da_ref[...] = (df * b * sg * (1.0 + a * (1.0 - sg))).astype(BF16)
        db_ref[...] = (df * a * sg).astype(BF16)

    blk = pl.BlockSpec((None, tm, FF_BLK), lambda j, i: (j, i, 0))
    oshape = jax.ShapeDtypeStruct((N_DEV, T, FF_BLK), BF16)
    return pl.pallas_call(
        body, name="ffn_bwd_act", grid=(N_DEV, T // tm),
        in_specs=[pl.BlockSpec((tm, D_MODEL), lambda j, i: (i, 0)),
                  pl.BlockSpec((None, FF_BLK, D_MODEL), lambda j, i: (j, 0, 0)), blk, blk],
        out_specs=[blk, blk], out_shape=[oshape, oshape],
        compiler_params=_params(2),
    )(dxb, wd, a, b)


def _ffn_bwd_in(da, db, wg, wu, tm, carry=None):
    T = da.shape[1]

    def body(da_ref, db_ref, wg_ref, wu_ref, dh_ref):
        part = _dot_nt(da_ref[...], wg_ref[...]) + _dot_nt(db_ref[...], wu_ref[...])

        @pl.when(pl.program_id(1) == 0)
        def _():
            dh_ref[...] = part

        @pl.when(pl.program_id(1) > 0)
        def _():
            dh_ref[...] += part

    ablk = pl.BlockSpec((None, tm, FF_BLK), lambda i, k: (k, i, 0))
    wblk = pl.BlockSpec((None, D_MODEL, FF_BLK), lambda i, k: (k, 0, 0))
    return _pcall(
        body, "ffn_bwd_in", (T // tm, N_DEV), [ablk, ablk, wblk, wblk],
        [pl.BlockSpec((tm, D_MODEL), lambda i, k: (i, 0))], [jax.ShapeDtypeStruct((T, D_MODEL), F32)],
        [], (da, db, wg, wu), carry)


def _ffn_wgrad_up(h, da, db, tk, carry=None):
    T = h.shape[0]
    nk = T // tk

    def body(h_ref, da_ref, db_ref, g_ref, u_ref, accg, accu):
        k = pl.program_id(1)

        @pl.when(k == 0)
        def _():
            accg[...] = jnp.zeros_like(accg)
            accu[...] = jnp.zeros_like(accu)

        hb = h_ref[...]
        accg[...] += _dot_tn(hb, da_ref[...])
        accu[...] += _dot_tn(hb, db_ref[...])

        @pl.when(k == nk - 1)
        def _():
            g_ref[...] = accg[...].astype(BF16)
            u_ref[...] = accu[...].astype(BF16)

    blk = pl.BlockSpec((None, tk, FF_BLK), lambda j, k: (j, k, 0))
    ospec = pl.BlockSpec((None, D_MODEL, FF_BLK), lambda j, k: (j, 0, 0))
    oshape = jax.ShapeDtypeStruct((N_DEV, D_MODEL, FF_BLK), BF16)
    return _pcall(
        body, "ffn_wgrad_up", (N_DEV, nk),
        [pl.BlockSpec((tk, D_MODEL), lambda j, k: (k, 0)), blk, blk],
        [ospec, ospec], [oshape, oshape],
        [pltpu.VMEM((D_MODEL, FF_BLK), F32), pltpu.VMEM((D_MODEL, FF_BLK), F32)], (h, da, db), carry)


def _ffn_wgrad_down(f, dxb, tk):
    T = dxb.shape[0]
    nk = T // tk

    def body(f_ref, dx_ref, o_ref, acc):
        k = pl.program_id(1)

        @pl.when(k == 0)
        def _():
            acc[...] = jnp.zeros_like(acc)

        acc[...] += _dot_tn(f_ref[...], dx_ref[...])

        @pl.when(k == nk - 1)
        def _():
            o_ref[...] = acc[...].astype(BF16)

    return pl.pallas_call(
        body, name="ffn_wgrad_down", grid=(N_DEV, nk),
        in_specs=[pl.BlockSpec((None, tk, FF_BLK), lambda j, k: (j, k, 0)),
                  pl.BlockSpec((tk, D_MODEL), lambda j, k: (k, 0))],
        out_specs=pl.BlockSpec((None, FF_BLK, D_MODEL), lambda j, k: (j, 0, 0)),
        out_shape=jax.ShapeDtypeStruct((N_DEV, FF_BLK, D_MODEL), BF16),
        scratch_shapes=[pltpu.VMEM((FF_BLK, D_MODEL), F32)],
        compiler_params=_params(2),
    )(f, dxb)


def _out_proj_bwd(dh2, x2, r2, g, dx3, w, tm):
    T = x2.shape[0]

    def body(dh_ref, x_ref, r_ref, g_ref, dx3_ref, w_ref, dx_ref, dxb_ref, dg_ref, a_ref, b_ref):
        @pl.when(pl.program_id(0) == 0)
        def _():
            dg_ref[...] = jnp.zeros_like(dg_ref)

        dxn, dgt = _rms_bwd(dh_ref[...], x_ref[...], r_ref[...], g_ref[...])
        dx = dx3_ref[...] + dxn
        dxv = dx.astype(BF16)
        dx_ref[...] = dx
        dxb_ref[...] = dxv
        dg_ref[...] += jnp.sum(dgt, axis=0, keepdims=True)
        a_ref[...] = _dot_nt(dxv, w_ref[0:RET_WIDTH, :])
        b_ref[...] = _dot_nt(dxv, w_ref[RET_WIDTH:D_MODEL, :])

    full = pl.BlockSpec((tm, D_MODEL), lambda i: (i, 0))
    vec = pl.BlockSpec((1, D_MODEL), lambda i: (0, 0))
    half = pl.BlockSpec((tm, RET_WIDTH), lambda i: (i, 0))
    hshape = jax.ShapeDtypeStruct((T, RET_WIDTH), F32)
    return pl.pallas_call(
        body, name="out_proj_bwd", grid=(T // tm,),
        in_specs=[full, full, pl.BlockSpec((tm, 1), lambda i: (i, 0)), vec, full,
                  pl.BlockSpec((D_MODEL, D_MODEL), lambda i: (0, 0))],
        out_specs=[full, full, vec, half, half],
        out_shape=[jax.ShapeDtypeStruct((T, D_MODEL), F32), jax.ShapeDtypeStruct((T, D_MODEL), BF16),
                   jax.ShapeDtypeStruct((1, D_MODEL), F32), hshape, hshape],
        compiler_params=_params(1),
    )(dh2, x2, r2, g, dx3, w)


def _wgrad_rows(name, a, b, tk):
    T, M = a.shape
    N = b.shape[1]
    nk = T // tk

    def body(a_ref, b_ref, o_ref, acc):
        k = pl.program_id(0)

        @pl.when(k == 0)
        def _():
            acc[...] = jnp.zeros_like(acc)

        acc[...] += _dot_tn(a_ref[...], b_ref[...])

        @pl.when(k == nk - 1)
        def _():
            o_ref[...] = acc[...].astype(BF16)

    return pl.pallas_call(
        body, name=name, grid=(nk,),
        in_specs=[pl.BlockSpec((tk, M), lambda k: (k, 0)), pl.BlockSpec((tk, N), lambda k: (k, 0))],
        out_specs=pl.BlockSpec((M, N), lambda k: (0, 0)),
        out_shape=jax.ShapeDtypeStruct((M, N), BF16),
        scratch_shapes=[pltpu.VMEM((M, N), F32)],
        compiler_params=_params(1),
    )(a, b)


def _glu_bwd(y, z, r, dyo, w, og, tm):
    T = y.shape[0]

    def body(y_ref, z_ref, r_ref, d_ref, w_ref, og_ref, dy_ref, dw_ref, db_ref, dog_ref):
        @pl.when(pl.program_id(0) == 0)
        def _():
            dw_ref[...] = jnp.zeros_like(dw_ref)
            db_ref[...] = jnp.zeros_like(db_ref)
            dog_ref[...] = jnp.zeros_like(dog_ref)

        y1, g1 = _gelu_and_grad(y_ref[...])
        sg = _sigmoid(z_ref[...])
        y2 = y1 * sg
        dy2, dogt = _rms_bwd(d_ref[...], y2, r_ref[...], og_ref[...])
        dog_ref[...] += jnp.sum(dogt, axis=0, keepdims=True)
        dz = dy2 * y1 * sg * (1.0 - sg)
        db_ref[...] += jnp.sum(dz, axis=0, keepdims=True)
        dzb = dz.astype(BF16)
        dw_ref[...] += _dot_tn(y1.astype(BF16), dzb)
        dy_ref[...] = (dy2 * sg + _dot_nt(dzb, w_ref[...])) * g1

    row = pl.BlockSpec((tm, SSM_WIDTH), lambda i: (i, 0))
    vec = pl.BlockSpec((1, SSM_WIDTH), lambda i: (0, 0))
    sq = pl.BlockSpec((SSM_WIDTH, SSM_WIDTH), lambda i: (0, 0))
    return pl.pallas_call(
        body, name="glu_bwd", grid=(T // tm,),
        in_specs=[row, row, pl.BlockSpec((tm, 1), lambda i: (i, 0)), row, sq, vec],
        out_specs=[row, sq, vec, vec],
        out_shape=[jax.ShapeDtypeStruct((T, SSM_WIDTH), F32), jax.ShapeDtypeStruct((SSM_WIDTH, SSM_WIDTH), F32),
                   jax.ShapeDtypeStruct((1, SSM_WIDTH), F32), jax.ShapeDtypeStruct((1, SSM_WIDTH), F32)],
        compiler_params=_params(1),
    )(y, z, r, dyo, w, og)


def _s5_bwd(proj, dy, bound, pm, pm_t, bre, bim, bre_t, bim_t, cre, cim, lam, d, carry=None):
    T = proj.shape[0]
    nt = T // S5_TILE
    sp = _s5_specs(T, True)

    def body(u_ref, dy_ref, bound_ref, pm_ref, pmt_ref, bre_ref, bim_ref, bret_ref, bimt_ref, cre_ref, cim_ref,
             lam_ref, d_ref,
             du_ref, dbre_ref, dbim_ref, dcre_ref, dcim_ref, dlam_ref, dd_ref, carry, ptab, sr, si, gr, gi):
        lr = lam_ref[0:1, :]
        li = lam_ref[1:2, :]

        @pl.when(pl.program_id(1) == 0)
        def _():
            carry[...] = jnp.zeros_like(carry)
            _fill_power_table(ptab, lr, li)
            for ref in (dbre_ref, dbim_ref, dcre_ref, dcim_ref, dlam_ref, dd_ref):
                ref[...] = jnp.zeros_like(ref)

        def one_tile(u_in, dy_in, b_r, b_i):
            u = _permute_rows_f32(pm_ref[...], u_in)
            ub = u.astype(BF16)
            dyv = _permute_rows_f32(pm_ref[...], dy_in)
            dyb = dyv.astype(BF16)
            _tile_set(sr, _dot(ub, bre_ref[...]))
            _tile_set(si, _dot(ub, bim_ref[...]))
            er, ei, _, _ = _s5_forward_states(sr, si, lr, li, b_r, b_i, ptab)
            _tile_set(gr, _dot(dyb, cre_ref[...]))
            _tile_set(gi, -_dot(dyb, cim_ref[...]))
            zr, zi = _chunk_scans(gr, gi, lr, -li, True)
            ar, ai = _table_rows(ptab, S5_STEPS - 1, True)
            fr, fi = _entering_states(zr, zi, carry[0:1, :], carry[1:2, :], ar, ai, True)
            acc_r = jnp.zeros((S5_CHUNKS, S5_LANES), F32)
            acc_i = jnp.zeros((S5_CHUNKS, S5_LANES), F32)
            for j in range(S5_STEPS):
                qr, qi = _table_rows(ptab, S5_STEPS - 1 - j, True)
                g_r = _step_get(gr, j) + qr * fr - qi * fi
                g_i = _step_get(gi, j) + qr * fi + qi * fr
                _step_set(gr, j, g_r)
                _step_set(gi, j, g_i)
                p_r, p_i = (er, ei) if j == 0 else (_step_get(sr, j - 1), _step_get(si, j - 1))
                acc_r += g_r * p_r + g_i * p_i
                acc_i += g_i * p_r - g_r * p_i
            dlam_ref[0:1, :] += jnp.sum(acc_r, axis=0, keepdims=True)
            dlam_ref[1:2, :] += jnp.sum(acc_i, axis=0, keepdims=True)
            g_all_r = _tile_get(gr)
            g_all_i = _tile_get(gi)
            carry[0:1, :] = g_all_r[0:1, :]
            carry[1:2, :] = g_all_i[0:1, :]
            grb = g_all_r.astype(BF16)
            gib = g_all_i.astype(BF16)
            du = (_dot(grb, bret_ref[...]) + _dot(gib, bimt_ref[...]) + d_ref[...] * dyv).astype(BF16)
            dbre_ref[...] += _dot_tn(grb, ub)
            dbim_ref[...] += _dot_tn(gib, ub)
            dcre_ref[...] += _dot_tn(dyb, _tile_get(sr).astype(BF16))
            dcim_ref[...] -= _dot_tn(dyb, _tile_get(si).astype(BF16))
            dd_ref[...] += jnp.sum(dyv * u, axis=0, keepdims=True)
            return _dot(pmt_ref[...], du).astype(BF16)

        for s in reversed(range(S5_PER_STEP)):
            rows = slice(s * S5_TILE, (s + 1) * S5_TILE)
            du_ref[rows, :] = one_tile(u_ref[rows, :], dy_ref[rows, :], bound_ref[s, 0:1, :], bound_ref[s, 1:2, :])

    acc_ts = pl.BlockSpec((None, S5_LANES, LANE), lambda b, t: (b, 0, 0))
    acc_fs = pl.BlockSpec((None, LANE, S5_LANES), lambda b, t: (b, 0, 0))
    return _pcall(
        body, "s5_bwd", (S5_NBLK, nt // S5_PER_STEP),
        [sp["u"], sp["rows"], sp["bound"], sp["perm"], sp["perm"], sp["to_state"], sp["to_state"],
         sp["from_state"], sp["from_state"], sp["to_state"], sp["to_state"], sp["lam"], sp["d"]],
        [sp["rows"], acc_ts, acc_ts, acc_fs, acc_fs, sp["lam"], sp["d"]],
        [jax.ShapeDtypeStruct((T, SSM_WIDTH), BF16),
         jax.ShapeDtypeStruct((S5_NBLK, S5_LANES, LANE), F32),
         jax.ShapeDtypeStruct((S5_NBLK, S5_LANES, LANE), F32),
         jax.ShapeDtypeStruct((S5_NBLK, LANE, S5_LANES), F32),
         jax.ShapeDtypeStruct((S5_NBLK, LANE, S5_LANES), F32),
         jax.ShapeDtypeStruct((S5_NBLK, 2, S5_LANES), F32),
         jax.ShapeDtypeStruct((1, SSM_WIDTH), F32)],
        [pltpu.VMEM((2, S5_LANES), F32), pltpu.VMEM((2, S5_TILE, S5_LANES), F32)]
        + [pltpu.VMEM(S5_STATE_TILE, F32)] * 4,
        (proj, dy, bound, pm, pm_t, bre, bim, bre_t, bim_t, cre, cim, lam, d), carry)


def _ret_bwd(proj, cosf, sinf, mask, rowdec, kdec, gtb, gn, sblk, dyr):
    T = proj.shape[0]
    nb = T // RET_BLOCK
    sp = _ret_specs(T, True)

    def body(q_ref, k_ref, v_ref, g_ref, cos_ref, sin_ref, mask_ref, rd_ref, kd_ref, gtb_ref, gn_ref, sb_ref, dy_ref,
             dq_ref, dk_ref, dv_ref, dg_ref, dgn_ref, dst):
        @pl.when(pl.program_id(1) == 0)
        def _():
            dst[...] = jnp.zeros_like(dst)
            dgn_ref[...] = jnp.zeros_like(dgn_ref)

        s_in = sb_ref[...]
        q, k, qb, kb, vb, pm, qd, o = _ret_common(q_ref, k_ref, v_ref, cos_ref, sin_ref, mask_ref, rd_ref, s_in)
        mu = jnp.mean(o, axis=-1, keepdims=True)
        oc = o - mu
        rstd = lax.rsqrt(jnp.mean(oc * oc, axis=-1, keepdims=True) + EPS)
        n = oc * rstd
        gt = g_ref[...]
        sg = _sigmoid(gt)
        sil = gt * sg
        gnv = gn_ref[...]
        dyv = dy_ref[...]
        dg_ref[...] = (dyv * (n * gnv) * (sg * (1.0 + gt * (1.0 - sg)))).astype(BF16)
        dgn_ref[...] += jnp.sum(dyv * sil * n, axis=0, keepdims=True)
        dn = dyv * sil * gnv
        do = rstd * (dn - jnp.mean(dn, axis=-1, keepdims=True) - n * jnp.mean(dn * n, axis=-1, keepdims=True))
        dob = do.astype(BF16)
        ds = dst[...]
        dsb = ds.astype(BF16)
        kd = kd_ref[...]
        rd = rd_ref[...]
        dv_ref[...] = (_dot_tn(pm, dob) + _dot((k * kd).astype(BF16), dsb)).astype(BF16)
        dpb = (_dot_nt(dob, vb) * mask_ref[...]).astype(BF16)
        dq = _dot(dpb, kb) + _dot_nt(dob, s_in.astype(BF16)) * rd
        dk = (_dot_tn(dpb, qb) + _dot_nt(vb, dsb) * kd) * (HEAD_DIM ** -0.5)
        dst[...] = gtb_ref[...] * ds + _dot_tn(qd, dob)
        c = cos_ref[...]
        s = sin_ref[...]
        dq_ref[...] = (dq * c + pltpu.roll(dq * s, HEAD_DIM // 2, 1)).astype(BF16)
        dk_ref[...] = (dk * c + pltpu.roll(dk * s, HEAD_DIM // 2, 1)).astype(BF16)

    oshape = jax.ShapeDtypeStruct((T, RET_WIDTH), BF16)
    ins = [sp[n] for n in ("q", "k", "v", "g", "tab", "tab", "mask", "dec", "dec", "gtb", "gn", "state", "rows")]
    outs = [sp["rows"], sp["rows"], sp["rows"], sp["rows"], sp["gn"]]
    return pl.pallas_call(
        _per_head(body, [kind for _, kind in ins + outs + [sp["scratch"]]]), name="ret_bwd",
        grid=(RET_HEADS // RET_HPS, nb), in_specs=[s for s, _ in ins], out_specs=[s for s, _ in outs],
        out_shape=[oshape, oshape, oshape, oshape, jax.ShapeDtypeStruct((1, RET_WIDTH), F32)],
        scratch_shapes=[sp["scratch"][0]],
        compiler_params=_params(2),
    )(proj, proj, proj, proj, cosf, sinf, mask, rowdec, kdec, gtb, gn, sblk, dyr)


def _in_proj_bwd(dproj, w, x, r1, g, dx2, tm, carry=None):
    T = x.shape[0]

    def body(dp_ref, w_hbm, x_ref, r_ref, g_ref, dx2_ref, gx_ref, dg_ref, w_ref, sem):
        @pl.when(pl.program_id(0) == 0)
        def _():
            _load_resident(w_hbm, w_ref, sem)
            dg_ref[...] = jnp.zeros_like(dg_ref)

        dh = _dot_nt(dp_ref[:, 0:WIN_BLK], w_ref[0])
        for k in range(1, N_DEV):
            dh = dh + _dot_nt(dp_ref[:, k * WIN_BLK:(k + 1) * WIN_BLK], w_ref[k])
        dxn, dgt = _rms_bwd(dh, x_ref[...], r_ref[...], g_ref[...])
        gx_ref[...] = dx2_ref[...] + dxn
        dg_ref[...] += jnp.sum(dgt, axis=0, keepdims=True)

    full = pl.BlockSpec((tm, D_MODEL), lambda i: (i, 0))
    vec = pl.BlockSpec((1, D_MODEL), lambda i: (0, 0))
    return _pcall(
        body, "in_proj_bwd", (T // tm,),
        [pl.BlockSpec((tm, IN_WIDTH), lambda i: (i, 0)), ANY_SPEC,
         full, pl.BlockSpec((tm, 1), lambda i: (i, 0)), vec, full],
        [full, vec],
        [jax.ShapeDtypeStruct((T, D_MODEL), F32), jax.ShapeDtypeStruct((1, D_MODEL), F32)],
        [pltpu.VMEM(w.shape, w.dtype), pltpu.SemaphoreType.DMA], (dproj, w, x, r1, g, dx2), carry)


def _in_proj_wgrad(h, dproj, tk, carry=None):
    T = h.shape[0]
    nk = T // tk

    def body(h_ref, dp_ref, o_ref, acc):
        k = pl.program_id(1)

        @pl.when(k == 0)
        def _():
            acc[...] = jnp.zeros_like(acc)

        acc[...] += _dot_tn(h_ref[...], dp_ref[...])

        @pl.when(k == nk - 1)
        def _():
            o_ref[...] = acc[...].astype(BF16)

    return _pcall(
        body, "in_proj_wgrad", (N_DEV, nk),
        [pl.BlockSpec((tk, D_MODEL), lambda j, k: (k, 0)), pl.BlockSpec((tk, WIN_BLK), lambda j, k: (k, j))],
        [pl.BlockSpec((None, D_MODEL, WIN_BLK), lambda j, k: (j, 0, 0))],
        [jax.ShapeDtypeStruct((N_DEV, D_MODEL, WIN_BLK), BF16)],
        [pltpu.VMEM((D_MODEL, WIN_BLK), F32)], (h, dproj), carry)


def _rope_tables(T):
    half = HEAD_DIM // 2
    freqs = ROPE_BASE ** (-jnp.arange(half, dtype=F32) / half)
    ang = jnp.arange(T, dtype=F32)[:, None] * freqs[None, :]
    c = jnp.cos(ang)
    s = jnp.sin(ang)
    return jnp.concatenate([c, c], axis=1), jnp.concatenate([-s, s], axis=1)


def _retention_tables():
    hh = jnp.arange(RET_HEADS, dtype=F32)
    log_g = jnp.log1p(-(2.0 ** (-5.0 - hh)))[:, None, None]
    i = jnp.arange(RET_BLOCK)
    ci = (i // CHUNK)[:, None]
    cj = (i // CHUNK)[None, :]
    diff = (i[:, None] - i[None, :]).astype(F32)
    expo = jnp.where(ci == cj, jnp.abs(diff), diff)
    mask = jnp.where((cj <= ci)[None], jnp.exp(log_g * expo[None]), 0.0)
    r = jnp.arange(RET_BLOCK, dtype=F32)[None, :, None]
    ones = jnp.ones((1, 1, HEAD_DIM), F32)
    rowdec = jnp.exp(log_g * (r + 1.0)) * ones
    kdec = jnp.exp(log_g * (RET_BLOCK - 1.0 - r)) * ones
    gtb = jnp.exp(log_g * float(RET_BLOCK)) * ones
    return mask, rowdec, kdec, gtb


def _s5_discretise(a_re, a_im, log_dt, b_re, b_im):
    lam = lax.complex(a_re, a_im)
    dt = jnp.exp(log_dt)[:, None]
    lam_bar = jnp.exp(lam * dt)
    b_bar = ((lam_bar - 1.0) / lam)[..., None] * lax.complex(b_re, b_im)
    return jnp.real(lam_bar), jnp.imag(lam_bar), jnp.real(b_bar), jnp.imag(b_bar)


def _to_state_blockdiag(m):
    eye = jnp.eye(S5_GB, dtype=m.dtype)
    t = jnp.einsum("bgpc,gh->bgchp", m.reshape(S5_NBLK, S5_GB, SSM_STATE, SSM_GROUP), eye)
    return t.reshape(S5_NBLK, LANE, S5_LANES)


def _from_state_blockdiag(m):
    eye = jnp.eye(S5_GB, dtype=m.dtype)
    t = jnp.einsum("bgcp,gh->bgphc", m.reshape(S5_NBLK, S5_GB, SSM_GROUP, SSM_STATE), eye)
    return t.reshape(S5_NBLK, S5_LANES, LANE)


def _diag_of_state_major(acc):
    eye = jnp.eye(S5_GB, dtype=acc.dtype)
    t = acc.reshape(S5_NBLK, S5_GB, SSM_STATE, S5_GB, SSM_GROUP)
    return jnp.einsum("bgphc,gh->bgpc", t, eye).reshape(SSM_GROUPS, SSM_STATE, SSM_GROUP)


def _diag_of_channel_major(acc):
    eye = jnp.eye(S5_GB, dtype=acc.dtype)
    t = acc.reshape(S5_NBLK, S5_GB, SSM_GROUP, S5_GB, SSM_STATE)
    return jnp.einsum("bgchp,gh->bgcp", t, eye).reshape(SSM_GROUPS, SSM_GROUP, SSM_STATE)


SMALL_PARTIALS = (("ret_gn_g", 1024), ("lam_re", 4096), ("lam_im", 4096),
                  ("bbar_re", 65536), ("bbar_im", 65536), ("c_re", 65536), ("c_im", 65536),
                  ("ssm_d", 1024), ("b_glu", 1024), ("out_g", 1024), ("norm_ffn_g", 2048), ("norm_final_g", 2048))


def _forward_backward(x, tgt, shards, sm):
    T = x.shape[0]
    tm = min(1024, T)
    cosf, sinf = _rope_tables(T)
    mask, rowdec, kdec, gtb = _retention_tables()
    lbr, lbi, bbr, bbi = _s5_discretise(sm["ssm_a_re"], sm["ssm_a_im"], sm["ssm_log_dt"], sm["ssm_b_re"],
                                        sm["ssm_b_im"])
    bre = _to_state_blockdiag(bbr).astype(BF16)
    bim = _to_state_blockdiag(bbi).astype(BF16)
    cre_t = _from_state_blockdiag(sm["ssm_c_re"]).astype(BF16)
    cim_t = _from_state_blockdiag(sm["ssm_c_im"]).astype(BF16)
    bre_t = jnp.swapaxes(bre, 1, 2)
    bim_t = jnp.swapaxes(bim, 1, 2)
    cre = jnp.swapaxes(cre_t, 1, 2)
    cim = jnp.swapaxes(cim_t, 1, 2)
    lam = jnp.stack([lbr.reshape(S5_NBLK, S5_LANES), lbi.reshape(S5_NBLK, S5_LANES)], axis=1)
    pm = _step_major_permutation()
    pm_t = pm.T
    row = lambda v: v.reshape(1, -1)
    g_mix, g_ffn, g_fin = row(sm["norm_mix_g"]), row(sm["norm_ffn_g"]), row(sm["norm_final_g"])
    gn, dsk, bglu, og = row(sm["ret_gn_g"]), row(sm["ssm_d"]), row(sm["ssm_b_glu"]), row(sm["ssm_out_g"])

    (w_in,) = _exchange_call("weight_gather", [shards["w_in"]], True, via_sibling=True)
    proj, h1, r1, w_gate = _in_proj_fwd(x, g_mix, w_in, 256, _Exchange([shards["w_gate"]], True, via_sibling=True))
    y_ret, sblk, w_glu, w_out = _ret_fwd(proj, cosf, sinf, mask, rowdec, kdec, gtb, gn,
                                         _Exchange([shards["ssm_w_glu"], shards["w_out"]], True))
    w_glu = w_glu.reshape(SSM_WIDTH, SSM_WIDTH)
    w_out = w_out.reshape(D_MODEL, D_MODEL)
    y_s5, bound, w_up = _s5_fwd(proj, pm, pm_t, bre, bim, cre_t, cim_t, lam, dsk, _Exchange([shards["w_up"]], True))
    z, y_ssm, r_ssm = _glu_fwd(y_s5, w_glu, bglu, og, 256)
    x2, h2, r2 = _out_proj_fwd(x, y_ret, y_ssm, w_out, g_ffn, 256)
    a, b, f, w_down = _ffn_up(h2, w_gate, w_up, tm, _Exchange([shards["w_down"]], True))
    dx3, dx3b, loss8, dg_fin = _ffn_down_loss(f, w_down, x2, tgt, g_fin, 256)

    landed = {}
    da, db = _ffn_bwd_act(dx3b, w_down, a, b, tm)
    dw_down = _ffn_wgrad_down(f, dx3b, tm)
    dw_gate, dw_up, landed["w_down"] = _ffn_wgrad_up(h2, da, db, tm, _Exchange([dw_down], False))
    dh2, landed["w_gate"] = _ffn_bwd_in(da, db, w_gate, w_up, min(1024, T), _Exchange([dw_gate], False))
    dx2, dx2b, dg_ffn, dy_ret, dy_ssm = _out_proj_bwd(dh2, x2, r2, g_ffn, dx3, w_out, 256)
    dw_out = jnp.concatenate([_wgrad_rows("out_proj_wgrad_ret", y_ret, dx2b, tm),
                              _wgrad_rows("out_proj_wgrad_ssm", y_ssm, dx2b, tm)], axis=0)
    dy_s5, dw_glu, db_glu, dog = _glu_bwd(y_s5, z, r_ssm, dy_ssm, w_glu, og, 256)
    du, dbre, dbim, dcre, dcim, dlam, dd, landed["w_up"] = _s5_bwd(
        proj, dy_s5, bound, pm, pm_t, bre, bim, bre_t, bim_t, cre, cim, lam, dsk, _Exchange([dw_up], False))
    dq, dk, dv, dgate, dgn = _ret_bwd(proj, cosf, sinf, mask, rowdec, kdec, gtb, gn, sblk, dy_ret)
    dproj = jnp.concatenate([dq, dk, dv, dgate, du], axis=1)
    small = dict(ret_gn_g=dgn, lam_re=dlam[:, 0], lam_im=dlam[:, 1],
                 bbar_re=_diag_of_state_major(dbre), bbar_im=_diag_of_state_major(dbim),
                 c_re=_diag_of_channel_major(dcre), c_im=_diag_of_channel_major(dcim),
                 ssm_d=dd, b_glu=db_glu, out_g=dog, norm_ffn_g=dg_ffn, norm_final_g=dg_fin)
    packed = _pack([small[n] for n, _ in SMALL_PARTIALS])
    dw_in, landed["w_out"], landed["ssm_w_glu"], small_landed = _in_proj_wgrad(
        h1, dproj, tm, _Exchange([dw_out.reshape(N_DEV, D_MODEL // N_DEV, D_MODEL),
                                  dw_glu.astype(BF16).reshape(N_DEV, SSM_WIDTH // N_DEV, SSM_WIDTH), packed],
                                 [False, False, True]))
    grad_x, dg_mix, landed["w_in"] = _in_proj_bwd(dproj, w_in, x, r1, g_mix, dx2, 256, _Exchange([dw_in], False))
    (mix_landed,) = _exchange_call("mix_gain_grad_gather", [_pack([dg_mix])], True)
    summed = dict(zip([n for n, _ in SMALL_PARTIALS],
                      _unpack(_sum_partials("small_grad_sum", small_landed), [(sz,) for _, sz in SMALL_PARTIALS])))
    summed["norm_mix_g"] = _sum_partials("mix_gain_grad_sum", mix_landed).reshape(-1)
    return loss8[0, 0], grad_x, landed, summed


def _small_grads(summed, sm):
    _, vjp = jax.vjp(_s5_discretise, sm["ssm_a_re"], sm["ssm_a_im"], sm["ssm_log_dt"], sm["ssm_b_re"], sm["ssm_b_im"])
    gp = (SSM_GROUPS, SSM_STATE)
    da_re, da_im, dlog_dt, db_re, db_im = vjp((summed["lam_re"].reshape(gp), summed["lam_im"].reshape(gp),
                                               summed["bbar_re"].reshape(gp + (SSM_GROUP,)),
                                               summed["bbar_im"].reshape(gp + (SSM_GROUP,))))
    return dict(norm_mix_g=summed["norm_mix_g"], ret_gn_g=summed["ret_gn_g"], ssm_a_re=da_re, ssm_a_im=da_im,
                ssm_log_dt=dlog_dt, ssm_b_re=db_re, ssm_b_im=db_im,
                ssm_c_re=summed["c_re"].reshape(SSM_GROUPS, SSM_GROUP, SSM_STATE),
                ssm_c_im=summed["c_im"].reshape(SSM_GROUPS, SSM_GROUP, SSM_STATE),
                ssm_d=summed["ssm_d"], ssm_b_glu=summed["b_glu"], ssm_out_g=summed["out_g"],
                norm_ffn_g=summed["norm_ffn_g"], norm_final_g=summed["norm_final_g"])


def _adamw_math(w, g, m, v):
    m2 = ADAM_B1 * m + (1.0 - ADAM_B1) * g
    v2 = ADAM_B2 * v + (1.0 - ADAM_B2) * (g * g)
    delta = -ADAM_LR * ((m2 / ADAM_BC1) / (jnp.sqrt(v2 / ADAM_BC2) + ADAM_EPS) + ADAM_WD * w)
    return delta, m2, v2


def _adamw_shard(name, parts, w, m, v, tr):
    rows, cols = w.shape

    def body(p_ref, w_ref, m_ref, v_ref, g_ref, d_ref, m2_ref, v2_ref):
        g = p_ref[0].astype(F32)
        for s in range(1, N_DEV):
            g = g + p_ref[s].astype(F32)
        d, m2, v2 = _adamw_math(w_ref[...], g, m_ref[...], v_ref[...])
        g_ref[...] = g
        d_ref[...] = d
        m2_ref[...] = m2
        v2_ref[...] = v2

    blk = pl.BlockSpec((tr, cols), lambda i: (i, 0))
    oshape = jax.ShapeDtypeStruct((rows, cols), F32)
    return pl.pallas_call(
        body, name=name, grid=(rows // tr,),
        in_specs=[pl.BlockSpec((N_DEV, tr, cols), lambda i: (0, i, 0)), blk, blk, blk],
        out_specs=[blk, blk, blk, blk], out_shape=[oshape] * 4,
        compiler_params=_params(1),
    )(parts, w, m, v)


def _sum_partials(name, parts):
    rows = parts.shape[1]

    def body(p_ref, o_ref):
        g = p_ref[0]
        for s in range(1, N_DEV):
            g = g + p_ref[s]
        o_ref[...] = g

    return pl.pallas_call(
        body, name=name, grid=(1,),
        in_specs=[pl.BlockSpec((N_DEV, rows, LANE), lambda i: (0, 0, 0))],
        out_specs=pl.BlockSpec((rows, LANE), lambda i: (0, 0)),
        out_shape=jax.ShapeDtypeStruct((rows, LANE), F32),
        compiler_params=_params(1),
    )(parts)


def _adamw_small(ws, gs, ms, vs):
    n = len(ws)

    def body(*refs):
        for i in range(n):
            w_ref, g_ref, m_ref, v_ref = (refs[k * n + i] for k in range(4))
            d_ref, m2_ref, v2_ref = (refs[(4 + k) * n + i] for k in range(3))
            d, m2, v2 = _adamw_math(w_ref[...], g_ref[...], m_ref[...], v_ref[...])
            d_ref[...] = d
            m2_ref[...] = m2
            v2_ref[...] = v2

    vmem = pl.BlockSpec(memory_space=pltpu.VMEM)
    out = pl.pallas_call(
        body, name="adamw_small", in_specs=[vmem] * (4 * n), out_specs=[vmem] * (3 * n),
        out_shape=[jax.ShapeDtypeStruct(w.shape, F32) for w in ws] * 3,
        compiler_params=pltpu.CompilerParams(vmem_limit_bytes=VMEM_LIMIT),
    )(*ws, *gs, *ms, *vs)
    return out[:n], out[n:2 * n], out[2 * n:]


def _pack(arrays):
    parts = [a.reshape(-1, LANE) for a in arrays]
    assert all(p.shape[0] % 8 == 0 for p in parts)
    return parts[0] if len(parts) == 1 else jnp.concatenate(parts, axis=0)


def _unpack(packed, shapes):
    flat = packed.reshape(-1)
    out, off = [], 0
    for shp in shapes:
        n = math.prod(shp)
        out.append(flat[off:off + n].reshape(shp))
        off += n + ((-n) % LANE)
    return out


WEIGHTS = ("norm_mix_g", "w_in", "ret_gn_g", "ssm_a_re", "ssm_a_im", "ssm_log_dt", "ssm_b_re", "ssm_b_im",
           "ssm_c_re", "ssm_c_im", "ssm_d", "ssm_w_glu", "ssm_b_glu", "ssm_out_g", "w_out", "norm_ffn_g", "w_gate",
           "w_up", "w_down", "norm_final_g")
BIG = ("w_in", "ssm_w_glu", "w_out", "w_gate", "w_up", "w_down")
SMALL = tuple(n for n in WEIGHTS if n not in BIG)
ADAM_ROWS = {"w_in": 256, "ssm_w_glu": 128, "w_out": 128, "w_gate": 256, "w_up": 256, "w_down": 176}


def kernel(x, norm_mix_g, w_in, ret_gn_g, ssm_a_re, ssm_a_im, ssm_log_dt, ssm_b_re, ssm_b_im, ssm_c_re, ssm_c_im, ssm_d, ssm_w_glu, ssm_b_glu, ssm_out_g, w_out, norm_ffn_g, w_gate, w_up, w_down, norm_final_g, loss_target, m_norm_mix_g, m_w_in, m_ret_gn_g, m_ssm_a_re, m_ssm_a_im, m_ssm_log_dt, m_ssm_b_re, m_ssm_b_im, m_ssm_c_re, m_ssm_c_im, m_ssm_d, m_ssm_w_glu, m_ssm_b_glu, m_ssm_out_g, m_w_out, m_norm_ffn_g, m_w_gate, m_w_up, m_w_down, m_norm_final_g, v_norm_mix_g, v_w_in, v_ret_gn_g, v_ssm_a_re, v_ssm_a_im, v_ssm_log_dt, v_ssm_b_re, v_ssm_b_im, v_ssm_c_re, v_ssm_c_im, v_ssm_d, v_ssm_w_glu, v_ssm_b_glu, v_ssm_out_g, v_w_out, v_norm_ffn_g, v_w_gate, v_w_up, v_w_down, v_norm_final_g):
    given = dict(locals())
    w = {n: given[n] for n in WEIGHTS}
    m = {n: given["m_" + n] for n in WEIGHTS}
    v = {n: given["v_" + n] for n in WEIGHTS}
    drop = lambda n, a: a if n == "norm_final_g" else a[0]
    w0 = {n: drop(n, w[n]) for n in WEIGHTS}
    m0 = {n: drop(n, m[n]) for n in WEIGHTS}
    v0 = {n: drop(n, v[n]) for n in WEIGHTS}

    sm = {n: w0[n] for n in SMALL}
    shards = {n: w0[n].astype(BF16) for n in BIG}
    loss_local, grad_x, landed, summed = _forward_backward(x[0], loss_target[0], shards, sm)
    loss = lax.psum(loss_local, MESH_AXES)
    gsmall = _small_grads(summed, sm)

    grads, delta, new_m, new_v = {}, {}, {}, {}
    for n in BIG:
        g, d, m2, v2 = _adamw_shard("adamw_" + n, landed[n], w0[n], m0[n], v0[n], ADAM_ROWS[n])
        grads[n], delta[n], new_m[n], new_v[n] = g, d, m2, v2
    as_given = lambda n, a: a.reshape(1, -1) if n == "norm_final_g" else a.reshape(w[n].shape)
    gs = [as_given(n, gsmall[n]) for n in SMALL]
    ds, m2s, v2s = _adamw_small([as_given(n, w[n]) for n in SMALL], gs, [as_given(n, m[n]) for n in SMALL],
                                [as_given(n, v[n]) for n in SMALL])
    for n, g, d, m2, v2 in zip(SMALL, gs, ds, m2s, v2s):
        grads[n], delta[n], new_m[n], new_v[n] = g, d, m2, v2

    lift = lambda n, a: a.reshape(w[n].shape)
    return (loss, grad_x[None], *[lift(n, grads[n]) for n in WEIGHTS], *[lift(n, delta[n]) for n in WEIGHTS],
            *[lift(n, new_m[n]) for n in WEIGHTS], *[lift(n, new_v[n]) for n in WEIGHTS])
```

```python
import functools
import math

import jax
import jax.numpy as jnp
from jax import lax
from jax.experimental import pallas as pl
from jax.experimental.pallas import tpu as pltpu

F32 = jnp.float32
BF16 = jnp.bfloat16

D_MODEL = 2048
RET_WIDTH = 1024
RET_HEADS = 8
HEAD_DIM = 128
CHUNK = 64
SSM_WIDTH = 1024
SSM_GROUP = 16
SSM_GROUPS = 64
SSM_STATE = 64
D_FF = 5632
IN_WIDTH = 5120
ROPE_BASE = 10000.0
EPS = 1e-6
N_DEV = 8
MESH_AXES = ("x", "y", "c")

WIN_BLK = IN_WIDTH // N_DEV
FF_BLK = D_FF // N_DEV
RET_BLOCK = 256
RET_HPS = 4
RET_PER_STEP = 2
S5_TILE = 256
S5_CHUNKS = 8
S5_STEPS = S5_TILE // S5_CHUNKS
S5_PER_STEP = 4
S5_GB = 8
S5_NBLK = SSM_GROUPS // S5_GB
S5_LANES = S5_GB * SSM_STATE
LANE = 128

ADAM_LR = 0.001
ADAM_B1 = 0.9
ADAM_B2 = 0.999
ADAM_EPS = 1e-08
ADAM_WD = 0.01
ADAM_STEP = 10
ADAM_BC1 = 1.0 - ADAM_B1 ** ADAM_STEP
ADAM_BC2 = 1.0 - ADAM_B2 ** ADAM_STEP

VMEM_LIMIT = 56 * 1024 * 1024

NT = (((1,), (1,)), ((), ()))
TN = (((0,), (0,)), ((), ()))


def _params(n_grid):
    return pltpu.CompilerParams(dimension_semantics=("arbitrary",) * n_grid, vmem_limit_bytes=VMEM_LIMIT)


def _dot(a, b):
    return jnp.dot(a, b, preferred_element_type=F32)


def _dot_nt(a, b):
    return lax.dot_general(a, b, NT, preferred_element_type=F32)


def _dot_tn(a, b):
    return lax.dot_general(a, b, TN, preferred_element_type=F32)


def _sigmoid(x):
    return 1.0 / (1.0 + jnp.exp(-x))


_GELU_C = math.sqrt(2.0 / math.pi)
_GELU_A = 0.044715


def _gelu(x):
    t = jnp.tanh(_GELU_C * (x + _GELU_A * x * x * x))
    return 0.5 * x * (1.0 + t)


def _gelu_and_grad(x):
    t = jnp.tanh(_GELU_C * (x + _GELU_A * x * x * x))
    g = 0.5 * (1.0 + t) + 0.5 * x * (1.0 - t * t) * _GELU_C * (1.0 + 3.0 * _GELU_A * x * x)
    return 0.5 * x * (1.0 + t), g


def _rms_bwd(dy, x, r, g):
    w = dy * g
    dx = r * w - x * (r * r * r) * jnp.mean(w * x, axis=-1, keepdims=True)
    return dx, dy * x * r


HBM_SPEC = pl.BlockSpec(memory_space=pltpu.HBM)
ANY_SPEC = pl.BlockSpec(memory_space=pl.ANY)


def _load_resident(src_hbm, dst_vmem, sem):
    cp = pltpu.make_async_copy(src_hbm, dst_vmem, sem)
    cp.start()
    cp.wait()


def _my_block():
    return 4 * lax.axis_index("x") + 2 * lax.axis_index("y") + lax.axis_index("c")


def _peer(k):
    px = lax.axis_index("x") ^ ((k >> 2) & 1)
    py = lax.axis_index("y") ^ ((k >> 1) & 1)
    pc = lax.axis_index("c") ^ (k & 1)
    return (px, py, pc), 4 * px + 2 * py + pc


class _Exchange:
    def __init__(self, payloads, gather, via_sibling=False):
        self.payloads = list(payloads)
        self.n = len(self.payloads)
        self.gather = [gather] * self.n if isinstance(gather, bool) else list(gather)
        self.via_sibling = via_sibling
        assert not via_sibling or all(self.gather)

    def out_shape(self):
        return [jax.ShapeDtypeStruct(((N_DEV,) if g else ()) + p.shape, p.dtype)
                for p, g in zip(self.payloads, self.gather)]

    def scratch_shapes(self):
        return [pltpu.SemaphoreType.DMA((self.n, N_DEV - 1)), pltpu.SemaphoreType.DMA((self.n, N_DEV - 1)),
                pltpu.SemaphoreType.DMA((self.n,))]

    def _copies(self, ins, outs, sems, incoming):
        send_sems, recv_sems, local_sems = sems
        me = _my_block()
        src_of = lambda i, blk: ins[i] if self.gather[i] else ins[i].at[blk]
        local, remote = [], []
        for i in range(self.n):
            if not incoming:
                local.append(pltpu.make_async_copy(src_of(i, me), outs[i].at[me], local_sems.at[i]))
            for k in range(1, N_DEV):
                dev, blk = _peer(k)
                src, dst = (outs[i].at[blk], outs[i].at[blk]) if incoming else (src_of(i, blk), outs[i].at[me])
                remote.append(pltpu.make_async_remote_copy(
                    src_ref=src, dst_ref=dst, send_sem=send_sems.at[i, k - 1], recv_sem=recv_sems.at[i, k - 1],
                    device_id=dev, device_id_type=pl.DeviceIdType.MESH))
        return local, remote

    def _copy(self, i, k, outs, sems, src, dst_blk, to_k):
        send_sems, recv_sems, _ = sems
        return pltpu.make_async_remote_copy(
            src_ref=src, dst_ref=outs[i].at[dst_blk], send_sem=send_sems.at[i, k - 1], recv_sem=recv_sems.at[i, k - 1],
            device_id=_peer(to_k)[0], device_id_type=pl.DeviceIdType.MESH)

    FIRST_HOPS = (1, 2, 4, 6)
    FROM_CHIPS = (2, 4, 6)

    def start(self, ins, outs, sems):
        if not self.via_sibling:
            local, sends = self._copies(ins, outs, sems, False)
            for cp in local + sends:
                cp.start()
            return
        me = _my_block()
        for i in range(self.n):
            pltpu.make_async_copy(ins[i], outs[i].at[me], sems[2].at[i]).start()
            for k in self.FIRST_HOPS:
                self._copy(i, k, outs, sems, ins[i], me, k).start()

    def wait(self, ins, outs, sems):
        if not self.via_sibling:
            for cp in self._copies(ins, outs, sems, True)[1]:
                cp.wait_recv()
            local, sends = self._copies(ins, outs, sems, False)
            for cp in sends:
                cp.wait_send()
            for cp in local:
                cp.wait()
            return
        me = _my_block()
        landed = lambda i, k: self._copy(i, k, outs, sems, outs[i].at[_peer(k)[1]], _peer(k)[1], k)
        for i in range(self.n):
            for s in self.FROM_CHIPS:
                landed(i, s).wait_recv()
                self._copy(i, s ^ 1, outs, sems, outs[i].at[_peer(s)[1]], _peer(s)[1], 1).start()
        for i in range(self.n):
            for k in (1, 3, 5, 7):
                landed(i, k).wait_recv()
            for k in self.FIRST_HOPS:
                self._copy(i, k, outs, sems, ins[i], me, k).wait_send()
            for s in self.FROM_CHIPS:
                self._copy(i, s ^ 1, outs, sems, outs[i].at[_peer(s)[1]], _peer(s)[1], 1).wait_send()
            pltpu.make_async_copy(ins[i], outs[i].at[me], sems[2].at[i]).wait()


def _pcall(body, name, grid, in_specs, out_specs, out_shape, scratch_shapes, args, carry=None):
    n_in, n_out, n_scr = len(in_specs), len(out_specs), len(scratch_shapes)
    if carry is None:
        return pl.pallas_call(body, name=name, grid=grid, in_specs=in_specs, out_specs=out_specs, out_shape=out_shape,
                              scratch_shapes=scratch_shapes, compiler_params=_params(len(grid)))(*args)
    nx = carry.n

    def wrapped(*refs):
        cin, xin = refs[:n_in], refs[n_in:n_in + nx]
        cout, xout = refs[n_in + nx:n_in + nx + n_out], refs[n_in + nx + n_out:n_in + 2 * nx + n_out]
        rest = refs[n_in + 2 * nx + n_out:]
        cscr, sems = rest[:n_scr], rest[n_scr:]
        first = functools.reduce(jnp.logical_and, [pl.program_id(a) == 0 for a in range(len(grid))])
        last = functools.reduce(jnp.logical_and, [pl.program_id(a) == grid[a] - 1 for a in range(len(grid))])

        @pl.when(first)
        def _():
            carry.start(xin, xout, sems)

        body(*cin, *cout, *cscr)

        @pl.when(last)
        def _():
            carry.wait(xin, xout, sems)

    return pl.pallas_call(
        wrapped, name=name, grid=grid, in_specs=list(in_specs) + [HBM_SPEC] * nx,
        out_specs=list(out_specs) + [HBM_SPEC] * nx, out_shape=list(out_shape) + carry.out_shape(),
        scratch_shapes=list(scratch_shapes) + carry.scratch_shapes(), compiler_params=_params(len(grid)),
    )(*args, *carry.payloads)


def _exchange_call(name, payloads, gather, via_sibling=False):
    ex = _Exchange(payloads, gather, via_sibling)

    def body(*refs):
        ins, outs, sems = refs[:ex.n], refs[ex.n:2 * ex.n], refs[2 * ex.n:]
        ex.start(ins, outs, sems)
        ex.wait(ins, outs, sems)

    return pl.pallas_call(body, name=name, in_specs=[HBM_SPEC] * ex.n, out_specs=[HBM_SPEC] * ex.n,
                          out_shape=ex.out_shape(), scratch_shapes=ex.scratch_shapes())(*ex.payloads)


def _in_proj_fwd(x, g, w, tm, carry=None):
    T = x.shape[0]

    def body(x_ref, g_ref, w_hbm, proj_ref, h_ref, r_ref, w_ref, sem):
        @pl.when(pl.program_id(0) == 0)
        def _():
            _load_resident(w_hbm, w_ref, sem)

        xf = x_ref[...]
        r = lax.rsqrt(jnp.mean(xf * xf, axis=-1, keepdims=True) + EPS)
        h = (xf * r * g_ref[...]).astype(BF16)
        h_ref[...] = h
        r_ref[...] = r
        for j in range(N_DEV):
            proj_ref[:, j * WIN_BLK:(j + 1) * WIN_BLK] = _dot(h, w_ref[j])

    return _pcall(
        body, "in_proj_fwd", (T // tm,),
        [pl.BlockSpec((tm, D_MODEL), lambda i: (i, 0)), pl.BlockSpec((1, D_MODEL), lambda i: (0, 0)), ANY_SPEC],
        [pl.BlockSpec((tm, IN_WIDTH), lambda i: (i, 0)),
         pl.BlockSpec((tm, D_MODEL), lambda i: (i, 0)),
         pl.BlockSpec((tm, 1), lambda i: (i, 0))],
        [jax.ShapeDtypeStruct((T, IN_WIDTH), F32),
         jax.ShapeDtypeStruct((T, D_MODEL), BF16),
         jax.ShapeDtypeStruct((T, 1), F32)],
        [pltpu.VMEM(w.shape, w.dtype), pltpu.SemaphoreType.DMA], (x, g, w), carry)


def _ret_common(q_ref, k_ref, v_ref, cos_ref, sin_ref, mask_ref, rd_ref, sin_state):
    c = cos_ref[...]
    s = sin_ref[...]
    q = q_ref[...]
    q = q * c + pltpu.roll(q, HEAD_DIM // 2, 1) * s
    k = k_ref[...]
    k = (k * c + pltpu.roll(k, HEAD_DIM // 2, 1) * s) * (HEAD_DIM ** -0.5)
    qb = q.astype(BF16)
    kb = k.astype(BF16)
    vb = v_ref[...].astype(BF16)
    pm = (_dot_nt(qb, kb) * mask_ref[...]).astype(BF16)
    qd = (q * rd_ref[...]).astype(BF16)
    o = _dot(pm, vb) + _dot(qd, sin_state.astype(BF16))
    return q, k, qb, kb, vb, pm, qd, o


def _ret_specs(T, rev):
    rows = RET_BLOCK * RET_PER_STEP
    nb = T // rows
    groups = RET_HEADS // RET_HPS
    wide = RET_HPS * HEAD_DIM
    blk = (lambda b: nb - 1 - b) if rev else (lambda b: b)
    col = lambda piece: (pl.BlockSpec((rows, wide), lambda h, b: (blk(b), piece * groups + h)), "rows_lane")
    return dict(
        q=col(0), k=col(1), v=col(2), g=col(3),
        tab=(pl.BlockSpec((rows, HEAD_DIM), lambda h, b: (blk(b), 0)), "rows"),
        mask=(pl.BlockSpec((RET_HPS, RET_BLOCK, RET_BLOCK), lambda h, b: (h, 0, 0)), "lead"),
        dec=(pl.BlockSpec((RET_HPS, RET_BLOCK, HEAD_DIM), lambda h, b: (h, 0, 0)), "lead"),
        gtb=(pl.BlockSpec((RET_HPS, 1, HEAD_DIM), lambda h, b: (h, 0, 0)), "lead"),
        gn=(pl.BlockSpec((1, wide), lambda h, b: (0, h)), "lane"),
        state=(pl.BlockSpec((RET_HPS, RET_PER_STEP, HEAD_DIM, HEAD_DIM), lambda h, b: (h, blk(b), 0, 0)), "state"),
        rows=(pl.BlockSpec((rows, wide), lambda h, b: (blk(b), h)), "rows_lane"),
        scratch=(pltpu.VMEM((RET_HPS, HEAD_DIM, HEAD_DIM), F32), "lead"),
    )


def _per_head(head_body, kinds, rev):
    def body(*refs):
        order = list(reversed(range(RET_PER_STEP))) if rev else list(range(RET_PER_STEP))
        for hh in range(RET_HPS):
            lanes = slice(hh * HEAD_DIM, (hh + 1) * HEAD_DIM)
            for s in order:
                rows = slice(s * RET_BLOCK, (s + 1) * RET_BLOCK)
                cut = {"rows_lane": lambda r: r.at[rows, lanes], "rows": lambda r: r.at[rows, :],
                       "lane": lambda r: r.at[:, lanes], "lead": lambda r: r.at[hh], "state": lambda r: r.at[hh, s]}
                head_body(s == order[0], *[cut[kind](ref) for ref, kind in zip(refs, kinds)])
    return body


def _ret_fwd(proj, cosf, sinf, mask, rowdec, kdec, gtb, gn, carry=None):
    T = proj.shape[0]
    nb = T // RET_BLOCK
    sp = _ret_specs(T, False)

    def body(first, q_ref, k_ref, v_ref, g_ref, cos_ref, sin_ref, mask_ref, rd_ref, kd_ref, gtb_ref, gn_ref,
             y_ref, sb_ref, st):
        if first:
            @pl.when(pl.program_id(1) == 0)
            def _():
                st[...] = jnp.zeros_like(st)
        s_in = st[...]
        sb_ref[...] = s_in
        q, k, qb, kb, vb, pm, qd, o = _ret_common(q_ref, k_ref, v_ref, cos_ref, sin_ref, mask_ref, rd_ref, s_in)
        st[...] = gtb_ref[...] * s_in + _dot_tn((k * kd_ref[...]).astype(BF16), vb)
        mu = jnp.mean(o, axis=-1, keepdims=True)
        oc = o - mu
        n = oc * lax.rsqrt(jnp.mean(oc * oc, axis=-1, keepdims=True) + EPS)
        gt = g_ref[...]
        y_ref[...] = (gt * _sigmoid(gt) * (n * gn_ref[...])).astype(BF16)

    ins = [sp[n] for n in ("q", "k", "v", "g", "tab", "tab", "mask", "dec", "dec", "gtb", "gn")]
    outs = [sp["rows"], sp["state"]]
    return _pcall(
        _per_head(body, [kind for _, kind in ins + outs + [sp["scratch"]]], False), "ret_fwd",
        (RET_HEADS // RET_HPS, nb // RET_PER_STEP), [s for s, _ in ins], [s for s, _ in outs],
        [jax.ShapeDtypeStruct((T, RET_WIDTH), BF16),
         jax.ShapeDtypeStruct((RET_HEADS, nb, HEAD_DIM, HEAD_DIM), F32)],
        [sp["scratch"][0]],
        (proj, proj, proj, proj, cosf, sinf, mask, rowdec, kdec, gtb, gn), carry)


def _scan(re, im, ar, ai, reverse):
    n = re.shape[0]
    row = lax.broadcasted_iota(jnp.int32, re.shape, 0)
    s = 1
    while s < n:
        if reverse:
            keep = row < n - s
            sr = jnp.where(keep, pltpu.roll(re, n - s, 0), 0.0)
            si = jnp.where(keep, pltpu.roll(im, n - s, 0), 0.0)
        else:
            keep = row >= s
            sr = jnp.where(keep, pltpu.roll(re, s, 0), 0.0)
            si = jnp.where(keep, pltpu.roll(im, s, 0), 0.0)
        re, im = re + ar * sr - ai * si, im + ar * si + ai * sr
        ar, ai = ar * ar - ai * ai, 2.0 * ar * ai
        s *= 2
    return re, im


S5_STATE_TILE = (S5_TILE, S5_LANES)


def _step_major_permutation():
    r = jnp.arange(S5_TILE)
    t_of_row = (r % S5_CHUNKS) * S5_STEPS + r // S5_CHUNKS
    return (t_of_row[:, None] == r[None, :]).astype(BF16)


def _permute_rows_f32(pm, x):
    hi = x.astype(BF16)
    rest = x - hi.astype(F32)
    mid = rest.astype(BF16)
    lo = (rest - mid.astype(F32)).astype(BF16)
    return _dot(pm, hi) + _dot(pm, mid) + _dot(pm, lo)


def _step_get(ref, j):
    return ref[j * S5_CHUNKS:(j + 1) * S5_CHUNKS, :]


def _step_set(ref, j, val):
    ref[j * S5_CHUNKS:(j + 1) * S5_CHUNKS, :] = val


def _tile_get(ref):
    return ref[...]


def _tile_set(ref, val):
    ref[...] = val


def _fill_power_table(ptab, lr, li):
    shape = (S5_CHUNKS, S5_LANES)
    lrb = jnp.broadcast_to(lr, shape)
    lib = jnp.broadcast_to(li, shape)
    pr, pi_ = lrb, lib
    for j in range(S5_STEPS):
        ptab[0, j * S5_CHUNKS:(j + 1) * S5_CHUNKS, :] = pr
        ptab[1, j * S5_CHUNKS:(j + 1) * S5_CHUNKS, :] = pi_
        pr, pi_ = lrb * pr - lib * pi_, lrb * pi_ + lib * pr


def _chunk_scans(xr, xi, lr, li, reverse):
    shape = (S5_CHUNKS, S5_LANES)
    lrb = jnp.broadcast_to(lr, shape)
    lib = jnp.broadcast_to(li, shape)
    sr = si = None
    for j in (range(S5_STEPS - 1, -1, -1) if reverse else range(S5_STEPS)):
        vr = _step_get(xr, j)
        vi = _step_get(xi, j)
        if sr is not None:
            vr, vi = vr + lrb * sr - lib * si, vi + lrb * si + lib * sr
            _step_set(xr, j, vr)
            _step_set(xi, j, vi)
        sr, si = vr, vi
    return sr, si


def _entering_states(zr, zi, cr, ci, ar, ai, reverse):
    shape = (S5_CHUNKS, S5_LANES)
    row = lax.broadcasted_iota(jnp.int32, shape, 0)
    if reverse:
        edge, shift = row == S5_CHUNKS - 1, S5_CHUNKS - 1
    else:
        edge, shift = row == 0, 1
    wr = jnp.where(edge, jnp.broadcast_to(cr, shape), pltpu.roll(zr, shift, 0))
    wi = jnp.where(edge, jnp.broadcast_to(ci, shape), pltpu.roll(zi, shift, 0))
    return _scan(wr, wi, ar, ai, reverse)


def _table_rows(ptab, j, conj):
    pr = ptab[0, j * S5_CHUNKS:(j + 1) * S5_CHUNKS, :]
    pi_ = ptab[1, j * S5_CHUNKS:(j + 1) * S5_CHUNKS, :]
    return pr, (-pi_ if conj else pi_)


def _s5_forward_states(xr, xi, lr, li, cr, ci, ptab):
    zr, zi = _chunk_scans(xr, xi, lr, li, False)
    ar, ai = _table_rows(ptab, S5_STEPS - 1, False)
    er, ei = _entering_states(zr, zi, cr, ci, ar, ai, False)
    for j in range(S5_STEPS):
        pr, pi_ = _table_rows(ptab, j, False)
        _step_set(xr, j, _step_get(xr, j) + pr * er - pi_ * ei)
        _step_set(xi, j, _step_get(xi, j) + pr * ei + pi_ * er)
    last = S5_CHUNKS - 1
    end_r = (ar * er - ai * ei + zr)[last:last + 1, :]
    end_i = (ar * ei + ai * er + zi)[last:last + 1, :]
    return er, ei, end_r, end_i


def _s5_specs(T, rev):
    rows = S5_TILE * S5_PER_STEP
    nt = T // rows
    tt = (lambda t: nt - 1 - t) if rev else (lambda t: t)
    return dict(
        u=pl.BlockSpec((rows, LANE), lambda b, t: (tt(t), 4 * RET_HEADS + b)),
        rows=pl.BlockSpec((rows, LANE), lambda b, t: (tt(t), b)),
        to_state=pl.BlockSpec((None, LANE, S5_LANES), lambda b, t: (b, 0, 0)),
        from_state=pl.BlockSpec((None, S5_LANES, LANE), lambda b, t: (b, 0, 0)),
        lam=pl.BlockSpec((None, 2, S5_LANES), lambda b, t: (b, 0, 0)),
        d=pl.BlockSpec((1, LANE), lambda b, t: (0, b)),
        perm=pl.BlockSpec((S5_TILE, S5_TILE), lambda b, t: (0, 0)),
        bound=pl.BlockSpec((None, S5_PER_STEP, 2, S5_LANES), lambda b, t: (b, tt(t), 0, 0)),
    )


def _s5_fwd(proj, pm, pm_t, bre, bim, cre_t, cim_t, lam, d, carry=None):
    T = proj.shape[0]
    nt = T // S5_TILE
    sp = _s5_specs(T, False)

    def body(u_ref, pm_ref, pmt_ref, bre_ref, bim_ref, cre_ref, cim_ref, lam_ref, d_ref, y_ref, bound_ref,
             carry, ptab, xr, xi):
        lr = lam_ref[0:1, :]
        li = lam_ref[1:2, :]

        @pl.when(pl.program_id(1) == 0)
        def _():
            carry[...] = jnp.zeros_like(carry)
            _fill_power_table(ptab, lr, li)

        for s in range(S5_PER_STEP):
            rows = slice(s * S5_TILE, (s + 1) * S5_TILE)
            u = _permute_rows_f32(pm_ref[...], u_ref[rows, :])
            ub = u.astype(BF16)
            _tile_set(xr, _dot(ub, bre_ref[...]))
            _tile_set(xi, _dot(ub, bim_ref[...]))
            bound_ref[s] = carry[...]
            _, _, end_r, end_i = _s5_forward_states(xr, xi, lr, li, carry[0:1, :], carry[1:2, :], ptab)
            carry[0:1, :] = end_r
            carry[1:2, :] = end_i
            y = (_dot(_tile_get(xr).astype(BF16), cre_ref[...]) - _dot(_tile_get(xi).astype(BF16), cim_ref[...])
                 + d_ref[...] * u)
            y_ref[rows, :] = _permute_rows_f32(pmt_ref[...], y)

    state = pltpu.VMEM(S5_STATE_TILE, F32)
    return _pcall(
        body, "s5_fwd", (S5_NBLK, nt // S5_PER_STEP),
        [sp["u"], sp["perm"], sp["perm"], sp["to_state"], sp["to_state"], sp["from_state"],
         sp["from_state"], sp["lam"], sp["d"]],
        [sp["rows"], sp["bound"]],
        [jax.ShapeDtypeStruct((T, SSM_WIDTH), F32),
         jax.ShapeDtypeStruct((S5_NBLK, nt, 2, S5_LANES), F32)],
        [pltpu.VMEM((2, S5_LANES), F32), pltpu.VMEM((2, S5_TILE, S5_LANES), F32), state, state],
        (proj, pm, pm_t, bre, bim, cre_t, cim_t, lam, d), carry)


def _glu_fwd(y, w, b, og, tm):
    T = y.shape[0]

    def body(y_ref, w_ref, b_ref, og_ref, z_ref, o_ref, r_ref):
        y1 = _gelu(y_ref[...])
        z = _dot(y1.astype(BF16), w_ref[...]) + b_ref[...]
        y2 = y1 * _sigmoid(z)
        r = lax.rsqrt(jnp.mean(y2 * y2, axis=-1, keepdims=True) + EPS)
        z_ref[...] = z
        o_ref[...] = (y2 * r * og_ref[...]).astype(BF16)
        r_ref[...] = r

    row = pl.BlockSpec((tm, SSM_WIDTH), lambda i: (i, 0))
    vec = pl.BlockSpec((1, SSM_WIDTH), lambda i: (0, 0))
    return pl.pallas_call(
        body, name="glu_fwd", grid=(T // tm,),
        in_specs=[row, pl.BlockSpec((SSM_WIDTH, SSM_WIDTH), lambda i: (0, 0)), vec, vec],
        out_specs=[row, row, pl.BlockSpec((tm, 1), lambda i: (i, 0))],
        out_shape=[jax.ShapeDtypeStruct((T, SSM_WIDTH), F32), jax.ShapeDtypeStruct((T, SSM_WIDTH), BF16),
                   jax.ShapeDtypeStruct((T, 1), F32)],
        compiler_params=_params(1),
    )(y, w, b, og)


def _out_proj_fwd(x, y_ret, y_ssm, w, g, tm):
    T = x.shape[0]

    def body(x_ref, a_ref, b_ref, w_ref, g_ref, x2_ref, h_ref, r_ref):
        x2 = x_ref[...] + _dot(a_ref[...], w_ref[0:RET_WIDTH, :]) + _dot(b_ref[...], w_ref[RET_WIDTH:D_MODEL, :])
        r = lax.rsqrt(jnp.mean(x2 * x2, axis=-1, keepdims=True) + EPS)
        x2_ref[...] = x2
        h_ref[...] = (x2 * r * g_ref[...]).astype(BF16)
        r_ref[...] = r

    full = pl.BlockSpec((tm, D_MODEL), lambda i: (i, 0))
    half = pl.BlockSpec((tm, RET_WIDTH), lambda i: (i, 0))
    return pl.pallas_call(
        body, name="out_proj_fwd", grid=(T // tm,),
        in_specs=[full, half, half, pl.BlockSpec((D_MODEL, D_MODEL), lambda i: (0, 0)),
                  pl.BlockSpec((1, D_MODEL), lambda i: (0, 0))],
        out_specs=[full, full, pl.BlockSpec((tm, 1), lambda i: (i, 0))],
        out_shape=[jax.ShapeDtypeStruct((T, D_MODEL), F32), jax.ShapeDtypeStruct((T, D_MODEL), BF16),
                   jax.ShapeDtypeStruct((T, 1), F32)],
        compiler_params=_params(1),
    )(x, y_ret, y_ssm, w, g)


def _ffn_up(h, wg, wu, tm, carry=None):
    T = h.shape[0]

    def body(h_ref, wg_ref, wu_ref, a_ref, b_ref, f_ref):
        hb = h_ref[...]
        a = _dot(hb, wg_ref[...])
        b = _dot(hb, wu_ref[...])
        a_ref[...] = a.astype(BF16)
        b_ref[...] = b.astype(BF16)
        f_ref[...] = (a * _sigmoid(a) * b).astype(BF16)

    wspec = pl.BlockSpec((None, D_MODEL, FF_BLK), lambda j, i: (j, 0, 0))
    ospec = pl.BlockSpec((None, tm, FF_BLK), lambda j, i: (j, i, 0))
    oshape = jax.ShapeDtypeStruct((N_DEV, T, FF_BLK), BF16)
    return _pcall(
        body, "ffn_up", (N_DEV, T // tm),
        [pl.BlockSpec((tm, D_MODEL), lambda j, i: (i, 0)), wspec, wspec],
        [ospec, ospec, ospec], [oshape, oshape, oshape], [], (h, wg, wu), carry)


def _ffn_down_loss(f, wd, x2, tgt, g, tm):
    T = x2.shape[0]

    def body(f_ref, w_hbm, x2_ref, t_ref, g_ref, dx_ref, dxb_ref, loss_ref, dg_ref, w_ref, sem):
        i = pl.program_id(0)

        @pl.when(i == 0)
        def _():
            _load_resident(w_hbm, w_ref, sem)
            loss_ref[...] = jnp.zeros_like(loss_ref)
            dg_ref[...] = jnp.zeros_like(dg_ref)

        gv = g_ref[...]
        x3 = x2_ref[...]
        for k in range(N_DEV):
            x3 = x3 + _dot(f_ref[k], w_ref[k])
        r = lax.rsqrt(jnp.mean(x3 * x3, axis=-1, keepdims=True) + EPS)
        err = x3 * r * gv - t_ref[...]
        tile_loss = 0.5 * jnp.sum(jnp.mean(err * err, axis=-1, keepdims=True), axis=0, keepdims=True)
        dx, dgt = _rms_bwd(err * (1.0 / D_MODEL), x3, r, gv)
        dx_ref[...] = dx
        dxb_ref[...] = dx.astype(BF16)
        loss_ref[...] += jnp.broadcast_to(tile_loss, loss_ref.shape)
        dg_ref[...] += jnp.sum(dgt, axis=0, keepdims=True)

    full = pl.BlockSpec((tm, D_MODEL), lambda i: (i, 0))
    vec = pl.BlockSpec((1, D_MODEL), lambda i: (0, 0))
    return pl.pallas_call(
        body, name="ffn_down_loss", grid=(T // tm,),
        in_specs=[pl.BlockSpec((N_DEV, tm, FF_BLK), lambda i: (0, i, 0)), ANY_SPEC, full, full, vec],
        out_specs=[full, full, pl.BlockSpec((8, LANE), lambda i: (0, 0)), vec],
        out_shape=[jax.ShapeDtypeStruct((T, D_MODEL), F32), jax.ShapeDtypeStruct((T, D_MODEL), BF16),
                   jax.ShapeDtypeStruct((8, LANE), F32), jax.ShapeDtypeStruct((1, D_MODEL), F32)],
        scratch_shapes=[pltpu.VMEM(wd.shape, wd.dtype), pltpu.SemaphoreType.DMA],
        compiler_params=_params(1),
    )(f, wd, x2, tgt, g)


def _ffn_bwd_act(dxb, wd, a, b, tm):
    T = dxb.shape[0]

    def body(dx_ref, w_ref, a_ref, b_ref, da_ref, db_ref):
        df = _dot_nt(dx_ref[...], w_ref[...])
        a = a_ref[...].astype(F32)
        b = b_ref[...].astype(F32)
        sg = _sigmoid(a)
        da_ref[...] = (df * b * sg * (1.0 + a * (1.0 - sg))).astype(BF16)
        db_ref[...] = (df * a * sg).astype(BF16)

    blk = pl.BlockSpec((None, tm, FF_BLK), lambda j, i: (j, i, 0))
    oshape = jax.ShapeDtypeStruct((N_DEV, T, FF_BLK), BF16)
    return pl.pallas_call(
        body, name="ffn_bwd_act", grid=(N_DEV, T // tm),
        in_specs=[pl.BlockSpec((tm, D_MODEL), lambda j, i: (i, 0)),
                  pl.BlockSpec((None, FF_BLK, D_MODEL), lambda j, i: (j, 0, 0)), blk, blk],
        out_specs=[blk, blk], out_shape=[oshape, oshape],
        compiler_params=_params(2),
    )(dxb, wd, a, b)


def _ffn_bwd_in(da, db, wg, wu, tm, carry=None):
    T = da.shape[1]

    def body(da_ref, db_ref, wg_ref, wu_ref, dh_ref):
        part = _dot_nt(da_ref[...], wg_ref[...]) + _dot_nt(db_ref[...], wu_ref[...])

        @pl.when(pl.program_id(1) == 0)
        def _():
            dh_ref[...] = part

        @pl.when(pl.program_id(1) > 0)
        def _():
            dh_ref[...] += part

    ablk = pl.BlockSpec((None, tm, FF_BLK), lambda i, k: (k, i, 0))
    wblk = pl.BlockSpec((None, D_MODEL, FF_BLK), lambda i, k: (k, 0, 0))
    return _pcall(
        body, "ffn_bwd_in", (T // tm, N_DEV), [ablk, ablk, wblk, wblk],
        [pl.BlockSpec((tm, D_MODEL), lambda i, k: (i, 0))], [jax.ShapeDtypeStruct((T, D_MODEL), F32)],
        [], (da, db, wg, wu), carry)


def _ffn_wgrad_up(h, da, db, tk, carry=None):
    T = h.shape[0]
    nk = T // tk

    def body(h_ref, da_ref, db_ref, g_ref, u_ref, accg, accu):
        k = pl.program_id(1)

        @pl.when(k == 0)
        def _():
            accg[...] = jnp.zeros_like(accg)
            accu[...] = jnp.zeros_like(accu)

        hb = h_ref[...]
        accg[...] += _dot_tn(hb, da_ref[...])
        accu[...] += _dot_tn(hb, db_ref[...])

        @pl.when(k == nk - 1)
        def _():
            g_ref[...] = accg[...].astype(BF16)
            u_ref[...] = accu[...].astype(BF16)

    blk = pl.BlockSpec((None, tk, FF_BLK), lambda j, k: (j, k, 0))
    ospec = pl.BlockSpec((None, D_MODEL, FF_BLK), lambda j, k: (j, 0, 0))
    oshape = jax.ShapeDtypeStruct((N_DEV, D_MODEL, FF_BLK), BF16)
    return _pcall(
        body, "ffn_wgrad_up", (N_DEV, nk),
        [pl.BlockSpec((tk, D_MODEL), lambda j, k: (k, 0)), blk, blk],
        [ospec, ospec], [oshape, oshape],
        [pltpu.VMEM((D_MODEL, FF_BLK), F32), pltpu.VMEM((D_MODEL, FF_BLK), F32)], (h, da, db), carry)


def _ffn_wgrad_down(f, dxb, tk):
    T = dxb.shape[0]
    nk = T // tk

    def body(f_ref, dx_ref, o_ref, acc):
        k = pl.program_id(1)

        @pl.when(k == 0)
        def _():
            acc[...] = jnp.zeros_like(acc)

        acc[...] += _dot_tn(f_ref[...], dx_ref[...])

        @pl.when(k == nk - 1)
        def _():
            o_ref[...] = acc[...].astype(BF16)

    return pl.pallas_call(
        body, name="ffn_wgrad_down", grid=(N_DEV, nk),
        in_specs=[pl.BlockSpec((None, tk, FF_BLK), lambda j, k: (j, k, 0)),
                  pl.BlockSpec((tk, D_MODEL), lambda j, k: (k, 0))],
        out_specs=pl.BlockSpec((None, FF_BLK, D_MODEL), lambda j, k: (j, 0, 0)),
        out_shape=jax.ShapeDtypeStruct((N_DEV, FF_BLK, D_MODEL), BF16),
        scratch_shapes=[pltpu.VMEM((FF_BLK, D_MODEL), F32)],
        compiler_params=_params(2),
    )(f, dxb)


def _out_proj_bwd(dh2, x2, r2, g, dx3, w, tm):
    T = x2.shape[0]

    def body(dh_ref, x_ref, r_ref, g_ref, dx3_ref, w_ref, dx_ref, dxb_ref, dg_ref, a_ref, b_ref):
        @pl.when(pl.program_id(0) == 0)
        def _():
            dg_ref[...] = jnp.zeros_like(dg_ref)

        dxn, dgt = _rms_bwd(dh_ref[...], x_ref[...], r_ref[...], g_ref[...])
        dx = dx3_ref[...] + dxn
        dxv = dx.astype(BF16)
        dx_ref[...] = dx
        dxb_ref[...] = dxv
        dg_ref[...] += jnp.sum(dgt, axis=0, keepdims=True)
        a_ref[...] = _dot_nt(dxv, w_ref[0:RET_WIDTH, :])
        b_ref[...] = _dot_nt(dxv, w_ref[RET_WIDTH:D_MODEL, :])

    full = pl.BlockSpec((tm, D_MODEL), lambda i: (i, 0))
    vec = pl.BlockSpec((1, D_MODEL), lambda i: (0, 0))
    half = pl.BlockSpec((tm, RET_WIDTH), lambda i: (i, 0))
    hshape = jax.ShapeDtypeStruct((T, RET_WIDTH), F32)
    return pl.pallas_call(
        body, name="out_proj_bwd", grid=(T // tm,),
        in_specs=[full, full, pl.BlockSpec((tm, 1), lambda i: (i, 0)), vec, full,
                  pl.BlockSpec((D_MODEL, D_MODEL), lambda i: (0, 0))],
        out_specs=[full, full, vec, half, half],
        out_shape=[jax.ShapeDtypeStruct((T, D_MODEL), F32), jax.ShapeDtypeStruct((T, D_MODEL), BF16),
                   jax.ShapeDtypeStruct((1, D_MODEL), F32), hshape, hshape],
        compiler_params=_params(1),
    )(dh2, x2, r2, g, dx3, w)


def _wgrad_rows(name, a, b, tk):
    T, M = a.shape
    N = b.shape[1]
    nk = T // tk

    def body(a_ref, b_ref, o_ref, acc):
        k = pl.program_id(0)

        @pl.when(k == 0)
        def _():
            acc[...] = jnp.zeros_like(acc)

        acc[...] += _dot_tn(a_ref[...], b_ref[...])

        @pl.when(k == nk - 1)
        def _():
            o_ref[...] = acc[...].astype(BF16)

    return pl.pallas_call(
        body, name=name, grid=(nk,),
        in_specs=[pl.BlockSpec((tk, M), lambda k: (k, 0)), pl.BlockSpec((tk, N), lambda k: (k, 0))],
        out_specs=pl.BlockSpec((M, N), lambda k: (0, 0)),
        out_shape=jax.ShapeDtypeStruct((M, N), BF16),
        scratch_shapes=[pltpu.VMEM((M, N), F32)],
        compiler_params=_params(1),
    )(a, b)


def _glu_bwd(y, z, r, dyo, w, og, tm):
    T = y.shape[0]

    def body(y_ref, z_ref, r_ref, d_ref, w_ref, og_ref, dy_ref, dw_ref, db_ref, dog_ref):
        @pl.when(pl.program_id(0) == 0)
        def _():
            dw_ref[...] = jnp.zeros_like(dw_ref)
            db_ref[...] = jnp.zeros_like(db_ref)
            dog_ref[...] = jnp.zeros_like(dog_ref)

        y1, g1 = _gelu_and_grad(y_ref[...])
        sg = _sigmoid(z_ref[...])
        y2 = y1 * sg
        dy2, dogt = _rms_bwd(d_ref[...], y2, r_ref[...], og_ref[...])
        dog_ref[...] += jnp.sum(dogt, axis=0, keepdims=True)
        dz = dy2 * y1 * sg * (1.0 - sg)
        db_ref[...] += jnp.sum(dz, axis=0, keepdims=True)
        dzb = dz.astype(BF16)
        dw_ref[...] += _dot_tn(y1.astype(BF16), dzb)
        dy_ref[...] = (dy2 * sg + _dot_nt(dzb, w_ref[...])) * g1

    row = pl.BlockSpec((tm, SSM_WIDTH), lambda i: (i, 0))
    vec = pl.BlockSpec((1, SSM_WIDTH), lambda i: (0, 0))
    sq = pl.BlockSpec((SSM_WIDTH, SSM_WIDTH), lambda i: (0, 0))
    return pl.pallas_call(
        body, name="glu_bwd", grid=(T // tm,),
        in_specs=[row, row, pl.BlockSpec((tm, 1), lambda i: (i, 0)), row, sq, vec],
        out_specs=[row, sq, vec, vec],
        out_shape=[jax.ShapeDtypeStruct((T, SSM_WIDTH), F32), jax.ShapeDtypeStruct((SSM_WIDTH, SSM_WIDTH), F32),
                   jax.ShapeDtypeStruct((1, SSM_WIDTH), F32), jax.ShapeDtypeStruct((1, SSM_WIDTH), F32)],
        compiler_params=_params(1),
    )(y, z, r, dyo, w, og)


def _s5_bwd(proj, dy, bound, pm, pm_t, bre, bim, bre_t, bim_t, cre, cim, lam, d, carry=None):
    T = proj.shape[0]
    nt = T // S5_TILE
    sp = _s5_specs(T, True)

    def body(u_ref, dy_ref, bound_ref, pm_ref, pmt_ref, bre_ref, bim_ref, bret_ref, bimt_ref, cre_ref, cim_ref,
             lam_ref, d_ref,
             du_ref, dbre_ref, dbim_ref, dcre_ref, dcim_ref, dlam_ref, dd_ref, carry, ptab, sr, si, gr, gi):
        lr = lam_ref[0:1, :]
        li = lam_ref[1:2, :]

        @pl.when(pl.program_id(1) == 0)
        def _():
            carry[...] = jnp.zeros_like(carry)
            _fill_power_table(ptab, lr, li)
            for ref in (dbre_ref, dbim_ref, dcre_ref, dcim_ref, dlam_ref, dd_ref):
                ref[...] = jnp.zeros_like(ref)

        def one_tile(u_in, dy_in, b_r, b_i):
            u = _permute_rows_f32(pm_ref[...], u_in)
            ub = u.astype(BF16)
            dyv = _permute_rows_f32(pm_ref[...], dy_in)
            dyb = dyv.astype(BF16)
            _tile_set(sr, _dot(ub, bre_ref[...]))
            _tile_set(si, _dot(ub, bim_ref[...]))
            er, ei, _, _ = _s5_forward_states(sr, si, lr, li, b_r, b_i, ptab)
            _tile_set(gr, _dot(dyb, cre_ref[...]))
            _tile_set(gi, -_dot(dyb, cim_ref[...]))
            zr, zi = _chunk_scans(gr, gi, lr, -li, True)
            ar, ai = _table_rows(ptab, S5_STEPS - 1, True)
            fr, fi = _entering_states(zr, zi, carry[0:1, :], carry[1:2, :], ar, ai, True)
            acc_r = jnp.zeros((S5_CHUNKS, S5_LANES), F32)
            acc_i = jnp.zeros((S5_CHUNKS, S5_LANES), F32)
            for j in range(S5_STEPS):
                qr, qi = _table_rows(ptab, S5_STEPS - 1 - j, True)
                g_r = _step_get(gr, j) + qr * fr - qi * fi
                g_i = _step_get(gi, j) + qr * fi + qi * fr
                _step_set(gr, j, g_r)
                _step_set(gi, j, g_i)
                p_r, p_i = (er, ei) if j == 0 else (_step_get(sr, j - 1), _step_get(si, j - 1))
                acc_r += g_r * p_r + g_i * p_i
                acc_i += g_i * p_r - g_r * p_i
            dlam_ref[0:1, :] += jnp.sum(acc_r, axis=0, keepdims=True)
            dlam_ref[1:2, :] += jnp.sum(acc_i, axis=0, keepdims=True)
            g_all_r = _tile_get(gr)
            g_all_i = _tile_get(gi)
            carry[0:1, :] = g_all_r[0:1, :]
            carry[1:2, :] = g_all_i[0:1, :]
            grb = g_all_r.astype(BF16)
            gib = g_all_i.astype(BF16)
            du = (_dot(grb, bret_ref[...]) + _dot(gib, bimt_ref[...]) + d_ref[...] * dyv).astype(BF16)
            dbre_ref[...] += _dot_tn(grb, ub)
            dbim_ref[...] += _dot_tn(gib, ub)
            dcre_ref[...] += _dot_tn(dyb, _tile_get(sr).astype(BF16))
            dcim_ref[...] -= _dot_tn(dyb, _tile_get(si).astype(BF16))
            dd_ref[...] += jnp.sum(dyv * u, axis=0, keepdims=True)
            return _dot(pmt_ref[...], du).astype(BF16)

        for s in reversed(range(S5_PER_STEP)):
            rows = slice(s * S5_TILE, (s + 1) * S5_TILE)
            du_ref[rows, :] = one_tile(u_ref[rows, :], dy_ref[rows, :], bound_ref[s, 0:1, :], bound_ref[s, 1:2, :])

    acc_ts = pl.BlockSpec((None, S5_LANES, LANE), lambda b, t: (b, 0, 0))
    acc_fs = pl.BlockSpec((None, LANE, S5_LANES), lambda b, t: (b, 0, 0))
    return _pcall(
        body, "s5_bwd", (S5_NBLK, nt // S5_PER_STEP),
        [sp["u"], sp["rows"], sp["bound"], sp["perm"], sp["perm"], sp["to_state"], sp["to_state"],
         sp["from_state"], sp["from_state"], sp["to_state"], sp["to_state"], sp["lam"], sp["d"]],
        [sp["rows"], acc_ts, acc_ts, acc_fs, acc_fs, sp["lam"], sp["d"]],
        [jax.ShapeDtypeStruct((T, SSM_WIDTH), BF16),
         jax.ShapeDtypeStruct((S5_NBLK, S5_LANES, LANE), F32),
         jax.ShapeDtypeStruct((S5_NBLK, S5_LANES, LANE), F32),
         jax.ShapeDtypeStruct((S5_NBLK, LANE, S5_LANES), F32),
         jax.ShapeDtypeStruct((S5_NBLK, LANE, S5_LANES), F32),
         jax.ShapeDtypeStruct((S5_NBLK, 2, S5_LANES), F32),
         jax.ShapeDtypeStruct((1, SSM_WIDTH), F32)],
        [pltpu.VMEM((2, S5_LANES), F32), pltpu.VMEM((2, S5_TILE, S5_LANES), F32)]
        + [pltpu.VMEM(S5_STATE_TILE, F32)] * 4,
        (proj, dy, bound, pm, pm_t, bre, bim, bre_t, bim_t, cre, cim, lam, d), carry)


def _ret_bwd(proj, cosf, sinf, mask, rowdec, kdec, gtb, gn, sblk, dyr):
    T = proj.shape[0]
    nb = T // RET_BLOCK
    sp = _ret_specs(T, True)

    def body(first, q_ref, k_ref, v_ref, g_ref, cos_ref, sin_ref, mask_ref, rd_ref, kd_ref, gtb_ref, gn_ref, sb_ref,
             dy_ref, dq_ref, dk_ref, dv_ref, dg_ref, dgn_ref, dst):
        if first:
            @pl.when(pl.program_id(1) == 0)
            def _():
                dst[...] = jnp.zeros_like(dst)
                dgn_ref[...] = jnp.zeros_like(dgn_ref)

        s_in = sb_ref[...]
        q, k, qb, kb, vb, pm, qd, o = _ret_common(q_ref, k_ref, v_ref, cos_ref, sin_ref, mask_ref, rd_ref, s_in)
        mu = jnp.mean(o, axis=-1, keepdims=True)
        oc = o - mu
        rstd = lax.rsqrt(jnp.mean(oc * oc, axis=-1, keepdims=True) + EPS)
        n = oc * rstd
        gt = g_ref[...]
        sg = _sigmoid(gt)
        sil = gt * sg
        gnv = gn_ref[...]
        dyv = dy_ref[...]
        dg_ref[...] = (dyv * (n * gnv) * (sg * (1.0 + gt * (1.0 - sg)))).astype(BF16)
        dgn_ref[...] += jnp.sum(dyv * sil * n, axis=0, keepdims=True)
        dn = dyv * sil * gnv
        do = rstd * (dn - jnp.mean(dn, axis=-1, keepdims=True) - n * jnp.mean(dn * n, axis=-1, keepdims=True))
        dob = do.astype(BF16)
        ds = dst[...]
        dsb = ds.astype(BF16)
        kd = kd_ref[...]
        rd = rd_ref[...]
        dv_ref[...] = (_dot_tn(pm, dob) + _dot((k * kd).astype(BF16), dsb)).astype(BF16)
        dpb = (_dot_nt(dob, vb) * mask_ref[...]).astype(BF16)
        dq = _dot(dpb, kb) + _dot_nt(dob, s_in.astype(BF16)) * rd
        dk = (_dot_tn(dpb, qb) + _dot_nt(vb, dsb) * kd) * (HEAD_DIM ** -0.5)
        dst[...] = gtb_ref[...] * ds + _dot_tn(qd, dob)
        c = cos_ref[...]
        s = sin_ref[...]
        dq_ref[...] = (dq * c + pltpu.roll(dq * s, HEAD_DIM // 2, 1)).astype(BF16)
        dk_ref[...] = (dk * c + pltpu.roll(dk * s, HEAD_DIM // 2, 1)).astype(BF16)

    oshape = jax.ShapeDtypeStruct((T, RET_WIDTH), BF16)
    ins = [sp[n] for n in ("q", "k", "v", "g", "tab", "tab", "mask", "dec", "dec", "gtb", "gn", "state", "rows")]
    outs = [sp["rows"], sp["rows"], sp["rows"], sp["rows"], sp["gn"]]
    return pl.pallas_call(
        _per_head(body, [kind for _, kind in ins + outs + [sp["scratch"]]], True), name="ret_bwd",
        grid=(RET_HEADS // RET_HPS, nb // RET_PER_STEP), in_specs=[s for s, _ in ins],
        out_specs=[s for s, _ in outs],
        out_shape=[oshape, oshape, oshape, oshape, jax.ShapeDtypeStruct((1, RET_WIDTH), F32)],
        scratch_shapes=[sp["scratch"][0]],
        compiler_params=_params(2),
    )(proj, proj, proj, proj, cosf, sinf, mask, rowdec, kdec, gtb, gn, sblk, dyr)


def _in_proj_bwd(dproj, w, x, r1, g, dx2, tm, carry=None):
    T = x.shape[0]

    def body(dp_ref, w_hbm, x_ref, r_ref, g_ref, dx2_ref, gx_ref, dg_ref, w_ref, sem):
        @pl.when(pl.program_id(0) == 0)
        def _():
            _load_resident(w_hbm, w_ref, sem)
            dg_ref[...] = jnp.zeros_like(dg_ref)

        dh = _dot_nt(dp_ref[:, 0:WIN_BLK], w_ref[0])
        for k in range(1, N_DEV):
            dh = dh + _dot_nt(dp_ref[:, k * WIN_BLK:(k + 1) * WIN_BLK], w_ref[k])
        dxn, dgt = _rms_bwd(dh, x_ref[...], r_ref[...], g_ref[...])
        gx_ref[...] = dx2_ref[...] + dxn
        dg_ref[...] += jnp.sum(dgt, axis=0, keepdims=True)

    full = pl.BlockSpec((tm, D_MODEL), lambda i: (i, 0))
    vec = pl.BlockSpec((1, D_MODEL), lambda i: (0, 0))
    return _pcall(
        body, "in_proj_bwd", (T // tm,),
        [pl.BlockSpec((tm, IN_WIDTH), lambda i: (i, 0)), ANY_SPEC,
         full, pl.BlockSpec((tm, 1), lambda i: (i, 0)), vec, full],
        [full, vec],
        [jax.ShapeDtypeStruct((T, D_MODEL), F32), jax.ShapeDtypeStruct((1, D_MODEL), F32)],
        [pltpu.VMEM(w.shape, w.dtype), pltpu.SemaphoreType.DMA], (dproj, w, x, r1, g, dx2), carry)


def _in_proj_wgrad(h, dproj, tk, carry=None):
    T = h.shape[0]
    nk = T // tk

    def body(h_ref, dp_ref, o_ref, acc):
        k = pl.program_id(1)

        @pl.when(k == 0)
        def _():
            acc[...] = jnp.zeros_like(acc)

        acc[...] += _dot_tn(h_ref[...], dp_ref[...])

        @pl.when(k == nk - 1)
        def _():
            o_ref[...] = acc[...].astype(BF16)

    return _pcall(
        body, "in_proj_wgrad", (N_DEV, nk),
        [pl.BlockSpec((tk, D_MODEL), lambda j, k: (k, 0)), pl.BlockSpec((tk, WIN_BLK), lambda j, k: (k, j))],
        [pl.BlockSpec((None, D_MODEL, WIN_BLK), lambda j, k: (j, 0, 0))],
        [jax.ShapeDtypeStruct((N_DEV, D_MODEL, WIN_BLK), BF16)],
        [pltpu.VMEM((D_MODEL, WIN_BLK), F32)], (h, dproj), carry)


def _rope_tables(T):
    half = HEAD_DIM // 2
    freqs = ROPE_BASE ** (-jnp.arange(half, dtype=F32) / half)
    ang = jnp.arange(T, dtype=F32)[:, None] * freqs[None, :]
    c = jnp.cos(ang)
    s = jnp.sin(ang)
    return jnp.concatenate([c, c], axis=1), jnp.concatenate([-s, s], axis=1)


def _retention_tables():
    hh = jnp.arange(RET_HEADS, dtype=F32)
    log_g = jnp.log1p(-(2.0 ** (-5.0 - hh)))[:, None, None]
    i = jnp.arange(RET_BLOCK)
    ci = (i // CHUNK)[:, None]
    cj = (i // CHUNK)[None, :]
    diff = (i[:, None] - i[None, :]).astype(F32)
    expo = jnp.where(ci == cj, jnp.abs(diff), diff)
    mask = jnp.where((cj <= ci)[None], jnp.exp(log_g * expo[None]), 0.0)
    r = jnp.arange(RET_BLOCK, dtype=F32)[None, :, None]
    ones = jnp.ones((1, 1, HEAD_DIM), F32)
    rowdec = jnp.exp(log_g * (r + 1.0)) * ones
    kdec = jnp.exp(log_g * (RET_BLOCK - 1.0 - r)) * ones
    gtb = jnp.exp(log_g * float(RET_BLOCK)) * ones
    return mask, rowdec, kdec, gtb


def _s5_discretise(a_re, a_im, log_dt, b_re, b_im):
    lam = lax.complex(a_re, a_im)
    dt = jnp.exp(log_dt)[:, None]
    lam_bar = jnp.exp(lam * dt)
    b_bar = ((lam_bar - 1.0) / lam)[..., None] * lax.complex(b_re, b_im)
    return jnp.real(lam_bar), jnp.imag(lam_bar), jnp.real(b_bar), jnp.imag(b_bar)


def _to_state_blockdiag(m):
    eye = jnp.eye(S5_GB, dtype=m.dtype)
    t = jnp.einsum("bgpc,gh->bgchp", m.reshape(S5_NBLK, S5_GB, SSM_STATE, SSM_GROUP), eye)
    return t.reshape(S5_NBLK, LANE, S5_LANES)


def _from_state_blockdiag(m):
    eye = jnp.eye(S5_GB, dtype=m.dtype)
    t = jnp.einsum("bgcp,gh->bgphc", m.reshape(S5_NBLK, S5_GB, SSM_GROUP, SSM_STATE), eye)
    return t.reshape(S5_NBLK, S5_LANES, LANE)


def _diag_of_state_major(acc):
    eye = jnp.eye(S5_GB, dtype=acc.dtype)
    t = acc.reshape(S5_NBLK, S5_GB, SSM_STATE, S5_GB, SSM_GROUP)
    return jnp.einsum("bgphc,gh->bgpc", t, eye).reshape(SSM_GROUPS, SSM_STATE, SSM_GROUP)


def _diag_of_channel_major(acc):
    eye = jnp.eye(S5_GB, dtype=acc.dtype)
    t = acc.reshape(S5_NBLK, S5_GB, SSM_GROUP, S5_GB, SSM_STATE)
    return jnp.einsum("bgchp,gh->bgcp", t, eye).reshape(SSM_GROUPS, SSM_GROUP, SSM_STATE)


SMALL_PARTIALS = (("ret_gn_g", 1024), ("lam_re", 4096), ("lam_im", 4096),
                  ("bbar_re", 65536), ("bbar_im", 65536), ("c_re", 65536), ("c_im", 65536),
                  ("ssm_d", 1024), ("b_glu", 1024), ("out_g", 1024), ("norm_ffn_g", 2048), ("norm_final_g", 2048))


def _forward_backward(x, tgt, shards, sm):
    T = x.shape[0]
    tm = min(1024, T)
    cosf, sinf = _rope_tables(T)
    mask, rowdec, kdec, gtb = _retention_tables()
    lbr, lbi, bbr, bbi = _s5_discretise(sm["ssm_a_re"], sm["ssm_a_im"], sm["ssm_log_dt"], sm["ssm_b_re"],
                                        sm["ssm_b_im"])
    bre = _to_state_blockdiag(bbr).astype(BF16)
    bim = _to_state_blockdiag(bbi).astype(BF16)
    cre_t = _from_state_blockdiag(sm["ssm_c_re"]).astype(BF16)
    cim_t = _from_state_blockdiag(sm["ssm_c_im"]).astype(BF16)
    bre_t = jnp.swapaxes(bre, 1, 2)
    bim_t = jnp.swapaxes(bim, 1, 2)
    cre = jnp.swapaxes(cre_t, 1, 2)
    cim = jnp.swapaxes(cim_t, 1, 2)
    lam = jnp.stack([lbr.reshape(S5_NBLK, S5_LANES), lbi.reshape(S5_NBLK, S5_LANES)], axis=1)
    pm = _step_major_permutation()
    pm_t = pm.T
    row = lambda v: v.reshape(1, -1)
    g_mix, g_ffn, g_fin = row(sm["norm_mix_g"]), row(sm["norm_ffn_g"]), row(sm["norm_final_g"])
    gn, dsk, bglu, og = row(sm["ret_gn_g"]), row(sm["ssm_d"]), row(sm["ssm_b_glu"]), row(sm["ssm_out_g"])

    (w_in,) = _exchange_call("weight_gather", [shards["w_in"]], True, via_sibling=True)
    proj, h1, r1, w_gate = _in_proj_fwd(x, g_mix, w_in, 256, _Exchange([shards["w_gate"]], True, via_sibling=True))
    y_ret, sblk, w_glu, w_out = _ret_fwd(proj, cosf, sinf, mask, rowdec, kdec, gtb, gn,
                                         _Exchange([shards["ssm_w_glu"], shards["w_out"]], True))
    w_glu = w_glu.reshape(SSM_WIDTH, SSM_WIDTH)
    w_out = w_out.reshape(D_MODEL, D_MODEL)
    y_s5, bound, w_up = _s5_fwd(proj, pm, pm_t, bre, bim, cre_t, cim_t, lam, dsk, _Exchange([shards["w_up"]], True))
    z, y_ssm, r_ssm = _glu_fwd(y_s5, w_glu, bglu, og, 256)
    x2, h2, r2 = _out_proj_fwd(x, y_ret, y_ssm, w_out, g_ffn, 256)
    a, b, f, w_down = _ffn_up(h2, w_gate, w_up, tm, _Exchange([shards["w_down"]], True))
    dx3, dx3b, loss8, dg_fin = _ffn_down_loss(f, w_down, x2, tgt, g_fin, 256)

    landed = {}
    da, db = _ffn_bwd_act(dx3b, w_down, a, b, tm)
    dw_down = _ffn_wgrad_down(f, dx3b, tm)
    dw_gate, dw_up, landed["w_down"] = _ffn_wgrad_up(h2, da, db, tm, _Exchange([dw_down], False))
    dh2, landed["w_gate"] = _ffn_bwd_in(da, db, w_gate, w_up, min(1024, T), _Exchange([dw_gate], False))
    dx2, dx2b, dg_ffn, dy_ret, dy_ssm = _out_proj_bwd(dh2, x2, r2, g_ffn, dx3, w_out, 256)
    dw_out = jnp.concatenate([_wgrad_rows("out_proj_wgrad_ret", y_ret, dx2b, tm),
                              _wgrad_rows("out_proj_wgrad_ssm", y_ssm, dx2b, tm)], axis=0)
    dy_s5, dw_glu, db_glu, dog = _glu_bwd(y_s5, z, r_ssm, dy_ssm, w_glu, og, 256)
    du, dbre, dbim, dcre, dcim, dlam, dd, landed["w_up"] = _s5_bwd(
        proj, dy_s5, bound, pm, pm_t, bre, bim, bre_t, bim_t, cre, cim, lam, dsk, _Exchange([dw_up], False))
    dq, dk, dv, dgate, dgn = _ret_bwd(proj, cosf, sinf, mask, rowdec, kdec, gtb, gn, sblk, dy_ret)
    dproj = jnp.concatenate([dq, dk, dv, dgate, du], axis=1)
    small = dict(ret_gn_g=dgn, lam_re=dlam[:, 0], lam_im=dlam[:, 1],
                 bbar_re=_diag_of_state_major(dbre), bbar_im=_diag_of_state_major(dbim),
                 c_re=_diag_of_channel_major(dcre), c_im=_diag_of_channel_major(dcim),
                 ssm_d=dd, b_glu=db_glu, out_g=dog, norm_ffn_g=dg_ffn, norm_final_g=dg_fin)
    packed = _pack([small[n] for n, _ in SMALL_PARTIALS])
    dw_in, landed["w_out"], landed["ssm_w_glu"], small_landed = _in_proj_wgrad(
        h1, dproj, tm, _Exchange([dw_out.reshape(N_DEV, D_MODEL // N_DEV, D_MODEL),
                                  dw_glu.astype(BF16).reshape(N_DEV, SSM_WIDTH // N_DEV, SSM_WIDTH), packed],
                                 [False, False, True]))
    grad_x, dg_mix, landed["w_in"] = _in_proj_bwd(dproj, w_in, x, r1, g_mix, dx2, 256, _Exchange([dw_in], False))
    (mix_landed,) = _exchange_call("mix_gain_grad_gather", [_pack([dg_mix])], True)
    summed = dict(zip([n for n, _ in SMALL_PARTIALS],
                      _unpack(_sum_partials("small_grad_sum", small_landed), [(sz,) for _, sz in SMALL_PARTIALS])))
    summed["norm_mix_g"] = _sum_partials("mix_gain_grad_sum", mix_landed).reshape(-1)
    return loss8[0, 0], grad_x, landed, summed


def _small_grads(summed, sm):
    _, vjp = jax.vjp(_s5_discretise, sm["ssm_a_re"], sm["ssm_a_im"], sm["ssm_log_dt"], sm["ssm_b_re"], sm["ssm_b_im"])
    gp = (SSM_GROUPS, SSM_STATE)
    da_re, da_im, dlog_dt, db_re, db_im = vjp((summed["lam_re"].reshape(gp), summed["lam_im"].reshape(gp),
                                               summed["bbar_re"].reshape(gp + (SSM_GROUP,)),
                                               summed["bbar_im"].reshape(gp + (SSM_GROUP,))))
    return dict(norm_mix_g=summed["norm_mix_g"], ret_gn_g=summed["ret_gn_g"], ssm_a_re=da_re, ssm_a_im=da_im,
                ssm_log_dt=dlog_dt, ssm_b_re=db_re, ssm_b_im=db_im,
                ssm_c_re=summed["c_re"].reshape(SSM_GROUPS, SSM_GROUP, SSM_STATE),
                ssm_c_im=summed["c_im"].reshape(SSM_GROUPS, SSM_GROUP, SSM_STATE),
                ssm_d=summed["ssm_d"], ssm_b_glu=summed["b_glu"], ssm_out_g=summed["out_g"],
                norm_ffn_g=summed["norm_ffn_g"], norm_final_g=summed["norm_final_g"])


def _adamw_math(w, g, m, v):
    m2 = ADAM_B1 * m + (1.0 - ADAM_B1) * g
    v2 = ADAM_B2 * v + (1.0 - ADAM_B2) * (g * g)
    delta = -ADAM_LR * ((m2 / ADAM_BC1) / (jnp.sqrt(v2 / ADAM_BC2) + ADAM_EPS) + ADAM_WD * w)
    return delta, m2, v2


def _adamw_shard(name, parts, w, m, v, tr):
    rows, cols = w.shape

    def body(p_ref, w_ref, m_ref, v_ref, g_ref, d_ref, m2_ref, v2_ref):
        g = p_ref[0].astype(F32)
        for s in range(1, N_DEV):
            g = g + p_ref[s].astype(F32)
        d, m2, v2 = _adamw_math(w_ref[...], g, m_ref[...], v_ref[...])
        g_ref[...] = g
        d_ref[...] = d
        m2_ref[...] = m2
        v2_ref[...] = v2

    blk = pl.BlockSpec((tr, cols), lambda i: (i, 0))
    oshape = jax.ShapeDtypeStruct((rows, cols), F32)
    return pl.pallas_call(
        body, name=name, grid=(rows // tr,),
        in_specs=[pl.BlockSpec((N_DEV, tr, cols), lambda i: (0, i, 0)), blk, blk, blk],
        out_specs=[blk, blk, blk, blk], out_shape=[oshape] * 4,
        compiler_params=_params(1),
    )(parts, w, m, v)


def _sum_partials(name, parts):
    rows = parts.shape[1]

    def body(p_ref, o_ref):
        g = p_ref[0]
        for s in range(1, N_DEV):
            g = g + p_ref[s]
        o_ref[...] = g

    return pl.pallas_call(
        body, name=name, grid=(1,),
        in_specs=[pl.BlockSpec((N_DEV, rows, LANE), lambda i: (0, 0, 0))],
        out_specs=pl.BlockSpec((rows, LANE), lambda i: (0, 0)),
        out_shape=jax.ShapeDtypeStruct((rows, LANE), F32),
        compiler_params=_params(1),
    )(parts)


def _adamw_small(ws, gs, ms, vs):
    n = len(ws)

    def body(*refs):
        for i in range(n):
            w_ref, g_ref, m_ref, v_ref = (refs[k * n + i] for k in range(4))
            d_ref, m2_ref, v2_ref = (refs[(4 + k) * n + i] for k in range(3))
            d, m2, v2 = _adamw_math(w_ref[...], g_ref[...], m_ref[...], v_ref[...])
            d_ref[...] = d
            m2_ref[...] = m2
            v2_ref[...] = v2

    vmem = pl.BlockSpec(memory_space=pltpu.VMEM)
    out = pl.pallas_call(
        body, name="adamw_small", in_specs=[vmem] * (4 * n), out_specs=[vmem] * (3 * n),
        out_shape=[jax.ShapeDtypeStruct(w.shape, F32) for w in ws] * 3,
        compiler_params=pltpu.CompilerParams(vmem_limit_bytes=VMEM_LIMIT),
    )(*ws, *gs, *ms, *vs)
    return out[:n], out[n:2 * n], out[2 * n:]


def _pack(arrays):
    parts = [a.reshape(-1, LANE) for a in arrays]
    assert all(p.shape[0] % 8 == 0 for p in parts)
    return parts[0] if len(parts) == 1 else jnp.concatenate(parts, axis=0)


def _unpack(packed, shapes):
    flat = packed.reshape(-1)
    out, off = [], 0
    for shp in shapes:
        n = math.prod(shp)
        out.append(flat[off:off + n].reshape(shp))
        off += n + ((-n) % LANE)
    return out


WEIGHTS = ("norm_mix_g", "w_in", "ret_gn_g", "ssm_a_re", "ssm_a_im", "ssm_log_dt", "ssm_b_re", "ssm_b_im",
           "ssm_c_re", "ssm_c_im", "ssm_d", "ssm_w_glu", "ssm_b_glu", "ssm_out_g", "w_out", "norm_ffn_g", "w_gate",
           "w_up", "w_down", "norm_final_g")
BIG = ("w_in", "ssm_w_glu", "w_out", "w_gate", "w_up", "w_down")
SMALL = tuple(n for n in WEIGHTS if n not in BIG)
ADAM_ROWS = {"w_in": 256, "ssm_w_glu": 128, "w_out": 128, "w_gate": 256, "w_up": 256, "w_down": 176}


def kernel(x, norm_mix_g, w_in, ret_gn_g, ssm_a_re, ssm_a_im, ssm_log_dt, ssm_b_re, ssm_b_im, ssm_c_re, ssm_c_im, ssm_d, ssm_w_glu, ssm_b_glu, ssm_out_g, w_out, norm_ffn_g, w_gate, w_up, w_down, norm_final_g, loss_target, m_norm_mix_g, m_w_in, m_ret_gn_g, m_ssm_a_re, m_ssm_a_im, m_ssm_log_dt, m_ssm_b_re, m_ssm_b_im, m_ssm_c_re, m_ssm_c_im, m_ssm_d, m_ssm_w_glu, m_ssm_b_glu, m_ssm_out_g, m_w_out, m_norm_ffn_g, m_w_gate, m_w_up, m_w_down, m_norm_final_g, v_norm_mix_g, v_w_in, v_ret_gn_g, v_ssm_a_re, v_ssm_a_im, v_ssm_log_dt, v_ssm_b_re, v_ssm_b_im, v_ssm_c_re, v_ssm_c_im, v_ssm_d, v_ssm_w_glu, v_ssm_b_glu, v_ssm_out_g, v_w_out, v_norm_ffn_g, v_w_gate, v_w_up, v_w_down, v_norm_final_g):
    given = dict(locals())
    w = {n: given[n] for n in WEIGHTS}
    m = {n: given["m_" + n] for n in WEIGHTS}
    v = {n: given["v_" + n] for n in WEIGHTS}
    drop = lambda n, a: a if n == "norm_final_g" else a[0]
    w0 = {n: drop(n, w[n]) for n in WEIGHTS}
    m0 = {n: drop(n, m[n]) for n in WEIGHTS}
    v0 = {n: drop(n, v[n]) for n in WEIGHTS}

    sm = {n: w0[n] for n in SMALL}
    shards = {n: w0[n].astype(BF16) for n in BIG}
    loss_local, grad_x, landed, summed = _forward_backward(x[0], loss_target[0], shards, sm)
    loss = lax.psum(loss_local, MESH_AXES)
    gsmall = _small_grads(summed, sm)

    grads, delta, new_m, new_v = {}, {}, {}, {}
    for n in BIG:
        g, d, m2, v2 = _adamw_shard("adamw_" + n, landed[n], w0[n], m0[n], v0[n], ADAM_ROWS[n])
        grads[n], delta[n], new_m[n], new_v[n] = g, d, m2, v2
    as_given = lambda n, a: a.reshape(1, -1) if n == "norm_final_g" else a.reshape(w[n].shape)
    gs = [as_given(n, gsmall[n]) for n in SMALL]
    ds, m2s, v2s = _adamw_small([as_given(n, w[n]) for n in SMALL], gs, [as_given(n, m[n]) for n in SMALL],
                                [as_given(n, v[n]) for n in SMALL])
    for n, g, d, m2, v2 in zip(SMALL, gs, ds, m2s, v2s):
        grads[n], delta[n], new_m[n], new_v[n] = g, d, m2, v2

    lift = lambda n, a: a.reshape(w[n].shape)
    return (loss, grad_x[None], *[lift(n, grads[n]) for n in WEIGHTS], *[lift(n, delta[n]) for n in WEIGHTS],
            *[lift(n, new_m[n]) for n in WEIGHTS], *[lift(n, new_v[n]) for n in WEIGHTS])
```

```python
import functools
import math

import jax
import jax.numpy as jnp
from jax import lax
from jax.experimental import pallas as pl
from jax.experimental.pallas import tpu as pltpu

F32 = jnp.float32
BF16 = jnp.bfloat16

D_MODEL = 2048
RET_WIDTH = 1024
RET_HEADS = 8
HEAD_DIM = 128
CHUNK = 64
SSM_WIDTH = 1024
SSM_GROUP = 16
SSM_GROUPS = 64
SSM_STATE = 64
D_FF = 5632
IN_WIDTH = 5120
ROPE_BASE = 10000.0
EPS = 1e-6
N_DEV = 8
MESH_AXES = ("x", "y", "c")

WIN_BLK = IN_WIDTH // N_DEV
FF_BLK = D_FF // N_DEV
RET_BLOCK = 256
RET_HPS = 4
RET_PER_STEP = 4
S5_TILE = 256
S5_CHUNKS = 8
S5_STEPS = S5_TILE // S5_CHUNKS
S5_PER_STEP = 8
S5_GB = 8
S5_NBLK = SSM_GROUPS // S5_GB
S5_LANES = S5_GB * SSM_STATE
LANE = 128

ADAM_LR = 0.001
ADAM_B1 = 0.9
ADAM_B2 = 0.999
ADAM_EPS = 1e-08
ADAM_WD = 0.01
ADAM_STEP = 10
ADAM_BC1 = 1.0 - ADAM_B1 ** ADAM_STEP
ADAM_BC2 = 1.0 - ADAM_B2 ** ADAM_STEP

VMEM_LIMIT = 56 * 1024 * 1024

NT = (((1,), (1,)), ((), ()))
TN = (((0,), (0,)), ((), ()))


def _params(n_grid):
    return pltpu.CompilerParams(dimension_semantics=("arbitrary",) * n_grid, vmem_limit_bytes=VMEM_LIMIT)


def _dot(a, b):
    return jnp.dot(a, b, preferred_element_type=F32)


def _dot_nt(a, b):
    return lax.dot_general(a, b, NT, preferred_element_type=F32)


def _dot_tn(a, b):
    return lax.dot_general(a, b, TN, preferred_element_type=F32)


def _sigmoid(x):
    return 1.0 / (1.0 + jnp.exp(-x))


_GELU_C = math.sqrt(2.0 / math.pi)
_GELU_A = 0.044715


def _gelu(x):
    t = jnp.tanh(_GELU_C * (x + _GELU_A * x * x * x))
    return 0.5 * x * (1.0 + t)


def _gelu_and_grad(x):
    t = jnp.tanh(_GELU_C * (x + _GELU_A * x * x * x))
    g = 0.5 * (1.0 + t) + 0.5 * x * (1.0 - t * t) * _GELU_C * (1.0 + 3.0 * _GELU_A * x * x)
    return 0.5 * x * (1.0 + t), g


def _rms_bwd(dy, x, r, g):
    w = dy * g
    dx = r * w - x * (r * r * r) * jnp.mean(w * x, axis=-1, keepdims=True)
    return dx, dy * x * r


HBM_SPEC = pl.BlockSpec(memory_space=pltpu.HBM)
ANY_SPEC = pl.BlockSpec(memory_space=pl.ANY)


def _load_resident(src_hbm, dst_vmem, sem):
    cp = pltpu.make_async_copy(src_hbm, dst_vmem, sem)
    cp.start()
    cp.wait()


def _my_block():
    return 4 * lax.axis_index("x") + 2 * lax.axis_index("y") + lax.axis_index("c")


def _peer(k):
    px = lax.axis_index("x") ^ ((k >> 2) & 1)
    py = lax.axis_index("y") ^ ((k >> 1) & 1)
    pc = lax.axis_index("c") ^ (k & 1)
    return (px, py, pc), 4 * px + 2 * py + pc


class _Exchange:
    def __init__(self, payloads, gather, via_sibling=False):
        self.payloads = list(payloads)
        self.n = len(self.payloads)
        self.gather = [gather] * self.n if isinstance(gather, bool) else list(gather)
        self.via_sibling = via_sibling
        assert not via_sibling or all(self.gather)

    def out_shape(self):
        return [jax.ShapeDtypeStruct(((N_DEV,) if g else ()) + p.shape, p.dtype)
                for p, g in zip(self.payloads, self.gather)]

    def scratch_shapes(self):
        return [pltpu.SemaphoreType.DMA((self.n, N_DEV - 1)), pltpu.SemaphoreType.DMA((self.n, N_DEV - 1)),
                pltpu.SemaphoreType.DMA((self.n,))]

    def _copies(self, ins, outs, sems, incoming):
        send_sems, recv_sems, local_sems = sems
        me = _my_block()
        src_of = lambda i, blk: ins[i] if self.gather[i] else ins[i].at[blk]
        local, remote = [], []
        for i in range(self.n):
            if not incoming:
                local.append(pltpu.make_async_copy(src_of(i, me), outs[i].at[me], local_sems.at[i]))
            for k in range(1, N_DEV):
                dev, blk = _peer(k)
                src, dst = (outs[i].at[blk], outs[i].at[blk]) if incoming else (src_of(i, blk), outs[i].at[me])
                remote.append(pltpu.make_async_remote_copy(
                    src_ref=src, dst_ref=dst, send_sem=send_sems.at[i, k - 1], recv_sem=recv_sems.at[i, k - 1],
                    device_id=dev, device_id_type=pl.DeviceIdType.MESH))
        return local, remote

    def _copy(self, i, k, outs, sems, src, dst_blk, to_k):
        send_sems, recv_sems, _ = sems
        return pltpu.make_async_remote_copy(
            src_ref=src, dst_ref=outs[i].at[dst_blk], send_sem=send_sems.at[i, k - 1], recv_sem=recv_sems.at[i, k - 1],
            device_id=_peer(to_k)[0], device_id_type=pl.DeviceIdType.MESH)

    FIRST_HOPS = (1, 2, 4, 6)
    FROM_CHIPS = (2, 4, 6)

    def start(self, ins, outs, sems):
        if not self.via_sibling:
            local, sends = self._copies(ins, outs, sems, False)
            for cp in local + sends:
                cp.start()
            return
        me = _my_block()
        for i in range(self.n):
            pltpu.make_async_copy(ins[i], outs[i].at[me], sems[2].at[i]).start()
            for k in self.FIRST_HOPS:
                self._copy(i, k, outs, sems, ins[i], me, k).start()

    def wait(self, ins, outs, sems):
        if not self.via_sibling:
            for cp in self._copies(ins, outs, sems, True)[1]:
                cp.wait_recv()
            local, sends = self._copies(ins, outs, sems, False)
            for cp in sends:
                cp.wait_send()
            for cp in local:
                cp.wait()
            return
        me = _my_block()
        landed = lambda i, k: self._copy(i, k, outs, sems, outs[i].at[_peer(k)[1]], _peer(k)[1], k)
        for i in range(self.n):
            for s in self.FROM_CHIPS:
                landed(i, s).wait_recv()
                self._copy(i, s ^ 1, outs, sems, outs[i].at[_peer(s)[1]], _peer(s)[1], 1).start()
        for i in range(self.n):
            for k in (1, 3, 5, 7):
                landed(i, k).wait_recv()
            for k in self.FIRST_HOPS:
                self._copy(i, k, outs, sems, ins[i], me, k).wait_send()
            for s in self.FROM_CHIPS:
                self._copy(i, s ^ 1, outs, sems, outs[i].at[_peer(s)[1]], _peer(s)[1], 1).wait_send()
            pltpu.make_async_copy(ins[i], outs[i].at[me], sems[2].at[i]).wait()


def _pcall(body, name, grid, in_specs, out_specs, out_shape, scratch_shapes, args, carry=None):
    n_in, n_out, n_scr = len(in_specs), len(out_specs), len(scratch_shapes)
    if carry is None:
        return pl.pallas_call(body, name=name, grid=grid, in_specs=in_specs, out_specs=out_specs, out_shape=out_shape,
                              scratch_shapes=scratch_shapes, compiler_params=_params(len(grid)))(*args)
    nx = carry.n

    def wrapped(*refs):
        cin, xin = refs[:n_in], refs[n_in:n_in + nx]
        cout, xout = refs[n_in + nx:n_in + nx + n_out], refs[n_in + nx + n_out:n_in + 2 * nx + n_out]
        rest = refs[n_in + 2 * nx + n_out:]
        cscr, sems = rest[:n_scr], rest[n_scr:]
        first = functools.reduce(jnp.logical_and, [pl.program_id(a) == 0 for a in range(len(grid))])
        last = functools.reduce(jnp.logical_and, [pl.program_id(a) == grid[a] - 1 for a in range(len(grid))])

        @pl.when(first)
        def _():
            carry.start(xin, xout, sems)

        body(*cin, *cout, *cscr)

        @pl.when(last)
        def _():
            carry.wait(xin, xout, sems)

    return pl.pallas_call(
        wrapped, name=name, grid=grid, in_specs=list(in_specs) + [HBM_SPEC] * nx,
        out_specs=list(out_specs) + [HBM_SPEC] * nx, out_shape=list(out_shape) + carry.out_shape(),
        scratch_shapes=list(scratch_shapes) + carry.scratch_shapes(), compiler_params=_params(len(grid)),
    )(*args, *carry.payloads)


def _exchange_call(name, payloads, gather, via_sibling=False):
    ex = _Exchange(payloads, gather, via_sibling)

    def body(*refs):
        ins, outs, sems = refs[:ex.n], refs[ex.n:2 * ex.n], refs[2 * ex.n:]
        ex.start(ins, outs, sems)
        ex.wait(ins, outs, sems)

    return pl.pallas_call(body, name=name, in_specs=[HBM_SPEC] * ex.n, out_specs=[HBM_SPEC] * ex.n,
                          out_shape=ex.out_shape(), scratch_shapes=ex.scratch_shapes())(*ex.payloads)


def _in_proj_fwd(x, g, w, tm, carry=None):
    T = x.shape[0]

    def body(x_ref, g_ref, w_hbm, proj_ref, h_ref, r_ref, w_ref, sem):
        @pl.when(pl.program_id(0) == 0)
        def _():
            _load_resident(w_hbm, w_ref, sem)

        xf = x_ref[...]
        r = lax.rsqrt(jnp.mean(xf * xf, axis=-1, keepdims=True) + EPS)
        h = (xf * r * g_ref[...]).astype(BF16)
        h_ref[...] = h
        r_ref[...] = r
        for j in range(N_DEV):
            proj_ref[:, j * WIN_BLK:(j + 1) * WIN_BLK] = _dot(h, w_ref[j])

    return _pcall(
        body, "in_proj_fwd", (T // tm,),
        [pl.BlockSpec((tm, D_MODEL), lambda i: (i, 0)), pl.BlockSpec((1, D_MODEL), lambda i: (0, 0)), ANY_SPEC],
        [pl.BlockSpec((tm, IN_WIDTH), lambda i: (i, 0)),
         pl.BlockSpec((tm, D_MODEL), lambda i: (i, 0)),
         pl.BlockSpec((tm, 1), lambda i: (i, 0))],
        [jax.ShapeDtypeStruct((T, IN_WIDTH), F32),
         jax.ShapeDtypeStruct((T, D_MODEL), BF16),
         jax.ShapeDtypeStruct((T, 1), F32)],
        [pltpu.VMEM(w.shape, w.dtype), pltpu.SemaphoreType.DMA], (x, g, w), carry)


def _ret_common(q_ref, k_ref, v_ref, cos_ref, sin_ref, mask_ref, rd_ref, sin_state):
    c = cos_ref[...]
    s = sin_ref[...]
    q = q_ref[...]
    q = q * c + pltpu.roll(q, HEAD_DIM // 2, 1) * s
    k = k_ref[...]
    k = (k * c + pltpu.roll(k, HEAD_DIM // 2, 1) * s) * (HEAD_DIM ** -0.5)
    qb = q.astype(BF16)
    kb = k.astype(BF16)
    vb = v_ref[...].astype(BF16)
    pm = (_dot_nt(qb, kb) * mask_ref[...]).astype(BF16)
    qd = (q * rd_ref[...]).astype(BF16)
    o = _dot(pm, vb) + _dot(qd, sin_state.astype(BF16))
    return q, k, qb, kb, vb, pm, qd, o


def _ret_specs(T, rev):
    rows = RET_BLOCK * RET_PER_STEP
    nb = T // rows
    groups = RET_HEADS // RET_HPS
    wide = RET_HPS * HEAD_DIM
    blk = (lambda b: nb - 1 - b) if rev else (lambda b: b)
    col = lambda piece: (pl.BlockSpec((rows, wide), lambda h, b: (blk(b), piece * groups + h)), "rows_lane")
    return dict(
        q=col(0), k=col(1), v=col(2), g=col(3),
        tab=(pl.BlockSpec((rows, HEAD_DIM), lambda h, b: (blk(b), 0)), "rows"),
        mask=(pl.BlockSpec((RET_HPS, RET_BLOCK, RET_BLOCK), lambda h, b: (h, 0, 0)), "lead"),
        dec=(pl.BlockSpec((RET_HPS, RET_BLOCK, HEAD_DIM), lambda h, b: (h, 0, 0)), "lead"),
        gtb=(pl.BlockSpec((RET_HPS, 1, HEAD_DIM), lambda h, b: (h, 0, 0)), "lead"),
        gn=(pl.BlockSpec((1, wide), lambda h, b: (0, h)), "lane"),
        state=(pl.BlockSpec((RET_HPS, RET_PER_STEP, HEAD_DIM, HEAD_DIM), lambda h, b: (h, blk(b), 0, 0)), "state"),
        rows=(pl.BlockSpec((rows, wide), lambda h, b: (blk(b), h)), "rows_lane"),
        scratch=(pltpu.VMEM((RET_HPS, HEAD_DIM, HEAD_DIM), F32), "lead"),
    )


def _per_head(head_body, kinds, rev):
    def body(*refs):
        order = list(reversed(range(RET_PER_STEP))) if rev else list(range(RET_PER_STEP))
        for hh in range(RET_HPS):
            lanes = slice(hh * HEAD_DIM, (hh + 1) * HEAD_DIM)
            for s in order:
                rows = slice(s * RET_BLOCK, (s + 1) * RET_BLOCK)
                cut = {"rows_lane": lambda r: r.at[rows, lanes], "rows": lambda r: r.at[rows, :],
                       "lane": lambda r: r.at[:, lanes], "lead": lambda r: r.at[hh], "state": lambda r: r.at[hh, s]}
                head_body(s == order[0], *[cut[kind](ref) for ref, kind in zip(refs, kinds)])
    return body


def _ret_fwd(proj, cosf, sinf, mask, rowdec, kdec, gtb, gn, carry=None):
    T = proj.shape[0]
    nb = T // RET_BLOCK
    sp = _ret_specs(T, False)

    def body(first, q_ref, k_ref, v_ref, g_ref, cos_ref, sin_ref, mask_ref, rd_ref, kd_ref, gtb_ref, gn_ref,
             y_ref, sb_ref, st):
        if first:
            @pl.when(pl.program_id(1) == 0)
            def _():
                st[...] = jnp.zeros_like(st)
        s_in = st[...]
        sb_ref[...] = s_in
        q, k, qb, kb, vb, pm, qd, o = _ret_common(q_ref, k_ref, v_ref, cos_ref, sin_ref, mask_ref, rd_ref, s_in)
        st[...] = gtb_ref[...] * s_in + _dot_tn((k * kd_ref[...]).astype(BF16), vb)
        mu = jnp.mean(o, axis=-1, keepdims=True)
        oc = o - mu
        n = oc * lax.rsqrt(jnp.mean(oc * oc, axis=-1, keepdims=True) + EPS)
        gt = g_ref[...]
        y_ref[...] = (gt * _sigmoid(gt) * (n * gn_ref[...])).astype(BF16)

    ins = [sp[n] for n in ("q", "k", "v", "g", "tab", "tab", "mask", "dec", "dec", "gtb", "gn")]
    outs = [sp["rows"], sp["state"]]
    return _pcall(
        _per_head(body, [kind for _, kind in ins + outs + [sp["scratch"]]], False), "ret_fwd",
        (RET_HEADS // RET_HPS, nb // RET_PER_STEP), [s for s, _ in ins], [s for s, _ in outs],
        [jax.ShapeDtypeStruct((T, RET_WIDTH), BF16),
         jax.ShapeDtypeStruct((RET_HEADS, nb, HEAD_DIM, HEAD_DIM), F32)],
        [sp["scratch"][0]],
        (proj, proj, proj, proj, cosf, sinf, mask, rowdec, kdec, gtb, gn), carry)


def _scan(re, im, ar, ai, reverse):
    n = re.shape[0]
    row = lax.broadcasted_iota(jnp.int32, re.shape, 0)
    s = 1
    while s < n:
        if reverse:
            keep = row < n - s
            sr = jnp.where(keep, pltpu.roll(re, n - s, 0), 0.0)
            si = jnp.where(keep, pltpu.roll(im, n - s, 0), 0.0)
        else:
            keep = row >= s
            sr = jnp.where(keep, pltpu.roll(re, s, 0), 0.0)
            si = jnp.where(keep, pltpu.roll(im, s, 0), 0.0)
        re, im = re + ar * sr - ai * si, im + ar * si + ai * sr
        ar, ai = ar * ar - ai * ai, 2.0 * ar * ai
        s *= 2
    return re, im


S5_STATE_TILE = (S5_TILE, S5_LANES)


def _step_major_permutation():
    r = jnp.arange(S5_TILE)
    t_of_row = (r % S5_CHUNKS) * S5_STEPS + r // S5_CHUNKS
    return (t_of_row[:, None] == r[None, :]).astype(BF16)


def _permute_rows_f32(pm, x):
    hi = x.astype(BF16)
    rest = x - hi.astype(F32)
    mid = rest.astype(BF16)
    lo = (rest - mid.astype(F32)).astype(BF16)
    return _dot(pm, hi) + _dot(pm, mid) + _dot(pm, lo)


def _step_get(ref, j):
    return ref[j * S5_CHUNKS:(j + 1) * S5_CHUNKS, :]


def _step_set(ref, j, val):
    ref[j * S5_CHUNKS:(j + 1) * S5_CHUNKS, :] = val


def _tile_get(ref):
    return ref[...]


def _tile_set(ref, val):
    ref[...] = val


def _fill_power_table(ptab, lr, li):
    shape = (S5_CHUNKS, S5_LANES)
    lrb = jnp.broadcast_to(lr, shape)
    lib = jnp.broadcast_to(li, shape)
    pr, pi_ = lrb, lib
    for j in range(S5_STEPS):
        ptab[0, j * S5_CHUNKS:(j + 1) * S5_CHUNKS, :] = pr
        ptab[1, j * S5_CHUNKS:(j + 1) * S5_CHUNKS, :] = pi_
        pr, pi_ = lrb * pr - lib * pi_, lrb * pi_ + lib * pr


def _chunk_scans(xr, xi, lr, li, reverse):
    shape = (S5_CHUNKS, S5_LANES)
    lrb = jnp.broadcast_to(lr, shape)
    lib = jnp.broadcast_to(li, shape)
    sr = si = None
    for j in (range(S5_STEPS - 1, -1, -1) if reverse else range(S5_STEPS)):
        vr = _step_get(xr, j)
        vi = _step_get(xi, j)
        if sr is not None:
            vr, vi = vr + lrb * sr - lib * si, vi + lrb * si + lib * sr
            _step_set(xr, j, vr)
            _step_set(xi, j, vi)
        sr, si = vr, vi
    return sr, si


def _entering_states(zr, zi, cr, ci, ar, ai, reverse):
    shape = (S5_CHUNKS, S5_LANES)
    row = lax.broadcasted_iota(jnp.int32, shape, 0)
    if reverse:
        edge, shift = row == S5_CHUNKS - 1, S5_CHUNKS - 1
    else:
        edge, shift = row == 0, 1
    wr = jnp.where(edge, jnp.broadcast_to(cr, shape), pltpu.roll(zr, shift, 0))
    wi = jnp.where(edge, jnp.broadcast_to(ci, shape), pltpu.roll(zi, shift, 0))
    return _scan(wr, wi, ar, ai, reverse)


def _table_rows(ptab, j, conj):
    pr = ptab[0, j * S5_CHUNKS:(j + 1) * S5_CHUNKS, :]
    pi_ = ptab[1, j * S5_CHUNKS:(j + 1) * S5_CHUNKS, :]
    return pr, (-pi_ if conj else pi_)


def _s5_forward_states(xr, xi, lr, li, cr, ci, ptab):
    zr, zi = _chunk_scans(xr, xi, lr, li, False)
    ar, ai = _table_rows(ptab, S5_STEPS - 1, False)
    er, ei = _entering_states(zr, zi, cr, ci, ar, ai, False)
    for j in range(S5_STEPS):
        pr, pi_ = _table_rows(ptab, j, False)
        _step_set(xr, j, _step_get(xr, j) + pr * er - pi_ * ei)
        _step_set(xi, j, _step_get(xi, j) + pr * ei + pi_ * er)
    last = S5_CHUNKS - 1
    end_r = (ar * er - ai * ei + zr)[last:last + 1, :]
    end_i = (ar * ei + ai * er + zi)[last:last + 1, :]
    return er, ei, end_r, end_i


def _s5_specs(T, rev):
    rows = S5_TILE * S5_PER_STEP
    nt = T // rows
    tt = (lambda t: nt - 1 - t) if rev else (lambda t: t)
    return dict(
        u=pl.BlockSpec((rows, LANE), lambda b, t: (tt(t), 4 * RET_HEADS + b)),
        rows=pl.BlockSpec((rows, LANE), lambda b, t: (tt(t), b)),
        to_state=pl.BlockSpec((None, LANE, S5_LANES), lambda b, t: (b, 0, 0)),
        from_state=pl.BlockSpec((None, S5_LANES, LANE), lambda b, t: (b, 0, 0)),
        lam=pl.BlockSpec((None, 2, S5_LANES), lambda b, t: (b, 0, 0)),
        d=pl.BlockSpec((1, LANE), lambda b, t: (0, b)),
        perm=pl.BlockSpec((S5_TILE, S5_TILE), lambda b, t: (0, 0)),
        bound=pl.BlockSpec((None, S5_PER_STEP, 2, S5_LANES), lambda b, t: (b, tt(t), 0, 0)),
    )


def _s5_fwd(proj, pm, pm_t, bre, bim, cre_t, cim_t, lam, d, carry=None):
    T = proj.shape[0]
    nt = T // S5_TILE
    sp = _s5_specs(T, False)

    def body(u_ref, pm_ref, pmt_ref, bre_ref, bim_ref, cre_ref, cim_ref, lam_ref, d_ref, y_ref, bound_ref,
             carry, ptab, xr, xi):
        lr = lam_ref[0:1, :]
        li = lam_ref[1:2, :]

        @pl.when(pl.program_id(1) == 0)
        def _():
            carry[...] = jnp.zeros_like(carry)
            _fill_power_table(ptab, lr, li)

        for s in range(S5_PER_STEP):
            rows = slice(s * S5_TILE, (s + 1) * S5_TILE)
            u = _permute_rows_f32(pm_ref[...], u_ref[rows, :])
            ub = u.astype(BF16)
            _tile_set(xr, _dot(ub, bre_ref[...]))
            _tile_set(xi, _dot(ub, bim_ref[...]))
            bound_ref[s] = carry[...]
            _, _, end_r, end_i = _s5_forward_states(xr, xi, lr, li, carry[0:1, :], carry[1:2, :], ptab)
            carry[0:1, :] = end_r
            carry[1:2, :] = end_i
            y = (_dot(_tile_get(xr).astype(BF16), cre_ref[...]) - _dot(_tile_get(xi).astype(BF16), cim_ref[...])
                 + d_ref[...] * u)
            y_ref[rows, :] = _permute_rows_f32(pmt_ref[...], y)

    state = pltpu.VMEM(S5_STATE_TILE, F32)
    return _pcall(
        body, "s5_fwd", (S5_NBLK, nt // S5_PER_STEP),
        [sp["u"], sp["perm"], sp["perm"], sp["to_state"], sp["to_state"], sp["from_state"],
         sp["from_state"], sp["lam"], sp["d"]],
        [sp["rows"], sp["bound"]],
        [jax.ShapeDtypeStruct((T, SSM_WIDTH), F32),
         jax.ShapeDtypeStruct((S5_NBLK, nt, 2, S5_LANES), F32)],
        [pltpu.VMEM((2, S5_LANES), F32), pltpu.VMEM((2, S5_TILE, S5_LANES), F32), state, state],
        (proj, pm, pm_t, bre, bim, cre_t, cim_t, lam, d), carry)


def _glu_fwd(y, w, b, og, tm):
    T = y.shape[0]

    def body(y_ref, w_ref, b_ref, og_ref, z_ref, o_ref, r_ref):
        y1 = _gelu(y_ref[...])
        z = _dot(y1.astype(BF16), w_ref[...]) + b_ref[...]
        y2 = y1 * _sigmoid(z)
        r = lax.rsqrt(jnp.mean(y2 * y2, axis=-1, keepdims=True) + EPS)
        z_ref[...] = z
        o_ref[...] = (y2 * r * og_ref[...]).astype(BF16)
        r_ref[...] = r

    row = pl.BlockSpec((tm, SSM_WIDTH), lambda i: (i, 0))
    vec = pl.BlockSpec((1, SSM_WIDTH), lambda i: (0, 0))
    return pl.pallas_call(
        body, name="glu_fwd", grid=(T // tm,),
        in_specs=[row, pl.BlockSpec((SSM_WIDTH, SSM_WIDTH), lambda i: (0, 0)), vec, vec],
        out_specs=[row, row, pl.BlockSpec((tm, 1), lambda i: (i, 0))],
        out_shape=[jax.ShapeDtypeStruct((T, SSM_WIDTH), F32), jax.ShapeDtypeStruct((T, SSM_WIDTH), BF16),
                   jax.ShapeDtypeStruct((T, 1), F32)],
        compiler_params=_params(1),
    )(y, w, b, og)


def _out_proj_fwd(x, y_ret, y_ssm, w, g, tm):
    T = x.shape[0]

    def body(x_ref, a_ref, b_ref, w_ref, g_ref, x2_ref, h_ref, r_ref):
        x2 = x_ref[...] + _dot(a_ref[...], w_ref[0:RET_WIDTH, :]) + _dot(b_ref[...], w_ref[RET_WIDTH:D_MODEL, :])
        r = lax.rsqrt(jnp.mean(x2 * x2, axis=-1, keepdims=True) + EPS)
        x2_ref[...] = x2
        h_ref[...] = (x2 * r * g_ref[...]).astype(BF16)
        r_ref[...] = r

    full = pl.BlockSpec((tm, D_MODEL), lambda i: (i, 0))
    half = pl.BlockSpec((tm, RET_WIDTH), lambda i: (i, 0))
    return pl.pallas_call(
        body, name="out_proj_fwd", grid=(T // tm,),
        in_specs=[full, half, half, pl.BlockSpec((D_MODEL, D_MODEL), lambda i: (0, 0)),
                  pl.BlockSpec((1, D_MODEL), lambda i: (0, 0))],
        out_specs=[full, full, pl.BlockSpec((tm, 1), lambda i: (i, 0))],
        out_shape=[jax.ShapeDtypeStruct((T, D_MODEL), F32), jax.ShapeDtypeStruct((T, D_MODEL), BF16),
                   jax.ShapeDtypeStruct((T, 1), F32)],
        compiler_params=_params(1),
    )(x, y_ret, y_ssm, w, g)


def _ffn_up(h, wg, wu, tm, carry=None):
    T = h.shape[0]

    def body(h_ref, wg_ref, wu_ref, a_ref, b_ref, f_ref):
        hb = h_ref[...]
        a = _dot(hb, wg_ref[...])
        b = _dot(hb, wu_ref[...])
        a_ref[...] = a.astype(BF16)
        b_ref[...] = b.astype(BF16)
        f_ref[...] = (a * _sigmoid(a) * b).astype(BF16)

    wspec = pl.BlockSpec((None, D_MODEL, FF_BLK), lambda j, i: (j, 0, 0))
    ospec = pl.BlockSpec((None, tm, FF_BLK), lambda j, i: (j, i, 0))
    oshape = jax.ShapeDtypeStruct((N_DEV, T, FF_BLK), BF16)
    return _pcall(
        body, "ffn_up", (N_DEV, T // tm),
        [pl.BlockSpec((tm, D_MODEL), lambda j, i: (i, 0)), wspec, wspec],
        [ospec, ospec, ospec], [oshape, oshape, oshape], [], (h, wg, wu), carry)


def _ffn_down_loss(f, wd, x2, tgt, g, tm):
    T = x2.shape[0]

    def body(f_ref, w_hbm, x2_ref, t_ref, g_ref, dx_ref, dxb_ref, loss_ref, dg_ref, w_ref, sem):
        i = pl.program_id(0)

        @pl.when(i == 0)
        def _():
            _load_resident(w_hbm, w_ref, sem)
            loss_ref[...] = jnp.zeros_like(loss_ref)
            dg_ref[...] = jnp.zeros_like(dg_ref)

        gv = g_ref[...]
        x3 = x2_ref[...]
        for k in range(N_DEV):
            x3 = x3 + _dot(f_ref[k], w_ref[k])
        r = lax.rsqrt(jnp.mean(x3 * x3, axis=-1, keepdims=True) + EPS)
        err = x3 * r * gv - t_ref[...]
        tile_loss = 0.5 * jnp.sum(jnp.mean(err * err, axis=-1, keepdims=True), axis=0, keepdims=True)
        dx, dgt = _rms_bwd(err * (1.0 / D_MODEL), x3, r, gv)
        dx_ref[...] = dx
        dxb_ref[...] = dx.astype(BF16)
        loss_ref[...] += jnp.broadcast_to(tile_loss, loss_ref.shape)
        dg_ref[...] += jnp.sum(dgt, axis=0, keepdims=True)

    full = pl.BlockSpec((tm, D_MODEL), lambda i: (i, 0))
    vec = pl.BlockSpec((1, D_MODEL), lambda i: (0, 0))
    return pl.pallas_call(
        body, name="ffn_down_loss", grid=(T // tm,),
        in_specs=[pl.BlockSpec((N_DEV, tm, FF_BLK), lambda i: (0, i, 0)), ANY_SPEC, full, full, vec],
        out_specs=[full, full, pl.BlockSpec((8, LANE), lambda i: (0, 0)), vec],
        out_shape=[jax.ShapeDtypeStruct((T, D_MODEL), F32), jax.ShapeDtypeStruct((T, D_MODEL), BF16),
                   jax.ShapeDtypeStruct((8, LANE), F32), jax.ShapeDtypeStruct((1, D_MODEL), F32)],
        scratch_shapes=[pltpu.VMEM(wd.shape, wd.dtype), pltpu.SemaphoreType.DMA],
        compiler_params=_params(1),
    )(f, wd, x2, tgt, g)


def _ffn_bwd_act(dxb, wd, a, b, tm):
    T = dxb.shape[0]

    def body(dx_ref, w_ref, a_ref, b_ref, da_ref, db_ref):
        df = _dot_nt(dx_ref[...], w_ref[...])
        a = a_ref[...].astype(F32)
        b = b_ref[...].astype(F32)
        sg = _sigmoid(a)
        da_ref[...] = (df * b * sg * (1.0 + a * (1.0 - sg))).astype(BF16)
        db_ref[...] = (df * a * sg).astype(BF16)

    blk = pl.BlockSpec((None, tm, FF_BLK), lambda j, i: (j, i, 0))
    oshape = jax.ShapeDtypeStruct((N_DEV, T, FF_BLK), BF16)
    return pl.pallas_call(
        body, name="ffn_bwd_act", grid=(N_DEV, T // tm),
        in_specs=[pl.BlockSpec((tm, D_MODEL), lambda j, i: (i, 0)),
                  pl.BlockSpec((None, FF_BLK, D_MODEL), lambda j, i: (j, 0, 0)), blk, blk],
        out_specs=[blk, blk], out_shape=[oshape, oshape],
        compiler_params=_params(2),
    )(dxb, wd, a, b)


def _ffn_bwd_in(da, db, wg, wu, tm, carry=None):
    T = da.shape[1]

    def body(da_ref, db_ref, wg_ref, wu_ref, dh_ref):
        part = _dot_nt(da_ref[...], wg_ref[...]) + _dot_nt(db_ref[...], wu_ref[...])

        @pl.when(pl.program_id(1) == 0)
        def _():
            dh_ref[...] = part

        @pl.when(pl.program_id(1) > 0)
        def _():
            dh_ref[...] += part

    ablk = pl.BlockSpec((None, tm, FF_BLK), lambda i, k: (k, i, 0))
    wblk = pl.BlockSpec((None, D_MODEL, FF_BLK), lambda i, k: (k, 0, 0))
    return _pcall(
        body, "ffn_bwd_in", (T // tm, N_DEV), [ablk, ablk, wblk, wblk],
        [pl.BlockSpec((tm, D_MODEL), lambda i, k: (i, 0))], [jax.ShapeDtypeStruct((T, D_MODEL), F32)],
        [], (da, db, wg, wu), carry)


def _ffn_wgrad_up(h, da, db, tk, carry=None):
    T = h.shape[0]
    nk = T // tk

    def body(h_ref, da_ref, db_ref, g_ref, u_ref, accg, accu):
        k = pl.program_id(1)

        @pl.when(k == 0)
        def _():
            accg[...] = jnp.zeros_like(accg)
            accu[...] = jnp.zeros_like(accu)

        hb = h_ref[...]
        accg[...] += _dot_tn(hb, da_ref[...])
        accu[...] += _dot_tn(hb, db_ref[...])

        @pl.when(k == nk - 1)
        def _():
            g_ref[...] = accg[...].astype(BF16)
            u_ref[...] = accu[...].astype(BF16)

    blk = pl.BlockSpec((None, tk, FF_BLK), lambda j, k: (j, k, 0))
    ospec = pl.BlockSpec((None, D_MODEL, FF_BLK), lambda j, k: (j, 0, 0))
    oshape = jax.ShapeDtypeStruct((N_DEV, D_MODEL, FF_BLK), BF16)
    return _pcall(
        body, "ffn_wgrad_up", (N_DEV, nk),
        [pl.BlockSpec((tk, D_MODEL), lambda j, k: (k, 0)), blk, blk],
        [ospec, ospec], [oshape, oshape],
        [pltpu.VMEM((D_MODEL, FF_BLK), F32), pltpu.VMEM((D_MODEL, FF_BLK), F32)], (h, da, db), carry)


def _ffn_wgrad_down(f, dxb, tk):
    T = dxb.shape[0]
    nk = T // tk

    def body(f_ref, dx_ref, o_ref, acc):
        k = pl.program_id(1)

        @pl.when(k == 0)
        def _():
            acc[...] = jnp.zeros_like(acc)

        acc[...] += _dot_tn(f_ref[...], dx_ref[...])

        @pl.when(k == nk - 1)
        def _():
            o_ref[...] = acc[...].astype(BF16)

    return pl.pallas_call(
        body, name="ffn_wgrad_down", grid=(N_DEV, nk),
        in_specs=[pl.BlockSpec((None, tk, FF_BLK), lambda j, k: (j, k, 0)),
                  pl.BlockSpec((tk, D_MODEL), lambda j, k: (k, 0))],
        out_specs=pl.BlockSpec((None, FF_BLK, D_MODEL), lambda j, k: (j, 0, 0)),
        out_shape=jax.ShapeDtypeStruct((N_DEV, FF_BLK, D_MODEL), BF16),
        scratch_shapes=[pltpu.VMEM((FF_BLK, D_MODEL), F32)],
        compiler_params=_params(2),
    )(f, dxb)


def _out_proj_bwd(dh2, x2, r2, g, dx3, w, tm):
    T = x2.shape[0]

    def body(dh_ref, x_ref, r_ref, g_ref, dx3_ref, w_ref, dx_ref, dxb_ref, dg_ref, a_ref, b_ref):
        @pl.when(pl.program_id(0) == 0)
        def _():
            dg_ref[...] = jnp.zeros_like(dg_ref)

        dxn, dgt = _rms_bwd(dh_ref[...], x_ref[...], r_ref[...], g_ref[...])
        dx = dx3_ref[...] + dxn
        dxv = dx.astype(BF16)
        dx_ref[...] = dx
        dxb_ref[...] = dxv
        dg_ref[...] += jnp.sum(dgt, axis=0, keepdims=True)
        a_ref[...] = _dot_nt(dxv, w_ref[0:RET_WIDTH, :])
        b_ref[...] = _dot_nt(dxv, w_ref[RET_WIDTH:D_MODEL, :])

    full = pl.BlockSpec((tm, D_MODEL), lambda i: (i, 0))
    vec = pl.BlockSpec((1, D_MODEL), lambda i: (0, 0))
    half = pl.BlockSpec((tm, RET_WIDTH), lambda i: (i, 0))
    hshape = jax.ShapeDtypeStruct((T, RET_WIDTH), F32)
    return pl.pallas_call(
        body, name="out_proj_bwd", grid=(T // tm,),
        in_specs=[full, full, pl.BlockSpec((tm, 1), lambda i: (i, 0)), vec, full,
                  pl.BlockSpec((D_MODEL, D_MODEL), lambda i: (0, 0))],
        out_specs=[full, full, vec, half, half],
        out_shape=[jax.ShapeDtypeStruct((T, D_MODEL), F32), jax.ShapeDtypeStruct((T, D_MODEL), BF16),
                   jax.ShapeDtypeStruct((1, D_MODEL), F32), hshape, hshape],
        compiler_params=_params(1),
    )(dh2, x2, r2, g, dx3, w)


def _wgrad_rows(name, a, b, tk):
    T, M = a.shape
    N = b.shape[1]
    nk = T // tk

    def body(a_ref, b_ref, o_ref, acc):
        k = pl.program_id(0)

        @pl.when(k == 0)
        def _():
            acc[...] = jnp.zeros_like(acc)

        acc[...] += _dot_tn(a_ref[...], b_ref[...])

        @pl.when(k == nk - 1)
        def _():
            o_ref[...] = acc[...].astype(BF16)

    return pl.pallas_call(
        body, name=name, grid=(nk,),
        in_specs=[pl.BlockSpec((tk, M), lambda k: (k, 0)), pl.BlockSpec((tk, N), lambda k: (k, 0))],
        out_specs=pl.BlockSpec((M, N), lambda k: (0, 0)),
        out_shape=jax.ShapeDtypeStruct((M, N), BF16),
        scratch_shapes=[pltpu.VMEM((M, N), F32)],
        compiler_params=_params(1),
    )(a, b)


def _glu_bwd(y, z, r, dyo, w, og, tm):
    T = y.shape[0]

    def body(y_ref, z_ref, r_ref, d_ref, w_ref, og_ref, dy_ref, dw_ref, db_ref, dog_ref):
        @pl.when(pl.program_id(0) == 0)
        def _():
            dw_ref[...] = jnp.zeros_like(dw_ref)
            db_ref[...] = jnp.zeros_like(db_ref)
            dog_ref[...] = jnp.zeros_like(dog_ref)

        y1, g1 = _gelu_and_grad(y_ref[...])
        sg = _sigmoid(z_ref[...])
        y2 = y1 * sg
        dy2, dogt = _rms_bwd(d_ref[...], y2, r_ref[...], og_ref[...])
        dog_ref[...] += jnp.sum(dogt, axis=0, keepdims=True)
        dz = dy2 * y1 * sg * (1.0 - sg)
        db_ref[...] += jnp.sum(dz, axis=0, keepdims=True)
        dzb = dz.astype(BF16)
        dw_ref[...] += _dot_tn(y1.astype(BF16), dzb)
        dy_ref[...] = (dy2 * sg + _dot_nt(dzb, w_ref[...])) * g1

    row = pl.BlockSpec((tm, SSM_WIDTH), lambda i: (i, 0))
    vec = pl.BlockSpec((1, SSM_WIDTH), lambda i: (0, 0))
    sq = pl.BlockSpec((SSM_WIDTH, SSM_WIDTH), lambda i: (0, 0))
    return pl.pallas_call(
        body, name="glu_bwd", grid=(T // tm,),
        in_specs=[row, row, pl.BlockSpec((tm, 1), lambda i: (i, 0)), row, sq, vec],
        out_specs=[row, sq, vec, vec],
        out_shape=[jax.ShapeDtypeStruct((T, SSM_WIDTH), F32), jax.ShapeDtypeStruct((SSM_WIDTH, SSM_WIDTH), F32),
                   jax.ShapeDtypeStruct((1, SSM_WIDTH), F32), jax.ShapeDtypeStruct((1, SSM_WIDTH), F32)],
        compiler_params=_params(1),
    )(y, z, r, dyo, w, og)


def _s5_bwd(proj, dy, bound, pm, pm_t, bre, bim, bre_t, bim_t, cre, cim, lam, d, carry=None):
    T = proj.shape[0]
    nt = T // S5_TILE
    sp = _s5_specs(T, True)

    def body(u_ref, dy_ref, bound_ref, pm_ref, pmt_ref, bre_ref, bim_ref, bret_ref, bimt_ref, cre_ref, cim_ref,
             lam_ref, d_ref,
             du_ref, dbre_ref, dbim_ref, dcre_ref, dcim_ref, dlam_ref, dd_ref, carry, ptab, sr, si, gr, gi):
        lr = lam_ref[0:1, :]
        li = lam_ref[1:2, :]

        @pl.when(pl.program_id(1) == 0)
        def _():
            carry[...] = jnp.zeros_like(carry)
            _fill_power_table(ptab, lr, li)
            for ref in (dbre_ref, dbim_ref, dcre_ref, dcim_ref, dlam_ref, dd_ref):
                ref[...] = jnp.zeros_like(ref)

        def one_tile(u_in, dy_in, b_r, b_i):
            u = _permute_rows_f32(pm_ref[...], u_in)
            ub = u.astype(BF16)
            dyv = _permute_rows_f32(pm_ref[...], dy_in)
            dyb = dyv.astype(BF16)
            _tile_set(sr, _dot(ub, bre_ref[...]))
            _tile_set(si, _dot(ub, bim_ref[...]))
            er, ei, _, _ = _s5_forward_states(sr, si, lr, li, b_r, b_i, ptab)
            _tile_set(gr, _dot(dyb, cre_ref[...]))
            _tile_set(gi, -_dot(dyb, cim_ref[...]))
            zr, zi = _chunk_scans(gr, gi, lr, -li, True)
            ar, ai = _table_rows(ptab, S5_STEPS - 1, True)
            fr, fi = _entering_states(zr, zi, carry[0:1, :], carry[1:2, :], ar, ai, True)
            acc_r = jnp.zeros((S5_CHUNKS, S5_LANES), F32)
            acc_i = jnp.zeros((S5_CHUNKS, S5_LANES), F32)
            for j in range(S5_STEPS):
                qr, qi = _table_rows(ptab, S5_STEPS - 1 - j, True)
                g_r = _step_get(gr, j) + qr * fr - qi * fi
                g_i = _step_get(gi, j) + qr * fi + qi * fr
                _step_set(gr, j, g_r)
                _step_set(gi, j, g_i)
                p_r, p_i = (er, ei) if j == 0 else (_step_get(sr, j - 1), _step_get(si, j - 1))
                acc_r += g_r * p_r + g_i * p_i
                acc_i += g_i * p_r - g_r * p_i
            dlam_ref[0:1, :] += jnp.sum(acc_r, axis=0, keepdims=True)
            dlam_ref[1:2, :] += jnp.sum(acc_i, axis=0, keepdims=True)
            g_all_r = _tile_get(gr)
            g_all_i = _tile_get(gi)
            carry[0:1, :] = g_all_r[0:1, :]
            carry[1:2, :] = g_all_i[0:1, :]
            grb = g_all_r.astype(BF16)
            gib = g_all_i.astype(BF16)
            du = (_dot(grb, bret_ref[...]) + _dot(gib, bimt_ref[...]) + d_ref[...] * dyv).astype(BF16)
            dbre_ref[...] += _dot_tn(grb, ub)
            dbim_ref[...] += _dot_tn(gib, ub)
            dcre_ref[...] += _dot_tn(dyb, _tile_get(sr).astype(BF16))
            dcim_ref[...] -= _dot_tn(dyb, _tile_get(si).astype(BF16))
            dd_ref[...] += jnp.sum(dyv * u, axis=0, keepdims=True)
            return _dot(pmt_ref[...], du).astype(BF16)

        for s in reversed(range(S5_PER_STEP)):
            rows = slice(s * S5_TILE, (s + 1) * S5_TILE)
            du_ref[rows, :] = one_tile(u_ref[rows, :], dy_ref[rows, :], bound_ref[s, 0:1, :], bound_ref[s, 1:2, :])

    acc_ts = pl.BlockSpec((None, S5_LANES, LANE), lambda b, t: (b, 0, 0))
    acc_fs = pl.BlockSpec((None, LANE, S5_LANES), lambda b, t: (b, 0, 0))
    return _pcall(
        body, "s5_bwd", (S5_NBLK, nt // S5_PER_STEP),
        [sp["u"], sp["rows"], sp["bound"], sp["perm"], sp["perm"], sp["to_state"], sp["to_state"],
         sp["from_state"], sp["from_state"], sp["to_state"], sp["to_state"], sp["lam"], sp["d"]],
        [sp["rows"], acc_ts, acc_ts, acc_fs, acc_fs, sp["lam"], sp["d"]],
        [jax.ShapeDtypeStruct((T, SSM_WIDTH), BF16),
         jax.ShapeDtypeStruct((S5_NBLK, S5_LANES, LANE), F32),
         jax.ShapeDtypeStruct((S5_NBLK, S5_LANES, LANE), F32),
         jax.ShapeDtypeStruct((S5_NBLK, LANE, S5_LANES), F32),
         jax.ShapeDtypeStruct((S5_NBLK, LANE, S5_LANES), F32),
         jax.ShapeDtypeStruct((S5_NBLK, 2, S5_LANES), F32),
         jax.ShapeDtypeStruct((1, SSM_WIDTH), F32)],
        [pltpu.VMEM((2, S5_LANES), F32), pltpu.VMEM((2, S5_TILE, S5_LANES), F32)]
        + [pltpu.VMEM(S5_STATE_TILE, F32)] * 4,
        (proj, dy, bound, pm, pm_t, bre, bim, bre_t, bim_t, cre, cim, lam, d), carry)


def _ret_bwd(proj, cosf, sinf, mask, rowdec, kdec, gtb, gn, sblk, dyr):
    T = proj.shape[0]
    nb = T // RET_BLOCK
    sp = _ret_specs(T, True)

    def body(first, q_ref, k_ref, v_ref, g_ref, cos_ref, sin_ref, mask_ref, rd_ref, kd_ref, gtb_ref, gn_ref, sb_ref,
             dy_ref, dq_ref, dk_ref, dv_ref, dg_ref, dgn_ref, dst):
        if first:
            @pl.when(pl.program_id(1) == 0)
            def _():
                dst[...] = jnp.zeros_like(dst)
                dgn_ref[...] = jnp.zeros_like(dgn_ref)

        s_in = sb_ref[...]
        q, k, qb, kb, vb, pm, qd, o = _ret_common(q_ref, k_ref, v_ref, cos_ref, sin_ref, mask_ref, rd_ref, s_in)
        mu = jnp.mean(o, axis=-1, keepdims=True)
        oc = o - mu
        rstd = lax.rsqrt(jnp.mean(oc * oc, axis=-1, keepdims=True) + EPS)
        n = oc * rstd
        gt = g_ref[...]
        sg = _sigmoid(gt)
        sil = gt * sg
        gnv = gn_ref[...]
        dyv = dy_ref[...]
        dg_ref[...] = (dyv * (n * gnv) * (sg * (1.0 + gt * (1.0 - sg)))).astype(BF16)
        dgn_ref[...] += jnp.sum(dyv * sil * n, axis=0, keepdims=True)
        dn = dyv * sil * gnv
        do = rstd * (dn - jnp.mean(dn, axis=-1, keepdims=True) - n * jnp.mean(dn * n, axis=-1, keepdims=True))
        dob = do.astype(BF16)
        ds = dst[...]
        dsb = ds.astype(BF16)
        kd = kd_ref[...]
        rd = rd_ref[...]
        dv_ref[...] = (_dot_tn(pm, dob) + _dot((k * kd).astype(BF16), dsb)).astype(BF16)
        dpb = (_dot_nt(dob, vb) * mask_ref[...]).astype(BF16)
        dq = _dot(dpb, kb) + _dot_nt(dob, s_in.astype(BF16)) * rd
        dk = (_dot_tn(dpb, qb) + _dot_nt(vb, dsb) * kd) * (HEAD_DIM ** -0.5)
        dst[...] = gtb_ref[...] * ds + _dot_tn(qd, dob)
        c = cos_ref[...]
        s = sin_ref[...]
        dq_ref[...] = (dq * c + pltpu.roll(dq * s, HEAD_DIM // 2, 1)).astype(BF16)
        dk_ref[...] = (dk * c + pltpu.roll(dk * s, HEAD_DIM // 2, 1)).astype(BF16)

    oshape = jax.ShapeDtypeStruct((T, RET_WIDTH), BF16)
    ins = [sp[n] for n in ("q", "k", "v", "g", "tab", "tab", "mask", "dec", "dec", "gtb", "gn", "state", "rows")]
    outs = [sp["rows"], sp["rows"], sp["rows"], sp["rows"], sp["gn"]]
    return pl.pallas_call(
        _per_head(body, [kind for _, kind in ins + outs + [sp["scratch"]]], True), name="ret_bwd",
        grid=(RET_HEADS // RET_HPS, nb // RET_PER_STEP), in_specs=[s for s, _ in ins],
        out_specs=[s for s, _ in outs],
        out_shape=[oshape, oshape, oshape, oshape, jax.ShapeDtypeStruct((1, RET_WIDTH), F32)],
        scratch_shapes=[sp["scratch"][0]],
        compiler_params=_params(2),
    )(proj, proj, proj, proj, cosf, sinf, mask, rowdec, kdec, gtb, gn, sblk, dyr)


def _in_proj_bwd(dproj, w, x, r1, g, dx2, tm, carry=None):
    T = x.shape[0]

    def body(dp_ref, w_hbm, x_ref, r_ref, g_ref, dx2_ref, gx_ref, dg_ref, w_ref, sem):
        @pl.when(pl.program_id(0) == 0)
        def _():
            _load_resident(w_hbm, w_ref, sem)
            dg_ref[...] = jnp.zeros_like(dg_ref)

        dh = _dot_nt(dp_ref[:, 0:WIN_BLK], w_ref[0])
        for k in range(1, N_DEV):
            dh = dh + _dot_nt(dp_ref[:, k * WIN_BLK:(k + 1) * WIN_BLK], w_ref[k])
        dxn, dgt = _rms_bwd(dh, x_ref[...], r_ref[...], g_ref[...])
        gx_ref[...] = dx2_ref[...] + dxn
        dg_ref[...] += jnp.sum(dgt, axis=0, keepdims=True)

    full = pl.BlockSpec((tm, D_MODEL), lambda i: (i, 0))
    vec = pl.BlockSpec((1, D_MODEL), lambda i: (0, 0))
    return _pcall(
        body, "in_proj_bwd", (T // tm,),
        [pl.BlockSpec((tm, IN_WIDTH), lambda i: (i, 0)), ANY_SPEC,
         full, pl.BlockSpec((tm, 1), lambda i: (i, 0)), vec, full],
        [full, vec],
        [jax.ShapeDtypeStruct((T, D_MODEL), F32), jax.ShapeDtypeStruct((1, D_MODEL), F32)],
        [pltpu.VMEM(w.shape, w.dtype), pltpu.SemaphoreType.DMA], (dproj, w, x, r1, g, dx2), carry)


def _in_proj_wgrad(h, dproj, tk, carry=None):
    T = h.shape[0]
    nk = T // tk

    def body(h_ref, dp_ref, o_ref, acc):
        k = pl.program_id(1)

        @pl.when(k == 0)
        def _():
            acc[...] = jnp.zeros_like(acc)

        acc[...] += _dot_tn(h_ref[...], dp_ref[...])

        @pl.when(k == nk - 1)
        def _():
            o_ref[...] = acc[...].astype(BF16)

    return _pcall(
        body, "in_proj_wgrad", (N_DEV, nk),
        [pl.BlockSpec((tk, D_MODEL), lambda j, k: (k, 0)), pl.BlockSpec((tk, WIN_BLK), lambda j, k: (k, j))],
        [pl.BlockSpec((None, D_MODEL, WIN_BLK), lambda j, k: (j, 0, 0))],
        [jax.ShapeDtypeStruct((N_DEV, D_MODEL, WIN_BLK), BF16)],
        [pltpu.VMEM((D_MODEL, WIN_BLK), F32)], (h, dproj), carry)


def _rope_tables(T):
    half = HEAD_DIM // 2
    freqs = ROPE_BASE ** (-jnp.arange(half, dtype=F32) / half)
    ang = jnp.arange(T, dtype=F32)[:, None] * freqs[None, :]
    c = jnp.cos(ang)
    s = jnp.sin(ang)
    return jnp.concatenate([c, c], axis=1), jnp.concatenate([-s, s], axis=1)


def _retention_tables():
    hh = jnp.arange(RET_HEADS, dtype=F32)
    log_g = jnp.log1p(-(2.0 ** (-5.0 - hh)))[:, None, None]
    i = jnp.arange(RET_BLOCK)
    ci = (i // CHUNK)[:, None]
    cj = (i // CHUNK)[None, :]
    diff = (i[:, None] - i[None, :]).astype(F32)
    expo = jnp.where(ci == cj, jnp.abs(diff), diff)
    mask = jnp.where((cj <= ci)[None], jnp.exp(log_g * expo[None]), 0.0)
    r = jnp.arange(RET_BLOCK, dtype=F32)[None, :, None]
    ones = jnp.ones((1, 1, HEAD_DIM), F32)
    rowdec = jnp.exp(log_g * (r + 1.0)) * ones
    kdec = jnp.exp(log_g * (RET_BLOCK - 1.0 - r)) * ones
    gtb = jnp.exp(log_g * float(RET_BLOCK)) * ones
    return mask, rowdec, kdec, gtb


def _s5_discretise(a_re, a_im, log_dt, b_re, b_im):
    lam = lax.complex(a_re, a_im)
    dt = jnp.exp(log_dt)[:, None]
    lam_bar = jnp.exp(lam * dt)
    b_bar = ((lam_bar - 1.0) / lam)[..., None] * lax.complex(b_re, b_im)
    return jnp.real(lam_bar), jnp.imag(lam_bar), jnp.real(b_bar), jnp.imag(b_bar)


def _to_state_blockdiag(m):
    eye = jnp.eye(S5_GB, dtype=m.dtype)
    t = jnp.einsum("bgpc,gh->bgchp", m.reshape(S5_NBLK, S5_GB, SSM_STATE, SSM_GROUP), eye)
    return t.reshape(S5_NBLK, LANE, S5_LANES)


def _from_state_blockdiag(m):
    eye = jnp.eye(S5_GB, dtype=m.dtype)
    t = jnp.einsum("bgcp,gh->bgphc", m.reshape(S5_NBLK, S5_GB, SSM_GROUP, SSM_STATE), eye)
    return t.reshape(S5_NBLK, S5_LANES, LANE)


def _diag_of_state_major(acc):
    eye = jnp.eye(S5_GB, dtype=acc.dtype)
    t = acc.reshape(S5_NBLK, S5_GB, SSM_STATE, S5_GB, SSM_GROUP)
    return jnp.einsum("bgphc,gh->bgpc", t, eye).reshape(SSM_GROUPS, SSM_STATE, SSM_GROUP)


def _diag_of_channel_major(acc):
    eye = jnp.eye(S5_GB, dtype=acc.dtype)
    t = acc.reshape(S5_NBLK, S5_GB, SSM_GROUP, S5_GB, SSM_STATE)
    return jnp.einsum("bgchp,gh->bgcp", t, eye).reshape(SSM_GROUPS, SSM_GROUP, SSM_STATE)


SMALL_PARTIALS = (("ret_gn_g", 1024), ("lam_re", 4096), ("lam_im", 4096),
                  ("bbar_re", 65536), ("bbar_im", 65536), ("c_re", 65536), ("c_im", 65536),
                  ("ssm_d", 1024), ("b_glu", 1024), ("out_g", 1024), ("norm_ffn_g", 2048), ("norm_final_g", 2048))


def _forward_backward(x, tgt, shards, sm):
    T = x.shape[0]
    tm = min(1024, T)
    cosf, sinf = _rope_tables(T)
    mask, rowdec, kdec, gtb = _retention_tables()
    lbr, lbi, bbr, bbi = _s5_discretise(sm["ssm_a_re"], sm["ssm_a_im"], sm["ssm_log_dt"], sm["ssm_b_re"],
                                        sm["ssm_b_im"])
    bre = _to_state_blockdiag(bbr).astype(BF16)
    bim = _to_state_blockdiag(bbi).astype(BF16)
    cre_t = _from_state_blockdiag(sm["ssm_c_re"]).astype(BF16)
    cim_t = _from_state_blockdiag(sm["ssm_c_im"]).astype(BF16)
    bre_t = jnp.swapaxes(bre, 1, 2)
    bim_t = jnp.swapaxes(bim, 1, 2)
    cre = jnp.swapaxes(cre_t, 1, 2)
    cim = jnp.swapaxes(cim_t, 1, 2)
    lam = jnp.stack([lbr.reshape(S5_NBLK, S5_LANES), lbi.reshape(S5_NBLK, S5_LANES)], axis=1)
    pm = _step_major_permutation()
    pm_t = pm.T
    row = lambda v: v.reshape(1, -1)
    g_mix, g_ffn, g_fin = row(sm["norm_mix_g"]), row(sm["norm_ffn_g"]), row(sm["norm_final_g"])
    gn, dsk, bglu, og = row(sm["ret_gn_g"]), row(sm["ssm_d"]), row(sm["ssm_b_glu"]), row(sm["ssm_out_g"])

    (w_in,) = _exchange_call("weight_gather", [shards["w_in"]], True, via_sibling=True)
    proj, h1, r1, w_gate = _in_proj_fwd(x, g_mix, w_in, 256, _Exchange([shards["w_gate"]], True, via_sibling=True))
    y_ret, sblk, w_glu, w_out = _ret_fwd(proj, cosf, sinf, mask, rowdec, kdec, gtb, gn,
                                         _Exchange([shards["ssm_w_glu"], shards["w_out"]], True))
    w_glu = w_glu.reshape(SSM_WIDTH, SSM_WIDTH)
    w_out = w_out.reshape(D_MODEL, D_MODEL)
    y_s5, bound, w_up = _s5_fwd(proj, pm, pm_t, bre, bim, cre_t, cim_t, lam, dsk, _Exchange([shards["w_up"]], True))
    z, y_ssm, r_ssm = _glu_fwd(y_s5, w_glu, bglu, og, 256)
    x2, h2, r2 = _out_proj_fwd(x, y_ret, y_ssm, w_out, g_ffn, 256)
    a, b, f, w_down = _ffn_up(h2, w_gate, w_up, tm, _Exchange([shards["w_down"]], True))
    dx3, dx3b, loss8, dg_fin = _ffn_down_loss(f, w_down, x2, tgt, g_fin, 256)

    landed = {}
    da, db = _ffn_bwd_act(dx3b, w_down, a, b, tm)
    dw_down = _ffn_wgrad_down(f, dx3b, tm)
    dw_gate, dw_up, landed["w_down"] = _ffn_wgrad_up(h2, da, db, tm, _Exchange([dw_down], False))
    dh2, landed["w_gate"] = _ffn_bwd_in(da, db, w_gate, w_up, min(1024, T), _Exchange([dw_gate], False))
    dx2, dx2b, dg_ffn, dy_ret, dy_ssm = _out_proj_bwd(dh2, x2, r2, g_ffn, dx3, w_out, 256)
    dw_out = jnp.concatenate([_wgrad_rows("out_proj_wgrad_ret", y_ret, dx2b, tm),
                              _wgrad_rows("out_proj_wgrad_ssm", y_ssm, dx2b, tm)], axis=0)
    dy_s5, dw_glu, db_glu, dog = _glu_bwd(y_s5, z, r_ssm, dy_ssm, w_glu, og, 256)
    du, dbre, dbim, dcre, dcim, dlam, dd, landed["w_up"] = _s5_bwd(
        proj, dy_s5, bound, pm, pm_t, bre, bim, bre_t, bim_t, cre, cim, lam, dsk, _Exchange([dw_up], False))
    dq, dk, dv, dgate, dgn = _ret_bwd(proj, cosf, sinf, mask, rowdec, kdec, gtb, gn, sblk, dy_ret)
    dproj = jnp.concatenate([dq, dk, dv, dgate, du], axis=1)
    small = dict(ret_gn_g=dgn, lam_re=dlam[:, 0], lam_im=dlam[:, 1],
                 bbar_re=_diag_of_state_major(dbre), bbar_im=_diag_of_state_major(dbim),
                 c_re=_diag_of_channel_major(dcre), c_im=_diag_of_channel_major(dcim),
                 ssm_d=dd, b_glu=db_glu, out_g=dog, norm_ffn_g=dg_ffn, norm_final_g=dg_fin)
    packed = _pack([small[n] for n, _ in SMALL_PARTIALS])
    dw_in, landed["w_out"], landed["ssm_w_glu"], small_landed = _in_proj_wgrad(
        h1, dproj, tm, _Exchange([dw_out.reshape(N_DEV, D_MODEL // N_DEV, D_MODEL),
                                  dw_glu.astype(BF16).reshape(N_DEV, SSM_WIDTH // N_DEV, SSM_WIDTH), packed],
                                 [False, False, True]))
    grad_x, dg_mix, landed["w_in"] = _in_proj_bwd(dproj, w_in, x, r1, g_mix, dx2, 256, _Exchange([dw_in], False))
    (mix_landed,) = _exchange_call("mix_gain_grad_gather", [_pack([dg_mix])], True)
    summed = dict(zip([n for n, _ in SMALL_PARTIALS],
                      _unpack(_sum_partials("small_grad_sum", small_landed), [(sz,) for _, sz in SMALL_PARTIALS])))
    summed["norm_mix_g"] = _sum_partials("mix_gain_grad_sum", mix_landed).reshape(-1)
    return loss8[0, 0], grad_x, landed, summed


def _small_grads(summed, sm):
    _, vjp = jax.vjp(_s5_discretise, sm["ssm_a_re"], sm["ssm_a_im"], sm["ssm_log_dt"], sm["ssm_b_re"], sm["ssm_b_im"])
    gp = (SSM_GROUPS, SSM_STATE)
    da_re, da_im, dlog_dt, db_re, db_im = vjp((summed["lam_re"].reshape(gp), summed["lam_im"].reshape(gp),
                                               summed["bbar_re"].reshape(gp + (SSM_GROUP,)),
                                               summed["bbar_im"].reshape(gp + (SSM_GROUP,))))
    return dict(norm_mix_g=summed["norm_mix_g"], ret_gn_g=summed["ret_gn_g"], ssm_a_re=da_re, ssm_a_im=da_im,
                ssm_log_dt=dlog_dt, ssm_b_re=db_re, ssm_b_im=db_im,
                ssm_c_re=summed["c_re"].reshape(SSM_GROUPS, SSM_GROUP, SSM_STATE),
                ssm_c_im=summed["c_im"].reshape(SSM_GROUPS, SSM_GROUP, SSM_STATE),
                ssm_d=summed["ssm_d"], ssm_b_glu=summed["b_glu"], ssm_out_g=summed["out_g"],
                norm_ffn_g=summed["norm_ffn_g"], norm_final_g=summed["norm_final_g"])


def _adamw_math(w, g, m, v):
    m2 = ADAM_B1 * m + (1.0 - ADAM_B1) * g
    v2 = ADAM_B2 * v + (1.0 - ADAM_B2) * (g * g)
    delta = -ADAM_LR * ((m2 / ADAM_BC1) / (jnp.sqrt(v2 / ADAM_BC2) + ADAM_EPS) + ADAM_WD * w)
    return delta, m2, v2


def _adamw_shard(name, parts, w, m, v, tr):
    rows, cols = w.shape

    def body(p_ref, w_ref, m_ref, v_ref, g_ref, d_ref, m2_ref, v2_ref):
        g = p_ref[0].astype(F32)
        for s in range(1, N_DEV):
            g = g + p_ref[s].astype(F32)
        d, m2, v2 = _adamw_math(w_ref[...], g, m_ref[...], v_ref[...])
        g_ref[...] = g
        d_ref[...] = d
        m2_ref[...] = m2
        v2_ref[...] = v2

    blk = pl.BlockSpec((tr, cols), lambda i: (i, 0))
    oshape = jax.ShapeDtypeStruct((rows, cols), F32)
    return pl.pallas_call(
        body, name=name, grid=(rows // tr,),
        in_specs=[pl.BlockSpec((N_DEV, tr, cols), lambda i: (0, i, 0)), blk, blk, blk],
        out_specs=[blk, blk, blk, blk], out_shape=[oshape] * 4,
        compiler_params=_params(1),
    )(parts, w, m, v)


def _sum_partials(name, parts):
    rows = parts.shape[1]

    def body(p_ref, o_ref):
        g = p_ref[0]
        for s in range(1, N_DEV):
            g = g + p_ref[s]
        o_ref[...] = g

    return pl.pallas_call(
        body, name=name, grid=(1,),
        in_specs=[pl.BlockSpec((N_DEV, rows, LANE), lambda i: (0, 0, 0))],
        out_specs=pl.BlockSpec((rows, LANE), lambda i: (0, 0)),
        out_shape=jax.ShapeDtypeStruct((rows, LANE), F32),
        compiler_params=_params(1),
    )(parts)


def _adamw_small(ws, gs, ms, vs):
    n = len(ws)

    def body(*refs):
        for i in range(n):
            w_ref, g_ref, m_ref, v_ref = (refs[k * n + i] for k in range(4))
            d_ref, m2_ref, v2_ref = (refs[(4 + k) * n + i] for k in range(3))
            d, m2, v2 = _adamw_math(w_ref[...], g_ref[...], m_ref[...], v_ref[...])
            d_ref[...] = d
            m2_ref[...] = m2
            v2_ref[...] = v2

    vmem = pl.BlockSpec(memory_space=pltpu.VMEM)
    out = pl.pallas_call(
        body, name="adamw_small", in_specs=[vmem] * (4 * n), out_specs=[vmem] * (3 * n),
        out_shape=[jax.ShapeDtypeStruct(w.shape, F32) for w in ws] * 3,
        compiler_params=pltpu.CompilerParams(vmem_limit_bytes=VMEM_LIMIT),
    )(*ws, *gs, *ms, *vs)
    return out[:n], out[n:2 * n], out[2 * n:]


def _pack(arrays):
    parts = [a.reshape(-1, LANE) for a in arrays]
    assert all(p.shape[0] % 8 == 0 for p in parts)
    return parts[0] if len(parts) == 1 else jnp.concatenate(parts, axis=0)


def _unpack(packed, shapes):
    flat = packed.reshape(-1)
    out, off = [], 0
    for shp in shapes:
        n = math.prod(shp)
        out.append(flat[off:off + n].reshape(shp))
        off += n + ((-n) % LANE)
    return out


WEIGHTS = ("norm_mix_g", "w_in", "ret_gn_g", "ssm_a_re", "ssm_a_im", "ssm_log_dt", "ssm_b_re", "ssm_b_im",
           "ssm_c_re", "ssm_c_im", "ssm_d", "ssm_w_glu", "ssm_b_glu", "ssm_out_g", "w_out", "norm_ffn_g", "w_gate",
           "w_up", "w_down", "norm_final_g")
BIG = ("w_in", "ssm_w_glu", "w_out", "w_gate", "w_up", "w_down")
SMALL = tuple(n for n in WEIGHTS if n not in BIG)
ADAM_ROWS = {"w_in": 256, "ssm_w_glu": 128, "w_out": 128, "w_gate": 256, "w_up": 256, "w_down": 176}


def kernel(x, norm_mix_g, w_in, ret_gn_g, ssm_a_re, ssm_a_im, ssm_log_dt, ssm_b_re, ssm_b_im, ssm_c_re, ssm_c_im, ssm_d, ssm_w_glu, ssm_b_glu, ssm_out_g, w_out, norm_ffn_g, w_gate, w_up, w_down, norm_final_g, loss_target, m_norm_mix_g, m_w_in, m_ret_gn_g, m_ssm_a_re, m_ssm_a_im, m_ssm_log_dt, m_ssm_b_re, m_ssm_b_im, m_ssm_c_re, m_ssm_c_im, m_ssm_d, m_ssm_w_glu, m_ssm_b_glu, m_ssm_out_g, m_w_out, m_norm_ffn_g, m_w_gate, m_w_up, m_w_down, m_norm_final_g, v_norm_mix_g, v_w_in, v_ret_gn_g, v_ssm_a_re, v_ssm_a_im, v_ssm_log_dt, v_ssm_b_re, v_ssm_b_im, v_ssm_c_re, v_ssm_c_im, v_ssm_d, v_ssm_w_glu, v_ssm_b_glu, v_ssm_out_g, v_w_out, v_norm_ffn_g, v_w_gate, v_w_up, v_w_down, v_norm_final_g):
    given = dict(locals())
    w = {n: given[n] for n in WEIGHTS}
    m = {n: given["m_" + n] for n in WEIGHTS}
    v = {n: given["v_" + n] for n in WEIGHTS}
    drop = lambda n, a: a if n == "norm_final_g" else a[0]
    w0 = {n: drop(n, w[n]) for n in WEIGHTS}
    m0 = {n: drop(n, m[n]) for n in WEIGHTS}
    v0 = {n: drop(n, v[n]) for n in WEIGHTS}

    sm = {n: w0[n] for n in SMALL}
    shards = {n: w0[n].astype(BF16) for n in BIG}
    loss_local, grad_x, landed, summed = _forward_backward(x[0], loss_target[0], shards, sm)
    loss = lax.psum(loss_local, MESH_AXES)
    gsmall = _small_grads(summed, sm)

    grads, delta, new_m, new_v = {}, {}, {}, {}
    for n in BIG:
        g, d, m2, v2 = _adamw_shard("adamw_" + n, landed[n], w0[n], m0[n], v0[n], ADAM_ROWS[n])
        grads[n], delta[n], new_m[n], new_v[n] = g, d, m2, v2
    as_given = lambda n, a: a.reshape(1, -1) if n == "norm_final_g" else a.reshape(w[n].shape)
    gs = [as_given(n, gsmall[n]) for n in SMALL]
    ds, m2s, v2s = _adamw_small([as_given(n, w[n]) for n in SMALL], gs, [as_given(n, m[n]) for n in SMALL],
                                [as_given(n, v[n]) for n in SMALL])
    for n, g, d, m2, v2 in zip(SMALL, gs, ds, m2s, v2s):
        grads[n], delta[n], new_m[n], new_v[n] = g, d, m2, v2

    lift = lambda n, a: a.reshape(w[n].shape)
    return (loss, grad_x[None], *[lift(n, grads[n]) for n in WEIGHTS], *[lift(n, delta[n]) for n in WEIGHTS],
            *[lift(n, new_m[n]) for n in WEIGHTS], *[lift(n, new_v[n]) for n in WEIGHTS])
```

```python
import functools
import math

import jax
import jax.numpy as jnp
from jax import lax
from jax.experimental import pallas as pl
from jax.experimental.pallas import tpu as pltpu

F32 = jnp.float32
BF16 = jnp.bfloat16

D_MODEL = 2048
RET_WIDTH = 1024
RET_HEADS = 8
HEAD_DIM = 128
CHUNK = 64
SSM_WIDTH = 1024
SSM_GROUP = 16
SSM_GROUPS = 64
SSM_STATE = 64
D_FF = 5632
IN_WIDTH = 5120
ROPE_BASE = 10000.0
EPS = 1e-6
N_DEV = 8
MESH_AXES = ("x", "y", "c")

WIN_BLK = IN_WIDTH // N_DEV
FF_BLK = D_FF // N_DEV
RET_BLOCK = 256
RET_HPS = 4
RET_PER_STEP = 4
S5_TILE = 256
S5_CHUNKS = 8
S5_STEPS = S5_TILE // S5_CHUNKS
S5_PER_STEP = 8
S5_GB = 8
S5_NBLK = SSM_GROUPS // S5_GB
S5_LANES = S5_GB * SSM_STATE
LANE = 128

ADAM_LR = 0.001
ADAM_B1 = 0.9
ADAM_B2 = 0.999
ADAM_EPS = 1e-08
ADAM_WD = 0.01
ADAM_STEP = 10
ADAM_BC1 = 1.0 - ADAM_B1 ** ADAM_STEP
ADAM_BC2 = 1.0 - ADAM_B2 ** ADAM_STEP

VMEM_LIMIT = 56 * 1024 * 1024

NT = (((1,), (1,)), ((), ()))
TN = (((0,), (0,)), ((), ()))


def _params(n_grid):
    return pltpu.CompilerParams(dimension_semantics=("arbitrary",) * n_grid, vmem_limit_bytes=VMEM_LIMIT)


def _dot(a, b):
    return jnp.dot(a, b, preferred_element_type=F32)


def _dot_nt(a, b):
    return lax.dot_general(a, b, NT, preferred_element_type=F32)


def _dot_tn(a, b):
    return lax.dot_general(a, b, TN, preferred_element_type=F32)


def _sigmoid(x):
    return 1.0 / (1.0 + jnp.exp(-x))


_GELU_C = math.sqrt(2.0 / math.pi)
_GELU_A = 0.044715


def _gelu(x):
    t = jnp.tanh(_GELU_C * (x + _GELU_A * x * x * x))
    return 0.5 * x * (1.0 + t)


def _gelu_and_grad(x):
    t = jnp.tanh(_GELU_C * (x + _GELU_A * x * x * x))
    g = 0.5 * (1.0 + t) + 0.5 * x * (1.0 - t * t) * _GELU_C * (1.0 + 3.0 * _GELU_A * x * x)
    return 0.5 * x * (1.0 + t), g


def _rms_bwd(dy, x, r, g):
    w = dy * g
    dx = r * w - x * (r * r * r) * jnp.mean(w * x, axis=-1, keepdims=True)
    return dx, dy * x * r


HBM_SPEC = pl.BlockSpec(memory_space=pltpu.HBM)
ANY_SPEC = pl.BlockSpec(memory_space=pl.ANY)


def _load_resident(src_hbm, dst_vmem, sem):
    cp = pltpu.make_async_copy(src_hbm, dst_vmem, sem)
    cp.start()
    cp.wait()


def _my_block():
    return 4 * lax.axis_index("x") + 2 * lax.axis_index("y") + lax.axis_index("c")


def _peer(k):
    px = lax.axis_index("x") ^ ((k >> 2) & 1)
    py = lax.axis_index("y") ^ ((k >> 1) & 1)
    pc = lax.axis_index("c") ^ (k & 1)
    return (px, py, pc), 4 * px + 2 * py + pc


class _Exchange:
    def __init__(self, payloads, gather, via_sibling=False):
        self.payloads = list(payloads)
        self.n = len(self.payloads)
        self.gather = [gather] * self.n if isinstance(gather, bool) else list(gather)
        self.via_sibling = via_sibling
        assert not via_sibling or all(self.gather)

    def out_shape(self):
        return [jax.ShapeDtypeStruct(((N_DEV,) if g else ()) + p.shape, p.dtype)
                for p, g in zip(self.payloads, self.gather)]

    def scratch_shapes(self):
        return [pltpu.SemaphoreType.DMA((self.n, N_DEV - 1)), pltpu.SemaphoreType.DMA((self.n, N_DEV - 1)),
                pltpu.SemaphoreType.DMA((self.n,))]

    def _copies(self, ins, outs, sems, incoming):
        send_sems, recv_sems, local_sems = sems
        me = _my_block()
        src_of = lambda i, blk: ins[i] if self.gather[i] else ins[i].at[blk]
        local, remote = [], []
        for i in range(self.n):
            if not incoming:
                local.append(pltpu.make_async_copy(src_of(i, me), outs[i].at[me], local_sems.at[i]))
            for k in range(1, N_DEV):
                dev, blk = _peer(k)
                src, dst = (outs[i].at[blk], outs[i].at[blk]) if incoming else (src_of(i, blk), outs[i].at[me])
                remote.append(pltpu.make_async_remote_copy(
                    src_ref=src, dst_ref=dst, send_sem=send_sems.at[i, k - 1], recv_sem=recv_sems.at[i, k - 1],
                    device_id=dev, device_id_type=pl.DeviceIdType.MESH))
        return local, remote

    def _copy(self, i, k, outs, sems, src, dst_blk, to_k):
        send_sems, recv_sems, _ = sems
        return pltpu.make_async_remote_copy(
            src_ref=src, dst_ref=outs[i].at[dst_blk], send_sem=send_sems.at[i, k - 1], recv_sem=recv_sems.at[i, k - 1],
            device_id=_peer(to_k)[0], device_id_type=pl.DeviceIdType.MESH)

    FIRST_HOPS = (1, 2, 4, 6)
    FROM_CHIPS = (2, 4, 6)

    def start(self, ins, outs, sems):
        if not self.via_sibling:
            local, sends = self._copies(ins, outs, sems, False)
            for cp in local + sends:
                cp.start()
            return
        me = _my_block()
        for i in range(self.n):
            pltpu.make_async_copy(ins[i], outs[i].at[me], sems[2].at[i]).start()
            for k in self.FIRST_HOPS:
                self._copy(i, k, outs, sems, ins[i], me, k).start()

    def wait(self, ins, outs, sems):
        if not self.via_sibling:
            for cp in self._copies(ins, outs, sems, True)[1]:
                cp.wait_recv()
            local, sends = self._copies(ins, outs, sems, False)
            for cp in sends:
                cp.wait_send()
            for cp in local:
                cp.wait()
            return
        me = _my_block()
        landed = lambda i, k: self._copy(i, k, outs, sems, outs[i].at[_peer(k)[1]], _peer(k)[1], k)
        for i in range(self.n):
            for s in self.FROM_CHIPS:
                landed(i, s).wait_recv()
                self._copy(i, s ^ 1, outs, sems, outs[i].at[_peer(s)[1]], _peer(s)[1], 1).start()
        for i in range(self.n):
            for k in (1, 3, 5, 7):
                landed(i, k).wait_recv()
            for k in self.FIRST_HOPS:
                self._copy(i, k, outs, sems, ins[i], me, k).wait_send()
            for s in self.FROM_CHIPS:
                self._copy(i, s ^ 1, outs, sems, outs[i].at[_peer(s)[1]], _peer(s)[1], 1).wait_send()
            pltpu.make_async_copy(ins[i], outs[i].at[me], sems[2].at[i]).wait()


def _pcall(body, name, grid, in_specs, out_specs, out_shape, scratch_shapes, args, carry=None):
    n_in, n_out, n_scr = len(in_specs), len(out_specs), len(scratch_shapes)
    if carry is None:
        return pl.pallas_call(body, name=name, grid=grid, in_specs=in_specs, out_specs=out_specs, out_shape=out_shape,
                              scratch_shapes=scratch_shapes, compiler_params=_params(len(grid)))(*args)
    nx = carry.n

    def wrapped(*refs):
        cin, xin = refs[:n_in], refs[n_in:n_in + nx]
        cout, xout = refs[n_in + nx:n_in + nx + n_out], refs[n_in + nx + n_out:n_in + 2 * nx + n_out]
        rest = refs[n_in + 2 * nx + n_out:]
        cscr, sems = rest[:n_scr], rest[n_scr:]
        first = functools.reduce(jnp.logical_and, [pl.program_id(a) == 0 for a in range(len(grid))])
        last = functools.reduce(jnp.logical_and, [pl.program_id(a) == grid[a] - 1 for a in range(len(grid))])

        @pl.when(first)
        def _():
            carry.start(xin, xout, sems)

        body(*cin, *cout, *cscr)

        @pl.when(last)
        def _():
            carry.wait(xin, xout, sems)

    return pl.pallas_call(
        wrapped, name=name, grid=grid, in_specs=list(in_specs) + [HBM_SPEC] * nx,
        out_specs=list(out_specs) + [HBM_SPEC] * nx, out_shape=list(out_shape) + carry.out_shape(),
        scratch_shapes=list(scratch_shapes) + carry.scratch_shapes(), compiler_params=_params(len(grid)),
    )(*args, *carry.payloads)


def _exchange_call(name, payloads, gather, via_sibling=False):
    ex = _Exchange(payloads, gather, via_sibling)

    def body(*refs):
        ins, outs, sems = refs[:ex.n], refs[ex.n:2 * ex.n], refs[2 * ex.n:]
        ex.start(ins, outs, sems)
        ex.wait(ins, outs, sems)

    return pl.pallas_call(body, name=name, in_specs=[HBM_SPEC] * ex.n, out_specs=[HBM_SPEC] * ex.n,
                          out_shape=ex.out_shape(), scratch_shapes=ex.scratch_shapes())(*ex.payloads)


def _in_proj_fwd(x, g, w, tm, carry=None):
    T = x.shape[0]

    def body(x_ref, g_ref, w_hbm, proj_ref, h_ref, r_ref, w_ref, sem):
        @pl.when(pl.program_id(0) == 0)
        def _():
            _load_resident(w_hbm, w_ref, sem)

        xf = x_ref[...]
        r = lax.rsqrt(jnp.mean(xf * xf, axis=-1, keepdims=True) + EPS)
        h = (xf * r * g_ref[...]).astype(BF16)
        h_ref[...] = h
        r_ref[...] = r
        for j in range(N_DEV):
            proj_ref[:, j * WIN_BLK:(j + 1) * WIN_BLK] = _dot(h, w_ref[j])

    return _pcall(
        body, "in_proj_fwd", (T // tm,),
        [pl.BlockSpec((tm, D_MODEL), lambda i: (i, 0)), pl.BlockSpec((1, D_MODEL), lambda i: (0, 0)), ANY_SPEC],
        [pl.BlockSpec((tm, IN_WIDTH), lambda i: (i, 0)),
         pl.BlockSpec((tm, D_MODEL), lambda i: (i, 0)),
         pl.BlockSpec((tm, 1), lambda i: (i, 0))],
        [jax.ShapeDtypeStruct((T, IN_WIDTH), F32),
         jax.ShapeDtypeStruct((T, D_MODEL), BF16),
         jax.ShapeDtypeStruct((T, 1), F32)],
        [pltpu.VMEM(w.shape, w.dtype), pltpu.SemaphoreType.DMA], (x, g, w), carry)


def _ret_common(q_ref, k_ref, v_ref, cos_ref, sin_ref, mask_ref, rd_ref, sin_state):
    c = cos_ref[...]
    s = sin_ref[...]
    q = q_ref[...]
    q = q * c + pltpu.roll(q, HEAD_DIM // 2, 1) * s
    k = k_ref[...]
    k = (k * c + pltpu.roll(k, HEAD_DIM // 2, 1) * s) * (HEAD_DIM ** -0.5)
    qb = q.astype(BF16)
    kb = k.astype(BF16)
    vb = v_ref[...].astype(BF16)
    pm = (_dot_nt(qb, kb) * mask_ref[...]).astype(BF16)
    qd = (q * rd_ref[...]).astype(BF16)
    o = _dot(pm, vb) + _dot(qd, sin_state.astype(BF16))
    return q, k, qb, kb, vb, pm, qd, o


def _ret_specs(T, rev):
    rows = RET_BLOCK * RET_PER_STEP
    nb = T // rows
    groups = RET_HEADS // RET_HPS
    wide = RET_HPS * HEAD_DIM
    blk = (lambda b: nb - 1 - b) if rev else (lambda b: b)
    col = lambda piece: (pl.BlockSpec((rows, wide), lambda h, b: (blk(b), piece * groups + h)), "rows_lane")
    return dict(
        q=col(0), k=col(1), v=col(2), g=col(3),
        tab=(pl.BlockSpec((rows, HEAD_DIM), lambda h, b: (blk(b), 0)), "rows"),
        mask=(pl.BlockSpec((RET_HPS, RET_BLOCK, RET_BLOCK), lambda h, b: (h, 0, 0)), "lead"),
        dec=(pl.BlockSpec((RET_HPS, RET_BLOCK, HEAD_DIM), lambda h, b: (h, 0, 0)), "lead"),
        gtb=(pl.BlockSpec((RET_HPS, 1, HEAD_DIM), lambda h, b: (h, 0, 0)), "lead"),
        gn=(pl.BlockSpec((1, wide), lambda h, b: (0, h)), "lane"),
        state=(pl.BlockSpec((RET_HPS, RET_PER_STEP, HEAD_DIM, HEAD_DIM), lambda h, b: (h, blk(b), 0, 0)), "state"),
        rows=(pl.BlockSpec((rows, wide), lambda h, b: (blk(b), h)), "rows_lane"),
        scratch=(pltpu.VMEM((RET_HPS, HEAD_DIM, HEAD_DIM), F32), "lead"),
    )


def _per_head(head_body, kinds, rev):
    def body(*refs):
        order = list(reversed(range(RET_PER_STEP))) if rev else list(range(RET_PER_STEP))
        for hh in range(RET_HPS):
            lanes = slice(hh * HEAD_DIM, (hh + 1) * HEAD_DIM)
            for s in order:
                rows = slice(s * RET_BLOCK, (s + 1) * RET_BLOCK)
                cut = {"rows_lane": lambda r: r.at[rows, lanes], "rows": lambda r: r.at[rows, :],
                       "lane": lambda r: r.at[:, lanes], "lead": lambda r: r.at[hh], "state": lambda r: r.at[hh, s]}
                head_body(s == order[0], *[cut[kind](ref) for ref, kind in zip(refs, kinds)])
    return body


def _ret_fwd(proj, cosf, sinf, mask, rowdec, kdec, gtb, gn, carry=None):
    T = proj.shape[0]
    nb = T // RET_BLOCK
    sp = _ret_specs(T, False)

    def body(first, q_ref, k_ref, v_ref, g_ref, cos_ref, sin_ref, mask_ref, rd_ref, kd_ref, gtb_ref, gn_ref,
             y_ref, sb_ref, st):
        if first:
            @pl.when(pl.program_id(1) == 0)
            def _():
                st[...] = jnp.zeros_like(st)
        s_in = st[...]
        sb_ref[...] = s_in
        q, k, qb, kb, vb, pm, qd, o = _ret_common(q_ref, k_ref, v_ref, cos_ref, sin_ref, mask_ref, rd_ref, s_in)
        st[...] = gtb_ref[...] * s_in + _dot_tn((k * kd_ref[...]).astype(BF16), vb)
        mu = jnp.mean(o, axis=-1, keepdims=True)
        oc = o - mu
        n = oc * lax.rsqrt(jnp.mean(oc * oc, axis=-1, keepdims=True) + EPS)
        gt = g_ref[...]
        y_ref[...] = (gt * _sigmoid(gt) * (n * gn_ref[...])).astype(BF16)

    ins = [sp[n] for n in ("q", "k", "v", "g", "tab", "tab", "mask", "dec", "dec", "gtb", "gn")]
    outs = [sp["rows"], sp["state"]]
    return _pcall(
        _per_head(body, [kind for _, kind in ins + outs + [sp["scratch"]]], False), "ret_fwd",
        (RET_HEADS // RET_HPS, nb // RET_PER_STEP), [s for s, _ in ins], [s for s, _ in outs],
        [jax.ShapeDtypeStruct((T, RET_WIDTH), BF16),
         jax.ShapeDtypeStruct((RET_HEADS, nb, HEAD_DIM, HEAD_DIM), F32)],
        [sp["scratch"][0]],
        (proj, proj, proj, proj, cosf, sinf, mask, rowdec, kdec, gtb, gn), carry)


def _scan(re, im, ar, ai, reverse):
    n = re.shape[0]
    row = lax.broadcasted_iota(jnp.int32, re.shape, 0)
    s = 1
    while s < n:
        if reverse:
            keep = row < n - s
            sr = jnp.where(keep, pltpu.roll(re, n - s, 0), 0.0)
            si = jnp.where(keep, pltpu.roll(im, n - s, 0), 0.0)
        else:
            keep = row >= s
            sr = jnp.where(keep, pltpu.roll(re, s, 0), 0.0)
            si = jnp.where(keep, pltpu.roll(im, s, 0), 0.0)
        re, im = re + ar * sr - ai * si, im + ar * si + ai * sr
        ar, ai = ar * ar - ai * ai, 2.0 * ar * ai
        s *= 2
    return re, im


S5_STATE_TILE = (S5_TILE, S5_LANES)


def _step_major_permutation():
    r = jnp.arange(S5_TILE)
    t_of_row = (r % S5_CHUNKS) * S5_STEPS + r // S5_CHUNKS
    return (t_of_row[:, None] == r[None, :]).astype(BF16)


def _permute_rows_f32(pm, x):
    hi = x.astype(BF16)
    rest = x - hi.astype(F32)
    mid = rest.astype(BF16)
    lo = (rest - mid.astype(F32)).astype(BF16)
    return _dot(pm, hi) + _dot(pm, mid) + _dot(pm, lo)


def _step_get(ref, j):
    return ref[j * S5_CHUNKS:(j + 1) * S5_CHUNKS, :]


def _step_set(ref, j, val):
    ref[j * S5_CHUNKS:(j + 1) * S5_CHUNKS, :] = val


def _tile_get(ref):
    return ref[...]


def _tile_set(ref, val):
    ref[...] = val


def _fill_power_table(ptab, lr, li):
    shape = (S5_CHUNKS, S5_LANES)
    lrb = jnp.broadcast_to(lr, shape)
    lib = jnp.broadcast_to(li, shape)
    pr, pi_ = lrb, lib
    for j in range(S5_STEPS):
        ptab[0, j * S5_CHUNKS:(j + 1) * S5_CHUNKS, :] = pr
        ptab[1, j * S5_CHUNKS:(j + 1) * S5_CHUNKS, :] = pi_
        pr, pi_ = lrb * pr - lib * pi_, lrb * pi_ + lib * pr


def _chunk_scans(xr, xi, lr, li, reverse):
    shape = (S5_CHUNKS, S5_LANES)
    lrb = jnp.broadcast_to(lr, shape)
    lib = jnp.broadcast_to(li, shape)
    sr = si = None
    for j in (range(S5_STEPS - 1, -1, -1) if reverse else range(S5_STEPS)):
        vr = _step_get(xr, j)
        vi = _step_get(xi, j)
        if sr is not None:
            vr, vi = vr + lrb * sr - lib * si, vi + lrb * si + lib * sr
            _step_set(xr, j, vr)
            _step_set(xi, j, vi)
        sr, si = vr, vi
    return sr, si


def _entering_states(zr, zi, cr, ci, ar, ai, reverse):
    shape = (S5_CHUNKS, S5_LANES)
    row = lax.broadcasted_iota(jnp.int32, shape, 0)
    if reverse:
        edge, shift = row == S5_CHUNKS - 1, S5_CHUNKS - 1
    else:
        edge, shift = row == 0, 1
    wr = jnp.where(edge, jnp.broadcast_to(cr, shape), pltpu.roll(zr, shift, 0))
    wi = jnp.where(edge, jnp.broadcast_to(ci, shape), pltpu.roll(zi, shift, 0))
    return _scan(wr, wi, ar, ai, reverse)


def _table_rows(ptab, j, conj):
    pr = ptab[0, j * S5_CHUNKS:(j + 1) * S5_CHUNKS, :]
    pi_ = ptab[1, j * S5_CHUNKS:(j + 1) * S5_CHUNKS, :]
    return pr, (-pi_ if conj else pi_)


def _s5_forward_states(xr, xi, lr, li, cr, ci, ptab):
    zr, zi = _chunk_scans(xr, xi, lr, li, False)
    ar, ai = _table_rows(ptab, S5_STEPS - 1, False)
    er, ei = _entering_states(zr, zi, cr, ci, ar, ai, False)
    for j in range(S5_STEPS):
        pr, pi_ = _table_rows(ptab, j, False)
        _step_set(xr, j, _step_get(xr, j) + pr * er - pi_ * ei)
        _step_set(xi, j, _step_get(xi, j) + pr * ei + pi_ * er)
    last = S5_CHUNKS - 1
    end_r = (ar * er - ai * ei + zr)[last:last + 1, :]
    end_i = (ar * ei + ai * er + zi)[last:last + 1, :]
    return er, ei, end_r, end_i


def _s5_specs(T, rev):
    rows = S5_TILE * S5_PER_STEP
    nt = T // rows
    tt = (lambda t: nt - 1 - t) if rev else (lambda t: t)
    return dict(
        u=pl.BlockSpec((rows, LANE), lambda b, t: (tt(t), 4 * RET_HEADS + b)),
        rows=pl.BlockSpec((rows, LANE), lambda b, t: (tt(t), b)),
        to_state=pl.BlockSpec((None, LANE, S5_LANES), lambda b, t: (b, 0, 0)),
        from_state=pl.BlockSpec((None, S5_LANES, LANE), lambda b, t: (b, 0, 0)),
        lam=pl.BlockSpec((None, 2, S5_LANES), lambda b, t: (b, 0, 0)),
        d=pl.BlockSpec((1, LANE), lambda b, t: (0, b)),
        perm=pl.BlockSpec((S5_TILE, S5_TILE), lambda b, t: (0, 0)),
        bound=pl.BlockSpec((None, S5_PER_STEP, 2, S5_LANES), lambda b, t: (b, tt(t), 0, 0)),
    )


def _s5_fwd(proj, pm, pm_t, bre, bim, cre_t, cim_t, lam, d, carry=None):
    T = proj.shape[0]
    nt = T // S5_TILE
    sp = _s5_specs(T, False)

    def body(u_ref, pm_ref, pmt_ref, bre_ref, bim_ref, cre_ref, cim_ref, lam_ref, d_ref, y_ref, bound_ref,
             carry, ptab, xr, xi):
        lr = lam_ref[0:1, :]
        li = lam_ref[1:2, :]

        @pl.when(pl.program_id(1) == 0)
        def _():
            carry[...] = jnp.zeros_like(carry)
            _fill_power_table(ptab, lr, li)

        for s in range(S5_PER_STEP):
            rows = slice(s * S5_TILE, (s + 1) * S5_TILE)
            u = _permute_rows_f32(pm_ref[...], u_ref[rows, :])
            ub = u.astype(BF16)
            _tile_set(xr, _dot(ub, bre_ref[...]))
            _tile_set(xi, _dot(ub, bim_ref[...]))
            bound_ref[s] = carry[...]
            _, _, end_r, end_i = _s5_forward_states(xr, xi, lr, li, carry[0:1, :], carry[1:2, :], ptab)
            carry[0:1, :] = end_r
            carry[1:2, :] = end_i
            y = (_dot(_tile_get(xr).astype(BF16), cre_ref[...]) - _dot(_tile_get(xi).astype(BF16), cim_ref[...])
                 + d_ref[...] * u)
            y_ref[rows, :] = _permute_rows_f32(pmt_ref[...], y)

    state = pltpu.VMEM(S5_STATE_TILE, F32)
    return _pcall(
        body, "s5_fwd", (S5_NBLK, nt // S5_PER_STEP),
        [sp["u"], sp["perm"], sp["perm"], sp["to_state"], sp["to_state"], sp["from_state"],
         sp["from_state"], sp["lam"], sp["d"]],
        [sp["rows"], sp["bound"]],
        [jax.ShapeDtypeStruct((T, SSM_WIDTH), F32),
         jax.ShapeDtypeStruct((S5_NBLK, nt, 2, S5_LANES), F32)],
        [pltpu.VMEM((2, S5_LANES), F32), pltpu.VMEM((2, S5_TILE, S5_LANES), F32), state, state],
        (proj, pm, pm_t, bre, bim, cre_t, cim_t, lam, d), carry)


def _glu_fwd(y, w, b, og, tm):
    T = y.shape[0]

    def body(y_ref, w_ref, b_ref, og_ref, z_ref, o_ref, r_ref):
        y1 = _gelu(y_ref[...])
        z = _dot(y1.astype(BF16), w_ref[...]) + b_ref[...]
        y2 = y1 * _sigmoid(z)
        r = lax.rsqrt(jnp.mean(y2 * y2, axis=-1, keepdims=True) + EPS)
        z_ref[...] = z
        o_ref[...] = (y2 * r * og_ref[...]).astype(BF16)
        r_ref[...] = r

    row = pl.BlockSpec((tm, SSM_WIDTH), lambda i: (i, 0))
    vec = pl.BlockSpec((1, SSM_WIDTH), lambda i: (0, 0))
    return pl.pallas_call(
        body, name="glu_fwd", grid=(T // tm,),
        in_specs=[row, pl.BlockSpec((SSM_WIDTH, SSM_WIDTH), lambda i: (0, 0)), vec, vec],
        out_specs=[row, row, pl.BlockSpec((tm, 1), lambda i: (i, 0))],
        out_shape=[jax.ShapeDtypeStruct((T, SSM_WIDTH), F32), jax.ShapeDtypeStruct((T, SSM_WIDTH), BF16),
                   jax.ShapeDtypeStruct((T, 1), F32)],
        compiler_params=_params(1),
    )(y, w, b, og)


def _out_proj_fwd(x, y_ret, y_ssm, w, g, tm):
    T = x.shape[0]

    def body(x_ref, a_ref, b_ref, w_ref, g_ref, x2_ref, h_ref, r_ref):
        x2 = x_ref[...] + _dot(a_ref[...], w_ref[0:RET_WIDTH, :]) + _dot(b_ref[...], w_ref[RET_WIDTH:D_MODEL, :])
        r = lax.rsqrt(jnp.mean(x2 * x2, axis=-1, keepdims=True) + EPS)
        x2_ref[...] = x2
        h_ref[...] = (x2 * r * g_ref[...]).astype(BF16)
        r_ref[...] = r

    full = pl.BlockSpec((tm, D_MODEL), lambda i: (i, 0))
    half = pl.BlockSpec((tm, RET_WIDTH), lambda i: (i, 0))
    return pl.pallas_call(
        body, name="out_proj_fwd", grid=(T // tm,),
        in_specs=[full, half, half, pl.BlockSpec((D_MODEL, D_MODEL), lambda i: (0, 0)),
                  pl.BlockSpec((1, D_MODEL), lambda i: (0, 0))],
        out_specs=[full, full, pl.BlockSpec((tm, 1), lambda i: (i, 0))],
        out_shape=[jax.ShapeDtypeStruct((T, D_MODEL), F32), jax.ShapeDtypeStruct((T, D_MODEL), BF16),
                   jax.ShapeDtypeStruct((T, 1), F32)],
        compiler_params=_params(1),
    )(x, y_ret, y_ssm, w, g)


def _ffn_up(h, wg, wu, tm, carry=None):
    T = h.shape[0]

    def body(h_ref, wg_ref, wu_ref, a_ref, b_ref, f_ref):
        hb = h_ref[...]
        a = _dot(hb, wg_ref[...])
        b = _dot(hb, wu_ref[...])
        a_ref[...] = a.astype(BF16)
        b_ref[...] = b.astype(BF16)
        f_ref[...] = (a * _sigmoid(a) * b).astype(BF16)

    wspec = pl.BlockSpec((None, D_MODEL, FF_BLK), lambda j, i: (j, 0, 0))
    ospec = pl.BlockSpec((None, tm, FF_BLK), lambda j, i: (j, i, 0))
    oshape = jax.ShapeDtypeStruct((N_DEV, T, FF_BLK), BF16)
    return _pcall(
        body, "ffn_up", (N_DEV, T // tm),
        [pl.BlockSpec((tm, D_MODEL), lambda j, i: (i, 0)), wspec, wspec],
        [ospec, ospec, ospec], [oshape, oshape, oshape], [], (h, wg, wu), carry)


def _ffn_down_loss(f, wd, x2, tgt, g, tm):
    T = x2.shape[0]

    def body(f_ref, w_hbm, x2_ref, t_ref, g_ref, dx_ref, dxb_ref, loss_ref, dg_ref, w_ref, sem):
        i = pl.program_id(0)

        @pl.when(i == 0)
        def _():
            _load_resident(w_hbm, w_ref, sem)
            loss_ref[...] = jnp.zeros_like(loss_ref)
            dg_ref[...] = jnp.zeros_like(dg_ref)

        gv = g_ref[...]
        x3 = x2_ref[...]
        for k in range(N_DEV):
            x3 = x3 + _dot(f_ref[k], w_ref[k])
        r = lax.rsqrt(jnp.mean(x3 * x3, axis=-1, keepdims=True) + EPS)
        err = x3 * r * gv - t_ref[...]
        tile_loss = 0.5 * jnp.sum(jnp.mean(err * err, axis=-1, keepdims=True), axis=0, keepdims=True)
        dx, dgt = _rms_bwd(err * (1.0 / D_MODEL), x3, r, gv)
        dx_ref[...] = dx
        dxb_ref[...] = dx.astype(BF16)
        loss_ref[...] += jnp.broadcast_to(tile_loss, loss_ref.shape)
        dg_ref[...] += jnp.sum(dgt, axis=0, keepdims=True)

    full = pl.BlockSpec((tm, D_MODEL), lambda i: (i, 0))
    vec = pl.BlockSpec((1, D_MODEL), lambda i: (0, 0))
    return pl.pallas_call(
        body, name="ffn_down_loss", grid=(T // tm,),
        in_specs=[pl.BlockSpec((N_DEV, tm, FF_BLK), lambda i: (0, i, 0)), ANY_SPEC, full, full, vec],
        out_specs=[full, full, pl.BlockSpec((8, LANE), lambda i: (0, 0)), vec],
        out_shape=[jax.ShapeDtypeStruct((T, D_MODEL), F32), jax.ShapeDtypeStruct((T, D_MODEL), BF16),
                   jax.ShapeDtypeStruct((8, LANE), F32), jax.ShapeDtypeStruct((1, D_MODEL), F32)],
        scratch_shapes=[pltpu.VMEM(wd.shape, wd.dtype), pltpu.SemaphoreType.DMA],
        compiler_params=_params(1),
    )(f, wd, x2, tgt, g)


def _ffn_bwd_act(dxb, wd, a, b, tm):
    T = dxb.shape[0]

    def body(dx_ref, w_ref, a_ref, b_ref, da_ref, db_ref):
        df = _dot_nt(dx_ref[...], w_ref[...])
        a = a_ref[...].astype(F32)
        b = b_ref[...].astype(F32)
        sg = _sigmoid(a)
        da_ref[...] = (df * b * sg * (1.0 + a * (1.0 - sg))).astype(BF16)
        db_ref[...] = (df * a * sg).astype(BF16)

    blk = pl.BlockSpec((None, tm, FF_BLK), lambda j, i: (j, i, 0))
    oshape = jax.ShapeDtypeStruct((N_DEV, T, FF_BLK), BF16)
    return pl.pallas_call(
        body, name="ffn_bwd_act", grid=(N_DEV, T // tm),
        in_specs=[pl.BlockSpec((tm, D_MODEL), lambda j, i: (i, 0)),
                  pl.BlockSpec((None, FF_BLK, D_MODEL), lambda j, i: (j, 0, 0)), blk, blk],
        out_specs=[blk, blk], out_shape=[oshape, oshape],
        compiler_params=_params(2),
    )(dxb, wd, a, b)


def _ffn_bwd_in(da, db, wg, wu, tm, carry=None):
    T = da.shape[1]

    def body(da_ref, db_ref, wg_ref, wu_ref, dh_ref):
        part = _dot_nt(da_ref[...], wg_ref[...]) + _dot_nt(db_ref[...], wu_ref[...])

        @pl.when(pl.program_id(1) == 0)
        def _():
            dh_ref[...] = part

        @pl.when(pl.program_id(1) > 0)
        def _():
            dh_ref[...] += part

    ablk = pl.BlockSpec((None, tm, FF_BLK), lambda i, k: (k, i, 0))
    wblk = pl.BlockSpec((None, D_MODEL, FF_BLK), lambda i, k: (k, 0, 0))
    return _pcall(
        body, "ffn_bwd_in", (T // tm, N_DEV), [ablk, ablk, wblk, wblk],
        [pl.BlockSpec((tm, D_MODEL), lambda i, k: (i, 0))], [jax.ShapeDtypeStruct((T, D_MODEL), F32)],
        [], (da, db, wg, wu), carry)


def _ffn_wgrad_up(h, da, db, tk, carry=None):
    T = h.shape[0]
    nk = T // tk

    def body(h_ref, da_ref, db_ref, g_ref, u_ref, accg, accu):
        k = pl.program_id(1)

        @pl.when(k == 0)
        def _():
            accg[...] = jnp.zeros_like(accg)
            accu[...] = jnp.zeros_like(accu)

        hb = h_ref[...]
        accg[...] += _dot_tn(hb, da_ref[...])
        accu[...] += _dot_tn(hb, db_ref[...])

        @pl.when(k == nk - 1)
        def _():
            g_ref[...] = accg[...].astype(BF16)
            u_ref[...] = accu[...].astype(BF16)

    blk = pl.BlockSpec((None, tk, FF_BLK), lambda j, k: (j, k, 0))
    ospec = pl.BlockSpec((None, D_MODEL, FF_BLK), lambda j, k: (j, 0, 0))
    oshape = jax.ShapeDtypeStruct((N_DEV, D_MODEL, FF_BLK), BF16)
    return _pcall(
        body, "ffn_wgrad_up", (N_DEV, nk),
        [pl.BlockSpec((tk, D_MODEL), lambda j, k: (k, 0)), blk, blk],
        [ospec, ospec], [oshape, oshape],
        [pltpu.VMEM((D_MODEL, FF_BLK), F32), pltpu.VMEM((D_MODEL, FF_BLK), F32)], (h, da, db), carry)


def _ffn_wgrad_down(f, dxb, tk):
    T = dxb.shape[0]
    nk = T // tk

    def body(f_ref, dx_ref, o_ref, acc):
        k = pl.program_id(1)

        @pl.when(k == 0)
        def _():
            acc[...] = jnp.zeros_like(acc)

        acc[...] += _dot_tn(f_ref[...], dx_ref[...])

        @pl.when(k == nk - 1)
        def _():
            o_ref[...] = acc[...].astype(BF16)

    return pl.pallas_call(
        body, name="ffn_wgrad_down", grid=(N_DEV, nk),
        in_specs=[pl.BlockSpec((None, tk, FF_BLK), lambda j, k: (j, k, 0)),
                  pl.BlockSpec((tk, D_MODEL), lambda j, k: (k, 0))],
        out_specs=pl.BlockSpec((None, FF_BLK, D_MODEL), lambda j, k: (j, 0, 0)),
        out_shape=jax.ShapeDtypeStruct((N_DEV, FF_BLK, D_MODEL), BF16),
        scratch_shapes=[pltpu.VMEM((FF_BLK, D_MODEL), F32)],
        compiler_params=_params(2),
    )(f, dxb)


def _out_proj_bwd(dh2, x2, r2, g, dx3, w, tm):
    T = x2.shape[0]

    def body(dh_ref, x_ref, r_ref, g_ref, dx3_ref, w_ref, dx_ref, dxb_ref, dg_ref, a_ref, b_ref):
        @pl.when(pl.program_id(0) == 0)
        def _():
            dg_ref[...] = jnp.zeros_like(dg_ref)

        dxn, dgt = _rms_bwd(dh_ref[...], x_ref[...], r_ref[...], g_ref[...])
        dx = dx3_ref[...] + dxn
        dxv = dx.astype(BF16)
        dx_ref[...] = dx
        dxb_ref[...] = dxv
        dg_ref[...] += jnp.sum(dgt, axis=0, keepdims=True)
        a_ref[...] = _dot_nt(dxv, w_ref[0:RET_WIDTH, :])
        b_ref[...] = _dot_nt(dxv, w_ref[RET_WIDTH:D_MODEL, :])

    full = pl.BlockSpec((tm, D_MODEL), lambda i: (i, 0))
    vec = pl.BlockSpec((1, D_MODEL), lambda i: (0, 0))
    half = pl.BlockSpec((tm, RET_WIDTH), lambda i: (i, 0))
    hshape = jax.ShapeDtypeStruct((T, RET_WIDTH), F32)
    return pl.pallas_call(
        body, name="out_proj_bwd", grid=(T // tm,),
        in_specs=[full, full, pl.BlockSpec((tm, 1), lambda i: (i, 0)), vec, full,
                  pl.BlockSpec((D_MODEL, D_MODEL), lambda i: (0, 0))],
        out_specs=[full, full, vec, half, half],
        out_shape=[jax.ShapeDtypeStruct((T, D_MODEL), F32), jax.ShapeDtypeStruct((T, D_MODEL), BF16),
                   jax.ShapeDtypeStruct((1, D_MODEL), F32), hshape, hshape],
        compiler_params=_params(1),
    )(dh2, x2, r2, g, dx3, w)


def _wgrad_rows(name, a, b, tk):
    T, M = a.shape
    N = b.shape[1]
    nk = T // tk

    def body(a_ref, b_ref, o_ref, acc):
        k = pl.program_id(0)

        @pl.when(k == 0)
        def _():
            acc[...] = jnp.zeros_like(acc)

        acc[...] += _dot_tn(a_ref[...], b_ref[...])

        @pl.when(k == nk - 1)
        def _():
            o_ref[...] = acc[...].astype(BF16)

    return pl.pallas_call(
        body, name=name, grid=(nk,),
        in_specs=[pl.BlockSpec((tk, M), lambda k: (k, 0)), pl.BlockSpec((tk, N), lambda k: (k, 0))],
        out_specs=pl.BlockSpec((M, N), lambda k: (0, 0)),
        out_shape=jax.ShapeDtypeStruct((M, N), BF16),
        scratch_shapes=[pltpu.VMEM((M, N), F32)],
        compiler_params=_params(1),
    )(a, b)


def _glu_bwd(y, z, r, dyo, w, og, tm):
    T = y.shape[0]

    def body(y_ref, z_ref, r_ref, d_ref, w_ref, og_ref, dy_ref, dw_ref, db_ref, dog_ref):
        @pl.when(pl.program_id(0) == 0)
        def _():
            dw_ref[...] = jnp.zeros_like(dw_ref)
            db_ref[...] = jnp.zeros_like(db_ref)
            dog_ref[...] = jnp.zeros_like(dog_ref)

        y1, g1 = _gelu_and_grad(y_ref[...])
        sg = _sigmoid(z_ref[...])
        y2 = y1 * sg
        dy2, dogt = _rms_bwd(d_ref[...], y2, r_ref[...], og_ref[...])
        dog_ref[...] += jnp.sum(dogt, axis=0, keepdims=True)
        dz = dy2 * y1 * sg * (1.0 - sg)
        db_ref[...] += jnp.sum(dz, axis=0, keepdims=True)
        dzb = dz.astype(BF16)
        dw_ref[...] += _dot_tn(y1.astype(BF16), dzb)
        dy_ref[...] = (dy2 * sg + _dot_nt(dzb, w_ref[...])) * g1

    row = pl.BlockSpec((tm, SSM_WIDTH), lambda i: (i, 0))
    vec = pl.BlockSpec((1, SSM_WIDTH), lambda i: (0, 0))
    sq = pl.BlockSpec((SSM_WIDTH, SSM_WIDTH), lambda i: (0, 0))
    return pl.pallas_call(
        body, name="glu_bwd", grid=(T // tm,),
        in_specs=[row, row, pl.BlockSpec((tm, 1), lambda i: (i, 0)), row, sq, vec],
        out_specs=[row, sq, vec, vec],
        out_shape=[jax.ShapeDtypeStruct((T, SSM_WIDTH), F32), jax.ShapeDtypeStruct((SSM_WIDTH, SSM_WIDTH), F32),
                   jax.ShapeDtypeStruct((1, SSM_WIDTH), F32), jax.ShapeDtypeStruct((1, SSM_WIDTH), F32)],
        compiler_params=_params(1),
    )(y, z, r, dyo, w, og)


def _s5_bwd(proj, dy, bound, pm, pm_t, bre, bim, bre_t, bim_t, cre, cim, lam, d, carry=None):
    T = proj.shape[0]
    nt = T // S5_TILE
    sp = _s5_specs(T, True)

    def body(u_ref, dy_ref, bound_ref, pm_ref, pmt_ref, bre_ref, bim_ref, bret_ref, bimt_ref, cre_ref, cim_ref,
             lam_ref, d_ref,
             du_ref, dbre_ref, dbim_ref, dcre_ref, dcim_ref, dlam_ref, dd_ref, carry, ptab, sr, si, gr, gi):
        lr = lam_ref[0:1, :]
        li = lam_ref[1:2, :]

        @pl.when(pl.program_id(1) == 0)
        def _():
            carry[...] = jnp.zeros_like(carry)
            _fill_power_table(ptab, lr, li)
            for ref in (dbre_ref, dbim_ref, dcre_ref, dcim_ref, dlam_ref, dd_ref):
                ref[...] = jnp.zeros_like(ref)

        def one_tile(u_in, dy_in, b_r, b_i):
            u = _permute_rows_f32(pm_ref[...], u_in)
            ub = u.astype(BF16)
            dyv = _permute_rows_f32(pm_ref[...], dy_in)
            dyb = dyv.astype(BF16)
            _tile_set(sr, _dot(ub, bre_ref[...]))
            _tile_set(si, _dot(ub, bim_ref[...]))
            er, ei, _, _ = _s5_forward_states(sr, si, lr, li, b_r, b_i, ptab)
            _tile_set(gr, _dot(dyb, cre_ref[...]))
            _tile_set(gi, -_dot(dyb, cim_ref[...]))
            zr, zi = _chunk_scans(gr, gi, lr, -li, True)
            ar, ai = _table_rows(ptab, S5_STEPS - 1, True)
            fr, fi = _entering_states(zr, zi, carry[0:1, :], carry[1:2, :], ar, ai, True)
            acc_r = jnp.zeros((S5_CHUNKS, S5_LANES), F32)
            acc_i = jnp.zeros((S5_CHUNKS, S5_LANES), F32)
            for j in range(S5_STEPS):
                qr, qi = _table_rows(ptab, S5_STEPS - 1 - j, True)
                g_r = _step_get(gr, j) + qr * fr - qi * fi
                g_i = _step_get(gi, j) + qr * fi + qi * fr
                _step_set(gr, j, g_r)
                _step_set(gi, j, g_i)
                p_r, p_i = (er, ei) if j == 0 else (_step_get(sr, j - 1), _step_get(si, j - 1))
                acc_r += g_r * p_r + g_i * p_i
                acc_i += g_i * p_r - g_r * p_i
            dlam_ref[0:1, :] += jnp.sum(acc_r, axis=0, keepdims=True)
            dlam_ref[1:2, :] += jnp.sum(acc_i, axis=0, keepdims=True)
            g_all_r = _tile_get(gr)
            g_all_i = _tile_get(gi)
            carry[0:1, :] = g_all_r[0:1, :]
            carry[1:2, :] = g_all_i[0:1, :]
            grb = g_all_r.astype(BF16)
            gib = g_all_i.astype(BF16)
            du = (_dot(grb, bret_ref[...]) + _dot(gib, bimt_ref[...]) + d_ref[...] * dyv).astype(BF16)
            dbre_ref[...] += _dot_tn(grb, ub)
            dbim_ref[...] += _dot_tn(gib, ub)
            dcre_ref[...] += _dot_tn(dyb, _tile_get(sr).astype(BF16))
            dcim_ref[...] -= _dot_tn(dyb, _tile_get(si).astype(BF16))
            dd_ref[...] += jnp.sum(dyv * u, axis=0, keepdims=True)
            return _dot(pmt_ref[...], du).astype(BF16)

        for s in reversed(range(S5_PER_STEP)):
            rows = slice(s * S5_TILE, (s + 1) * S5_TILE)
            du_ref[rows, :] = one_tile(u_ref[rows, :], dy_ref[rows, :], bound_ref[s, 0:1, :], bound_ref[s, 1:2, :])

    acc_ts = pl.BlockSpec((None, S5_LANES, LANE), lambda b, t: (b, 0, 0))
    acc_fs = pl.BlockSpec((None, LANE, S5_LANES), lambda b, t: (b, 0, 0))
    return _pcall(
        body, "s5_bwd", (S5_NBLK, nt // S5_PER_STEP),
        [sp["u"], sp["rows"], sp["bound"], sp["perm"], sp["perm"], sp["to_state"], sp["to_state"],
         sp["from_state"], sp["from_state"], sp["to_state"], sp["to_state"], sp["lam"], sp["d"]],
        [sp["rows"], acc_ts, acc_ts, acc_fs, acc_fs, sp["lam"], sp["d"]],
        [jax.ShapeDtypeStruct((T, SSM_WIDTH), BF16),
         jax.ShapeDtypeStruct((S5_NBLK, S5_LANES, LANE), F32),
         jax.ShapeDtypeStruct((S5_NBLK, S5_LANES, LANE), F32),
         jax.ShapeDtypeStruct((S5_NBLK, LANE, S5_LANES), F32),
         jax.ShapeDtypeStruct((S5_NBLK, LANE, S5_LANES), F32),
         jax.ShapeDtypeStruct((S5_NBLK, 2, S5_LANES), F32),
         jax.ShapeDtypeStruct((1, SSM_WIDTH), F32)],
        [pltpu.VMEM((2, S5_LANES), F32), pltpu.VMEM((2, S5_TILE, S5_LANES), F32)]
        + [pltpu.VMEM(S5_STATE_TILE, F32)] * 4,
        (proj, dy, bound, pm, pm_t, bre, bim, bre_t, bim_t, cre, cim, lam, d), carry)


def _ret_bwd(proj, cosf, sinf, mask, rowdec, kdec, gtb, gn, sblk, dyr):
    T = proj.shape[0]
    nb = T // RET_BLOCK
    sp = _ret_specs(T, True)

    def body(first, q_ref, k_ref, v_ref, g_ref, cos_ref, sin_ref, mask_ref, rd_ref, kd_ref, gtb_ref, gn_ref, sb_ref,
             dy_ref, dq_ref, dk_ref, dv_ref, dg_ref, dgn_ref, dst):
        if first:
            @pl.when(pl.program_id(1) == 0)
            def _():
                dst[...] = jnp.zeros_like(dst)
                dgn_ref[...] = jnp.zeros_like(dgn_ref)

        s_in = sb_ref[...]
        q, k, qb, kb, vb, pm, qd, o = _ret_common(q_ref, k_ref, v_ref, cos_ref, sin_ref, mask_ref, rd_ref, s_in)
        mu = jnp.mean(o, axis=-1, keepdims=True)
        oc = o - mu
        rstd = lax.rsqrt(jnp.mean(oc * oc, axis=-1, keepdims=True) + EPS)
        n = oc * rstd
        gt = g_ref[...]
        sg = _sigmoid(gt)
        sil = gt * sg
        gnv = gn_ref[...]
        dyv = dy_ref[...]
        dg_ref[...] = (dyv * (n * gnv) * (sg * (1.0 + gt * (1.0 - sg)))).astype(BF16)
        dgn_ref[...] += jnp.sum(dyv * sil * n, axis=0, keepdims=True)
        dn = dyv * sil * gnv
        do = rstd * (dn - jnp.mean(dn, axis=-1, keepdims=True) - n * jnp.mean(dn * n, axis=-1, keepdims=True))
        dob = do.astype(BF16)
        ds = dst[...]
        dsb = ds.astype(BF16)
        kd = kd_ref[...]
        rd = rd_ref[...]
        dv_ref[...] = (_dot_tn(pm, dob) + _dot((k * kd).astype(BF16), dsb)).astype(BF16)
        dpb = (_dot_nt(dob, vb) * mask_ref[...]).astype(BF16)
        dq = _dot(dpb, kb) + _dot_nt(dob, s_in.astype(BF16)) * rd
        dk = (_dot_tn(dpb, qb) + _dot_nt(vb, dsb) * kd) * (HEAD_DIM ** -0.5)
        dst[...] = gtb_ref[...] * ds + _dot_tn(qd, dob)
        c = cos_ref[...]
        s = sin_ref[...]
        dq_ref[...] = (dq * c + pltpu.roll(dq * s, HEAD_DIM // 2, 1)).astype(BF16)
        dk_ref[...] = (dk * c + pltpu.roll(dk * s, HEAD_DIM // 2, 1)).astype(BF16)

    oshape = jax.ShapeDtypeStruct((T, RET_WIDTH), BF16)
    ins = [sp[n] for n in ("q", "k", "v", "g", "tab", "tab", "mask", "dec", "dec", "gtb", "gn", "state", "rows")]
    outs = [sp["rows"], sp["rows"], sp["rows"], sp["rows"], sp["gn"]]
    return pl.pallas_call(
        _per_head(body, [kind for _, kind in ins + outs + [sp["scratch"]]], True), name="ret_bwd",
        grid=(RET_HEADS // RET_HPS, nb // RET_PER_STEP), in_specs=[s for s, _ in ins],
        out_specs=[s for s, _ in outs],
        out_shape=[oshape, oshape, oshape, oshape, jax.ShapeDtypeStruct((1, RET_WIDTH), F32)],
        scratch_shapes=[sp["scratch"][0]],
        compiler_params=_params(2),
    )(proj, proj, proj, proj, cosf, sinf, mask, rowdec, kdec, gtb, gn, sblk, dyr)


def _in_proj_bwd(dproj, w, x, r1, g, dx2, tm, carry=None):
    T = x.shape[0]

    def body(dp_ref, w_hbm, x_ref, r_ref, g_ref, dx2_ref, gx_ref, dg_ref, w_ref, sem):
        @pl.when(pl.program_id(0) == 0)
        def _():
            _load_resident(w_hbm, w_ref, sem)
            dg_ref[...] = jnp.zeros_like(dg_ref)

        dh = _dot_nt(dp_ref[:, 0:WIN_BLK], w_ref[0])
        for k in range(1, N_DEV):
            dh = dh + _dot_nt(dp_ref[:, k * WIN_BLK:(k + 1) * WIN_BLK], w_ref[k])
        dxn, dgt = _rms_bwd(dh, x_ref[...], r_ref[...], g_ref[...])
        gx_ref[...] = dx2_ref[...] + dxn
        dg_ref[...] += jnp.sum(dgt, axis=0, keepdims=True)

    full = pl.BlockSpec((tm, D_MODEL), lambda i: (i, 0))
    vec = pl.BlockSpec((1, D_MODEL), lambda i: (0, 0))
    return _pcall(
        body, "in_proj_bwd", (T // tm,),
        [pl.BlockSpec((tm, IN_WIDTH), lambda i: (i, 0)), ANY_SPEC,
         full, pl.BlockSpec((tm, 1), lambda i: (i, 0)), vec, full],
        [full, vec],
        [jax.ShapeDtypeStruct((T, D_MODEL), F32), jax.ShapeDtypeStruct((1, D_MODEL), F32)],
        [pltpu.VMEM(w.shape, w.dtype), pltpu.SemaphoreType.DMA], (dproj, w, x, r1, g, dx2), carry)


def _in_proj_wgrad(h, dproj, tk, carry=None):
    T = h.shape[0]
    nk = T // tk

    def body(h_ref, dp_ref, o_ref, acc):
        k = pl.program_id(1)

        @pl.when(k == 0)
        def _():
            acc[...] = jnp.zeros_like(acc)

        acc[...] += _dot_tn(h_ref[...], dp_ref[...])

        @pl.when(k == nk - 1)
        def _():
            o_ref[...] = acc[...].astype(BF16)

    return _pcall(
        body, "in_proj_wgrad", (N_DEV, nk),
        [pl.BlockSpec((tk, D_MODEL), lambda j, k: (k, 0)), pl.BlockSpec((tk, WIN_BLK), lambda j, k: (k, j))],
        [pl.BlockSpec((None, D_MODEL, WIN_BLK), lambda j, k: (j, 0, 0))],
        [jax.ShapeDtypeStruct((N_DEV, D_MODEL, WIN_BLK), BF16)],
        [pltpu.VMEM((D_MODEL, WIN_BLK), F32)], (h, dproj), carry)


def _rope_tables(T):
    half = HEAD_DIM // 2
    freqs = ROPE_BASE ** (-jnp.arange(half, dtype=F32) / half)
    ang = jnp.arange(T, dtype=F32)[:, None] * freqs[None, :]
    c = jnp.cos(ang)
    s = jnp.sin(ang)
    return jnp.concatenate([c, c], axis=1), jnp.concatenate([-s, s], axis=1)


def _retention_tables():
    hh = jnp.arange(RET_HEADS, dtype=F32)
    log_g = jnp.log1p(-(2.0 ** (-5.0 - hh)))[:, None, None]
    i = jnp.arange(RET_BLOCK)
    ci = (i // CHUNK)[:, None]
    cj = (i // CHUNK)[None, :]
    diff = (i[:, None] - i[None, :]).astype(F32)
    expo = jnp.where(ci == cj, jnp.abs(diff), diff)
    mask = jnp.where((cj <= ci)[None], jnp.exp(log_g * expo[None]), 0.0)
    r = jnp.arange(RET_BLOCK, dtype=F32)[None, :, None]
    ones = jnp.ones((1, 1, HEAD_DIM), F32)
    rowdec = jnp.exp(log_g * (r + 1.0)) * ones
    kdec = jnp.exp(log_g * (RET_BLOCK - 1.0 - r)) * ones
    gtb = jnp.exp(log_g * float(RET_BLOCK)) * ones
    return mask, rowdec, kdec, gtb


def _s5_discretise(a_re, a_im, log_dt, b_re, b_im):
    lam = lax.complex(a_re, a_im)
    dt = jnp.exp(log_dt)[:, None]
    lam_bar = jnp.exp(lam * dt)
    b_bar = ((lam_bar - 1.0) / lam)[..., None] * lax.complex(b_re, b_im)
    return jnp.real(lam_bar), jnp.imag(lam_bar), jnp.real(b_bar), jnp.imag(b_bar)


def _to_state_blockdiag(m):
    eye = jnp.eye(S5_GB, dtype=m.dtype)
    t = jnp.einsum("bgpc,gh->bgchp", m.reshape(S5_NBLK, S5_GB, SSM_STATE, SSM_GROUP), eye)
    return t.reshape(S5_NBLK, LANE, S5_LANES)


def _from_state_blockdiag(m):
    eye = jnp.eye(S5_GB, dtype=m.dtype)
    t = jnp.einsum("bgcp,gh->bgphc", m.reshape(S5_NBLK, S5_GB, SSM_GROUP, SSM_STATE), eye)
    return t.reshape(S5_NBLK, S5_LANES, LANE)


def _diag_of_state_major(acc):
    eye = jnp.eye(S5_GB, dtype=acc.dtype)
    t = acc.reshape(S5_NBLK, S5_GB, SSM_STATE, S5_GB, SSM_GROUP)
    return jnp.einsum("bgphc,gh->bgpc", t, eye).reshape(SSM_GROUPS, SSM_STATE, SSM_GROUP)


def _diag_of_channel_major(acc):
    eye = jnp.eye(S5_GB, dtype=acc.dtype)
    t = acc.reshape(S5_NBLK, S5_GB, SSM_GROUP, S5_GB, SSM_STATE)
    return jnp.einsum("bgchp,gh->bgcp", t, eye).reshape(SSM_GROUPS, SSM_GROUP, SSM_STATE)


SMALL_PARTIALS = (("ret_gn_g", 1024), ("lam_re", 4096), ("lam_im", 4096),
                  ("bbar_re", 65536), ("bbar_im", 65536), ("c_re", 65536), ("c_im", 65536),
                  ("ssm_d", 1024), ("b_glu", 1024), ("out_g", 1024), ("norm_ffn_g", 2048), ("norm_final_g", 2048))


def _forward_backward(x, tgt, shards, sm):
    T = x.shape[0]
    tm = min(1024, T)
    cosf, sinf = _rope_tables(T)
    mask, rowdec, kdec, gtb = _retention_tables()
    lbr, lbi, bbr, bbi = _s5_discretise(sm["ssm_a_re"], sm["ssm_a_im"], sm["ssm_log_dt"], sm["ssm_b_re"],
                                        sm["ssm_b_im"])
    bre = _to_state_blockdiag(bbr).astype(BF16)
    bim = _to_state_blockdiag(bbi).astype(BF16)
    cre_t = _from_state_blockdiag(sm["ssm_c_re"]).astype(BF16)
    cim_t = _from_state_blockdiag(sm["ssm_c_im"]).astype(BF16)
    bre_t = jnp.swapaxes(bre, 1, 2)
    bim_t = jnp.swapaxes(bim, 1, 2)
    cre = jnp.swapaxes(cre_t, 1, 2)
    cim = jnp.swapaxes(cim_t, 1, 2)
    lam = jnp.stack([lbr.reshape(S5_NBLK, S5_LANES), lbi.reshape(S5_NBLK, S5_LANES)], axis=1)
    pm = _step_major_permutation()
    pm_t = pm.T
    row = lambda v: v.reshape(1, -1)
    g_mix, g_ffn, g_fin = row(sm["norm_mix_g"]), row(sm["norm_ffn_g"]), row(sm["norm_final_g"])
    gn, dsk, bglu, og = row(sm["ret_gn_g"]), row(sm["ssm_d"]), row(sm["ssm_b_glu"]), row(sm["ssm_out_g"])

    (w_in,) = _exchange_call("weight_gather", [shards["w_in"]], True, via_sibling=True)
    proj, h1, r1, w_gate = _in_proj_fwd(x, g_mix, w_in, 256, _Exchange([shards["w_gate"]], True, via_sibling=True))
    y_ret, sblk, w_glu, w_out = _ret_fwd(proj, cosf, sinf, mask, rowdec, kdec, gtb, gn,
                                         _Exchange([shards["ssm_w_glu"], shards["w_out"]], True, via_sibling=True))
    w_glu = w_glu.reshape(SSM_WIDTH, SSM_WIDTH)
    w_out = w_out.reshape(D_MODEL, D_MODEL)
    y_s5, bound, w_up = _s5_fwd(proj, pm, pm_t, bre, bim, cre_t, cim_t, lam, dsk, _Exchange([shards["w_up"]], True))
    z, y_ssm, r_ssm = _glu_fwd(y_s5, w_glu, bglu, og, 256)
    x2, h2, r2 = _out_proj_fwd(x, y_ret, y_ssm, w_out, g_ffn, 256)
    a, b, f, w_down = _ffn_up(h2, w_gate, w_up, tm, _Exchange([shards["w_down"]], True))
    dx3, dx3b, loss8, dg_fin = _ffn_down_loss(f, w_down, x2, tgt, g_fin, 256)

    landed = {}
    da, db = _ffn_bwd_act(dx3b, w_down, a, b, tm)
    dw_down = _ffn_wgrad_down(f, dx3b, tm)
    dw_gate, dw_up, landed["w_down"] = _ffn_wgrad_up(h2, da, db, tm, _Exchange([dw_down], False))
    dh2, landed["w_gate"] = _ffn_bwd_in(da, db, w_gate, w_up, min(1024, T), _Exchange([dw_gate], False))
    dx2, dx2b, dg_ffn, dy_ret, dy_ssm = _out_proj_bwd(dh2, x2, r2, g_ffn, dx3, w_out, 256)
    dw_out = jnp.concatenate([_wgrad_rows("out_proj_wgrad_ret", y_ret, dx2b, tm),
                              _wgrad_rows("out_proj_wgrad_ssm", y_ssm, dx2b, tm)], axis=0)
    dy_s5, dw_glu, db_glu, dog = _glu_bwd(y_s5, z, r_ssm, dy_ssm, w_glu, og, 256)
    du, dbre, dbim, dcre, dcim, dlam, dd, landed["w_up"] = _s5_bwd(
        proj, dy_s5, bound, pm, pm_t, bre, bim, bre_t, bim_t, cre, cim, lam, dsk, _Exchange([dw_up], False))
    dq, dk, dv, dgate, dgn = _ret_bwd(proj, cosf, sinf, mask, rowdec, kdec, gtb, gn, sblk, dy_ret)
    dproj = jnp.concatenate([dq, dk, dv, dgate, du], axis=1)
    small = dict(ret_gn_g=dgn, lam_re=dlam[:, 0], lam_im=dlam[:, 1],
                 bbar_re=_diag_of_state_major(dbre), bbar_im=_diag_of_state_major(dbim),
                 c_re=_diag_of_channel_major(dcre), c_im=_diag_of_channel_major(dcim),
                 ssm_d=dd, b_glu=db_glu, out_g=dog, norm_ffn_g=dg_ffn, norm_final_g=dg_fin)
    packed = _pack([small[n] for n, _ in SMALL_PARTIALS])
    dw_in, landed["w_out"], landed["ssm_w_glu"], small_landed = _in_proj_wgrad(
        h1, dproj, tm, _Exchange([dw_out.reshape(N_DEV, D_MODEL // N_DEV, D_MODEL),
                                  dw_glu.astype(BF16).reshape(N_DEV, SSM_WIDTH // N_DEV, SSM_WIDTH), packed],
                                 [False, False, True]))
    grad_x, dg_mix, landed["w_in"] = _in_proj_bwd(dproj, w_in, x, r1, g_mix, dx2, 256, _Exchange([dw_in], False))
    (mix_landed,) = _exchange_call("mix_gain_grad_gather", [_pack([dg_mix])], True)
    summed = dict(zip([n for n, _ in SMALL_PARTIALS],
                      _unpack(_sum_partials("small_grad_sum", small_landed), [(sz,) for _, sz in SMALL_PARTIALS])))
    summed["norm_mix_g"] = _sum_partials("mix_gain_grad_sum", mix_landed).reshape(-1)
    return loss8[0, 0], grad_x, landed, summed


def _small_grads(summed, sm):
    _, vjp = jax.vjp(_s5_discretise, sm["ssm_a_re"], sm["ssm_a_im"], sm["ssm_log_dt"], sm["ssm_b_re"], sm["ssm_b_im"])
    gp = (SSM_GROUPS, SSM_STATE)
    da_re, da_im, dlog_dt, db_re, db_im = vjp((summed["lam_re"].reshape(gp), summed["lam_im"].reshape(gp),
                                               summed["bbar_re"].reshape(gp + (SSM_GROUP,)),
                                               summed["bbar_im"].reshape(gp + (SSM_GROUP,))))
    return dict(norm_mix_g=summed["norm_mix_g"], ret_gn_g=summed["ret_gn_g"], ssm_a_re=da_re, ssm_a_im=da_im,
                ssm_log_dt=dlog_dt, ssm_b_re=db_re, ssm_b_im=db_im,
                ssm_c_re=summed["c_re"].reshape(SSM_GROUPS, SSM_GROUP, SSM_STATE),
                ssm_c_im=summed["c_im"].reshape(SSM_GROUPS, SSM_GROUP, SSM_STATE),
                ssm_d=summed["ssm_d"], ssm_b_glu=summed["b_glu"], ssm_out_g=summed["out_g"],
                norm_ffn_g=summed["norm_ffn_g"], norm_final_g=summed["norm_final_g"])


def _adamw_math(w, g, m, v):
    m2 = ADAM_B1 * m + (1.0 - ADAM_B1) * g
    v2 = ADAM_B2 * v + (1.0 - ADAM_B2) * (g * g)
    delta = -ADAM_LR * ((m2 / ADAM_BC1) / (jnp.sqrt(v2 / ADAM_BC2) + ADAM_EPS) + ADAM_WD * w)
    return delta, m2, v2


def _adamw_shard(name, parts, w, m, v, tr):
    rows, cols = w.shape

    def body(p_ref, w_ref, m_ref, v_ref, g_ref, d_ref, m2_ref, v2_ref):
        g = p_ref[0].astype(F32)
        for s in range(1, N_DEV):
            g = g + p_ref[s].astype(F32)
        d, m2, v2 = _adamw_math(w_ref[...], g, m_ref[...], v_ref[...])
        g_ref[...] = g
        d_ref[...] = d
        m2_ref[...] = m2
        v2_ref[...] = v2

    blk = pl.BlockSpec((tr, cols), lambda i: (i, 0))
    oshape = jax.ShapeDtypeStruct((rows, cols), F32)
    return pl.pallas_call(
        body, name=name, grid=(rows // tr,),
        in_specs=[pl.BlockSpec((N_DEV, tr, cols), lambda i: (0, i, 0)), blk, blk, blk],
        out_specs=[blk, blk, blk, blk], out_shape=[oshape] * 4,
        compiler_params=_params(1),
    )(parts, w, m, v)


def _sum_partials(name, parts):
    rows = parts.shape[1]

    def body(p_ref, o_ref):
        g = p_ref[0]
        for s in range(1, N_DEV):
            g = g + p_ref[s]
        o_ref[...] = g

    return pl.pallas_call(
        body, name=name, grid=(1,),
        in_specs=[pl.BlockSpec((N_DEV, rows, LANE), lambda i: (0, 0, 0))],
        out_specs=pl.BlockSpec((rows, LANE), lambda i: (0, 0)),
        out_shape=jax.ShapeDtypeStruct((rows, LANE), F32),
        compiler_params=_params(1),
    )(parts)


def _adamw_small(ws, gs, ms, vs):
    n = len(ws)

    def body(*refs):
        for i in range(n):
            w_ref, g_ref, m_ref, v_ref = (refs[k * n + i] for k in range(4))
            d_ref, m2_ref, v2_ref = (refs[(4 + k) * n + i] for k in range(3))
            d, m2, v2 = _adamw_math(w_ref[...], g_ref[...], m_ref[...], v_ref[...])
            d_ref[...] = d
            m2_ref[...] = m2
            v2_ref[...] = v2

    vmem = pl.BlockSpec(memory_space=pltpu.VMEM)
    out = pl.pallas_call(
        body, name="adamw_small", in_specs=[vmem] * (4 * n), out_specs=[vmem] * (3 * n),
        out_shape=[jax.ShapeDtypeStruct(w.shape, F32) for w in ws] * 3,
        compiler_params=pltpu.CompilerParams(vmem_limit_bytes=VMEM_LIMIT),
    )(*ws, *gs, *ms, *vs)
    return out[:n], out[n:2 * n], out[2 * n:]


def _pack(arrays):
    parts = [a.reshape(-1, LANE) for a in arrays]
    assert all(p.shape[0] % 8 == 0 for p in parts)
    return parts[0] if len(parts) == 1 else jnp.concatenate(parts, axis=0)


def _unpack(packed, shapes):
    flat = packed.reshape(-1)
    out, off = [], 0
    for shp in shapes:
        n = math.prod(shp)
        out.append(flat[off:off + n].reshape(shp))
        off += n + ((-n) % LANE)
    return out


WEIGHTS = ("norm_mix_g", "w_in", "ret_gn_g", "ssm_a_re", "ssm_a_im", "ssm_log_dt", "ssm_b_re", "ssm_b_im",
           "ssm_c_re", "ssm_c_im", "ssm_d", "ssm_w_glu", "ssm_b_glu", "ssm_out_g", "w_out", "norm_ffn_g", "w_gate",
           "w_up", "w_down", "norm_final_g")
BIG = ("w_in", "ssm_w_glu", "w_out", "w_gate", "w_up", "w_down")
SMALL = tuple(n for n in WEIGHTS if n not in BIG)
ADAM_ROWS = {"w_in": 256, "ssm_w_glu": 128, "w_out": 128, "w_gate": 256, "w_up": 256, "w_down": 176}


def kernel(x, norm_mix_g, w_in, ret_gn_g, ssm_a_re, ssm_a_im, ssm_log_dt, ssm_b_re, ssm_b_im, ssm_c_re, ssm_c_im, ssm_d, ssm_w_glu, ssm_b_glu, ssm_out_g, w_out, norm_ffn_g, w_gate, w_up, w_down, norm_final_g, loss_target, m_norm_mix_g, m_w_in, m_ret_gn_g, m_ssm_a_re, m_ssm_a_im, m_ssm_log_dt, m_ssm_b_re, m_ssm_b_im, m_ssm_c_re, m_ssm_c_im, m_ssm_d, m_ssm_w_glu, m_ssm_b_glu, m_ssm_out_g, m_w_out, m_norm_ffn_g, m_w_gate, m_w_up, m_w_down, m_norm_final_g, v_norm_mix_g, v_w_in, v_ret_gn_g, v_ssm_a_re, v_ssm_a_im, v_ssm_log_dt, v_ssm_b_re, v_ssm_b_im, v_ssm_c_re, v_ssm_c_im, v_ssm_d, v_ssm_w_glu, v_ssm_b_glu, v_ssm_out_g, v_w_out, v_norm_ffn_g, v_w_gate, v_w_up, v_w_down, v_norm_final_g):
    given = dict(locals())
    w = {n: given[n] for n in WEIGHTS}
    m = {n: given["m_" + n] for n in WEIGHTS}
    v = {n: given["v_" + n] for n in WEIGHTS}
    drop = lambda n, a: a if n == "norm_final_g" else a[0]
    w0 = {n: drop(n, w[n]) for n in WEIGHTS}
    m0 = {n: drop(n, m[n]) for n in WEIGHTS}
    v0 = {n: drop(n, v[n]) for n in WEIGHTS}

    sm = {n: w0[n] for n in SMALL}
    shards = {n: w0[n].astype(BF16) for n in BIG}
    loss_local, grad_x, landed, summed = _forward_backward(x[0], loss_target[0], shards, sm)
    loss = lax.psum(loss_local, MESH_AXES)
    gsmall = _small_grads(summed, sm)

    grads, delta, new_m, new_v = {}, {}, {}, {}
    for n in BIG:
        g, d, m2, v2 = _adamw_shard("adamw_" + n, landed[n], w0[n], m0[n], v0[n], ADAM_ROWS[n])
        grads[n], delta[n], new_m[n], new_v[n] = g, d, m2, v2
    as_given = lambda n, a: a.reshape(1, -1) if n == "norm_final_g" else a.reshape(w[n].shape)
    gs = [as_given(n, gsmall[n]) for n in SMALL]
    ds, m2s, v2s = _adamw_small([as_given(n, w[n]) for n in SMALL], gs, [as_given(n, m[n]) for n in SMALL],
                                [as_given(n, v[n]) for n in SMALL])
    for n, g, d, m2, v2 in zip(SMALL, gs, ds, m2s, v2s):
        grads[n], delta[n], new_m[n], new_v[n] = g, d, m2, v2

    lift = lambda n, a: a.reshape(w[n].shape)
    return (loss, grad_x[None], *[lift(n, grads[n]) for n in WEIGHTS], *[lift(n, delta[n]) for n in WEIGHTS],
            *[lift(n, new_m[n]) for n in WEIGHTS], *[lift(n, new_v[n]) for n in WEIGHTS])
```

```python
import functools
import math

import jax
import jax.numpy as jnp
from jax import lax
from jax.experimental import pallas as pl
from jax.experimental.pallas import tpu as pltpu

F32 = jnp.float32
BF16 = jnp.bfloat16

D_MODEL = 2048
RET_WIDTH = 1024
RET_HEADS = 8
HEAD_DIM = 128
CHUNK = 64
SSM_WIDTH = 1024
SSM_GROUP = 16
SSM_GROUPS = 64
SSM_STATE = 64
D_FF = 5632
IN_WIDTH = 5120
ROPE_BASE = 10000.0
EPS = 1e-6
N_DEV = 8
MESH_AXES = ("x", "y", "c")

WIN_BLK = IN_WIDTH // N_DEV
FF_BLK = D_FF // N_DEV
RET_BLOCK = 256
RET_HPS = 4
RET_PER_STEP = 4
S5_TILE = 256
S5_CHUNKS = 8
S5_STEPS = S5_TILE // S5_CHUNKS
S5_PER_STEP = 16
S5_GB = 8
S5_NBLK = SSM_GROUPS // S5_GB
S5_LANES = S5_GB * SSM_STATE
LANE = 128

ADAM_LR = 0.001
ADAM_B1 = 0.9
ADAM_B2 = 0.999
ADAM_EPS = 1e-08
ADAM_WD = 0.01
ADAM_STEP = 10
ADAM_BC1 = 1.0 - ADAM_B1 ** ADAM_STEP
ADAM_BC2 = 1.0 - ADAM_B2 ** ADAM_STEP

VMEM_LIMIT = 56 * 1024 * 1024

NT = (((1,), (1,)), ((), ()))
TN = (((0,), (0,)), ((), ()))


def _params(n_grid):
    return pltpu.CompilerParams(dimension_semantics=("arbitrary",) * n_grid, vmem_limit_bytes=VMEM_LIMIT)


def _dot(a, b):
    return jnp.dot(a, b, preferred_element_type=F32)


def _dot_nt(a, b):
    return lax.dot_general(a, b, NT, preferred_element_type=F32)


def _dot_tn(a, b):
    return lax.dot_general(a, b, TN, preferred_element_type=F32)


def _sigmoid(x):
    return 1.0 / (1.0 + jnp.exp(-x))


_GELU_C = math.sqrt(2.0 / math.pi)
_GELU_A = 0.044715


def _gelu(x):
    t = jnp.tanh(_GELU_C * (x + _GELU_A * x * x * x))
    return 0.5 * x * (1.0 + t)


def _gelu_and_grad(x):
    t = jnp.tanh(_GELU_C * (x + _GELU_A * x * x * x))
    g = 0.5 * (1.0 + t) + 0.5 * x * (1.0 - t * t) * _GELU_C * (1.0 + 3.0 * _GELU_A * x * x)
    return 0.5 * x * (1.0 + t), g


def _rms_bwd(dy, x, r, g):
    w = dy * g
    dx = r * w - x * (r * r * r) * jnp.mean(w * x, axis=-1, keepdims=True)
    return dx, dy * x * r


HBM_SPEC = pl.BlockSpec(memory_space=pltpu.HBM)
ANY_SPEC = pl.BlockSpec(memory_space=pl.ANY)


def _load_resident(src_hbm, dst_vmem, sem):
    cp = pltpu.make_async_copy(src_hbm, dst_vmem, sem)
    cp.start()
    cp.wait()


def _my_block():
    return 4 * lax.axis_index("x") + 2 * lax.axis_index("y") + lax.axis_index("c")


def _peer(k):
    px = lax.axis_index("x") ^ ((k >> 2) & 1)
    py = lax.axis_index("y") ^ ((k >> 1) & 1)
    pc = lax.axis_index("c") ^ (k & 1)
    return (px, py, pc), 4 * px + 2 * py + pc


class _Exchange:
    def __init__(self, payloads, gather, via_sibling=False):
        self.payloads = list(payloads)
        self.n = len(self.payloads)
        self.gather = [gather] * self.n if isinstance(gather, bool) else list(gather)
        self.via_sibling = via_sibling
        assert not via_sibling or all(self.gather)

    def out_shape(self):
        return [jax.ShapeDtypeStruct(((N_DEV,) if g else ()) + p.shape, p.dtype)
                for p, g in zip(self.payloads, self.gather)]

    def scratch_shapes(self):
        return [pltpu.SemaphoreType.DMA((self.n, N_DEV - 1)), pltpu.SemaphoreType.DMA((self.n, N_DEV - 1)),
                pltpu.SemaphoreType.DMA((self.n,))]

    def _copies(self, ins, outs, sems, incoming):
        send_sems, recv_sems, local_sems = sems
        me = _my_block()
        src_of = lambda i, blk: ins[i] if self.gather[i] else ins[i].at[blk]
        local, remote = [], []
        for i in range(self.n):
            if not incoming:
                local.append(pltpu.make_async_copy(src_of(i, me), outs[i].at[me], local_sems.at[i]))
            for k in range(1, N_DEV):
                dev, blk = _peer(k)
                src, dst = (outs[i].at[blk], outs[i].at[blk]) if incoming else (src_of(i, blk), outs[i].at[me])
                remote.append(pltpu.make_async_remote_copy(
                    src_ref=src, dst_ref=dst, send_sem=send_sems.at[i, k - 1], recv_sem=recv_sems.at[i, k - 1],
                    device_id=dev, device_id_type=pl.DeviceIdType.MESH))
        return local, remote

    def _copy(self, i, k, outs, sems, src, dst_blk, to_k):
        send_sems, recv_sems, _ = sems
        return pltpu.make_async_remote_copy(
            src_ref=src, dst_ref=outs[i].at[dst_blk], send_sem=send_sems.at[i, k - 1], recv_sem=recv_sems.at[i, k - 1],
            device_id=_peer(to_k)[0], device_id_type=pl.DeviceIdType.MESH)

    FIRST_HOPS = (1, 2, 4, 6)
    FROM_CHIPS = (2, 4, 6)

    def start(self, ins, outs, sems):
        if not self.via_sibling:
            local, sends = self._copies(ins, outs, sems, False)
            for cp in local + sends:
                cp.start()
            return
        me = _my_block()
        for i in range(self.n):
            pltpu.make_async_copy(ins[i], outs[i].at[me], sems[2].at[i]).start()
            for k in self.FIRST_HOPS:
                self._copy(i, k, outs, sems, ins[i], me, k).start()

    def wait(self, ins, outs, sems):
        if not self.via_sibling:
            for cp in self._copies(ins, outs, sems, True)[1]:
                cp.wait_recv()
            local, sends = self._copies(ins, outs, sems, False)
            for cp in sends:
                cp.wait_send()
            for cp in local:
                cp.wait()
            return
        me = _my_block()
        landed = lambda i, k: self._copy(i, k, outs, sems, outs[i].at[_peer(k)[1]], _peer(k)[1], k)
        for i in range(self.n):
            for s in self.FROM_CHIPS:
                landed(i, s).wait_recv()
                self._copy(i, s ^ 1, outs, sems, outs[i].at[_peer(s)[1]], _peer(s)[1], 1).start()
        for i in range(self.n):
            for k in (1, 3, 5, 7):
                landed(i, k).wait_recv()
            for k in self.FIRST_HOPS:
                self._copy(i, k, outs, sems, ins[i], me, k).wait_send()
            for s in self.FROM_CHIPS:
                self._copy(i, s ^ 1, outs, sems, outs[i].at[_peer(s)[1]], _peer(s)[1], 1).wait_send()
            pltpu.make_async_copy(ins[i], outs[i].at[me], sems[2].at[i]).wait()


def _pcall(body, name, grid, in_specs, out_specs, out_shape, scratch_shapes, args, carry=None):
    n_in, n_out, n_scr = len(in_specs), len(out_specs), len(scratch_shapes)
    if carry is None:
        return pl.pallas_call(body, name=name, grid=grid, in_specs=in_specs, out_specs=out_specs, out_shape=out_shape,
                              scratch_shapes=scratch_shapes, compiler_params=_params(len(grid)))(*args)
    nx = carry.n

    def wrapped(*refs):
        cin, xin = refs[:n_in], refs[n_in:n_in + nx]
        cout, xout = refs[n_in + nx:n_in + nx + n_out], refs[n_in + nx + n_out:n_in + 2 * nx + n_out]
        rest = refs[n_in + 2 * nx + n_out:]
        cscr, sems = rest[:n_scr], rest[n_scr:]
        first = functools.reduce(jnp.logical_and, [pl.program_id(a) == 0 for a in range(len(grid))])
        last = functools.reduce(jnp.logical_and, [pl.program_id(a) == grid[a] - 1 for a in range(len(grid))])

        @pl.when(first)
        def _():
            carry.start(xin, xout, sems)

        body(*cin, *cout, *cscr)

        @pl.when(last)
        def _():
            carry.wait(xin, xout, sems)

    return pl.pallas_call(
        wrapped, name=name, grid=grid, in_specs=list(in_specs) + [HBM_SPEC] * nx,
        out_specs=list(out_specs) + [HBM_SPEC] * nx, out_shape=list(out_shape) + carry.out_shape(),
        scratch_shapes=list(scratch_shapes) + carry.scratch_shapes(), compiler_params=_params(len(grid)),
    )(*args, *carry.payloads)


def _exchange_call(name, payloads, gather, via_sibling=False):
    ex = _Exchange(payloads, gather, via_sibling)

    def body(*refs):
        ins, outs, sems = refs[:ex.n], refs[ex.n:2 * ex.n], refs[2 * ex.n:]
        ex.start(ins, outs, sems)
        ex.wait(ins, outs, sems)

    return pl.pallas_call(body, name=name, in_specs=[HBM_SPEC] * ex.n, out_specs=[HBM_SPEC] * ex.n,
                          out_shape=ex.out_shape(), scratch_shapes=ex.scratch_shapes())(*ex.payloads)


def _in_proj_fwd(x, g, w, tm, carry=None):
    T = x.shape[0]

    def body(x_ref, g_ref, w_hbm, proj_ref, h_ref, r_ref, w_ref, sem):
        @pl.when(pl.program_id(0) == 0)
        def _():
            _load_resident(w_hbm, w_ref, sem)

        xf = x_ref[...]
        r = lax.rsqrt(jnp.mean(xf * xf, axis=-1, keepdims=True) + EPS)
        h = (xf * r * g_ref[...]).astype(BF16)
        h_ref[...] = h
        r_ref[...] = r
        for j in range(N_DEV):
            proj_ref[:, j * WIN_BLK:(j + 1) * WIN_BLK] = _dot(h, w_ref[j])

    return _pcall(
        body, "in_proj_fwd", (T // tm,),
        [pl.BlockSpec((tm, D_MODEL), lambda i: (i, 0)), pl.BlockSpec((1, D_MODEL), lambda i: (0, 0)), ANY_SPEC],
        [pl.BlockSpec((tm, IN_WIDTH), lambda i: (i, 0)),
         pl.BlockSpec((tm, D_MODEL), lambda i: (i, 0)),
         pl.BlockSpec((tm, 1), lambda i: (i, 0))],
        [jax.ShapeDtypeStruct((T, IN_WIDTH), F32),
         jax.ShapeDtypeStruct((T, D_MODEL), BF16),
         jax.ShapeDtypeStruct((T, 1), F32)],
        [pltpu.VMEM(w.shape, w.dtype), pltpu.SemaphoreType.DMA], (x, g, w), carry)


def _ret_common(q_ref, k_ref, v_ref, cos_ref, sin_ref, mask_ref, rd_ref, sin_state):
    c = cos_ref[...]
    s = sin_ref[...]
    q = q_ref[...]
    q = q * c + pltpu.roll(q, HEAD_DIM // 2, 1) * s
    k = k_ref[...]
    k = (k * c + pltpu.roll(k, HEAD_DIM // 2, 1) * s) * (HEAD_DIM ** -0.5)
    qb = q.astype(BF16)
    kb = k.astype(BF16)
    vb = v_ref[...].astype(BF16)
    pm = (_dot_nt(qb, kb) * mask_ref[...]).astype(BF16)
    qd = (q * rd_ref[...]).astype(BF16)
    o = _dot(pm, vb) + _dot(qd, sin_state.astype(BF16))
    return q, k, qb, kb, vb, pm, qd, o


def _ret_specs(T, rev):
    rows = RET_BLOCK * RET_PER_STEP
    nb = T // rows
    groups = RET_HEADS // RET_HPS
    wide = RET_HPS * HEAD_DIM
    blk = (lambda b: nb - 1 - b) if rev else (lambda b: b)
    col = lambda piece: (pl.BlockSpec((rows, wide), lambda h, b: (blk(b), piece * groups + h)), "rows_lane")
    return dict(
        q=col(0), k=col(1), v=col(2), g=col(3),
        tab=(pl.BlockSpec((rows, HEAD_DIM), lambda h, b: (blk(b), 0)), "rows"),
        mask=(pl.BlockSpec((RET_HPS, RET_BLOCK, RET_BLOCK), lambda h, b: (h, 0, 0)), "lead"),
        dec=(pl.BlockSpec((RET_HPS, RET_BLOCK, HEAD_DIM), lambda h, b: (h, 0, 0)), "lead"),
        gtb=(pl.BlockSpec((RET_HPS, 1, HEAD_DIM), lambda h, b: (h, 0, 0)), "lead"),
        gn=(pl.BlockSpec((1, wide), lambda h, b: (0, h)), "lane"),
        state=(pl.BlockSpec((RET_HPS, RET_PER_STEP, HEAD_DIM, HEAD_DIM), lambda h, b: (h, blk(b), 0, 0)), "state"),
        rows=(pl.BlockSpec((rows, wide), lambda h, b: (blk(b), h)), "rows_lane"),
        scratch=(pltpu.VMEM((RET_HPS, HEAD_DIM, HEAD_DIM), F32), "lead"),
    )


def _per_head(head_body, kinds, rev):
    def body(*refs):
        order = list(reversed(range(RET_PER_STEP))) if rev else list(range(RET_PER_STEP))
        for hh in range(RET_HPS):
            lanes = slice(hh * HEAD_DIM, (hh + 1) * HEAD_DIM)
            for s in order:
                rows = slice(s * RET_BLOCK, (s + 1) * RET_BLOCK)
                cut = {"rows_lane": lambda r: r.at[rows, lanes], "rows": lambda r: r.at[rows, :],
                       "lane": lambda r: r.at[:, lanes], "lead": lambda r: r.at[hh], "state": lambda r: r.at[hh, s]}
                head_body(s == order[0], *[cut[kind](ref) for ref, kind in zip(refs, kinds)])
    return body


def _ret_fwd(proj, cosf, sinf, mask, rowdec, kdec, gtb, gn, carry=None):
    T = proj.shape[0]
    nb = T // RET_BLOCK
    sp = _ret_specs(T, False)

    def body(first, q_ref, k_ref, v_ref, g_ref, cos_ref, sin_ref, mask_ref, rd_ref, kd_ref, gtb_ref, gn_ref,
             y_ref, sb_ref, st):
        if first:
            @pl.when(pl.program_id(1) == 0)
            def _():
                st[...] = jnp.zeros_like(st)
        s_in = st[...]
        sb_ref[...] = s_in
        q, k, qb, kb, vb, pm, qd, o = _ret_common(q_ref, k_ref, v_ref, cos_ref, sin_ref, mask_ref, rd_ref, s_in)
        st[...] = gtb_ref[...] * s_in + _dot_tn((k * kd_ref[...]).astype(BF16), vb)
        mu = jnp.mean(o, axis=-1, keepdims=True)
        oc = o - mu
        n = oc * lax.rsqrt(jnp.mean(oc * oc, axis=-1, keepdims=True) + EPS)
        gt = g_ref[...]
        y_ref[...] = (gt * _sigmoid(gt) * (n * gn_ref[...])).astype(BF16)

    ins = [sp[n] for n in ("q", "k", "v", "g", "tab", "tab", "mask", "dec", "dec", "gtb", "gn")]
    outs = [sp["rows"], sp["state"]]
    return _pcall(
        _per_head(body, [kind for _, kind in ins + outs + [sp["scratch"]]], False), "ret_fwd",
        (RET_HEADS // RET_HPS, nb // RET_PER_STEP), [s for s, _ in ins], [s for s, _ in outs],
        [jax.ShapeDtypeStruct((T, RET_WIDTH), BF16),
         jax.ShapeDtypeStruct((RET_HEADS, nb, HEAD_DIM, HEAD_DIM), F32)],
        [sp["scratch"][0]],
        (proj, proj, proj, proj, cosf, sinf, mask, rowdec, kdec, gtb, gn), carry)


def _scan(re, im, ar, ai, reverse):
    n = re.shape[0]
    row = lax.broadcasted_iota(jnp.int32, re.shape, 0)
    s = 1
    while s < n:
        if reverse:
            keep = row < n - s
            sr = jnp.where(keep, pltpu.roll(re, n - s, 0), 0.0)
            si = jnp.where(keep, pltpu.roll(im, n - s, 0), 0.0)
        else:
            keep = row >= s
            sr = jnp.where(keep, pltpu.roll(re, s, 0), 0.0)
            si = jnp.where(keep, pltpu.roll(im, s, 0), 0.0)
        re, im = re + ar * sr - ai * si, im + ar * si + ai * sr
        ar, ai = ar * ar - ai * ai, 2.0 * ar * ai
        s *= 2
    return re, im


S5_STATE_TILE = (S5_TILE, S5_LANES)


def _step_major_permutation():
    r = jnp.arange(S5_TILE)
    t_of_row = (r % S5_CHUNKS) * S5_STEPS + r // S5_CHUNKS
    return (t_of_row[:, None] == r[None, :]).astype(BF16)


def _permute_rows_f32(pm, x):
    hi = x.astype(BF16)
    rest = x - hi.astype(F32)
    mid = rest.astype(BF16)
    lo = (rest - mid.astype(F32)).astype(BF16)
    return _dot(pm, hi) + _dot(pm, mid) + _dot(pm, lo)


def _step_get(ref, j):
    return ref[j * S5_CHUNKS:(j + 1) * S5_CHUNKS, :]


def _step_set(ref, j, val):
    ref[j * S5_CHUNKS:(j + 1) * S5_CHUNKS, :] = val


def _tile_get(ref):
    return ref[...]


def _tile_set(ref, val):
    ref[...] = val


def _fill_power_table(ptab, lr, li):
    shape = (S5_CHUNKS, S5_LANES)
    lrb = jnp.broadcast_to(lr, shape)
    lib = jnp.broadcast_to(li, shape)
    pr, pi_ = lrb, lib
    for j in range(S5_STEPS):
        ptab[0, j * S5_CHUNKS:(j + 1) * S5_CHUNKS, :] = pr
        ptab[1, j * S5_CHUNKS:(j + 1) * S5_CHUNKS, :] = pi_
        pr, pi_ = lrb * pr - lib * pi_, lrb * pi_ + lib * pr


def _chunk_scans(xr, xi, lr, li, reverse):
    shape = (S5_CHUNKS, S5_LANES)
    lrb = jnp.broadcast_to(lr, shape)
    lib = jnp.broadcast_to(li, shape)
    sr = si = None
    for j in (range(S5_STEPS - 1, -1, -1) if reverse else range(S5_STEPS)):
        vr = _step_get(xr, j)
        vi = _step_get(xi, j)
        if sr is not None:
            vr, vi = vr + lrb * sr - lib * si, vi + lrb * si + lib * sr
            _step_set(xr, j, vr)
            _step_set(xi, j, vi)
        sr, si = vr, vi
    return sr, si


def _entering_states(zr, zi, cr, ci, ar, ai, reverse):
    shape = (S5_CHUNKS, S5_LANES)
    row = lax.broadcasted_iota(jnp.int32, shape, 0)
    if reverse:
        edge, shift = row == S5_CHUNKS - 1, S5_CHUNKS - 1
    else:
        edge, shift = row == 0, 1
    wr = jnp.where(edge, jnp.broadcast_to(cr, shape), pltpu.roll(zr, shift, 0))
    wi = jnp.where(edge, jnp.broadcast_to(ci, shape), pltpu.roll(zi, shift, 0))
    return _scan(wr, wi, ar, ai, reverse)


def _table_rows(ptab, j, conj):
    pr = ptab[0, j * S5_CHUNKS:(j + 1) * S5_CHUNKS, :]
    pi_ = ptab[1, j * S5_CHUNKS:(j + 1) * S5_CHUNKS, :]
    return pr, (-pi_ if conj else pi_)


def _s5_forward_states(xr, xi, lr, li, cr, ci, ptab):
    zr, zi = _chunk_scans(xr, xi, lr, li, False)
    ar, ai = _table_rows(ptab, S5_STEPS - 1, False)
    er, ei = _entering_states(zr, zi, cr, ci, ar, ai, False)
    for j in range(S5_STEPS):
        pr, pi_ = _table_rows(ptab, j, False)
        _step_set(xr, j, _step_get(xr, j) + pr * er - pi_ * ei)
        _step_set(xi, j, _step_get(xi, j) + pr * ei + pi_ * er)
    last = S5_CHUNKS - 1
    end_r = (ar * er - ai * ei + zr)[last:last + 1, :]
    end_i = (ar * ei + ai * er + zi)[last:last + 1, :]
    return er, ei, end_r, end_i


def _s5_specs(T, rev):
    rows = S5_TILE * S5_PER_STEP
    nt = T // rows
    tt = (lambda t: nt - 1 - t) if rev else (lambda t: t)
    return dict(
        u=pl.BlockSpec((rows, LANE), lambda b, t: (tt(t), 4 * RET_HEADS + b)),
        rows=pl.BlockSpec((rows, LANE), lambda b, t: (tt(t), b)),
        to_state=pl.BlockSpec((None, LANE, S5_LANES), lambda b, t: (b, 0, 0)),
        from_state=pl.BlockSpec((None, S5_LANES, LANE), lambda b, t: (b, 0, 0)),
        lam=pl.BlockSpec((None, 2, S5_LANES), lambda b, t: (b, 0, 0)),
        d=pl.BlockSpec((1, LANE), lambda b, t: (0, b)),
        perm=pl.BlockSpec((S5_TILE, S5_TILE), lambda b, t: (0, 0)),
        bound=pl.BlockSpec((None, S5_PER_STEP, 2, S5_LANES), lambda b, t: (b, tt(t), 0, 0)),
    )


def _s5_fwd(proj, pm, pm_t, bre, bim, cre_t, cim_t, lam, d, carry=None):
    T = proj.shape[0]
    nt = T // S5_TILE
    sp = _s5_specs(T, False)

    def body(u_ref, pm_ref, pmt_ref, bre_ref, bim_ref, cre_ref, cim_ref, lam_ref, d_ref, y_ref, bound_ref,
             carry, ptab, xr, xi):
        lr = lam_ref[0:1, :]
        li = lam_ref[1:2, :]

        @pl.when(pl.program_id(1) == 0)
        def _():
            carry[...] = jnp.zeros_like(carry)
            _fill_power_table(ptab, lr, li)

        for s in range(S5_PER_STEP):
            rows = slice(s * S5_TILE, (s + 1) * S5_TILE)
            u = _permute_rows_f32(pm_ref[...], u_ref[rows, :])
            ub = u.astype(BF16)
            _tile_set(xr, _dot(ub, bre_ref[...]))
            _tile_set(xi, _dot(ub, bim_ref[...]))
            bound_ref[s] = carry[...]
            _, _, end_r, end_i = _s5_forward_states(xr, xi, lr, li, carry[0:1, :], carry[1:2, :], ptab)
            carry[0:1, :] = end_r
            carry[1:2, :] = end_i
            y = (_dot(_tile_get(xr).astype(BF16), cre_ref[...]) - _dot(_tile_get(xi).astype(BF16), cim_ref[...])
                 + d_ref[...] * u)
            y_ref[rows, :] = _permute_rows_f32(pmt_ref[...], y)

    state = pltpu.VMEM(S5_STATE_TILE, F32)
    return _pcall(
        body, "s5_fwd", (S5_NBLK, nt // S5_PER_STEP),
        [sp["u"], sp["perm"], sp["perm"], sp["to_state"], sp["to_state"], sp["from_state"],
         sp["from_state"], sp["lam"], sp["d"]],
        [sp["rows"], sp["bound"]],
        [jax.ShapeDtypeStruct((T, SSM_WIDTH), F32),
         jax.ShapeDtypeStruct((S5_NBLK, nt, 2, S5_LANES), F32)],
        [pltpu.VMEM((2, S5_LANES), F32), pltpu.VMEM((2, S5_TILE, S5_LANES), F32), state, state],
        (proj, pm, pm_t, bre, bim, cre_t, cim_t, lam, d), carry)


def _glu_fwd(y, w, b, og, tm):
    T = y.shape[0]

    def body(y_ref, w_ref, b_ref, og_ref, z_ref, o_ref, r_ref):
        y1 = _gelu(y_ref[...])
        z = _dot(y1.astype(BF16), w_ref[...]) + b_ref[...]
        y2 = y1 * _sigmoid(z)
        r = lax.rsqrt(jnp.mean(y2 * y2, axis=-1, keepdims=True) + EPS)
        z_ref[...] = z
        o_ref[...] = (y2 * r * og_ref[...]).astype(BF16)
        r_ref[...] = r

    row = pl.BlockSpec((tm, SSM_WIDTH), lambda i: (i, 0))
    vec = pl.BlockSpec((1, SSM_WIDTH), lambda i: (0, 0))
    return pl.pallas_call(
        body, name="glu_fwd", grid=(T // tm,),
        in_specs=[row, pl.BlockSpec((SSM_WIDTH, SSM_WIDTH), lambda i: (0, 0)), vec, vec],
        out_specs=[row, row, pl.BlockSpec((tm, 1), lambda i: (i, 0))],
        out_shape=[jax.ShapeDtypeStruct((T, SSM_WIDTH), F32), jax.ShapeDtypeStruct((T, SSM_WIDTH), BF16),
                   jax.ShapeDtypeStruct((T, 1), F32)],
        compiler_params=_params(1),
    )(y, w, b, og)


def _out_proj_fwd(x, y_ret, y_ssm, w, g, tm):
    T = x.shape[0]

    def body(x_ref, a_ref, b_ref, w_ref, g_ref, x2_ref, h_ref, r_ref):
        x2 = x_ref[...] + _dot(a_ref[...], w_ref[0:RET_WIDTH, :]) + _dot(b_ref[...], w_ref[RET_WIDTH:D_MODEL, :])
        r = lax.rsqrt(jnp.mean(x2 * x2, axis=-1, keepdims=True) + EPS)
        x2_ref[...] = x2
        h_ref[...] = (x2 * r * g_ref[...]).astype(BF16)
        r_ref[...] = r

    full = pl.BlockSpec((tm, D_MODEL), lambda i: (i, 0))
    half = pl.BlockSpec((tm, RET_WIDTH), lambda i: (i, 0))
    return pl.pallas_call(
        body, name="out_proj_fwd", grid=(T // tm,),
        in_specs=[full, half, half, pl.BlockSpec((D_MODEL, D_MODEL), lambda i: (0, 0)),
                  pl.BlockSpec((1, D_MODEL), lambda i: (0, 0))],
        out_specs=[full, full, pl.BlockSpec((tm, 1), lambda i: (i, 0))],
        out_shape=[jax.ShapeDtypeStruct((T, D_MODEL), F32), jax.ShapeDtypeStruct((T, D_MODEL), BF16),
                   jax.ShapeDtypeStruct((T, 1), F32)],
        compiler_params=_params(1),
    )(x, y_ret, y_ssm, w, g)


def _ffn_up(h, wg, wu, tm, carry=None):
    T = h.shape[0]

    def body(h_ref, wg_ref, wu_ref, a_ref, b_ref, f_ref):
        hb = h_ref[...]
        a = _dot(hb, wg_ref[...])
        b = _dot(hb, wu_ref[...])
        a_ref[...] = a.astype(BF16)
        b_ref[...] = b.astype(BF16)
        f_ref[...] = (a * _sigmoid(a) * b).astype(BF16)

    wspec = pl.BlockSpec((None, D_MODEL, FF_BLK), lambda j, i: (j, 0, 0))
    ospec = pl.BlockSpec((None, tm, FF_BLK), lambda j, i: (j, i, 0))
    oshape = jax.ShapeDtypeStruct((N_DEV, T, FF_BLK), BF16)
    return _pcall(
        body, "ffn_up", (N_DEV, T // tm),
        [pl.BlockSpec((tm, D_MODEL), lambda j, i: (i, 0)), wspec, wspec],
        [ospec, ospec, ospec], [oshape, oshape, oshape], [], (h, wg, wu), carry)


def _ffn_down_loss(f, wd, x2, tgt, g, tm):
    T = x2.shape[0]

    def body(f_ref, w_hbm, x2_ref, t_ref, g_ref, dx_ref, dxb_ref, loss_ref, dg_ref, w_ref, sem):
        i = pl.program_id(0)

        @pl.when(i == 0)
        def _():
            _load_resident(w_hbm, w_ref, sem)
            loss_ref[...] = jnp.zeros_like(loss_ref)
            dg_ref[...] = jnp.zeros_like(dg_ref)

        gv = g_ref[...]
        x3 = x2_ref[...]
        for k in range(N_DEV):
            x3 = x3 + _dot(f_ref[k], w_ref[k])
        r = lax.rsqrt(jnp.mean(x3 * x3, axis=-1, keepdims=True) + EPS)
        err = x3 * r * gv - t_ref[...]
        tile_loss = 0.5 * jnp.sum(jnp.mean(err * err, axis=-1, keepdims=True), axis=0, keepdims=True)
        dx, dgt = _rms_bwd(err * (1.0 / D_MODEL), x3, r, gv)
        dx_ref[...] = dx
        dxb_ref[...] = dx.astype(BF16)
        loss_ref[...] += jnp.broadcast_to(tile_loss, loss_ref.shape)
        dg_ref[...] += jnp.sum(dgt, axis=0, keepdims=True)

    full = pl.BlockSpec((tm, D_MODEL), lambda i: (i, 0))
    vec = pl.BlockSpec((1, D_MODEL), lambda i: (0, 0))
    return pl.pallas_call(
        body, name="ffn_down_loss", grid=(T // tm,),
        in_specs=[pl.BlockSpec((N_DEV, tm, FF_BLK), lambda i: (0, i, 0)), ANY_SPEC, full, full, vec],
        out_specs=[full, full, pl.BlockSpec((8, LANE), lambda i: (0, 0)), vec],
        out_shape=[jax.ShapeDtypeStruct((T, D_MODEL), F32), jax.ShapeDtypeStruct((T, D_MODEL), BF16),
                   jax.ShapeDtypeStruct((8, LANE), F32), jax.ShapeDtypeStruct((1, D_MODEL), F32)],
        scratch_shapes=[pltpu.VMEM(wd.shape, wd.dtype), pltpu.SemaphoreType.DMA],
        compiler_params=_params(1),
    )(f, wd, x2, tgt, g)


def _ffn_bwd_act(dxb, wd, a, b, tm):
    T = dxb.shape[0]

    def body(dx_ref, w_ref, a_ref, b_ref, da_ref, db_ref):
        df = _dot_nt(dx_ref[...], w_ref[...])
        a = a_ref[...].astype(F32)
        b = b_ref[...].astype(F32)
        sg = _sigmoid(a)
        da_ref[...] = (df * b * sg * (1.0 + a * (1.0 - sg))).astype(BF16)
        db_ref[...] = (df * a * sg).astype(BF16)

    blk = pl.BlockSpec((None, tm, FF_BLK), lambda j, i: (j, i, 0))
    oshape = jax.ShapeDtypeStruct((N_DEV, T, FF_BLK), BF16)
    return pl.pallas_call(
        body, name="ffn_bwd_act", grid=(N_DEV, T // tm),
        in_specs=[pl.BlockSpec((tm, D_MODEL), lambda j, i: (i, 0)),
                  pl.BlockSpec((None, FF_BLK, D_MODEL), lambda j, i: (j, 0, 0)), blk, blk],
        out_specs=[blk, blk], out_shape=[oshape, oshape],
        compiler_params=_params(2),
    )(dxb, wd, a, b)


def _ffn_bwd_in(da, db, wg, wu, tm, carry=None):
    T = da.shape[1]

    def body(da_ref, db_ref, wg_ref, wu_ref, dh_ref):
        part = _dot_nt(da_ref[...], wg_ref[...]) + _dot_nt(db_ref[...], wu_ref[...])

        @pl.when(pl.program_id(1) == 0)
        def _():
            dh_ref[...] = part

        @pl.when(pl.program_id(1) > 0)
        def _():
            dh_ref[...] += part

    ablk = pl.BlockSpec((None, tm, FF_BLK), lambda i, k: (k, i, 0))
    wblk = pl.BlockSpec((None, D_MODEL, FF_BLK), lambda i, k: (k, 0, 0))
    return _pcall(
        body, "ffn_bwd_in", (T // tm, N_DEV), [ablk, ablk, wblk, wblk],
        [pl.BlockSpec((tm, D_MODEL), lambda i, k: (i, 0))], [jax.ShapeDtypeStruct((T, D_MODEL), F32)],
        [], (da, db, wg, wu), carry)


def _ffn_wgrad_up(h, da, db, tk, carry=None):
    T = h.shape[0]
    nk = T // tk

    def body(h_ref, da_ref, db_ref, g_ref, u_ref, accg, accu):
        k = pl.program_id(1)

        @pl.when(k == 0)
        def _():
            accg[...] = jnp.zeros_like(accg)
            accu[...] = jnp.zeros_like(accu)

        hb = h_ref[...]
        accg[...] += _dot_tn(hb, da_ref[...])
        accu[...] += _dot_tn(hb, db_ref[...])

        @pl.when(k == nk - 1)
        def _():
            g_ref[...] = accg[...].astype(BF16)
            u_ref[...] = accu[...].astype(BF16)

    blk = pl.BlockSpec((None, tk, FF_BLK), lambda j, k: (j, k, 0))
    ospec = pl.BlockSpec((None, D_MODEL, FF_BLK), lambda j, k: (j, 0, 0))
    oshape = jax.ShapeDtypeStruct((N_DEV, D_MODEL, FF_BLK), BF16)
    return _pcall(
        body, "ffn_wgrad_up", (N_DEV, nk),
        [pl.BlockSpec((tk, D_MODEL), lambda j, k: (k, 0)), blk, blk],
        [ospec, ospec], [oshape, oshape],
        [pltpu.VMEM((D_MODEL, FF_BLK), F32), pltpu.VMEM((D_MODEL, FF_BLK), F32)], (h, da, db), carry)


def _ffn_wgrad_down(f, dxb, tk):
    T = dxb.shape[0]
    nk = T // tk

    def body(f_ref, dx_ref, o_ref, acc):
        k = pl.program_id(1)

        @pl.when(k == 0)
        def _():
            acc[...] = jnp.zeros_like(acc)

        acc[...] += _dot_tn(f_ref[...], dx_ref[...])

        @pl.when(k == nk - 1)
        def _():
            o_ref[...] = acc[...].astype(BF16)

    return pl.pallas_call(
        body, name="ffn_wgrad_down", grid=(N_DEV, nk),
        in_specs=[pl.BlockSpec((None, tk, FF_BLK), lambda j, k: (j, k, 0)),
                  pl.BlockSpec((tk, D_MODEL), lambda j, k: (k, 0))],
        out_specs=pl.BlockSpec((None, FF_BLK, D_MODEL), lambda j, k: (j, 0, 0)),
        out_shape=jax.ShapeDtypeStruct((N_DEV, FF_BLK, D_MODEL), BF16),
        scratch_shapes=[pltpu.VMEM((FF_BLK, D_MODEL), F32)],
        compiler_params=_params(2),
    )(f, dxb)


def _out_proj_bwd(dh2, x2, r2, g, dx3, w, tm):
    T = x2.shape[0]

    def body(dh_ref, x_ref, r_ref, g_ref, dx3_ref, w_ref, dx_ref, dxb_ref, dg_ref, a_ref, b_ref):
        @pl.when(pl.program_id(0) == 0)
        def _():
            dg_ref[...] = jnp.zeros_like(dg_ref)

        dxn, dgt = _rms_bwd(dh_ref[...], x_ref[...], r_ref[...], g_ref[...])
        dx = dx3_ref[...] + dxn
        dxv = dx.astype(BF16)
        dx_ref[...] = dx
        dxb_ref[...] = dxv
        dg_ref[...] += jnp.sum(dgt, axis=0, keepdims=True)
        a_ref[...] = _dot_nt(dxv, w_ref[0:RET_WIDTH, :])
        b_ref[...] = _dot_nt(dxv, w_ref[RET_WIDTH:D_MODEL, :])

    full = pl.BlockSpec((tm, D_MODEL), lambda i: (i, 0))
    vec = pl.BlockSpec((1, D_MODEL), lambda i: (0, 0))
    half = pl.BlockSpec((tm, RET_WIDTH), lambda i: (i, 0))
    hshape = jax.ShapeDtypeStruct((T, RET_WIDTH), F32)
    return pl.pallas_call(
        body, name="out_proj_bwd", grid=(T // tm,),
        in_specs=[full, full, pl.BlockSpec((tm, 1), lambda i: (i, 0)), vec, full,
                  pl.BlockSpec((D_MODEL, D_MODEL), lambda i: (0, 0))],
        out_specs=[full, full, vec, half, half],
        out_shape=[jax.ShapeDtypeStruct((T, D_MODEL), F32), jax.ShapeDtypeStruct((T, D_MODEL), BF16),
                   jax.ShapeDtypeStruct((1, D_MODEL), F32), hshape, hshape],
        compiler_params=_params(1),
    )(dh2, x2, r2, g, dx3, w)


def _wgrad_rows(name, a, b, tk):
    T, M = a.shape
    N = b.shape[1]
    nk = T // tk

    def body(a_ref, b_ref, o_ref, acc):
        k = pl.program_id(0)

        @pl.when(k == 0)
        def _():
            acc[...] = jnp.zeros_like(acc)

        acc[...] += _dot_tn(a_ref[...], b_ref[...])

        @pl.when(k == nk - 1)
        def _():
            o_ref[...] = acc[...].astype(BF16)

    return pl.pallas_call(
        body, name=name, grid=(nk,),
        in_specs=[pl.BlockSpec((tk, M), lambda k: (k, 0)), pl.BlockSpec((tk, N), lambda k: (k, 0))],
        out_specs=pl.BlockSpec((M, N), lambda k: (0, 0)),
        out_shape=jax.ShapeDtypeStruct((M, N), BF16),
        scratch_shapes=[pltpu.VMEM((M, N), F32)],
        compiler_params=_params(1),
    )(a, b)


def _glu_bwd(y, z, r, dyo, w, og, tm):
    T = y.shape[0]

    def body(y_ref, z_ref, r_ref, d_ref, w_ref, og_ref, dy_ref, dw_ref, db_ref, dog_ref):
        @pl.when(pl.program_id(0) == 0)
        def _():
            dw_ref[...] = jnp.zeros_like(dw_ref)
            db_ref[...] = jnp.zeros_like(db_ref)
            dog_ref[...] = jnp.zeros_like(dog_ref)

        y1, g1 = _gelu_and_grad(y_ref[...])
        sg = _sigmoid(z_ref[...])
        y2 = y1 * sg
        dy2, dogt = _rms_bwd(d_ref[...], y2, r_ref[...], og_ref[...])
        dog_ref[...] += jnp.sum(dogt, axis=0, keepdims=True)
        dz = dy2 * y1 * sg * (1.0 - sg)
        db_ref[...] += jnp.sum(dz, axis=0, keepdims=True)
        dzb = dz.astype(BF16)
        dw_ref[...] += _dot_tn(y1.astype(BF16), dzb)
        dy_ref[...] = (dy2 * sg + _dot_nt(dzb, w_ref[...])) * g1

    row = pl.BlockSpec((tm, SSM_WIDTH), lambda i: (i, 0))
    vec = pl.BlockSpec((1, SSM_WIDTH), lambda i: (0, 0))
    sq = pl.BlockSpec((SSM_WIDTH, SSM_WIDTH), lambda i: (0, 0))
    return pl.pallas_call(
        body, name="glu_bwd", grid=(T // tm,),
        in_specs=[row, row, pl.BlockSpec((tm, 1), lambda i: (i, 0)), row, sq, vec],
        out_specs=[row, sq, vec, vec],
        out_shape=[jax.ShapeDtypeStruct((T, SSM_WIDTH), F32), jax.ShapeDtypeStruct((SSM_WIDTH, SSM_WIDTH), F32),
                   jax.ShapeDtypeStruct((1, SSM_WIDTH), F32), jax.ShapeDtypeStruct((1, SSM_WIDTH), F32)],
        compiler_params=_params(1),
    )(y, z, r, dyo, w, og)


def _s5_bwd(proj, dy, bound, pm, pm_t, bre, bim, bre_t, bim_t, cre, cim, lam, d, carry=None):
    T = proj.shape[0]
    nt = T // S5_TILE
    sp = _s5_specs(T, True)

    def body(u_ref, dy_ref, bound_ref, pm_ref, pmt_ref, bre_ref, bim_ref, bret_ref, bimt_ref, cre_ref, cim_ref,
             lam_ref, d_ref,
             du_ref, dbre_ref, dbim_ref, dcre_ref, dcim_ref, dlam_ref, dd_ref, carry, ptab, sr, si, gr, gi):
        lr = lam_ref[0:1, :]
        li = lam_ref[1:2, :]

        @pl.when(pl.program_id(1) == 0)
        def _():
            carry[...] = jnp.zeros_like(carry)
            _fill_power_table(ptab, lr, li)
            for ref in (dbre_ref, dbim_ref, dcre_ref, dcim_ref, dlam_ref, dd_ref):
                ref[...] = jnp.zeros_like(ref)

        def one_tile(u_in, dy_in, b_r, b_i):
            u = _permute_rows_f32(pm_ref[...], u_in)
            ub = u.astype(BF16)
            dyv = _permute_rows_f32(pm_ref[...], dy_in)
            dyb = dyv.astype(BF16)
            _tile_set(sr, _dot(ub, bre_ref[...]))
            _tile_set(si, _dot(ub, bim_ref[...]))
            er, ei, _, _ = _s5_forward_states(sr, si, lr, li, b_r, b_i, ptab)
            _tile_set(gr, _dot(dyb, cre_ref[...]))
            _tile_set(gi, -_dot(dyb, cim_ref[...]))
            zr, zi = _chunk_scans(gr, gi, lr, -li, True)
            ar, ai = _table_rows(ptab, S5_STEPS - 1, True)
            fr, fi = _entering_states(zr, zi, carry[0:1, :], carry[1:2, :], ar, ai, True)
            acc_r = jnp.zeros((S5_CHUNKS, S5_LANES), F32)
            acc_i = jnp.zeros((S5_CHUNKS, S5_LANES), F32)
            for j in range(S5_STEPS):
                qr, qi = _table_rows(ptab, S5_STEPS - 1 - j, True)
                g_r = _step_get(gr, j) + qr * fr - qi * fi
                g_i = _step_get(gi, j) + qr * fi + qi * fr
                _step_set(gr, j, g_r)
                _step_set(gi, j, g_i)
                p_r, p_i = (er, ei) if j == 0 else (_step_get(sr, j - 1), _step_get(si, j - 1))
                acc_r += g_r * p_r + g_i * p_i
                acc_i += g_i * p_r - g_r * p_i
            dlam_ref[0:1, :] += jnp.sum(acc_r, axis=0, keepdims=True)
            dlam_ref[1:2, :] += jnp.sum(acc_i, axis=0, keepdims=True)
            g_all_r = _tile_get(gr)
            g_all_i = _tile_get(gi)
            carry[0:1, :] = g_all_r[0:1, :]
            carry[1:2, :] = g_all_i[0:1, :]
            grb = g_all_r.astype(BF16)
            gib = g_all_i.astype(BF16)
            du = (_dot(grb, bret_ref[...]) + _dot(gib, bimt_ref[...]) + d_ref[...] * dyv).astype(BF16)
            dbre_ref[...] += _dot_tn(grb, ub)
            dbim_ref[...] += _dot_tn(gib, ub)
            dcre_ref[...] += _dot_tn(dyb, _tile_get(sr).astype(BF16))
            dcim_ref[...] -= _dot_tn(dyb, _tile_get(si).astype(BF16))
            dd_ref[...] += jnp.sum(dyv * u, axis=0, keepdims=True)
            return _dot(pmt_ref[...], du).astype(BF16)

        for s in reversed(range(S5_PER_STEP)):
            rows = slice(s * S5_TILE, (s + 1) * S5_TILE)
            du_ref[rows, :] = one_tile(u_ref[rows, :], dy_ref[rows, :], bound_ref[s, 0:1, :], bound_ref[s, 1:2, :])

    acc_ts = pl.BlockSpec((None, S5_LANES, LANE), lambda b, t: (b, 0, 0))
    acc_fs = pl.BlockSpec((None, LANE, S5_LANES), lambda b, t: (b, 0, 0))
    return _pcall(
        body, "s5_bwd", (S5_NBLK, nt // S5_PER_STEP),
        [sp["u"], sp["rows"], sp["bound"], sp["perm"], sp["perm"], sp["to_state"], sp["to_state"],
         sp["from_state"], sp["from_state"], sp["to_state"], sp["to_state"], sp["lam"], sp["d"]],
        [sp["rows"], acc_ts, acc_ts, acc_fs, acc_fs, sp["lam"], sp["d"]],
        [jax.ShapeDtypeStruct((T, SSM_WIDTH), BF16),
         jax.ShapeDtypeStruct((S5_NBLK, S5_LANES, LANE), F32),
         jax.ShapeDtypeStruct((S5_NBLK, S5_LANES, LANE), F32),
         jax.ShapeDtypeStruct((S5_NBLK, LANE, S5_LANES), F32),
         jax.ShapeDtypeStruct((S5_NBLK, LANE, S5_LANES), F32),
         jax.ShapeDtypeStruct((S5_NBLK, 2, S5_LANES), F32),
         jax.ShapeDtypeStruct((1, SSM_WIDTH), F32)],
        [pltpu.VMEM((2, S5_LANES), F32), pltpu.VMEM((2, S5_TILE, S5_LANES), F32)]
        + [pltpu.VMEM(S5_STATE_TILE, F32)] * 4,
        (proj, dy, bound, pm, pm_t, bre, bim, bre_t, bim_t, cre, cim, lam, d), carry)


def _ret_bwd(proj, cosf, sinf, mask, rowdec, kdec, gtb, gn, sblk, dyr):
    T = proj.shape[0]
    nb = T // RET_BLOCK
    sp = _ret_specs(T, True)

    def body(first, q_ref, k_ref, v_ref, g_ref, cos_ref, sin_ref, mask_ref, rd_ref, kd_ref, gtb_ref, gn_ref, sb_ref,
             dy_ref, dq_ref, dk_ref, dv_ref, dg_ref, dgn_ref, dst):
        if first:
            @pl.when(pl.program_id(1) == 0)
            def _():
                dst[...] = jnp.zeros_like(dst)
                dgn_ref[...] = jnp.zeros_like(dgn_ref)

        s_in = sb_ref[...]
        q, k, qb, kb, vb, pm, qd, o = _ret_common(q_ref, k_ref, v_ref, cos_ref, sin_ref, mask_ref, rd_ref, s_in)
        mu = jnp.mean(o, axis=-1, keepdims=True)
        oc = o - mu
        rstd = lax.rsqrt(jnp.mean(oc * oc, axis=-1, keepdims=True) + EPS)
        n = oc * rstd
        gt = g_ref[...]
        sg = _sigmoid(gt)
        sil = gt * sg
        gnv = gn_ref[...]
        dyv = dy_ref[...]
        dg_ref[...] = (dyv * (n * gnv) * (sg * (1.0 + gt * (1.0 - sg)))).astype(BF16)
        dgn_ref[...] += jnp.sum(dyv * sil * n, axis=0, keepdims=True)
        dn = dyv * sil * gnv
        do = rstd * (dn - jnp.mean(dn, axis=-1, keepdims=True) - n * jnp.mean(dn * n, axis=-1, keepdims=True))
        dob = do.astype(BF16)
        ds = dst[...]
        dsb = ds.astype(BF16)
        kd = kd_ref[...]
        rd = rd_ref[...]
        dv_ref[...] = (_dot_tn(pm, dob) + _dot((k * kd).astype(BF16), dsb)).astype(BF16)
        dpb = (_dot_nt(dob, vb) * mask_ref[...]).astype(BF16)
        dq = _dot(dpb, kb) + _dot_nt(dob, s_in.astype(BF16)) * rd
        dk = (_dot_tn(dpb, qb) + _dot_nt(vb, dsb) * kd) * (HEAD_DIM ** -0.5)
        dst[...] = gtb_ref[...] * ds + _dot_tn(qd, dob)
        c = cos_ref[...]
        s = sin_ref[...]
        dq_ref[...] = (dq * c + pltpu.roll(dq * s, HEAD_DIM // 2, 1)).astype(BF16)
        dk_ref[...] = (dk * c + pltpu.roll(dk * s, HEAD_DIM // 2, 1)).astype(BF16)

    oshape = jax.ShapeDtypeStruct((T, RET_WIDTH), BF16)
    ins = [sp[n] for n in ("q", "k", "v", "g", "tab", "tab", "mask", "dec", "dec", "gtb", "gn", "state", "rows")]
    outs = [sp["rows"], sp["rows"], sp["rows"], sp["rows"], sp["gn"]]
    return pl.pallas_call(
        _per_head(body, [kind for _, kind in ins + outs + [sp["scratch"]]], True), name="ret_bwd",
        grid=(RET_HEADS // RET_HPS, nb // RET_PER_STEP), in_specs=[s for s, _ in ins],
        out_specs=[s for s, _ in outs],
        out_shape=[oshape, oshape, oshape, oshape, jax.ShapeDtypeStruct((1, RET_WIDTH), F32)],
        scratch_shapes=[sp["scratch"][0]],
        compiler_params=_params(2),
    )(proj, proj, proj, proj, cosf, sinf, mask, rowdec, kdec, gtb, gn, sblk, dyr)


def _in_proj_bwd(dproj, w, x, r1, g, dx2, tm, carry=None):
    T = x.shape[0]

    def body(dp_ref, w_hbm, x_ref, r_ref, g_ref, dx2_ref, gx_ref, dg_ref, w_ref, sem):
        @pl.when(pl.program_id(0) == 0)
        def _():
            _load_resident(w_hbm, w_ref, sem)
            dg_ref[...] = jnp.zeros_like(dg_ref)

        dh = _dot_nt(dp_ref[:, 0:WIN_BLK], w_ref[0])
        for k in range(1, N_DEV):
            dh = dh + _dot_nt(dp_ref[:, k * WIN_BLK:(k + 1) * WIN_BLK], w_ref[k])
        dxn, dgt = _rms_bwd(dh, x_ref[...], r_ref[...], g_ref[...])
        gx_ref[...] = dx2_ref[...] + dxn
        dg_ref[...] += jnp.sum(dgt, axis=0, keepdims=True)

    full = pl.BlockSpec((tm, D_MODEL), lambda i: (i, 0))
    vec = pl.BlockSpec((1, D_MODEL), lambda i: (0, 0))
    return _pcall(
        body, "in_proj_bwd", (T // tm,),
        [pl.BlockSpec((tm, IN_WIDTH), lambda i: (i, 0)), ANY_SPEC,
         full, pl.BlockSpec((tm, 1), lambda i: (i, 0)), vec, full],
        [full, vec],
        [jax.ShapeDtypeStruct((T, D_MODEL), F32), jax.ShapeDtypeStruct((1, D_MODEL), F32)],
        [pltpu.VMEM(w.shape, w.dtype), pltpu.SemaphoreType.DMA], (dproj, w, x, r1, g, dx2), carry)


def _in_proj_wgrad(h, dproj, tk, carry=None):
    T = h.shape[0]
    nk = T // tk

    def body(h_ref, dp_ref, o_ref, acc):
        k = pl.program_id(1)

        @pl.when(k == 0)
        def _():
            acc[...] = jnp.zeros_like(acc)

        acc[...] += _dot_tn(h_ref[...], dp_ref[...])

        @pl.when(k == nk - 1)
        def _():
            o_ref[...] = acc[...].astype(BF16)

    return _pcall(
        body, "in_proj_wgrad", (N_DEV, nk),
        [pl.BlockSpec((tk, D_MODEL), lambda j, k: (k, 0)), pl.BlockSpec((tk, WIN_BLK), lambda j, k: (k, j))],
        [pl.BlockSpec((None, D_MODEL, WIN_BLK), lambda j, k: (j, 0, 0))],
        [jax.ShapeDtypeStruct((N_DEV, D_MODEL, WIN_BLK), BF16)],
        [pltpu.VMEM((D_MODEL, WIN_BLK), F32)], (h, dproj), carry)


def _rope_tables(T):
    half = HEAD_DIM // 2
    freqs = ROPE_BASE ** (-jnp.arange(half, dtype=F32) / half)
    ang = jnp.arange(T, dtype=F32)[:, None] * freqs[None, :]
    c = jnp.cos(ang)
    s = jnp.sin(ang)
    return jnp.concatenate([c, c], axis=1), jnp.concatenate([-s, s], axis=1)


def _retention_tables():
    hh = jnp.arange(RET_HEADS, dtype=F32)
    log_g = jnp.log1p(-(2.0 ** (-5.0 - hh)))[:, None, None]
    i = jnp.arange(RET_BLOCK)
    ci = (i // CHUNK)[:, None]
    cj = (i // CHUNK)[None, :]
    diff = (i[:, None] - i[None, :]).astype(F32)
    expo = jnp.where(ci == cj, jnp.abs(diff), diff)
    mask = jnp.where((cj <= ci)[None], jnp.exp(log_g * expo[None]), 0.0)
    r = jnp.arange(RET_BLOCK, dtype=F32)[None, :, None]
    ones = jnp.ones((1, 1, HEAD_DIM), F32)
    rowdec = jnp.exp(log_g * (r + 1.0)) * ones
    kdec = jnp.exp(log_g * (RET_BLOCK - 1.0 - r)) * ones
    gtb = jnp.exp(log_g * float(RET_BLOCK)) * ones
    return mask, rowdec, kdec, gtb


def _s5_discretise(a_re, a_im, log_dt, b_re, b_im):
    lam = lax.complex(a_re, a_im)
    dt = jnp.exp(log_dt)[:, None]
    lam_bar = jnp.exp(lam * dt)
    b_bar = ((lam_bar - 1.0) / lam)[..., None] * lax.complex(b_re, b_im)
    return jnp.real(lam_bar), jnp.imag(lam_bar), jnp.real(b_bar), jnp.imag(b_bar)


def _to_state_blockdiag(m):
    eye = jnp.eye(S5_GB, dtype=m.dtype)
    t = jnp.einsum("bgpc,gh->bgchp", m.reshape(S5_NBLK, S5_GB, SSM_STATE, SSM_GROUP), eye)
    return t.reshape(S5_NBLK, LANE, S5_LANES)


def _from_state_blockdiag(m):
    eye = jnp.eye(S5_GB, dtype=m.dtype)
    t = jnp.einsum("bgcp,gh->bgphc", m.reshape(S5_NBLK, S5_GB, SSM_GROUP, SSM_STATE), eye)
    return t.reshape(S5_NBLK, S5_LANES, LANE)


def _diag_of_state_major(acc):
    eye = jnp.eye(S5_GB, dtype=acc.dtype)
    t = acc.reshape(S5_NBLK, S5_GB, SSM_STATE, S5_GB, SSM_GROUP)
    return jnp.einsum("bgphc,gh->bgpc", t, eye).reshape(SSM_GROUPS, SSM_STATE, SSM_GROUP)


def _diag_of_channel_major(acc):
    eye = jnp.eye(S5_GB, dtype=acc.dtype)
    t = acc.reshape(S5_NBLK, S5_GB, SSM_GROUP, S5_GB, SSM_STATE)
    return jnp.einsum("bgchp,gh->bgcp", t, eye).reshape(SSM_GROUPS, SSM_GROUP, SSM_STATE)


SMALL_PARTIALS = (("ret_gn_g", 1024), ("lam_re", 4096), ("lam_im", 4096),
                  ("bbar_re", 65536), ("bbar_im", 65536), ("c_re", 65536), ("c_im", 65536),
                  ("ssm_d", 1024), ("b_glu", 1024), ("out_g", 1024), ("norm_ffn_g", 2048), ("norm_final_g", 2048))


def _forward_backward(x, tgt, shards, sm):
    T = x.shape[0]
    tm = min(1024, T)
    cosf, sinf = _rope_tables(T)
    mask, rowdec, kdec, gtb = _retention_tables()
    lbr, lbi, bbr, bbi = _s5_discretise(sm["ssm_a_re"], sm["ssm_a_im"], sm["ssm_log_dt"], sm["ssm_b_re"],
                                        sm["ssm_b_im"])
    bre = _to_state_blockdiag(bbr).astype(BF16)
    bim = _to_state_blockdiag(bbi).astype(BF16)
    cre_t = _from_state_blockdiag(sm["ssm_c_re"]).astype(BF16)
    cim_t = _from_state_blockdiag(sm["ssm_c_im"]).astype(BF16)
    bre_t = jnp.swapaxes(bre, 1, 2)
    bim_t = jnp.swapaxes(bim, 1, 2)
    cre = jnp.swapaxes(cre_t, 1, 2)
    cim = jnp.swapaxes(cim_t, 1, 2)
    lam = jnp.stack([lbr.reshape(S5_NBLK, S5_LANES), lbi.reshape(S5_NBLK, S5_LANES)], axis=1)
    pm = _step_major_permutation()
    pm_t = pm.T
    row = lambda v: v.reshape(1, -1)
    g_mix, g_ffn, g_fin = row(sm["norm_mix_g"]), row(sm["norm_ffn_g"]), row(sm["norm_final_g"])
    gn, dsk, bglu, og = row(sm["ret_gn_g"]), row(sm["ssm_d"]), row(sm["ssm_b_glu"]), row(sm["ssm_out_g"])

    (w_in,) = _exchange_call("weight_gather", [shards["w_in"]], True, via_sibling=True)
    proj, h1, r1, w_gate = _in_proj_fwd(x, g_mix, w_in, 256, _Exchange([shards["w_gate"]], True, via_sibling=True))
    y_ret, sblk, w_glu, w_out = _ret_fwd(proj, cosf, sinf, mask, rowdec, kdec, gtb, gn,
                                         _Exchange([shards["ssm_w_glu"], shards["w_out"]], True, via_sibling=True))
    w_glu = w_glu.reshape(SSM_WIDTH, SSM_WIDTH)
    w_out = w_out.reshape(D_MODEL, D_MODEL)
    y_s5, bound, w_up = _s5_fwd(proj, pm, pm_t, bre, bim, cre_t, cim_t, lam, dsk, _Exchange([shards["w_up"]], True))
    z, y_ssm, r_ssm = _glu_fwd(y_s5, w_glu, bglu, og, 256)
    x2, h2, r2 = _out_proj_fwd(x, y_ret, y_ssm, w_out, g_ffn, 256)
    a, b, f, w_down = _ffn_up(h2, w_gate, w_up, tm, _Exchange([shards["w_down"]], True))
    dx3, dx3b, loss8, dg_fin = _ffn_down_loss(f, w_down, x2, tgt, g_fin, 256)

    landed = {}
    da, db = _ffn_bwd_act(dx3b, w_down, a, b, tm)
    dw_down = _ffn_wgrad_down(f, dx3b, tm)
    dw_gate, dw_up, landed["w_down"] = _ffn_wgrad_up(h2, da, db, tm, _Exchange([dw_down], False))
    dh2, landed["w_gate"] = _ffn_bwd_in(da, db, w_gate, w_up, min(1024, T), _Exchange([dw_gate], False))
    dx2, dx2b, dg_ffn, dy_ret, dy_ssm = _out_proj_bwd(dh2, x2, r2, g_ffn, dx3, w_out, 256)
    dw_out = jnp.concatenate([_wgrad_rows("out_proj_wgrad_ret", y_ret, dx2b, tm),
                              _wgrad_rows("out_proj_wgrad_ssm", y_ssm, dx2b, tm)], axis=0)
    dy_s5, dw_glu, db_glu, dog = _glu_bwd(y_s5, z, r_ssm, dy_ssm, w_glu, og, 256)
    du, dbre, dbim, dcre, dcim, dlam, dd, landed["w_up"] = _s5_bwd(
        proj, dy_s5, bound, pm, pm_t, bre, bim, bre_t, bim_t, cre, cim, lam, dsk, _Exchange([dw_up], False))
    dq, dk, dv, dgate, dgn = _ret_bwd(proj, cosf, sinf, mask, rowdec, kdec, gtb, gn, sblk, dy_ret)
    dproj = jnp.concatenate([dq, dk, dv, dgate, du], axis=1)
    small = dict(ret_gn_g=dgn, lam_re=dlam[:, 0], lam_im=dlam[:, 1],
                 bbar_re=_diag_of_state_major(dbre), bbar_im=_diag_of_state_major(dbim),
                 c_re=_diag_of_channel_major(dcre), c_im=_diag_of_channel_major(dcim),
                 ssm_d=dd, b_glu=db_glu, out_g=dog, norm_ffn_g=dg_ffn, norm_final_g=dg_fin)
    packed = _pack([small[n] for n, _ in SMALL_PARTIALS])
    dw_in, landed["w_out"], landed["ssm_w_glu"], small_landed = _in_proj_wgrad(
        h1, dproj, tm, _Exchange([dw_out.reshape(N_DEV, D_MODEL // N_DEV, D_MODEL),
                                  dw_glu.astype(BF16).reshape(N_DEV, SSM_WIDTH // N_DEV, SSM_WIDTH), packed],
                                 [False, False, True]))
    grad_x, dg_mix, landed["w_in"] = _in_proj_bwd(dproj, w_in, x, r1, g_mix, dx2, 256, _Exchange([dw_in], False))
    (mix_landed,) = _exchange_call("mix_gain_grad_gather", [_pack([dg_mix])], True)
    summed = dict(zip([n for n, _ in SMALL_PARTIALS],
                      _unpack(_sum_partials("small_grad_sum", small_landed), [(sz,) for _, sz in SMALL_PARTIALS])))
    summed["norm_mix_g"] = _sum_partials("mix_gain_grad_sum", mix_landed).reshape(-1)
    return loss8[0, 0], grad_x, landed, summed


def _small_grads(summed, sm):
    _, vjp = jax.vjp(_s5_discretise, sm["ssm_a_re"], sm["ssm_a_im"], sm["ssm_log_dt"], sm["ssm_b_re"], sm["ssm_b_im"])
    gp = (SSM_GROUPS, SSM_STATE)
    da_re, da_im, dlog_dt, db_re, db_im = vjp((summed["lam_re"].reshape(gp), summed["lam_im"].reshape(gp),
                                               summed["bbar_re"].reshape(gp + (SSM_GROUP,)),
                                               summed["bbar_im"].reshape(gp + (SSM_GROUP,))))
    return dict(norm_mix_g=summed["norm_mix_g"], ret_gn_g=summed["ret_gn_g"], ssm_a_re=da_re, ssm_a_im=da_im,
                ssm_log_dt=dlog_dt, ssm_b_re=db_re, ssm_b_im=db_im,
                ssm_c_re=summed["c_re"].reshape(SSM_GROUPS, SSM_GROUP, SSM_STATE),
                ssm_c_im=summed["c_im"].reshape(SSM_GROUPS, SSM_GROUP, SSM_STATE),
                ssm_d=summed["ssm_d"], ssm_b_glu=summed["b_glu"], ssm_out_g=summed["out_g"],
                norm_ffn_g=summed["norm_ffn_g"], norm_final_g=summed["norm_final_g"])


def _adamw_math(w, g, m, v):
    m2 = ADAM_B1 * m + (1.0 - ADAM_B1) * g
    v2 = ADAM_B2 * v + (1.0 - ADAM_B2) * (g * g)
    delta = -ADAM_LR * ((m2 / ADAM_BC1) / (jnp.sqrt(v2 / ADAM_BC2) + ADAM_EPS) + ADAM_WD * w)
    return delta, m2, v2


def _adamw_shard(name, parts, w, m, v, tr):
    rows, cols = w.shape

    def body(p_ref, w_ref, m_ref, v_ref, g_ref, d_ref, m2_ref, v2_ref):
        g = p_ref[0].astype(F32)
        for s in range(1, N_DEV):
            g = g + p_ref[s].astype(F32)
        d, m2, v2 = _adamw_math(w_ref[...], g, m_ref[...], v_ref[...])
        g_ref[...] = g
        d_ref[...] = d
        m2_ref[...] = m2
        v2_ref[...] = v2

    blk = pl.BlockSpec((tr, cols), lambda i: (i, 0))
    oshape = jax.ShapeDtypeStruct((rows, cols), F32)
    return pl.pallas_call(
        body, name=name, grid=(rows // tr,),
        in_specs=[pl.BlockSpec((N_DEV, tr, cols), lambda i: (0, i, 0)), blk, blk, blk],
        out_specs=[blk, blk, blk, blk], out_shape=[oshape] * 4,
        compiler_params=_params(1),
    )(parts, w, m, v)


def _sum_partials(name, parts):
    rows = parts.shape[1]

    def body(p_ref, o_ref):
        g = p_ref[0]
        for s in range(1, N_DEV):
            g = g + p_ref[s]
        o_ref[...] = g

    return pl.pallas_call(
        body, name=name, grid=(1,),
        in_specs=[pl.BlockSpec((N_DEV, rows, LANE), lambda i: (0, 0, 0))],
        out_specs=pl.BlockSpec((rows, LANE), lambda i: (0, 0)),
        out_shape=jax.ShapeDtypeStruct((rows, LANE), F32),
        compiler_params=_params(1),
    )(parts)


def _adamw_small(ws, gs, ms, vs):
    n = len(ws)

    def body(*refs):
        for i in range(n):
            w_ref, g_ref, m_ref, v_ref = (refs[k * n + i] for k in range(4))
            d_ref, m2_ref, v2_ref = (refs[(4 + k) * n + i] for k in range(3))
            d, m2, v2 = _adamw_math(w_ref[...], g_ref[...], m_ref[...], v_ref[...])
            d_ref[...] = d
            m2_ref[...] = m2
            v2_ref[...] = v2

    vmem = pl.BlockSpec(memory_space=pltpu.VMEM)
    out = pl.pallas_call(
        body, name="adamw_small", in_specs=[vmem] * (4 * n), out_specs=[vmem] * (3 * n),
        out_shape=[jax.ShapeDtypeStruct(w.shape, F32) for w in ws] * 3,
        compiler_params=pltpu.CompilerParams(vmem_limit_bytes=VMEM_LIMIT),
    )(*ws, *gs, *ms, *vs)
    return out[:n], out[n:2 * n], out[2 * n:]


def _pack(arrays):
    parts = [a.reshape(-1, LANE) for a in arrays]
    assert all(p.shape[0] % 8 == 0 for p in parts)
    return parts[0] if len(parts) == 1 else jnp.concatenate(parts, axis=0)


def _unpack(packed, shapes):
    flat = packed.reshape(-1)
    out, off = [], 0
    for shp in shapes:
        n = math.prod(shp)
        out.append(flat[off:off + n].reshape(shp))
        off += n + ((-n) % LANE)
    return out


WEIGHTS = ("norm_mix_g", "w_in", "ret_gn_g", "ssm_a_re", "ssm_a_im", "ssm_log_dt", "ssm_b_re", "ssm_b_im",
           "ssm_c_re", "ssm_c_im", "ssm_d", "ssm_w_glu", "ssm_b_glu", "ssm_out_g", "w_out", "norm_ffn_g", "w_gate",
           "w_up", "w_down", "norm_final_g")
BIG = ("w_in", "ssm_w_glu", "w_out", "w_gate", "w_up", "w_down")
SMALL = tuple(n for n in WEIGHTS if n not in BIG)
ADAM_ROWS = {"w_in": 256, "ssm_w_glu": 128, "w_out": 128, "w_gate": 256, "w_up": 256, "w_down": 176}


def kernel(x, norm_mix_g, w_in, ret_gn_g, ssm_a_re, ssm_a_im, ssm_log_dt, ssm_b_re, ssm_b_im, ssm_c_re, ssm_c_im, ssm_d, ssm_w_glu, ssm_b_glu, ssm_out_g, w_out, norm_ffn_g, w_gate, w_up, w_down, norm_final_g, loss_target, m_norm_mix_g, m_w_in, m_ret_gn_g, m_ssm_a_re, m_ssm_a_im, m_ssm_log_dt, m_ssm_b_re, m_ssm_b_im, m_ssm_c_re, m_ssm_c_im, m_ssm_d, m_ssm_w_glu, m_ssm_b_glu, m_ssm_out_g, m_w_out, m_norm_ffn_g, m_w_gate, m_w_up, m_w_down, m_norm_final_g, v_norm_mix_g, v_w_in, v_ret_gn_g, v_ssm_a_re, v_ssm_a_im, v_ssm_log_dt, v_ssm_b_re, v_ssm_b_im, v_ssm_c_re, v_ssm_c_im, v_ssm_d, v_ssm_w_glu, v_ssm_b_glu, v_ssm_out_g, v_w_out, v_norm_ffn_g, v_w_gate, v_w_up, v_w_down, v_norm_final_g):
    given = dict(locals())
    w = {n: given[n] for n in WEIGHTS}
    m = {n: given["m_" + n] for n in WEIGHTS}
    v = {n: given["v_" + n] for n in WEIGHTS}
    drop = lambda n, a: a if n == "norm_final_g" else a[0]
    w0 = {n: drop(n, w[n]) for n in WEIGHTS}
    m0 = {n: drop(n, m[n]) for n in WEIGHTS}
    v0 = {n: drop(n, v[n]) for n in WEIGHTS}

    sm = {n: w0[n] for n in SMALL}
    shards = {n: w0[n].astype(BF16) for n in BIG}
    loss_local, grad_x, landed, summed = _forward_backward(x[0], loss_target[0], shards, sm)
    loss = lax.psum(loss_local, MESH_AXES)
    gsmall = _small_grads(summed, sm)

    grads, delta, new_m, new_v = {}, {}, {}, {}
    for n in BIG:
        g, d, m2, v2 = _adamw_shard("adamw_" + n, landed[n], w0[n], m0[n], v0[n], ADAM_ROWS[n])
        grads[n], delta[n], new_m[n], new_v[n] = g, d, m2, v2
    as_given = lambda n, a: a.reshape(1, -1) if n == "norm_final_g" else a.reshape(w[n].shape)
    gs = [as_given(n, gsmall[n]) for n in SMALL]
    ds, m2s, v2s = _adamw_small([as_given(n, w[n]) for n in SMALL], gs, [as_given(n, m[n]) for n in SMALL],
                                [as_given(n, v[n]) for n in SMALL])
    for n, g, d, m2, v2 in zip(SMALL, gs, ds, m2s, v2s):
        grads[n], delta[n], new_m[n], new_v[n] = g, d, m2, v2

    lift = lambda n, a: a.reshape(w[n].shape)
    return (loss, grad_x[None], *[lift(n, grads[n]) for n in WEIGHTS], *[lift(n, delta[n]) for n in WEIGHTS],
            *[lift(n, new_m[n]) for n in WEIGHTS], *[lift(n, new_v[n]) for n in WEIGHTS])
```
